```python
import math
import jax
import jax.numpy as jnp
from jax import lax
import numpy as np

D_MODEL = 1024
BATCH = 4
SEQ = 4096
DEPTH = 4

HEAD_DIM = 64
MIX_WIDTH = D_MODEL
GROUP_WIDTH = MIX_WIDTH // 4
GROUP_HEADS = GROUP_WIDTH // HEAD_DIM
BLOCK = 128
ROPE_THETA = 10000.0
NORM_EPS = 1e-6

DIFF_HEADS = GROUP_HEADS
DIFF_DV = HEAD_DIM
DIFF_DK = HEAD_DIM // 2

NSA_HEADS = GROUP_HEADS
NSA_DH = HEAD_DIM
CMP_LEN = 32
CMP_STRIDE = 16
CMP_HIDDEN = 4 * HEAD_DIM
SEL_BLOCK = 64
SEL_TOPN = 16
NSA_WINDOW = 512
FORCED_SCORE = 1e6

MLA_HEADS = GROUP_HEADS
MLA_NOPE = HEAD_DIM
MLA_ROPE = HEAD_DIM // 2
MLA_DV = HEAD_DIM
MLA_Q_LORA = 3 * D_MODEL // 8
MLA_KV_LORA = D_MODEL // 8

DIL_HEADS = GROUP_HEADS
DIL_DH = HEAD_DIM
DILATED_PATTERNS = ((128, 1), (512, 4), (2048, 16))

MEM_LEN = 256
CROSS_HEADS = 4
CROSS_DH = D_MODEL // CROSS_HEADS

D_FF = 256 * ((8 * D_MODEL // 3 + 255) // 256)
N_EXPERTS = 8
TOP_K = 2
D_FF_EXPERT = 7 * D_MODEL // 2
MOE_BLOCK = 256

A_COLS = 2 * DIFF_HEADS * 2 * DIFF_DK + DIFF_HEADS * DIFF_DV
NSA_Q = NSA_HEADS * NSA_DH
B_COLS = NSA_Q + 6 * NSA_DH + 3 * NSA_HEADS
C_COLS = MLA_Q_LORA + MLA_KV_LORA + MLA_ROPE
D_COLS = 3 * DIL_HEADS * DIL_DH
IN_COLS = A_COLS + B_COLS + C_COLS + D_COLS
MIXER_SPLITS = (A_COLS, A_COLS + B_COLS, A_COLS + B_COLS + C_COLS)
NSA_SPLITS = tuple(NSA_Q + i * NSA_DH for i in range(7))

kernel_name = 'hybrid_parallel_heads_diff_nsa_mla_dilated_moe'


def rmsnorm(t, g):
    tf = t.astype(jnp.float32)
    tf = tf * lax.rsqrt(jnp.mean(tf * tf, axis=-1, keepdims=True) + NORM_EPS)
    return (tf * g.astype(jnp.float32)).astype(t.dtype)


def rope(t, pos):
    half = t.shape[-1] // 2
    inv_freq = ROPE_THETA ** (-jnp.arange(half, dtype=jnp.float32) / half)
    ang = pos.astype(jnp.float32)[:, None] * inv_freq[None, :]
    cos = jnp.cos(ang).astype(t.dtype)
    sin = jnp.sin(ang).astype(t.dtype)
    t1, t2 = t[..., :half], t[..., half:]
    return jnp.concatenate([t1 * cos - t2 * sin, t1 * sin + t2 * cos], axis=-1)


def split_heads(t, n):
    b, s, _ = t.shape
    return t.reshape(b, s, n, -1).transpose(0, 2, 1, 3)


def merge_heads(t):
    b, h, s, d = t.shape
    return t.transpose(0, 2, 1, 3).reshape(b, s, h * d)


def causal_attention(q, k, v, coef, scale):
    b, h, m, s, dk = q.shape
    nb = s // BLOCK
    qb = q.reshape(b, h, m, nb, BLOCK, dk).transpose(3, 0, 1, 2, 4, 5)
    kpos = jnp.arange(s)
    vf = v.astype(jnp.float32)

    def one_block(args):
        q_blk, start = args
        sc = jnp.einsum('bhmqd,bhmkd->bhmqk', q_blk, k).astype(jnp.float32) * scale
        qpos = start + jnp.arange(BLOCK)
        sc = jnp.where(kpos[None, :] <= qpos[:, None], sc, -jnp.inf)
        p = jax.nn.softmax(sc, axis=-1)
        a = jnp.einsum('bhmqk,m->bhqk', p, coef)
        return jnp.einsum('bhqk,bhkd->bhqd', a, vf)

    o = lax.map(one_block, (qb, jnp.arange(nb) * BLOCK))
    return o.transpose(1, 2, 0, 3, 4).reshape(b, h, s, -1).astype(v.dtype)


def banded_attention(q, k, v, window, scale):
    b, h, L, dk = q.shape
    g = k.shape[1]
    r = h // g
    dv = v.shape[-1]
    nprev = -(-window // BLOCK)
    nb = -(-L // BLOCK)
    pad = nb * BLOCK - L
    qb = jnp.pad(q, ((0, 0), (0, 0), (0, pad), (0, 0))).reshape(b, g, r, nb, BLOCK, dk)
    kp = jnp.pad(k, ((0, 0), (0, 0), (nprev * BLOCK, pad), (0, 0)))
    vp = jnp.pad(v, ((0, 0), (0, 0), (nprev * BLOCK, pad), (0, 0)))
    span = (nprev + 1) * BLOCK
    idx = jnp.arange(nb)[:, None] * BLOCK + jnp.arange(span)[None, :]
    kw = kp[:, :, idx]
    vw = vp[:, :, idx].astype(jnp.float32)
    qpos = jnp.arange(nb)[:, None] * BLOCK + jnp.arange(BLOCK)[None, :]
    kpos = idx - nprev * BLOCK
    dist = qpos[:, :, None] - kpos[:, None, :]
    mask = (dist >= 0) & (dist <= window) & (kpos[:, None, :] >= 0)
    sc = jnp.einsum('bgrnqd,bgnkd->bgrnqk', qb, kw).astype(jnp.float32) * scale
    sc = jnp.where(mask, sc, -jnp.inf)
    mx = jnp.max(sc, axis=-1, keepdims=True)
    p = jnp.exp(sc - mx)
    den = jnp.sum(p, axis=-1, keepdims=True)
    o = jnp.einsum('bgrnqk,bgnkd->bgrnqd', p, vw) / den
    lse = (mx + jnp.log(den))[..., 0]
    o = o.reshape(b, h, nb * BLOCK, dv)[:, :, :L]
    lse = lse.reshape(b, h, nb * BLOCK)[:, :, :L]
    return o, lse


def to_residue_classes(t, dil):
    b, h, s, d = t.shape
    return t.reshape(b, h, s // dil, dil, d).transpose(0, 1, 3, 2, 4).reshape(b, h * dil, s // dil, d)


def dilated_attention(q, k, v):
    b, h, s, dh = q.shape
    outs, lses = [], []
    for window, dil in DILATED_PATTERNS:
        L = s // dil
        o, lse = banded_attention(to_residue_classes(q, dil), to_residue_classes(k, dil),
                                  to_residue_classes(v, dil), window // dil, dh ** -0.5)
        outs.append(o.reshape(b, h, dil, L, dh).transpose(0, 1, 3, 2, 4).reshape(b, h, s, dh))
        lses.append(lse.reshape(b, h, dil, L).transpose(0, 1, 3, 2).reshape(b, h, s))
    alpha = jax.nn.softmax(jnp.stack(lses), axis=0)
    return jnp.einsum('pbhs,pbhsd->bhsd', alpha, jnp.stack(outs)).astype(q.dtype)


def nsa_compress(t, pos_emb, w1, w2):
    b, s, dk = t.shape
    n_c = (s - CMP_LEN) // CMP_STRIDE + 1
    idx = jnp.arange(n_c)[:, None] * CMP_STRIDE + jnp.arange(CMP_LEN)[None, :]
    blocks = t[:, idx] + pos_emb
    flat = blocks.reshape(b, n_c, CMP_LEN * dk)
    return jax.nn.gelu(flat @ w1) @ w2


def nsa_compressed_attention(q, kc, vc, pos):
    n_c = kc.shape[1]
    sc = jnp.einsum('bhtd,bcd->bhtc', q, kc).astype(jnp.float32) * NSA_DH ** -0.5
    blk_end = jnp.arange(n_c) * CMP_STRIDE + CMP_LEN - 1
    mask = blk_end[None, :] <= pos[:, None]
    sc = jnp.where(mask, sc, -jnp.inf)
    mx = jnp.max(sc, axis=-1, keepdims=True)
    mx = jnp.where(jnp.isfinite(mx), mx, 0.0)
    e = jnp.where(mask, jnp.exp(sc - mx), 0.0)
    p = e / jnp.maximum(jnp.sum(e, axis=-1, keepdims=True), 1e-30)
    o = jnp.einsum('bhtc,bcd->bhtd', p, vc.astype(jnp.float32))
    return o, p


def nsa_selected_attention(q, k, v, p_cmp, pos):
    b, h, s, dk = q.shape
    n_c = p_cmp.shape[-1]
    n_s = s // SEL_BLOCK
    k_top = min(SEL_TOPN, n_s)
    c_start = jnp.arange(n_c) * CMP_STRIDE
    s_start = jnp.arange(n_s) * SEL_BLOCK
    cover = jnp.clip(jnp.minimum(c_start[:, None] + CMP_LEN, s_start[None, :] + SEL_BLOCK)
                     - jnp.maximum(c_start[:, None], s_start[None, :]), 0, None).astype(jnp.float32) / CMP_LEN
    imp = jnp.einsum('bhtc,cj->btj', p_cmp, cover)
    cur = pos // SEL_BLOCK
    j = jnp.arange(n_s)
    valid = j[None, :] <= cur[:, None]
    forced = (j[None, :] == 0) | (j[None, :] == cur[:, None]) | (j[None, :] == cur[:, None] - 1)
    score = jnp.where(valid, jnp.where(forced, FORCED_SCORE, imp), -jnp.inf)
    top_s, top_j = lax.top_k(score, k_top)
    blk_ok = jnp.isfinite(top_s)
    kb = k.reshape(b, n_s, SEL_BLOCK, dk)
    vb = v.reshape(b, n_s, SEL_BLOCK, -1).astype(jnp.float32)
    nq = s // BLOCK
    q_ch = q.reshape(b, h, nq, BLOCK, dk).transpose(2, 0, 1, 3, 4)
    j_ch = top_j.reshape(b, nq, BLOCK, k_top).transpose(1, 0, 2, 3)
    ok_ch = blk_ok.reshape(b, nq, BLOCK, k_top).transpose(1, 0, 2, 3)
    bidx = jnp.arange(b)[:, None, None]
    offs = jnp.arange(SEL_BLOCK)
    n_keys = k_top * SEL_BLOCK

    def one_chunk(args):
        qc, jc, okc, start = args
        kg = kb[bidx, jc].reshape(b, BLOCK, n_keys, dk)
        vg = vb[bidx, jc].reshape(b, BLOCK, n_keys, -1)
        kpos = (jc[..., None] * SEL_BLOCK + offs).reshape(b, BLOCK, n_keys)
        qpos = start + jnp.arange(BLOCK)
        ok = (kpos <= qpos[None, :, None]) & jnp.repeat(okc, SEL_BLOCK, axis=-1)
        sc = jnp.einsum('bhqd,bqnd->bhqn', qc, kg).astype(jnp.float32) * dk ** -0.5
        sc = jnp.where(ok[:, None], sc, -jnp.inf)
        p = jax.nn.softmax(sc, axis=-1)
        return jnp.einsum('bhqn,bqnd->bhqd', p, vg)

    o = lax.map(one_chunk, (q_ch, j_ch, ok_ch, jnp.arange(nq) * BLOCK))
    return o.transpose(1, 2, 0, 3, 4).reshape(b, h, s, -1)


def token_mixers(h, layer, w_in, w_out, diff_lambda, diff_subln, cmp_pos, cmp_w1, cmp_w2,
                 q_norm, kv_norm, w_uq, w_ukv):
    b, s, _ = h.shape
    pos = jnp.arange(s)
    z = h @ w_in
    z_a, z_b, z_c, z_d = jnp.split(z, MIXER_SPLITS, axis=-1)

    qa, ka, va = jnp.split(z_a, (2 * DIFF_HEADS * DIFF_DK, 4 * DIFF_HEADS * DIFF_DK), axis=-1)
    qa = rope(qa.reshape(b, s, DIFF_HEADS, 2, DIFF_DK).transpose(0, 2, 3, 1, 4), pos)
    ka = rope(ka.reshape(b, s, DIFF_HEADS, 2, DIFF_DK).transpose(0, 2, 3, 1, 4), pos)
    lam_init = 0.8 - 0.6 * math.exp(-0.3 * layer)
    lp = diff_lambda.astype(jnp.float32)
    lam = jnp.exp(jnp.sum(lp[0] * lp[1])) - jnp.exp(jnp.sum(lp[2] * lp[3])) + lam_init
    coef = jnp.stack([jnp.ones((), jnp.float32), -lam])
    oa = causal_attention(qa, ka, split_heads(va, DIFF_HEADS), coef, DIFF_DK ** -0.5)
    oa = rmsnorm(oa, diff_subln) * (1.0 - lam_init)

    qn, kc, vc, ks, vs, kw, vw, gb = jnp.split(z_b, NSA_SPLITS, axis=-1)
    qn = split_heads(qn, NSA_HEADS)
    kc = nsa_compress(kc, cmp_pos[0], cmp_w1[0], cmp_w2[0])
    vc = nsa_compress(vc, cmp_pos[1], cmp_w1[1], cmp_w2[1])
    o_cmp, p_cmp = nsa_compressed_attention(qn, kc, vc, pos)
    qr = rope(qn, pos)
    o_sel = nsa_selected_attention(qr, rope(ks, pos), vs, p_cmp, pos)
    o_win, _ = banded_attention(qr, rope(kw, pos)[:, None], vw[:, None], NSA_WINDOW - 1, NSA_DH ** -0.5)
    g = jax.nn.sigmoid(gb.astype(jnp.float32)).reshape(b, s, NSA_HEADS, 3).transpose(0, 2, 1, 3)[..., None]
    ob = (g[..., 0, :] * o_cmp + g[..., 1, :] * o_sel + g[..., 2, :] * o_win).astype(h.dtype)

    cq, ckv, kr = jnp.split(z_c, (MLA_Q_LORA, MLA_Q_LORA + MLA_KV_LORA), axis=-1)
    qc = split_heads(rmsnorm(cq, q_norm) @ w_uq, MLA_HEADS)
    kvc = split_heads(rmsnorm(ckv, kv_norm) @ w_ukv, MLA_HEADS)
    k_rope = jnp.broadcast_to(rope(kr, pos)[:, None], (b, MLA_HEADS, s, MLA_ROPE))
    qc = jnp.concatenate([qc[..., :MLA_NOPE], rope(qc[..., MLA_NOPE:], pos)], axis=-1)
    kcat = jnp.concatenate([kvc[..., :MLA_NOPE], k_rope], axis=-1)
    oc = causal_attention(qc[:, :, None], kcat[:, :, None], kvc[..., MLA_NOPE:],
                          jnp.ones((1,), jnp.float32), (MLA_NOPE + MLA_ROPE) ** -0.5)

    qd, kd, vd = jnp.split(z_d, (GROUP_WIDTH, 2 * GROUP_WIDTH), axis=-1)
    od = dilated_attention(rope(split_heads(qd, DIL_HEADS), pos), rope(split_heads(kd, DIL_HEADS), pos),
                           split_heads(vd, DIL_HEADS))

    y = jnp.concatenate([merge_heads(oa), merge_heads(ob), merge_heads(oc), merge_heads(od)], axis=-1)
    return y @ w_out


def cross_attention(h, m, wq, wk, wv, wo):
    q = split_heads(h @ wq, CROSS_HEADS)
    k = split_heads(m @ wk, CROSS_HEADS)
    v = split_heads(m @ wv, CROSS_HEADS)
    sc = jnp.einsum('bhqd,bhkd->bhqk', q, k).astype(jnp.float32) * CROSS_DH ** -0.5
    p = jax.nn.softmax(sc, axis=-1)
    o = jnp.einsum('bhqk,bhkd->bhqd', p, v.astype(jnp.float32)).astype(h.dtype)
    return merge_heads(o) @ wo


def swiglu(h, wg, wu, wd):
    return (jax.nn.silu(h @ wg) * (h @ wu)) @ wd


def moe_swiglu(h, router, w_gate, w_up, w_down):
    b, s, d = h.shape
    t = b * s
    n_assign = t * TOP_K
    ht = h.reshape(t, d)
    logits = (ht @ router).astype(jnp.float32)
    top_logit, top_e = lax.top_k(logits, TOP_K)
    gate = jax.nn.softmax(top_logit, axis=-1)
    flat_e = top_e.reshape(-1)
    flat_tok = jnp.arange(n_assign) // TOP_K
    order = jnp.argsort(flat_e)
    se = flat_e[order]
    stok = flat_tok[order]
    sgate = gate.reshape(-1)[order]
    counts = jnp.bincount(flat_e, length=N_EXPERTS)
    start = jnp.cumsum(counts) - counts
    padded = (counts + MOE_BLOCK - 1) // MOE_BLOCK * MOE_BLOCK
    pend = jnp.cumsum(padded)
    pstart = pend - padded
    dest = pstart[se] + jnp.arange(n_assign) - start[se]
    n_rows = -(-n_assign // MOE_BLOCK) * MOE_BLOCK + N_EXPERTS * MOE_BLOCK
    n_blk = n_rows // MOE_BLOCK
    row_tok = jnp.zeros((n_rows,), jnp.int32).at[dest].set(stok)
    xs = ht[row_tok].reshape(n_blk, MOE_BLOCK, d)
    blk_e = jnp.minimum(jnp.searchsorted(pend, jnp.arange(n_blk) * MOE_BLOCK, side='right'), N_EXPERTS - 1)

    def expert_block(args):
        xb, e = args
        return (jax.nn.silu(xb @ w_gate[e]) * (xb @ w_up[e])) @ w_down[e]

    ys = lax.map(expert_block, (xs, blk_e)).reshape(n_rows, d)
    contrib = ys[dest].astype(jnp.float32) * sgate[:, None]
    out = jax.ops.segment_sum(contrib, stok, num_segments=t)
    return out.reshape(b, s, d).astype(h.dtype)


def setup_inputs(seed: int = 0) -> dict:
    key = jax.random.key(seed)
    keys = iter(jax.random.split(key, 40))

    def nrm(shape, scale):
        return scale * jax.random.normal(next(keys), shape, jnp.float32)

    def gain(shape):
        return 1.0 + 0.05 * jax.random.normal(next(keys), shape, jnp.float32)

    n_dense = (DEPTH + 1) // 2
    n_moe = DEPTH // 2
    return {
        'x': nrm((BATCH, SEQ, D_MODEL), 1.0),
        'mem': nrm((BATCH, MEM_LEN, D_MODEL), 1.0),
        'norm_mix': gain((DEPTH, D_MODEL)),
        'w_in': nrm((DEPTH, D_MODEL, IN_COLS), D_MODEL ** -0.5),
        'w_out': nrm((DEPTH, MIX_WIDTH, D_MODEL), MIX_WIDTH ** -0.5),
        'diff_lambda': nrm((DEPTH, 4, DIFF_DK), 0.1),
        'diff_subln': gain((DEPTH, DIFF_DV)),
        'nsa_cmp_pos': nrm((DEPTH, 2, CMP_LEN, NSA_DH), 0.1),
        'nsa_cmp_w1': nrm((DEPTH, 2, CMP_LEN * NSA_DH, CMP_HIDDEN), (CMP_LEN * NSA_DH) ** -0.5),
        'nsa_cmp_w2': nrm((DEPTH, 2, CMP_HIDDEN, NSA_DH), CMP_HIDDEN ** -0.5),
        'mla_q_norm': gain((DEPTH, MLA_Q_LORA)),
        'mla_kv_norm': gain((DEPTH, MLA_KV_LORA)),
        'mla_w_uq': nrm((DEPTH, MLA_Q_LORA, MLA_HEADS * (MLA_NOPE + MLA_ROPE)), MLA_Q_LORA ** -0.5),
        'mla_w_ukv': nrm((DEPTH, MLA_KV_LORA, MLA_HEADS * (MLA_NOPE + MLA_DV)), MLA_KV_LORA ** -0.5),
        'norm_cross': gain((DEPTH, D_MODEL)),
        'norm_mem': gain((DEPTH, D_MODEL)),
        'cross_wq': nrm((DEPTH, D_MODEL, D_MODEL), D_MODEL ** -0.5),
        'cross_wk': nrm((DEPTH, D_MODEL, D_MODEL), D_MODEL ** -0.5),
        'cross_wv': nrm((DEPTH, D_MODEL, D_MODEL), D_MODEL ** -0.5),
        'cross_wo': nrm((DEPTH, D_MODEL, D_MODEL), D_MODEL ** -0.5),
        'norm_ffn': gain((DEPTH, D_MODEL)),
        'ffn_w_gate': nrm((n_dense, D_MODEL, D_FF), D_MODEL ** -0.5),
        'ffn_w_up': nrm((n_dense, D_MODEL, D_FF), D_MODEL ** -0.5),
        'ffn_w_down': nrm((n_dense, D_FF, D_MODEL), D_FF ** -0.5),
        'moe_router': nrm((n_moe, D_MODEL, N_EXPERTS), D_MODEL ** -0.5),
        'moe_w_gate': nrm((n_moe, N_EXPERTS, D_MODEL, D_FF_EXPERT), D_MODEL ** -0.5),
        'moe_w_up': nrm((n_moe, N_EXPERTS, D_MODEL, D_FF_EXPERT), D_MODEL ** -0.5),
        'moe_w_down': nrm((n_moe, N_EXPERTS, D_FF_EXPERT, D_MODEL), D_FF_EXPERT ** -0.5),
        'final_norm': gain((D_MODEL,)),
    }


def reference(x, mem, norm_mix, w_in, w_out, diff_lambda, diff_subln, nsa_cmp_pos, nsa_cmp_w1,
              nsa_cmp_w2, mla_q_norm, mla_kv_norm, mla_w_uq, mla_w_ukv, norm_cross, norm_mem,
              cross_wq, cross_wk, cross_wv, cross_wo, norm_ffn, ffn_w_gate, ffn_w_up, ffn_w_down,
              moe_router, moe_w_gate, moe_w_up, moe_w_down, final_norm):
    for l in range(DEPTH):
        x = x + token_mixers(rmsnorm(x, norm_mix[l]), l, w_in[l], w_out[l], diff_lambda[l],
                             diff_subln[l], nsa_cmp_pos[l], nsa_cmp_w1[l], nsa_cmp_w2[l],
                             mla_q_norm[l], mla_kv_norm[l], mla_w_uq[l], mla_w_ukv[l])
        x = x + cross_attention(rmsnorm(x, norm_cross[l]), rmsnorm(mem, norm_mem[l]),
                                cross_wq[l], cross_wk[l], cross_wv[l], cross_wo[l])
        h = rmsnorm(x, norm_ffn[l])
        i = l // 2
        if l % 2 == 0:
            x = x + swiglu(h, ffn_w_gate[i], ffn_w_up[i], ffn_w_down[i])
        else:
            x = x + moe_swiglu(h, moe_router[i], moe_w_gate[i], moe_w_up[i], moe_w_down[i])
    return rmsnorm(x, final_norm)
```

```python
import functools
import math

import numpy as np
import jax
import jax.numpy as jnp
from jax import lax
from jax.experimental import pallas as pl
from jax.experimental.pallas import tpu as pltpu

F32 = jnp.float32
BF16 = jnp.bfloat16

HEAD_DIM = 64
GROUP_HEADS = 4
GROUP_WIDTH = HEAD_DIM * GROUP_HEADS
ROPE_THETA = 10000.0
NORM_EPS = 1e-6
DIFF_DK = HEAD_DIM // 2
CMP_LEN = 32
CMP_STRIDE = 16
SEL_BLOCK = 64
SEL_SHIFT = 6
SEL_TOPN = 16
NSA_WINDOW = 512
FORCED_SCORE = 1e6
MLA_NOPE = 64
MLA_ROPE = 32
MLA_DV = 64
DILATED_PATTERNS = ((128, 1), (512, 4), (2048, 16))
CROSS_HEADS = 4
N_EXPERTS = 8
TOP_K = 2

LANES = 128
V7X_VMEM_BYTES = 64 * 1024 * 1024
VMEM_LIMIT = V7X_VMEM_BYTES * 7 // 8

NEG_BIG = -1e30
SEL_BIAS = -1e9


def _cparams(*sem):
    return pltpu.CompilerParams(dimension_semantics=sem, vmem_limit_bytes=VMEM_LIMIT)


def _rms(xf, gain):
    ms = jnp.mean(xf * xf, axis=-1, keepdims=True)
    return xf * lax.rsqrt(ms + NORM_EPS) * gain


def _mm_body(*refs, nx, norm, res, splits):
    x_refs = refs[:nx]
    pos = nx
    g_ref = refs[pos] if norm else None
    pos += int(norm)
    w_ref = refs[pos]
    pos += 1
    r_ref = refs[pos] if res else None
    pos += int(res)
    o_refs = refs[pos:pos + len(splits)]
    xs_ref = refs[pos + len(splits)]

    @pl.when(pl.program_id(1) == 0)
    def _():
        off = 0
        for xr in x_refs:
            xv = xr[...]
            if norm:
                xv = _rms(xv.astype(F32), g_ref[...])
            kw = xv.shape[-1]
            xs_ref[:, off:off + kw] = xv.astype(BF16)
            off += kw

    acc = jnp.dot(xs_ref[...], w_ref[...], preferred_element_type=F32)
    if res:
        acc = acc + r_ref[...]
    off = 0
    for o_ref, width in zip(o_refs, splits):
        o_ref[...] = acc[:, off:off + width].astype(o_ref.dtype)
        off += width


def _mm(xs, w, *, gain=None, res=None, tm=512, tn=None, splits=None, out_dtype=F32, name="mm"):
    t = xs[0].shape[0]
    k, n = w.shape
    assert sum(x.shape[1] for x in xs) == k
    tn = n if tn is None else tn
    splits = (tn,) if splits is None else tuple(splits)
    assert sum(splits) == tn and (len(splits) == 1 or tn == n)
    tm = min(tm, t)
    assert t % tm == 0 and n % tn == 0
    norm = gain is not None
    assert not norm or len(xs) == 1
    in_specs = [pl.BlockSpec((tm, x.shape[1]), lambda i, j: (i, 0)) for x in xs]
    args = list(xs)
    if norm:
        in_specs.append(pl.BlockSpec((1, k), lambda i, j: (0, 0)))
        args.append(gain.reshape(1, k).astype(F32))
    in_specs.append(pl.BlockSpec((k, tn), lambda i, j: (0, j)))
    args.append(w)
    if res is not None:
        in_specs.append(pl.BlockSpec((tm, tn), lambda i, j: (i, j)))
        args.append(res)
    if len(splits) == 1:
        out_shape = [jax.ShapeDtypeStruct((t, n), out_dtype)]
        out_specs = [pl.BlockSpec((tm, tn), lambda i, j: (i, j))]
    else:
        out_shape = [jax.ShapeDtypeStruct((t, s), out_dtype) for s in splits]
        out_specs = [pl.BlockSpec((tm, s), lambda i, j: (i, 0)) for s in splits]
    outs = pl.pallas_call(
        functools.partial(_mm_body, nx=len(xs), norm=norm, res=res is not None, splits=splits),
        grid=(t // tm, n // tn),
        in_specs=in_specs,
        out_specs=out_specs,
        out_shape=out_shape,
        scratch_shapes=[pltpu.VMEM((tm, k), BF16)],
        compiler_params=_cparams("parallel", "arbitrary"),
        name=name,
    )(*args)
    return outs[0] if len(outs) == 1 else outs


def _rope_tables(s, group, width):
    half = group // 2
    pos = jnp.arange(s, dtype=F32)
    inv_freq = ROPE_THETA ** (-jnp.arange(half, dtype=F32) / half)
    ang = pos[:, None] * inv_freq[None, :]
    cos, sin = jnp.cos(ang), jnp.sin(ang)
    zero = jnp.zeros_like(sin)
    reps = width // group
    c = jnp.tile(jnp.concatenate([cos, cos], -1), (1, reps))
    s1 = jnp.tile(jnp.concatenate([zero, sin], -1), (1, reps))
    s2 = jnp.tile(jnp.concatenate([-sin, zero], -1), (1, reps))
    return c, s1, s2


def _rope(x, c, s1, s2, half):
    w = x.shape[-1]
    return x * c + pltpu.roll(x, half, 1) * s1 + pltpu.roll(x, w - half, 1) * s2


def _prep_a_body(z_ref, c_ref, s1_ref, s2_ref, q_ref, k_ref, v_ref):
    z = z_ref[0]
    qk = _rope(z[:, :512], c_ref[...], s1_ref[...], s2_ref[...], DIFF_DK // 2)
    lane = lax.broadcasted_iota(jnp.int32, (z.shape[0], HEAD_DIM), 1)
    scale = DIFF_DK ** -0.5
    for h in range(GROUP_HEADS):
        q = qk[:, h * 64:(h + 1) * 64] * scale
        q_ref[0, h, 0] = jnp.where(lane < DIFF_DK, q, 0.0).astype(BF16)
        q_ref[0, h, 1] = jnp.where(lane >= DIFF_DK, q, 0.0).astype(BF16)
        k_ref[0, h] = qk[:, 256 + h * 64:256 + (h + 1) * 64].astype(BF16)
        v_ref[0, h] = z[:, 512 + h * 64:512 + (h + 1) * 64].astype(BF16)


def _prep_a(z_a, tabs, ts):
    b, s, _ = z_a.shape
    tspec = pl.BlockSpec((ts, 512), lambda bi, i: (i, 0))
    return pl.pallas_call(
        _prep_a_body,
        grid=(b, s // ts),
        in_specs=[pl.BlockSpec((1, ts, 768), lambda bi, i: (bi, i, 0)), tspec, tspec, tspec],
        out_specs=[pl.BlockSpec((1, 4, 2, ts, 64), lambda bi, i: (bi, 0, 0, i, 0)),
                   pl.BlockSpec((1, 4, ts, 64), lambda bi, i: (bi, 0, i, 0)),
                   pl.BlockSpec((1, 4, ts, 64), lambda bi, i: (bi, 0, i, 0))],
        out_shape=[jax.ShapeDtypeStruct((b, 4, 2, s, 64), BF16),
                   jax.ShapeDtypeStruct((b, 4, s, 64), BF16),
                   jax.ShapeDtypeStruct((b, 4, s, 64), BF16)],
        compiler_params=_cparams("parallel", "parallel"),
        name="prep_a",
    )(z_a, *tabs)


def _prep_b_body(z_ref, c_ref, s1_ref, s2_ref, qn_ref, qr_ref, ka_ref, kw_ref, kc_ref, vc_ref,
                 vs_ref, vw_ref, g_ref, *, ts):
    z = z_ref[0]
    scale = HEAD_DIM ** -0.5
    rp = _rope(z[:, :384], c_ref[...], s1_ref[...], s2_ref[...], HEAD_DIM // 2)
    for h in range(GROUP_HEADS):
        qn_ref[0, h] = (z[:, h * 64:(h + 1) * 64] * scale).astype(BF16)
        qr_ref[0, h] = (rp[:, h * 64:(h + 1) * 64] * scale).astype(BF16)
    kpos = pl.program_id(1) * ts + lax.broadcasted_iota(jnp.int32, (ts, SEL_BLOCK), 0)
    lane = lax.broadcasted_iota(jnp.int32, (ts, SEL_BLOCK), 1)
    onehot = jnp.where((kpos >> SEL_SHIFT) == lane, 1.0, 0.0)
    ka_ref[0] = jnp.concatenate([rp[:, 256:320], onehot], axis=-1).astype(BF16)
    kw_ref[0] = rp[:, 320:384].astype(BF16)
    kc_ref[0] = z[:, 384:448].astype(BF16)
    vc_ref[0] = z[:, 448:512].astype(BF16)
    vs_ref[0] = z[:, 512:576].astype(BF16)
    vw_ref[0] = z[:, 576:640].astype(BF16)
    g_ref[0] = jax.nn.sigmoid(z[:, 640:768])


def _prep_b(z_b, tabs, ts):
    b, s, _ = z_b.shape
    tspec = pl.BlockSpec((ts, 384), lambda bi, i: (i, 0))
    h4 = pl.BlockSpec((1, 4, ts, 64), lambda bi, i: (bi, 0, i, 0))
    s64 = pl.BlockSpec((1, ts, 64), lambda bi, i: (bi, i, 0))
    s128 = pl.BlockSpec((1, ts, 128), lambda bi, i: (bi, i, 0))
    sd = lambda *shape, dt=BF16: jax.ShapeDtypeStruct(shape, dt)
    return pl.pallas_call(
        functools.partial(_prep_b_body, ts=ts),
        grid=(b, s // ts),
        in_specs=[pl.BlockSpec((1, ts, 768), lambda bi, i: (bi, i, 0)), tspec, tspec, tspec],
        out_specs=[h4, h4, s128, s64, s64, s64, s64, s64, s128],
        out_shape=[sd(b, 4, s, 64), sd(b, 4, s, 64), sd(b, s, 128), sd(b, s, 64), sd(b, s, 64),
                   sd(b, s, 64), sd(b, s, 64), sd(b, s, 64), sd(b, s, 128, dt=F32)],
        compiler_params=_cparams("parallel", "parallel"),
        name="prep_b",
    )(z_b, *tabs)


def _prep_c_body(z_ref, c_ref, s1_ref, s2_ref, qg_ref, kg_ref, wq_ref, wkv_ref, q_ref, k_ref, v_ref):
    z = z_ref[0]
    scale = (MLA_NOPE + MLA_ROPE) ** -0.5
    cq = _rms(z[:, :384], qg_ref[...]).astype(BF16)
    ckv = _rms(z[:, 384:512], kg_ref[...]).astype(BF16)
    qc = jnp.dot(cq, wq_ref[...], preferred_element_type=F32)
    kvc = jnp.dot(ckv, wkv_ref[...], preferred_element_type=F32)
    half = MLA_ROPE // 2
    c, s1, s2 = c_ref[...], s1_ref[...], s2_ref[...]
    q_rope = _rope(qc[:, 256:384], c, s1, s2, half)
    k_rope = _rope(z[:, 512:640], c, s1, s2, half)[:, :MLA_ROPE]
    for h in range(GROUP_HEADS):
        q = jnp.concatenate([qc[:, h * 64:(h + 1) * 64], q_rope[:, h * 32:(h + 1) * 32]], axis=-1)
        q_ref[0, h] = (q * scale).astype(BF16)
        k_ref[0, h] = jnp.concatenate([kvc[:, h * 64:(h + 1) * 64], k_rope], axis=-1).astype(BF16)
        v_ref[0, h] = kvc[:, 256 + h * 64:256 + (h + 1) * 64].astype(BF16)


def _prep_c(z_c, tabs, q_gain, kv_gain, wq, wkv, ts):
    b, s, _ = z_c.shape
    tspec = pl.BlockSpec((ts, 128), lambda bi, i: (i, 0))
    full = lambda a: pl.BlockSpec(a.shape, lambda bi, i: (0,) * a.ndim)
    dk = MLA_NOPE + MLA_ROPE
    return pl.pallas_call(
        _prep_c_body,
        grid=(b, s // ts),
        in_specs=[pl.BlockSpec((1, ts, 640), lambda bi, i: (bi, i, 0)), tspec, tspec, tspec,
                  full(q_gain), full(kv_gain), full(wq), full(wkv)],
        out_specs=[pl.BlockSpec((1, 4, ts, dk), lambda bi, i: (bi, 0, i, 0)),
                   pl.BlockSpec((1, 4, ts, dk), lambda bi, i: (bi, 0, i, 0)),
                   pl.BlockSpec((1, 4, ts, 64), lambda bi, i: (bi, 0, i, 0))],
        out_shape=[jax.ShapeDtypeStruct((b, 4, s, dk), BF16),
                   jax.ShapeDtypeStruct((b, 4, s, dk), BF16),
                   jax.ShapeDtypeStruct((b, 4, s, 64), BF16)],
        compiler_params=_cparams("parallel", "parallel"),
        name="prep_c",
    )(z_c, *tabs, q_gain, kv_gain, wq, wkv)


def _prep_d_body(z_ref, c_ref, s1_ref, s2_ref, q_ref, k_ref, v_ref):
    z = z_ref[0]
    scale = HEAD_DIM ** -0.5
    qk = _rope(z[:, :512], c_ref[...], s1_ref[...], s2_ref[...], HEAD_DIM // 2)
    for h in range(GROUP_HEADS):
        q_ref[0, h] = (qk[:, h * 64:(h + 1) * 64] * scale).astype(BF16)
        k_ref[0, h] = qk[:, 256 + h * 64:256 + (h + 1) * 64].astype(BF16)
        v_ref[0, h] = z[:, 512 + h * 64:512 + (h + 1) * 64].astype(BF16)


def _prep_d(z_d, tabs, ts):
    b, s, _ = z_d.shape
    tspec = pl.BlockSpec((ts, 512), lambda bi, i: (i, 0))
    h4 = pl.BlockSpec((1, 4, ts, 64), lambda bi, i: (bi, 0, i, 0))
    return pl.pallas_call(
        _prep_d_body,
        grid=(b, s // ts),
        in_specs=[pl.BlockSpec((1, ts, 768), lambda bi, i: (bi, i, 0)), tspec, tspec, tspec],
        out_specs=[h4, h4, h4],
        out_shape=[jax.ShapeDtypeStruct((b, 4, s, 64), BF16)] * 3,
        compiler_params=_cparams("parallel", "parallel"),
        name="prep_d",
    )(z_d, *tabs)


def _flash_schedule(nq, tq, tk, sk, window):
    qi, kj, fl = [], [], []
    for i in range(nq):
        q_lo, q_hi = i * tq, i * tq + tq - 1
        j_hi = min(q_hi, sk - 1) // tk
        j_lo = 0 if window is None else max(0, q_lo - window) // tk
        for j in range(j_lo, j_hi + 1):
            k_min, k_max = j * tk, j * tk + tk - 1
            full = k_max <= q_lo and (window is None or q_hi - k_min <= window)
            qi.append(i)
            kj.append(j)
            fl.append((1 if j == j_lo else 0) | (2 if j == j_hi else 0) | (0 if full else 4))
    return (np.asarray(qi, np.int32), np.asarray(kj, np.int32), np.asarray(fl, np.int32))


def _flash_body(qi_ref, kj_ref, fl_ref, q_ref, k_ref, v_ref, o_ref, m_sc, l_sc, acc_sc, *,
                gb, hq, tq, tk, window, with_lse):
    t = pl.program_id(1)
    fl = fl_ref[t]
    rows = hq * tq

    @pl.when((fl & 1) != 0)
    def _():
        m_sc[...] = jnp.full(m_sc.shape, NEG_BIG, F32)
        l_sc[...] = jnp.zeros(l_sc.shape, F32)
        acc_sc[...] = jnp.zeros(acc_sc.shape, F32)

    def step(masked):
        q = q_ref[...].reshape(gb, rows, q_ref.shape[-1])
        s = jnp.einsum("gqd,gkd->gqk", q, k_ref[...], preferred_element_type=F32)
        if masked:
            row = lax.broadcasted_iota(jnp.int32, (rows, tk), 0)
            if hq > 1:
                row = row & (tq - 1)
            col = lax.broadcasted_iota(jnp.int32, (rows, tk), 1)
            dist = (qi_ref[t] * tq - kj_ref[t] * tk) + row - col
            ok = dist >= 0
            if window is not None:
                ok = ok & (dist <= window)
            s = jnp.where(ok[None], s, NEG_BIG)
        m_prev = m_sc[...]
        m_new = jnp.maximum(m_prev, jnp.max(s, axis=-1, keepdims=True))
        alpha = jnp.exp(m_prev - m_new)
        p = jnp.exp(s - m_new)
        l_sc[...] = alpha * l_sc[...] + jnp.sum(p, axis=-1, keepdims=True)
        acc_sc[...] = alpha * acc_sc[...] + jnp.einsum(
            "gqk,gkd->gqd", p.astype(BF16), v_ref[...], preferred_element_type=F32)
        m_sc[...] = m_new

    @pl.when((fl & 4) != 0)
    def _():
        step(True)

    @pl.when((fl & 4) == 0)
    def _():
        step(False)

    @pl.when((fl & 2) != 0)
    def _():
        l = l_sc[...]
        o = acc_sc[...] / l
        if with_lse:
            lse = m_sc[...] + jnp.log(l)
            o = jnp.concatenate([o, jnp.broadcast_to(lse, o.shape)], axis=-1)
        o_ref[...] = o.reshape(o_ref.shape).astype(o_ref.dtype)


def _flash(q, k, v, *, tq, tk, gb=1, window=None, with_lse=False, name="flash"):
    g, hq, sq, dk = q.shape
    _, sk, dv = v.shape
    tq, tk = min(tq, sq), min(tk, sk)
    assert g % gb == 0 and sq % tq == 0 and sk % tk == 0 and (gb == 1 or hq == 1)
    assert tq & (tq - 1) == 0
    qi, kj, fl = _flash_schedule(sq // tq, tq, tk, sk, window)
    dvo = 2 * dv if with_lse else dv
    rows = hq * tq
    grid_spec = pltpu.PrefetchScalarGridSpec(
        num_scalar_prefetch=3,
        grid=(g // gb, len(qi)),
        in_specs=[pl.BlockSpec((gb, hq, tq, dk), lambda gi, t, qi, kj, fl: (gi, 0, qi[t], 0)),
                  pl.BlockSpec((gb, tk, dk), lambda gi, t, qi, kj, fl: (gi, kj[t], 0)),
                  pl.BlockSpec((gb, tk, dv), lambda gi, t, qi, kj, fl: (gi, kj[t], 0))],
        out_specs=pl.BlockSpec((gb, hq, tq, dvo), lambda gi, t, qi, kj, fl: (gi, 0, qi[t], 0)),
        scratch_shapes=[pltpu.VMEM((gb, rows, 1), F32), pltpu.VMEM((gb, rows, 1), F32),
                        pltpu.VMEM((gb, rows, dv), F32)],
    )
    return pl.pallas_call(
        functools.partial(_flash_body, gb=gb, hq=hq, tq=tq, tk=tk, window=window, with_lse=with_lse),
        grid_spec=grid_spec,
        out_shape=jax.ShapeDtypeStruct((g, hq, sq, dvo), F32),
        compiler_params=_cparams("parallel", "arbitrary"),
        name=name,
    )(jnp.asarray(qi), jnp.asarray(kj), jnp.asarray(fl), q, k, v)


def _gelu_tanh(x):
    return 0.5 * x * (1.0 + jnp.tanh(math.sqrt(2.0 / math.pi) * (x + 0.044715 * (x * x * x))))


def _compress_body(r_ref, pos_ref, w1_ref, w2_ref, o_ref):
    half = CMP_STRIDE * HEAD_DIM
    r = r_ref[0, 0]
    w1 = w1_ref[0]
    top = jnp.dot(r, w1[:half], preferred_element_type=F32)
    bot = jnp.dot(r, w1[half:], preferred_element_type=F32)
    nc = r.shape[0]
    posb = jnp.dot(pos_ref[0], w1, preferred_element_type=F32)[:1]
    pre = top + pltpu.roll(bot, nc - 1, 0) + posb
    hid = _gelu_tanh(pre).astype(BF16)
    o_ref[0, 0] = jnp.dot(hid, w2_ref[0], preferred_element_type=F32).astype(o_ref.dtype)


def _nsa_compress(kv_rows, pos_flat, w1, w2):
    _, b, nc, _ = kv_rows.shape
    return pl.pallas_call(
        _compress_body,
        grid=(2, b),
        in_specs=[pl.BlockSpec((1, 1, nc, 1024), lambda w, bi: (w, bi, 0, 0)),
                  pl.BlockSpec((1, 8, 2048), lambda w, bi: (w, 0, 0)),
                  pl.BlockSpec((1, 2048, 256), lambda w, bi: (w, 0, 0)),
                  pl.BlockSpec((1, 256, 64), lambda w, bi: (w, 0, 0))],
        out_specs=pl.BlockSpec((1, 1, nc, 64), lambda w, bi: (w, bi, 0, 0)),
        out_shape=jax.ShapeDtypeStruct((2, b, nc, 64), BF16),
        compiler_params=_cparams("parallel", "parallel"),
        name="nsa_compress",
    )(kv_rows, pos_flat, w1, w2)


def _cmp_select_body(qn_ref, qr_ref, kc_ref, vc_ref, cov_ref, o_ref, qa_ref, *, tq, n_sel, k_top):
    ncp = kc_ref.shape[2]
    pos = pl.program_id(1) * tq + lax.broadcasted_iota(jnp.int32, (tq, ncp), 0)
    blk_end = lax.broadcasted_iota(jnp.int32, (tq, ncp), 1) * CMP_STRIDE + (CMP_LEN - 1)
    mask = blk_end <= pos
    kc, vc = kc_ref[0, 0], vc_ref[0, 0]
    p_sum = jnp.zeros((tq, ncp), F32)
    for h in range(GROUP_HEADS):
        s = lax.dot_general(qn_ref[0, h], kc, (((1,), (1,)), ((), ())), preferred_element_type=F32)
        s = jnp.where(mask, s, NEG_BIG)
        mx = jnp.max(s, axis=-1, keepdims=True)
        mx = jnp.where(mx > 0.5 * NEG_BIG, mx, 0.0)
        e = jnp.where(mask, jnp.exp(s - mx), 0.0)
        p = e / jnp.maximum(jnp.sum(e, axis=-1, keepdims=True), 1e-30)
        o_ref[0, h] = jnp.dot(p.astype(BF16), vc, preferred_element_type=F32)
        p_sum = p_sum + p
    p_hi = p_sum.astype(BF16)
    p_lo = (p_sum - p_hi.astype(F32)).astype(BF16)
    cov = cov_ref[...]
    imp = (jnp.dot(p_hi, cov, preferred_element_type=F32) + jnp.dot(p_lo, cov, preferred_element_type=F32))
    lanes = cov.shape[1]
    tpos = pl.program_id(1) * tq + lax.broadcasted_iota(jnp.int32, (tq, lanes), 0)
    cur = tpos >> SEL_SHIFT
    j = lax.broadcasted_iota(jnp.int32, (tq, lanes), 1)
    valid = j <= cur
    forced = (j == 0) | (j == cur) | (j == cur - 1)
    score = jnp.where(valid, jnp.where(forced, FORCED_SCORE, imp), -jnp.inf)
    rank = jnp.zeros((tq, lanes), jnp.int32)
    for i in range(n_sel):
        col = score[:, i:i + 1]
        ahead = (col > score) | ((col == score) & (i < j))
        rank = rank + ahead.astype(jnp.int32)
    bias = jnp.where(valid & (rank < k_top), 0.0, SEL_BIAS).astype(BF16)
    for h in range(GROUP_HEADS):
        qa_ref[0, h] = jnp.concatenate([qr_ref[0, h], bias], axis=-1)


def _cover_matrix(ncp, n_sel, lanes):
    c_start = np.arange(ncp)[:, None] * CMP_STRIDE
    s_start = np.arange(lanes)[None, :] * SEL_BLOCK
    cov = np.clip(np.minimum(c_start + CMP_LEN, s_start + SEL_BLOCK) - np.maximum(c_start, s_start), 0, None)
    cov = cov.astype(np.float32) / CMP_LEN
    cov[:, n_sel:] = 0.0
    cov[ncp - 1:, :] = 0.0
    return cov


def _cmp_select(qn, qr, kvc, tq):
    b, _, s, _ = qn.shape
    ncp = kvc.shape[2]
    n_sel = s // SEL_BLOCK
    assert n_sel <= SEL_BLOCK
    k_top = min(SEL_TOPN, n_sel)
    cov = jnp.asarray(_cover_matrix(ncp, n_sel, SEL_BLOCK), BF16)
    h4 = lambda d: pl.BlockSpec((1, 4, tq, d), lambda bi, i: (bi, 0, i, 0))
    return pl.pallas_call(
        functools.partial(_cmp_select_body, tq=tq, n_sel=n_sel, k_top=k_top),
        grid=(b, s // tq),
        in_specs=[h4(64), h4(64),
                  pl.BlockSpec((1, 1, ncp, 64), lambda bi, i: (0, bi, 0, 0)),
                  pl.BlockSpec((1, 1, ncp, 64), lambda bi, i: (1, bi, 0, 0)),
                  pl.BlockSpec(cov.shape, lambda bi, i: (0, 0))],
        out_specs=[h4(64), h4(128)],
        out_shape=[jax.ShapeDtypeStruct((b, 4, s, 64), F32), jax.ShapeDtypeStruct((b, 4, s, 128), BF16)],
        compiler_params=_cparams("parallel", "parallel"),
        name="nsa_cmp_select",
    )(qn, qr, kvc, kvc, cov)


def _merge_a_body(o_ref, lam_ref, g_ref, y_ref, *, out_scale):
    outs = []
    for h in range(GROUP_HEADS):
        o = o_ref[0, h, 0] - lam_ref[...] * o_ref[0, h, 1]
        outs.append(_rms(o, g_ref[...]) * out_scale)
    y_ref[0] = jnp.concatenate(outs, axis=-1).astype(y_ref.dtype)


def _merge_a(o, lam, subln, out_scale, ts):
    b, _, _, s, _ = o.shape
    return pl.pallas_call(
        functools.partial(_merge_a_body, out_scale=out_scale),
        grid=(b, s // ts),
        in_specs=[pl.BlockSpec((1, 4, 2, ts, 64), lambda bi, i: (bi, 0, 0, i, 0)),
                  pl.BlockSpec((1, 64), lambda bi, i: (0, 0)),
                  pl.BlockSpec((1, 64), lambda bi, i: (0, 0))],
        out_specs=pl.BlockSpec((1, ts, 256), lambda bi, i: (bi, i, 0)),
        out_shape=jax.ShapeDtypeStruct((b, s, 256), BF16),
        compiler_params=_cparams("parallel", "parallel"),
        name="merge_a",
    )(o, lam, subln)


def _merge_b_body(oc_ref, os_ref, ow_ref, g_ref, y_ref):
    g = g_ref[0]
    outs = []
    for h in range(GROUP_HEADS):
        outs.append(g[:, 3 * h:3 * h + 1] * oc_ref[0, h] + g[:, 3 * h + 1:3 * h + 2] * os_ref[0, h]
                    + g[:, 3 * h + 2:3 * h + 3] * ow_ref[0, h])
    y_ref[0] = jnp.concatenate(outs, axis=-1).astype(y_ref.dtype)


def _merge_b(o_cmp, o_sel, o_win, gates, ts):
    b, _, s, _ = o_cmp.shape
    h4 = pl.BlockSpec((1, 4, ts, 64), lambda bi, i: (bi, 0, i, 0))
    return pl.pallas_call(
        _merge_b_body,
        grid=(b, s // ts),
        in_specs=[h4, h4, h4, pl.BlockSpec((1, ts, 128), lambda bi, i: (bi, i, 0))],
        out_specs=pl.BlockSpec((1, ts, 256), lambda bi, i: (bi, i, 0)),
        out_shape=jax.ShapeDtypeStruct((b, s, 256), BF16),
        compiler_params=_cparams("parallel", "parallel"),
        name="merge_b",
    )(o_cmp, o_sel, o_win, gates)


def _merge_c_body(o_ref, y_ref):
    y_ref[0] = jnp.concatenate([o_ref[0, h] for h in range(GROUP_HEADS)], axis=-1).astype(y_ref.dtype)


def _merge_c(o, ts):
    b, _, s, _ = o.shape
    return pl.pallas_call(
        _merge_c_body,
        grid=(b, s // ts),
        in_specs=[pl.BlockSpec((1, 4, ts, 64), lambda bi, i: (bi, 0, i, 0))],
        out_specs=pl.BlockSpec((1, ts, 256), lambda bi, i: (bi, i, 0)),
        out_shape=jax.ShapeDtypeStruct((b, s, 256), BF16),
        compiler_params=_cparams("parallel", "parallel"),
        name="merge_c",
    )(o)


def _merge_d_body(o1_ref, o2_ref, o3_ref, y_ref):
    outs = []
    for h in range(GROUP_HEADS):
        e = [r[0, h] for r in (o1_ref, o2_ref, o3_ref)]
        lse = [x[:, 64:] for x in e]
        mx = jnp.maximum(jnp.maximum(lse[0], lse[1]), lse[2])
        w = [jnp.exp(x - mx) for x in lse]
        den = w[0] + w[1] + w[2]
        outs.append((w[0] * e[0][:, :64] + w[1] * e[1][:, :64] + w[2] * e[2][:, :64]) / den)
    y_ref[0] = jnp.concatenate(outs, axis=-1).astype(y_ref.dtype)


def _merge_d(o1, o2, o3, ts):
    b, _, s, _ = o1.shape
    h4 = pl.BlockSpec((1, 4, ts, 128), lambda bi, i: (bi, 0, i, 0))
    return pl.pallas_call(
        _merge_d_body,
        grid=(b, s // ts),
        in_specs=[h4, h4, h4],
        out_specs=pl.BlockSpec((1, ts, 256), lambda bi, i: (bi, i, 0)),
        out_shape=jax.ShapeDtypeStruct((b, s, 256), BF16),
        compiler_params=_cparams("parallel", "parallel"),
        name="merge_d",
    )(o1, o2, o3)


def _permute_w_in(w_in):
    a_cols = 768
    b0 = a_cols
    nq = 256
    seg = lambda i: w_in[:, b0 + nq + i * 64: b0 + nq + (i + 1) * 64]
    kc, vc, ks, vs, kw, vw = (seg(i) for i in range(6))
    gb = w_in[:, b0 + nq + 384: b0 + nq + 384 + 12]
    d = w_in.shape[0]
    w_b = jnp.concatenate([w_in[:, b0:b0 + nq], ks, kw, kc, vc, vs, vw, gb, jnp.zeros((d, 116), w_in.dtype)], axis=1)
    c0 = b0 + 652
    w_c = jnp.concatenate([w_in[:, c0:c0 + 544], jnp.zeros((d, 96), w_in.dtype)], axis=1)
    d0 = c0 + 544
    return jnp.concatenate([w_in[:, :a_cols], w_b, w_c, w_in[:, d0:d0 + 768]], axis=1).astype(BF16)


def _residue_classes(t, dil):
    g, s, d = t.shape
    return t.reshape(g, s // dil, dil, d).transpose(0, 2, 1, 3).reshape(g * dil, s // dil, d)


def _from_residue_classes(t, dil):
    gd, l, d = t.shape
    return t.reshape(gd // dil, dil, l, d).transpose(0, 2, 1, 3).reshape(gd // dil, l * dil, d)


def _token_mixers(x, layer, rope_tabs, norm_g, w_in, w_out, diff_lambda, diff_subln, cmp_pos, cmp_w1, cmp_w2,
                  q_norm, kv_norm, w_uq, w_ukv):
    b, s, d = x.shape
    t = b * s
    ts = min(512, s)
    z_a, z_b, z_c, z_d = _mm([x.reshape(t, d)], _permute_w_in(w_in), gain=norm_g, tm=min(512, t),
                             splits=(768, 768, 640, 768), name="w_in")
    z_a, z_b, z_c, z_d = (z.reshape(b, s, -1) for z in (z_a, z_b, z_c, z_d))

    qa, ka, va = _prep_a(z_a, rope_tabs["g32w512"], ts)
    oa = _flash(qa.reshape(b * 4, 2, s, 64), ka.reshape(b * 4, s, 64), va.reshape(b * 4, s, 64),
                tq=512, tk=512, name="flash_diff").reshape(b, 4, 2, s, 64)
    lam_init = 0.8 - 0.6 * math.exp(-0.3 * layer)
    lp = diff_lambda.astype(F32)
    lam = jnp.exp(jnp.sum(lp[0] * lp[1])) - jnp.exp(jnp.sum(lp[2] * lp[3])) + lam_init
    y_a = _merge_a(oa, jnp.full((1, 64), lam, F32), diff_subln.reshape(1, 64).astype(F32), 1.0 - lam_init, ts)

    qn, qr, k_aug, kw, kc, vc, vs, vw, gates = _prep_b(z_b, rope_tabs["g64w384"], ts)
    kv_rows = jnp.stack([kc, vc]).reshape(2, b, s // CMP_STRIDE, CMP_STRIDE * HEAD_DIM)
    pos_flat = jnp.broadcast_to(cmp_pos.reshape(2, 1, CMP_LEN * HEAD_DIM), (2, 8, CMP_LEN * HEAD_DIM)).astype(BF16)
    kvc = _nsa_compress(kv_rows, pos_flat, cmp_w1.astype(BF16), cmp_w2.astype(BF16))
    o_cmp, q_aug = _cmp_select(qn, qr, kvc, min(256, s))
    o_sel = _flash(q_aug, k_aug, vs, tq=256, tk=512, name="flash_sel")
    o_win = _flash(qr, kw, vw, tq=256, tk=256, window=NSA_WINDOW - 1, name="flash_win")
    y_b = _merge_b(o_cmp, o_sel, o_win, gates, ts)

    wq = w_uq.reshape(-1, 4, MLA_NOPE + MLA_ROPE)
    wq = jnp.concatenate([wq[:, :, :MLA_NOPE].reshape(-1, 256), wq[:, :, MLA_NOPE:].reshape(-1, 128)], axis=1)
    wkv = w_ukv.reshape(-1, 4, MLA_NOPE + MLA_DV)
    wkv = jnp.concatenate([wkv[:, :, :MLA_NOPE].reshape(-1, 256), wkv[:, :, MLA_NOPE:].reshape(-1, 256)], axis=1)
    qc, kcat, vcv = _prep_c(z_c, rope_tabs["g32w128"], q_norm.reshape(1, -1).astype(F32),
                            kv_norm.reshape(1, -1).astype(F32), wq.astype(BF16), wkv.astype(BF16), ts)
    dk = MLA_NOPE + MLA_ROPE
    oc = _flash(qc.reshape(b * 4, 1, s, dk), kcat.reshape(b * 4, s, dk), vcv.reshape(b * 4, s, 64),
                tq=512, tk=512, name="flash_mla").reshape(b, 4, s, 64)
    y_c = _merge_c(oc, ts)

    qd, kd, vd = (a.reshape(b * 4, s, 64) for a in _prep_d(z_d, rope_tabs["g64w512"], ts))
    o_d = []
    for window, dil in DILATED_PATTERNS:
        l = s // dil
        qq, kk, vv = (_residue_classes(a, dil) for a in (qd, kd, vd))
        g = qq.shape[0]
        gb = math.gcd(g, 16)
        o = _flash(qq.reshape(g, 1, l, 64), kk, vv, tq=128, tk=128, gb=gb, window=window // dil,
                   with_lse=True, name=f"flash_dil{dil}")
        o_d.append(_from_residue_classes(o.reshape(g, l, 128), dil).reshape(b, 4, s, 128))
    y_d = _merge_d(*o_d, ts)

    ys = [y.reshape(t, 256) for y in (y_a, y_b, y_c, y_d)]
    return _mm(ys, w_out.astype(BF16), res=x.reshape(t, d), tm=min(512, t), name="w_out").reshape(b, s, d)


def _cross_body(x_ref, g_ref, wq_ref, k_ref, v_ref, wo_ref, o_ref):
    x = x_ref[0]
    h = _rms(x, g_ref[...]).astype(BF16)
    q = jnp.dot(h, wq_ref[...], preferred_element_type=F32).astype(BF16)
    dh = q.shape[-1] // CROSS_HEADS
    outs = []
    for hd in range(CROSS_HEADS):
        sl = slice(hd * dh, (hd + 1) * dh)
        s = lax.dot_general(q[:, sl], k_ref[0, :, sl], (((1,), (1,)), ((), ())), preferred_element_type=F32)
        e = jnp.exp(s - jnp.max(s, axis=-1, keepdims=True))
        p = e / jnp.sum(e, axis=-1, keepdims=True)
        outs.append(jnp.dot(p.astype(BF16), v_ref[0, :, sl], preferred_element_type=F32))
    o = jnp.concatenate(outs, axis=-1).astype(BF16)
    o_ref[0] = x + jnp.dot(o, wo_ref[...], preferred_element_type=F32)


def _cross_attention(x, mem, g_x, g_mem, wq, wk, wv, wo):
    b, s, d = x.shape
    m = mem.shape[1]
    ts = min(512, s)
    dh = d // CROSS_HEADS
    k, v = _mm([mem.reshape(b * m, d)], jnp.concatenate([wk, wv], axis=1).astype(BF16), gain=g_mem,
               tm=min(512, b * m), splits=(d, d), out_dtype=BF16, name="cross_kv")
    full = lambda a: pl.BlockSpec(a.shape, lambda bi, i: (0,) * a.ndim)
    wq_s = (wq * dh ** -0.5).astype(BF16)
    wo_b = wo.astype(BF16)
    g2 = g_x.reshape(1, d).astype(F32)
    return pl.pallas_call(
        _cross_body,
        grid=(b, s // ts),
        in_specs=[pl.BlockSpec((1, ts, d), lambda bi, i: (bi, i, 0)), full(g2), full(wq_s),
                  pl.BlockSpec((1, m, d), lambda bi, i: (bi, 0, 0)),
                  pl.BlockSpec((1, m, d), lambda bi, i: (bi, 0, 0)), full(wo_b)],
        out_specs=pl.BlockSpec((1, ts, d), lambda bi, i: (bi, i, 0)),
        out_shape=jax.ShapeDtypeStruct((b, s, d), F32),
        compiler_params=_cparams("parallel", "parallel"),
        name="cross_attn",
    )(x, g2, wq_s, k.reshape(b, m, d), v.reshape(b, m, d), wo_b)


def _swiglu_body(x_ref, g_ref, wg_ref, wu_ref, wd_ref, o_ref, h_sc, acc_sc):
    f = pl.program_id(1)

    @pl.when(f == 0)
    def _():
        h_sc[...] = _rms(x_ref[...], g_ref[...]).astype(BF16)
        acc_sc[...] = x_ref[...]

    h = h_sc[...]
    gate = jnp.dot(h, wg_ref[...], preferred_element_type=F32)
    up = jnp.dot(h, wu_ref[...], preferred_element_type=F32)
    act = (gate * jax.nn.sigmoid(gate) * up).astype(BF16)
    acc_sc[...] += jnp.dot(act, wd_ref[...], preferred_element_type=F32)

    @pl.when(f == pl.num_programs(1) - 1)
    def _():
        o_ref[...] = acc_sc[...]


def _swiglu(x, gain, wg, wu, wd, tm=512, tf=1408):
    t, d = x.shape
    ff = wg.shape[1]
    tm, tf = min(tm, t), min(tf, ff)
    assert t % tm == 0 and ff % tf == 0
    return pl.pallas_call(
        _swiglu_body,
        grid=(t // tm, ff // tf),
        in_specs=[pl.BlockSpec((tm, d), lambda i, f: (i, 0)),
                  pl.BlockSpec((1, d), lambda i, f: (0, 0)),
                  pl.BlockSpec((d, tf), lambda i, f: (0, f)),
                  pl.BlockSpec((d, tf), lambda i, f: (0, f)),
                  pl.BlockSpec((tf, d), lambda i, f: (f, 0))],
        out_specs=pl.BlockSpec((tm, d), lambda i, f: (i, 0)),
        out_shape=jax.ShapeDtypeStruct((t, d), F32),
        scratch_shapes=[pltpu.VMEM((tm, d), BF16), pltpu.VMEM((tm, d), F32)],
        compiler_params=_cparams("parallel", "arbitrary"),
        name="swiglu",
    )(x, gain.reshape(1, d).astype(F32), wg.astype(BF16), wu.astype(BF16), wd.astype(BF16))


def _router_body(x_ref, g_ref, r_ref, h_ref, lg_ref):
    h = _rms(x_ref[...], g_ref[...])
    h_hi = h.astype(BF16)
    h_ref[...] = h_hi
    h_lo = (h - h_hi.astype(F32)).astype(BF16)
    r = r_ref[...]
    r_hi = r.astype(BF16)
    r_lo = (r - r_hi.astype(F32)).astype(BF16)
    lg_ref[...] = (jnp.dot(h_hi, r_hi, preferred_element_type=F32) + jnp.dot(h_lo, r_hi, preferred_element_type=F32)
                   + jnp.dot(h_hi, r_lo, preferred_element_type=F32))


def _router(x, gain, router, tm=512):
    t, d = x.shape
    tm = min(tm, t)
    r_pad = jnp.zeros((d, LANES), F32).at[:, :N_EXPERTS].set(router.astype(F32))
    return pl.pallas_call(
        _router_body,
        grid=(t // tm,),
        in_specs=[pl.BlockSpec((tm, d), lambda i: (i, 0)), pl.BlockSpec((1, d), lambda i: (0, 0)),
                  pl.BlockSpec((d, LANES), lambda i: (0, 0))],
        out_specs=[pl.BlockSpec((tm, d), lambda i: (i, 0)), pl.BlockSpec((tm, LANES), lambda i: (i, 0))],
        out_shape=[jax.ShapeDtypeStruct((t, d), BF16), jax.ShapeDtypeStruct((t, LANES), F32)],
        compiler_params=_cparams("parallel"),
        name="moe_router",
    )(x, gain.reshape(1, d).astype(F32), r_pad)


def _experts_body(be_ref, nb_ref, x_ref, wg_ref, wu_ref, wd_ref, o_ref, acc_sc):
    i, f = pl.program_id(0), pl.program_id(1)

    @pl.when(i < nb_ref[0])
    def _():
        @pl.when(f == 0)
        def _():
            acc_sc[...] = jnp.zeros(acc_sc.shape, F32)

        x = x_ref[...]
        gate = jnp.dot(x, wg_ref[0], preferred_element_type=F32)
        up = jnp.dot(x, wu_ref[0], preferred_element_type=F32)
        act = (gate * jax.nn.sigmoid(gate) * up).astype(BF16)
        acc_sc[...] += jnp.dot(act, wd_ref[0], preferred_element_type=F32)

        @pl.when(f == pl.num_programs(1) - 1)
        def _():
            o_ref[...] = acc_sc[...]


def _experts(xs, blk_e, n_used, wg, wu, wd, rows, tf=1792):
    n_rows, d = xs.shape
    ff = wg.shape[2]
    n_blk = n_rows // rows
    grid_spec = pltpu.PrefetchScalarGridSpec(
        num_scalar_prefetch=2,
        grid=(n_blk, ff // tf),
        in_specs=[pl.BlockSpec((rows, d), lambda i, f, be, nb: (jnp.minimum(i, nb[0] - 1), 0)),
                  pl.BlockSpec((1, d, tf), lambda i, f, be, nb: (be[i], 0, jnp.where(i < nb[0], f, ff // tf - 1))),
                  pl.BlockSpec((1, d, tf), lambda i, f, be, nb: (be[i], 0, jnp.where(i < nb[0], f, ff // tf - 1))),
                  pl.BlockSpec((1, tf, d), lambda i, f, be, nb: (be[i], jnp.where(i < nb[0], f, ff // tf - 1), 0))],
        out_specs=pl.BlockSpec((rows, d), lambda i, f, be, nb: (i, 0)),
        scratch_shapes=[pltpu.VMEM((rows, d), F32)],
    )
    return pl.pallas_call(
        _experts_body,
        grid_spec=grid_spec,
        out_shape=jax.ShapeDtypeStruct((n_rows, d), F32),
        compiler_params=_cparams("arbitrary", "arbitrary"),
        name="moe_experts",
    )(blk_e, n_used, xs, wg, wu, wd)


MOE_ROWS = 512


def _moe(x, gain, router, w_gate, w_up, w_down):
    t, d = x.shape
    h, logits = _router(x, gain, router)
    top_logit, top_e = lax.top_k(logits[:, :N_EXPERTS], TOP_K)
    gate = jax.nn.softmax(top_logit, axis=-1)
    n_assign = t * TOP_K
    flat_e = top_e.reshape(-1)
    onehot = (flat_e[:, None] == jnp.arange(N_EXPERTS)[None, :]).astype(jnp.int32)
    before = jnp.cumsum(onehot, axis=0) - onehot
    counts = jnp.sum(onehot, axis=0)
    padded = (counts + MOE_ROWS - 1) // MOE_ROWS * MOE_ROWS
    pend = jnp.cumsum(padded)
    pstart = pend - padded
    dest = pstart[flat_e] + jnp.sum(before * onehot, axis=1)
    n_rows = -(-n_assign // MOE_ROWS) * MOE_ROWS + N_EXPERTS * MOE_ROWS
    n_blk = n_rows // MOE_ROWS
    row_tok = jnp.zeros((n_rows,), jnp.int32).at[dest].set(jnp.arange(n_assign, dtype=jnp.int32) // TOP_K)
    blk_e = jnp.minimum(jnp.searchsorted(pend, jnp.arange(n_blk) * MOE_ROWS, side="right"), N_EXPERTS - 1)
    n_used = (pend[-1] // MOE_ROWS).astype(jnp.int32).reshape(1)
    xs = h[row_tok]
    ys = _experts(xs, blk_e.astype(jnp.int32), n_used, w_gate.astype(BF16), w_up.astype(BF16),
                  w_down.astype(BF16), MOE_ROWS)
    contrib = ys[dest].reshape(t, TOP_K, d) * gate[:, :, None]
    return x + contrib[:, 0] + contrib[:, 1]


def _final_norm_body(x_ref, g_ref, o_ref):
    o_ref[...] = _rms(x_ref[...], g_ref[...])


def _final_norm(x, gain, tm=1024):
    t, d = x.shape
    tm = min(tm, t)
    return pl.pallas_call(
        _final_norm_body,
        grid=(t // tm,),
        in_specs=[pl.BlockSpec((tm, d), lambda i: (i, 0)), pl.BlockSpec((1, d), lambda i: (0, 0))],
        out_specs=pl.BlockSpec((tm, d), lambda i: (i, 0)),
        out_shape=jax.ShapeDtypeStruct((t, d), F32),
        compiler_params=_cparams("parallel"),
        name="final_norm",
    )(x, gain.reshape(1, d).astype(F32))


def _all_rope_tables(s):
    return {"g32w512": _rope_tables(s, 32, 512), "g64w384": _rope_tables(s, 64, 384),
            "g32w128": _rope_tables(s, 32, 128), "g64w512": _rope_tables(s, 64, 512)}


def kernel(x, mem, norm_mix, w_in, w_out, diff_lambda, diff_subln, nsa_cmp_pos, nsa_cmp_w1, nsa_cmp_w2, mla_q_norm, mla_kv_norm, mla_w_uq, mla_w_ukv, norm_cross, norm_mem, cross_wq, cross_wk, cross_wv, cross_wo, norm_ffn, ffn_w_gate, ffn_w_up, ffn_w_down, moe_router, moe_w_gate, moe_w_up, moe_w_down, final_norm):
    b, s, d = x.shape
    depth = w_in.shape[0]
    tabs = _all_rope_tables(s)
    for l in range(depth):
        x = _token_mixers(x, l, tabs, norm_mix[l], w_in[l], w_out[l], diff_lambda[l], diff_subln[l],
                          nsa_cmp_pos[l], nsa_cmp_w1[l], nsa_cmp_w2[l], mla_q_norm[l], mla_kv_norm[l],
                          mla_w_uq[l], mla_w_ukv[l])
        x = _cross_attention(x, mem, norm_cross[l], norm_mem[l], cross_wq[l], cross_wk[l], cross_wv[l],
                             cross_wo[l])
        xt = x.reshape(b * s, d)
        i = l // 2
        if l % 2 == 0:
            xt = _swiglu(xt, norm_ffn[l], ffn_w_gate[i], ffn_w_up[i], ffn_w_down[i])
        else:
            xt = _moe(xt, norm_ffn[l], moe_router[i], moe_w_gate[i], moe_w_up[i], moe_w_down[i])
        x = xt.reshape(b, s, d)
    return _final_norm(x.reshape(b * s, d), final_norm).reshape(b, s, d)
```

```python
import functools
import math

import numpy as np
import jax
import jax.numpy as jnp
from jax import lax
from jax.experimental import pallas as pl
from jax.experimental.pallas import tpu as pltpu

F32 = jnp.float32
BF16 = jnp.bfloat16

HEAD_DIM = 64
GROUP_HEADS = 4
GROUP_WIDTH = HEAD_DIM * GROUP_HEADS
ROPE_THETA = 10000.0
NORM_EPS = 1e-6
DIFF_DK = HEAD_DIM // 2
CMP_LEN = 32
CMP_STRIDE = 16
SEL_BLOCK = 64
SEL_SHIFT = 6
SEL_TOPN = 16
NSA_WINDOW = 512
FORCED_SCORE = 1e6
MLA_NOPE = 64
MLA_ROPE = 32
MLA_DV = 64
DILATED_PATTERNS = ((128, 1), (512, 4), (2048, 16))
CROSS_HEADS = 4
N_EXPERTS = 8
TOP_K = 2

LANES = 128
V7X_VMEM_BYTES = 64 * 1024 * 1024
VMEM_LIMIT = V7X_VMEM_BYTES * 7 // 8

NEG_BIG = -1e30
SEL_BIAS = -1e9


def _cparams(*sem):
    return pltpu.CompilerParams(dimension_semantics=sem, vmem_limit_bytes=VMEM_LIMIT)


def _rms(xf, gain):
    ms = jnp.mean(xf * xf, axis=-1, keepdims=True)
    return xf * lax.rsqrt(ms + NORM_EPS) * gain


def _mm_body(*refs, nx, norm, res, splits):
    x_refs = refs[:nx]
    pos = nx
    g_ref = refs[pos] if norm else None
    pos += int(norm)
    w_ref = refs[pos]
    pos += 1
    r_ref = refs[pos] if res else None
    pos += int(res)
    o_refs = refs[pos:pos + len(splits)]
    xs_ref = refs[pos + len(splits)]

    @pl.when(pl.program_id(1) == 0)
    def _():
        off = 0
        for xr in x_refs:
            xv = xr[...]
            if norm:
                xv = _rms(xv.astype(F32), g_ref[...])
            kw = xv.shape[-1]
            xs_ref[:, off:off + kw] = xv.astype(BF16)
            off += kw

    acc = jnp.dot(xs_ref[...], w_ref[...], preferred_element_type=F32)
    if res:
        acc = acc + r_ref[...]
    off = 0
    for o_ref, width in zip(o_refs, splits):
        o_ref[...] = acc[:, off:off + width].astype(o_ref.dtype)
        off += width


def _mm(xs, w, *, gain=None, res=None, tm=512, tn=None, splits=None, out_dtype=F32, name="mm"):
    t = xs[0].shape[0]
    k, n = w.shape
    assert sum(x.shape[1] for x in xs) == k
    tn = n if tn is None else tn
    splits = (tn,) if splits is None else tuple(splits)
    assert sum(splits) == tn and (len(splits) == 1 or tn == n)
    tm = min(tm, t)
    assert t % tm == 0 and n % tn == 0
    norm = gain is not None
    assert not norm or len(xs) == 1
    in_specs = [pl.BlockSpec((tm, x.shape[1]), lambda i, j: (i, 0)) for x in xs]
    args = list(xs)
    if norm:
        in_specs.append(pl.BlockSpec((1, k), lambda i, j: (0, 0)))
        args.append(gain.reshape(1, k).astype(F32))
    in_specs.append(pl.BlockSpec((k, tn), lambda i, j: (0, j)))
    args.append(w)
    if res is not None:
        in_specs.append(pl.BlockSpec((tm, tn), lambda i, j: (i, j)))
        args.append(res)
    if len(splits) == 1:
        out_shape = [jax.ShapeDtypeStruct((t, n), out_dtype)]
        out_specs = [pl.BlockSpec((tm, tn), lambda i, j: (i, j))]
    else:
        out_shape = [jax.ShapeDtypeStruct((t, s), out_dtype) for s in splits]
        out_specs = [pl.BlockSpec((tm, s), lambda i, j: (i, 0)) for s in splits]
    outs = pl.pallas_call(
        functools.partial(_mm_body, nx=len(xs), norm=norm, res=res is not None, splits=splits),
        grid=(t // tm, n // tn),
        in_specs=in_specs,
        out_specs=out_specs,
        out_shape=out_shape,
        scratch_shapes=[pltpu.VMEM((tm, k), BF16)],
        compiler_params=_cparams("parallel", "arbitrary"),
        name=name,
    )(*args)
    return outs[0] if len(outs) == 1 else outs


def _rope_tables(s, group, width):
    half = group // 2
    pos = jnp.arange(s, dtype=F32)
    inv_freq = ROPE_THETA ** (-jnp.arange(half, dtype=F32) / half)
    ang = pos[:, None] * inv_freq[None, :]
    cos, sin = jnp.cos(ang), jnp.sin(ang)
    zero = jnp.zeros_like(sin)
    reps = width // group
    c = jnp.tile(jnp.concatenate([cos, cos], -1), (1, reps))
    s1 = jnp.tile(jnp.concatenate([zero, sin], -1), (1, reps))
    s2 = jnp.tile(jnp.concatenate([-sin, zero], -1), (1, reps))
    return c, s1, s2


def _rope(x, c, s1, s2, half):
    w = x.shape[-1]
    return x * c + pltpu.roll(x, half, 1) * s1 + pltpu.roll(x, w - half, 1) * s2


def _prep_a_body(z_ref, c_ref, s1_ref, s2_ref, q_ref, k_ref, vt_ref):
    z = z_ref[0]
    qk = _rope(z[:, :512], c_ref[...], s1_ref[...], s2_ref[...], DIFF_DK // 2)
    lane = lax.broadcasted_iota(jnp.int32, (z.shape[0], HEAD_DIM), 1)
    scale = DIFF_DK ** -0.5
    for h in range(GROUP_HEADS):
        q = qk[:, h * 64:(h + 1) * 64] * scale
        q_ref[0, h, 0] = jnp.where(lane < DIFF_DK, q, 0.0).astype(BF16)
        q_ref[0, h, 1] = jnp.where(lane >= DIFF_DK, q, 0.0).astype(BF16)
        k_ref[0, h] = qk[:, 256 + h * 64:256 + (h + 1) * 64].astype(BF16)
    vt_ref[0] = z[:, 512:768].T.reshape(vt_ref.shape[1:]).astype(BF16)


def _prep_a(z_a, tabs, ts):
    b, s, _ = z_a.shape
    tspec = pl.BlockSpec((ts, 512), lambda bi, i: (i, 0))
    return pl.pallas_call(
        _prep_a_body,
        grid=(b, s // ts),
        in_specs=[pl.BlockSpec((1, ts, 768), lambda bi, i: (bi, i, 0)), tspec, tspec, tspec],
        out_specs=[pl.BlockSpec((1, 4, 2, ts, 64), lambda bi, i: (bi, 0, 0, i, 0)),
                   pl.BlockSpec((1, 4, ts, 64), lambda bi, i: (bi, 0, i, 0)),
                   pl.BlockSpec((1, 4, 64, ts), lambda bi, i: (bi, 0, 0, i))],
        out_shape=[jax.ShapeDtypeStruct((b, 4, 2, s, 64), BF16),
                   jax.ShapeDtypeStruct((b, 4, s, 64), BF16),
                   jax.ShapeDtypeStruct((b, 4, 64, s), BF16)],
        compiler_params=_cparams("parallel", "parallel"),
        name="prep_a",
    )(z_a, *tabs)


def _prep_b_body(z_ref, c_ref, s1_ref, s2_ref, qn_ref, qr_ref, ka_ref, kw_ref, kc_ref, vc_ref,
                 vs_ref, vw_ref, g_ref, *, ts):
    z = z_ref[0]
    scale = HEAD_DIM ** -0.5
    rp = _rope(z[:, :384], c_ref[...], s1_ref[...], s2_ref[...], HEAD_DIM // 2)
    for h in range(GROUP_HEADS):
        qn_ref[0, h] = (z[:, h * 64:(h + 1) * 64] * scale).astype(BF16)
        qr_ref[0, h] = (rp[:, h * 64:(h + 1) * 64] * scale).astype(BF16)
    kpos = pl.program_id(1) * ts + lax.broadcasted_iota(jnp.int32, (ts, SEL_BLOCK), 0)
    lane = lax.broadcasted_iota(jnp.int32, (ts, SEL_BLOCK), 1)
    onehot = jnp.where((kpos >> SEL_SHIFT) == lane, 1.0, 0.0)
    ka_ref[0] = jnp.concatenate([rp[:, 256:320], onehot], axis=-1).astype(BF16)
    kw_ref[0] = rp[:, 320:384].astype(BF16)
    kc_ref[0] = z[:, 384:448].astype(BF16)
    vc_ref[0] = z[:, 448:512].astype(BF16)
    vt = z[:, 512:640].T.astype(BF16)
    vs_ref[0] = vt[:HEAD_DIM]
    vw_ref[0] = vt[HEAD_DIM:]
    g_ref[0] = jax.nn.sigmoid(z[:, 640:768])


def _prep_b(z_b, tabs, ts):
    b, s, _ = z_b.shape
    tspec = pl.BlockSpec((ts, 384), lambda bi, i: (i, 0))
    h4 = pl.BlockSpec((1, 4, ts, 64), lambda bi, i: (bi, 0, i, 0))
    s64 = pl.BlockSpec((1, ts, 64), lambda bi, i: (bi, i, 0))
    s128 = pl.BlockSpec((1, ts, 128), lambda bi, i: (bi, i, 0))
    t64 = pl.BlockSpec((1, 64, ts), lambda bi, i: (bi, 0, i))
    sd = lambda *shape, dt=BF16: jax.ShapeDtypeStruct(shape, dt)
    return pl.pallas_call(
        functools.partial(_prep_b_body, ts=ts),
        grid=(b, s // ts),
        in_specs=[pl.BlockSpec((1, ts, 768), lambda bi, i: (bi, i, 0)), tspec, tspec, tspec],
        out_specs=[h4, h4, s128, s64, s64, s64, t64, t64, s128],
        out_shape=[sd(b, 4, s, 64), sd(b, 4, s, 64), sd(b, s, 128), sd(b, s, 64), sd(b, s, 64),
                   sd(b, s, 64), sd(b, 64, s), sd(b, 64, s), sd(b, s, 128, dt=F32)],
        compiler_params=_cparams("parallel", "parallel"),
        name="prep_b",
    )(z_b, *tabs)


def _prep_c_body(z_ref, c_ref, s1_ref, s2_ref, qg_ref, kg_ref, wq_ref, wkv_ref, q_ref, k_ref, vt_ref):
    z = z_ref[0]
    scale = (MLA_NOPE + MLA_ROPE) ** -0.5
    cq = _rms(z[:, :384], qg_ref[...]).astype(BF16)
    ckv = _rms(z[:, 384:512], kg_ref[...]).astype(BF16)
    qc = jnp.dot(cq, wq_ref[...], preferred_element_type=F32)
    kvc = jnp.dot(ckv, wkv_ref[...], preferred_element_type=F32)
    half = MLA_ROPE // 2
    c, s1, s2 = c_ref[...], s1_ref[...], s2_ref[...]
    q_rope = _rope(qc[:, 256:384], c, s1, s2, half)
    k_rope = _rope(z[:, 512:640], c, s1, s2, half)[:, :MLA_ROPE]
    for h in range(GROUP_HEADS):
        q = jnp.concatenate([qc[:, h * 64:(h + 1) * 64], q_rope[:, h * 32:(h + 1) * 32]], axis=-1)
        q_ref[0, h] = (q * scale).astype(BF16)
        k_ref[0, h] = jnp.concatenate([kvc[:, h * 64:(h + 1) * 64], k_rope], axis=-1).astype(BF16)
    vt_ref[0] = kvc[:, 256:512].T.reshape(vt_ref.shape[1:]).astype(BF16)


def _prep_c(z_c, tabs, q_gain, kv_gain, wq, wkv, ts):
    b, s, _ = z_c.shape
    tspec = pl.BlockSpec((ts, 128), lambda bi, i: (i, 0))
    full = lambda a: pl.BlockSpec(a.shape, lambda bi, i: (0,) * a.ndim)
    dk = MLA_NOPE + MLA_ROPE
    return pl.pallas_call(
        _prep_c_body,
        grid=(b, s // ts),
        in_specs=[pl.BlockSpec((1, ts, 640), lambda bi, i: (bi, i, 0)), tspec, tspec, tspec,
                  full(q_gain), full(kv_gain), full(wq), full(wkv)],
        out_specs=[pl.BlockSpec((1, 4, ts, dk), lambda bi, i: (bi, 0, i, 0)),
                   pl.BlockSpec((1, 4, ts, dk), lambda bi, i: (bi, 0, i, 0)),
                   pl.BlockSpec((1, 4, 64, ts), lambda bi, i: (bi, 0, 0, i))],
        out_shape=[jax.ShapeDtypeStruct((b, 4, s, dk), BF16),
                   jax.ShapeDtypeStruct((b, 4, s, dk), BF16),
                   jax.ShapeDtypeStruct((b, 4, 64, s), BF16)],
        compiler_params=_cparams("parallel", "parallel"),
        name="prep_c",
    )(z_c, *tabs, q_gain, kv_gain, wq, wkv)


def _prep_d_body(z_ref, c_ref, s1_ref, s2_ref, q_ref, k_ref, vt_ref):
    z = z_ref[0]
    scale = HEAD_DIM ** -0.5
    qk = _rope(z[:, :512], c_ref[...], s1_ref[...], s2_ref[...], HEAD_DIM // 2)
    for h in range(GROUP_HEADS):
        q_ref[0, h] = (qk[:, h * 64:(h + 1) * 64] * scale).astype(BF16)
        k_ref[0, h] = qk[:, 256 + h * 64:256 + (h + 1) * 64].astype(BF16)
    vt_ref[0] = z[:, 512:768].T.reshape(vt_ref.shape[1:]).astype(BF16)


def _prep_d(z_d, tabs, ts):
    b, s, _ = z_d.shape
    tspec = pl.BlockSpec((ts, 512), lambda bi, i: (i, 0))
    h4 = pl.BlockSpec((1, 4, ts, 64), lambda bi, i: (bi, 0, i, 0))
    return pl.pallas_call(
        _prep_d_body,
        grid=(b, s // ts),
        in_specs=[pl.BlockSpec((1, ts, 768), lambda bi, i: (bi, i, 0)), tspec, tspec, tspec],
        out_specs=[h4, h4, pl.BlockSpec((1, 4, 64, ts), lambda bi, i: (bi, 0, 0, i))],
        out_shape=[jax.ShapeDtypeStruct((b, 4, s, 64), BF16)] * 2 + [jax.ShapeDtypeStruct((b, 4, 64, s), BF16)],
        compiler_params=_cparams("parallel", "parallel"),
        name="prep_d",
    )(z_d, *tabs)


def _flash_schedule(nq, tq, tk, sk, window):
    qi, kj, fl = [], [], []
    for i in range(nq):
        q_lo, q_hi = i * tq, i * tq + tq - 1
        j_hi = min(q_hi, sk - 1) // tk
        j_lo = 0 if window is None else max(0, q_lo - window) // tk
        for j in range(j_lo, j_hi + 1):
            k_min, k_max = j * tk, j * tk + tk - 1
            full = k_max <= q_lo and (window is None or q_hi - k_min <= window)
            qi.append(i)
            kj.append(j)
            fl.append((1 if j == j_lo else 0) | (2 if j == j_hi else 0) | (0 if full else 4))
    return (np.asarray(qi, np.int32), np.asarray(kj, np.int32), np.asarray(fl, np.int32))


def _flash_body(qi_ref, kj_ref, fl_ref, q_ref, k_ref, vt_ref, o_ref, m_sc, l_sc, acc_sc, *,
                gb, hq, tq, tk, window, with_lse):
    t = pl.program_id(1)
    fl = fl_ref[t]
    rows = hq * tq
    dv = vt_ref.shape[1]

    @pl.when((fl & 1) != 0)
    def _():
        m_sc[...] = jnp.full(m_sc.shape, NEG_BIG, F32)
        l_sc[...] = jnp.zeros(l_sc.shape, F32)
        acc_sc[...] = jnp.zeros(acc_sc.shape, F32)

    def step(masked):
        q = q_ref[...].reshape(gb, rows, q_ref.shape[-1])
        st = jnp.einsum("gkd,gqd->gkq", k_ref[...], q, preferred_element_type=F32)
        if masked:
            krow = lax.broadcasted_iota(jnp.int32, (tk, rows), 0)
            qcol = lax.broadcasted_iota(jnp.int32, (tk, rows), 1)
            if hq > 1:
                qcol = qcol & (tq - 1)
            dist = (qi_ref[t] * tq - kj_ref[t] * tk) + qcol - krow
            ok = dist >= 0
            if window is not None:
                ok = ok & (dist <= window)
            st = jnp.where(ok[None], st, NEG_BIG)
        m_prev = m_sc[...]
        m_new = jnp.maximum(m_prev, jnp.max(st, axis=1, keepdims=True))
        alpha = jnp.exp(m_prev - m_new)
        p = jnp.exp(st - m_new)
        l_sc[...] = alpha * l_sc[...] + jnp.sum(p, axis=1, keepdims=True)
        acc_sc[...] = alpha * acc_sc[...] + jnp.einsum(
            "gdk,gkq->gdq", vt_ref[...], p.astype(BF16), preferred_element_type=F32)
        m_sc[...] = m_new

    @pl.when((fl & 4) != 0)
    def _():
        step(True)

    @pl.when((fl & 4) == 0)
    def _():
        step(False)

    @pl.when((fl & 2) != 0)
    def _():
        l = l_sc[...]
        o_t = acc_sc[...] / l
        if with_lse:
            extra = jnp.broadcast_to(m_sc[...] + jnp.log(l), o_t.shape)
        else:
            extra = jnp.zeros((gb, LANES - dv, rows), F32)
        dvo = o_ref.shape[-1]
        for g in range(gb):
            o = jnp.concatenate([o_t[g], extra[g]], axis=0).T
            o_ref[g] = o[:, :dvo].reshape(o_ref.shape[1:]).astype(o_ref.dtype)


def _flash(q, k, vt, *, tq, tk, gb=1, window=None, with_lse=False, name="flash"):
    g, hq, sq, dk = q.shape
    _, dv, sk = vt.shape
    tq, tk = min(tq, sq), min(tk, sk)
    assert g % gb == 0 and sq % tq == 0 and sk % tk == 0 and (gb == 1 or hq == 1)
    assert tq & (tq - 1) == 0 and 2 * dv == LANES
    qi, kj, fl = _flash_schedule(sq // tq, tq, tk, sk, window)
    dvo = 2 * dv if with_lse else dv
    rows = hq * tq
    grid_spec = pltpu.PrefetchScalarGridSpec(
        num_scalar_prefetch=3,
        grid=(g // gb, len(qi)),
        in_specs=[pl.BlockSpec((gb, hq, tq, dk), lambda gi, t, qi, kj, fl: (gi, 0, qi[t], 0)),
                  pl.BlockSpec((gb, tk, dk), lambda gi, t, qi, kj, fl: (gi, kj[t], 0)),
                  pl.BlockSpec((gb, dv, tk), lambda gi, t, qi, kj, fl: (gi, 0, kj[t]))],
        out_specs=pl.BlockSpec((gb, hq, tq, dvo), lambda gi, t, qi, kj, fl: (gi, 0, qi[t], 0)),
        scratch_shapes=[pltpu.VMEM((gb, 1, rows), F32), pltpu.VMEM((gb, 1, rows), F32),
                        pltpu.VMEM((gb, dv, rows), F32)],
    )
    return pl.pallas_call(
        functools.partial(_flash_body, gb=gb, hq=hq, tq=tq, tk=tk, window=window, with_lse=with_lse),
        grid_spec=grid_spec,
        out_shape=jax.ShapeDtypeStruct((g, hq, sq, dvo), F32),
        compiler_params=_cparams("parallel", "arbitrary"),
        name=name,
    )(jnp.asarray(qi), jnp.asarray(kj), jnp.asarray(fl), q, k, vt)


def _gelu_tanh(x):
    return 0.5 * x * (1.0 + jnp.tanh(math.sqrt(2.0 / math.pi) * (x + 0.044715 * (x * x * x))))


def _compress_body(r_ref, pos_ref, w1_ref, w2_ref, o_ref):
    half = CMP_STRIDE * HEAD_DIM
    r = r_ref[0, 0]
    w1 = w1_ref[0]
    top = jnp.dot(r, w1[:half], preferred_element_type=F32)
    bot = jnp.dot(r, w1[half:], preferred_element_type=F32)
    nc = r.shape[0]
    posb = jnp.dot(pos_ref[0], w1, preferred_element_type=F32)[:1]
    pre = top + pltpu.roll(bot, nc - 1, 0) + posb
    hid = _gelu_tanh(pre).astype(BF16)
    o_ref[0, 0] = jnp.dot(hid, w2_ref[0], preferred_element_type=F32).astype(o_ref.dtype)


def _nsa_compress(kv_rows, pos_flat, w1, w2):
    _, b, nc, _ = kv_rows.shape
    return pl.pallas_call(
        _compress_body,
        grid=(2, b),
        in_specs=[pl.BlockSpec((1, 1, nc, 1024), lambda w, bi: (w, bi, 0, 0)),
                  pl.BlockSpec((1, 8, 2048), lambda w, bi: (w, 0, 0)),
                  pl.BlockSpec((1, 2048, 256), lambda w, bi: (w, 0, 0)),
                  pl.BlockSpec((1, 256, 64), lambda w, bi: (w, 0, 0))],
        out_specs=pl.BlockSpec((1, 1, nc, 64), lambda w, bi: (w, bi, 0, 0)),
        out_shape=jax.ShapeDtypeStruct((2, b, nc, 64), BF16),
        compiler_params=_cparams("parallel", "parallel"),
        name="nsa_compress",
    )(kv_rows, pos_flat, w1, w2)


def _cmp_select_body(qn_ref, qr_ref, kc_ref, vc_ref, cov_ref, o_ref, qa_ref, *, tq, n_sel, k_top):
    ncp = kc_ref.shape[2]
    pos = pl.program_id(1) * tq + lax.broadcasted_iota(jnp.int32, (tq, ncp), 0)
    blk_end = lax.broadcasted_iota(jnp.int32, (tq, ncp), 1) * CMP_STRIDE + (CMP_LEN - 1)
    mask = blk_end <= pos
    kc, vc = kc_ref[0, 0], vc_ref[0, 0]
    p_sum = jnp.zeros((tq, ncp), F32)
    for h in range(GROUP_HEADS):
        s = lax.dot_general(qn_ref[0, h], kc, (((1,), (1,)), ((), ())), preferred_element_type=F32)
        s = jnp.where(mask, s, NEG_BIG)
        mx = jnp.max(s, axis=-1, keepdims=True)
        mx = jnp.where(mx > 0.5 * NEG_BIG, mx, 0.0)
        e = jnp.where(mask, jnp.exp(s - mx), 0.0)
        p = e / jnp.maximum(jnp.sum(e, axis=-1, keepdims=True), 1e-30)
        o_ref[0, h] = jnp.dot(p.astype(BF16), vc, preferred_element_type=F32)
        p_sum = p_sum + p
    p_hi = p_sum.astype(BF16)
    p_lo = (p_sum - p_hi.astype(F32)).astype(BF16)
    cov = cov_ref[...]
    imp = (jnp.dot(p_hi, cov, preferred_element_type=F32) + jnp.dot(p_lo, cov, preferred_element_type=F32))
    lanes = cov.shape[1]
    tpos = pl.program_id(1) * tq + lax.broadcasted_iota(jnp.int32, (tq, lanes), 0)
    cur = tpos >> SEL_SHIFT
    j = lax.broadcasted_iota(jnp.int32, (tq, lanes), 1)
    valid = j <= cur
    forced = (j == 0) | (j == cur) | (j == cur - 1)
    score = jnp.where(valid, jnp.where(forced, FORCED_SCORE, imp), -jnp.inf)
    rank = jnp.zeros((tq, lanes), jnp.int32)
    for i in range(n_sel):
        col = score[:, i:i + 1]
        ahead = (col > score) | ((col == score) & (i < j))
        rank = rank + ahead.astype(jnp.int32)
    bias = jnp.where(valid & (rank < k_top), 0.0, SEL_BIAS).astype(BF16)
    for h in range(GROUP_HEADS):
        qa_ref[0, h] = jnp.concatenate([qr_ref[0, h], bias], axis=-1)


def _cover_matrix(ncp, n_sel, lanes):
    c_start = np.arange(ncp)[:, None] * CMP_STRIDE
    s_start = np.arange(lanes)[None, :] * SEL_BLOCK
    cov = np.clip(np.minimum(c_start + CMP_LEN, s_start + SEL_BLOCK) - np.maximum(c_start, s_start), 0, None)
    cov = cov.astype(np.float32) / CMP_LEN
    cov[:, n_sel:] = 0.0
    cov[ncp - 1:, :] = 0.0
    return cov


def _cmp_select(qn, qr, kvc, tq):
    b, _, s, _ = qn.shape
    ncp = kvc.shape[2]
    n_sel = s // SEL_BLOCK
    assert n_sel <= SEL_BLOCK
    k_top = min(SEL_TOPN, n_sel)
    cov = jnp.asarray(_cover_matrix(ncp, n_sel, SEL_BLOCK), BF16)
    h4 = lambda d: pl.BlockSpec((1, 4, tq, d), lambda bi, i: (bi, 0, i, 0))
    return pl.pallas_call(
        functools.partial(_cmp_select_body, tq=tq, n_sel=n_sel, k_top=k_top),
        grid=(b, s // tq),
        in_specs=[h4(64), h4(64),
                  pl.BlockSpec((1, 1, ncp, 64), lambda bi, i: (0, bi, 0, 0)),
                  pl.BlockSpec((1, 1, ncp, 64), lambda bi, i: (1, bi, 0, 0)),
                  pl.BlockSpec(cov.shape, lambda bi, i: (0, 0))],
        out_specs=[h4(64), h4(128)],
        out_shape=[jax.ShapeDtypeStruct((b, 4, s, 64), F32), jax.ShapeDtypeStruct((b, 4, s, 128), BF16)],
        compiler_params=_cparams("parallel", "parallel"),
        name="nsa_cmp_select",
    )(qn, qr, kvc, kvc, cov)


def _merge_a_body(o_ref, lam_ref, g_ref, y_ref, *, out_scale):
    outs = []
    for h in range(GROUP_HEADS):
        o = o_ref[0, h, 0] - lam_ref[...] * o_ref[0, h, 1]
        outs.append(_rms(o, g_ref[...]) * out_scale)
    y_ref[0] = jnp.concatenate(outs, axis=-1).astype(y_ref.dtype)


def _merge_a(o, lam, subln, out_scale, ts):
    b, _, _, s, _ = o.shape
    return pl.pallas_call(
        functools.partial(_merge_a_body, out_scale=out_scale),
        grid=(b, s // ts),
        in_specs=[pl.BlockSpec((1, 4, 2, ts, 64), lambda bi, i: (bi, 0, 0, i, 0)),
                  pl.BlockSpec((1, 64), lambda bi, i: (0, 0)),
                  pl.BlockSpec((1, 64), lambda bi, i: (0, 0))],
        out_specs=pl.BlockSpec((1, ts, 256), lambda bi, i: (bi, i, 0)),
        out_shape=jax.ShapeDtypeStruct((b, s, 256), BF16),
        compiler_params=_cparams("parallel", "parallel"),
        name="merge_a",
    )(o, lam, subln)


def _merge_b_body(oc_ref, os_ref, ow_ref, g_ref, y_ref):
    g = g_ref[0]
    outs = []
    for h in range(GROUP_HEADS):
        outs.append(g[:, 3 * h:3 * h + 1] * oc_ref[0, h] + g[:, 3 * h + 1:3 * h + 2] * os_ref[0, h]
                    + g[:, 3 * h + 2:3 * h + 3] * ow_ref[0, h])
    y_ref[0] = jnp.concatenate(outs, axis=-1).astype(y_ref.dtype)


def _merge_b(o_cmp, o_sel, o_win, gates, ts):
    b, _, s, _ = o_cmp.shape
    h4 = pl.BlockSpec((1, 4, ts, 64), lambda bi, i: (bi, 0, i, 0))
    return pl.pallas_call(
        _merge_b_body,
        grid=(b, s // ts),
        in_specs=[h4, h4, h4, pl.BlockSpec((1, ts, 128), lambda bi, i: (bi, i, 0))],
        out_specs=pl.BlockSpec((1, ts, 256), lambda bi, i: (bi, i, 0)),
        out_shape=jax.ShapeDtypeStruct((b, s, 256), BF16),
        compiler_params=_cparams("parallel", "parallel"),
        name="merge_b",
    )(o_cmp, o_sel, o_win, gates)


def _merge_c_body(o_ref, y_ref):
    y_ref[0] = jnp.concatenate([o_ref[0, h] for h in range(GROUP_HEADS)], axis=-1).astype(y_ref.dtype)


def _merge_c(o, ts):
    b, _, s, _ = o.shape
    return pl.pallas_call(
        _merge_c_body,
        grid=(b, s // ts),
        in_specs=[pl.BlockSpec((1, 4, ts, 64), lambda bi, i: (bi, 0, i, 0))],
        out_specs=pl.BlockSpec((1, ts, 256), lambda bi, i: (bi, i, 0)),
        out_shape=jax.ShapeDtypeStruct((b, s, 256), BF16),
        compiler_params=_cparams("parallel", "parallel"),
        name="merge_c",
    )(o)


def _merge_d_body(o1_ref, o2_ref, o3_ref, y_ref):
    outs = []
    for h in range(GROUP_HEADS):
        e = [r[0, h] for r in (o1_ref, o2_ref, o3_ref)]
        lse = [x[:, 64:] for x in e]
        mx = jnp.maximum(jnp.maximum(lse[0], lse[1]), lse[2])
        w = [jnp.exp(x - mx) for x in lse]
        den = w[0] + w[1] + w[2]
        outs.append((w[0] * e[0][:, :64] + w[1] * e[1][:, :64] + w[2] * e[2][:, :64]) / den)
    y_ref[0] = jnp.concatenate(outs, axis=-1).astype(y_ref.dtype)


def _merge_d(o1, o2, o3, ts):
    b, _, s, _ = o1.shape
    h4 = pl.BlockSpec((1, 4, ts, 128), lambda bi, i: (bi, 0, i, 0))
    return pl.pallas_call(
        _merge_d_body,
        grid=(b, s // ts),
        in_specs=[h4, h4, h4],
        out_specs=pl.BlockSpec((1, ts, 256), lambda bi, i: (bi, i, 0)),
        out_shape=jax.ShapeDtypeStruct((b, s, 256), BF16),
        compiler_params=_cparams("parallel", "parallel"),
        name="merge_d",
    )(o1, o2, o3)


def _permute_w_in(w_in):
    a_cols = 768
    b0 = a_cols
    nq = 256
    seg = lambda i: w_in[:, b0 + nq + i * 64: b0 + nq + (i + 1) * 64]
    kc, vc, ks, vs, kw, vw = (seg(i) for i in range(6))
    gb = w_in[:, b0 + nq + 384: b0 + nq + 384 + 12]
    d = w_in.shape[0]
    w_b = jnp.concatenate([w_in[:, b0:b0 + nq], ks, kw, kc, vc, vs, vw, gb, jnp.zeros((d, 116), w_in.dtype)], axis=1)
    c0 = b0 + 652
    w_c = jnp.concatenate([w_in[:, c0:c0 + 544], jnp.zeros((d, 96), w_in.dtype)], axis=1)
    d0 = c0 + 544
    return jnp.concatenate([w_in[:, :a_cols], w_b, w_c, w_in[:, d0:d0 + 768]], axis=1).astype(BF16)


def _residue_classes(t, dil):
    g, s, d = t.shape
    return t.reshape(g, s // dil, dil, d).transpose(0, 2, 1, 3).reshape(g * dil, s // dil, d)


def _residue_classes_t(t, dil):
    g, d, s = t.shape
    return t.reshape(g, d, s // dil, dil).transpose(0, 3, 1, 2).reshape(g * dil, d, s // dil)


def _from_residue_classes(t, dil):
    gd, l, d = t.shape
    return t.reshape(gd // dil, dil, l, d).transpose(0, 2, 1, 3).reshape(gd // dil, l * dil, d)


def _token_mixers(x, layer, rope_tabs, norm_g, w_in, w_out, diff_lambda, diff_subln, cmp_pos, cmp_w1, cmp_w2,
                  q_norm, kv_norm, w_uq, w_ukv):
    b, s, d = x.shape
    t = b * s
    ts = min(512, s)
    z_a, z_b, z_c, z_d = _mm([x.reshape(t, d)], _permute_w_in(w_in), gain=norm_g, tm=min(512, t),
                             splits=(768, 768, 640, 768), name="w_in")
    z_a, z_b, z_c, z_d = (z.reshape(b, s, -1) for z in (z_a, z_b, z_c, z_d))

    qa, ka, va = _prep_a(z_a, rope_tabs["g32w512"], ts)
    oa = _flash(qa.reshape(b * 4, 2, s, 64), ka.reshape(b * 4, s, 64), va.reshape(b * 4, 64, s),
                tq=512, tk=512, name="flash_diff").reshape(b, 4, 2, s, 64)
    lam_init = 0.8 - 0.6 * math.exp(-0.3 * layer)
    lp = diff_lambda.astype(F32)
    lam = jnp.exp(jnp.sum(lp[0] * lp[1])) - jnp.exp(jnp.sum(lp[2] * lp[3])) + lam_init
    y_a = _merge_a(oa, jnp.full((1, 64), lam, F32), diff_subln.reshape(1, 64).astype(F32), 1.0 - lam_init, ts)

    qn, qr, k_aug, kw, kc, vc, vs, vw, gates = _prep_b(z_b, rope_tabs["g64w384"], ts)
    kv_rows = jnp.stack([kc, vc]).reshape(2, b, s // CMP_STRIDE, CMP_STRIDE * HEAD_DIM)
    pos_flat = jnp.broadcast_to(cmp_pos.reshape(2, 1, CMP_LEN * HEAD_DIM), (2, 8, CMP_LEN * HEAD_DIM)).astype(BF16)
    kvc = _nsa_compress(kv_rows, pos_flat, cmp_w1.astype(BF16), cmp_w2.astype(BF16))
    o_cmp, q_aug = _cmp_select(qn, qr, kvc, min(256, s))
    o_sel = _flash(q_aug, k_aug, vs, tq=256, tk=512, name="flash_sel")
    o_win = _flash(qr, kw, vw, tq=256, tk=256, window=NSA_WINDOW - 1, name="flash_win")
    y_b = _merge_b(o_cmp, o_sel, o_win, gates, ts)

    wq = w_uq.reshape(-1, 4, MLA_NOPE + MLA_ROPE)
    wq = jnp.concatenate([wq[:, :, :MLA_NOPE].reshape(-1, 256), wq[:, :, MLA_NOPE:].reshape(-1, 128)], axis=1)
    wkv = w_ukv.reshape(-1, 4, MLA_NOPE + MLA_DV)
    wkv = jnp.concatenate([wkv[:, :, :MLA_NOPE].reshape(-1, 256), wkv[:, :, MLA_NOPE:].reshape(-1, 256)], axis=1)
    qc, kcat, vcv = _prep_c(z_c, rope_tabs["g32w128"], q_norm.reshape(1, -1).astype(F32),
                            kv_norm.reshape(1, -1).astype(F32), wq.astype(BF16), wkv.astype(BF16), ts)
    dk = MLA_NOPE + MLA_ROPE
    oc = _flash(qc.reshape(b * 4, 1, s, dk), kcat.reshape(b * 4, s, dk), vcv.reshape(b * 4, 64, s),
                tq=512, tk=512, name="flash_mla").reshape(b, 4, s, 64)
    y_c = _merge_c(oc, ts)

    qd, kd, vd = _prep_d(z_d, rope_tabs["g64w512"], ts)
    qd, kd, vd = qd.reshape(b * 4, s, 64), kd.reshape(b * 4, s, 64), vd.reshape(b * 4, 64, s)
    o_d = []
    for window, dil in DILATED_PATTERNS:
        l = s // dil
        qq, kk, vv = _residue_classes(qd, dil), _residue_classes(kd, dil), _residue_classes_t(vd, dil)
        g = qq.shape[0]
        gb = math.gcd(g, 16)
        o = _flash(qq.reshape(g, 1, l, 64), kk, vv, tq=128, tk=128, gb=gb, window=window // dil,
                   with_lse=True, name=f"flash_dil{dil}")
        o_d.append(_from_residue_classes(o.reshape(g, l, 128), dil).reshape(b, 4, s, 128))
    y_d = _merge_d(*o_d, ts)

    ys = [y.reshape(t, 256) for y in (y_a, y_b, y_c, y_d)]
    return _mm(ys, w_out.astype(BF16), res=x.reshape(t, d), tm=min(512, t), name="w_out").reshape(b, s, d)


def _cross_body(x_ref, g_ref, wq_ref, k_ref, v_ref, wo_ref, o_ref):
    x = x_ref[0]
    h = _rms(x, g_ref[...]).astype(BF16)
    q = jnp.dot(h, wq_ref[...], preferred_element_type=F32).astype(BF16)
    dh = q.shape[-1] // CROSS_HEADS
    outs = []
    for hd in range(CROSS_HEADS):
        sl = slice(hd * dh, (hd + 1) * dh)
        s = lax.dot_general(q[:, sl], k_ref[0, :, sl], (((1,), (1,)), ((), ())), preferred_element_type=F32)
        e = jnp.exp(s - jnp.max(s, axis=-1, keepdims=True))
        p = e / jnp.sum(e, axis=-1, keepdims=True)
        outs.append(jnp.dot(p.astype(BF16), v_ref[0, :, sl], preferred_element_type=F32))
    o = jnp.concatenate(outs, axis=-1).astype(BF16)
    o_ref[0] = x + jnp.dot(o, wo_ref[...], preferred_element_type=F32)


def _cross_attention(x, mem, g_x, g_mem, wq, wk, wv, wo):
    b, s, d = x.shape
    m = mem.shape[1]
    ts = min(512, s)
    dh = d // CROSS_HEADS
    k, v = _mm([mem.reshape(b * m, d)], jnp.concatenate([wk, wv], axis=1).astype(BF16), gain=g_mem,
               tm=min(512, b * m), splits=(d, d), out_dtype=BF16, name="cross_kv")
    full = lambda a: pl.BlockSpec(a.shape, lambda bi, i: (0,) * a.ndim)
    wq_s = (wq * dh ** -0.5).astype(BF16)
    wo_b = wo.astype(BF16)
    g2 = g_x.reshape(1, d).astype(F32)
    return pl.pallas_call(
        _cross_body,
        grid=(b, s // ts),
        in_specs=[pl.BlockSpec((1, ts, d), lambda bi, i: (bi, i, 0)), full(g2), full(wq_s),
                  pl.BlockSpec((1, m, d), lambda bi, i: (bi, 0, 0)),
                  pl.BlockSpec((1, m, d), lambda bi, i: (bi, 0, 0)), full(wo_b)],
        out_specs=pl.BlockSpec((1, ts, d), lambda bi, i: (bi, i, 0)),
        out_shape=jax.ShapeDtypeStruct((b, s, d), F32),
        compiler_params=_cparams("parallel", "parallel"),
        name="cross_attn",
    )(x, g2, wq_s, k.reshape(b, m, d), v.reshape(b, m, d), wo_b)


def _swiglu_body(x_ref, g_ref, wg_ref, wu_ref, wd_ref, o_ref, h_sc, acc_sc):
    f = pl.program_id(1)

    @pl.when(f == 0)
    def _():
        h_sc[...] = _rms(x_ref[...], g_ref[...]).astype(BF16)
        acc_sc[...] = x_ref[...]

    h = h_sc[...]
    gate = jnp.dot(h, wg_ref[...], preferred_element_type=F32)
    up = jnp.dot(h, wu_ref[...], preferred_element_type=F32)
    act = (gate * jax.nn.sigmoid(gate) * up).astype(BF16)
    acc_sc[...] += jnp.dot(act, wd_ref[...], preferred_element_type=F32)

    @pl.when(f == pl.num_programs(1) - 1)
    def _():
        o_ref[...] = acc_sc[...]


def _swiglu(x, gain, wg, wu, wd, tm=512, tf=1408):
    t, d = x.shape
    ff = wg.shape[1]
    tm, tf = min(tm, t), min(tf, ff)
    assert t % tm == 0 and ff % tf == 0
    return pl.pallas_call(
        _swiglu_body,
        grid=(t // tm, ff // tf),
        in_specs=[pl.BlockSpec((tm, d), lambda i, f: (i, 0)),
                  pl.BlockSpec((1, d), lambda i, f: (0, 0)),
                  pl.BlockSpec((d, tf), lambda i, f: (0, f)),
                  pl.BlockSpec((d, tf), lambda i, f: (0, f)),
                  pl.BlockSpec((tf, d), lambda i, f: (f, 0))],
        out_specs=pl.BlockSpec((tm, d), lambda i, f: (i, 0)),
        out_shape=jax.ShapeDtypeStruct((t, d), F32),
        scratch_shapes=[pltpu.VMEM((tm, d), BF16), pltpu.VMEM((tm, d), F32)],
        compiler_params=_cparams("parallel", "arbitrary"),
        name="swiglu",
    )(x, gain.reshape(1, d).astype(F32), wg.astype(BF16), wu.astype(BF16), wd.astype(BF16))


def _router_body(x_ref, g_ref, r_ref, h_ref, lg_ref):
    h = _rms(x_ref[...], g_ref[...])
    h_hi = h.astype(BF16)
    h_ref[...] = h_hi
    h_lo = (h - h_hi.astype(F32)).astype(BF16)
    r = r_ref[...]
    r_hi = r.astype(BF16)
    r_lo = (r - r_hi.astype(F32)).astype(BF16)
    lg_ref[...] = (jnp.dot(h_hi, r_hi, preferred_element_type=F32) + jnp.dot(h_lo, r_hi, preferred_element_type=F32)
                   + jnp.dot(h_hi, r_lo, preferred_element_type=F32))


def _router(x, gain, router, tm=512):
    t, d = x.shape
    tm = min(tm, t)
    r_pad = jnp.zeros((d, LANES), F32).at[:, :N_EXPERTS].set(router.astype(F32))
    return pl.pallas_call(
        _router_body,
        grid=(t // tm,),
        in_specs=[pl.BlockSpec((tm, d), lambda i: (i, 0)), pl.BlockSpec((1, d), lambda i: (0, 0)),
                  pl.BlockSpec((d, LANES), lambda i: (0, 0))],
        out_specs=[pl.BlockSpec((tm, d), lambda i: (i, 0)), pl.BlockSpec((tm, LANES), lambda i: (i, 0))],
        out_shape=[jax.ShapeDtypeStruct((t, d), BF16), jax.ShapeDtypeStruct((t, LANES), F32)],
        compiler_params=_cparams("parallel"),
        name="moe_router",
    )(x, gain.reshape(1, d).astype(F32), r_pad)


def _experts_body(be_ref, nb_ref, x_ref, wg_ref, wu_ref, wd_ref, o_ref, acc_sc):
    i, f = pl.program_id(0), pl.program_id(1)

    @pl.when(i < nb_ref[0])
    def _():
        @pl.when(f == 0)
        def _():
            acc_sc[...] = jnp.zeros(acc_sc.shape, F32)

        x = x_ref[...]
        gate = jnp.dot(x, wg_ref[0], preferred_element_type=F32)
        up = jnp.dot(x, wu_ref[0], preferred_element_type=F32)
        act = (gate * jax.nn.sigmoid(gate) * up).astype(BF16)
        acc_sc[...] += jnp.dot(act, wd_ref[0], preferred_element_type=F32)

        @pl.when(f == pl.num_programs(1) - 1)
        def _():
            o_ref[...] = acc_sc[...]


def _experts(xs, blk_e, n_used, wg, wu, wd, rows, tf=1792):
    n_rows, d = xs.shape
    ff = wg.shape[2]
    n_blk = n_rows // rows
    grid_spec = pltpu.PrefetchScalarGridSpec(
        num_scalar_prefetch=2,
        grid=(n_blk, ff // tf),
        in_specs=[pl.BlockSpec((rows, d), lambda i, f, be, nb: (jnp.minimum(i, nb[0] - 1), 0)),
                  pl.BlockSpec((1, d, tf), lambda i, f, be, nb: (be[i], 0, jnp.where(i < nb[0], f, ff // tf - 1))),
                  pl.BlockSpec((1, d, tf), lambda i, f, be, nb: (be[i], 0, jnp.where(i < nb[0], f, ff // tf - 1))),
                  pl.BlockSpec((1, tf, d), lambda i, f, be, nb: (be[i], jnp.where(i < nb[0], f, ff // tf - 1), 0))],
        out_specs=pl.BlockSpec((rows, d), lambda i, f, be, nb: (i, 0)),
        scratch_shapes=[pltpu.VMEM((rows, d), F32)],
    )
    return pl.pallas_call(
        _experts_body,
        grid_spec=grid_spec,
        out_shape=jax.ShapeDtypeStruct((n_rows, d), F32),
        compiler_params=_cparams("arbitrary", "arbitrary"),
        name="moe_experts",
    )(blk_e, n_used, xs, wg, wu, wd)


MOE_ROWS = 512


def _moe(x, gain, router, w_gate, w_up, w_down):
    t, d = x.shape
    h, logits = _router(x, gain, router)
    top_logit, top_e = lax.top_k(logits[:, :N_EXPERTS], TOP_K)
    gate = jax.nn.softmax(top_logit, axis=-1)
    n_assign = t * TOP_K
    flat_e = top_e.reshape(-1)
    onehot = (flat_e[:, None] == jnp.arange(N_EXPERTS)[None, :]).astype(jnp.int32)
    before = jnp.cumsum(onehot, axis=0) - onehot
    counts = jnp.sum(onehot, axis=0)
    padded = (counts + MOE_ROWS - 1) // MOE_ROWS * MOE_ROWS
    pend = jnp.cumsum(padded)
    pstart = pend - padded
    dest = pstart[flat_e] + jnp.sum(before * onehot, axis=1)
    n_rows = -(-n_assign // MOE_ROWS) * MOE_ROWS + N_EXPERTS * MOE_ROWS
    n_blk = n_rows // MOE_ROWS
    row_tok = jnp.zeros((n_rows,), jnp.int32).at[dest].set(jnp.arange(n_assign, dtype=jnp.int32) // TOP_K)
    blk_e = jnp.minimum(jnp.searchsorted(pend, jnp.arange(n_blk) * MOE_ROWS, side="right"), N_EXPERTS - 1)
    n_used = (pend[-1] // MOE_ROWS).astype(jnp.int32).reshape(1)
    xs = h[row_tok]
    ys = _experts(xs, blk_e.astype(jnp.int32), n_used, w_gate.astype(BF16), w_up.astype(BF16),
                  w_down.astype(BF16), MOE_ROWS)
    contrib = ys[dest].reshape(t, TOP_K, d) * gate[:, :, None]
    return x + contrib[:, 0] + contrib[:, 1]


def _final_norm_body(x_ref, g_ref, o_ref):
    o_ref[...] = _rms(x_ref[...], g_ref[...])


def _final_norm(x, gain, tm=1024):
    t, d = x.shape
    tm = min(tm, t)
    return pl.pallas_call(
        _final_norm_body,
        grid=(t // tm,),
        in_specs=[pl.BlockSpec((tm, d), lambda i: (i, 0)), pl.BlockSpec((1, d), lambda i: (0, 0))],
        out_specs=pl.BlockSpec((tm, d), lambda i: (i, 0)),
        out_shape=jax.ShapeDtypeStruct((t, d), F32),
        compiler_params=_cparams("parallel"),
        name="final_norm",
    )(x, gain.reshape(1, d).astype(F32))


def _all_rope_tables(s):
    return {"g32w512": _rope_tables(s, 32, 512), "g64w384": _rope_tables(s, 64, 384),
            "g32w128": _rope_tables(s, 32, 128), "g64w512": _rope_tables(s, 64, 512)}


def kernel(x, mem, norm_mix, w_in, w_out, diff_lambda, diff_subln, nsa_cmp_pos, nsa_cmp_w1, nsa_cmp_w2, mla_q_norm, mla_kv_norm, mla_w_uq, mla_w_ukv, norm_cross, norm_mem, cross_wq, cross_wk, cross_wv, cross_wo, norm_ffn, ffn_w_gate, ffn_w_up, ffn_w_down, moe_router, moe_w_gate, moe_w_up, moe_w_down, final_norm):
    b, s, d = x.shape
    depth = w_in.shape[0]
    tabs = _all_rope_tables(s)
    for l in range(depth):
        x = _token_mixers(x, l, tabs, norm_mix[l], w_in[l], w_out[l], diff_lambda[l], diff_subln[l],
                          nsa_cmp_pos[l], nsa_cmp_w1[l], nsa_cmp_w2[l], mla_q_norm[l], mla_kv_norm[l],
                          mla_w_uq[l], mla_w_ukv[l])
        x = _cross_attention(x, mem, norm_cross[l], norm_mem[l], cross_wq[l], cross_wk[l], cross_wv[l],
                             cross_wo[l])
        xt = x.reshape(b * s, d)
        i = l // 2
        if l % 2 == 0:
            xt = _swiglu(xt, norm_ffn[l], ffn_w_gate[i], ffn_w_up[i], ffn_w_down[i])
        else:
            xt = _moe(xt, norm_ffn[l], moe_router[i], moe_w_gate[i], moe_w_up[i], moe_w_down[i])
        x = xt.reshape(b, s, d)
    return _final_norm(x.reshape(b * s, d), final_norm).reshape(b, s, d)
```

```python
import functools
import math

import numpy as np
import jax
import jax.numpy as jnp
from jax import lax
from jax.experimental import pallas as pl
from jax.experimental.pallas import tpu as pltpu

F32 = jnp.float32
BF16 = jnp.bfloat16

HEAD_DIM = 64
GROUP_HEADS = 4
GROUP_WIDTH = HEAD_DIM * GROUP_HEADS
ROPE_THETA = 10000.0
NORM_EPS = 1e-6
DIFF_DK = HEAD_DIM // 2
CMP_LEN = 32
CMP_STRIDE = 16
SEL_BLOCK = 64
SEL_SHIFT = 6
SEL_TOPN = 16
NSA_WINDOW = 512
FORCED_SCORE = 1e6
MLA_NOPE = 64
MLA_ROPE = 32
MLA_DV = 64
DILATED_PATTERNS = ((128, 1), (512, 4), (2048, 16))
CROSS_HEADS = 4
N_EXPERTS = 8
TOP_K = 2

LANES = 128
V7X_VMEM_BYTES = 64 * 1024 * 1024
VMEM_LIMIT = V7X_VMEM_BYTES * 7 // 8

ONES_ROWS = 16
NEG_BIG = -1e30
SEL_BIAS = -1e9
LOG2E = math.log2(math.e)
LN2 = math.log(2.0)


def _cparams(*sem):
    return pltpu.CompilerParams(dimension_semantics=sem, vmem_limit_bytes=VMEM_LIMIT)


def _rms(xf, gain):
    ms = jnp.mean(xf * xf, axis=-1, keepdims=True)
    return xf * lax.rsqrt(ms + NORM_EPS) * gain


def _mm_body(*refs, nx, norm, res, splits):
    x_refs = refs[:nx]
    pos = nx
    g_ref = refs[pos] if norm else None
    pos += int(norm)
    w_ref = refs[pos]
    pos += 1
    r_ref = refs[pos] if res else None
    pos += int(res)
    o_refs = refs[pos:pos + len(splits)]
    xs_ref = refs[pos + len(splits)]

    @pl.when(pl.program_id(1) == 0)
    def _():
        off = 0
        for xr in x_refs:
            xv = xr[...]
            if norm:
                xv = _rms(xv.astype(F32), g_ref[...])
            kw = xv.shape[-1]
            xs_ref[:, off:off + kw] = xv.astype(BF16)
            off += kw

    acc = jnp.dot(xs_ref[...], w_ref[...], preferred_element_type=F32)
    if res:
        acc = acc + r_ref[...]
    off = 0
    for o_ref, width in zip(o_refs, splits):
        o_ref[...] = acc[:, off:off + width].astype(o_ref.dtype)
        off += width


def _mm(xs, w, *, gain=None, res=None, tm=512, tn=None, splits=None, out_dtype=F32, name="mm"):
    t = xs[0].shape[0]
    k, n = w.shape
    assert sum(x.shape[1] for x in xs) == k
    tn = n if tn is None else tn
    splits = (tn,) if splits is None else tuple(splits)
    assert sum(splits) == tn and (len(splits) == 1 or tn == n)
    tm = min(tm, t)
    assert t % tm == 0 and n % tn == 0
    norm = gain is not None
    assert not norm or len(xs) == 1
    in_specs = [pl.BlockSpec((tm, x.shape[1]), lambda i, j: (i, 0)) for x in xs]
    args = list(xs)
    if norm:
        in_specs.append(pl.BlockSpec((1, k), lambda i, j: (0, 0)))
        args.append(gain.reshape(1, k).astype(F32))
    in_specs.append(pl.BlockSpec((k, tn), lambda i, j: (0, j)))
    args.append(w)
    if res is not None:
        in_specs.append(pl.BlockSpec((tm, tn), lambda i, j: (i, j)))
        args.append(res)
    if len(splits) == 1:
        out_shape = [jax.ShapeDtypeStruct((t, n), out_dtype)]
        out_specs = [pl.BlockSpec((tm, tn), lambda i, j: (i, j))]
    else:
        out_shape = [jax.ShapeDtypeStruct((t, s), out_dtype) for s in splits]
        out_specs = [pl.BlockSpec((tm, s), lambda i, j: (i, 0)) for s in splits]
    outs = pl.pallas_call(
        functools.partial(_mm_body, nx=len(xs), norm=norm, res=res is not None, splits=splits),
        grid=(t // tm, n // tn),
        in_specs=in_specs,
        out_specs=out_specs,
        out_shape=out_shape,
        scratch_shapes=[pltpu.VMEM((tm, k), BF16)],
        compiler_params=_cparams("parallel", "arbitrary"),
        name=name,
    )(*args)
    return outs[0] if len(outs) == 1 else outs


def _rope_tables(s, group, width):
    half = group // 2
    pos = jnp.arange(s, dtype=F32)
    inv_freq = ROPE_THETA ** (-jnp.arange(half, dtype=F32) / half)
    ang = pos[:, None] * inv_freq[None, :]
    cos, sin = jnp.cos(ang), jnp.sin(ang)
    zero = jnp.zeros_like(sin)
    reps = width // group
    c = jnp.tile(jnp.concatenate([cos, cos], -1), (1, reps))
    s1 = jnp.tile(jnp.concatenate([zero, sin], -1), (1, reps))
    s2 = jnp.tile(jnp.concatenate([-sin, zero], -1), (1, reps))
    return c, s1, s2


def _rope(x, c, s1, s2, half):
    w = x.shape[-1]
    return x * c + pltpu.roll(x, half, 1) * s1 + pltpu.roll(x, w - half, 1) * s2


def _prep_a_body(z_ref, c_ref, s1_ref, s2_ref, q_ref, k_ref, vt_ref):
    z = z_ref[0]
    qk = _rope(z[:, :512], c_ref[...], s1_ref[...], s2_ref[...], DIFF_DK // 2)
    lane = lax.broadcasted_iota(jnp.int32, (z.shape[0], HEAD_DIM), 1)
    scale = DIFF_DK ** -0.5 * LOG2E
    for h in range(GROUP_HEADS):
        q = qk[:, h * 64:(h + 1) * 64] * scale
        q_ref[0, h, 0] = jnp.where(lane < DIFF_DK, q, 0.0).astype(BF16)
        q_ref[0, h, 1] = jnp.where(lane >= DIFF_DK, q, 0.0).astype(BF16)
        k_ref[0, h] = qk[:, 256 + h * 64:256 + (h + 1) * 64].astype(BF16)
    vt_ref[0] = z[:, 512:768].T.reshape(vt_ref.shape[1:]).astype(BF16)


def _prep_a(z_a, tabs, ts):
    b, s, _ = z_a.shape
    tspec = pl.BlockSpec((ts, 512), lambda bi, i: (i, 0))
    return pl.pallas_call(
        _prep_a_body,
        grid=(b, s // ts),
        in_specs=[pl.BlockSpec((1, ts, 768), lambda bi, i: (bi, i, 0)), tspec, tspec, tspec],
        out_specs=[pl.BlockSpec((1, 4, 2, ts, 64), lambda bi, i: (bi, 0, 0, i, 0)),
                   pl.BlockSpec((1, 4, ts, 64), lambda bi, i: (bi, 0, i, 0)),
                   pl.BlockSpec((1, 4, 64, ts), lambda bi, i: (bi, 0, 0, i))],
        out_shape=[jax.ShapeDtypeStruct((b, 4, 2, s, 64), BF16),
                   jax.ShapeDtypeStruct((b, 4, s, 64), BF16),
                   jax.ShapeDtypeStruct((b, 4, 64, s), BF16)],
        compiler_params=_cparams("parallel", "parallel"),
        name="prep_a",
    )(z_a, *tabs)


def _prep_b_body(z_ref, c_ref, s1_ref, s2_ref, qn_ref, qr_ref, ka_ref, kw_ref, kc_ref, vc_ref,
                 vs_ref, vw_ref, g_ref, *, ts):
    z = z_ref[0]
    scale = HEAD_DIM ** -0.5 * LOG2E
    rp = _rope(z[:, :384], c_ref[...], s1_ref[...], s2_ref[...], HEAD_DIM // 2)
    for h in range(GROUP_HEADS):
        qn_ref[0, h] = (z[:, h * 64:(h + 1) * 64] * scale).astype(BF16)
        qr_ref[0, h] = (rp[:, h * 64:(h + 1) * 64] * scale).astype(BF16)
    kpos = pl.program_id(1) * ts + lax.broadcasted_iota(jnp.int32, (ts, SEL_BLOCK), 0)
    lane = lax.broadcasted_iota(jnp.int32, (ts, SEL_BLOCK), 1)
    onehot = jnp.where((kpos >> SEL_SHIFT) == lane, 1.0, 0.0)
    ka_ref[0] = jnp.concatenate([rp[:, 256:320], onehot], axis=-1).astype(BF16)
    kw_ref[0] = rp[:, 320:384].astype(BF16)
    kc_ref[0] = z[:, 384:448].astype(BF16)
    vc_ref[0] = z[:, 448:512].astype(BF16)
    vt = z[:, 512:640].T.astype(BF16)
    vs_ref[0] = vt[:HEAD_DIM]
    vw_ref[0] = vt[HEAD_DIM:]
    g_ref[0] = jax.nn.sigmoid(z[:, 640:768])


def _prep_b(z_b, tabs, ts):
    b, s, _ = z_b.shape
    tspec = pl.BlockSpec((ts, 384), lambda bi, i: (i, 0))
    h4 = pl.BlockSpec((1, 4, ts, 64), lambda bi, i: (bi, 0, i, 0))
    s64 = pl.BlockSpec((1, ts, 64), lambda bi, i: (bi, i, 0))
    s128 = pl.BlockSpec((1, ts, 128), lambda bi, i: (bi, i, 0))
    t64 = pl.BlockSpec((1, 64, ts), lambda bi, i: (bi, 0, i))
    sd = lambda *shape, dt=BF16: jax.ShapeDtypeStruct(shape, dt)
    return pl.pallas_call(
        functools.partial(_prep_b_body, ts=ts),
        grid=(b, s // ts),
        in_specs=[pl.BlockSpec((1, ts, 768), lambda bi, i: (bi, i, 0)), tspec, tspec, tspec],
        out_specs=[h4, h4, s128, s64, s64, s64, t64, t64, s128],
        out_shape=[sd(b, 4, s, 64), sd(b, 4, s, 64), sd(b, s, 128), sd(b, s, 64), sd(b, s, 64),
                   sd(b, s, 64), sd(b, 64, s), sd(b, 64, s), sd(b, s, 128, dt=F32)],
        compiler_params=_cparams("parallel", "parallel"),
        name="prep_b",
    )(z_b, *tabs)


def _prep_c_body(z_ref, c_ref, s1_ref, s2_ref, qg_ref, kg_ref, wq_ref, wkv_ref, q_ref, k_ref, vt_ref):
    z = z_ref[0]
    scale = (MLA_NOPE + MLA_ROPE) ** -0.5 * LOG2E
    cq = _rms(z[:, :384], qg_ref[...]).astype(BF16)
    ckv = _rms(z[:, 384:512], kg_ref[...]).astype(BF16)
    qc = jnp.dot(cq, wq_ref[...], preferred_element_type=F32)
    kvc = jnp.dot(ckv, wkv_ref[...], preferred_element_type=F32)
    half = MLA_ROPE // 2
    c, s1, s2 = c_ref[...], s1_ref[...], s2_ref[...]
    q_rope = _rope(qc[:, 256:384], c, s1, s2, half)
    k_rope = _rope(z[:, 512:640], c, s1, s2, half)[:, :MLA_ROPE]
    for h in range(GROUP_HEADS):
        q = jnp.concatenate([qc[:, h * 64:(h + 1) * 64], q_rope[:, h * 32:(h + 1) * 32]], axis=-1)
        q_ref[0, h] = (q * scale).astype(BF16)
        k_ref[0, h] = jnp.concatenate([kvc[:, h * 64:(h + 1) * 64], k_rope], axis=-1).astype(BF16)
    vt_ref[0] = kvc[:, 256:512].T.reshape(vt_ref.shape[1:]).astype(BF16)


def _prep_c(z_c, tabs, q_gain, kv_gain, wq, wkv, ts):
    b, s, _ = z_c.shape
    tspec = pl.BlockSpec((ts, 128), lambda bi, i: (i, 0))
    full = lambda a: pl.BlockSpec(a.shape, lambda bi, i: (0,) * a.ndim)
    dk = MLA_NOPE + MLA_ROPE
    return pl.pallas_call(
        _prep_c_body,
        grid=(b, s // ts),
        in_specs=[pl.BlockSpec((1, ts, 640), lambda bi, i: (bi, i, 0)), tspec, tspec, tspec,
                  full(q_gain), full(kv_gain), full(wq), full(wkv)],
        out_specs=[pl.BlockSpec((1, 4, ts, dk), lambda bi, i: (bi, 0, i, 0)),
                   pl.BlockSpec((1, 4, ts, dk), lambda bi, i: (bi, 0, i, 0)),
                   pl.BlockSpec((1, 4, 64, ts), lambda bi, i: (bi, 0, 0, i))],
        out_shape=[jax.ShapeDtypeStruct((b, 4, s, dk), BF16),
                   jax.ShapeDtypeStruct((b, 4, s, dk), BF16),
                   jax.ShapeDtypeStruct((b, 4, 64, s), BF16)],
        compiler_params=_cparams("parallel", "parallel"),
        name="prep_c",
    )(z_c, *tabs, q_gain, kv_gain, wq, wkv)


def _prep_d_body(z_ref, c_ref, s1_ref, s2_ref, q_ref, k_ref, vt_ref):
    z = z_ref[0]
    scale = HEAD_DIM ** -0.5 * LOG2E
    qk = _rope(z[:, :512], c_ref[...], s1_ref[...], s2_ref[...], HEAD_DIM // 2)
    for h in range(GROUP_HEADS):
        q_ref[0, h] = (qk[:, h * 64:(h + 1) * 64] * scale).astype(BF16)
        k_ref[0, h] = qk[:, 256 + h * 64:256 + (h + 1) * 64].astype(BF16)
    vt_ref[0] = z[:, 512:768].T.reshape(vt_ref.shape[1:]).astype(BF16)


def _prep_d(z_d, tabs, ts):
    b, s, _ = z_d.shape
    tspec = pl.BlockSpec((ts, 512), lambda bi, i: (i, 0))
    h4 = pl.BlockSpec((1, 4, ts, 64), lambda bi, i: (bi, 0, i, 0))
    return pl.pallas_call(
        _prep_d_body,
        grid=(b, s // ts),
        in_specs=[pl.BlockSpec((1, ts, 768), lambda bi, i: (bi, i, 0)), tspec, tspec, tspec],
        out_specs=[h4, h4, pl.BlockSpec((1, 4, 64, ts), lambda bi, i: (bi, 0, 0, i))],
        out_shape=[jax.ShapeDtypeStruct((b, 4, s, 64), BF16)] * 2 + [jax.ShapeDtypeStruct((b, 4, 64, s), BF16)],
        compiler_params=_cparams("parallel", "parallel"),
        name="prep_d",
    )(z_d, *tabs)


def _flash_schedule(nq, tq, tk, sk, window):
    qi, kj, fl = [], [], []
    for i in range(nq):
        q_lo, q_hi = i * tq, i * tq + tq - 1
        j_hi = min(q_hi, sk - 1) // tk
        j_lo = 0 if window is None else max(0, q_lo - window) // tk
        for j in range(j_lo, j_hi + 1):
            k_min, k_max = j * tk, j * tk + tk - 1
            full = k_max <= q_lo and (window is None or q_hi - k_min <= window)
            qi.append(i)
            kj.append(j)
            fl.append((1 if j == j_lo else 0) | (2 if j == j_hi else 0) | (0 if full else 4))
    return (np.asarray(qi, np.int32), np.asarray(kj, np.int32), np.asarray(fl, np.int32))


def _flash_body(qi_ref, kj_ref, fl_ref, q_ref, k_ref, vt_ref, o_ref, m_sc, acc_sc, *,
                gb, hq, tq, tk, window, with_lse):
    t = pl.program_id(1)
    fl = fl_ref[t]
    rows = hq * tq
    dv = vt_ref.shape[1]

    @pl.when((fl & 1) != 0)
    def _():
        m_sc[...] = jnp.full(m_sc.shape, NEG_BIG, F32)
        acc_sc[...] = jnp.zeros(acc_sc.shape, F32)

    def step(masked):
        q = q_ref[...].reshape(gb, rows, q_ref.shape[-1])
        st = jnp.einsum("gkd,gqd->gkq", k_ref[...], q, preferred_element_type=F32)
        if masked:
            krow = lax.broadcasted_iota(jnp.int32, (tk, rows), 0)
            qcol = lax.broadcasted_iota(jnp.int32, (tk, rows), 1)
            if hq > 1:
                qcol = qcol & (tq - 1)
            dist = (qi_ref[t] * tq - kj_ref[t] * tk) + qcol - krow
            ok = dist >= 0
            if window is not None:
                ok = ok & (dist <= window)
            st = jnp.where(ok[None], st, NEG_BIG)
        m_prev = m_sc[...]
        m_new = jnp.maximum(m_prev, jnp.max(st, axis=1, keepdims=True))
        alpha = jnp.exp2(m_prev - m_new)
        p = jnp.exp2((st - m_new).astype(BF16))
        v_ext = jnp.concatenate([vt_ref[...], jnp.ones((gb, ONES_ROWS, tk), BF16)], axis=1)
        acc_sc[...] = alpha * acc_sc[...] + jnp.einsum("gdk,gkq->gdq", v_ext, p, preferred_element_type=F32)
        m_sc[...] = m_new

    @pl.when((fl & 4) != 0)
    def _():
        step(True)

    @pl.when((fl & 4) == 0)
    def _():
        step(False)

    @pl.when((fl & 2) != 0)
    def _():
        l = acc_sc[:, dv:dv + 1, :]
        o_t = acc_sc[:, :dv, :] / l
        if with_lse:
            extra = jnp.broadcast_to(m_sc[...] * LN2 + jnp.log(l), o_t.shape)
        else:
            extra = jnp.zeros((gb, LANES - dv, rows), F32)
        dvo = o_ref.shape[-1]
        for g in range(gb):
            o = jnp.concatenate([o_t[g], extra[g]], axis=0).T
            o_ref[g] = o[:, :dvo].reshape(o_ref.shape[1:]).astype(o_ref.dtype)


def _flash(q, k, vt, *, tq, tk, gb=1, window=None, with_lse=False, name="flash"):
    g, hq, sq, dk = q.shape
    _, dv, sk = vt.shape
    tq, tk = min(tq, sq), min(tk, sk)
    assert g % gb == 0 and sq % tq == 0 and sk % tk == 0
    assert tq & (tq - 1) == 0 and 2 * dv == LANES
    qi, kj, fl = _flash_schedule(sq // tq, tq, tk, sk, window)
    dvo = 2 * dv if with_lse else dv
    rows = hq * tq
    grid_spec = pltpu.PrefetchScalarGridSpec(
        num_scalar_prefetch=3,
        grid=(g // gb, len(qi)),
        in_specs=[pl.BlockSpec((gb, hq, tq, dk), lambda gi, t, qi, kj, fl: (gi, 0, qi[t], 0)),
                  pl.BlockSpec((gb, tk, dk), lambda gi, t, qi, kj, fl: (gi, kj[t], 0)),
                  pl.BlockSpec((gb, dv, tk), lambda gi, t, qi, kj, fl: (gi, 0, kj[t]))],
        out_specs=pl.BlockSpec((gb, hq, tq, dvo), lambda gi, t, qi, kj, fl: (gi, 0, qi[t], 0)),
        scratch_shapes=[pltpu.VMEM((gb, 1, rows), F32), pltpu.VMEM((gb, dv + ONES_ROWS, rows), F32)],
    )
    return pl.pallas_call(
        functools.partial(_flash_body, gb=gb, hq=hq, tq=tq, tk=tk, window=window, with_lse=with_lse),
        grid_spec=grid_spec,
        out_shape=jax.ShapeDtypeStruct((g, hq, sq, dvo), F32),
        compiler_params=_cparams("parallel", "arbitrary"),
        name=name,
    )(jnp.asarray(qi), jnp.asarray(kj), jnp.asarray(fl), q, k, vt)


def _gelu_tanh(x):
    return 0.5 * x * (1.0 + jnp.tanh(math.sqrt(2.0 / math.pi) * (x + 0.044715 * (x * x * x))))


def _compress_body(r_ref, pos_ref, w1_ref, w2_ref, o_ref):
    half = CMP_STRIDE * HEAD_DIM
    r = r_ref[0, 0]
    w1 = w1_ref[0]
    top = jnp.dot(r, w1[:half], preferred_element_type=F32)
    bot = jnp.dot(r, w1[half:], preferred_element_type=F32)
    nc = r.shape[0]
    posb = jnp.dot(pos_ref[0], w1, preferred_element_type=F32)[:1]
    pre = top + pltpu.roll(bot, nc - 1, 0) + posb
    hid = _gelu_tanh(pre).astype(BF16)
    o_ref[0, 0] = jnp.dot(hid, w2_ref[0], preferred_element_type=F32).astype(o_ref.dtype)


def _nsa_compress(kv_rows, pos_flat, w1, w2):
    _, b, nc, _ = kv_rows.shape
    return pl.pallas_call(
        _compress_body,
        grid=(2, b),
        in_specs=[pl.BlockSpec((1, 1, nc, 1024), lambda w, bi: (w, bi, 0, 0)),
                  pl.BlockSpec((1, 8, 2048), lambda w, bi: (w, 0, 0)),
                  pl.BlockSpec((1, 2048, 256), lambda w, bi: (w, 0, 0)),
                  pl.BlockSpec((1, 256, 64), lambda w, bi: (w, 0, 0))],
        out_specs=pl.BlockSpec((1, 1, nc, 64), lambda w, bi: (w, bi, 0, 0)),
        out_shape=jax.ShapeDtypeStruct((2, b, nc, 64), BF16),
        compiler_params=_cparams("parallel", "parallel"),
        name="nsa_compress",
    )(kv_rows, pos_flat, w1, w2)


def _cmp_select_body(qn_ref, qr_ref, kc_ref, vc_ref, cov_ref, o_ref, qa_ref, *, tq, n_sel, k_top):
    ncp = kc_ref.shape[2]
    pos = pl.program_id(1) * tq + lax.broadcasted_iota(jnp.int32, (tq, ncp), 0)
    blk_end = lax.broadcasted_iota(jnp.int32, (tq, ncp), 1) * CMP_STRIDE + (CMP_LEN - 1)
    mask = blk_end <= pos
    kc, vc = kc_ref[0, 0], vc_ref[0, 0]
    p_sum = jnp.zeros((tq, ncp), F32)
    for h in range(GROUP_HEADS):
        s = lax.dot_general(qn_ref[0, h], kc, (((1,), (1,)), ((), ())), preferred_element_type=F32)
        s = jnp.where(mask, s, NEG_BIG)
        mx = jnp.max(s, axis=-1, keepdims=True)
        mx = jnp.where(mx > 0.5 * NEG_BIG, mx, 0.0)
        e = jnp.where(mask, jnp.exp2(s - mx), 0.0)
        p = e / jnp.maximum(jnp.sum(e, axis=-1, keepdims=True), 1e-30)
        o_ref[0, h] = jnp.dot(p.astype(BF16), vc, preferred_element_type=F32)
        p_sum = p_sum + p
    p_hi = p_sum.astype(BF16)
    p_lo = (p_sum - p_hi.astype(F32)).astype(BF16)
    cov = cov_ref[...]
    imp = (jnp.dot(p_hi, cov, preferred_element_type=F32) + jnp.dot(p_lo, cov, preferred_element_type=F32))
    lanes = cov.shape[1]
    tpos = pl.program_id(1) * tq + lax.broadcasted_iota(jnp.int32, (tq, lanes), 0)
    cur = tpos >> SEL_SHIFT
    j = lax.broadcasted_iota(jnp.int32, (tq, lanes), 1)
    valid = j <= cur
    forced = (j == 0) | (j == cur) | (j == cur - 1)
    score = jnp.where(valid, jnp.where(forced, FORCED_SCORE, imp), -jnp.inf)
    rank = jnp.zeros((tq, lanes), jnp.int32)
    for i in range(n_sel):
        col = score[:, i:i + 1]
        ahead = (col > score) | ((col == score) & (i < j))
        rank = rank + ahead.astype(jnp.int32)
    bias = jnp.where(valid & (rank < k_top), 0.0, SEL_BIAS).astype(BF16)
    for h in range(GROUP_HEADS):
        qa_ref[0, h] = jnp.concatenate([qr_ref[0, h], bias], axis=-1)


def _cover_matrix(ncp, n_sel, lanes):
    c_start = np.arange(ncp)[:, None] * CMP_STRIDE
    s_start = np.arange(lanes)[None, :] * SEL_BLOCK
    cov = np.clip(np.minimum(c_start + CMP_LEN, s_start + SEL_BLOCK) - np.maximum(c_start, s_start), 0, None)
    cov = cov.astype(np.float32) / CMP_LEN
    cov[:, n_sel:] = 0.0
    cov[ncp - 1:, :] = 0.0
    return cov


def _cmp_select(qn, qr, kvc, tq):
    b, _, s, _ = qn.shape
    ncp = kvc.shape[2]
    n_sel = s // SEL_BLOCK
    assert n_sel <= SEL_BLOCK
    k_top = min(SEL_TOPN, n_sel)
    cov = jnp.asarray(_cover_matrix(ncp, n_sel, SEL_BLOCK), BF16)
    h4 = lambda d: pl.BlockSpec((1, 4, tq, d), lambda bi, i: (bi, 0, i, 0))
    return pl.pallas_call(
        functools.partial(_cmp_select_body, tq=tq, n_sel=n_sel, k_top=k_top),
        grid=(b, s // tq),
        in_specs=[h4(64), h4(64),
                  pl.BlockSpec((1, 1, ncp, 64), lambda bi, i: (0, bi, 0, 0)),
                  pl.BlockSpec((1, 1, ncp, 64), lambda bi, i: (1, bi, 0, 0)),
                  pl.BlockSpec(cov.shape, lambda bi, i: (0, 0))],
        out_specs=[h4(64), h4(128)],
        out_shape=[jax.ShapeDtypeStruct((b, 4, s, 64), F32), jax.ShapeDtypeStruct((b, 4, s, 128), BF16)],
        compiler_params=_cparams("parallel", "parallel"),
        name="nsa_cmp_select",
    )(qn, qr, kvc, kvc, cov)


def _merge_a_body(o_ref, lam_ref, g_ref, y_ref, *, out_scale):
    outs = []
    for h in range(GROUP_HEADS):
        o = o_ref[0, h, 0] - lam_ref[...] * o_ref[0, h, 1]
        outs.append(_rms(o, g_ref[...]) * out_scale)
    y_ref[0] = jnp.concatenate(outs, axis=-1).astype(y_ref.dtype)


def _merge_a(o, lam, subln, out_scale, ts):
    b, _, _, s, _ = o.shape
    return pl.pallas_call(
        functools.partial(_merge_a_body, out_scale=out_scale),
        grid=(b, s // ts),
        in_specs=[pl.BlockSpec((1, 4, 2, ts, 64), lambda bi, i: (bi, 0, 0, i, 0)),
                  pl.BlockSpec((1, 64), lambda bi, i: (0, 0)),
                  pl.BlockSpec((1, 64), lambda bi, i: (0, 0))],
        out_specs=pl.BlockSpec((1, ts, 256), lambda bi, i: (bi, i, 0)),
        out_shape=jax.ShapeDtypeStruct((b, s, 256), BF16),
        compiler_params=_cparams("parallel", "parallel"),
        name="merge_a",
    )(o, lam, subln)


def _merge_b_body(oc_ref, os_ref, ow_ref, g_ref, y_ref):
    g = g_ref[0]
    outs = []
    for h in range(GROUP_HEADS):
        outs.append(g[:, 3 * h:3 * h + 1] * oc_ref[0, h] + g[:, 3 * h + 1:3 * h + 2] * os_ref[0, h]
                    + g[:, 3 * h + 2:3 * h + 3] * ow_ref[0, h])
    y_ref[0] = jnp.concatenate(outs, axis=-1).astype(y_ref.dtype)


def _merge_b(o_cmp, o_sel, o_win, gates, ts):
    b, _, s, _ = o_cmp.shape
    h4 = pl.BlockSpec((1, 4, ts, 64), lambda bi, i: (bi, 0, i, 0))
    return pl.pallas_call(
        _merge_b_body,
        grid=(b, s // ts),
        in_specs=[h4, h4, h4, pl.BlockSpec((1, ts, 128), lambda bi, i: (bi, i, 0))],
        out_specs=pl.BlockSpec((1, ts, 256), lambda bi, i: (bi, i, 0)),
        out_shape=jax.ShapeDtypeStruct((b, s, 256), BF16),
        compiler_params=_cparams("parallel", "parallel"),
        name="merge_b",
    )(o_cmp, o_sel, o_win, gates)


def _merge_c_body(o_ref, y_ref):
    y_ref[0] = jnp.concatenate([o_ref[0, h] for h in range(GROUP_HEADS)], axis=-1).astype(y_ref.dtype)


def _merge_c(o, ts):
    b, _, s, _ = o.shape
    return pl.pallas_call(
        _merge_c_body,
        grid=(b, s // ts),
        in_specs=[pl.BlockSpec((1, 4, ts, 64), lambda bi, i: (bi, 0, i, 0))],
        out_specs=pl.BlockSpec((1, ts, 256), lambda bi, i: (bi, i, 0)),
        out_shape=jax.ShapeDtypeStruct((b, s, 256), BF16),
        compiler_params=_cparams("parallel", "parallel"),
        name="merge_c",
    )(o)


def _merge_d_body(o1_ref, o2_ref, o3_ref, y_ref):
    outs = []
    for h in range(GROUP_HEADS):
        e = [r[0, h] for r in (o1_ref, o2_ref, o3_ref)]
        lse = [x[:, 64:] for x in e]
        mx = jnp.maximum(jnp.maximum(lse[0], lse[1]), lse[2])
        w = [jnp.exp(x - mx) for x in lse]
        den = w[0] + w[1] + w[2]
        outs.append((w[0] * e[0][:, :64] + w[1] * e[1][:, :64] + w[2] * e[2][:, :64]) / den)
    y_ref[0] = jnp.concatenate(outs, axis=-1).astype(y_ref.dtype)


def _merge_d(o1, o2, o3, ts):
    b, _, s, _ = o1.shape
    h4 = pl.BlockSpec((1, 4, ts, 128), lambda bi, i: (bi, 0, i, 0))
    return pl.pallas_call(
        _merge_d_body,
        grid=(b, s // ts),
        in_specs=[h4, h4, h4],
        out_specs=pl.BlockSpec((1, ts, 256), lambda bi, i: (bi, i, 0)),
        out_shape=jax.ShapeDtypeStruct((b, s, 256), BF16),
        compiler_params=_cparams("parallel", "parallel"),
        name="merge_d",
    )(o1, o2, o3)


def _permute_w_in(w_in):
    a_cols = 768
    b0 = a_cols
    nq = 256
    seg = lambda i: w_in[:, b0 + nq + i * 64: b0 + nq + (i + 1) * 64]
    kc, vc, ks, vs, kw, vw = (seg(i) for i in range(6))
    gb = w_in[:, b0 + nq + 384: b0 + nq + 384 + 12]
    d = w_in.shape[0]
    w_b = jnp.concatenate([w_in[:, b0:b0 + nq], ks, kw, kc, vc, vs, vw, gb, jnp.zeros((d, 116), w_in.dtype)], axis=1)
    c0 = b0 + 652
    w_c = jnp.concatenate([w_in[:, c0:c0 + 544], jnp.zeros((d, 96), w_in.dtype)], axis=1)
    d0 = c0 + 544
    return jnp.concatenate([w_in[:, :a_cols], w_b, w_c, w_in[:, d0:d0 + 768]], axis=1).astype(BF16)


def _residue_classes(t, dil):
    g, s, d = t.shape
    return t.reshape(g, s // dil, dil, d).transpose(0, 2, 1, 3).reshape(g * dil, s // dil, d)


def _residue_classes_t(t, dil):
    g, d, s = t.shape
    return t.reshape(g, d, s // dil, dil).transpose(0, 3, 1, 2).reshape(g * dil, d, s // dil)


def _from_residue_classes(t, dil):
    gd, l, d = t.shape
    return t.reshape(gd // dil, dil, l, d).transpose(0, 2, 1, 3).reshape(gd // dil, l * dil, d)


def _token_mixers(x, layer, rope_tabs, norm_g, w_in, w_out, diff_lambda, diff_subln, cmp_pos, cmp_w1, cmp_w2,
                  q_norm, kv_norm, w_uq, w_ukv):
    b, s, d = x.shape
    t = b * s
    ts = min(512, s)
    z_a, z_b, z_c, z_d = _mm([x.reshape(t, d)], _permute_w_in(w_in), gain=norm_g, tm=min(512, t),
                             splits=(768, 768, 640, 768), name="w_in")
    z_a, z_b, z_c, z_d = (z.reshape(b, s, -1) for z in (z_a, z_b, z_c, z_d))

    qa, ka, va = _prep_a(z_a, rope_tabs["g32w512"], ts)
    oa = _flash(qa.reshape(b * 4, 2, s, 64), ka.reshape(b * 4, s, 64), va.reshape(b * 4, 64, s),
                tq=512, tk=512, gb=2, name="flash_diff").reshape(b, 4, 2, s, 64)
    lam_init = 0.8 - 0.6 * math.exp(-0.3 * layer)
    lp = diff_lambda.astype(F32)
    lam = jnp.exp(jnp.sum(lp[0] * lp[1])) - jnp.exp(jnp.sum(lp[2] * lp[3])) + lam_init
    y_a = _merge_a(oa, jnp.full((1, 64), lam, F32), diff_subln.reshape(1, 64).astype(F32), 1.0 - lam_init, ts)

    qn, qr, k_aug, kw, kc, vc, vs, vw, gates = _prep_b(z_b, rope_tabs["g64w384"], ts)
    kv_rows = jnp.stack([kc, vc]).reshape(2, b, s // CMP_STRIDE, CMP_STRIDE * HEAD_DIM)
    pos_flat = jnp.broadcast_to(cmp_pos.reshape(2, 1, CMP_LEN * HEAD_DIM), (2, 8, CMP_LEN * HEAD_DIM)).astype(BF16)
    kvc = _nsa_compress(kv_rows, pos_flat, cmp_w1.astype(BF16), cmp_w2.astype(BF16))
    o_cmp, q_aug = _cmp_select(qn, qr, kvc, min(256, s))
    o_sel = _flash(q_aug, k_aug, vs, tq=256, tk=512, name="flash_sel")
    o_win = _flash(qr, kw, vw, tq=256, tk=256, window=NSA_WINDOW - 1, name="flash_win")
    y_b = _merge_b(o_cmp, o_sel, o_win, gates, ts)

    wq = w_uq.reshape(-1, 4, MLA_NOPE + MLA_ROPE)
    wq = jnp.concatenate([wq[:, :, :MLA_NOPE].reshape(-1, 256), wq[:, :, MLA_NOPE:].reshape(-1, 128)], axis=1)
    wkv = w_ukv.reshape(-1, 4, MLA_NOPE + MLA_DV)
    wkv = jnp.concatenate([wkv[:, :, :MLA_NOPE].reshape(-1, 256), wkv[:, :, MLA_NOPE:].reshape(-1, 256)], axis=1)
    qc, kcat, vcv = _prep_c(z_c, rope_tabs["g32w128"], q_norm.reshape(1, -1).astype(F32),
                            kv_norm.reshape(1, -1).astype(F32), wq.astype(BF16), wkv.astype(BF16), ts)
    dk = MLA_NOPE + MLA_ROPE
    oc = _flash(qc.reshape(b * 4, 1, s, dk), kcat.reshape(b * 4, s, dk), vcv.reshape(b * 4, 64, s),
                tq=512, tk=512, gb=4, name="flash_mla").reshape(b, 4, s, 64)
    y_c = _merge_c(oc, ts)

    qd, kd, vd = _prep_d(z_d, rope_tabs["g64w512"], ts)
    qd, kd, vd = qd.reshape(b * 4, s, 64), kd.reshape(b * 4, s, 64), vd.reshape(b * 4, 64, s)
    o_d = []
    for window, dil in DILATED_PATTERNS:
        l = s // dil
        qq, kk, vv = _residue_classes(qd, dil), _residue_classes(kd, dil), _residue_classes_t(vd, dil)
        g = qq.shape[0]
        gb = math.gcd(g, 16)
        o = _flash(qq.reshape(g, 1, l, 64), kk, vv, tq=128, tk=128, gb=gb, window=window // dil,
                   with_lse=True, name=f"flash_dil{dil}")
        o_d.append(_from_residue_classes(o.reshape(g, l, 128), dil).reshape(b, 4, s, 128))
    y_d = _merge_d(*o_d, ts)

    ys = [y.reshape(t, 256) for y in (y_a, y_b, y_c, y_d)]
    return _mm(ys, w_out.astype(BF16), res=x.reshape(t, d), tm=min(512, t), name="w_out").reshape(b, s, d)


def _cross_body(x_ref, g_ref, wq_ref, k_ref, v_ref, wo_ref, o_ref):
    x = x_ref[0]
    h = _rms(x, g_ref[...]).astype(BF16)
    q = jnp.dot(h, wq_ref[...], preferred_element_type=F32).astype(BF16)
    dh = q.shape[-1] // CROSS_HEADS
    outs = []
    for hd in range(CROSS_HEADS):
        sl = slice(hd * dh, (hd + 1) * dh)
        s = lax.dot_general(q[:, sl], k_ref[0, :, sl], (((1,), (1,)), ((), ())), preferred_element_type=F32)
        e = jnp.exp(s - jnp.max(s, axis=-1, keepdims=True))
        p = e / jnp.sum(e, axis=-1, keepdims=True)
        outs.append(jnp.dot(p.astype(BF16), v_ref[0, :, sl], preferred_element_type=F32))
    o = jnp.concatenate(outs, axis=-1).astype(BF16)
    o_ref[0] = x + jnp.dot(o, wo_ref[...], preferred_element_type=F32)


def _cross_attention(x, mem, g_x, g_mem, wq, wk, wv, wo):
    b, s, d = x.shape
    m = mem.shape[1]
    ts = min(512, s)
    dh = d // CROSS_HEADS
    k, v = _mm([mem.reshape(b * m, d)], jnp.concatenate([wk, wv], axis=1).astype(BF16), gain=g_mem,
               tm=min(512, b * m), splits=(d, d), out_dtype=BF16, name="cross_kv")
    full = lambda a: pl.BlockSpec(a.shape, lambda bi, i: (0,) * a.ndim)
    wq_s = (wq * dh ** -0.5).astype(BF16)
    wo_b = wo.astype(BF16)
    g2 = g_x.reshape(1, d).astype(F32)
    return pl.pallas_call(
        _cross_body,
        grid=(b, s // ts),
        in_specs=[pl.BlockSpec((1, ts, d), lambda bi, i: (bi, i, 0)), full(g2), full(wq_s),
                  pl.BlockSpec((1, m, d), lambda bi, i: (bi, 0, 0)),
                  pl.BlockSpec((1, m, d), lambda bi, i: (bi, 0, 0)), full(wo_b)],
        out_specs=pl.BlockSpec((1, ts, d), lambda bi, i: (bi, i, 0)),
        out_shape=jax.ShapeDtypeStruct((b, s, d), F32),
        compiler_params=_cparams("parallel", "parallel"),
        name="cross_attn",
    )(x, g2, wq_s, k.reshape(b, m, d), v.reshape(b, m, d), wo_b)


def _swiglu_body(x_ref, g_ref, wg_ref, wu_ref, wd_ref, o_ref, h_sc, acc_sc):
    f = pl.program_id(1)

    @pl.when(f == 0)
    def _():
        h_sc[...] = _rms(x_ref[...], g_ref[...]).astype(BF16)
        acc_sc[...] = x_ref[...]

    h = h_sc[...]
    gate = jnp.dot(h, wg_ref[...], preferred_element_type=F32)
    up = jnp.dot(h, wu_ref[...], preferred_element_type=F32)
    act = (gate * jax.nn.sigmoid(gate) * up).astype(BF16)
    acc_sc[...] += jnp.dot(act, wd_ref[...], preferred_element_type=F32)

    @pl.when(f == pl.num_programs(1) - 1)
    def _():
        o_ref[...] = acc_sc[...]


def _swiglu(x, gain, wg, wu, wd, tm=512, tf=1408):
    t, d = x.shape
    ff = wg.shape[1]
    tm, tf = min(tm, t), min(tf, ff)
    assert t % tm == 0 and ff % tf == 0
    return pl.pallas_call(
        _swiglu_body,
        grid=(t // tm, ff // tf),
        in_specs=[pl.BlockSpec((tm, d), lambda i, f: (i, 0)),
                  pl.BlockSpec((1, d), lambda i, f: (0, 0)),
                  pl.BlockSpec((d, tf), lambda i, f: (0, f)),
                  pl.BlockSpec((d, tf), lambda i, f: (0, f)),
                  pl.BlockSpec((tf, d), lambda i, f: (f, 0))],
        out_specs=pl.BlockSpec((tm, d), lambda i, f: (i, 0)),
        out_shape=jax.ShapeDtypeStruct((t, d), F32),
        scratch_shapes=[pltpu.VMEM((tm, d), BF16), pltpu.VMEM((tm, d), F32)],
        compiler_params=_cparams("parallel", "arbitrary"),
        name="swiglu",
    )(x, gain.reshape(1, d).astype(F32), wg.astype(BF16), wu.astype(BF16), wd.astype(BF16))


def _router_body(x_ref, g_ref, r_ref, h_ref, lg_ref):
    h = _rms(x_ref[...], g_ref[...])
    h_hi = h.astype(BF16)
    h_ref[...] = h_hi
    h_lo = (h - h_hi.astype(F32)).astype(BF16)
    r = r_ref[...]
    r_hi = r.astype(BF16)
    r_lo = (r - r_hi.astype(F32)).astype(BF16)
    lg_ref[...] = (jnp.dot(h_hi, r_hi, preferred_element_type=F32) + jnp.dot(h_lo, r_hi, preferred_element_type=F32)
                   + jnp.dot(h_hi, r_lo, preferred_element_type=F32))


def _router(x, gain, router, tm=512):
    t, d = x.shape
    tm = min(tm, t)
    r_pad = jnp.zeros((d, LANES), F32).at[:, :N_EXPERTS].set(router.astype(F32))
    return pl.pallas_call(
        _router_body,
        grid=(t // tm,),
        in_specs=[pl.BlockSpec((tm, d), lambda i: (i, 0)), pl.BlockSpec((1, d), lambda i: (0, 0)),
                  pl.BlockSpec((d, LANES), lambda i: (0, 0))],
        out_specs=[pl.BlockSpec((tm, d), lambda i: (i, 0)), pl.BlockSpec((tm, LANES), lambda i: (i, 0))],
        out_shape=[jax.ShapeDtypeStruct((t, d), BF16), jax.ShapeDtypeStruct((t, LANES), F32)],
        compiler_params=_cparams("parallel"),
        name="moe_router",
    )(x, gain.reshape(1, d).astype(F32), r_pad)


def _experts_body(be_ref, nb_ref, x_ref, wg_ref, wu_ref, wd_ref, o_ref, acc_sc):
    i, f = pl.program_id(0), pl.program_id(1)

    @pl.when(i < nb_ref[0])
    def _():
        @pl.when(f == 0)
        def _():
            acc_sc[...] = jnp.zeros(acc_sc.shape, F32)

        x = x_ref[...]
        gate = jnp.dot(x, wg_ref[0], preferred_element_type=F32)
        up = jnp.dot(x, wu_ref[0], preferred_element_type=F32)
        act = (gate * jax.nn.sigmoid(gate) * up).astype(BF16)
        acc_sc[...] += jnp.dot(act, wd_ref[0], preferred_element_type=F32)

        @pl.when(f == pl.num_programs(1) - 1)
        def _():
            o_ref[...] = acc_sc[...]

    @pl.when(i >= nb_ref[0])
    def _():
        o_ref[...] = jnp.zeros(o_ref.shape, F32)


def _experts(xs, blk_e, n_used, wg, wu, wd, rows, tf=1792):
    n_rows, d = xs.shape
    ff = wg.shape[2]
    n_blk = n_rows // rows
    grid_spec = pltpu.PrefetchScalarGridSpec(
        num_scalar_prefetch=2,
        grid=(n_blk, ff // tf),
        in_specs=[pl.BlockSpec((rows, d), lambda i, f, be, nb: (jnp.minimum(i, nb[0] - 1), 0)),
                  pl.BlockSpec((1, d, tf), lambda i, f, be, nb: (be[i], 0, jnp.where(i < nb[0], f, ff // tf - 1))),
                  pl.BlockSpec((1, d, tf), lambda i, f, be, nb: (be[i], 0, jnp.where(i < nb[0], f, ff // tf - 1))),
                  pl.BlockSpec((1, tf, d), lambda i, f, be, nb: (be[i], jnp.where(i < nb[0], f, ff // tf - 1), 0))],
        out_specs=pl.BlockSpec((rows, d), lambda i, f, be, nb: (i, 0)),
        scratch_shapes=[pltpu.VMEM((rows, d), F32)],
    )
    return pl.pallas_call(
        _experts_body,
        grid_spec=grid_spec,
        out_shape=jax.ShapeDtypeStruct((n_rows, d), F32),
        compiler_params=_cparams("arbitrary", "arbitrary"),
        name="moe_experts",
    )(blk_e, n_used, xs, wg, wu, wd)


MOE_ROWS = 512


def _moe(x, gain, router, w_gate, w_up, w_down):
    t, d = x.shape
    h, logits = _router(x, gain, router)
    top_logit, top_e = lax.top_k(logits[:, :N_EXPERTS], TOP_K)
    gate = jax.nn.softmax(top_logit, axis=-1)
    n_assign = t * TOP_K
    flat_e = top_e.reshape(-1)
    onehot = (flat_e[:, None] == jnp.arange(N_EXPERTS)[None, :]).astype(jnp.int32)
    before = jnp.cumsum(onehot, axis=0) - onehot
    counts = jnp.sum(onehot, axis=0)
    padded = (counts + MOE_ROWS - 1) // MOE_ROWS * MOE_ROWS
    pend = jnp.cumsum(padded)
    pstart = pend - padded
    dest = pstart[flat_e] + jnp.sum(before * onehot, axis=1)
    n_rows = -(-n_assign // MOE_ROWS) * MOE_ROWS + N_EXPERTS * MOE_ROWS
    n_blk = n_rows // MOE_ROWS
    row_tok = jnp.zeros((n_rows,), jnp.int32).at[dest].set(jnp.arange(n_assign, dtype=jnp.int32) // TOP_K)
    blk_e = jnp.minimum(jnp.searchsorted(pend, jnp.arange(n_blk) * MOE_ROWS, side="right"), N_EXPERTS - 1)
    n_used = (pend[-1] // MOE_ROWS).astype(jnp.int32).reshape(1)
    xs = h[row_tok]
    ys = _experts(xs, blk_e.astype(jnp.int32), n_used, w_gate.astype(BF16), w_up.astype(BF16),
                  w_down.astype(BF16), MOE_ROWS)
    contrib = ys[dest].reshape(t, TOP_K, d) * gate[:, :, None]
    return x + contrib[:, 0] + contrib[:, 1]


def _final_norm_body(x_ref, g_ref, o_ref):
    o_ref[...] = _rms(x_ref[...], g_ref[...])


def _final_norm(x, gain, tm=1024):
    t, d = x.shape
    tm = min(tm, t)
    return pl.pallas_call(
        _final_norm_body,
        grid=(t // tm,),
        in_specs=[pl.BlockSpec((tm, d), lambda i: (i, 0)), pl.BlockSpec((1, d), lambda i: (0, 0))],
        out_specs=pl.BlockSpec((tm, d), lambda i: (i, 0)),
        out_shape=jax.ShapeDtypeStruct((t, d), F32),
        compiler_params=_cparams("parallel"),
        name="final_norm",
    )(x, gain.reshape(1, d).astype(F32))


def _all_rope_tables(s):
    return {"g32w512": _rope_tables(s, 32, 512), "g64w384": _rope_tables(s, 64, 384),
            "g32w128": _rope_tables(s, 32, 128), "g64w512": _rope_tables(s, 64, 512)}


def kernel(x, mem, norm_mix, w_in, w_out, diff_lambda, diff_subln, nsa_cmp_pos, nsa_cmp_w1, nsa_cmp_w2, mla_q_norm, mla_kv_norm, mla_w_uq, mla_w_ukv, norm_cross, norm_mem, cross_wq, cross_wk, cross_wv, cross_wo, norm_ffn, ffn_w_gate, ffn_w_up, ffn_w_down, moe_router, moe_w_gate, moe_w_up, moe_w_down, final_norm):
    b, s, d = x.shape
    depth = w_in.shape[0]
    tabs = _all_rope_tables(s)
    for l in range(depth):
        x = _token_mixers(x, l, tabs, norm_mix[l], w_in[l], w_out[l], diff_lambda[l], diff_subln[l],
                          nsa_cmp_pos[l], nsa_cmp_w1[l], nsa_cmp_w2[l], mla_q_norm[l], mla_kv_norm[l],
                          mla_w_uq[l], mla_w_ukv[l])
        x = _cross_attention(x, mem, norm_cross[l], norm_mem[l], cross_wq[l], cross_wk[l], cross_wv[l],
                             cross_wo[l])
        xt = x.reshape(b * s, d)
        i = l // 2
        if l % 2 == 0:
            xt = _swiglu(xt, norm_ffn[l], ffn_w_gate[i], ffn_w_up[i], ffn_w_down[i])
        else:
            xt = _moe(xt, norm_ffn[l], moe_router[i], moe_w_gate[i], moe_w_up[i], moe_w_down[i])
        x = xt.reshape(b, s, d)
    return _final_norm(x.reshape(b * s, d), final_norm).reshape(b, s, d)
```

```python
import functools
import math

import numpy as np
import jax
import jax.numpy as jnp
from jax import lax
from jax.experimental import pallas as pl
from jax.experimental.pallas import tpu as pltpu

F32 = jnp.float32
BF16 = jnp.bfloat16

HEAD_DIM = 64
GROUP_HEADS = 4
GROUP_WIDTH = HEAD_DIM * GROUP_HEADS
ROPE_THETA = 10000.0
NORM_EPS = 1e-6
DIFF_DK = HEAD_DIM // 2
CMP_LEN = 32
CMP_STRIDE = 16
SEL_BLOCK = 64
SEL_SHIFT = 6
SEL_TOPN = 16
NSA_WINDOW = 512
FORCED_SCORE = 1e6
MLA_NOPE = 64
MLA_ROPE = 32
MLA_DV = 64
DILATED_PATTERNS = ((128, 1), (512, 4), (2048, 16))
CROSS_HEADS = 4
N_EXPERTS = 8
TOP_K = 2

LANES = 128
V7X_VMEM_BYTES = 64 * 1024 * 1024
VMEM_LIMIT = V7X_VMEM_BYTES * 7 // 8

ONES_ROWS = 16
NEG_BIG = -1e30
SEL_BIAS = -1e9
LOG2E = math.log2(math.e)
LN2 = math.log(2.0)


def _cparams(*sem):
    return pltpu.CompilerParams(dimension_semantics=sem, vmem_limit_bytes=VMEM_LIMIT)


def _rms(xf, gain):
    ms = jnp.mean(xf * xf, axis=-1, keepdims=True)
    return xf * lax.rsqrt(ms + NORM_EPS) * gain


def _mm_body(*refs, nx, norm, res, splits):
    x_refs = refs[:nx]
    pos = nx
    g_ref = refs[pos] if norm else None
    pos += int(norm)
    w_ref = refs[pos]
    pos += 1
    r_ref = refs[pos] if res else None
    pos += int(res)
    o_refs = refs[pos:pos + len(splits)]
    xs_ref = refs[pos + len(splits)]

    @pl.when(pl.program_id(1) == 0)
    def _():
        off = 0
        for xr in x_refs:
            xv = xr[...]
            if norm:
                xv = _rms(xv.astype(F32), g_ref[...])
            kw = xv.shape[-1]
            xs_ref[:, off:off + kw] = xv.astype(BF16)
            off += kw

    acc = jnp.dot(xs_ref[...], w_ref[...], preferred_element_type=F32)
    if res:
        acc = acc + r_ref[...]
    off = 0
    for o_ref, width in zip(o_refs, splits):
        o_ref[...] = acc[:, off:off + width].astype(o_ref.dtype)
        off += width


def _mm(xs, w, *, gain=None, res=None, tm=512, tn=None, splits=None, out_dtype=F32, name="mm"):
    t = xs[0].shape[0]
    k, n = w.shape
    assert sum(x.shape[1] for x in xs) == k
    tn = n if tn is None else tn
    splits = (tn,) if splits is None else tuple(splits)
    assert sum(splits) == tn and (len(splits) == 1 or tn == n)
    tm = min(tm, t)
    assert t % tm == 0 and n % tn == 0
    norm = gain is not None
    assert not norm or len(xs) == 1
    in_specs = [pl.BlockSpec((tm, x.shape[1]), lambda i, j: (i, 0)) for x in xs]
    args = list(xs)
    if norm:
        in_specs.append(pl.BlockSpec((1, k), lambda i, j: (0, 0)))
        args.append(gain.reshape(1, k).astype(F32))
    in_specs.append(pl.BlockSpec((k, tn), lambda i, j: (0, j)))
    args.append(w)
    if res is not None:
        in_specs.append(pl.BlockSpec((tm, tn), lambda i, j: (i, j)))
        args.append(res)
    if len(splits) == 1:
        out_shape = [jax.ShapeDtypeStruct((t, n), out_dtype)]
        out_specs = [pl.BlockSpec((tm, tn), lambda i, j: (i, j))]
    else:
        out_shape = [jax.ShapeDtypeStruct((t, s), out_dtype) for s in splits]
        out_specs = [pl.BlockSpec((tm, s), lambda i, j: (i, 0)) for s in splits]
    outs = pl.pallas_call(
        functools.partial(_mm_body, nx=len(xs), norm=norm, res=res is not None, splits=splits),
        grid=(t // tm, n // tn),
        in_specs=in_specs,
        out_specs=out_specs,
        out_shape=out_shape,
        scratch_shapes=[pltpu.VMEM((tm, k), BF16)],
        compiler_params=_cparams("parallel", "arbitrary"),
        name=name,
    )(*args)
    return outs[0] if len(outs) == 1 else outs


def _rope_tables(s, group, width):
    half = group // 2
    pos = jnp.arange(s, dtype=F32)
    inv_freq = ROPE_THETA ** (-jnp.arange(half, dtype=F32) / half)
    ang = pos[:, None] * inv_freq[None, :]
    cos, sin = jnp.cos(ang), jnp.sin(ang)
    zero = jnp.zeros_like(sin)
    reps = width // group
    c = jnp.tile(jnp.concatenate([cos, cos], -1), (1, reps))
    s1 = jnp.tile(jnp.concatenate([zero, sin], -1), (1, reps))
    s2 = jnp.tile(jnp.concatenate([-sin, zero], -1), (1, reps))
    return c, s1, s2


def _rope(x, c, s1, s2, half):
    w = x.shape[-1]
    return x * c + pltpu.roll(x, half, 1) * s1 + pltpu.roll(x, w - half, 1) * s2


def _prep_a_body(z_ref, c_ref, s1_ref, s2_ref, q_ref, k_ref, vt_ref):
    z = z_ref[0]
    qk = _rope(z[:, :512], c_ref[...], s1_ref[...], s2_ref[...], DIFF_DK // 2)
    lane = lax.broadcasted_iota(jnp.int32, (z.shape[0], HEAD_DIM), 1)
    scale = DIFF_DK ** -0.5 * LOG2E
    for h in range(GROUP_HEADS):
        q = qk[:, h * 64:(h + 1) * 64] * scale
        q_ref[0, h, 0] = jnp.where(lane < DIFF_DK, q, 0.0).astype(BF16)
        q_ref[0, h, 1] = jnp.where(lane >= DIFF_DK, q, 0.0).astype(BF16)
        k_ref[0, h] = qk[:, 256 + h * 64:256 + (h + 1) * 64].astype(BF16)
    vt_ref[0] = z[:, 512:768].T.reshape(vt_ref.shape[1:]).astype(BF16)


def _prep_a(z_a, tabs, ts):
    b, s, _ = z_a.shape
    tspec = pl.BlockSpec((ts, 512), lambda bi, i: (i, 0))
    return pl.pallas_call(
        _prep_a_body,
        grid=(b, s // ts),
        in_specs=[pl.BlockSpec((1, ts, 768), lambda bi, i: (bi, i, 0)), tspec, tspec, tspec],
        out_specs=[pl.BlockSpec((1, 4, 2, ts, 64), lambda bi, i: (bi, 0, 0, i, 0)),
                   pl.BlockSpec((1, 4, ts, 64), lambda bi, i: (bi, 0, i, 0)),
                   pl.BlockSpec((1, 4, 64, ts), lambda bi, i: (bi, 0, 0, i))],
        out_shape=[jax.ShapeDtypeStruct((b, 4, 2, s, 64), BF16),
                   jax.ShapeDtypeStruct((b, 4, s, 64), BF16),
                   jax.ShapeDtypeStruct((b, 4, 64, s), BF16)],
        compiler_params=_cparams("parallel", "parallel"),
        name="prep_a",
    )(z_a, *tabs)


def _prep_b_body(z_ref, c_ref, s1_ref, s2_ref, qn_ref, qr_ref, ka_ref, kw_ref, kc_ref, vc_ref,
                 vs_ref, vw_ref, g_ref, *, ts):
    z = z_ref[0]
    scale = HEAD_DIM ** -0.5 * LOG2E
    rp = _rope(z[:, :384], c_ref[...], s1_ref[...], s2_ref[...], HEAD_DIM // 2)
    for h in range(GROUP_HEADS):
        qn_ref[0, h] = (z[:, h * 64:(h + 1) * 64] * scale).astype(BF16)
        qr_ref[0, h] = (rp[:, h * 64:(h + 1) * 64] * scale).astype(BF16)
    kpos = pl.program_id(1) * ts + lax.broadcasted_iota(jnp.int32, (ts, SEL_BLOCK), 0)
    lane = lax.broadcasted_iota(jnp.int32, (ts, SEL_BLOCK), 1)
    onehot = jnp.where((kpos >> SEL_SHIFT) == lane, 1.0, 0.0)
    ka_ref[0] = jnp.concatenate([rp[:, 256:320], onehot], axis=-1).astype(BF16)
    kw_ref[0] = rp[:, 320:384].astype(BF16)
    kc_ref[0] = z[:, 384:448].astype(BF16)
    vc_ref[0] = z[:, 448:512].astype(BF16)
    vt = z[:, 512:640].T.astype(BF16)
    vs_ref[0] = vt[:HEAD_DIM]
    vw_ref[0] = vt[HEAD_DIM:]
    g_ref[0] = jax.nn.sigmoid(z[:, 640:768])


def _prep_b(z_b, tabs, ts):
    b, s, _ = z_b.shape
    tspec = pl.BlockSpec((ts, 384), lambda bi, i: (i, 0))
    h4 = pl.BlockSpec((1, 4, ts, 64), lambda bi, i: (bi, 0, i, 0))
    s64 = pl.BlockSpec((1, ts, 64), lambda bi, i: (bi, i, 0))
    s128 = pl.BlockSpec((1, ts, 128), lambda bi, i: (bi, i, 0))
    t64 = pl.BlockSpec((1, 64, ts), lambda bi, i: (bi, 0, i))
    sd = lambda *shape, dt=BF16: jax.ShapeDtypeStruct(shape, dt)
    return pl.pallas_call(
        functools.partial(_prep_b_body, ts=ts),
        grid=(b, s // ts),
        in_specs=[pl.BlockSpec((1, ts, 768), lambda bi, i: (bi, i, 0)), tspec, tspec, tspec],
        out_specs=[h4, h4, s128, s64, s64, s64, t64, t64, s128],
        out_shape=[sd(b, 4, s, 64), sd(b, 4, s, 64), sd(b, s, 128), sd(b, s, 64), sd(b, s, 64),
                   sd(b, s, 64), sd(b, 64, s), sd(b, 64, s), sd(b, s, 128, dt=F32)],
        compiler_params=_cparams("parallel", "parallel"),
        name="prep_b",
    )(z_b, *tabs)


def _prep_c_body(z_ref, c_ref, s1_ref, s2_ref, qg_ref, kg_ref, wq_ref, wkv_ref, q_ref, k_ref, vt_ref):
    z = z_ref[0]
    scale = (MLA_NOPE + MLA_ROPE) ** -0.5 * LOG2E
    cq = _rms(z[:, :384], qg_ref[...]).astype(BF16)
    ckv = _rms(z[:, 384:512], kg_ref[...]).astype(BF16)
    qc = jnp.dot(cq, wq_ref[...], preferred_element_type=F32)
    kvc = jnp.dot(ckv, wkv_ref[...], preferred_element_type=F32)
    half = MLA_ROPE // 2
    c, s1, s2 = c_ref[...], s1_ref[...], s2_ref[...]
    q_rope = _rope(qc[:, 256:384], c, s1, s2, half)
    k_rope = _rope(z[:, 512:640], c, s1, s2, half)[:, :MLA_ROPE]
    for h in range(GROUP_HEADS):
        q = jnp.concatenate([qc[:, h * 64:(h + 1) * 64], q_rope[:, h * 32:(h + 1) * 32]], axis=-1)
        q_ref[0, h] = (q * scale).astype(BF16)
        k_ref[0, h] = jnp.concatenate([kvc[:, h * 64:(h + 1) * 64], k_rope], axis=-1).astype(BF16)
    vt_ref[0] = kvc[:, 256:512].T.reshape(vt_ref.shape[1:]).astype(BF16)


def _prep_c(z_c, tabs, q_gain, kv_gain, wq, wkv, ts):
    b, s, _ = z_c.shape
    tspec = pl.BlockSpec((ts, 128), lambda bi, i: (i, 0))
    full = lambda a: pl.BlockSpec(a.shape, lambda bi, i: (0,) * a.ndim)
    dk = MLA_NOPE + MLA_ROPE
    return pl.pallas_call(
        _prep_c_body,
        grid=(b, s // ts),
        in_specs=[pl.BlockSpec((1, ts, 640), lambda bi, i: (bi, i, 0)), tspec, tspec, tspec,
                  full(q_gain), full(kv_gain), full(wq), full(wkv)],
        out_specs=[pl.BlockSpec((1, 4, ts, dk), lambda bi, i: (bi, 0, i, 0)),
                   pl.BlockSpec((1, 4, ts, dk), lambda bi, i: (bi, 0, i, 0)),
                   pl.BlockSpec((1, 4, 64, ts), lambda bi, i: (bi, 0, 0, i))],
        out_shape=[jax.ShapeDtypeStruct((b, 4, s, dk), BF16),
                   jax.ShapeDtypeStruct((b, 4, s, dk), BF16),
                   jax.ShapeDtypeStruct((b, 4, 64, s), BF16)],
        compiler_params=_cparams("parallel", "parallel"),
        name="prep_c",
    )(z_c, *tabs, q_gain, kv_gain, wq, wkv)


def _prep_d_body(z_ref, c_ref, s1_ref, s2_ref, q_ref, k_ref, vt_ref):
    z = z_ref[0]
    scale = HEAD_DIM ** -0.5 * LOG2E
    qk = _rope(z[:, :512], c_ref[...], s1_ref[...], s2_ref[...], HEAD_DIM // 2)
    for h in range(GROUP_HEADS):
        q_ref[0, h] = (qk[:, h * 64:(h + 1) * 64] * scale).astype(BF16)
        k_ref[0, h] = qk[:, 256 + h * 64:256 + (h + 1) * 64].astype(BF16)
    vt_ref[0] = z[:, 512:768].T.reshape(vt_ref.shape[1:]).astype(BF16)


def _prep_d(z_d, tabs, ts):
    b, s, _ = z_d.shape
    tspec = pl.BlockSpec((ts, 512), lambda bi, i: (i, 0))
    h4 = pl.BlockSpec((1, 4, ts, 64), lambda bi, i: (bi, 0, i, 0))
    return pl.pallas_call(
        _prep_d_body,
        grid=(b, s // ts),
        in_specs=[pl.BlockSpec((1, ts, 768), lambda bi, i: (bi, i, 0)), tspec, tspec, tspec],
        out_specs=[h4, h4, pl.BlockSpec((1, 4, 64, ts), lambda bi, i: (bi, 0, 0, i))],
        out_shape=[jax.ShapeDtypeStruct((b, 4, s, 64), BF16)] * 2 + [jax.ShapeDtypeStruct((b, 4, 64, s), BF16)],
        compiler_params=_cparams("parallel", "parallel"),
        name="prep_d",
    )(z_d, *tabs)


def _flash_schedule(nq, tq, tk, sk, window):
    qi, kj, fl = [], [], []
    for i in range(nq):
        q_lo, q_hi = i * tq, i * tq + tq - 1
        j_hi = min(q_hi, sk - 1) // tk
        j_lo = 0 if window is None else max(0, q_lo - window) // tk
        for j in range(j_lo, j_hi + 1):
            k_min, k_max = j * tk, j * tk + tk - 1
            full = k_max <= q_lo and (window is None or q_hi - k_min <= window)
            qi.append(i)
            kj.append(j)
            fl.append((1 if j == j_lo else 0) | (2 if j == j_hi else 0) | (0 if full else 4))
    return (np.asarray(qi, np.int32), np.asarray(kj, np.int32), np.asarray(fl, np.int32))


def _flash_body(qi_ref, kj_ref, fl_ref, q_ref, k_ref, vt_ref, o_ref, m_sc, acc_sc, *,
                gb, hq, tq, tk, window, with_lse):
    t = pl.program_id(1)
    fl = fl_ref[t]
    rows = hq * tq
    dv = vt_ref.shape[1]

    @pl.when((fl & 1) != 0)
    def _():
        m_sc[...] = jnp.full(m_sc.shape, NEG_BIG, F32)
        acc_sc[...] = jnp.zeros(acc_sc.shape, F32)

    def step(masked):
        q = q_ref[...].reshape(gb, rows, q_ref.shape[-1])
        st = jnp.einsum("gkd,gqd->gkq", k_ref[...], q, preferred_element_type=F32)
        if masked:
            krow = lax.broadcasted_iota(jnp.int32, (tk, rows), 0)
            qcol = lax.broadcasted_iota(jnp.int32, (tk, rows), 1)
            if hq > 1:
                qcol = qcol & (tq - 1)
            dist = (qi_ref[t] * tq - kj_ref[t] * tk) + qcol - krow
            ok = dist >= 0
            if window is not None:
                ok = ok & (dist <= window)
            st = jnp.where(ok[None], st, NEG_BIG)
        m_prev = m_sc[...]
        m_new = jnp.maximum(m_prev, jnp.max(st, axis=1, keepdims=True))
        alpha = jnp.exp2(m_prev - m_new)
        p = jnp.exp2((st - m_new).astype(BF16))
        v_ext = jnp.concatenate([vt_ref[...], jnp.ones((gb, ONES_ROWS, tk), BF16)], axis=1)
        acc_sc[...] = alpha * acc_sc[...] + jnp.einsum("gdk,gkq->gdq", v_ext, p, preferred_element_type=F32)
        m_sc[...] = m_new

    @pl.when((fl & 4) != 0)
    def _():
        step(True)

    @pl.when((fl & 4) == 0)
    def _():
        step(False)

    @pl.when((fl & 2) != 0)
    def _():
        l = acc_sc[:, dv:dv + 1, :]
        o_t = acc_sc[:, :dv, :] / l
        if with_lse:
            extra = jnp.broadcast_to(m_sc[...] * LN2 + jnp.log(l), o_t.shape)
        else:
            extra = jnp.zeros((gb, LANES - dv, rows), F32)
        dvo = o_ref.shape[-1]
        for g in range(gb):
            o = jnp.concatenate([o_t[g], extra[g]], axis=0).T
            o_ref[g] = o[:, :dvo].reshape(o_ref.shape[1:]).astype(o_ref.dtype)


def _flash(q, k, vt, *, tq, tk, gb=1, window=None, with_lse=False, name="flash"):
    g, hq, sq, dk = q.shape
    _, dv, sk = vt.shape
    tq, tk = min(tq, sq), min(tk, sk)
    assert g % gb == 0 and sq % tq == 0 and sk % tk == 0
    assert tq & (tq - 1) == 0 and 2 * dv == LANES
    qi, kj, fl = _flash_schedule(sq // tq, tq, tk, sk, window)
    dvo = 2 * dv if with_lse else dv
    rows = hq * tq
    grid_spec = pltpu.PrefetchScalarGridSpec(
        num_scalar_prefetch=3,
        grid=(g // gb, len(qi)),
        in_specs=[pl.BlockSpec((gb, hq, tq, dk), lambda gi, t, qi, kj, fl: (gi, 0, qi[t], 0)),
                  pl.BlockSpec((gb, tk, dk), lambda gi, t, qi, kj, fl: (gi, kj[t], 0)),
                  pl.BlockSpec((gb, dv, tk), lambda gi, t, qi, kj, fl: (gi, 0, kj[t]))],
        out_specs=pl.BlockSpec((gb, hq, tq, dvo), lambda gi, t, qi, kj, fl: (gi, 0, qi[t], 0)),
        scratch_shapes=[pltpu.VMEM((gb, 1, rows), F32), pltpu.VMEM((gb, dv + ONES_ROWS, rows), F32)],
    )
    return pl.pallas_call(
        functools.partial(_flash_body, gb=gb, hq=hq, tq=tq, tk=tk, window=window, with_lse=with_lse),
        grid_spec=grid_spec,
        out_shape=jax.ShapeDtypeStruct((g, hq, sq, dvo), F32),
        compiler_params=_cparams("parallel", "arbitrary"),
        name=name,
    )(jnp.asarray(qi), jnp.asarray(kj), jnp.asarray(fl), q, k, vt)


def _gelu_tanh(x):
    return 0.5 * x * (1.0 + jnp.tanh(math.sqrt(2.0 / math.pi) * (x + 0.044715 * (x * x * x))))


def _compress_body(r_ref, pos_ref, w1_ref, w2_ref, o_ref):
    half = CMP_STRIDE * HEAD_DIM
    r = r_ref[0, 0]
    w1 = w1_ref[0]
    top = jnp.dot(r, w1[:half], preferred_element_type=F32)
    bot = jnp.dot(r, w1[half:], preferred_element_type=F32)
    nc = r.shape[0]
    posb = jnp.dot(pos_ref[0], w1, preferred_element_type=F32)[:1]
    pre = top + pltpu.roll(bot, nc - 1, 0) + posb
    hid = _gelu_tanh(pre).astype(BF16)
    o_ref[0, 0] = jnp.dot(hid, w2_ref[0], preferred_element_type=F32).astype(o_ref.dtype)


def _nsa_compress(kv_rows, pos_flat, w1, w2):
    _, b, nc, _ = kv_rows.shape
    return pl.pallas_call(
        _compress_body,
        grid=(2, b),
        in_specs=[pl.BlockSpec((1, 1, nc, 1024), lambda w, bi: (w, bi, 0, 0)),
                  pl.BlockSpec((1, 8, 2048), lambda w, bi: (w, 0, 0)),
                  pl.BlockSpec((1, 2048, 256), lambda w, bi: (w, 0, 0)),
                  pl.BlockSpec((1, 256, 64), lambda w, bi: (w, 0, 0))],
        out_specs=pl.BlockSpec((1, 1, nc, 64), lambda w, bi: (w, bi, 0, 0)),
        out_shape=jax.ShapeDtypeStruct((2, b, nc, 64), BF16),
        compiler_params=_cparams("parallel", "parallel"),
        name="nsa_compress",
    )(kv_rows, pos_flat, w1, w2)


def _cmp_select_body(qn_ref, qr_ref, kc_ref, vc_ref, cov_ref, o_ref, qa_ref, *, tq, n_sel, k_top):
    ncp = kc_ref.shape[2]
    pos = pl.program_id(1) * tq + lax.broadcasted_iota(jnp.int32, (tq, ncp), 0)
    blk_end = lax.broadcasted_iota(jnp.int32, (tq, ncp), 1) * CMP_STRIDE + (CMP_LEN - 1)
    mask = blk_end <= pos
    kc, vc = kc_ref[0, 0], vc_ref[0, 0]
    p_sum = jnp.zeros((tq, ncp), F32)
    for h in range(GROUP_HEADS):
        s = lax.dot_general(qn_ref[0, h], kc, (((1,), (1,)), ((), ())), preferred_element_type=F32)
        s = jnp.where(mask, s, NEG_BIG)
        mx = jnp.max(s, axis=-1, keepdims=True)
        mx = jnp.where(mx > 0.5 * NEG_BIG, mx, 0.0)
        e = jnp.where(mask, jnp.exp2(s - mx), 0.0)
        p = e / jnp.maximum(jnp.sum(e, axis=-1, keepdims=True), 1e-30)
        o_ref[0, h] = jnp.dot(p.astype(BF16), vc, preferred_element_type=F32)
        p_sum = p_sum + p
    p_hi = p_sum.astype(BF16)
    p_lo = (p_sum - p_hi.astype(F32)).astype(BF16)
    cov = cov_ref[...]
    imp = (jnp.dot(p_hi, cov, preferred_element_type=F32) + jnp.dot(p_lo, cov, preferred_element_type=F32))
    lanes = cov.shape[1]
    tpos = pl.program_id(1) * tq + lax.broadcasted_iota(jnp.int32, (tq, lanes), 0)
    cur = tpos >> SEL_SHIFT
    j = lax.broadcasted_iota(jnp.int32, (tq, lanes), 1)
    valid = j <= cur
    forced = (j == 0) | (j == cur) | (j == cur - 1)
    score = jnp.where(valid, jnp.where(forced, FORCED_SCORE, imp), -jnp.inf)
    rank = jnp.zeros((tq, lanes), jnp.int32)
    for i in range(n_sel):
        col = score[:, i:i + 1]
        ahead = (col > score) | ((col == score) & (i < j))
        rank = rank + ahead.astype(jnp.int32)
    bias = jnp.where(valid & (rank < k_top), 0.0, SEL_BIAS).astype(BF16)
    for h in range(GROUP_HEADS):
        qa_ref[0, h] = jnp.concatenate([qr_ref[0, h], bias], axis=-1)


def _cover_matrix(ncp, n_sel, lanes):
    c_start = np.arange(ncp)[:, None] * CMP_STRIDE
    s_start = np.arange(lanes)[None, :] * SEL_BLOCK
    cov = np.clip(np.minimum(c_start + CMP_LEN, s_start + SEL_BLOCK) - np.maximum(c_start, s_start), 0, None)
    cov = cov.astype(np.float32) / CMP_LEN
    cov[:, n_sel:] = 0.0
    cov[ncp - 1:, :] = 0.0
    return cov


def _cmp_select(qn, qr, kvc, tq):
    b, _, s, _ = qn.shape
    ncp = kvc.shape[2]
    n_sel = s // SEL_BLOCK
    assert n_sel <= SEL_BLOCK
    k_top = min(SEL_TOPN, n_sel)
    cov = jnp.asarray(_cover_matrix(ncp, n_sel, SEL_BLOCK), BF16)
    h4 = lambda d: pl.BlockSpec((1, 4, tq, d), lambda bi, i: (bi, 0, i, 0))
    return pl.pallas_call(
        functools.partial(_cmp_select_body, tq=tq, n_sel=n_sel, k_top=k_top),
        grid=(b, s // tq),
        in_specs=[h4(64), h4(64),
                  pl.BlockSpec((1, 1, ncp, 64), lambda bi, i: (0, bi, 0, 0)),
                  pl.BlockSpec((1, 1, ncp, 64), lambda bi, i: (1, bi, 0, 0)),
                  pl.BlockSpec(cov.shape, lambda bi, i: (0, 0))],
        out_specs=[h4(64), h4(128)],
        out_shape=[jax.ShapeDtypeStruct((b, 4, s, 64), F32), jax.ShapeDtypeStruct((b, 4, s, 128), BF16)],
        compiler_params=_cparams("parallel", "parallel"),
        name="nsa_cmp_select",
    )(qn, qr, kvc, kvc, cov)


def _merge_a_body(o_ref, lam_ref, g_ref, y_ref, *, out_scale):
    outs = []
    for h in range(GROUP_HEADS):
        o = o_ref[0, h, 0] - lam_ref[...] * o_ref[0, h, 1]
        outs.append(_rms(o, g_ref[...]) * out_scale)
    y_ref[0] = jnp.concatenate(outs, axis=-1).astype(y_ref.dtype)


def _merge_a(o, lam, subln, out_scale, ts):
    b, _, _, s, _ = o.shape
    return pl.pallas_call(
        functools.partial(_merge_a_body, out_scale=out_scale),
        grid=(b, s // ts),
        in_specs=[pl.BlockSpec((1, 4, 2, ts, 64), lambda bi, i: (bi, 0, 0, i, 0)),
                  pl.BlockSpec((1, 64), lambda bi, i: (0, 0)),
                  pl.BlockSpec((1, 64), lambda bi, i: (0, 0))],
        out_specs=pl.BlockSpec((1, ts, 256), lambda bi, i: (bi, i, 0)),
        out_shape=jax.ShapeDtypeStruct((b, s, 256), BF16),
        compiler_params=_cparams("parallel", "parallel"),
        name="merge_a",
    )(o, lam, subln)


def _merge_b_body(oc_ref, os_ref, ow_ref, g_ref, y_ref):
    g = g_ref[0]
    outs = []
    for h in range(GROUP_HEADS):
        outs.append(g[:, 3 * h:3 * h + 1] * oc_ref[0, h] + g[:, 3 * h + 1:3 * h + 2] * os_ref[0, h]
                    + g[:, 3 * h + 2:3 * h + 3] * ow_ref[0, h])
    y_ref[0] = jnp.concatenate(outs, axis=-1).astype(y_ref.dtype)


def _merge_b(o_cmp, o_sel, o_win, gates, ts):
    b, _, s, _ = o_cmp.shape
    h4 = pl.BlockSpec((1, 4, ts, 64), lambda bi, i: (bi, 0, i, 0))
    return pl.pallas_call(
        _merge_b_body,
        grid=(b, s // ts),
        in_specs=[h4, h4, h4, pl.BlockSpec((1, ts, 128), lambda bi, i: (bi, i, 0))],
        out_specs=pl.BlockSpec((1, ts, 256), lambda bi, i: (bi, i, 0)),
        out_shape=jax.ShapeDtypeStruct((b, s, 256), BF16),
        compiler_params=_cparams("parallel", "parallel"),
        name="merge_b",
    )(o_cmp, o_sel, o_win, gates)


def _merge_c_body(o_ref, y_ref):
    y_ref[0] = jnp.concatenate([o_ref[0, h] for h in range(GROUP_HEADS)], axis=-1).astype(y_ref.dtype)


def _merge_c(o, ts):
    b, _, s, _ = o.shape
    return pl.pallas_call(
        _merge_c_body,
        grid=(b, s // ts),
        in_specs=[pl.BlockSpec((1, 4, ts, 64), lambda bi, i: (bi, 0, i, 0))],
        out_specs=pl.BlockSpec((1, ts, 256), lambda bi, i: (bi, i, 0)),
        out_shape=jax.ShapeDtypeStruct((b, s, 256), BF16),
        compiler_params=_cparams("parallel", "parallel"),
        name="merge_c",
    )(o)


def _merge_d_body(o1_ref, o2_ref, o3_ref, y_ref):
    outs = []
    for h in range(GROUP_HEADS):
        e = [r[0, h] for r in (o1_ref, o2_ref, o3_ref)]
        lse = [x[:, 64:] for x in e]
        mx = jnp.maximum(jnp.maximum(lse[0], lse[1]), lse[2])
        w = [jnp.exp(x - mx) for x in lse]
        den = w[0] + w[1] + w[2]
        outs.append((w[0] * e[0][:, :64] + w[1] * e[1][:, :64] + w[2] * e[2][:, :64]) / den)
    y_ref[0] = jnp.concatenate(outs, axis=-1).astype(y_ref.dtype)


def _merge_d(o1, o2, o3, ts):
    b, _, s, _ = o1.shape
    h4 = pl.BlockSpec((1, 4, ts, 128), lambda bi, i: (bi, 0, i, 0))
    return pl.pallas_call(
        _merge_d_body,
        grid=(b, s // ts),
        in_specs=[h4, h4, h4],
        out_specs=pl.BlockSpec((1, ts, 256), lambda bi, i: (bi, i, 0)),
        out_shape=jax.ShapeDtypeStruct((b, s, 256), BF16),
        compiler_params=_cparams("parallel", "parallel"),
        name="merge_d",
    )(o1, o2, o3)


def _permute_w_in(w_in):
    a_cols = 768
    b0 = a_cols
    nq = 256
    seg = lambda i: w_in[:, b0 + nq + i * 64: b0 + nq + (i + 1) * 64]
    kc, vc, ks, vs, kw, vw = (seg(i) for i in range(6))
    gb = w_in[:, b0 + nq + 384: b0 + nq + 384 + 12]
    d = w_in.shape[0]
    w_b = jnp.concatenate([w_in[:, b0:b0 + nq], ks, kw, kc, vc, vs, vw, gb, jnp.zeros((d, 116), w_in.dtype)], axis=1)
    c0 = b0 + 652
    w_c = jnp.concatenate([w_in[:, c0:c0 + 544], jnp.zeros((d, 96), w_in.dtype)], axis=1)
    d0 = c0 + 544
    return jnp.concatenate([w_in[:, :a_cols], w_b, w_c, w_in[:, d0:d0 + 768]], axis=1).astype(BF16)


def _residue_classes(t, dil):
    g, s, d = t.shape
    return t.reshape(g, s // dil, dil, d).transpose(0, 2, 1, 3).reshape(g * dil, s // dil, d)


def _residue_classes_t(t, dil):
    g, d, s = t.shape
    return t.reshape(g, d, s // dil, dil).transpose(0, 3, 1, 2).reshape(g * dil, d, s // dil)


def _from_residue_classes(t, dil):
    gd, l, d = t.shape
    return t.reshape(gd // dil, dil, l, d).transpose(0, 2, 1, 3).reshape(gd // dil, l * dil, d)


def _token_mixers(x, layer, rope_tabs, norm_g, w_in, w_out, diff_lambda, diff_subln, cmp_pos, cmp_w1, cmp_w2,
                  q_norm, kv_norm, w_uq, w_ukv):
    b, s, d = x.shape
    t = b * s
    ts = min(512, s)
    z_a, z_b, z_c, z_d = _mm([x.reshape(t, d)], _permute_w_in(w_in), gain=norm_g, tm=min(512, t),
                             splits=(768, 768, 640, 768), name="w_in")
    z_a, z_b, z_c, z_d = (z.reshape(b, s, -1) for z in (z_a, z_b, z_c, z_d))

    qa, ka, va = _prep_a(z_a, rope_tabs["g32w512"], ts)
    oa = _flash(qa.reshape(b * 4, 2, s, 64), ka.reshape(b * 4, s, 64), va.reshape(b * 4, 64, s),
                tq=512, tk=512, gb=2, name="flash_diff").reshape(b, 4, 2, s, 64)
    lam_init = 0.8 - 0.6 * math.exp(-0.3 * layer)
    lp = diff_lambda.astype(F32)
    lam = jnp.exp(jnp.sum(lp[0] * lp[1])) - jnp.exp(jnp.sum(lp[2] * lp[3])) + lam_init
    y_a = _merge_a(oa, jnp.full((1, 64), lam, F32), diff_subln.reshape(1, 64).astype(F32), 1.0 - lam_init, ts)

    qn, qr, k_aug, kw, kc, vc, vs, vw, gates = _prep_b(z_b, rope_tabs["g64w384"], ts)
    kv_rows = jnp.stack([kc, vc]).reshape(2, b, s // CMP_STRIDE, CMP_STRIDE * HEAD_DIM)
    pos_flat = jnp.broadcast_to(cmp_pos.reshape(2, 1, CMP_LEN * HEAD_DIM), (2, 8, CMP_LEN * HEAD_DIM)).astype(BF16)
    kvc = _nsa_compress(kv_rows, pos_flat, cmp_w1.astype(BF16), cmp_w2.astype(BF16))
    o_cmp, q_aug = _cmp_select(qn, qr, kvc, min(256, s))
    o_sel = _flash(q_aug, k_aug, vs, tq=256, tk=512, name="flash_sel")
    o_win = _flash(qr, kw, vw, tq=256, tk=256, window=NSA_WINDOW - 1, name="flash_win")
    y_b = _merge_b(o_cmp, o_sel, o_win, gates, ts)

    wq = w_uq.reshape(-1, 4, MLA_NOPE + MLA_ROPE)
    wq = jnp.concatenate([wq[:, :, :MLA_NOPE].reshape(-1, 256), wq[:, :, MLA_NOPE:].reshape(-1, 128)], axis=1)
    wkv = w_ukv.reshape(-1, 4, MLA_NOPE + MLA_DV)
    wkv = jnp.concatenate([wkv[:, :, :MLA_NOPE].reshape(-1, 256), wkv[:, :, MLA_NOPE:].reshape(-1, 256)], axis=1)
    qc, kcat, vcv = _prep_c(z_c, rope_tabs["g32w128"], q_norm.reshape(1, -1).astype(F32),
                            kv_norm.reshape(1, -1).astype(F32), wq.astype(BF16), wkv.astype(BF16), ts)
    dk = MLA_NOPE + MLA_ROPE
    oc = _flash(qc.reshape(b * 4, 1, s, dk), kcat.reshape(b * 4, s, dk), vcv.reshape(b * 4, 64, s),
                tq=512, tk=512, gb=4, name="flash_mla").reshape(b, 4, s, 64)
    y_c = _merge_c(oc, ts)

    qd, kd, vd = _prep_d(z_d, rope_tabs["g64w512"], ts)
    qd, kd, vd = qd.reshape(b * 4, s, 64), kd.reshape(b * 4, s, 64), vd.reshape(b * 4, 64, s)
    o_d = []
    for window, dil in DILATED_PATTERNS:
        l = s // dil
        qq, kk, vv = _residue_classes(qd, dil), _residue_classes(kd, dil), _residue_classes_t(vd, dil)
        g = qq.shape[0]
        gb = math.gcd(g, 16)
        o = _flash(qq.reshape(g, 1, l, 64), kk, vv, tq=128, tk=128, gb=gb, window=window // dil,
                   with_lse=True, name=f"flash_dil{dil}")
        o_d.append(_from_residue_classes(o.reshape(g, l, 128), dil).reshape(b, 4, s, 128))
    y_d = _merge_d(*o_d, ts)

    ys = [y.reshape(t, 256) for y in (y_a, y_b, y_c, y_d)]
    return _mm(ys, w_out.astype(BF16), res=x.reshape(t, d), tm=min(512, t), name="w_out").reshape(b, s, d)


def _cross_body(x_ref, g_ref, wq_ref, k_ref, v_ref, wo_ref, o_ref):
    x = x_ref[0]
    h = _rms(x, g_ref[...]).astype(BF16)
    q = jnp.dot(h, wq_ref[...], preferred_element_type=F32).astype(BF16)
    dh = q.shape[-1] // CROSS_HEADS
    outs = []
    for hd in range(CROSS_HEADS):
        sl = slice(hd * dh, (hd + 1) * dh)
        s = lax.dot_general(q[:, sl], k_ref[0, :, sl], (((1,), (1,)), ((), ())), preferred_element_type=F32)
        e = jnp.exp(s - jnp.max(s, axis=-1, keepdims=True))
        p = e / jnp.sum(e, axis=-1, keepdims=True)
        outs.append(jnp.dot(p.astype(BF16), v_ref[0, :, sl], preferred_element_type=F32))
    o = jnp.concatenate(outs, axis=-1).astype(BF16)
    o_ref[0] = x + jnp.dot(o, wo_ref[...], preferred_element_type=F32)


def _cross_attention(x, mem, g_x, g_mem, wq, wk, wv, wo):
    b, s, d = x.shape
    m = mem.shape[1]
    ts = min(512, s)
    dh = d // CROSS_HEADS
    k, v = _mm([mem.reshape(b * m, d)], jnp.concatenate([wk, wv], axis=1).astype(BF16), gain=g_mem,
               tm=min(512, b * m), splits=(d, d), out_dtype=BF16, name="cross_kv")
    full = lambda a: pl.BlockSpec(a.shape, lambda bi, i: (0,) * a.ndim)
    wq_s = (wq * dh ** -0.5).astype(BF16)
    wo_b = wo.astype(BF16)
    g2 = g_x.reshape(1, d).astype(F32)
    return pl.pallas_call(
        _cross_body,
        grid=(b, s // ts),
        in_specs=[pl.BlockSpec((1, ts, d), lambda bi, i: (bi, i, 0)), full(g2), full(wq_s),
                  pl.BlockSpec((1, m, d), lambda bi, i: (bi, 0, 0)),
                  pl.BlockSpec((1, m, d), lambda bi, i: (bi, 0, 0)), full(wo_b)],
        out_specs=pl.BlockSpec((1, ts, d), lambda bi, i: (bi, i, 0)),
        out_shape=jax.ShapeDtypeStruct((b, s, d), F32),
        compiler_params=_cparams("parallel", "parallel"),
        name="cross_attn",
    )(x, g2, wq_s, k.reshape(b, m, d), v.reshape(b, m, d), wo_b)


def _swiglu_body(x_ref, g_ref, wg_ref, wu_ref, wd_ref, o_ref, h_sc, acc_sc):
    f = pl.program_id(1)

    @pl.when(f == 0)
    def _():
        h_sc[...] = _rms(x_ref[...], g_ref[...]).astype(BF16)
        acc_sc[...] = x_ref[...]

    h = h_sc[...]
    gate = jnp.dot(h, wg_ref[...], preferred_element_type=F32)
    up = jnp.dot(h, wu_ref[...], preferred_element_type=F32)
    act = (gate * jax.nn.sigmoid(gate) * up).astype(BF16)
    acc_sc[...] += jnp.dot(act, wd_ref[...], preferred_element_type=F32)

    @pl.when(f == pl.num_programs(1) - 1)
    def _():
        o_ref[...] = acc_sc[...]


def _swiglu(x, gain, wg, wu, wd, tm=512, tf=1408):
    t, d = x.shape
    ff = wg.shape[1]
    tm, tf = min(tm, t), min(tf, ff)
    assert t % tm == 0 and ff % tf == 0
    return pl.pallas_call(
        _swiglu_body,
        grid=(t // tm, ff // tf),
        in_specs=[pl.BlockSpec((tm, d), lambda i, f: (i, 0)),
                  pl.BlockSpec((1, d), lambda i, f: (0, 0)),
                  pl.BlockSpec((d, tf), lambda i, f: (0, f)),
                  pl.BlockSpec((d, tf), lambda i, f: (0, f)),
                  pl.BlockSpec((tf, d), lambda i, f: (f, 0))],
        out_specs=pl.BlockSpec((tm, d), lambda i, f: (i, 0)),
        out_shape=jax.ShapeDtypeStruct((t, d), F32),
        scratch_shapes=[pltpu.VMEM((tm, d), BF16), pltpu.VMEM((tm, d), F32)],
        compiler_params=_cparams("parallel", "arbitrary"),
        name="swiglu",
    )(x, gain.reshape(1, d).astype(F32), wg.astype(BF16), wu.astype(BF16), wd.astype(BF16))


def _router_body(x_ref, g_ref, r_ref, h_ref, rt_ref):
    h = _rms(x_ref[...], g_ref[...])
    h_ref[...] = h
    h_hi = h.astype(BF16)
    h_lo = (h - h_hi.astype(F32)).astype(BF16)
    r = r_ref[...]
    r_hi = r.astype(BF16)
    r_lo = (r - r_hi.astype(F32)).astype(BF16)
    logits = (jnp.dot(h_hi, r_hi, preferred_element_type=F32) + jnp.dot(h_lo, r_hi, preferred_element_type=F32)
              + jnp.dot(h_hi, r_lo, preferred_element_type=F32))
    lane = lax.broadcasted_iota(jnp.int32, logits.shape, 1)
    lg = jnp.where(lane < N_EXPERTS, logits, -jnp.inf)
    v0 = jnp.max(lg, axis=-1, keepdims=True)
    e0 = jnp.min(jnp.where(lg == v0, lane, LANES), axis=-1, keepdims=True)
    lg1 = jnp.where(lane == e0, -jnp.inf, lg)
    v1 = jnp.max(lg1, axis=-1, keepdims=True)
    e1 = jnp.min(jnp.where(lg1 == v1, lane, LANES), axis=-1, keepdims=True)
    g1 = 1.0 / (1.0 + jnp.exp(v0 - v1))
    g0 = 1.0 - g1
    out = jnp.where(lane == 0, g0, jnp.where(lane == 1, g1, 0.0))
    out = jnp.where(lane == 2, e0.astype(F32), jnp.where(lane == 3, e1.astype(F32), out))
    rt_ref[...] = out


def _router(x, gain, router, tm=512):
    t, d = x.shape
    tm = min(tm, t)
    r_pad = jnp.zeros((d, LANES), F32).at[:, :N_EXPERTS].set(router.astype(F32))
    return pl.pallas_call(
        _router_body,
        grid=(t // tm,),
        in_specs=[pl.BlockSpec((tm, d), lambda i: (i, 0)), pl.BlockSpec((1, d), lambda i: (0, 0)),
                  pl.BlockSpec((d, LANES), lambda i: (0, 0))],
        out_specs=[pl.BlockSpec((tm, d), lambda i: (i, 0)), pl.BlockSpec((tm, LANES), lambda i: (i, 0))],
        out_shape=[jax.ShapeDtypeStruct((t, d), F32), jax.ShapeDtypeStruct((t, LANES), F32)],
        compiler_params=_cparams("parallel"),
        name="moe_router",
    )(x, gain.reshape(1, d).astype(F32), r_pad)


def _row_copy(src_hbm, src_row, dst_ref, dst_row, sem):
    return pltpu.make_async_copy(src_hbm.at[pl.ds(src_row, 1)], dst_ref.at[pl.ds(dst_row, 1)], sem)


def _gather_rows(idx_ref, idx_base, src_hbm, dst_ref, sem, n):
    def start(r, c):
        _row_copy(src_hbm, idx_ref[idx_base + r], dst_ref, r, sem).start()
        return c

    lax.fori_loop(0, n, start, 0, unroll=8)

    def wait(r, c):
        _row_copy(src_hbm, 0, dst_ref, r, sem).wait()
        return c

    lax.fori_loop(0, n, wait, 0, unroll=8)


def _experts_body(be_ref, nb_ref, tok_ref, h_hbm, wg_ref, wu_ref, wd_ref, o_ref, xf_sc, xb_sc, acc_sc, sem):
    i, f = pl.program_id(0), pl.program_id(1)
    rows = xf_sc.shape[0]

    @pl.when(i < nb_ref[0])
    def _():
        @pl.when(f == 0)
        def _():
            _gather_rows(tok_ref, i * rows, h_hbm, xf_sc, sem, rows)
            xb_sc[...] = xf_sc[...].astype(BF16)
            acc_sc[...] = jnp.zeros(acc_sc.shape, F32)

        x = xb_sc[...]
        gate = jnp.dot(x, wg_ref[0], preferred_element_type=F32)
        up = jnp.dot(x, wu_ref[0], preferred_element_type=F32)
        act = (gate * jax.nn.sigmoid(gate) * up).astype(BF16)
        acc_sc[...] += jnp.dot(act, wd_ref[0], preferred_element_type=F32)

        @pl.when(f == pl.num_programs(1) - 1)
        def _():
            o_ref[...] = acc_sc[...]

    @pl.when(i >= nb_ref[0])
    def _():
        o_ref[...] = jnp.zeros(o_ref.shape, F32)


def _experts(h, row_tok, blk_e, n_used, wg, wu, wd, rows, tf=1792):
    n_rows = row_tok.shape[0]
    d = h.shape[1]
    ff = wg.shape[2]
    n_blk = n_rows // rows
    last = ff // tf - 1
    grid_spec = pltpu.PrefetchScalarGridSpec(
        num_scalar_prefetch=3,
        grid=(n_blk, ff // tf),
        in_specs=[pl.BlockSpec(memory_space=pl.ANY),
                  pl.BlockSpec((1, d, tf), lambda i, f, be, nb, tok: (be[i], 0, jnp.where(i < nb[0], f, last))),
                  pl.BlockSpec((1, d, tf), lambda i, f, be, nb, tok: (be[i], 0, jnp.where(i < nb[0], f, last))),
                  pl.BlockSpec((1, tf, d), lambda i, f, be, nb, tok: (be[i], jnp.where(i < nb[0], f, last), 0))],
        out_specs=pl.BlockSpec((rows, d), lambda i, f, be, nb, tok: (i, 0)),
        scratch_shapes=[pltpu.VMEM((rows, d), F32), pltpu.VMEM((rows, d), BF16), pltpu.VMEM((rows, d), F32),
                        pltpu.SemaphoreType.DMA(())],
    )
    return pl.pallas_call(
        _experts_body,
        grid_spec=grid_spec,
        out_shape=jax.ShapeDtypeStruct((n_rows, d), F32),
        compiler_params=_cparams("arbitrary", "arbitrary"),
        name="moe_experts",
    )(blk_e, n_used, row_tok, h, wg, wu, wd)


def _combine_body(dest_ref, x_ref, rt_ref, ys_hbm, o_ref, buf0, buf1, sem):
    tm = x_ref.shape[0]
    base = pl.program_id(0) * tm

    def start(r, c):
        _row_copy(ys_hbm, dest_ref[(base + r) * TOP_K], buf0, r, sem).start()
        _row_copy(ys_hbm, dest_ref[(base + r) * TOP_K + 1], buf1, r, sem).start()
        return c

    lax.fori_loop(0, tm, start, 0, unroll=4)

    def wait(r, c):
        _row_copy(ys_hbm, 0, buf0, r, sem).wait()
        _row_copy(ys_hbm, 0, buf1, r, sem).wait()
        return c

    lax.fori_loop(0, tm, wait, 0, unroll=4)
    rt = rt_ref[...]
    o_ref[...] = x_ref[...] + rt[:, 0:1] * buf0[...] + rt[:, 1:2] * buf1[...]


def _combine(x, routing, ys, dest, tm=256):
    t, d = x.shape
    tm = min(tm, t)
    grid_spec = pltpu.PrefetchScalarGridSpec(
        num_scalar_prefetch=1,
        grid=(t // tm,),
        in_specs=[pl.BlockSpec((tm, d), lambda i, dest: (i, 0)),
                  pl.BlockSpec((tm, LANES), lambda i, dest: (i, 0)),
                  pl.BlockSpec(memory_space=pl.ANY)],
        out_specs=pl.BlockSpec((tm, d), lambda i, dest: (i, 0)),
        scratch_shapes=[pltpu.VMEM((tm, d), F32), pltpu.VMEM((tm, d), F32), pltpu.SemaphoreType.DMA(())],
    )
    return pl.pallas_call(
        _combine_body,
        grid_spec=grid_spec,
        out_shape=jax.ShapeDtypeStruct((t, d), F32),
        compiler_params=_cparams("arbitrary"),
        name="moe_combine",
    )(dest, x, routing, ys)


MOE_ROWS = 512


def _moe(x, gain, router, w_gate, w_up, w_down):
    t, d = x.shape
    h, routing = _router(x, gain, router)
    n_assign = t * TOP_K
    flat_e = routing[:, 2:2 + TOP_K].astype(jnp.int32).reshape(-1)
    onehot = (flat_e[:, None] == jnp.arange(N_EXPERTS)[None, :]).astype(jnp.int32)
    before = jnp.cumsum(onehot, axis=0) - onehot
    counts = jnp.sum(onehot, axis=0)
    padded = (counts + MOE_ROWS - 1) // MOE_ROWS * MOE_ROWS
    pend = jnp.cumsum(padded)
    pstart = pend - padded
    dest = (jnp.sum((pstart[None, :] + before) * onehot, axis=1)).astype(jnp.int32)
    n_rows = -(-n_assign // MOE_ROWS) * MOE_ROWS + N_EXPERTS * MOE_ROWS
    n_blk = n_rows // MOE_ROWS
    row_tok = jnp.zeros((n_rows,), jnp.int32).at[dest].set(jnp.arange(n_assign, dtype=jnp.int32) // TOP_K)
    blk_start = jnp.arange(n_blk, dtype=jnp.int32) * MOE_ROWS
    blk_e = jnp.minimum(jnp.sum((pend[None, :] <= blk_start[:, None]).astype(jnp.int32), axis=1), N_EXPERTS - 1)
    n_used = (pend[-1] // MOE_ROWS).astype(jnp.int32).reshape(1)
    ys = _experts(h, row_tok, blk_e.astype(jnp.int32), n_used, w_gate.astype(BF16), w_up.astype(BF16),
                  w_down.astype(BF16), MOE_ROWS)
    return _combine(x, routing, ys, dest)


def _final_norm_body(x_ref, g_ref, o_ref):
    o_ref[...] = _rms(x_ref[...], g_ref[...])


def _final_norm(x, gain, tm=1024):
    t, d = x.shape
    tm = min(tm, t)
    return pl.pallas_call(
        _final_norm_body,
        grid=(t // tm,),
        in_specs=[pl.BlockSpec((tm, d), lambda i: (i, 0)), pl.BlockSpec((1, d), lambda i: (0, 0))],
        out_specs=pl.BlockSpec((tm, d), lambda i: (i, 0)),
        out_shape=jax.ShapeDtypeStruct((t, d), F32),
        compiler_params=_cparams("parallel"),
        name="final_norm",
    )(x, gain.reshape(1, d).astype(F32))


def _all_rope_tables(s):
    return {"g32w512": _rope_tables(s, 32, 512), "g64w384": _rope_tables(s, 64, 384),
            "g32w128": _rope_tables(s, 32, 128), "g64w512": _rope_tables(s, 64, 512)}


def kernel(x, mem, norm_mix, w_in, w_out, diff_lambda, diff_subln, nsa_cmp_pos, nsa_cmp_w1, nsa_cmp_w2, mla_q_norm, mla_kv_norm, mla_w_uq, mla_w_ukv, norm_cross, norm_mem, cross_wq, cross_wk, cross_wv, cross_wo, norm_ffn, ffn_w_gate, ffn_w_up, ffn_w_down, moe_router, moe_w_gate, moe_w_up, moe_w_down, final_norm):
    b, s, d = x.shape
    depth = w_in.shape[0]
    tabs = _all_rope_tables(s)
    for l in range(depth):
        x = _token_mixers(x, l, tabs, norm_mix[l], w_in[l], w_out[l], diff_lambda[l], diff_subln[l],
                          nsa_cmp_pos[l], nsa_cmp_w1[l], nsa_cmp_w2[l], mla_q_norm[l], mla_kv_norm[l],
                          mla_w_uq[l], mla_w_ukv[l])
        x = _cross_attention(x, mem, norm_cross[l], norm_mem[l], cross_wq[l], cross_wk[l], cross_wv[l],
                             cross_wo[l])
        xt = x.reshape(b * s, d)
        i = l // 2
        if l % 2 == 0:
            xt = _swiglu(xt, norm_ffn[l], ffn_w_gate[i], ffn_w_up[i], ffn_w_down[i])
        else:
            xt = _moe(xt, norm_ffn[l], moe_router[i], moe_w_gate[i], moe_w_up[i], moe_w_down[i])
        x = xt.reshape(b, s, d)
    return _final_norm(x.reshape(b * s, d), final_norm).reshape(b, s, d)
```

```python
import functools
import math

import numpy as np
import jax
import jax.numpy as jnp
from jax import lax
from jax.experimental import pallas as pl
from jax.experimental.pallas import tpu as pltpu

F32 = jnp.float32
BF16 = jnp.bfloat16

HEAD_DIM = 64
GROUP_HEADS = 4
GROUP_WIDTH = HEAD_DIM * GROUP_HEADS
ROPE_THETA = 10000.0
NORM_EPS = 1e-6
DIFF_DK = HEAD_DIM // 2
CMP_LEN = 32
CMP_STRIDE = 16
SEL_BLOCK = 64
SEL_SHIFT = 6
SEL_TOPN = 16
NSA_WINDOW = 512
FORCED_SCORE = 1e6
MLA_NOPE = 64
MLA_ROPE = 32
MLA_DV = 64
DILATED_PATTERNS = ((128, 1), (512, 4), (2048, 16))
CROSS_HEADS = 4
N_EXPERTS = 8
TOP_K = 2

LANES = 128
V7X_VMEM_BYTES = 64 * 1024 * 1024
VMEM_LIMIT = V7X_VMEM_BYTES * 7 // 8

ONES_ROWS = 16
NEG_BIG = -1e30
SEL_BIAS = -1e9
LOG2E = math.log2(math.e)
LN2 = math.log(2.0)


def _cparams(*sem):
    return pltpu.CompilerParams(dimension_semantics=sem, vmem_limit_bytes=VMEM_LIMIT)


def _rms(xf, gain):
    ms = jnp.mean(xf * xf, axis=-1, keepdims=True)
    return xf * lax.rsqrt(ms + NORM_EPS) * gain


def _mm_body(*refs, nx, norm, res, splits):
    x_refs = refs[:nx]
    pos = nx
    g_ref = refs[pos] if norm else None
    pos += int(norm)
    w_ref = refs[pos]
    pos += 1
    r_ref = refs[pos] if res else None
    pos += int(res)
    o_refs = refs[pos:pos + len(splits)]
    xs_ref = refs[pos + len(splits)]

    @pl.when(pl.program_id(1) == 0)
    def _():
        off = 0
        for xr in x_refs:
            xv = xr[...]
            if norm:
                xv = _rms(xv.astype(F32), g_ref[...])
            kw = xv.shape[-1]
            xs_ref[:, off:off + kw] = xv.astype(BF16)
            off += kw

    acc = jnp.dot(xs_ref[...], w_ref[...], preferred_element_type=F32)
    if res:
        acc = acc + r_ref[...]
    off = 0
    for o_ref, width in zip(o_refs, splits):
        o_ref[...] = acc[:, off:off + width].astype(o_ref.dtype)
        off += width


def _mm(xs, w, *, gain=None, res=None, tm=512, tn=None, splits=None, out_dtype=F32, name="mm"):
    t = xs[0].shape[0]
    k, n = w.shape
    assert sum(x.shape[1] for x in xs) == k
    tn = n if tn is None else tn
    splits = (tn,) if splits is None else tuple(splits)
    assert sum(splits) == tn and (len(splits) == 1 or tn == n)
    tm = min(tm, t)
    assert t % tm == 0 and n % tn == 0
    norm = gain is not None
    assert not norm or len(xs) == 1
    in_specs = [pl.BlockSpec((tm, x.shape[1]), lambda i, j: (i, 0)) for x in xs]
    args = list(xs)
    if norm:
        in_specs.append(pl.BlockSpec((1, k), lambda i, j: (0, 0)))
        args.append(gain.reshape(1, k).astype(F32))
    in_specs.append(pl.BlockSpec((k, tn), lambda i, j: (0, j)))
    args.append(w)
    if res is not None:
        in_specs.append(pl.BlockSpec((tm, tn), lambda i, j: (i, j)))
        args.append(res)
    if len(splits) == 1:
        out_shape = [jax.ShapeDtypeStruct((t, n), out_dtype)]
        out_specs = [pl.BlockSpec((tm, tn), lambda i, j: (i, j))]
    else:
        out_shape = [jax.ShapeDtypeStruct((t, s), out_dtype) for s in splits]
        out_specs = [pl.BlockSpec((tm, s), lambda i, j: (i, 0)) for s in splits]
    outs = pl.pallas_call(
        functools.partial(_mm_body, nx=len(xs), norm=norm, res=res is not None, splits=splits),
        grid=(t // tm, n // tn),
        in_specs=in_specs,
        out_specs=out_specs,
        out_shape=out_shape,
        scratch_shapes=[pltpu.VMEM((tm, k), BF16)],
        compiler_params=_cparams("parallel", "arbitrary"),
        name=name,
    )(*args)
    return outs[0] if len(outs) == 1 else outs


def _rope_tables(s, group, width):
    half = group // 2
    pos = jnp.arange(s, dtype=F32)
    inv_freq = ROPE_THETA ** (-jnp.arange(half, dtype=F32) / half)
    ang = pos[:, None] * inv_freq[None, :]
    cos, sin = jnp.cos(ang), jnp.sin(ang)
    zero = jnp.zeros_like(sin)
    reps = width // group
    c = jnp.tile(jnp.concatenate([cos, cos], -1), (1, reps))
    s1 = jnp.tile(jnp.concatenate([zero, sin], -1), (1, reps))
    s2 = jnp.tile(jnp.concatenate([-sin, zero], -1), (1, reps))
    return c, s1, s2


def _rope(x, c, s1, s2, half):
    w = x.shape[-1]
    return x * c + pltpu.roll(x, half, 1) * s1 + pltpu.roll(x, w - half, 1) * s2


def _prep_a_body(z_ref, c_ref, s1_ref, s2_ref, q_ref, k_ref, vt_ref):
    z = z_ref[0]
    qk = _rope(z[:, :512], c_ref[...], s1_ref[...], s2_ref[...], DIFF_DK // 2)
    lane = lax.broadcasted_iota(jnp.int32, (z.shape[0], HEAD_DIM), 1)
    scale = DIFF_DK ** -0.5 * LOG2E
    for h in range(GROUP_HEADS):
        q = qk[:, h * 64:(h + 1) * 64] * scale
        q_ref[0, h, 0] = jnp.where(lane < DIFF_DK, q, 0.0).astype(BF16)
        q_ref[0, h, 1] = jnp.where(lane >= DIFF_DK, q, 0.0).astype(BF16)
        k_ref[0, h] = qk[:, 256 + h * 64:256 + (h + 1) * 64].astype(BF16)
    vt_ref[0] = z[:, 512:768].T.reshape(vt_ref.shape[1:]).astype(BF16)


def _prep_a(z_a, tabs, ts):
    b, s, _ = z_a.shape
    tspec = pl.BlockSpec((ts, 512), lambda bi, i: (i, 0))
    return pl.pallas_call(
        _prep_a_body,
        grid=(b, s // ts),
        in_specs=[pl.BlockSpec((1, ts, 768), lambda bi, i: (bi, i, 0)), tspec, tspec, tspec],
        out_specs=[pl.BlockSpec((1, 4, 2, ts, 64), lambda bi, i: (bi, 0, 0, i, 0)),
                   pl.BlockSpec((1, 4, ts, 64), lambda bi, i: (bi, 0, i, 0)),
                   pl.BlockSpec((1, 4, 64, ts), lambda bi, i: (bi, 0, 0, i))],
        out_shape=[jax.ShapeDtypeStruct((b, 4, 2, s, 64), BF16),
                   jax.ShapeDtypeStruct((b, 4, s, 64), BF16),
                   jax.ShapeDtypeStruct((b, 4, 64, s), BF16)],
        compiler_params=_cparams("parallel", "parallel"),
        name="prep_a",
    )(z_a, *tabs)


def _prep_b_body(z_ref, c_ref, s1_ref, s2_ref, qn_ref, qr_ref, ka_ref, kw_ref, kc_ref, vc_ref,
                 vs_ref, vw_ref, g_ref, *, ts):
    z = z_ref[0]
    scale = HEAD_DIM ** -0.5 * LOG2E
    rp = _rope(z[:, :384], c_ref[...], s1_ref[...], s2_ref[...], HEAD_DIM // 2)
    for h in range(GROUP_HEADS):
        qn_ref[0, h] = (z[:, h * 64:(h + 1) * 64] * scale).astype(BF16)
        qr_ref[0, h] = (rp[:, h * 64:(h + 1) * 64] * scale).astype(BF16)
    kpos = pl.program_id(1) * ts + lax.broadcasted_iota(jnp.int32, (ts, SEL_BLOCK), 0)
    lane = lax.broadcasted_iota(jnp.int32, (ts, SEL_BLOCK), 1)
    onehot = jnp.where((kpos >> SEL_SHIFT) == lane, 1.0, 0.0)
    ka_ref[0] = jnp.concatenate([rp[:, 256:320], onehot], axis=-1).astype(BF16)
    kw_ref[0] = rp[:, 320:384].astype(BF16)
    kc_ref[0] = z[:, 384:448].astype(BF16)
    vc_ref[0] = z[:, 448:512].astype(BF16)
    vt = z[:, 512:640].T.astype(BF16)
    vs_ref[0] = vt[:HEAD_DIM]
    vw_ref[0] = vt[HEAD_DIM:]
    g_ref[0] = jax.nn.sigmoid(z[:, 640:768])


def _prep_b(z_b, tabs, ts):
    b, s, _ = z_b.shape
    tspec = pl.BlockSpec((ts, 384), lambda bi, i: (i, 0))
    h4 = pl.BlockSpec((1, 4, ts, 64), lambda bi, i: (bi, 0, i, 0))
    s64 = pl.BlockSpec((1, ts, 64), lambda bi, i: (bi, i, 0))
    s128 = pl.BlockSpec((1, ts, 128), lambda bi, i: (bi, i, 0))
    t64 = pl.BlockSpec((1, 64, ts), lambda bi, i: (bi, 0, i))
    sd = lambda *shape, dt=BF16: jax.ShapeDtypeStruct(shape, dt)
    return pl.pallas_call(
        functools.partial(_prep_b_body, ts=ts),
        grid=(b, s // ts),
        in_specs=[pl.BlockSpec((1, ts, 768), lambda bi, i: (bi, i, 0)), tspec, tspec, tspec],
        out_specs=[h4, h4, s128, s64, s64, s64, t64, t64, s128],
        out_shape=[sd(b, 4, s, 64), sd(b, 4, s, 64), sd(b, s, 128), sd(b, s, 64), sd(b, s, 64),
                   sd(b, s, 64), sd(b, 64, s), sd(b, 64, s), sd(b, s, 128, dt=F32)],
        compiler_params=_cparams("parallel", "parallel"),
        name="prep_b",
    )(z_b, *tabs)


def _prep_c_body(z_ref, c_ref, s1_ref, s2_ref, qg_ref, kg_ref, wq_ref, wkv_ref, q_ref, k_ref, vt_ref):
    z = z_ref[0]
    scale = (MLA_NOPE + MLA_ROPE) ** -0.5 * LOG2E
    cq = _rms(z[:, :384], qg_ref[...]).astype(BF16)
    ckv = _rms(z[:, 384:512], kg_ref[...]).astype(BF16)
    qc = jnp.dot(cq, wq_ref[...], preferred_element_type=F32)
    kvc = jnp.dot(ckv, wkv_ref[...], preferred_element_type=F32)
    half = MLA_ROPE // 2
    c, s1, s2 = c_ref[...], s1_ref[...], s2_ref[...]
    q_rope = _rope(qc[:, 256:384], c, s1, s2, half)
    k_rope = _rope(z[:, 512:640], c, s1, s2, half)[:, :MLA_ROPE]
    for h in range(GROUP_HEADS):
        q = jnp.concatenate([qc[:, h * 64:(h + 1) * 64], q_rope[:, h * 32:(h + 1) * 32]], axis=-1)
        q_ref[0, h] = (q * scale).astype(BF16)
        k_ref[0, h] = jnp.concatenate([kvc[:, h * 64:(h + 1) * 64], k_rope], axis=-1).astype(BF16)
    vt_ref[0] = kvc[:, 256:512].T.reshape(vt_ref.shape[1:]).astype(BF16)


def _prep_c(z_c, tabs, q_gain, kv_gain, wq, wkv, ts):
    b, s, _ = z_c.shape
    tspec = pl.BlockSpec((ts, 128), lambda bi, i: (i, 0))
    full = lambda a: pl.BlockSpec(a.shape, lambda bi, i: (0,) * a.ndim)
    dk = MLA_NOPE + MLA_ROPE
    return pl.pallas_call(
        _prep_c_body,
        grid=(b, s // ts),
        in_specs=[pl.BlockSpec((1, ts, 640), lambda bi, i: (bi, i, 0)), tspec, tspec, tspec,
                  full(q_gain), full(kv_gain), full(wq), full(wkv)],
        out_specs=[pl.BlockSpec((1, 4, ts, dk), lambda bi, i: (bi, 0, i, 0)),
                   pl.BlockSpec((1, 4, ts, dk), lambda bi, i: (bi, 0, i, 0)),
                   pl.BlockSpec((1, 4, 64, ts), lambda bi, i: (bi, 0, 0, i))],
        out_shape=[jax.ShapeDtypeStruct((b, 4, s, dk), BF16),
                   jax.ShapeDtypeStruct((b, 4, s, dk), BF16),
                   jax.ShapeDtypeStruct((b, 4, 64, s), BF16)],
        compiler_params=_cparams("parallel", "parallel"),
        name="prep_c",
    )(z_c, *tabs, q_gain, kv_gain, wq, wkv)


def _prep_d_body(z_ref, c_ref, s1_ref, s2_ref, q_ref, k_ref, vt_ref):
    z = z_ref[0]
    scale = HEAD_DIM ** -0.5 * LOG2E
    qk = _rope(z[:, :512], c_ref[...], s1_ref[...], s2_ref[...], HEAD_DIM // 2)
    for h in range(GROUP_HEADS):
        q_ref[0, h] = (qk[:, h * 64:(h + 1) * 64] * scale).astype(BF16)
        k_ref[0, h] = qk[:, 256 + h * 64:256 + (h + 1) * 64].astype(BF16)
    vt_ref[0] = z[:, 512:768].T.reshape(vt_ref.shape[1:]).astype(BF16)


def _prep_d(z_d, tabs, ts):
    b, s, _ = z_d.shape
    tspec = pl.BlockSpec((ts, 512), lambda bi, i: (i, 0))
    h4 = pl.BlockSpec((1, 4, ts, 64), lambda bi, i: (bi, 0, i, 0))
    return pl.pallas_call(
        _prep_d_body,
        grid=(b, s // ts),
        in_specs=[pl.BlockSpec((1, ts, 768), lambda bi, i: (bi, i, 0)), tspec, tspec, tspec],
        out_specs=[h4, h4, pl.BlockSpec((1, 4, 64, ts), lambda bi, i: (bi, 0, 0, i))],
        out_shape=[jax.ShapeDtypeStruct((b, 4, s, 64), BF16)] * 2 + [jax.ShapeDtypeStruct((b, 4, 64, s), BF16)],
        compiler_params=_cparams("parallel", "parallel"),
        name="prep_d",
    )(z_d, *tabs)


def _flash_schedule(nq, tq, tk, sk, window):
    qi, kj, fl = [], [], []
    for i in range(nq):
        q_lo, q_hi = i * tq, i * tq + tq - 1
        j_hi = min(q_hi, sk - 1) // tk
        j_lo = 0 if window is None else max(0, q_lo - window) // tk
        for j in range(j_lo, j_hi + 1):
            k_min, k_max = j * tk, j * tk + tk - 1
            full = k_max <= q_lo and (window is None or q_hi - k_min <= window)
            qi.append(i)
            kj.append(j)
            fl.append((1 if j == j_lo else 0) | (2 if j == j_hi else 0) | (0 if full else 4))
    return (np.asarray(qi, np.int32), np.asarray(kj, np.int32), np.asarray(fl, np.int32))


def _flash_body(qi_ref, kj_ref, fl_ref, q_ref, k_ref, vt_ref, o_ref, m_sc, acc_sc, *,
                gb, hq, tq, tk, window, with_lse):
    t = pl.program_id(1)
    fl = fl_ref[t]
    rows = hq * tq
    dv = vt_ref.shape[1]

    @pl.when((fl & 1) != 0)
    def _():
        m_sc[...] = jnp.full(m_sc.shape, NEG_BIG, F32)
        acc_sc[...] = jnp.zeros(acc_sc.shape, F32)

    def step(masked):
        q = q_ref[...].reshape(gb, rows, q_ref.shape[-1])
        st = jnp.einsum("gkd,gqd->gkq", k_ref[...], q, preferred_element_type=F32)
        if masked:
            krow = lax.broadcasted_iota(jnp.int32, (tk, rows), 0)
            qcol = lax.broadcasted_iota(jnp.int32, (tk, rows), 1)
            if hq > 1:
                qcol = qcol & (tq - 1)
            dist = (qi_ref[t] * tq - kj_ref[t] * tk) + qcol - krow
            ok = dist >= 0
            if window is not None:
                ok = ok & (dist <= window)
            st = jnp.where(ok[None], st, NEG_BIG)
        m_prev = m_sc[...]
        m_new = jnp.maximum(m_prev, jnp.max(st, axis=1, keepdims=True))
        alpha = jnp.exp2(m_prev - m_new)
        p = jnp.exp2((st - m_new).astype(BF16))
        v_ext = jnp.concatenate([vt_ref[...], jnp.ones((gb, ONES_ROWS, tk), BF16)], axis=1)
        acc_sc[...] = alpha * acc_sc[...] + jnp.einsum("gdk,gkq->gdq", v_ext, p, preferred_element_type=F32)
        m_sc[...] = m_new

    @pl.when((fl & 4) != 0)
    def _():
        step(True)

    @pl.when((fl & 4) == 0)
    def _():
        step(False)

    @pl.when((fl & 2) != 0)
    def _():
        l = acc_sc[:, dv:dv + 1, :]
        o_t = acc_sc[:, :dv, :] / l
        if with_lse:
            extra = jnp.broadcast_to(m_sc[...] * LN2 + jnp.log(l), o_t.shape)
        else:
            extra = jnp.zeros((gb, LANES - dv, rows), F32)
        dvo = o_ref.shape[-1]
        for g in range(gb):
            o = jnp.concatenate([o_t[g], extra[g]], axis=0).T
            o_ref[g] = o[:, :dvo].reshape(o_ref.shape[1:]).astype(o_ref.dtype)


def _flash(q, k, vt, *, tq, tk, gb=1, window=None, with_lse=False, name="flash"):
    g, hq, sq, dk = q.shape
    _, dv, sk = vt.shape
    tq, tk = min(tq, sq), min(tk, sk)
    assert g % gb == 0 and sq % tq == 0 and sk % tk == 0
    assert tq & (tq - 1) == 0 and 2 * dv == LANES
    qi, kj, fl = _flash_schedule(sq // tq, tq, tk, sk, window)
    dvo = 2 * dv if with_lse else dv
    rows = hq * tq
    grid_spec = pltpu.PrefetchScalarGridSpec(
        num_scalar_prefetch=3,
        grid=(g // gb, len(qi)),
        in_specs=[pl.BlockSpec((gb, hq, tq, dk), lambda gi, t, qi, kj, fl: (gi, 0, qi[t], 0)),
                  pl.BlockSpec((gb, tk, dk), lambda gi, t, qi, kj, fl: (gi, kj[t], 0)),
                  pl.BlockSpec((gb, dv, tk), lambda gi, t, qi, kj, fl: (gi, 0, kj[t]))],
        out_specs=pl.BlockSpec((gb, hq, tq, dvo), lambda gi, t, qi, kj, fl: (gi, 0, qi[t], 0)),
        scratch_shapes=[pltpu.VMEM((gb, 1, rows), F32), pltpu.VMEM((gb, dv + ONES_ROWS, rows), F32)],
    )
    return pl.pallas_call(
        functools.partial(_flash_body, gb=gb, hq=hq, tq=tq, tk=tk, window=window, with_lse=with_lse),
        grid_spec=grid_spec,
        out_shape=jax.ShapeDtypeStruct((g, hq, sq, dvo), F32),
        compiler_params=_cparams("parallel", "arbitrary"),
        name=name,
    )(jnp.asarray(qi), jnp.asarray(kj), jnp.asarray(fl), q, k, vt)


def _gelu_tanh(x):
    return 0.5 * x * (1.0 + jnp.tanh(math.sqrt(2.0 / math.pi) * (x + 0.044715 * (x * x * x))))


def _compress_body(r_ref, pos_ref, w1_ref, w2_ref, o_ref):
    half = CMP_STRIDE * HEAD_DIM
    r = r_ref[0, 0]
    w1 = w1_ref[0]
    top = jnp.dot(r, w1[:half], preferred_element_type=F32)
    bot = jnp.dot(r, w1[half:], preferred_element_type=F32)
    nc = r.shape[0]
    posb = jnp.dot(pos_ref[0], w1, preferred_element_type=F32)[:1]
    pre = top + pltpu.roll(bot, nc - 1, 0) + posb
    hid = _gelu_tanh(pre).astype(BF16)
    o_ref[0, 0] = jnp.dot(hid, w2_ref[0], preferred_element_type=F32).astype(o_ref.dtype)


def _nsa_compress(kv_rows, pos_flat, w1, w2):
    _, b, nc, _ = kv_rows.shape
    return pl.pallas_call(
        _compress_body,
        grid=(2, b),
        in_specs=[pl.BlockSpec((1, 1, nc, 1024), lambda w, bi: (w, bi, 0, 0)),
                  pl.BlockSpec((1, 8, 2048), lambda w, bi: (w, 0, 0)),
                  pl.BlockSpec((1, 2048, 256), lambda w, bi: (w, 0, 0)),
                  pl.BlockSpec((1, 256, 64), lambda w, bi: (w, 0, 0))],
        out_specs=pl.BlockSpec((1, 1, nc, 64), lambda w, bi: (w, bi, 0, 0)),
        out_shape=jax.ShapeDtypeStruct((2, b, nc, 64), BF16),
        compiler_params=_cparams("parallel", "parallel"),
        name="nsa_compress",
    )(kv_rows, pos_flat, w1, w2)


def _cmp_select_body(qn_ref, qr_ref, kc_ref, vc_ref, cov_ref, o_ref, qa_ref, *, tq, n_sel, k_top):
    ncp = kc_ref.shape[2]
    pos = pl.program_id(1) * tq + lax.broadcasted_iota(jnp.int32, (tq, ncp), 0)
    blk_end = lax.broadcasted_iota(jnp.int32, (tq, ncp), 1) * CMP_STRIDE + (CMP_LEN - 1)
    mask = blk_end <= pos
    kc, vc = kc_ref[0, 0], vc_ref[0, 0]
    p_sum = jnp.zeros((tq, ncp), F32)
    for h in range(GROUP_HEADS):
        s = lax.dot_general(qn_ref[0, h], kc, (((1,), (1,)), ((), ())), preferred_element_type=F32)
        s = jnp.where(mask, s, NEG_BIG)
        mx = jnp.max(s, axis=-1, keepdims=True)
        mx = jnp.where(mx > 0.5 * NEG_BIG, mx, 0.0)
        e = jnp.where(mask, jnp.exp2(s - mx), 0.0)
        p = e / jnp.maximum(jnp.sum(e, axis=-1, keepdims=True), 1e-30)
        o_ref[0, h] = jnp.dot(p.astype(BF16), vc, preferred_element_type=F32)
        p_sum = p_sum + p
    p_hi = p_sum.astype(BF16)
    p_lo = (p_sum - p_hi.astype(F32)).astype(BF16)
    cov = cov_ref[...]
    imp = (jnp.dot(p_hi, cov, preferred_element_type=F32) + jnp.dot(p_lo, cov, preferred_element_type=F32))
    nblk = SEL_BLOCK
    imp_t = imp.T[:nblk]
    tpos = pl.program_id(1) * tq + lax.broadcasted_iota(jnp.int32, (nblk, tq), 1)
    cur = tpos >> SEL_SHIFT
    j = lax.broadcasted_iota(jnp.int32, (nblk, tq), 0)
    valid = j <= cur
    forced = (j == 0) | (j == cur) | (j == cur - 1)
    score = jnp.where(valid, jnp.where(forced, FORCED_SCORE, imp_t), -jnp.inf)
    sub = 8
    groups = [score[g * sub:(g + 1) * sub] for g in range(nblk // sub)]
    ranks = [jnp.zeros((sub, tq), jnp.int32) for _ in groups]
    jj = lax.broadcasted_iota(jnp.int32, (sub, tq), 0)
    for i in range(n_sel):
        row = jnp.broadcast_to(score[i:i + 1], (sub, tq))
        for g, sg in enumerate(groups):
            if i < g * sub:
                ahead = row >= sg
            elif i >= (g + 1) * sub:
                ahead = row > sg
            else:
                ahead = (row > sg) | ((row == sg) & (i - g * sub < jj))
            ranks[g] = ranks[g] + ahead.astype(jnp.int32)
    rank = jnp.concatenate(ranks, axis=0)
    bias_t = jnp.where(valid & (rank < k_top), 0.0, SEL_BIAS)
    bias = jnp.concatenate([bias_t, jnp.zeros((LANES - nblk, tq), F32)], axis=0).T[:, :nblk].astype(BF16)
    for h in range(GROUP_HEADS):
        qa_ref[0, h] = jnp.concatenate([qr_ref[0, h], bias], axis=-1)


def _cover_matrix(ncp, n_sel, lanes):
    c_start = np.arange(ncp)[:, None] * CMP_STRIDE
    s_start = np.arange(lanes)[None, :] * SEL_BLOCK
    cov = np.clip(np.minimum(c_start + CMP_LEN, s_start + SEL_BLOCK) - np.maximum(c_start, s_start), 0, None)
    cov = cov.astype(np.float32) / CMP_LEN
    cov[:, n_sel:] = 0.0
    cov[ncp - 1:, :] = 0.0
    return cov


def _cmp_select(qn, qr, kvc, tq):
    b, _, s, _ = qn.shape
    ncp = kvc.shape[2]
    n_sel = s // SEL_BLOCK
    assert n_sel <= SEL_BLOCK
    k_top = min(SEL_TOPN, n_sel)
    cov = jnp.asarray(_cover_matrix(ncp, n_sel, LANES), BF16)
    h4 = lambda d: pl.BlockSpec((1, 4, tq, d), lambda bi, i: (bi, 0, i, 0))
    return pl.pallas_call(
        functools.partial(_cmp_select_body, tq=tq, n_sel=n_sel, k_top=k_top),
        grid=(b, s // tq),
        in_specs=[h4(64), h4(64),
                  pl.BlockSpec((1, 1, ncp, 64), lambda bi, i: (0, bi, 0, 0)),
                  pl.BlockSpec((1, 1, ncp, 64), lambda bi, i: (1, bi, 0, 0)),
                  pl.BlockSpec(cov.shape, lambda bi, i: (0, 0))],
        out_specs=[h4(64), h4(128)],
        out_shape=[jax.ShapeDtypeStruct((b, 4, s, 64), F32), jax.ShapeDtypeStruct((b, 4, s, 128), BF16)],
        compiler_params=_cparams("parallel", "parallel"),
        name="nsa_cmp_select",
    )(qn, qr, kvc, kvc, cov)


def _merge_a_body(o_ref, lam_ref, g_ref, y_ref, *, out_scale):
    outs = []
    for h in range(GROUP_HEADS):
        o = o_ref[0, h, 0] - lam_ref[...] * o_ref[0, h, 1]
        outs.append(_rms(o, g_ref[...]) * out_scale)
    y_ref[0] = jnp.concatenate(outs, axis=-1).astype(y_ref.dtype)


def _merge_a(o, lam, subln, out_scale, ts):
    b, _, _, s, _ = o.shape
    return pl.pallas_call(
        functools.partial(_merge_a_body, out_scale=out_scale),
        grid=(b, s // ts),
        in_specs=[pl.BlockSpec((1, 4, 2, ts, 64), lambda bi, i: (bi, 0, 0, i, 0)),
                  pl.BlockSpec((1, 64), lambda bi, i: (0, 0)),
                  pl.BlockSpec((1, 64), lambda bi, i: (0, 0))],
        out_specs=pl.BlockSpec((1, ts, 256), lambda bi, i: (bi, i, 0)),
        out_shape=jax.ShapeDtypeStruct((b, s, 256), BF16),
        compiler_params=_cparams("parallel", "parallel"),
        name="merge_a",
    )(o, lam, subln)


def _merge_b_body(oc_ref, os_ref, ow_ref, g_ref, y_ref):
    g = g_ref[0]
    outs = []
    for h in range(GROUP_HEADS):
        outs.append(g[:, 3 * h:3 * h + 1] * oc_ref[0, h] + g[:, 3 * h + 1:3 * h + 2] * os_ref[0, h]
                    + g[:, 3 * h + 2:3 * h + 3] * ow_ref[0, h])
    y_ref[0] = jnp.concatenate(outs, axis=-1).astype(y_ref.dtype)


def _merge_b(o_cmp, o_sel, o_win, gates, ts):
    b, _, s, _ = o_cmp.shape
    h4 = pl.BlockSpec((1, 4, ts, 64), lambda bi, i: (bi, 0, i, 0))
    return pl.pallas_call(
        _merge_b_body,
        grid=(b, s // ts),
        in_specs=[h4, h4, h4, pl.BlockSpec((1, ts, 128), lambda bi, i: (bi, i, 0))],
        out_specs=pl.BlockSpec((1, ts, 256), lambda bi, i: (bi, i, 0)),
        out_shape=jax.ShapeDtypeStruct((b, s, 256), BF16),
        compiler_params=_cparams("parallel", "parallel"),
        name="merge_b",
    )(o_cmp, o_sel, o_win, gates)


def _merge_c_body(o_ref, y_ref):
    y_ref[0] = jnp.concatenate([o_ref[0, h] for h in range(GROUP_HEADS)], axis=-1).astype(y_ref.dtype)


def _merge_c(o, ts):
    b, _, s, _ = o.shape
    return pl.pallas_call(
        _merge_c_body,
        grid=(b, s // ts),
        in_specs=[pl.BlockSpec((1, 4, ts, 64), lambda bi, i: (bi, 0, i, 0))],
        out_specs=pl.BlockSpec((1, ts, 256), lambda bi, i: (bi, i, 0)),
        out_shape=jax.ShapeDtypeStruct((b, s, 256), BF16),
        compiler_params=_cparams("parallel", "parallel"),
        name="merge_c",
    )(o)


def _merge_d_body(o1_ref, o2_ref, o3_ref, y_ref):
    outs = []
    for h in range(GROUP_HEADS):
        e = [r[0, h] for r in (o1_ref, o2_ref, o3_ref)]
        lse = [x[:, 64:] for x in e]
        mx = jnp.maximum(jnp.maximum(lse[0], lse[1]), lse[2])
        w = [jnp.exp(x - mx) for x in lse]
        den = w[0] + w[1] + w[2]
        outs.append((w[0] * e[0][:, :64] + w[1] * e[1][:, :64] + w[2] * e[2][:, :64]) / den)
    y_ref[0] = jnp.concatenate(outs, axis=-1).astype(y_ref.dtype)


def _merge_d(o1, o2, o3, ts):
    b, _, s, _ = o1.shape
    h4 = pl.BlockSpec((1, 4, ts, 128), lambda bi, i: (bi, 0, i, 0))
    return pl.pallas_call(
        _merge_d_body,
        grid=(b, s // ts),
        in_specs=[h4, h4, h4],
        out_specs=pl.BlockSpec((1, ts, 256), lambda bi, i: (bi, i, 0)),
        out_shape=jax.ShapeDtypeStruct((b, s, 256), BF16),
        compiler_params=_cparams("parallel", "parallel"),
        name="merge_d",
    )(o1, o2, o3)


def _permute_w_in(w_in):
    a_cols = 768
    b0 = a_cols
    nq = 256
    seg = lambda i: w_in[:, b0 + nq + i * 64: b0 + nq + (i + 1) * 64]
    kc, vc, ks, vs, kw, vw = (seg(i) for i in range(6))
    gb = w_in[:, b0 + nq + 384: b0 + nq + 384 + 12]
    d = w_in.shape[0]
    w_b = jnp.concatenate([w_in[:, b0:b0 + nq], ks, kw, kc, vc, vs, vw, gb, jnp.zeros((d, 116), w_in.dtype)], axis=1)
    c0 = b0 + 652
    w_c = jnp.concatenate([w_in[:, c0:c0 + 544], jnp.zeros((d, 96), w_in.dtype)], axis=1)
    d0 = c0 + 544
    return jnp.concatenate([w_in[:, :a_cols], w_b, w_c, w_in[:, d0:d0 + 768]], axis=1).astype(BF16)


def _residue_classes(t, dil):
    g, s, d = t.shape
    return t.reshape(g, s // dil, dil, d).transpose(0, 2, 1, 3).reshape(g * dil, s // dil, d)


def _residue_classes_t(t, dil):
    g, d, s = t.shape
    return t.reshape(g, d, s // dil, dil).transpose(0, 3, 1, 2).reshape(g * dil, d, s // dil)


def _from_residue_classes(t, dil):
    gd, l, d = t.shape
    return t.reshape(gd // dil, dil, l, d).transpose(0, 2, 1, 3).reshape(gd // dil, l * dil, d)


def _token_mixers(x, layer, rope_tabs, norm_g, w_in, w_out, diff_lambda, diff_subln, cmp_pos, cmp_w1, cmp_w2,
                  q_norm, kv_norm, w_uq, w_ukv):
    b, s, d = x.shape
    t = b * s
    ts = min(512, s)
    z_a, z_b, z_c, z_d = _mm([x.reshape(t, d)], _permute_w_in(w_in), gain=norm_g, tm=min(512, t),
                             splits=(768, 768, 640, 768), name="w_in")
    z_a, z_b, z_c, z_d = (z.reshape(b, s, -1) for z in (z_a, z_b, z_c, z_d))

    qa, ka, va = _prep_a(z_a, rope_tabs["g32w512"], ts)
    oa = _flash(qa.reshape(b * 4, 2, s, 64), ka.reshape(b * 4, s, 64), va.reshape(b * 4, 64, s),
                tq=512, tk=512, gb=2, name="flash_diff").reshape(b, 4, 2, s, 64)
    lam_init = 0.8 - 0.6 * math.exp(-0.3 * layer)
    lp = diff_lambda.astype(F32)
    lam = jnp.exp(jnp.sum(lp[0] * lp[1])) - jnp.exp(jnp.sum(lp[2] * lp[3])) + lam_init
    y_a = _merge_a(oa, jnp.full((1, 64), lam, F32), diff_subln.reshape(1, 64).astype(F32), 1.0 - lam_init, ts)

    qn, qr, k_aug, kw, kc, vc, vs, vw, gates = _prep_b(z_b, rope_tabs["g64w384"], ts)
    kv_rows = jnp.stack([kc, vc]).reshape(2, b, s // CMP_STRIDE, CMP_STRIDE * HEAD_DIM)
    pos_flat = jnp.broadcast_to(cmp_pos.reshape(2, 1, CMP_LEN * HEAD_DIM), (2, 8, CMP_LEN * HEAD_DIM)).astype(BF16)
    kvc = _nsa_compress(kv_rows, pos_flat, cmp_w1.astype(BF16), cmp_w2.astype(BF16))
    o_cmp, q_aug = _cmp_select(qn, qr, kvc, min(256, s))
    o_sel = _flash(q_aug, k_aug, vs, tq=256, tk=512, name="flash_sel")
    o_win = _flash(qr, kw, vw, tq=256, tk=256, window=NSA_WINDOW - 1, name="flash_win")
    y_b = _merge_b(o_cmp, o_sel, o_win, gates, ts)

    wq = w_uq.reshape(-1, 4, MLA_NOPE + MLA_ROPE)
    wq = jnp.concatenate([wq[:, :, :MLA_NOPE].reshape(-1, 256), wq[:, :, MLA_NOPE:].reshape(-1, 128)], axis=1)
    wkv = w_ukv.reshape(-1, 4, MLA_NOPE + MLA_DV)
    wkv = jnp.concatenate([wkv[:, :, :MLA_NOPE].reshape(-1, 256), wkv[:, :, MLA_NOPE:].reshape(-1, 256)], axis=1)
    qc, kcat, vcv = _prep_c(z_c, rope_tabs["g32w128"], q_norm.reshape(1, -1).astype(F32),
                            kv_norm.reshape(1, -1).astype(F32), wq.astype(BF16), wkv.astype(BF16), ts)
    dk = MLA_NOPE + MLA_ROPE
    oc = _flash(qc.reshape(b * 4, 1, s, dk), kcat.reshape(b * 4, s, dk), vcv.reshape(b * 4, 64, s),
                tq=512, tk=512, gb=4, name="flash_mla").reshape(b, 4, s, 64)
    y_c = _merge_c(oc, ts)

    qd, kd, vd = _prep_d(z_d, rope_tabs["g64w512"], ts)
    qd, kd, vd = qd.reshape(b * 4, s, 64), kd.reshape(b * 4, s, 64), vd.reshape(b * 4, 64, s)
    o_d = []
    for window, dil in DILATED_PATTERNS:
        l = s // dil
        qq, kk, vv = _residue_classes(qd, dil), _residue_classes(kd, dil), _residue_classes_t(vd, dil)
        g = qq.shape[0]
        gb = math.gcd(g, 16)
        o = _flash(qq.reshape(g, 1, l, 64), kk, vv, tq=128, tk=128, gb=gb, window=window // dil,
                   with_lse=True, name=f"flash_dil{dil}")
        o_d.append(_from_residue_classes(o.reshape(g, l, 128), dil).reshape(b, 4, s, 128))
    y_d = _merge_d(*o_d, ts)

    ys = [y.reshape(t, 256) for y in (y_a, y_b, y_c, y_d)]
    return _mm(ys, w_out.astype(BF16), res=x.reshape(t, d), tm=min(512, t), name="w_out").reshape(b, s, d)


def _cross_body(x_ref, g_ref, wq_ref, k_ref, v_ref, wo_ref, o_ref):
    x = x_ref[0]
    h = _rms(x, g_ref[...]).astype(BF16)
    q = jnp.dot(h, wq_ref[...], preferred_element_type=F32).astype(BF16)
    dh = q.shape[-1] // CROSS_HEADS
    outs = []
    for hd in range(CROSS_HEADS):
        sl = slice(hd * dh, (hd + 1) * dh)
        s = lax.dot_general(q[:, sl], k_ref[0, :, sl], (((1,), (1,)), ((), ())), preferred_element_type=F32)
        e = jnp.exp(s - jnp.max(s, axis=-1, keepdims=True))
        p = e / jnp.sum(e, axis=-1, keepdims=True)
        outs.append(jnp.dot(p.astype(BF16), v_ref[0, :, sl], preferred_element_type=F32))
    o = jnp.concatenate(outs, axis=-1).astype(BF16)
    o_ref[0] = x + jnp.dot(o, wo_ref[...], preferred_element_type=F32)


def _cross_attention(x, mem, g_x, g_mem, wq, wk, wv, wo):
    b, s, d = x.shape
    m = mem.shape[1]
    ts = min(512, s)
    dh = d // CROSS_HEADS
    k, v = _mm([mem.reshape(b * m, d)], jnp.concatenate([wk, wv], axis=1).astype(BF16), gain=g_mem,
               tm=min(512, b * m), splits=(d, d), out_dtype=BF16, name="cross_kv")
    full = lambda a: pl.BlockSpec(a.shape, lambda bi, i: (0,) * a.ndim)
    wq_s = (wq * dh ** -0.5).astype(BF16)
    wo_b = wo.astype(BF16)
    g2 = g_x.reshape(1, d).astype(F32)
    return pl.pallas_call(
        _cross_body,
        grid=(b, s // ts),
        in_specs=[pl.BlockSpec((1, ts, d), lambda bi, i: (bi, i, 0)), full(g2), full(wq_s),
                  pl.BlockSpec((1, m, d), lambda bi, i: (bi, 0, 0)),
                  pl.BlockSpec((1, m, d), lambda bi, i: (bi, 0, 0)), full(wo_b)],
        out_specs=pl.BlockSpec((1, ts, d), lambda bi, i: (bi, i, 0)),
        out_shape=jax.ShapeDtypeStruct((b, s, d), F32),
        compiler_params=_cparams("parallel", "parallel"),
        name="cross_attn",
    )(x, g2, wq_s, k.reshape(b, m, d), v.reshape(b, m, d), wo_b)


def _swiglu_body(x_ref, g_ref, wg_ref, wu_ref, wd_ref, o_ref, h_sc, acc_sc):
    f = pl.program_id(1)

    @pl.when(f == 0)
    def _():
        h_sc[...] = _rms(x_ref[...], g_ref[...]).astype(BF16)
        acc_sc[...] = x_ref[...]

    h = h_sc[...]
    gate = jnp.dot(h, wg_ref[...], preferred_element_type=F32)
    up = jnp.dot(h, wu_ref[...], preferred_element_type=F32)
    act = (gate * jax.nn.sigmoid(gate) * up).astype(BF16)
    acc_sc[...] += jnp.dot(act, wd_ref[...], preferred_element_type=F32)

    @pl.when(f == pl.num_programs(1) - 1)
    def _():
        o_ref[...] = acc_sc[...]


def _swiglu(x, gain, wg, wu, wd, tm=512, tf=1408):
    t, d = x.shape
    ff = wg.shape[1]
    tm, tf = min(tm, t), min(tf, ff)
    assert t % tm == 0 and ff % tf == 0
    return pl.pallas_call(
        _swiglu_body,
        grid=(t // tm, ff // tf),
        in_specs=[pl.BlockSpec((tm, d), lambda i, f: (i, 0)),
                  pl.BlockSpec((1, d), lambda i, f: (0, 0)),
                  pl.BlockSpec((d, tf), lambda i, f: (0, f)),
                  pl.BlockSpec((d, tf), lambda i, f: (0, f)),
                  pl.BlockSpec((tf, d), lambda i, f: (f, 0))],
        out_specs=pl.BlockSpec((tm, d), lambda i, f: (i, 0)),
        out_shape=jax.ShapeDtypeStruct((t, d), F32),
        scratch_shapes=[pltpu.VMEM((tm, d), BF16), pltpu.VMEM((tm, d), F32)],
        compiler_params=_cparams("parallel", "arbitrary"),
        name="swiglu",
    )(x, gain.reshape(1, d).astype(F32), wg.astype(BF16), wu.astype(BF16), wd.astype(BF16))


def _router_body(x_ref, g_ref, r_ref, h_ref, rt_ref):
    h = _rms(x_ref[...], g_ref[...])
    h_ref[...] = h
    h_hi = h.astype(BF16)
    h_lo = (h - h_hi.astype(F32)).astype(BF16)
    r = r_ref[...]
    r_hi = r.astype(BF16)
    r_lo = (r - r_hi.astype(F32)).astype(BF16)
    logits = (jnp.dot(h_hi, r_hi, preferred_element_type=F32) + jnp.dot(h_lo, r_hi, preferred_element_type=F32)
              + jnp.dot(h_hi, r_lo, preferred_element_type=F32))
    lane = lax.broadcasted_iota(jnp.int32, logits.shape, 1)
    lg = jnp.where(lane < N_EXPERTS, logits, -jnp.inf)
    v0 = jnp.max(lg, axis=-1, keepdims=True)
    e0 = jnp.min(jnp.where(lg == v0, lane, LANES), axis=-1, keepdims=True)
    lg1 = jnp.where(lane == e0, -jnp.inf, lg)
    v1 = jnp.max(lg1, axis=-1, keepdims=True)
    e1 = jnp.min(jnp.where(lg1 == v1, lane, LANES), axis=-1, keepdims=True)
    g1 = 1.0 / (1.0 + jnp.exp(v0 - v1))
    g0 = 1.0 - g1
    out = jnp.where(lane == 0, g0, jnp.where(lane == 1, g1, 0.0))
    out = jnp.where(lane == 2, e0.astype(F32), jnp.where(lane == 3, e1.astype(F32), out))
    rt_ref[...] = out


def _router(x, gain, router, tm=512):
    t, d = x.shape
    tm = min(tm, t)
    r_pad = jnp.zeros((d, LANES), F32).at[:, :N_EXPERTS].set(router.astype(F32))
    return pl.pallas_call(
        _router_body,
        grid=(t // tm,),
        in_specs=[pl.BlockSpec((tm, d), lambda i: (i, 0)), pl.BlockSpec((1, d), lambda i: (0, 0)),
                  pl.BlockSpec((d, LANES), lambda i: (0, 0))],
        out_specs=[pl.BlockSpec((tm, d), lambda i: (i, 0)), pl.BlockSpec((tm, LANES), lambda i: (i, 0))],
        out_shape=[jax.ShapeDtypeStruct((t, d), F32), jax.ShapeDtypeStruct((t, LANES), F32)],
        compiler_params=_cparams("parallel"),
        name="moe_router",
    )(x, gain.reshape(1, d).astype(F32), r_pad)


def _row_copy(src_hbm, src_row, dst_ref, dst_row, sem):
    return pltpu.make_async_copy(src_hbm.at[pl.ds(src_row, 1)], dst_ref.at[pl.ds(dst_row, 1)], sem)


def _experts_body(be_ref, nb_ref, tok_ref, h_hbm, wg_ref, wu_ref, wd_ref, o_ref, xf_sc, xb_sc, acc_sc, sem, *,
                  n_blk, n_f):
    i, f = pl.program_id(0), pl.program_id(1)
    rows = xf_sc.shape[1]
    part = rows // n_f
    slot = i & 1
    nb = nb_ref[0]

    @pl.when((i == 0) & (f == 0))
    def _():
        for r in range(rows):
            _row_copy(h_hbm, tok_ref[r], xf_sc.at[0], r, sem.at[0]).start()

    @pl.when((f == 0) & (i <= nb))
    def _():
        for r in range(rows):
            _row_copy(h_hbm, 0, xf_sc.at[slot], r, sem.at[slot]).wait()

    @pl.when(i < nb)
    def _():
        @pl.when(f == 0)
        def _():
            xb_sc[...] = xf_sc[slot].astype(BF16)
            acc_sc[...] = jnp.zeros(acc_sc.shape, F32)

        base = jnp.minimum(i + 1, n_blk - 1) * rows + f * part
        for r in range(part):
            _row_copy(h_hbm, tok_ref[base + r], xf_sc.at[1 - slot], f * part + r, sem.at[1 - slot]).start()

        x = xb_sc[...]
        gate = jnp.dot(x, wg_ref[0], preferred_element_type=F32)
        up = jnp.dot(x, wu_ref[0], preferred_element_type=F32)
        act = (gate * jax.nn.sigmoid(gate) * up).astype(BF16)
        acc_sc[...] += jnp.dot(act, wd_ref[0], preferred_element_type=F32)

        @pl.when(f == n_f - 1)
        def _():
            o_ref[...] = acc_sc[...]

    @pl.when(i >= nb)
    def _():
        o_ref[...] = jnp.zeros(o_ref.shape, F32)


def _experts(h, row_tok, blk_e, n_used, wg, wu, wd, rows, tf=1792):
    n_rows = row_tok.shape[0]
    d = h.shape[1]
    ff = wg.shape[2]
    n_blk = n_rows // rows
    last = ff // tf - 1
    grid_spec = pltpu.PrefetchScalarGridSpec(
        num_scalar_prefetch=3,
        grid=(n_blk, ff // tf),
        in_specs=[pl.BlockSpec(memory_space=pl.ANY),
                  pl.BlockSpec((1, d, tf), lambda i, f, be, nb, tok: (be[i], 0, jnp.where(i < nb[0], f, last))),
                  pl.BlockSpec((1, d, tf), lambda i, f, be, nb, tok: (be[i], 0, jnp.where(i < nb[0], f, last))),
                  pl.BlockSpec((1, tf, d), lambda i, f, be, nb, tok: (be[i], jnp.where(i < nb[0], f, last), 0))],
        out_specs=pl.BlockSpec((rows, d), lambda i, f, be, nb, tok: (i, 0)),
        scratch_shapes=[pltpu.VMEM((2, rows, d), F32), pltpu.VMEM((rows, d), BF16), pltpu.VMEM((rows, d), F32),
                        pltpu.SemaphoreType.DMA((2,))],
    )
    assert n_blk * rows > TOP_K * h.shape[0] + N_EXPERTS * (rows - 1)
    return pl.pallas_call(
        functools.partial(_experts_body, n_blk=n_blk, n_f=ff // tf),
        grid_spec=grid_spec,
        out_shape=jax.ShapeDtypeStruct((n_rows, d), F32),
        compiler_params=_cparams("arbitrary", "arbitrary"),
        name="moe_experts",
    )(blk_e, n_used, row_tok, h, wg, wu, wd)


def _combine_body(dest_ref, x_ref, rt_ref, ys_hbm, o_ref, buf, sem):
    tm = x_ref.shape[0]
    i = pl.program_id(0)
    slot = i & 1

    def start(tile, s):
        def body(r, c):
            for k in range(TOP_K):
                _row_copy(ys_hbm, dest_ref[(tile * tm + r) * TOP_K + k], buf.at[s, k], r, sem.at[s]).start()
            return c
        lax.fori_loop(0, tm, body, 0, unroll=8)

    @pl.when(i == 0)
    def _():
        start(0, 0)

    @pl.when(i + 1 < pl.num_programs(0))
    def _():
        start(i + 1, 1 - slot)

    for r in range(tm):
        for k in range(TOP_K):
            _row_copy(ys_hbm, 0, buf.at[slot, k], r, sem.at[slot]).wait()
    rt = rt_ref[...]
    o_ref[...] = x_ref[...] + rt[:, 0:1] * buf[slot, 0] + rt[:, 1:2] * buf[slot, 1]


def _combine(x, routing, ys, dest, tm=256):
    t, d = x.shape
    tm = min(tm, t)
    grid_spec = pltpu.PrefetchScalarGridSpec(
        num_scalar_prefetch=1,
        grid=(t // tm,),
        in_specs=[pl.BlockSpec((tm, d), lambda i, dest: (i, 0)),
                  pl.BlockSpec((tm, LANES), lambda i, dest: (i, 0)),
                  pl.BlockSpec(memory_space=pl.ANY)],
        out_specs=pl.BlockSpec((tm, d), lambda i, dest: (i, 0)),
        scratch_shapes=[pltpu.VMEM((2, TOP_K, tm, d), F32), pltpu.SemaphoreType.DMA((2,))],
    )
    return pl.pallas_call(
        _combine_body,
        grid_spec=grid_spec,
        out_shape=jax.ShapeDtypeStruct((t, d), F32),
        compiler_params=_cparams("arbitrary"),
        name="moe_combine",
    )(dest, x, routing, ys)


MOE_ROWS = 512


def _moe(x, gain, router, w_gate, w_up, w_down):
    t, d = x.shape
    h, routing = _router(x, gain, router)
    n_assign = t * TOP_K
    flat_e = routing[:, 2:2 + TOP_K].astype(jnp.int32).reshape(-1)
    onehot = (flat_e[:, None] == jnp.arange(N_EXPERTS)[None, :]).astype(jnp.int32)
    before = jnp.cumsum(onehot, axis=0) - onehot
    counts = jnp.sum(onehot, axis=0)
    padded = (counts + MOE_ROWS - 1) // MOE_ROWS * MOE_ROWS
    pend = jnp.cumsum(padded)
    pstart = pend - padded
    dest = (jnp.sum((pstart[None, :] + before) * onehot, axis=1)).astype(jnp.int32)
    n_rows = -(-n_assign // MOE_ROWS) * MOE_ROWS + N_EXPERTS * MOE_ROWS
    n_blk = n_rows // MOE_ROWS
    row_tok = jnp.zeros((n_rows,), jnp.int32).at[dest].set(jnp.arange(n_assign, dtype=jnp.int32) // TOP_K)
    blk_start = jnp.arange(n_blk, dtype=jnp.int32) * MOE_ROWS
    blk_e = jnp.minimum(jnp.sum((pend[None, :] <= blk_start[:, None]).astype(jnp.int32), axis=1), N_EXPERTS - 1)
    n_used = (pend[-1] // MOE_ROWS).astype(jnp.int32).reshape(1)
    ys = _experts(h, row_tok, blk_e.astype(jnp.int32), n_used, w_gate.astype(BF16), w_up.astype(BF16),
                  w_down.astype(BF16), MOE_ROWS)
    return _combine(x, routing, ys, dest)


def _final_norm_body(x_ref, g_ref, o_ref):
    o_ref[...] = _rms(x_ref[...], g_ref[...])


def _final_norm(x, gain, tm=1024):
    t, d = x.shape
    tm = min(tm, t)
    return pl.pallas_call(
        _final_norm_body,
        grid=(t // tm,),
        in_specs=[pl.BlockSpec((tm, d), lambda i: (i, 0)), pl.BlockSpec((1, d), lambda i: (0, 0))],
        out_specs=pl.BlockSpec((tm, d), lambda i: (i, 0)),
        out_shape=jax.ShapeDtypeStruct((t, d), F32),
        compiler_params=_cparams("parallel"),
        name="final_norm",
    )(x, gain.reshape(1, d).astype(F32))


def _all_rope_tables(s):
    return {"g32w512": _rope_tables(s, 32, 512), "g64w384": _rope_tables(s, 64, 384),
            "g32w128": _rope_tables(s, 32, 128), "g64w512": _rope_tables(s, 64, 512)}


def kernel(x, mem, norm_mix, w_in, w_out, diff_lambda, diff_subln, nsa_cmp_pos, nsa_cmp_w1, nsa_cmp_w2, mla_q_norm, mla_kv_norm, mla_w_uq, mla_w_ukv, norm_cross, norm_mem, cross_wq, cross_wk, cross_wv, cross_wo, norm_ffn, ffn_w_gate, ffn_w_up, ffn_w_down, moe_router, moe_w_gate, moe_w_up, moe_w_down, final_norm):
    b, s, d = x.shape
    depth = w_in.shape[0]
    tabs = _all_rope_tables(s)
    for l in range(depth):
        x = _token_mixers(x, l, tabs, norm_mix[l], w_in[l], w_out[l], diff_lambda[l], diff_subln[l],
                          nsa_cmp_pos[l], nsa_cmp_w1[l], nsa_cmp_w2[l], mla_q_norm[l], mla_kv_norm[l],
                          mla_w_uq[l], mla_w_ukv[l])
        x = _cross_attention(x, mem, norm_cross[l], norm_mem[l], cross_wq[l], cross_wk[l], cross_wv[l],
                             cross_wo[l])
        xt = x.reshape(b * s, d)
        i = l // 2
        if l % 2 == 0:
            xt = _swiglu(xt, norm_ffn[l], ffn_w_gate[i], ffn_w_up[i], ffn_w_down[i])
        else:
            xt = _moe(xt, norm_ffn[l], moe_router[i], moe_w_gate[i], moe_w_up[i], moe_w_down[i])
        x = xt.reshape(b, s, d)
    return _final_norm(x.reshape(b * s, d), final_norm).reshape(b, s, d)
```

```python
import functools
import math

import numpy as np
import jax
import jax.numpy as jnp
from jax import lax
from jax.experimental import pallas as pl
from jax.experimental.pallas import tpu as pltpu

F32 = jnp.float32
BF16 = jnp.bfloat16

HEAD_DIM = 64
GROUP_HEADS = 4
GROUP_WIDTH = HEAD_DIM * GROUP_HEADS
ROPE_THETA = 10000.0
NORM_EPS = 1e-6
DIFF_DK = HEAD_DIM // 2
CMP_LEN = 32
CMP_STRIDE = 16
SEL_BLOCK = 64
SEL_SHIFT = 6
SEL_TOPN = 16
NSA_WINDOW = 512
FORCED_SCORE = 1e6
MLA_NOPE = 64
MLA_ROPE = 32
MLA_DV = 64
DILATED_PATTERNS = ((128, 1), (512, 4), (2048, 16))
CROSS_HEADS = 4
N_EXPERTS = 8
TOP_K = 2

LANES = 128
V7X_VMEM_BYTES = 64 * 1024 * 1024
VMEM_LIMIT = V7X_VMEM_BYTES * 7 // 8

ONES_ROWS = 16
NEG_BIG = -1e30
SEL_BIAS = -1e9
LOG2E = math.log2(math.e)
LN2 = math.log(2.0)


def _cparams(*sem):
    return pltpu.CompilerParams(dimension_semantics=sem, vmem_limit_bytes=VMEM_LIMIT)


def _rms(xf, gain):
    ms = jnp.mean(xf * xf, axis=-1, keepdims=True)
    return xf * lax.rsqrt(ms + NORM_EPS) * gain


def _mm_body(*refs, nx, norm, res, splits):
    x_refs = refs[:nx]
    pos = nx
    g_ref = refs[pos] if norm else None
    pos += int(norm)
    w_ref = refs[pos]
    pos += 1
    r_ref = refs[pos] if res else None
    pos += int(res)
    o_refs = refs[pos:pos + len(splits)]
    xs_ref = refs[pos + len(splits)]

    @pl.when(pl.program_id(1) == 0)
    def _():
        off = 0
        for xr in x_refs:
            xv = xr[...]
            if norm:
                xv = _rms(xv.astype(F32), g_ref[...])
            kw = xv.shape[-1]
            xs_ref[:, off:off + kw] = xv.astype(BF16)
            off += kw

    acc = jnp.dot(xs_ref[...], w_ref[...], preferred_element_type=F32)
    if res:
        acc = acc + r_ref[...]
    off = 0
    for o_ref, width in zip(o_refs, splits):
        o_ref[...] = acc[:, off:off + width].astype(o_ref.dtype)
        off += width


def _mm(xs, w, *, gain=None, res=None, tm=512, tn=None, splits=None, out_dtype=F32, name="mm"):
    t = xs[0].shape[0]
    k, n = w.shape
    assert sum(x.shape[1] for x in xs) == k
    tn = n if tn is None else tn
    splits = (tn,) if splits is None else tuple(splits)
    assert sum(splits) == tn and (len(splits) == 1 or tn == n)
    tm = min(tm, t)
    assert t % tm == 0 and n % tn == 0
    norm = gain is not None
    assert not norm or len(xs) == 1
    in_specs = [pl.BlockSpec((tm, x.shape[1]), lambda i, j: (i, 0)) for x in xs]
    args = list(xs)
    if norm:
        in_specs.append(pl.BlockSpec((1, k), lambda i, j: (0, 0)))
        args.append(gain.reshape(1, k).astype(F32))
    in_specs.append(pl.BlockSpec((k, tn), lambda i, j: (0, j)))
    args.append(w)
    if res is not None:
        in_specs.append(pl.BlockSpec((tm, tn), lambda i, j: (i, j)))
        args.append(res)
    if len(splits) == 1:
        out_shape = [jax.ShapeDtypeStruct((t, n), out_dtype)]
        out_specs = [pl.BlockSpec((tm, tn), lambda i, j: (i, j))]
    else:
        out_shape = [jax.ShapeDtypeStruct((t, s), out_dtype) for s in splits]
        out_specs = [pl.BlockSpec((tm, s), lambda i, j: (i, 0)) for s in splits]
    outs = pl.pallas_call(
        functools.partial(_mm_body, nx=len(xs), norm=norm, res=res is not None, splits=splits),
        grid=(t // tm, n // tn),
        in_specs=in_specs,
        out_specs=out_specs,
        out_shape=out_shape,
        scratch_shapes=[pltpu.VMEM((tm, k), BF16)],
        compiler_params=_cparams("parallel", "arbitrary"),
        name=name,
    )(*args)
    return outs[0] if len(outs) == 1 else outs


def _rope_tables(s, group, width):
    half = group // 2
    pos = jnp.arange(s, dtype=F32)
    inv_freq = ROPE_THETA ** (-jnp.arange(half, dtype=F32) / half)
    ang = pos[:, None] * inv_freq[None, :]
    cos, sin = jnp.cos(ang), jnp.sin(ang)
    zero = jnp.zeros_like(sin)
    reps = width // group
    c = jnp.tile(jnp.concatenate([cos, cos], -1), (1, reps))
    s1 = jnp.tile(jnp.concatenate([zero, sin], -1), (1, reps))
    s2 = jnp.tile(jnp.concatenate([-sin, zero], -1), (1, reps))
    return c, s1, s2


def _rope(x, c, s1, s2, half):
    w = x.shape[-1]
    return x * c + pltpu.roll(x, half, 1) * s1 + pltpu.roll(x, w - half, 1) * s2


def _prep_a_body(z_ref, c_ref, s1_ref, s2_ref, q_ref, k_ref, vt_ref):
    z = z_ref[0]
    qk = _rope(z[:, :512], c_ref[...], s1_ref[...], s2_ref[...], DIFF_DK // 2)
    lane = lax.broadcasted_iota(jnp.int32, (z.shape[0], HEAD_DIM), 1)
    scale = DIFF_DK ** -0.5 * LOG2E
    for h in range(GROUP_HEADS):
        q = qk[:, h * 64:(h + 1) * 64] * scale
        q_ref[0, h, 0] = jnp.where(lane < DIFF_DK, q, 0.0).astype(BF16)
        q_ref[0, h, 1] = jnp.where(lane >= DIFF_DK, q, 0.0).astype(BF16)
        k_ref[0, h] = qk[:, 256 + h * 64:256 + (h + 1) * 64].astype(BF16)
    vt_ref[0] = z[:, 512:768].T.reshape(vt_ref.shape[1:]).astype(BF16)


def _prep_a(z_a, tabs, ts):
    b, s, _ = z_a.shape
    tspec = pl.BlockSpec((ts, 512), lambda i, bi: (i, 0))
    return pl.pallas_call(
        _prep_a_body,
        grid=(s // ts, b),
        in_specs=[pl.BlockSpec((1, ts, 768), lambda i, bi: (bi, i, 0)), tspec, tspec, tspec],
        out_specs=[pl.BlockSpec((1, 4, 2, ts, 64), lambda i, bi: (bi, 0, 0, i, 0)),
                   pl.BlockSpec((1, 4, ts, 64), lambda i, bi: (bi, 0, i, 0)),
                   pl.BlockSpec((1, 4, 64, ts), lambda i, bi: (bi, 0, 0, i))],
        out_shape=[jax.ShapeDtypeStruct((b, 4, 2, s, 64), BF16),
                   jax.ShapeDtypeStruct((b, 4, s, 64), BF16),
                   jax.ShapeDtypeStruct((b, 4, 64, s), BF16)],
        compiler_params=_cparams("parallel", "parallel"),
        name="prep_a",
    )(z_a, *tabs)


def _prep_b_body(z_ref, c_ref, s1_ref, s2_ref, qn_ref, qr_ref, ka_ref, kw_ref, kc_ref, vc_ref,
                 vs_ref, vw_ref, g_ref, *, ts):
    z = z_ref[0]
    scale = HEAD_DIM ** -0.5 * LOG2E
    rp = _rope(z[:, :384], c_ref[...], s1_ref[...], s2_ref[...], HEAD_DIM // 2)
    for h in range(GROUP_HEADS):
        qn_ref[0, h] = (z[:, h * 64:(h + 1) * 64] * scale).astype(BF16)
        qr_ref[0, h] = (rp[:, h * 64:(h + 1) * 64] * scale).astype(BF16)
    kpos = pl.program_id(0) * ts + lax.broadcasted_iota(jnp.int32, (ts, SEL_BLOCK), 0)
    lane = lax.broadcasted_iota(jnp.int32, (ts, SEL_BLOCK), 1)
    onehot = jnp.where((kpos >> SEL_SHIFT) == lane, 1.0, 0.0)
    ka_ref[0] = jnp.concatenate([rp[:, 256:320], onehot], axis=-1).astype(BF16)
    kw_ref[0] = rp[:, 320:384].astype(BF16)
    kc_ref[0] = z[:, 384:448].astype(BF16)
    vc_ref[0] = z[:, 448:512].astype(BF16)
    vt = z[:, 512:640].T.astype(BF16)
    vs_ref[0] = vt[:HEAD_DIM]
    vw_ref[0] = vt[HEAD_DIM:]
    g_ref[0] = jax.nn.sigmoid(z[:, 640:768])


def _prep_b(z_b, tabs, ts):
    b, s, _ = z_b.shape
    tspec = pl.BlockSpec((ts, 384), lambda i, bi: (i, 0))
    h4 = pl.BlockSpec((1, 4, ts, 64), lambda i, bi: (bi, 0, i, 0))
    s64 = pl.BlockSpec((1, ts, 64), lambda i, bi: (bi, i, 0))
    s128 = pl.BlockSpec((1, ts, 128), lambda i, bi: (bi, i, 0))
    t64 = pl.BlockSpec((1, 64, ts), lambda i, bi: (bi, 0, i))
    sd = lambda *shape, dt=BF16: jax.ShapeDtypeStruct(shape, dt)
    return pl.pallas_call(
        functools.partial(_prep_b_body, ts=ts),
        grid=(s // ts, b),
        in_specs=[pl.BlockSpec((1, ts, 768), lambda i, bi: (bi, i, 0)), tspec, tspec, tspec],
        out_specs=[h4, h4, s128, s64, s64, s64, t64, t64, s128],
        out_shape=[sd(b, 4, s, 64), sd(b, 4, s, 64), sd(b, s, 128), sd(b, s, 64), sd(b, s, 64),
                   sd(b, s, 64), sd(b, 64, s), sd(b, 64, s), sd(b, s, 128, dt=F32)],
        compiler_params=_cparams("parallel", "parallel"),
        name="prep_b",
    )(z_b, *tabs)


def _prep_c_body(z_ref, c_ref, s1_ref, s2_ref, qg_ref, kg_ref, wq_ref, wkv_ref, q_ref, k_ref, vt_ref):
    z = z_ref[0]
    scale = (MLA_NOPE + MLA_ROPE) ** -0.5 * LOG2E
    cq = _rms(z[:, :384], qg_ref[...]).astype(BF16)
    ckv = _rms(z[:, 384:512], kg_ref[...]).astype(BF16)
    qc = jnp.dot(cq, wq_ref[...], preferred_element_type=F32)
    kvc = jnp.dot(ckv, wkv_ref[...], preferred_element_type=F32)
    half = MLA_ROPE // 2
    c, s1, s2 = c_ref[...], s1_ref[...], s2_ref[...]
    q_rope = _rope(qc[:, 256:384], c, s1, s2, half)
    k_rope = _rope(z[:, 512:640], c, s1, s2, half)[:, :MLA_ROPE]
    for h in range(GROUP_HEADS):
        q = jnp.concatenate([qc[:, h * 64:(h + 1) * 64], q_rope[:, h * 32:(h + 1) * 32]], axis=-1)
        q_ref[0, h] = (q * scale).astype(BF16)
        k_ref[0, h] = jnp.concatenate([kvc[:, h * 64:(h + 1) * 64], k_rope], axis=-1).astype(BF16)
    vt_ref[0] = kvc[:, 256:512].T.reshape(vt_ref.shape[1:]).astype(BF16)


def _prep_c(z_c, tabs, q_gain, kv_gain, wq, wkv, ts):
    b, s, _ = z_c.shape
    tspec = pl.BlockSpec((ts, 128), lambda i, bi: (i, 0))
    full = lambda a: pl.BlockSpec(a.shape, lambda i, bi: (0,) * a.ndim)
    dk = MLA_NOPE + MLA_ROPE
    return pl.pallas_call(
        _prep_c_body,
        grid=(s // ts, b),
        in_specs=[pl.BlockSpec((1, ts, 640), lambda i, bi: (bi, i, 0)), tspec, tspec, tspec,
                  full(q_gain), full(kv_gain), full(wq), full(wkv)],
        out_specs=[pl.BlockSpec((1, 4, ts, dk), lambda i, bi: (bi, 0, i, 0)),
                   pl.BlockSpec((1, 4, ts, dk), lambda i, bi: (bi, 0, i, 0)),
                   pl.BlockSpec((1, 4, 64, ts), lambda i, bi: (bi, 0, 0, i))],
        out_shape=[jax.ShapeDtypeStruct((b, 4, s, dk), BF16),
                   jax.ShapeDtypeStruct((b, 4, s, dk), BF16),
                   jax.ShapeDtypeStruct((b, 4, 64, s), BF16)],
        compiler_params=_cparams("parallel", "parallel"),
        name="prep_c",
    )(z_c, *tabs, q_gain, kv_gain, wq, wkv)


def _prep_d_body(z_ref, c_ref, s1_ref, s2_ref, q_ref, k_ref, vt_ref):
    z = z_ref[0]
    scale = HEAD_DIM ** -0.5 * LOG2E
    qk = _rope(z[:, :512], c_ref[...], s1_ref[...], s2_ref[...], HEAD_DIM // 2)
    for h in range(GROUP_HEADS):
        q_ref[0, h] = (qk[:, h * 64:(h + 1) * 64] * scale).astype(BF16)
        k_ref[0, h] = qk[:, 256 + h * 64:256 + (h + 1) * 64].astype(BF16)
    vt_ref[0] = z[:, 512:768].T.reshape(vt_ref.shape[1:]).astype(BF16)


def _prep_d(z_d, tabs, ts):
    b, s, _ = z_d.shape
    tspec = pl.BlockSpec((ts, 512), lambda i, bi: (i, 0))
    h4 = pl.BlockSpec((1, 4, ts, 64), lambda i, bi: (bi, 0, i, 0))
    return pl.pallas_call(
        _prep_d_body,
        grid=(s // ts, b),
        in_specs=[pl.BlockSpec((1, ts, 768), lambda i, bi: (bi, i, 0)), tspec, tspec, tspec],
        out_specs=[h4, h4, pl.BlockSpec((1, 4, 64, ts), lambda i, bi: (bi, 0, 0, i))],
        out_shape=[jax.ShapeDtypeStruct((b, 4, s, 64), BF16)] * 2 + [jax.ShapeDtypeStruct((b, 4, 64, s), BF16)],
        compiler_params=_cparams("parallel", "parallel"),
        name="prep_d",
    )(z_d, *tabs)


def _flash_schedule(nq, tq, tk, sk, window):
    qi, kj, fl = [], [], []
    for i in range(nq):
        q_lo, q_hi = i * tq, i * tq + tq - 1
        j_hi = min(q_hi, sk - 1) // tk
        j_lo = 0 if window is None else max(0, q_lo - window) // tk
        for j in range(j_lo, j_hi + 1):
            k_min, k_max = j * tk, j * tk + tk - 1
            full = k_max <= q_lo and (window is None or q_hi - k_min <= window)
            qi.append(i)
            kj.append(j)
            fl.append((1 if j == j_lo else 0) | (2 if j == j_hi else 0) | (0 if full else 4))
    return (np.asarray(qi, np.int32), np.asarray(kj, np.int32), np.asarray(fl, np.int32))


def _flash_body(qi_ref, kj_ref, fl_ref, q_ref, k_ref, vt_ref, o_ref, m_sc, acc_sc, *,
                gb, hq, tq, tk, window, with_lse, merge_groups):
    t = pl.program_id(1)
    fl = fl_ref[t]
    rows = hq * tq
    dv = vt_ref.shape[1]

    @pl.when((fl & 1) != 0)
    def _():
        m_sc[...] = jnp.full(m_sc.shape, NEG_BIG, F32)
        acc_sc[...] = jnp.zeros(acc_sc.shape, F32)

    def step(masked):
        q = q_ref[...].reshape(gb, rows, q_ref.shape[-1])
        st = jnp.einsum("gkd,gqd->gkq", k_ref[...], q, preferred_element_type=F32)
        if masked:
            krow = lax.broadcasted_iota(jnp.int32, (tk, rows), 0)
            qcol = lax.broadcasted_iota(jnp.int32, (tk, rows), 1)
            if hq > 1:
                qcol = qcol & (tq - 1)
            dist = (qi_ref[t] * tq - kj_ref[t] * tk) + qcol - krow
            ok = dist >= 0
            if window is not None:
                ok = ok & (dist <= window)
            st = jnp.where(ok[None], st, NEG_BIG)
        m_prev = m_sc[...]
        m_new = jnp.maximum(m_prev, jnp.max(st, axis=1, keepdims=True))
        alpha = jnp.exp2(m_prev - m_new)
        p = jnp.exp2((st - m_new).astype(BF16))
        v_ext = jnp.concatenate([vt_ref[...], jnp.ones((gb, ONES_ROWS, tk), BF16)], axis=1)
        acc_sc[...] = alpha * acc_sc[...] + jnp.einsum("gdk,gkq->gdq", v_ext, p, preferred_element_type=F32)
        m_sc[...] = m_new

    @pl.when((fl & 4) != 0)
    def _():
        step(True)

    @pl.when((fl & 4) == 0)
    def _():
        step(False)

    @pl.when((fl & 2) != 0)
    def _():
        l = acc_sc[:, dv:dv + 1, :]
        o_t = acc_sc[:, :dv, :] / l
        if merge_groups:
            o_ref[0] = o_t.reshape(gb * dv, rows).T.astype(o_ref.dtype)
            return
        if with_lse:
            extra = jnp.broadcast_to(m_sc[...] * LN2 + jnp.log(l), o_t.shape)
        else:
            extra = jnp.zeros((gb, LANES - dv, rows), F32)
        dvo = o_ref.shape[-1]
        for g in range(gb):
            o = jnp.concatenate([o_t[g], extra[g]], axis=0).T
            o_ref[g] = o[:, :dvo].reshape(o_ref.shape[1:]).astype(o_ref.dtype)


def _flash(q, k, vt, *, tq, tk, gb=1, window=None, with_lse=False, merge_groups=False, out_dtype=F32,
           name="flash"):
    g, hq, sq, dk = q.shape
    _, dv, sk = vt.shape
    tq, tk = min(tq, sq), min(tk, sk)
    assert g % gb == 0 and sq % tq == 0 and sk % tk == 0
    assert tq & (tq - 1) == 0 and 2 * dv == LANES
    qi, kj, fl = _flash_schedule(sq // tq, tq, tk, sk, window)
    dvo = 2 * dv if with_lse else dv
    rows = hq * tq
    if merge_groups:
        assert hq == 1 and not with_lse and (gb * dv) % LANES == 0
        out_spec = pl.BlockSpec((1, tq, gb * dv), lambda gi, t, qi, kj, fl: (gi, qi[t], 0))
        out_shape = jax.ShapeDtypeStruct((g // gb, sq, gb * dv), out_dtype)
    else:
        out_spec = pl.BlockSpec((gb, hq, tq, dvo), lambda gi, t, qi, kj, fl: (gi, 0, qi[t], 0))
        out_shape = jax.ShapeDtypeStruct((g, hq, sq, dvo), out_dtype)
    grid_spec = pltpu.PrefetchScalarGridSpec(
        num_scalar_prefetch=3,
        grid=(g // gb, len(qi)),
        in_specs=[pl.BlockSpec((gb, hq, tq, dk), lambda gi, t, qi, kj, fl: (gi, 0, qi[t], 0)),
                  pl.BlockSpec((gb, tk, dk), lambda gi, t, qi, kj, fl: (gi, kj[t], 0)),
                  pl.BlockSpec((gb, dv, tk), lambda gi, t, qi, kj, fl: (gi, 0, kj[t]))],
        out_specs=out_spec,
        scratch_shapes=[pltpu.VMEM((gb, 1, rows), F32), pltpu.VMEM((gb, dv + ONES_ROWS, rows), F32)],
    )
    return pl.pallas_call(
        functools.partial(_flash_body, gb=gb, hq=hq, tq=tq, tk=tk, window=window, with_lse=with_lse,
                          merge_groups=merge_groups),
        grid_spec=grid_spec,
        out_shape=out_shape,
        compiler_params=_cparams("parallel", "arbitrary"),
        name=name,
    )(jnp.asarray(qi), jnp.asarray(kj), jnp.asarray(fl), q, k, vt)


def _gelu_tanh(x):
    return 0.5 * x * (1.0 + jnp.tanh(math.sqrt(2.0 / math.pi) * (x + 0.044715 * (x * x * x))))


def _compress_body(r_ref, pos_ref, w1_ref, w2_ref, o_ref):
    half = CMP_STRIDE * HEAD_DIM
    r = r_ref[0, 0]
    w1 = w1_ref[0]
    top = jnp.dot(r, w1[:half], preferred_element_type=F32)
    bot = jnp.dot(r, w1[half:], preferred_element_type=F32)
    nc = r.shape[0]
    posb = jnp.dot(pos_ref[0], w1, preferred_element_type=F32)[:1]
    pre = top + pltpu.roll(bot, nc - 1, 0) + posb
    hid = _gelu_tanh(pre).astype(BF16)
    o_ref[0, 0] = jnp.dot(hid, w2_ref[0], preferred_element_type=F32).astype(o_ref.dtype)


def _nsa_compress(kv_rows, pos_flat, w1, w2):
    _, b, nc, _ = kv_rows.shape
    return pl.pallas_call(
        _compress_body,
        grid=(2, b),
        in_specs=[pl.BlockSpec((1, 1, nc, 1024), lambda w, bi: (w, bi, 0, 0)),
                  pl.BlockSpec((1, 8, 2048), lambda w, bi: (w, 0, 0)),
                  pl.BlockSpec((1, 2048, 256), lambda w, bi: (w, 0, 0)),
                  pl.BlockSpec((1, 256, 64), lambda w, bi: (w, 0, 0))],
        out_specs=pl.BlockSpec((1, 1, nc, 64), lambda w, bi: (w, bi, 0, 0)),
        out_shape=jax.ShapeDtypeStruct((2, b, nc, 64), BF16),
        compiler_params=_cparams("parallel", "parallel"),
        name="nsa_compress",
    )(kv_rows, pos_flat, w1, w2)


def _cmp_select_body(qn_ref, qr_ref, kc_ref, vc_ref, cov_ref, o_ref, qa_ref, *, tq, n_sel, k_top):
    ncp = kc_ref.shape[2]
    pos = pl.program_id(1) * tq + lax.broadcasted_iota(jnp.int32, (tq, ncp), 0)
    blk_end = lax.broadcasted_iota(jnp.int32, (tq, ncp), 1) * CMP_STRIDE + (CMP_LEN - 1)
    mask = blk_end <= pos
    kc, vc = kc_ref[0, 0], vc_ref[0, 0]
    p_sum = jnp.zeros((tq, ncp), F32)
    for h in range(GROUP_HEADS):
        s = lax.dot_general(qn_ref[0, h], kc, (((1,), (1,)), ((), ())), preferred_element_type=F32)
        s = jnp.where(mask, s, NEG_BIG)
        mx = jnp.max(s, axis=-1, keepdims=True)
        mx = jnp.where(mx > 0.5 * NEG_BIG, mx, 0.0)
        e = jnp.where(mask, jnp.exp2(s - mx), 0.0)
        p = e / jnp.maximum(jnp.sum(e, axis=-1, keepdims=True), 1e-30)
        o_ref[0, h] = jnp.dot(p.astype(BF16), vc, preferred_element_type=F32).astype(o_ref.dtype)
        p_sum = p_sum + p
    p_hi = p_sum.astype(BF16)
    p_lo = (p_sum - p_hi.astype(F32)).astype(BF16)
    cov = cov_ref[...]
    imp = (jnp.dot(p_hi, cov, preferred_element_type=F32) + jnp.dot(p_lo, cov, preferred_element_type=F32))
    nblk = SEL_BLOCK
    imp_t = imp.T[:nblk]
    tpos = pl.program_id(1) * tq + lax.broadcasted_iota(jnp.int32, (nblk, tq), 1)
    cur = tpos >> SEL_SHIFT
    j = lax.broadcasted_iota(jnp.int32, (nblk, tq), 0)
    valid = j <= cur
    forced = (j == 0) | (j == cur) | (j == cur - 1)
    score = jnp.where(valid, jnp.where(forced, FORCED_SCORE, imp_t), -jnp.inf)
    sub = 8
    groups = [score[g * sub:(g + 1) * sub] for g in range(nblk // sub)]
    ranks = [jnp.zeros((sub, tq), jnp.int32) for _ in groups]
    jj = lax.broadcasted_iota(jnp.int32, (sub, tq), 0)
    for i in range(n_sel):
        row = jnp.broadcast_to(score[i:i + 1], (sub, tq))
        for g, sg in enumerate(groups):
            if i < g * sub:
                ahead = row >= sg
            elif i >= (g + 1) * sub:
                ahead = row > sg
            else:
                ahead = (row > sg) | ((row == sg) & (i - g * sub < jj))
            ranks[g] = ranks[g] + ahead.astype(jnp.int32)
    rank = jnp.concatenate(ranks, axis=0)
    bias_t = jnp.where(valid & (rank < k_top), 0.0, SEL_BIAS)
    bias = jnp.concatenate([bias_t, jnp.zeros((LANES - nblk, tq), F32)], axis=0).T[:, :nblk].astype(BF16)
    for h in range(GROUP_HEADS):
        qa_ref[0, h] = jnp.concatenate([qr_ref[0, h], bias], axis=-1)


def _cover_matrix(ncp, n_sel, lanes):
    c_start = np.arange(ncp)[:, None] * CMP_STRIDE
    s_start = np.arange(lanes)[None, :] * SEL_BLOCK
    cov = np.clip(np.minimum(c_start + CMP_LEN, s_start + SEL_BLOCK) - np.maximum(c_start, s_start), 0, None)
    cov = cov.astype(np.float32) / CMP_LEN
    cov[:, n_sel:] = 0.0
    cov[ncp - 1:, :] = 0.0
    return cov


def _cmp_select(qn, qr, kvc, tq):
    b, _, s, _ = qn.shape
    ncp = kvc.shape[2]
    n_sel = s // SEL_BLOCK
    assert n_sel <= SEL_BLOCK
    k_top = min(SEL_TOPN, n_sel)
    cov = jnp.asarray(_cover_matrix(ncp, n_sel, LANES), BF16)
    h4 = lambda d: pl.BlockSpec((1, 4, tq, d), lambda bi, i: (bi, 0, i, 0))
    return pl.pallas_call(
        functools.partial(_cmp_select_body, tq=tq, n_sel=n_sel, k_top=k_top),
        grid=(b, s // tq),
        in_specs=[h4(64), h4(64),
                  pl.BlockSpec((1, 1, ncp, 64), lambda bi, i: (0, bi, 0, 0)),
                  pl.BlockSpec((1, 1, ncp, 64), lambda bi, i: (1, bi, 0, 0)),
                  pl.BlockSpec(cov.shape, lambda bi, i: (0, 0))],
        out_specs=[h4(64), h4(128)],
        out_shape=[jax.ShapeDtypeStruct((b, 4, s, 64), BF16), jax.ShapeDtypeStruct((b, 4, s, 128), BF16)],
        compiler_params=_cparams("parallel", "parallel"),
        name="nsa_cmp_select",
    )(qn, qr, kvc, kvc, cov)


def _merge_a_body(o_ref, lam_ref, g_ref, y_ref, *, out_scale):
    outs = []
    for h in range(GROUP_HEADS):
        o = o_ref[0, h, 0].astype(F32) - lam_ref[...] * o_ref[0, h, 1].astype(F32)
        outs.append(_rms(o, g_ref[...]) * out_scale)
    y_ref[0] = jnp.concatenate(outs, axis=-1).astype(y_ref.dtype)


def _merge_a(o, lam, subln, out_scale, ts):
    b, _, _, s, _ = o.shape
    return pl.pallas_call(
        functools.partial(_merge_a_body, out_scale=out_scale),
        grid=(b, s // ts),
        in_specs=[pl.BlockSpec((1, 4, 2, ts, 64), lambda bi, i: (bi, 0, 0, i, 0)),
                  pl.BlockSpec((1, 64), lambda bi, i: (0, 0)),
                  pl.BlockSpec((1, 64), lambda bi, i: (0, 0))],
        out_specs=pl.BlockSpec((1, ts, 256), lambda bi, i: (bi, i, 0)),
        out_shape=jax.ShapeDtypeStruct((b, s, 256), BF16),
        compiler_params=_cparams("parallel", "parallel"),
        name="merge_a",
    )(o, lam, subln)


def _merge_b_body(oc_ref, os_ref, ow_ref, g_ref, y_ref):
    g = g_ref[0]
    outs = []
    for h in range(GROUP_HEADS):
        outs.append(g[:, 3 * h:3 * h + 1] * oc_ref[0, h].astype(F32)
                    + g[:, 3 * h + 1:3 * h + 2] * os_ref[0, h].astype(F32)
                    + g[:, 3 * h + 2:3 * h + 3] * ow_ref[0, h].astype(F32))
    y_ref[0] = jnp.concatenate(outs, axis=-1).astype(y_ref.dtype)


def _merge_b(o_cmp, o_sel, o_win, gates, ts):
    b, _, s, _ = o_cmp.shape
    h4 = pl.BlockSpec((1, 4, ts, 64), lambda bi, i: (bi, 0, i, 0))
    return pl.pallas_call(
        _merge_b_body,
        grid=(b, s // ts),
        in_specs=[h4, h4, h4, pl.BlockSpec((1, ts, 128), lambda bi, i: (bi, i, 0))],
        out_specs=pl.BlockSpec((1, ts, 256), lambda bi, i: (bi, i, 0)),
        out_shape=jax.ShapeDtypeStruct((b, s, 256), BF16),
        compiler_params=_cparams("parallel", "parallel"),
        name="merge_b",
    )(o_cmp, o_sel, o_win, gates)


def _merge_d_body(o1_ref, o2_ref, o3_ref, y_ref):
    outs = []
    for h in range(GROUP_HEADS):
        e = [r[0, h] for r in (o1_ref, o2_ref, o3_ref)]
        lse = [x[:, 64:] for x in e]
        mx = jnp.maximum(jnp.maximum(lse[0], lse[1]), lse[2])
        w = [jnp.exp(x - mx) for x in lse]
        den = w[0] + w[1] + w[2]
        outs.append((w[0] * e[0][:, :64] + w[1] * e[1][:, :64] + w[2] * e[2][:, :64]) / den)
    y_ref[0] = jnp.concatenate(outs, axis=-1).astype(y_ref.dtype)


def _merge_d(o1, o2, o3, ts):
    b, _, s, _ = o1.shape
    h4 = pl.BlockSpec((1, 4, ts, 128), lambda bi, i: (bi, 0, i, 0))
    return pl.pallas_call(
        _merge_d_body,
        grid=(b, s // ts),
        in_specs=[h4, h4, h4],
        out_specs=pl.BlockSpec((1, ts, 256), lambda bi, i: (bi, i, 0)),
        out_shape=jax.ShapeDtypeStruct((b, s, 256), BF16),
        compiler_params=_cparams("parallel", "parallel"),
        name="merge_d",
    )(o1, o2, o3)


def _permute_w_in(w_in):
    a_cols = 768
    b0 = a_cols
    nq = 256
    seg = lambda i: w_in[:, b0 + nq + i * 64: b0 + nq + (i + 1) * 64]
    kc, vc, ks, vs, kw, vw = (seg(i) for i in range(6))
    gb = w_in[:, b0 + nq + 384: b0 + nq + 384 + 12]
    d = w_in.shape[0]
    w_b = jnp.concatenate([w_in[:, b0:b0 + nq], ks, kw, kc, vc, vs, vw, gb, jnp.zeros((d, 116), w_in.dtype)], axis=1)
    c0 = b0 + 652
    w_c = jnp.concatenate([w_in[:, c0:c0 + 544], jnp.zeros((d, 96), w_in.dtype)], axis=1)
    d0 = c0 + 544
    return jnp.concatenate([w_in[:, :a_cols], w_b, w_c, w_in[:, d0:d0 + 768]], axis=1).astype(BF16)


def _residue_classes(t, dil):
    g, s, d = t.shape
    return t.reshape(g, s // dil, dil, d).transpose(0, 2, 1, 3).reshape(g * dil, s // dil, d)


def _residue_classes_t(t, dil):
    g, d, s = t.shape
    return t.reshape(g, d, s // dil, dil).transpose(0, 3, 1, 2).reshape(g * dil, d, s // dil)


def _from_residue_classes(t, dil):
    gd, l, d = t.shape
    return t.reshape(gd // dil, dil, l, d).transpose(0, 2, 1, 3).reshape(gd // dil, l * dil, d)


def _token_mixers(x, layer, rope_tabs, norm_g, w_in, w_out, diff_lambda, diff_subln, cmp_pos, cmp_w1, cmp_w2,
                  q_norm, kv_norm, w_uq, w_ukv):
    b, s, d = x.shape
    t = b * s
    ts = min(512, s)
    z_a, z_b, z_c, z_d = _mm([x.reshape(t, d)], _permute_w_in(w_in), gain=norm_g, tm=min(512, t),
                             splits=(768, 768, 640, 768), name="w_in")
    z_a, z_b, z_c, z_d = (z.reshape(b, s, -1) for z in (z_a, z_b, z_c, z_d))

    qa, ka, va = _prep_a(z_a, rope_tabs["g32w512"], ts)
    oa = _flash(qa.reshape(b * 4, 2, s, 64), ka.reshape(b * 4, s, 64), va.reshape(b * 4, 64, s),
                tq=512, tk=512, gb=2, out_dtype=BF16, name="flash_diff").reshape(b, 4, 2, s, 64)
    lam_init = 0.8 - 0.6 * math.exp(-0.3 * layer)
    lp = diff_lambda.astype(F32)
    lam = jnp.exp(jnp.sum(lp[0] * lp[1])) - jnp.exp(jnp.sum(lp[2] * lp[3])) + lam_init
    y_a = _merge_a(oa, jnp.full((1, 64), lam, F32), diff_subln.reshape(1, 64).astype(F32), 1.0 - lam_init, ts)

    qn, qr, k_aug, kw, kc, vc, vs, vw, gates = _prep_b(z_b, rope_tabs["g64w384"], ts)
    kv_rows = jnp.stack([kc, vc]).reshape(2, b, s // CMP_STRIDE, CMP_STRIDE * HEAD_DIM)
    pos_flat = jnp.broadcast_to(cmp_pos.reshape(2, 1, CMP_LEN * HEAD_DIM), (2, 8, CMP_LEN * HEAD_DIM)).astype(BF16)
    kvc = _nsa_compress(kv_rows, pos_flat, cmp_w1.astype(BF16), cmp_w2.astype(BF16))
    o_cmp, q_aug = _cmp_select(qn, qr, kvc, min(256, s))
    o_sel = _flash(q_aug, k_aug, vs, tq=256, tk=512, out_dtype=BF16, name="flash_sel")
    o_win = _flash(qr, kw, vw, tq=256, tk=256, window=NSA_WINDOW - 1, out_dtype=BF16, name="flash_win")
    y_b = _merge_b(o_cmp, o_sel, o_win, gates, ts)

    wq = w_uq.reshape(-1, 4, MLA_NOPE + MLA_ROPE)
    wq = jnp.concatenate([wq[:, :, :MLA_NOPE].reshape(-1, 256), wq[:, :, MLA_NOPE:].reshape(-1, 128)], axis=1)
    wkv = w_ukv.reshape(-1, 4, MLA_NOPE + MLA_DV)
    wkv = jnp.concatenate([wkv[:, :, :MLA_NOPE].reshape(-1, 256), wkv[:, :, MLA_NOPE:].reshape(-1, 256)], axis=1)
    qc, kcat, vcv = _prep_c(z_c, rope_tabs["g32w128"], q_norm.reshape(1, -1).astype(F32),
                            kv_norm.reshape(1, -1).astype(F32), wq.astype(BF16), wkv.astype(BF16), ts)
    dk = MLA_NOPE + MLA_ROPE
    y_c = _flash(qc.reshape(b * 4, 1, s, dk), kcat.reshape(b * 4, s, dk), vcv.reshape(b * 4, 64, s),
                 tq=512, tk=512, gb=4, merge_groups=True, out_dtype=BF16, name="flash_mla")

    qd, kd, vd = _prep_d(z_d, rope_tabs["g64w512"], ts)
    qd, kd, vd = qd.reshape(b * 4, s, 64), kd.reshape(b * 4, s, 64), vd.reshape(b * 4, 64, s)
    o_d = []
    for window, dil in DILATED_PATTERNS:
        l = s // dil
        qq, kk, vv = _residue_classes(qd, dil), _residue_classes(kd, dil), _residue_classes_t(vd, dil)
        g = qq.shape[0]
        gb = math.gcd(g, 16)
        o = _flash(qq.reshape(g, 1, l, 64), kk, vv, tq=128, tk=128, gb=gb, window=window // dil,
                   with_lse=True, name=f"flash_dil{dil}")
        o_d.append(_from_residue_classes(o.reshape(g, l, 128), dil).reshape(b, 4, s, 128))
    y_d = _merge_d(*o_d, ts)

    ys = [y.reshape(t, 256) for y in (y_a, y_b, y_c, y_d)]
    return _mm(ys, w_out.astype(BF16), res=x.reshape(t, d), tm=min(512, t), name="w_out").reshape(b, s, d)


def _cross_body(x_ref, g_ref, wq_ref, k_ref, v_ref, wo_ref, o_ref):
    x = x_ref[0]
    h = _rms(x, g_ref[...]).astype(BF16)
    q = jnp.dot(h, wq_ref[...], preferred_element_type=F32).astype(BF16)
    dh = q.shape[-1] // CROSS_HEADS
    outs = []
    for hd in range(CROSS_HEADS):
        sl = slice(hd * dh, (hd + 1) * dh)
        s = lax.dot_general(q[:, sl], k_ref[0, :, sl], (((1,), (1,)), ((), ())), preferred_element_type=F32)
        e = jnp.exp(s - jnp.max(s, axis=-1, keepdims=True))
        p = e / jnp.sum(e, axis=-1, keepdims=True)
        outs.append(jnp.dot(p.astype(BF16), v_ref[0, :, sl], preferred_element_type=F32))
    o = jnp.concatenate(outs, axis=-1).astype(BF16)
    o_ref[0] = x + jnp.dot(o, wo_ref[...], preferred_element_type=F32)


def _cross_attention(x, mem, g_x, g_mem, wq, wk, wv, wo):
    b, s, d = x.shape
    m = mem.shape[1]
    ts = min(512, s)
    dh = d // CROSS_HEADS
    k, v = _mm([mem.reshape(b * m, d)], jnp.concatenate([wk, wv], axis=1).astype(BF16), gain=g_mem,
               tm=min(512, b * m), splits=(d, d), out_dtype=BF16, name="cross_kv")
    full = lambda a: pl.BlockSpec(a.shape, lambda bi, i: (0,) * a.ndim)
    wq_s = (wq * dh ** -0.5).astype(BF16)
    wo_b = wo.astype(BF16)
    g2 = g_x.reshape(1, d).astype(F32)
    return pl.pallas_call(
        _cross_body,
        grid=(b, s // ts),
        in_specs=[pl.BlockSpec((1, ts, d), lambda bi, i: (bi, i, 0)), full(g2), full(wq_s),
                  pl.BlockSpec((1, m, d), lambda bi, i: (bi, 0, 0)),
                  pl.BlockSpec((1, m, d), lambda bi, i: (bi, 0, 0)), full(wo_b)],
        out_specs=pl.BlockSpec((1, ts, d), lambda bi, i: (bi, i, 0)),
        out_shape=jax.ShapeDtypeStruct((b, s, d), F32),
        compiler_params=_cparams("parallel", "parallel"),
        name="cross_attn",
    )(x, g2, wq_s, k.reshape(b, m, d), v.reshape(b, m, d), wo_b)


def _swiglu_body(x_ref, g_ref, wg_ref, wu_ref, wd_ref, o_ref, h_sc, acc_sc):
    f = pl.program_id(1)

    @pl.when(f == 0)
    def _():
        h_sc[...] = _rms(x_ref[...], g_ref[...]).astype(BF16)
        acc_sc[...] = x_ref[...]

    h = h_sc[...]
    gate = jnp.dot(h, wg_ref[...], preferred_element_type=F32)
    up = jnp.dot(h, wu_ref[...], preferred_element_type=F32)
    act = (gate * jax.nn.sigmoid(gate) * up).astype(BF16)
    acc_sc[...] += jnp.dot(act, wd_ref[...], preferred_element_type=F32)

    @pl.when(f == pl.num_programs(1) - 1)
    def _():
        o_ref[...] = acc_sc[...]


def _swiglu(x, gain, wg, wu, wd, tm=512, tf=1408):
    t, d = x.shape
    ff = wg.shape[1]
    tm, tf = min(tm, t), min(tf, ff)
    assert t % tm == 0 and ff % tf == 0
    return pl.pallas_call(
        _swiglu_body,
        grid=(t // tm, ff // tf),
        in_specs=[pl.BlockSpec((tm, d), lambda i, f: (i, 0)),
                  pl.BlockSpec((1, d), lambda i, f: (0, 0)),
                  pl.BlockSpec((d, tf), lambda i, f: (0, f)),
                  pl.BlockSpec((d, tf), lambda i, f: (0, f)),
                  pl.BlockSpec((tf, d), lambda i, f: (f, 0))],
        out_specs=pl.BlockSpec((tm, d), lambda i, f: (i, 0)),
        out_shape=jax.ShapeDtypeStruct((t, d), F32),
        scratch_shapes=[pltpu.VMEM((tm, d), BF16), pltpu.VMEM((tm, d), F32)],
        compiler_params=_cparams("parallel", "arbitrary"),
        name="swiglu",
    )(x, gain.reshape(1, d).astype(F32), wg.astype(BF16), wu.astype(BF16), wd.astype(BF16))


def _router_body(x_ref, g_ref, r_ref, h_ref, rt_ref):
    h = _rms(x_ref[...], g_ref[...])
    h_ref[...] = h
    h_hi = h.astype(BF16)
    h_lo = (h - h_hi.astype(F32)).astype(BF16)
    r = r_ref[...]
    r_hi = r.astype(BF16)
    r_lo = (r - r_hi.astype(F32)).astype(BF16)
    logits = (jnp.dot(h_hi, r_hi, preferred_element_type=F32) + jnp.dot(h_lo, r_hi, preferred_element_type=F32)
              + jnp.dot(h_hi, r_lo, preferred_element_type=F32))
    lane = lax.broadcasted_iota(jnp.int32, logits.shape, 1)
    lg = jnp.where(lane < N_EXPERTS, logits, -jnp.inf)
    v0 = jnp.max(lg, axis=-1, keepdims=True)
    e0 = jnp.min(jnp.where(lg == v0, lane, LANES), axis=-1, keepdims=True)
    lg1 = jnp.where(lane == e0, -jnp.inf, lg)
    v1 = jnp.max(lg1, axis=-1, keepdims=True)
    e1 = jnp.min(jnp.where(lg1 == v1, lane, LANES), axis=-1, keepdims=True)
    g1 = 1.0 / (1.0 + jnp.exp(v0 - v1))
    g0 = 1.0 - g1
    out = jnp.where(lane == 0, g0, jnp.where(lane == 1, g1, 0.0))
    out = jnp.where(lane == 2, e0.astype(F32), jnp.where(lane == 3, e1.astype(F32), out))
    rt_ref[...] = out


def _router(x, gain, router, tm=512):
    t, d = x.shape
    tm = min(tm, t)
    r_pad = jnp.zeros((d, LANES), F32).at[:, :N_EXPERTS].set(router.astype(F32))
    return pl.pallas_call(
        _router_body,
        grid=(t // tm,),
        in_specs=[pl.BlockSpec((tm, d), lambda i: (i, 0)), pl.BlockSpec((1, d), lambda i: (0, 0)),
                  pl.BlockSpec((d, LANES), lambda i: (0, 0))],
        out_specs=[pl.BlockSpec((tm, d), lambda i: (i, 0)), pl.BlockSpec((tm, LANES), lambda i: (i, 0))],
        out_shape=[jax.ShapeDtypeStruct((t, d), F32), jax.ShapeDtypeStruct((t, LANES), F32)],
        compiler_params=_cparams("parallel"),
        name="moe_router",
    )(x, gain.reshape(1, d).astype(F32), r_pad)


def _row_copy(src_hbm, src_row, dst_ref, dst_row, sem):
    return pltpu.make_async_copy(src_hbm.at[pl.ds(src_row, 1)], dst_ref.at[pl.ds(dst_row, 1)], sem)


def _experts_body(be_ref, nb_ref, tok_ref, h_hbm, wg_ref, wu_ref, wd_ref, o_ref, xf_sc, xb_sc, acc_sc, sem, *,
                  n_blk, n_f):
    i, f = pl.program_id(0), pl.program_id(1)
    rows = xf_sc.shape[1]
    part = rows // n_f
    slot = i & 1
    nb = nb_ref[0]

    @pl.when((i == 0) & (f == 0))
    def _():
        for r in range(rows):
            _row_copy(h_hbm, tok_ref[r], xf_sc.at[0], r, sem.at[0]).start()

    @pl.when((f == 0) & (i <= nb))
    def _():
        for r in range(rows):
            _row_copy(h_hbm, 0, xf_sc.at[slot], r, sem.at[slot]).wait()

    @pl.when(i < nb)
    def _():
        @pl.when(f == 0)
        def _():
            xb_sc[...] = xf_sc[slot].astype(BF16)
            acc_sc[...] = jnp.zeros(acc_sc.shape, F32)

        base = jnp.minimum(i + 1, n_blk - 1) * rows + f * part
        for r in range(part):
            _row_copy(h_hbm, tok_ref[base + r], xf_sc.at[1 - slot], f * part + r, sem.at[1 - slot]).start()

        x = xb_sc[...]
        gate = jnp.dot(x, wg_ref[0], preferred_element_type=F32)
        up = jnp.dot(x, wu_ref[0], preferred_element_type=F32)
        act = (gate * jax.nn.sigmoid(gate) * up).astype(BF16)
        acc_sc[...] += jnp.dot(act, wd_ref[0], preferred_element_type=F32)

        @pl.when(f == n_f - 1)
        def _():
            o_ref[...] = acc_sc[...]

    @pl.when(i >= nb)
    def _():
        o_ref[...] = jnp.zeros(o_ref.shape, F32)


def _experts(h, row_tok, blk_e, n_used, wg, wu, wd, rows, tf=1792):
    n_rows = row_tok.shape[0]
    d = h.shape[1]
    ff = wg.shape[2]
    n_blk = n_rows // rows
    last = ff // tf - 1
    grid_spec = pltpu.PrefetchScalarGridSpec(
        num_scalar_prefetch=3,
        grid=(n_blk, ff // tf),
        in_specs=[pl.BlockSpec(memory_space=pl.ANY),
                  pl.BlockSpec((1, d, tf), lambda i, f, be, nb, tok: (be[i], 0, jnp.where(i < nb[0], f, last))),
                  pl.BlockSpec((1, d, tf), lambda i, f, be, nb, tok: (be[i], 0, jnp.where(i < nb[0], f, last))),
                  pl.BlockSpec((1, tf, d), lambda i, f, be, nb, tok: (be[i], jnp.where(i < nb[0], f, last), 0))],
        out_specs=pl.BlockSpec((rows, d), lambda i, f, be, nb, tok: (i, 0)),
        scratch_shapes=[pltpu.VMEM((2, rows, d), F32), pltpu.VMEM((rows, d), BF16), pltpu.VMEM((rows, d), F32),
                        pltpu.SemaphoreType.DMA((2,))],
    )
    assert n_blk * rows > TOP_K * h.shape[0] + N_EXPERTS * (rows - 1)
    return pl.pallas_call(
        functools.partial(_experts_body, n_blk=n_blk, n_f=ff // tf),
        grid_spec=grid_spec,
        out_shape=jax.ShapeDtypeStruct((n_rows, d), F32),
        compiler_params=_cparams("arbitrary", "arbitrary"),
        name="moe_experts",
    )(blk_e, n_used, row_tok, h, wg, wu, wd)


def _combine_body(dest_ref, x_ref, rt_ref, ys_hbm, o_ref, buf, sem):
    tm = x_ref.shape[0]
    i = pl.program_id(0)
    slot = i & 1

    def start(tile, s):
        def body(r, c):
            for k in range(TOP_K):
                _row_copy(ys_hbm, dest_ref[(tile * tm + r) * TOP_K + k], buf.at[s, k], r, sem.at[s]).start()
            return c
        lax.fori_loop(0, tm, body, 0, unroll=8)

    @pl.when(i == 0)
    def _():
        start(0, 0)

    @pl.when(i + 1 < pl.num_programs(0))
    def _():
        start(i + 1, 1 - slot)

    for r in range(tm):
        for k in range(TOP_K):
            _row_copy(ys_hbm, 0, buf.at[slot, k], r, sem.at[slot]).wait()
    rt = rt_ref[...]
    o_ref[...] = x_ref[...] + rt[:, 0:1] * buf[slot, 0] + rt[:, 1:2] * buf[slot, 1]


def _combine(x, routing, ys, dest, tm=256):
    t, d = x.shape
    tm = min(tm, t)
    grid_spec = pltpu.PrefetchScalarGridSpec(
        num_scalar_prefetch=1,
        grid=(t // tm,),
        in_specs=[pl.BlockSpec((tm, d), lambda i, dest: (i, 0)),
                  pl.BlockSpec((tm, LANES), lambda i, dest: (i, 0)),
                  pl.BlockSpec(memory_space=pl.ANY)],
        out_specs=pl.BlockSpec((tm, d), lambda i, dest: (i, 0)),
        scratch_shapes=[pltpu.VMEM((2, TOP_K, tm, d), F32), pltpu.SemaphoreType.DMA((2,))],
    )
    return pl.pallas_call(
        _combine_body,
        grid_spec=grid_spec,
        out_shape=jax.ShapeDtypeStruct((t, d), F32),
        compiler_params=_cparams("arbitrary"),
        name="moe_combine",
    )(dest, x, routing, ys)


MOE_ROWS = 512


def _moe(x, gain, router, w_gate, w_up, w_down):
    t, d = x.shape
    h, routing = _router(x, gain, router)
    n_assign = t * TOP_K
    flat_e = routing[:, 2:2 + TOP_K].astype(jnp.int32).reshape(-1)
    onehot = (flat_e[:, None] == jnp.arange(N_EXPERTS)[None, :]).astype(jnp.int32)
    before = jnp.cumsum(onehot, axis=0) - onehot
    counts = jnp.sum(onehot, axis=0)
    padded = (counts + MOE_ROWS - 1) // MOE_ROWS * MOE_ROWS
    pend = jnp.cumsum(padded)
    pstart = pend - padded
    dest = (jnp.sum((pstart[None, :] + before) * onehot, axis=1)).astype(jnp.int32)
    n_rows = -(-n_assign // MOE_ROWS) * MOE_ROWS + N_EXPERTS * MOE_ROWS
    n_blk = n_rows // MOE_ROWS
    row_tok = jnp.zeros((n_rows,), jnp.int32).at[dest].set(jnp.arange(n_assign, dtype=jnp.int32) // TOP_K)
    blk_start = jnp.arange(n_blk, dtype=jnp.int32) * MOE_ROWS
    blk_e = jnp.minimum(jnp.sum((pend[None, :] <= blk_start[:, None]).astype(jnp.int32), axis=1), N_EXPERTS - 1)
    n_used = (pend[-1] // MOE_ROWS).astype(jnp.int32).reshape(1)
    ys = _experts(h, row_tok, blk_e.astype(jnp.int32), n_used, w_gate.astype(BF16), w_up.astype(BF16),
                  w_down.astype(BF16), MOE_ROWS)
    return _combine(x, routing, ys, dest)


def _final_norm_body(x_ref, g_ref, o_ref):
    o_ref[...] = _rms(x_ref[...], g_ref[...])


def _final_norm(x, gain, tm=1024):
    t, d = x.shape
    tm = min(tm, t)
    return pl.pallas_call(
        _final_norm_body,
        grid=(t // tm,),
        in_specs=[pl.BlockSpec((tm, d), lambda i: (i, 0)), pl.BlockSpec((1, d), lambda i: (0, 0))],
        out_specs=pl.BlockSpec((tm, d), lambda i: (i, 0)),
        out_shape=jax.ShapeDtypeStruct((t, d), F32),
        compiler_params=_cparams("parallel"),
        name="final_norm",
    )(x, gain.reshape(1, d).astype(F32))


def _all_rope_tables(s):
    return {"g32w512": _rope_tables(s, 32, 512), "g64w384": _rope_tables(s, 64, 384),
            "g32w128": _rope_tables(s, 32, 128), "g64w512": _rope_tables(s, 64, 512)}


def kernel(x, mem, norm_mix, w_in, w_out, diff_lambda, diff_subln, nsa_cmp_pos, nsa_cmp_w1, nsa_cmp_w2, mla_q_norm, mla_kv_norm, mla_w_uq, mla_w_ukv, norm_cross, norm_mem, cross_wq, cross_wk, cross_wv, cross_wo, norm_ffn, ffn_w_gate, ffn_w_up, ffn_w_down, moe_router, moe_w_gate, moe_w_up, moe_w_down, final_norm):
    b, s, d = x.shape
    depth = w_in.shape[0]
    tabs = _all_rope_tables(s)
    for l in range(depth):
        x = _token_mixers(x, l, tabs, norm_mix[l], w_in[l], w_out[l], diff_lambda[l], diff_subln[l],
                          nsa_cmp_pos[l], nsa_cmp_w1[l], nsa_cmp_w2[l], mla_q_norm[l], mla_kv_norm[l],
                          mla_w_uq[l], mla_w_ukv[l])
        x = _cross_attention(x, mem, norm_cross[l], norm_mem[l], cross_wq[l], cross_wk[l], cross_wv[l],
                             cross_wo[l])
        xt = x.reshape(b * s, d)
        i = l // 2
        if l % 2 == 0:
            xt = _swiglu(xt, norm_ffn[l], ffn_w_gate[i], ffn_w_up[i], ffn_w_down[i])
        else:
            xt = _moe(xt, norm_ffn[l], moe_router[i], moe_w_gate[i], moe_w_up[i], moe_w_down[i])
        x = xt.reshape(b, s, d)
    return _final_norm(x.reshape(b * s, d), final_norm).reshape(b, s, d)
```

```python
import functools
import math

import numpy as np
import jax
import jax.numpy as jnp
from jax import lax
from jax.experimental import pallas as pl
from jax.experimental.pallas import tpu as pltpu

F32 = jnp.float32
BF16 = jnp.bfloat16

HEAD_DIM = 64
GROUP_HEADS = 4
GROUP_WIDTH = HEAD_DIM * GROUP_HEADS
ROPE_THETA = 10000.0
NORM_EPS = 1e-6
DIFF_DK = HEAD_DIM // 2
CMP_LEN = 32
CMP_STRIDE = 16
SEL_BLOCK = 64
SEL_SHIFT = 6
SEL_TOPN = 16
NSA_WINDOW = 512
FORCED_SCORE = 1e6
MLA_NOPE = 64
MLA_ROPE = 32
MLA_DV = 64
DILATED_PATTERNS = ((128, 1), (512, 4), (2048, 16))
CROSS_HEADS = 4
N_EXPERTS = 8
TOP_K = 2

LANES = 128
V7X_VMEM_BYTES = 64 * 1024 * 1024
VMEM_LIMIT = V7X_VMEM_BYTES * 7 // 8

ONES_ROWS = 16
NEG_BIG = -1e30
SEL_BIAS = -1e9
LOG2E = math.log2(math.e)
LN2 = math.log(2.0)


def _cparams(*sem):
    return pltpu.CompilerParams(dimension_semantics=sem, vmem_limit_bytes=VMEM_LIMIT)


def _rms(xf, gain):
    ms = jnp.mean(xf * xf, axis=-1, keepdims=True)
    return xf * lax.rsqrt(ms + NORM_EPS) * gain


def _mm_body(*refs, nx, norm, res, splits):
    x_refs = refs[:nx]
    pos = nx
    g_ref = refs[pos] if norm else None
    pos += int(norm)
    w_ref = refs[pos]
    pos += 1
    r_ref = refs[pos] if res else None
    pos += int(res)
    o_refs = refs[pos:pos + len(splits)]
    xs_ref = refs[pos + len(splits)]

    @pl.when(pl.program_id(1) == 0)
    def _():
        off = 0
        for xr in x_refs:
            xv = xr[...]
            if norm:
                xv = _rms(xv.astype(F32), g_ref[...])
            kw = xv.shape[-1]
            xs_ref[:, off:off + kw] = xv.astype(BF16)
            off += kw

    acc = jnp.dot(xs_ref[...], w_ref[...], preferred_element_type=F32)
    if res:
        acc = acc + r_ref[...]
    off = 0
    for o_ref, width in zip(o_refs, splits):
        o_ref[...] = acc[:, off:off + width].astype(o_ref.dtype)
        off += width


def _mm(xs, w, *, gain=None, res=None, tm=512, tn=None, splits=None, out_dtype=F32, name="mm"):
    t = xs[0].shape[0]
    k, n = w.shape
    assert sum(x.shape[1] for x in xs) == k
    tn = n if tn is None else tn
    splits = (tn,) if splits is None else tuple(splits)
    assert sum(splits) == tn and (len(splits) == 1 or tn == n)
    tm = min(tm, t)
    assert t % tm == 0 and n % tn == 0
    norm = gain is not None
    assert not norm or len(xs) == 1
    in_specs = [pl.BlockSpec((tm, x.shape[1]), lambda i, j: (i, 0)) for x in xs]
    args = list(xs)
    if norm:
        in_specs.append(pl.BlockSpec((1, k), lambda i, j: (0, 0)))
        args.append(gain.reshape(1, k).astype(F32))
    in_specs.append(pl.BlockSpec((k, tn), lambda i, j: (0, j)))
    args.append(w)
    if res is not None:
        in_specs.append(pl.BlockSpec((tm, tn), lambda i, j: (i, j)))
        args.append(res)
    if len(splits) == 1:
        out_shape = [jax.ShapeDtypeStruct((t, n), out_dtype)]
        out_specs = [pl.BlockSpec((tm, tn), lambda i, j: (i, j))]
    else:
        out_shape = [jax.ShapeDtypeStruct((t, s), out_dtype) for s in splits]
        out_specs = [pl.BlockSpec((tm, s), lambda i, j: (i, 0)) for s in splits]
    outs = pl.pallas_call(
        functools.partial(_mm_body, nx=len(xs), norm=norm, res=res is not None, splits=splits),
        grid=(t // tm, n // tn),
        in_specs=in_specs,
        out_specs=out_specs,
        out_shape=out_shape,
        scratch_shapes=[pltpu.VMEM((tm, k), BF16)],
        compiler_params=_cparams("parallel", "arbitrary"),
        name=name,
    )(*args)
    return outs[0] if len(outs) == 1 else outs


def _rope_tables(s, group):
    half = group // 2
    pos = jnp.arange(s, dtype=F32)
    inv_freq = ROPE_THETA ** (-jnp.arange(half, dtype=F32) / half)
    ang = pos[:, None] * inv_freq[None, :]
    cos, sin = jnp.cos(ang), jnp.sin(ang)
    zero = jnp.zeros_like(sin)
    reps = LANES // group
    c = jnp.tile(jnp.concatenate([cos, cos], -1), (1, reps))
    s1 = jnp.tile(jnp.concatenate([zero, sin], -1), (1, reps))
    s2 = jnp.tile(jnp.concatenate([-sin, zero], -1), (1, reps))
    return c, s1, s2


def _rope(x, tabs, half):
    w = x.shape[-1]
    c, s1, s2 = (jnp.tile(t, (1, w // LANES)) for t in tabs)
    return x * c + pltpu.roll(x, half, 1) * s1 + pltpu.roll(x, w - half, 1) * s2


def _prep_a(z, t32, q_ref, k_ref, vt_ref):
    qk = _rope(z[:, :512], t32, DIFF_DK // 2)
    lane = lax.broadcasted_iota(jnp.int32, (z.shape[0], HEAD_DIM), 1)
    scale = DIFF_DK ** -0.5 * LOG2E
    for h in range(GROUP_HEADS):
        q = qk[:, h * 64:(h + 1) * 64] * scale
        q_ref[0, h, 0] = jnp.where(lane < DIFF_DK, q, 0.0).astype(BF16)
        q_ref[0, h, 1] = jnp.where(lane >= DIFF_DK, q, 0.0).astype(BF16)
        k_ref[0, h] = qk[:, 256 + h * 64:256 + (h + 1) * 64].astype(BF16)
    vt_ref[0] = z[:, 512:768].T.reshape(vt_ref.shape[1:]).astype(BF16)


def _prep_b(z, t64, qn_ref, qr_ref, ka_ref, kw_ref, kc_ref, vc_ref, vs_ref, vw_ref, g_ref):
    ts = z.shape[0]
    scale = HEAD_DIM ** -0.5 * LOG2E
    rp = _rope(z[:, :384], t64, HEAD_DIM // 2)
    for h in range(GROUP_HEADS):
        qn_ref[0, h] = (z[:, h * 64:(h + 1) * 64] * scale).astype(BF16)
        qr_ref[0, h] = (rp[:, h * 64:(h + 1) * 64] * scale).astype(BF16)
    kpos = pl.program_id(0) * ts + lax.broadcasted_iota(jnp.int32, (ts, SEL_BLOCK), 0)
    lane = lax.broadcasted_iota(jnp.int32, (ts, SEL_BLOCK), 1)
    onehot = jnp.where((kpos >> SEL_SHIFT) == lane, 1.0, 0.0)
    ka_ref[0] = jnp.concatenate([rp[:, 256:320], onehot], axis=-1).astype(BF16)
    kw_ref[0] = rp[:, 320:384].astype(BF16)
    kc_ref[0] = z[:, 384:448].astype(BF16)
    vc_ref[0] = z[:, 448:512].astype(BF16)
    vt = z[:, 512:640].T.astype(BF16)
    vs_ref[0] = vt[:HEAD_DIM]
    vw_ref[0] = vt[HEAD_DIM:]
    g_ref[0] = jax.nn.sigmoid(z[:, 640:768])


def _prep_c(z, t32, qg_ref, kg_ref, wq_ref, wkv_ref, q_ref, k_ref, vt_ref):
    scale = (MLA_NOPE + MLA_ROPE) ** -0.5 * LOG2E
    cq = _rms(z[:, :384], qg_ref[...]).astype(BF16)
    ckv = _rms(z[:, 384:512], kg_ref[...]).astype(BF16)
    qc = jnp.dot(cq, wq_ref[...], preferred_element_type=F32)
    kvc = jnp.dot(ckv, wkv_ref[...], preferred_element_type=F32)
    half = MLA_ROPE // 2
    q_rope = _rope(qc[:, 256:384], t32, half)
    k_rope = _rope(z[:, 512:640], t32, half)[:, :MLA_ROPE]
    for h in range(GROUP_HEADS):
        q = jnp.concatenate([qc[:, h * 64:(h + 1) * 64], q_rope[:, h * 32:(h + 1) * 32]], axis=-1)
        q_ref[0, h] = (q * scale).astype(BF16)
        k_ref[0, h] = jnp.concatenate([kvc[:, h * 64:(h + 1) * 64], k_rope], axis=-1).astype(BF16)
    vt_ref[0] = kvc[:, 256:512].T.reshape(vt_ref.shape[1:]).astype(BF16)


def _prep_d(z, t64, q_ref, k_ref, vt_ref):
    scale = HEAD_DIM ** -0.5 * LOG2E
    qk = _rope(z[:, :512], t64, HEAD_DIM // 2)
    for h in range(GROUP_HEADS):
        q_ref[0, h] = (qk[:, h * 64:(h + 1) * 64] * scale).astype(BF16)
        k_ref[0, h] = qk[:, 256 + h * 64:256 + (h + 1) * 64].astype(BF16)
    vt_ref[0] = z[:, 512:768].T.reshape(vt_ref.shape[1:]).astype(BF16)


Z_SPLITS = (768, 768, 640, 768)
N_PREP_OUT = (3, 9, 3, 3)


def _project_prep_body(*refs):
    x_ref, g_ref, w_ref = refs[:3]
    t64 = tuple(r[...] for r in refs[3:6])
    t32 = tuple(r[...] for r in refs[6:9])
    qg_ref, kg_ref, wq_ref, wkv_ref = refs[9:13]
    outs = refs[13:]
    h = _rms(x_ref[0], g_ref[...]).astype(BF16)
    z = jnp.dot(h, w_ref[...], preferred_element_type=F32)
    o0, o1, o2, o3 = np.cumsum((0,) + Z_SPLITS[:3])
    n0, n1, n2 = np.cumsum(N_PREP_OUT[:3])
    _prep_a(z[:, o0:o0 + Z_SPLITS[0]], t32, *outs[:n0])
    _prep_b(z[:, o1:o1 + Z_SPLITS[1]], t64, *outs[n0:n1])
    _prep_c(z[:, o2:o2 + Z_SPLITS[2]], t32, qg_ref, kg_ref, wq_ref, wkv_ref, *outs[n1:n2])
    _prep_d(z[:, o3:o3 + Z_SPLITS[3]], t64, *outs[n2:])


def _project_prep(x, gain, w, tabs64, tabs32, q_gain, kv_gain, wq, wkv, ts):
    b, s, d = x.shape
    dk = MLA_NOPE + MLA_ROPE
    full = lambda a: pl.BlockSpec(a.shape, lambda i, bi: (0,) * a.ndim)
    tspec = pl.BlockSpec((ts, LANES), lambda i, bi: (i, 0))
    h4 = lambda dd: pl.BlockSpec((1, 4, ts, dd), lambda i, bi: (bi, 0, i, 0))
    h4t = pl.BlockSpec((1, 4, 64, ts), lambda i, bi: (bi, 0, 0, i))
    sq = lambda dd: pl.BlockSpec((1, ts, dd), lambda i, bi: (bi, i, 0))
    t64 = pl.BlockSpec((1, 64, ts), lambda i, bi: (bi, 0, i))
    sd = lambda *shape, dt=BF16: jax.ShapeDtypeStruct(shape, dt)
    out_specs = [pl.BlockSpec((1, 4, 2, ts, 64), lambda i, bi: (bi, 0, 0, i, 0)), h4(64), h4t,
                 h4(64), h4(64), sq(128), sq(64), sq(64), sq(64), t64, t64, sq(128),
                 h4(dk), h4(dk), h4t,
                 h4(64), h4(64), h4t]
    out_shape = [sd(b, 4, 2, s, 64), sd(b, 4, s, 64), sd(b, 4, 64, s),
                 sd(b, 4, s, 64), sd(b, 4, s, 64), sd(b, s, 128), sd(b, s, 64), sd(b, s, 64), sd(b, s, 64),
                 sd(b, 64, s), sd(b, 64, s), sd(b, s, 128, dt=F32),
                 sd(b, 4, s, dk), sd(b, 4, s, dk), sd(b, 4, 64, s),
                 sd(b, 4, s, 64), sd(b, 4, s, 64), sd(b, 4, 64, s)]
    g2 = gain.reshape(1, d).astype(F32)
    outs = pl.pallas_call(
        _project_prep_body,
        grid=(s // ts, b),
        in_specs=[pl.BlockSpec((1, ts, d), lambda i, bi: (bi, i, 0)), full(g2), full(w)] + [tspec] * 6
                 + [full(q_gain), full(kv_gain), full(wq), full(wkv)],
        out_specs=out_specs,
        out_shape=out_shape,
        compiler_params=_cparams("parallel", "parallel"),
        name="w_in_prep",
    )(x, g2, w, *tabs64, *tabs32, q_gain, kv_gain, wq, wkv)
    n0, n1, n2 = np.cumsum(N_PREP_OUT[:3])
    return outs[:n0], outs[n0:n1], outs[n1:n2], outs[n2:]


def _flash_schedule(nq, tq, tk, sk, window):
    qi, kj, fl = [], [], []
    for i in range(nq):
        q_lo, q_hi = i * tq, i * tq + tq - 1
        j_hi = min(q_hi, sk - 1) // tk
        j_lo = 0 if window is None else max(0, q_lo - window) // tk
        for j in range(j_lo, j_hi + 1):
            k_min, k_max = j * tk, j * tk + tk - 1
            full = k_max <= q_lo and (window is None or q_hi - k_min <= window)
            qi.append(i)
            kj.append(j)
            fl.append((1 if j == j_lo else 0) | (2 if j == j_hi else 0) | (0 if full else 4))
    return (np.asarray(qi, np.int32), np.asarray(kj, np.int32), np.asarray(fl, np.int32))


def _flash_body(qi_ref, kj_ref, fl_ref, q_ref, k_ref, vt_ref, o_ref, m_sc, acc_sc, *,
                gb, hq, tq, tk, window, with_lse, merge_groups):
    t = pl.program_id(1)
    fl = fl_ref[t]
    rows = hq * tq
    dv = vt_ref.shape[1]

    @pl.when((fl & 1) != 0)
    def _():
        m_sc[...] = jnp.full(m_sc.shape, NEG_BIG, F32)
        acc_sc[...] = jnp.zeros(acc_sc.shape, F32)

    def step(masked):
        q = q_ref[...].reshape(gb, rows, q_ref.shape[-1])
        st = jnp.einsum("gkd,gqd->gkq", k_ref[...], q, preferred_element_type=F32)
        if masked:
            krow = lax.broadcasted_iota(jnp.int32, (tk, rows), 0)
            qcol = lax.broadcasted_iota(jnp.int32, (tk, rows), 1)
            if hq > 1:
                qcol = qcol & (tq - 1)
            dist = (qi_ref[t] * tq - kj_ref[t] * tk) + qcol - krow
            ok = dist >= 0
            if window is not None:
                ok = ok & (dist <= window)
            st = jnp.where(ok[None], st, NEG_BIG)
        m_prev = m_sc[...]
        m_new = jnp.maximum(m_prev, jnp.max(st, axis=1, keepdims=True))
        alpha = jnp.exp2(m_prev - m_new)
        p = jnp.exp2((st - m_new).astype(BF16))
        v_ext = jnp.concatenate([vt_ref[...], jnp.ones((gb, ONES_ROWS, tk), BF16)], axis=1)
        acc_sc[...] = alpha * acc_sc[...] + jnp.einsum("gdk,gkq->gdq", v_ext, p, preferred_element_type=F32)
        m_sc[...] = m_new

    @pl.when((fl & 4) != 0)
    def _():
        step(True)

    @pl.when((fl & 4) == 0)
    def _():
        step(False)

    @pl.when((fl & 2) != 0)
    def _():
        l = acc_sc[:, dv:dv + 1, :]
        o_t = acc_sc[:, :dv, :] / l
        if merge_groups:
            o_ref[0] = o_t.reshape(gb * dv, rows).T.astype(o_ref.dtype)
            return
        if with_lse:
            extra = jnp.broadcast_to(m_sc[...] * LN2 + jnp.log(l), o_t.shape)
        else:
            extra = jnp.zeros((gb, LANES - dv, rows), F32)
        dvo = o_ref.shape[-1]
        for g in range(gb):
            o = jnp.concatenate([o_t[g], extra[g]], axis=0).T
            o_ref[g] = o[:, :dvo].reshape(o_ref.shape[1:]).astype(o_ref.dtype)


def _flash(q, k, vt, *, tq, tk, gb=1, window=None, with_lse=False, merge_groups=False, out_dtype=F32,
           name="flash"):
    g, hq, sq, dk = q.shape
    _, dv, sk = vt.shape
    tq, tk = min(tq, sq), min(tk, sk)
    assert g % gb == 0 and sq % tq == 0 and sk % tk == 0
    assert tq & (tq - 1) == 0 and 2 * dv == LANES
    qi, kj, fl = _flash_schedule(sq // tq, tq, tk, sk, window)
    dvo = 2 * dv if with_lse else dv
    rows = hq * tq
    if merge_groups:
        assert hq == 1 and not with_lse and (gb * dv) % LANES == 0
        out_spec = pl.BlockSpec((1, tq, gb * dv), lambda gi, t, qi, kj, fl: (gi, qi[t], 0))
        out_shape = jax.ShapeDtypeStruct((g // gb, sq, gb * dv), out_dtype)
    else:
        out_spec = pl.BlockSpec((gb, hq, tq, dvo), lambda gi, t, qi, kj, fl: (gi, 0, qi[t], 0))
        out_shape = jax.ShapeDtypeStruct((g, hq, sq, dvo), out_dtype)
    grid_spec = pltpu.PrefetchScalarGridSpec(
        num_scalar_prefetch=3,
        grid=(g // gb, len(qi)),
        in_specs=[pl.BlockSpec((gb, hq, tq, dk), lambda gi, t, qi, kj, fl: (gi, 0, qi[t], 0)),
                  pl.BlockSpec((gb, tk, dk), lambda gi, t, qi, kj, fl: (gi, kj[t], 0)),
                  pl.BlockSpec((gb, dv, tk), lambda gi, t, qi, kj, fl: (gi, 0, kj[t]))],
        out_specs=out_spec,
        scratch_shapes=[pltpu.VMEM((gb, 1, rows), F32), pltpu.VMEM((gb, dv + ONES_ROWS, rows), F32)],
    )
    return pl.pallas_call(
        functools.partial(_flash_body, gb=gb, hq=hq, tq=tq, tk=tk, window=window, with_lse=with_lse,
                          merge_groups=merge_groups),
        grid_spec=grid_spec,
        out_shape=out_shape,
        compiler_params=_cparams("parallel", "arbitrary"),
        name=name,
    )(jnp.asarray(qi), jnp.asarray(kj), jnp.asarray(fl), q, k, vt)


def _gelu_tanh(x):
    return 0.5 * x * (1.0 + jnp.tanh(math.sqrt(2.0 / math.pi) * (x + 0.044715 * (x * x * x))))


def _compress_body(r_ref, pos_ref, w1_ref, w2_ref, o_ref):
    half = CMP_STRIDE * HEAD_DIM
    r = r_ref[0, 0]
    w1 = w1_ref[0]
    top = jnp.dot(r, w1[:half], preferred_element_type=F32)
    bot = jnp.dot(r, w1[half:], preferred_element_type=F32)
    nc = r.shape[0]
    posb = jnp.dot(pos_ref[0], w1, preferred_element_type=F32)[:1]
    pre = top + pltpu.roll(bot, nc - 1, 0) + posb
    hid = _gelu_tanh(pre).astype(BF16)
    o_ref[0, 0] = jnp.dot(hid, w2_ref[0], preferred_element_type=F32).astype(o_ref.dtype)


def _nsa_compress(kv_rows, pos_flat, w1, w2):
    _, b, nc, _ = kv_rows.shape
    return pl.pallas_call(
        _compress_body,
        grid=(2, b),
        in_specs=[pl.BlockSpec((1, 1, nc, 1024), lambda w, bi: (w, bi, 0, 0)),
                  pl.BlockSpec((1, 8, 2048), lambda w, bi: (w, 0, 0)),
                  pl.BlockSpec((1, 2048, 256), lambda w, bi: (w, 0, 0)),
                  pl.BlockSpec((1, 256, 64), lambda w, bi: (w, 0, 0))],
        out_specs=pl.BlockSpec((1, 1, nc, 64), lambda w, bi: (w, bi, 0, 0)),
        out_shape=jax.ShapeDtypeStruct((2, b, nc, 64), BF16),
        compiler_params=_cparams("parallel", "parallel"),
        name="nsa_compress",
    )(kv_rows, pos_flat, w1, w2)


def _cmp_select_body(qn_ref, qr_ref, kc_ref, vc_ref, cov_ref, o_ref, qa_ref, *, tq, n_sel, k_top):
    ncp = kc_ref.shape[2]
    pos = pl.program_id(1) * tq + lax.broadcasted_iota(jnp.int32, (tq, ncp), 0)
    blk_end = lax.broadcasted_iota(jnp.int32, (tq, ncp), 1) * CMP_STRIDE + (CMP_LEN - 1)
    mask = blk_end <= pos
    kc, vc = kc_ref[0, 0], vc_ref[0, 0]
    p_sum = jnp.zeros((tq, ncp), F32)
    for h in range(GROUP_HEADS):
        s = lax.dot_general(qn_ref[0, h], kc, (((1,), (1,)), ((), ())), preferred_element_type=F32)
        s = jnp.where(mask, s, NEG_BIG)
        mx = jnp.max(s, axis=-1, keepdims=True)
        mx = jnp.where(mx > 0.5 * NEG_BIG, mx, 0.0)
        e = jnp.where(mask, jnp.exp2(s - mx), 0.0)
        p = e / jnp.maximum(jnp.sum(e, axis=-1, keepdims=True), 1e-30)
        o_ref[0, h] = jnp.dot(p.astype(BF16), vc, preferred_element_type=F32).astype(o_ref.dtype)
        p_sum = p_sum + p
    p_hi = p_sum.astype(BF16)
    p_lo = (p_sum - p_hi.astype(F32)).astype(BF16)
    cov = cov_ref[...]
    imp = (jnp.dot(p_hi, cov, preferred_element_type=F32) + jnp.dot(p_lo, cov, preferred_element_type=F32))
    nblk = SEL_BLOCK
    imp_t = imp.T[:nblk]
    tpos = pl.program_id(1) * tq + lax.broadcasted_iota(jnp.int32, (nblk, tq), 1)
    cur = tpos >> SEL_SHIFT
    j = lax.broadcasted_iota(jnp.int32, (nblk, tq), 0)
    valid = j <= cur
    forced = (j == 0) | (j == cur) | (j == cur - 1)
    score = jnp.where(valid, jnp.where(forced, FORCED_SCORE, imp_t), -jnp.inf)
    sub = 8
    groups = [score[g * sub:(g + 1) * sub] for g in range(nblk // sub)]
    ranks = [jnp.zeros((sub, tq), jnp.int32) for _ in groups]
    jj = lax.broadcasted_iota(jnp.int32, (sub, tq), 0)
    for i in range(n_sel):
        row = jnp.broadcast_to(score[i:i + 1], (sub, tq))
        for g, sg in enumerate(groups):
            if i < g * sub:
                ahead = row >= sg
            elif i >= (g + 1) * sub:
                ahead = row > sg
            else:
                ahead = (row > sg) | ((row == sg) & (i - g * sub < jj))
            ranks[g] = ranks[g] + ahead.astype(jnp.int32)
    rank = jnp.concatenate(ranks, axis=0)
    bias_t = jnp.where(valid & (rank < k_top), 0.0, SEL_BIAS)
    bias = jnp.concatenate([bias_t, jnp.zeros((LANES - nblk, tq), F32)], axis=0).T[:, :nblk].astype(BF16)
    for h in range(GROUP_HEADS):
        qa_ref[0, h] = jnp.concatenate([qr_ref[0, h], bias], axis=-1)


def _cover_matrix(ncp, n_sel, lanes):
    c_start = np.arange(ncp)[:, None] * CMP_STRIDE
    s_start = np.arange(lanes)[None, :] * SEL_BLOCK
    cov = np.clip(np.minimum(c_start + CMP_LEN, s_start + SEL_BLOCK) - np.maximum(c_start, s_start), 0, None)
    cov = cov.astype(np.float32) / CMP_LEN
    cov[:, n_sel:] = 0.0
    cov[ncp - 1:, :] = 0.0
    return cov


def _cmp_select(qn, qr, kvc, tq):
    b, _, s, _ = qn.shape
    ncp = kvc.shape[2]
    n_sel = s // SEL_BLOCK
    assert n_sel <= SEL_BLOCK
    k_top = min(SEL_TOPN, n_sel)
    cov = jnp.asarray(_cover_matrix(ncp, n_sel, LANES), BF16)
    h4 = lambda d: pl.BlockSpec((1, 4, tq, d), lambda bi, i: (bi, 0, i, 0))
    return pl.pallas_call(
        functools.partial(_cmp_select_body, tq=tq, n_sel=n_sel, k_top=k_top),
        grid=(b, s // tq),
        in_specs=[h4(64), h4(64),
                  pl.BlockSpec((1, 1, ncp, 64), lambda bi, i: (0, bi, 0, 0)),
                  pl.BlockSpec((1, 1, ncp, 64), lambda bi, i: (1, bi, 0, 0)),
                  pl.BlockSpec(cov.shape, lambda bi, i: (0, 0))],
        out_specs=[h4(64), h4(128)],
        out_shape=[jax.ShapeDtypeStruct((b, 4, s, 64), BF16), jax.ShapeDtypeStruct((b, 4, s, 128), BF16)],
        compiler_params=_cparams("parallel", "parallel"),
        name="nsa_cmp_select",
    )(qn, qr, kvc, kvc, cov)


def _merge_a_body(o_ref, lam_ref, g_ref, y_ref, *, out_scale):
    outs = []
    for h in range(GROUP_HEADS):
        o = o_ref[0, h, 0].astype(F32) - lam_ref[...] * o_ref[0, h, 1].astype(F32)
        outs.append(_rms(o, g_ref[...]) * out_scale)
    y_ref[0] = jnp.concatenate(outs, axis=-1).astype(y_ref.dtype)


def _merge_a(o, lam, subln, out_scale, ts):
    b, _, _, s, _ = o.shape
    return pl.pallas_call(
        functools.partial(_merge_a_body, out_scale=out_scale),
        grid=(b, s // ts),
        in_specs=[pl.BlockSpec((1, 4, 2, ts, 64), lambda bi, i: (bi, 0, 0, i, 0)),
                  pl.BlockSpec((1, 64), lambda bi, i: (0, 0)),
                  pl.BlockSpec((1, 64), lambda bi, i: (0, 0))],
        out_specs=pl.BlockSpec((1, ts, 256), lambda bi, i: (bi, i, 0)),
        out_shape=jax.ShapeDtypeStruct((b, s, 256), BF16),
        compiler_params=_cparams("parallel", "parallel"),
        name="merge_a",
    )(o, lam, subln)


def _merge_b_body(oc_ref, os_ref, ow_ref, g_ref, y_ref):
    g = g_ref[0]
    outs = []
    for h in range(GROUP_HEADS):
        outs.append(g[:, 3 * h:3 * h + 1] * oc_ref[0, h].astype(F32)
                    + g[:, 3 * h + 1:3 * h + 2] * os_ref[0, h].astype(F32)
                    + g[:, 3 * h + 2:3 * h + 3] * ow_ref[0, h].astype(F32))
    y_ref[0] = jnp.concatenate(outs, axis=-1).astype(y_ref.dtype)


def _merge_b(o_cmp, o_sel, o_win, gates, ts):
    b, _, s, _ = o_cmp.shape
    h4 = pl.BlockSpec((1, 4, ts, 64), lambda bi, i: (bi, 0, i, 0))
    return pl.pallas_call(
        _merge_b_body,
        grid=(b, s // ts),
        in_specs=[h4, h4, h4, pl.BlockSpec((1, ts, 128), lambda bi, i: (bi, i, 0))],
        out_specs=pl.BlockSpec((1, ts, 256), lambda bi, i: (bi, i, 0)),
        out_shape=jax.ShapeDtypeStruct((b, s, 256), BF16),
        compiler_params=_cparams("parallel", "parallel"),
        name="merge_b",
    )(o_cmp, o_sel, o_win, gates)


def _merge_d_body(o1_ref, o2_ref, o3_ref, y_ref):
    outs = []
    for h in range(GROUP_HEADS):
        e = [r[0, h] for r in (o1_ref, o2_ref, o3_ref)]
        lse = [x[:, 64:] for x in e]
        mx = jnp.maximum(jnp.maximum(lse[0], lse[1]), lse[2])
        w = [jnp.exp(x - mx) for x in lse]
        den = w[0] + w[1] + w[2]
        outs.append((w[0] * e[0][:, :64] + w[1] * e[1][:, :64] + w[2] * e[2][:, :64]) / den)
    y_ref[0] = jnp.concatenate(outs, axis=-1).astype(y_ref.dtype)


def _merge_d(o1, o2, o3, ts):
    b, _, s, _ = o1.shape
    h4 = pl.BlockSpec((1, 4, ts, 128), lambda bi, i: (bi, 0, i, 0))
    return pl.pallas_call(
        _merge_d_body,
        grid=(b, s // ts),
        in_specs=[h4, h4, h4],
        out_specs=pl.BlockSpec((1, ts, 256), lambda bi, i: (bi, i, 0)),
        out_shape=jax.ShapeDtypeStruct((b, s, 256), BF16),
        compiler_params=_cparams("parallel", "parallel"),
        name="merge_d",
    )(o1, o2, o3)


def _permute_w_in(w_in):
    a_cols = 768
    b0 = a_cols
    nq = 256
    seg = lambda i: w_in[:, b0 + nq + i * 64: b0 + nq + (i + 1) * 64]
    kc, vc, ks, vs, kw, vw = (seg(i) for i in range(6))
    gb = w_in[:, b0 + nq + 384: b0 + nq + 384 + 12]
    d = w_in.shape[0]
    w_b = jnp.concatenate([w_in[:, b0:b0 + nq], ks, kw, kc, vc, vs, vw, gb, jnp.zeros((d, 116), w_in.dtype)], axis=1)
    c0 = b0 + 652
    w_c = jnp.concatenate([w_in[:, c0:c0 + 544], jnp.zeros((d, 96), w_in.dtype)], axis=1)
    d0 = c0 + 544
    return jnp.concatenate([w_in[:, :a_cols], w_b, w_c, w_in[:, d0:d0 + 768]], axis=1).astype(BF16)


def _residue_classes(t, dil):
    g, s, d = t.shape
    return t.reshape(g, s // dil, dil, d).transpose(0, 2, 1, 3).reshape(g * dil, s // dil, d)


def _residue_classes_t(t, dil):
    g, d, s = t.shape
    return t.reshape(g, d, s // dil, dil).transpose(0, 3, 1, 2).reshape(g * dil, d, s // dil)


def _from_residue_classes(t, dil):
    gd, l, d = t.shape
    return t.reshape(gd // dil, dil, l, d).transpose(0, 2, 1, 3).reshape(gd // dil, l * dil, d)


def _token_mixers(x, layer, rope_tabs, norm_g, w_in, w_out, diff_lambda, diff_subln, cmp_pos, cmp_w1, cmp_w2,
                  q_norm, kv_norm, w_uq, w_ukv):
    b, s, d = x.shape
    t = b * s
    ts = min(512, s)
    wq = w_uq.reshape(-1, 4, MLA_NOPE + MLA_ROPE)
    wq = jnp.concatenate([wq[:, :, :MLA_NOPE].reshape(-1, 256), wq[:, :, MLA_NOPE:].reshape(-1, 128)], axis=1)
    wkv = w_ukv.reshape(-1, 4, MLA_NOPE + MLA_DV)
    wkv = jnp.concatenate([wkv[:, :, :MLA_NOPE].reshape(-1, 256), wkv[:, :, MLA_NOPE:].reshape(-1, 256)], axis=1)
    prep_a, prep_b, prep_c, prep_d = _project_prep(
        x, norm_g, _permute_w_in(w_in), rope_tabs[64], rope_tabs[32], q_norm.reshape(1, -1).astype(F32),
        kv_norm.reshape(1, -1).astype(F32), wq.astype(BF16), wkv.astype(BF16), ts)

    qa, ka, va = prep_a
    oa = _flash(qa.reshape(b * 4, 2, s, 64), ka.reshape(b * 4, s, 64), va.reshape(b * 4, 64, s),
                tq=512, tk=512, gb=2, out_dtype=BF16, name="flash_diff").reshape(b, 4, 2, s, 64)
    lam_init = 0.8 - 0.6 * math.exp(-0.3 * layer)
    lp = diff_lambda.astype(F32)
    lam = jnp.exp(jnp.sum(lp[0] * lp[1])) - jnp.exp(jnp.sum(lp[2] * lp[3])) + lam_init
    y_a = _merge_a(oa, jnp.full((1, 64), lam, F32), diff_subln.reshape(1, 64).astype(F32), 1.0 - lam_init, ts)

    qn, qr, k_aug, kw, kc, vc, vs, vw, gates = prep_b
    kv_rows = jnp.stack([kc, vc]).reshape(2, b, s // CMP_STRIDE, CMP_STRIDE * HEAD_DIM)
    pos_flat = jnp.broadcast_to(cmp_pos.reshape(2, 1, CMP_LEN * HEAD_DIM), (2, 8, CMP_LEN * HEAD_DIM)).astype(BF16)
    kvc = _nsa_compress(kv_rows, pos_flat, cmp_w1.astype(BF16), cmp_w2.astype(BF16))
    o_cmp, q_aug = _cmp_select(qn, qr, kvc, min(256, s))
    o_sel = _flash(q_aug, k_aug, vs, tq=256, tk=512, out_dtype=BF16, name="flash_sel")
    o_win = _flash(qr, kw, vw, tq=256, tk=256, window=NSA_WINDOW - 1, out_dtype=BF16, name="flash_win")
    y_b = _merge_b(o_cmp, o_sel, o_win, gates, ts)

    qc, kcat, vcv = prep_c
    dk = MLA_NOPE + MLA_ROPE
    y_c = _flash(qc.reshape(b * 4, 1, s, dk), kcat.reshape(b * 4, s, dk), vcv.reshape(b * 4, 64, s),
                 tq=512, tk=512, gb=4, merge_groups=True, out_dtype=BF16, name="flash_mla")

    qd, kd, vd = prep_d
    qd, kd, vd = qd.reshape(b * 4, s, 64), kd.reshape(b * 4, s, 64), vd.reshape(b * 4, 64, s)
    o_d = []
    for window, dil in DILATED_PATTERNS:
        l = s // dil
        qq, kk, vv = _residue_classes(qd, dil), _residue_classes(kd, dil), _residue_classes_t(vd, dil)
        g = qq.shape[0]
        gb = math.gcd(g, 16)
        o = _flash(qq.reshape(g, 1, l, 64), kk, vv, tq=128, tk=128, gb=gb, window=window // dil,
                   with_lse=True, name=f"flash_dil{dil}")
        o_d.append(_from_residue_classes(o.reshape(g, l, 128), dil).reshape(b, 4, s, 128))
    y_d = _merge_d(*o_d, ts)

    ys = [y.reshape(t, 256) for y in (y_a, y_b, y_c, y_d)]
    return _mm(ys, w_out.astype(BF16), res=x.reshape(t, d), tm=min(512, t), name="w_out").reshape(b, s, d)


def _cross_body(x_ref, g_ref, wq_ref, k_ref, v_ref, wo_ref, o_ref):
    x = x_ref[0]
    h = _rms(x, g_ref[...]).astype(BF16)
    q = jnp.dot(h, wq_ref[...], preferred_element_type=F32).astype(BF16)
    dh = q.shape[-1] // CROSS_HEADS
    outs = []
    for hd in range(CROSS_HEADS):
        sl = slice(hd * dh, (hd + 1) * dh)
        s = lax.dot_general(q[:, sl], k_ref[0, :, sl], (((1,), (1,)), ((), ())), preferred_element_type=F32)
        e = jnp.exp(s - jnp.max(s, axis=-1, keepdims=True))
        p = e / jnp.sum(e, axis=-1, keepdims=True)
        outs.append(jnp.dot(p.astype(BF16), v_ref[0, :, sl], preferred_element_type=F32))
    o = jnp.concatenate(outs, axis=-1).astype(BF16)
    o_ref[0] = x + jnp.dot(o, wo_ref[...], preferred_element_type=F32)


def _cross_attention(x, mem, g_x, g_mem, wq, wk, wv, wo):
    b, s, d = x.shape
    m = mem.shape[1]
    ts = min(512, s)
    dh = d // CROSS_HEADS
    k, v = _mm([mem.reshape(b * m, d)], jnp.concatenate([wk, wv], axis=1).astype(BF16), gain=g_mem,
               tm=min(512, b * m), splits=(d, d), out_dtype=BF16, name="cross_kv")
    full = lambda a: pl.BlockSpec(a.shape, lambda bi, i: (0,) * a.ndim)
    wq_s = (wq * dh ** -0.5).astype(BF16)
    wo_b = wo.astype(BF16)
    g2 = g_x.reshape(1, d).astype(F32)
    return pl.pallas_call(
        _cross_body,
        grid=(b, s // ts),
        in_specs=[pl.BlockSpec((1, ts, d), lambda bi, i: (bi, i, 0)), full(g2), full(wq_s),
                  pl.BlockSpec((1, m, d), lambda bi, i: (bi, 0, 0)),
                  pl.BlockSpec((1, m, d), lambda bi, i: (bi, 0, 0)), full(wo_b)],
        out_specs=pl.BlockSpec((1, ts, d), lambda bi, i: (bi, i, 0)),
        out_shape=jax.ShapeDtypeStruct((b, s, d), F32),
        compiler_params=_cparams("parallel", "parallel"),
        name="cross_attn",
    )(x, g2, wq_s, k.reshape(b, m, d), v.reshape(b, m, d), wo_b)


def _swiglu_body(x_ref, g_ref, wg_ref, wu_ref, wd_ref, o_ref, h_sc, acc_sc):
    f = pl.program_id(1)

    @pl.when(f == 0)
    def _():
        h_sc[...] = _rms(x_ref[...], g_ref[...]).astype(BF16)
        acc_sc[...] = x_ref[...]

    h = h_sc[...]
    gate = jnp.dot(h, wg_ref[...], preferred_element_type=F32)
    up = jnp.dot(h, wu_ref[...], preferred_element_type=F32)
    act = (gate * jax.nn.sigmoid(gate) * up).astype(BF16)
    acc_sc[...] += jnp.dot(act, wd_ref[...], preferred_element_type=F32)

    @pl.when(f == pl.num_programs(1) - 1)
    def _():
        o_ref[...] = acc_sc[...]


def _swiglu(x, gain, wg, wu, wd, tm=512, tf=1408):
    t, d = x.shape
    ff = wg.shape[1]
    tm, tf = min(tm, t), min(tf, ff)
    assert t % tm == 0 and ff % tf == 0
    return pl.pallas_call(
        _swiglu_body,
        grid=(t // tm, ff // tf),
        in_specs=[pl.BlockSpec((tm, d), lambda i, f: (i, 0)),
                  pl.BlockSpec((1, d), lambda i, f: (0, 0)),
                  pl.BlockSpec((d, tf), lambda i, f: (0, f)),
                  pl.BlockSpec((d, tf), lambda i, f: (0, f)),
                  pl.BlockSpec((tf, d), lambda i, f: (f, 0))],
        out_specs=pl.BlockSpec((tm, d), lambda i, f: (i, 0)),
        out_shape=jax.ShapeDtypeStruct((t, d), F32),
        scratch_shapes=[pltpu.VMEM((tm, d), BF16), pltpu.VMEM((tm, d), F32)],
        compiler_params=_cparams("parallel", "arbitrary"),
        name="swiglu",
    )(x, gain.reshape(1, d).astype(F32), wg.astype(BF16), wu.astype(BF16), wd.astype(BF16))


def _router_body(x_ref, g_ref, r_ref, h_ref, rt_ref):
    h = _rms(x_ref[...], g_ref[...])
    h_ref[...] = h
    h_hi = h.astype(BF16)
    h_lo = (h - h_hi.astype(F32)).astype(BF16)
    r = r_ref[...]
    r_hi = r.astype(BF16)
    r_lo = (r - r_hi.astype(F32)).astype(BF16)
    logits = (jnp.dot(h_hi, r_hi, preferred_element_type=F32) + jnp.dot(h_lo, r_hi, preferred_element_type=F32)
              + jnp.dot(h_hi, r_lo, preferred_element_type=F32))
    lane = lax.broadcasted_iota(jnp.int32, logits.shape, 1)
    lg = jnp.where(lane < N_EXPERTS, logits, -jnp.inf)
    v0 = jnp.max(lg, axis=-1, keepdims=True)
    e0 = jnp.min(jnp.where(lg == v0, lane, LANES), axis=-1, keepdims=True)
    lg1 = jnp.where(lane == e0, -jnp.inf, lg)
    v1 = jnp.max(lg1, axis=-1, keepdims=True)
    e1 = jnp.min(jnp.where(lg1 == v1, lane, LANES), axis=-1, keepdims=True)
    g1 = 1.0 / (1.0 + jnp.exp(v0 - v1))
    g0 = 1.0 - g1
    out = jnp.where(lane == 0, g0, jnp.where(lane == 1, g1, 0.0))
    out = jnp.where(lane == 2, e0.astype(F32), jnp.where(lane == 3, e1.astype(F32), out))
    rt_ref[...] = out


def _router(x, gain, router, tm=512):
    t, d = x.shape
    tm = min(tm, t)
    r_pad = jnp.zeros((d, LANES), F32).at[:, :N_EXPERTS].set(router.astype(F32))
    return pl.pallas_call(
        _router_body,
        grid=(t // tm,),
        in_specs=[pl.BlockSpec((tm, d), lambda i: (i, 0)), pl.BlockSpec((1, d), lambda i: (0, 0)),
                  pl.BlockSpec((d, LANES), lambda i: (0, 0))],
        out_specs=[pl.BlockSpec((tm, d), lambda i: (i, 0)), pl.BlockSpec((tm, LANES), lambda i: (i, 0))],
        out_shape=[jax.ShapeDtypeStruct((t, d), F32), jax.ShapeDtypeStruct((t, LANES), F32)],
        compiler_params=_cparams("parallel"),
        name="moe_router",
    )(x, gain.reshape(1, d).astype(F32), r_pad)


def _row_copy(src_hbm, src_row, dst_ref, dst_row, sem):
    return pltpu.make_async_copy(src_hbm.at[pl.ds(src_row, 1)], dst_ref.at[pl.ds(dst_row, 1)], sem)


def _experts_body(be_ref, nb_ref, tok_ref, h_hbm, wg_ref, wu_ref, wd_ref, o_ref, xf_sc, xb_sc, acc_sc, sem, *,
                  n_blk, n_f):
    i, f = pl.program_id(0), pl.program_id(1)
    rows = xf_sc.shape[1]
    part = rows // n_f
    slot = i & 1
    nb = nb_ref[0]

    @pl.when((i == 0) & (f == 0))
    def _():
        for r in range(rows):
            _row_copy(h_hbm, tok_ref[r], xf_sc.at[0], r, sem.at[0]).start()

    @pl.when((f == 0) & (i <= nb))
    def _():
        for r in range(rows):
            _row_copy(h_hbm, 0, xf_sc.at[slot], r, sem.at[slot]).wait()

    @pl.when(i < nb)
    def _():
        @pl.when(f == 0)
        def _():
            xb_sc[...] = xf_sc[slot].astype(BF16)
            acc_sc[...] = jnp.zeros(acc_sc.shape, F32)

        base = jnp.minimum(i + 1, n_blk - 1) * rows + f * part
        for r in range(part):
            _row_copy(h_hbm, tok_ref[base + r], xf_sc.at[1 - slot], f * part + r, sem.at[1 - slot]).start()

        x = xb_sc[...]
        gate = jnp.dot(x, wg_ref[0], preferred_element_type=F32)
        up = jnp.dot(x, wu_ref[0], preferred_element_type=F32)
        act = (gate * jax.nn.sigmoid(gate) * up).astype(BF16)
        acc_sc[...] += jnp.dot(act, wd_ref[0], preferred_element_type=F32)

        @pl.when(f == n_f - 1)
        def _():
            o_ref[...] = acc_sc[...]

    @pl.when(i >= nb)
    def _():
        o_ref[...] = jnp.zeros(o_ref.shape, F32)


def _experts(h, row_tok, blk_e, n_used, wg, wu, wd, rows, tf=1792):
    n_rows = row_tok.shape[0]
    d = h.shape[1]
    ff = wg.shape[2]
    n_blk = n_rows // rows
    last = ff // tf - 1
    grid_spec = pltpu.PrefetchScalarGridSpec(
        num_scalar_prefetch=3,
        grid=(n_blk, ff // tf),
        in_specs=[pl.BlockSpec(memory_space=pl.ANY),
                  pl.BlockSpec((1, d, tf), lambda i, f, be, nb, tok: (be[i], 0, jnp.where(i < nb[0], f, last))),
                  pl.BlockSpec((1, d, tf), lambda i, f, be, nb, tok: (be[i], 0, jnp.where(i < nb[0], f, last))),
                  pl.BlockSpec((1, tf, d), lambda i, f, be, nb, tok: (be[i], jnp.where(i < nb[0], f, last), 0))],
        out_specs=pl.BlockSpec((rows, d), lambda i, f, be, nb, tok: (i, 0)),
        scratch_shapes=[pltpu.VMEM((2, rows, d), F32), pltpu.VMEM((rows, d), BF16), pltpu.VMEM((rows, d), F32),
                        pltpu.SemaphoreType.DMA((2,))],
    )
    assert n_blk * rows > TOP_K * h.shape[0] + N_EXPERTS * (rows - 1)
    return pl.pallas_call(
        functools.partial(_experts_body, n_blk=n_blk, n_f=ff // tf),
        grid_spec=grid_spec,
        out_shape=jax.ShapeDtypeStruct((n_rows, d), F32),
        compiler_params=_cparams("arbitrary", "arbitrary"),
        name="moe_experts",
    )(blk_e, n_used, row_tok, h, wg, wu, wd)


def _combine_body(dest_ref, x_ref, rt_ref, ys_hbm, o_ref, buf, sem):
    tm = x_ref.shape[0]
    i = pl.program_id(0)
    slot = i & 1

    def start(tile, s):
        def body(r, c):
            for k in range(TOP_K):
                _row_copy(ys_hbm, dest_ref[(tile * tm + r) * TOP_K + k], buf.at[s, k], r, sem.at[s]).start()
            return c
        lax.fori_loop(0, tm, body, 0, unroll=8)

    @pl.when(i == 0)
    def _():
        start(0, 0)

    @pl.when(i + 1 < pl.num_programs(0))
    def _():
        start(i + 1, 1 - slot)

    for r in range(tm):
        for k in range(TOP_K):
            _row_copy(ys_hbm, 0, buf.at[slot, k], r, sem.at[slot]).wait()
    rt = rt_ref[...]
    o_ref[...] = x_ref[...] + rt[:, 0:1] * buf[slot, 0] + rt[:, 1:2] * buf[slot, 1]


def _combine(x, routing, ys, dest, tm=256):
    t, d = x.shape
    tm = min(tm, t)
    grid_spec = pltpu.PrefetchScalarGridSpec(
        num_scalar_prefetch=1,
        grid=(t // tm,),
        in_specs=[pl.BlockSpec((tm, d), lambda i, dest: (i, 0)),
                  pl.BlockSpec((tm, LANES), lambda i, dest: (i, 0)),
                  pl.BlockSpec(memory_space=pl.ANY)],
        out_specs=pl.BlockSpec((tm, d), lambda i, dest: (i, 0)),
        scratch_shapes=[pltpu.VMEM((2, TOP_K, tm, d), F32), pltpu.SemaphoreType.DMA((2,))],
    )
    return pl.pallas_call(
        _combine_body,
        grid_spec=grid_spec,
        out_shape=jax.ShapeDtypeStruct((t, d), F32),
        compiler_params=_cparams("arbitrary"),
        name="moe_combine",
    )(dest, x, routing, ys)


MOE_ROWS = 512


def _moe(x, gain, router, w_gate, w_up, w_down):
    t, d = x.shape
    h, routing = _router(x, gain, router)
    n_assign = t * TOP_K
    flat_e = routing[:, 2:2 + TOP_K].astype(jnp.int32).reshape(-1)
    onehot = (flat_e[:, None] == jnp.arange(N_EXPERTS)[None, :]).astype(jnp.int32)
    before = jnp.cumsum(onehot, axis=0) - onehot
    counts = jnp.sum(onehot, axis=0)
    padded = (counts + MOE_ROWS - 1) // MOE_ROWS * MOE_ROWS
    pend = jnp.cumsum(padded)
    pstart = pend - padded
    dest = (jnp.sum((pstart[None, :] + before) * onehot, axis=1)).astype(jnp.int32)
    n_rows = -(-n_assign // MOE_ROWS) * MOE_ROWS + N_EXPERTS * MOE_ROWS
    n_blk = n_rows // MOE_ROWS
    row_tok = jnp.zeros((n_rows,), jnp.int32).at[dest].set(jnp.arange(n_assign, dtype=jnp.int32) // TOP_K)
    blk_start = jnp.arange(n_blk, dtype=jnp.int32) * MOE_ROWS
    blk_e = jnp.minimum(jnp.sum((pend[None, :] <= blk_start[:, None]).astype(jnp.int32), axis=1), N_EXPERTS - 1)
    n_used = (pend[-1] // MOE_ROWS).astype(jnp.int32).reshape(1)
    ys = _experts(h, row_tok, blk_e.astype(jnp.int32), n_used, w_gate.astype(BF16), w_up.astype(BF16),
                  w_down.astype(BF16), MOE_ROWS)
    return _combine(x, routing, ys, dest)


def _final_norm_body(x_ref, g_ref, o_ref):
    o_ref[...] = _rms(x_ref[...], g_ref[...])


def _final_norm(x, gain, tm=1024):
    t, d = x.shape
    tm = min(tm, t)
    return pl.pallas_call(
        _final_norm_body,
        grid=(t // tm,),
        in_specs=[pl.BlockSpec((tm, d), lambda i: (i, 0)), pl.BlockSpec((1, d), lambda i: (0, 0))],
        out_specs=pl.BlockSpec((tm, d), lambda i: (i, 0)),
        out_shape=jax.ShapeDtypeStruct((t, d), F32),
        compiler_params=_cparams("parallel"),
        name="final_norm",
    )(x, gain.reshape(1, d).astype(F32))


def _all_rope_tables(s):
    return {64: _rope_tables(s, 64), 32: _rope_tables(s, 32)}


def kernel(x, mem, norm_mix, w_in, w_out, diff_lambda, diff_subln, nsa_cmp_pos, nsa_cmp_w1, nsa_cmp_w2, mla_q_norm, mla_kv_norm, mla_w_uq, mla_w_ukv, norm_cross, norm_mem, cross_wq, cross_wk, cross_wv, cross_wo, norm_ffn, ffn_w_gate, ffn_w_up, ffn_w_down, moe_router, moe_w_gate, moe_w_up, moe_w_down, final_norm):
    b, s, d = x.shape
    depth = w_in.shape[0]
    tabs = _all_rope_tables(s)
    for l in range(depth):
        x = _token_mixers(x, l, tabs, norm_mix[l], w_in[l], w_out[l], diff_lambda[l], diff_subln[l],
                          nsa_cmp_pos[l], nsa_cmp_w1[l], nsa_cmp_w2[l], mla_q_norm[l], mla_kv_norm[l],
                          mla_w_uq[l], mla_w_ukv[l])
        x = _cross_attention(x, mem, norm_cross[l], norm_mem[l], cross_wq[l], cross_wk[l], cross_wv[l],
                             cross_wo[l])
        xt = x.reshape(b * s, d)
        i = l // 2
        if l % 2 == 0:
            xt = _swiglu(xt, norm_ffn[l], ffn_w_gate[i], ffn_w_up[i], ffn_w_down[i])
        else:
            xt = _moe(xt, norm_ffn[l], moe_router[i], moe_w_gate[i], moe_w_up[i], moe_w_down[i])
        x = xt.reshape(b, s, d)
    return _final_norm(x.reshape(b * s, d), final_norm).reshape(b, s, d)
```

```python
import functools
import math

import numpy as np
import jax
import jax.numpy as jnp
from jax import lax
from jax.experimental import pallas as pl
from jax.experimental.pallas import tpu as pltpu

F32 = jnp.float32
BF16 = jnp.bfloat16

HEAD_DIM = 64
GROUP_HEADS = 4
GROUP_WIDTH = HEAD_DIM * GROUP_HEADS
ROPE_THETA = 10000.0
NORM_EPS = 1e-6
DIFF_DK = HEAD_DIM // 2
CMP_LEN = 32
CMP_STRIDE = 16
SEL_BLOCK = 64
SEL_SHIFT = 6
SEL_TOPN = 16
NSA_WINDOW = 512
FORCED_SCORE = 1e6
MLA_NOPE = 64
MLA_ROPE = 32
MLA_DV = 64
DILATED_PATTERNS = ((128, 1), (512, 4), (2048, 16))
CROSS_HEADS = 4
N_EXPERTS = 8
TOP_K = 2

LANES = 128
V7X_VMEM_BYTES = 64 * 1024 * 1024
VMEM_LIMIT = V7X_VMEM_BYTES * 7 // 8

ONES_ROWS = 16
NEG_BIG = -1e30
SEL_BIAS = -1e9
LOG2E = math.log2(math.e)
LN2 = math.log(2.0)


def _cparams(*sem):
    return pltpu.CompilerParams(dimension_semantics=sem, vmem_limit_bytes=VMEM_LIMIT)


def _rms(xf, gain):
    ms = jnp.mean(xf * xf, axis=-1, keepdims=True)
    return xf * lax.rsqrt(ms + NORM_EPS) * gain


def _mm_body(*refs, nx, norm, res, splits):
    x_refs = refs[:nx]
    pos = nx
    g_ref = refs[pos] if norm else None
    pos += int(norm)
    w_ref = refs[pos]
    pos += 1
    r_ref = refs[pos] if res else None
    pos += int(res)
    o_refs = refs[pos:pos + len(splits)]
    xs_ref = refs[pos + len(splits)]

    @pl.when(pl.program_id(1) == 0)
    def _():
        off = 0
        for xr in x_refs:
            xv = xr[...]
            if norm:
                xv = _rms(xv.astype(F32), g_ref[...])
            kw = xv.shape[-1]
            xs_ref[:, off:off + kw] = xv.astype(BF16)
            off += kw

    acc = jnp.dot(xs_ref[...], w_ref[...], preferred_element_type=F32)
    if res:
        acc = acc + r_ref[...]
    off = 0
    for o_ref, width in zip(o_refs, splits):
        o_ref[...] = acc[:, off:off + width].astype(o_ref.dtype)
        off += width


def _mm(xs, w, *, gain=None, res=None, tm=512, tn=None, splits=None, out_dtype=F32, name="mm"):
    t = xs[0].shape[0]
    k, n = w.shape
    assert sum(x.shape[1] for x in xs) == k
    tn = n if tn is None else tn
    splits = (tn,) if splits is None else tuple(splits)
    assert sum(splits) == tn and (len(splits) == 1 or tn == n)
    tm = min(tm, t)
    assert t % tm == 0 and n % tn == 0
    norm = gain is not None
    assert not norm or len(xs) == 1
    in_specs = [pl.BlockSpec((tm, x.shape[1]), lambda i, j: (i, 0)) for x in xs]
    args = list(xs)
    if norm:
        in_specs.append(pl.BlockSpec((1, k), lambda i, j: (0, 0)))
        args.append(gain.reshape(1, k).astype(F32))
    in_specs.append(pl.BlockSpec((k, tn), lambda i, j: (0, j)))
    args.append(w)
    if res is not None:
        in_specs.append(pl.BlockSpec((tm, tn), lambda i, j: (i, j)))
        args.append(res)
    if len(splits) == 1:
        out_shape = [jax.ShapeDtypeStruct((t, n), out_dtype)]
        out_specs = [pl.BlockSpec((tm, tn), lambda i, j: (i, j))]
    else:
        out_shape = [jax.ShapeDtypeStruct((t, s), out_dtype) for s in splits]
        out_specs = [pl.BlockSpec((tm, s), lambda i, j: (i, 0)) for s in splits]
    outs = pl.pallas_call(
        functools.partial(_mm_body, nx=len(xs), norm=norm, res=res is not None, splits=splits),
        grid=(t // tm, n // tn),
        in_specs=in_specs,
        out_specs=out_specs,
        out_shape=out_shape,
        scratch_shapes=[pltpu.VMEM((tm, k), BF16)],
        compiler_params=_cparams("parallel", "arbitrary"),
        name=name,
    )(*args)
    return outs[0] if len(outs) == 1 else outs


def _rope_tables(s, group):
    half = group // 2
    pos = jnp.arange(s, dtype=F32)
    inv_freq = ROPE_THETA ** (-jnp.arange(half, dtype=F32) / half)
    ang = pos[:, None] * inv_freq[None, :]
    cos, sin = jnp.cos(ang), jnp.sin(ang)
    zero = jnp.zeros_like(sin)
    reps = LANES // group
    c = jnp.tile(jnp.concatenate([cos, cos], -1), (1, reps))
    s1 = jnp.tile(jnp.concatenate([zero, sin], -1), (1, reps))
    s2 = jnp.tile(jnp.concatenate([-sin, zero], -1), (1, reps))
    return c, s1, s2


def _rope(x, tabs, half):
    w = x.shape[-1]
    c, s1, s2 = (jnp.tile(t, (1, w // LANES)) for t in tabs)
    return x * c + pltpu.roll(x, half, 1) * s1 + pltpu.roll(x, w - half, 1) * s2


def _prep_a(z, t32, q_ref, k_ref, vt_ref):
    qk = _rope(z[:, :512], t32, DIFF_DK // 2)
    lane = lax.broadcasted_iota(jnp.int32, (z.shape[0], HEAD_DIM), 1)
    scale = DIFF_DK ** -0.5 * LOG2E
    for h in range(GROUP_HEADS):
        q = qk[:, h * 64:(h + 1) * 64] * scale
        q_ref[0, h, 0] = jnp.where(lane < DIFF_DK, q, 0.0).astype(BF16)
        q_ref[0, h, 1] = jnp.where(lane >= DIFF_DK, q, 0.0).astype(BF16)
        k_ref[0, h] = qk[:, 256 + h * 64:256 + (h + 1) * 64].astype(BF16)
    vt_ref[0] = z[:, 512:768].T.reshape(vt_ref.shape[1:]).astype(BF16)


def _prep_b(z, t64, qn_ref, qr_ref, ka_ref, kw_ref, kc_ref, vc_ref, vs_ref, vw_ref, g_ref):
    ts = z.shape[0]
    scale = HEAD_DIM ** -0.5 * LOG2E
    rp = _rope(z[:, :384], t64, HEAD_DIM // 2)
    for h in range(GROUP_HEADS):
        qn_ref[0, h] = (z[:, h * 64:(h + 1) * 64] * scale).astype(BF16)
        qr_ref[0, h] = (rp[:, h * 64:(h + 1) * 64] * scale).astype(BF16)
    kpos = pl.program_id(0) * ts + lax.broadcasted_iota(jnp.int32, (ts, SEL_BLOCK), 0)
    lane = lax.broadcasted_iota(jnp.int32, (ts, SEL_BLOCK), 1)
    onehot = jnp.where((kpos >> SEL_SHIFT) == lane, 1.0, 0.0)
    ka_ref[0] = jnp.concatenate([rp[:, 256:320], onehot], axis=-1).astype(BF16)
    kw_ref[0] = rp[:, 320:384].astype(BF16)
    kc_ref[0] = z[:, 384:448].astype(BF16)
    vc_ref[0] = z[:, 448:512].astype(BF16)
    vt = z[:, 512:640].T.astype(BF16)
    vs_ref[0] = vt[:HEAD_DIM]
    vw_ref[0] = vt[HEAD_DIM:]
    g_ref[0] = jax.nn.sigmoid(z[:, 640:768])


def _prep_c(z, t32, qg_ref, kg_ref, wq_ref, wkv_ref, q_ref, k_ref, vt_ref):
    scale = (MLA_NOPE + MLA_ROPE) ** -0.5 * LOG2E
    cq = _rms(z[:, :384], qg_ref[...]).astype(BF16)
    ckv = _rms(z[:, 384:512], kg_ref[...]).astype(BF16)
    qc = jnp.dot(cq, wq_ref[...], preferred_element_type=F32)
    kvc = jnp.dot(ckv, wkv_ref[...], preferred_element_type=F32)
    half = MLA_ROPE // 2
    q_rope = _rope(qc[:, 256:384], t32, half)
    k_rope = _rope(z[:, 512:640], t32, half)[:, :MLA_ROPE]
    for h in range(GROUP_HEADS):
        q = jnp.concatenate([qc[:, h * 64:(h + 1) * 64], q_rope[:, h * 32:(h + 1) * 32]], axis=-1)
        q_ref[0, h] = (q * scale).astype(BF16)
        k_ref[0, h] = jnp.concatenate([kvc[:, h * 64:(h + 1) * 64], k_rope], axis=-1).astype(BF16)
    vt_ref[0] = kvc[:, 256:512].T.reshape(vt_ref.shape[1:]).astype(BF16)


def _prep_d(z, t64, q_ref, k_ref, vt_ref):
    scale = HEAD_DIM ** -0.5 * LOG2E
    qk = _rope(z[:, :512], t64, HEAD_DIM // 2)
    for h in range(GROUP_HEADS):
        q_ref[0, h] = (qk[:, h * 64:(h + 1) * 64] * scale).astype(BF16)
        k_ref[0, h] = qk[:, 256 + h * 64:256 + (h + 1) * 64].astype(BF16)
    vt_ref[0] = z[:, 512:768].T.reshape(vt_ref.shape[1:]).astype(BF16)


Z_SPLITS = (768, 768, 640, 768)
N_PREP_OUT = (3, 9, 3, 3)


def _project_prep_body(*refs):
    x_ref, g_ref, w_ref = refs[:3]
    t64 = tuple(r[...] for r in refs[3:6])
    t32 = tuple(r[...] for r in refs[6:9])
    qg_ref, kg_ref, wq_ref, wkv_ref = refs[9:13]
    outs = refs[13:]
    h = _rms(x_ref[0], g_ref[...]).astype(BF16)
    z = jnp.dot(h, w_ref[...], preferred_element_type=F32)
    o0, o1, o2, o3 = np.cumsum((0,) + Z_SPLITS[:3])
    n0, n1, n2 = np.cumsum(N_PREP_OUT[:3])
    _prep_a(z[:, o0:o0 + Z_SPLITS[0]], t32, *outs[:n0])
    _prep_b(z[:, o1:o1 + Z_SPLITS[1]], t64, *outs[n0:n1])
    _prep_c(z[:, o2:o2 + Z_SPLITS[2]], t32, qg_ref, kg_ref, wq_ref, wkv_ref, *outs[n1:n2])
    _prep_d(z[:, o3:o3 + Z_SPLITS[3]], t64, *outs[n2:])


def _project_prep(x, gain, w, tabs64, tabs32, q_gain, kv_gain, wq, wkv, ts):
    b, s, d = x.shape
    dk = MLA_NOPE + MLA_ROPE
    full = lambda a: pl.BlockSpec(a.shape, lambda i, bi: (0,) * a.ndim)
    tspec = pl.BlockSpec((ts, LANES), lambda i, bi: (i, 0))
    h4 = lambda dd: pl.BlockSpec((1, 4, ts, dd), lambda i, bi: (bi, 0, i, 0))
    h4t = pl.BlockSpec((1, 4, 64, ts), lambda i, bi: (bi, 0, 0, i))
    sq = lambda dd: pl.BlockSpec((1, ts, dd), lambda i, bi: (bi, i, 0))
    t64 = pl.BlockSpec((1, 64, ts), lambda i, bi: (bi, 0, i))
    sd = lambda *shape, dt=BF16: jax.ShapeDtypeStruct(shape, dt)
    out_specs = [pl.BlockSpec((1, 4, 2, ts, 64), lambda i, bi: (bi, 0, 0, i, 0)), h4(64), h4t,
                 h4(64), h4(64), sq(128), sq(64), sq(64), sq(64), t64, t64, sq(128),
                 h4(dk), h4(dk), h4t,
                 h4(64), h4(64), h4t]
    out_shape = [sd(b, 4, 2, s, 64), sd(b, 4, s, 64), sd(b, 4, 64, s),
                 sd(b, 4, s, 64), sd(b, 4, s, 64), sd(b, s, 128), sd(b, s, 64), sd(b, s, 64), sd(b, s, 64),
                 sd(b, 64, s), sd(b, 64, s), sd(b, s, 128, dt=F32),
                 sd(b, 4, s, dk), sd(b, 4, s, dk), sd(b, 4, 64, s),
                 sd(b, 4, s, 64), sd(b, 4, s, 64), sd(b, 4, 64, s)]
    g2 = gain.reshape(1, d).astype(F32)
    outs = pl.pallas_call(
        _project_prep_body,
        grid=(s // ts, b),
        in_specs=[pl.BlockSpec((1, ts, d), lambda i, bi: (bi, i, 0)), full(g2), full(w)] + [tspec] * 6
                 + [full(q_gain), full(kv_gain), full(wq), full(wkv)],
        out_specs=out_specs,
        out_shape=out_shape,
        compiler_params=_cparams("parallel", "parallel"),
        name="w_in_prep",
    )(x, g2, w, *tabs64, *tabs32, q_gain, kv_gain, wq, wkv)
    n0, n1, n2 = np.cumsum(N_PREP_OUT[:3])
    return outs[:n0], outs[n0:n1], outs[n1:n2], outs[n2:]


def _flash_schedule(nq, tq, tk, sk, window):
    qi, kj, fl = [], [], []
    for i in range(nq):
        q_lo, q_hi = i * tq, i * tq + tq - 1
        j_hi = min(q_hi, sk - 1) // tk
        j_lo = 0 if window is None else max(0, q_lo - window) // tk
        for j in range(j_lo, j_hi + 1):
            k_min, k_max = j * tk, j * tk + tk - 1
            full = k_max <= q_lo and (window is None or q_hi - k_min <= window)
            qi.append(i)
            kj.append(j)
            fl.append((1 if j == j_lo else 0) | (2 if j == j_hi else 0) | (0 if full else 4))
    return (np.asarray(qi, np.int32), np.asarray(kj, np.int32), np.asarray(fl, np.int32))


def _flash_body(qi_ref, kj_ref, fl_ref, q_ref, k_ref, vt_ref, o_ref, m_sc, acc_sc, *,
                gb, hq, tq, tk, window, with_lse, merge_groups):
    t = pl.program_id(1)
    fl = fl_ref[t]
    rows = hq * tq
    dv = vt_ref.shape[1]

    @pl.when((fl & 1) != 0)
    def _():
        m_sc[...] = jnp.full(m_sc.shape, NEG_BIG, F32)
        acc_sc[...] = jnp.zeros(acc_sc.shape, F32)

    def step(masked):
        q = q_ref[...].reshape(gb, rows, q_ref.shape[-1])
        st = jnp.einsum("gkd,gqd->gkq", k_ref[...], q, preferred_element_type=F32)
        if masked:
            krow = lax.broadcasted_iota(jnp.int32, (tk, rows), 0)
            qcol = lax.broadcasted_iota(jnp.int32, (tk, rows), 1)
            if hq > 1:
                qcol = qcol & (tq - 1)
            dist = (qi_ref[t] * tq - kj_ref[t] * tk) + qcol - krow
            ok = dist >= 0
            if window is not None:
                ok = ok & (dist <= window)
            st = jnp.where(ok[None], st, NEG_BIG)
        m_prev = m_sc[...]
        m_new = jnp.maximum(m_prev, jnp.max(st, axis=1, keepdims=True))
        alpha = jnp.exp2(m_prev - m_new)
        p = jnp.exp2((st - m_new).astype(BF16))
        v_ext = jnp.concatenate([vt_ref[...], jnp.ones((gb, ONES_ROWS, tk), BF16)], axis=1)
        acc_sc[...] = alpha * acc_sc[...] + jnp.einsum("gdk,gkq->gdq", v_ext, p, preferred_element_type=F32)
        m_sc[...] = m_new

    @pl.when((fl & 4) != 0)
    def _():
        step(True)

    @pl.when((fl & 4) == 0)
    def _():
        step(False)

    @pl.when((fl & 2) != 0)
    def _():
        l = acc_sc[:, dv:dv + 1, :]
        o_t = acc_sc[:, :dv, :] / l
        if merge_groups:
            o_ref[0] = o_t.reshape(gb * dv, rows).T.astype(o_ref.dtype)
            return
        if with_lse:
            extra = jnp.broadcast_to(m_sc[...] * LN2 + jnp.log(l), o_t.shape)
        else:
            extra = jnp.zeros((gb, LANES - dv, rows), F32)
        dvo = o_ref.shape[-1]
        for g in range(gb):
            o = jnp.concatenate([o_t[g], extra[g]], axis=0).T
            o_ref[g] = o[:, :dvo].reshape(o_ref.shape[1:]).astype(o_ref.dtype)


def _flash(q, k, vt, *, tq, tk, gb=1, window=None, with_lse=False, merge_groups=False, out_dtype=F32,
           name="flash"):
    g, hq, sq, dk = q.shape
    _, dv, sk = vt.shape
    tq, tk = min(tq, sq), min(tk, sk)
    assert g % gb == 0 and sq % tq == 0 and sk % tk == 0
    assert tq & (tq - 1) == 0 and 2 * dv == LANES
    qi, kj, fl = _flash_schedule(sq // tq, tq, tk, sk, window)
    dvo = 2 * dv if with_lse else dv
    rows = hq * tq
    if merge_groups:
        assert hq == 1 and not with_lse and (gb * dv) % LANES == 0
        out_spec = pl.BlockSpec((1, tq, gb * dv), lambda gi, t, qi, kj, fl: (gi, qi[t], 0))
        out_shape = jax.ShapeDtypeStruct((g // gb, sq, gb * dv), out_dtype)
    else:
        out_spec = pl.BlockSpec((gb, hq, tq, dvo), lambda gi, t, qi, kj, fl: (gi, 0, qi[t], 0))
        out_shape = jax.ShapeDtypeStruct((g, hq, sq, dvo), out_dtype)
    grid_spec = pltpu.PrefetchScalarGridSpec(
        num_scalar_prefetch=3,
        grid=(g // gb, len(qi)),
        in_specs=[pl.BlockSpec((gb, hq, tq, dk), lambda gi, t, qi, kj, fl: (gi, 0, qi[t], 0)),
                  pl.BlockSpec((gb, tk, dk), lambda gi, t, qi, kj, fl: (gi, kj[t], 0)),
                  pl.BlockSpec((gb, dv, tk), lambda gi, t, qi, kj, fl: (gi, 0, kj[t]))],
        out_specs=out_spec,
        scratch_shapes=[pltpu.VMEM((gb, 1, rows), F32), pltpu.VMEM((gb, dv + ONES_ROWS, rows), F32)],
    )
    return pl.pallas_call(
        functools.partial(_flash_body, gb=gb, hq=hq, tq=tq, tk=tk, window=window, with_lse=with_lse,
                          merge_groups=merge_groups),
        grid_spec=grid_spec,
        out_shape=out_shape,
        compiler_params=_cparams("parallel", "arbitrary"),
        name=name,
    )(jnp.asarray(qi), jnp.asarray(kj), jnp.asarray(fl), q, k, vt)


def _gelu_tanh(x):
    return 0.5 * x * (1.0 + jnp.tanh(math.sqrt(2.0 / math.pi) * (x + 0.044715 * (x * x * x))))


def _compress_body(r_ref, pos_ref, w1_ref, w2_ref, o_ref):
    half = CMP_STRIDE * HEAD_DIM
    r = r_ref[0, 0]
    w1 = w1_ref[0]
    top = jnp.dot(r, w1[:half], preferred_element_type=F32)
    bot = jnp.dot(r, w1[half:], preferred_element_type=F32)
    nc = r.shape[0]
    posb = jnp.dot(pos_ref[0], w1, preferred_element_type=F32)[:1]
    pre = top + pltpu.roll(bot, nc - 1, 0) + posb
    hid = _gelu_tanh(pre).astype(BF16)
    o_ref[0, 0] = jnp.dot(hid, w2_ref[0], preferred_element_type=F32).astype(o_ref.dtype)


def _nsa_compress(kv_rows, pos_flat, w1, w2):
    _, b, nc, _ = kv_rows.shape
    return pl.pallas_call(
        _compress_body,
        grid=(2, b),
        in_specs=[pl.BlockSpec((1, 1, nc, 1024), lambda w, bi: (w, bi, 0, 0)),
                  pl.BlockSpec((1, 8, 2048), lambda w, bi: (w, 0, 0)),
                  pl.BlockSpec((1, 2048, 256), lambda w, bi: (w, 0, 0)),
                  pl.BlockSpec((1, 256, 64), lambda w, bi: (w, 0, 0))],
        out_specs=pl.BlockSpec((1, 1, nc, 64), lambda w, bi: (w, bi, 0, 0)),
        out_shape=jax.ShapeDtypeStruct((2, b, nc, 64), BF16),
        compiler_params=_cparams("parallel", "parallel"),
        name="nsa_compress",
    )(kv_rows, pos_flat, w1, w2)


def _cmp_select_body(qn_ref, qr_ref, kc_ref, vc_ref, cov_ref, o_ref, qa_ref, *, tq, n_sel, k_top):
    ncp = kc_ref.shape[2]
    pos = pl.program_id(1) * tq + lax.broadcasted_iota(jnp.int32, (tq, ncp), 0)
    blk_end = lax.broadcasted_iota(jnp.int32, (tq, ncp), 1) * CMP_STRIDE + (CMP_LEN - 1)
    mask = blk_end <= pos
    kc, vc = kc_ref[0, 0], vc_ref[0, 0]
    p_sum = jnp.zeros((tq, ncp), F32)
    for h in range(GROUP_HEADS):
        s = lax.dot_general(qn_ref[0, h], kc, (((1,), (1,)), ((), ())), preferred_element_type=F32)
        s = jnp.where(mask, s, NEG_BIG)
        mx = jnp.max(s, axis=-1, keepdims=True)
        mx = jnp.where(mx > 0.5 * NEG_BIG, mx, 0.0)
        e = jnp.where(mask, jnp.exp2(s - mx), 0.0)
        p = e / jnp.maximum(jnp.sum(e, axis=-1, keepdims=True), 1e-30)
        o_ref[0, h] = jnp.dot(p.astype(BF16), vc, preferred_element_type=F32).astype(o_ref.dtype)
        p_sum = p_sum + p
    p_hi = p_sum.astype(BF16)
    p_lo = (p_sum - p_hi.astype(F32)).astype(BF16)
    cov = cov_ref[...]
    imp = (jnp.dot(p_hi, cov, preferred_element_type=F32) + jnp.dot(p_lo, cov, preferred_element_type=F32))
    nblk = SEL_BLOCK
    imp_t = imp.T[:nblk]
    tpos = pl.program_id(1) * tq + lax.broadcasted_iota(jnp.int32, (nblk, tq), 1)
    cur = tpos >> SEL_SHIFT
    j = lax.broadcasted_iota(jnp.int32, (nblk, tq), 0)
    valid = j <= cur
    forced = (j == 0) | (j == cur) | (j == cur - 1)
    score = jnp.where(valid, jnp.where(forced, FORCED_SCORE, imp_t), -jnp.inf)
    sub = 8
    groups = [score[g * sub:(g + 1) * sub] for g in range(nblk // sub)]
    ranks = [jnp.zeros((sub, tq), jnp.int32) for _ in groups]
    jj = lax.broadcasted_iota(jnp.int32, (sub, tq), 0)
    for i in range(n_sel):
        row = jnp.broadcast_to(score[i:i + 1], (sub, tq))
        for g, sg in enumerate(groups):
            if i < g * sub:
                ahead = row >= sg
            elif i >= (g + 1) * sub:
                ahead = row > sg
            else:
                ahead = (row > sg) | ((row == sg) & (i - g * sub < jj))
            ranks[g] = ranks[g] + ahead.astype(jnp.int32)
    rank = jnp.concatenate(ranks, axis=0)
    bias_t = jnp.where(valid & (rank < k_top), 0.0, SEL_BIAS)
    bias = jnp.concatenate([bias_t, jnp.zeros((LANES - nblk, tq), F32)], axis=0).T[:, :nblk].astype(BF16)
    for h in range(GROUP_HEADS):
        qa_ref[0, h] = jnp.concatenate([qr_ref[0, h], bias], axis=-1)


def _cover_matrix(ncp, n_sel, lanes):
    c_start = np.arange(ncp)[:, None] * CMP_STRIDE
    s_start = np.arange(lanes)[None, :] * SEL_BLOCK
    cov = np.clip(np.minimum(c_start + CMP_LEN, s_start + SEL_BLOCK) - np.maximum(c_start, s_start), 0, None)
    cov = cov.astype(np.float32) / CMP_LEN
    cov[:, n_sel:] = 0.0
    cov[ncp - 1:, :] = 0.0
    return cov


def _cmp_select(qn, qr, kvc, tq):
    b, _, s, _ = qn.shape
    ncp = kvc.shape[2]
    n_sel = s // SEL_BLOCK
    assert n_sel <= SEL_BLOCK
    k_top = min(SEL_TOPN, n_sel)
    cov = jnp.asarray(_cover_matrix(ncp, n_sel, LANES), BF16)
    h4 = lambda d: pl.BlockSpec((1, 4, tq, d), lambda bi, i: (bi, 0, i, 0))
    return pl.pallas_call(
        functools.partial(_cmp_select_body, tq=tq, n_sel=n_sel, k_top=k_top),
        grid=(b, s // tq),
        in_specs=[h4(64), h4(64),
                  pl.BlockSpec((1, 1, ncp, 64), lambda bi, i: (0, bi, 0, 0)),
                  pl.BlockSpec((1, 1, ncp, 64), lambda bi, i: (1, bi, 0, 0)),
                  pl.BlockSpec(cov.shape, lambda bi, i: (0, 0))],
        out_specs=[h4(64), h4(128)],
        out_shape=[jax.ShapeDtypeStruct((b, 4, s, 64), BF16), jax.ShapeDtypeStruct((b, 4, s, 128), BF16)],
        compiler_params=_cparams("parallel", "parallel"),
        name="nsa_cmp_select",
    )(qn, qr, kvc, kvc, cov)


def _merge_a_body(o_ref, lam_ref, g_ref, y_ref, *, out_scale):
    outs = []
    for h in range(GROUP_HEADS):
        o = o_ref[0, h, 0].astype(F32) - lam_ref[...] * o_ref[0, h, 1].astype(F32)
        outs.append(_rms(o, g_ref[...]) * out_scale)
    y_ref[0] = jnp.concatenate(outs, axis=-1).astype(y_ref.dtype)


def _merge_a(o, lam, subln, out_scale, ts):
    b, _, _, s, _ = o.shape
    return pl.pallas_call(
        functools.partial(_merge_a_body, out_scale=out_scale),
        grid=(b, s // ts),
        in_specs=[pl.BlockSpec((1, 4, 2, ts, 64), lambda bi, i: (bi, 0, 0, i, 0)),
                  pl.BlockSpec((1, 64), lambda bi, i: (0, 0)),
                  pl.BlockSpec((1, 64), lambda bi, i: (0, 0))],
        out_specs=pl.BlockSpec((1, ts, 256), lambda bi, i: (bi, i, 0)),
        out_shape=jax.ShapeDtypeStruct((b, s, 256), BF16),
        compiler_params=_cparams("parallel", "parallel"),
        name="merge_a",
    )(o, lam, subln)


def _merge_b_body(oc_ref, os_ref, ow_ref, g_ref, y_ref):
    g = g_ref[0]
    outs = []
    for h in range(GROUP_HEADS):
        outs.append(g[:, 3 * h:3 * h + 1] * oc_ref[0, h].astype(F32)
                    + g[:, 3 * h + 1:3 * h + 2] * os_ref[0, h].astype(F32)
                    + g[:, 3 * h + 2:3 * h + 3] * ow_ref[0, h].astype(F32))
    y_ref[0] = jnp.concatenate(outs, axis=-1).astype(y_ref.dtype)


def _merge_b(o_cmp, o_sel, o_win, gates, ts):
    b, _, s, _ = o_cmp.shape
    h4 = pl.BlockSpec((1, 4, ts, 64), lambda bi, i: (bi, 0, i, 0))
    return pl.pallas_call(
        _merge_b_body,
        grid=(b, s // ts),
        in_specs=[h4, h4, h4, pl.BlockSpec((1, ts, 128), lambda bi, i: (bi, i, 0))],
        out_specs=pl.BlockSpec((1, ts, 256), lambda bi, i: (bi, i, 0)),
        out_shape=jax.ShapeDtypeStruct((b, s, 256), BF16),
        compiler_params=_cparams("parallel", "parallel"),
        name="merge_b",
    )(o_cmp, o_sel, o_win, gates)


def _merge_d_body(o1_ref, o2_ref, o3_ref, y_ref):
    outs = []
    for h in range(GROUP_HEADS):
        e = [r[0, h] for r in (o1_ref, o2_ref, o3_ref)]
        lse = [x[:, 64:] for x in e]
        mx = jnp.maximum(jnp.maximum(lse[0], lse[1]), lse[2])
        w = [jnp.exp(x - mx) for x in lse]
        den = w[0] + w[1] + w[2]
        outs.append((w[0] * e[0][:, :64] + w[1] * e[1][:, :64] + w[2] * e[2][:, :64]) / den)
    y_ref[0] = jnp.concatenate(outs, axis=-1).astype(y_ref.dtype)


def _merge_d(o1, o2, o3, ts):
    b, _, s, _ = o1.shape
    h4 = pl.BlockSpec((1, 4, ts, 128), lambda bi, i: (bi, 0, i, 0))
    return pl.pallas_call(
        _merge_d_body,
        grid=(b, s // ts),
        in_specs=[h4, h4, h4],
        out_specs=pl.BlockSpec((1, ts, 256), lambda bi, i: (bi, i, 0)),
        out_shape=jax.ShapeDtypeStruct((b, s, 256), BF16),
        compiler_params=_cparams("parallel", "parallel"),
        name="merge_d",
    )(o1, o2, o3)


def _permute_w_in(w_in):
    a_cols = 768
    b0 = a_cols
    nq = 256
    seg = lambda i: w_in[:, b0 + nq + i * 64: b0 + nq + (i + 1) * 64]
    kc, vc, ks, vs, kw, vw = (seg(i) for i in range(6))
    gb = w_in[:, b0 + nq + 384: b0 + nq + 384 + 12]
    d = w_in.shape[0]
    w_b = jnp.concatenate([w_in[:, b0:b0 + nq], ks, kw, kc, vc, vs, vw, gb, jnp.zeros((d, 116), w_in.dtype)], axis=1)
    c0 = b0 + 652
    w_c = jnp.concatenate([w_in[:, c0:c0 + 544], jnp.zeros((d, 96), w_in.dtype)], axis=1)
    d0 = c0 + 544
    return jnp.concatenate([w_in[:, :a_cols], w_b, w_c, w_in[:, d0:d0 + 768]], axis=1).astype(BF16)


def _residue_classes(t, dil):
    g, s, d = t.shape
    return t.reshape(g, s // dil, dil, d).transpose(0, 2, 1, 3).reshape(g * dil, s // dil, d)


def _residue_classes_t(t, dil):
    g, d, s = t.shape
    return t.reshape(g, d, s // dil, dil).transpose(0, 3, 1, 2).reshape(g * dil, d, s // dil)


def _from_residue_classes(t, dil):
    gd, l, d = t.shape
    return t.reshape(gd // dil, dil, l, d).transpose(0, 2, 1, 3).reshape(gd // dil, l * dil, d)


def _token_mixers(x, layer, rope_tabs, norm_g, w_in, w_out, diff_lambda, diff_subln, cmp_pos, cmp_w1, cmp_w2,
                  q_norm, kv_norm, w_uq, w_ukv):
    b, s, d = x.shape
    t = b * s
    ts = min(512, s)
    wq = w_uq.reshape(-1, 4, MLA_NOPE + MLA_ROPE)
    wq = jnp.concatenate([wq[:, :, :MLA_NOPE].reshape(-1, 256), wq[:, :, MLA_NOPE:].reshape(-1, 128)], axis=1)
    wkv = w_ukv.reshape(-1, 4, MLA_NOPE + MLA_DV)
    wkv = jnp.concatenate([wkv[:, :, :MLA_NOPE].reshape(-1, 256), wkv[:, :, MLA_NOPE:].reshape(-1, 256)], axis=1)
    prep_a, prep_b, prep_c, prep_d = _project_prep(
        x, norm_g, _permute_w_in(w_in), rope_tabs[64], rope_tabs[32], q_norm.reshape(1, -1).astype(F32),
        kv_norm.reshape(1, -1).astype(F32), wq.astype(BF16), wkv.astype(BF16), ts)

    qa, ka, va = prep_a
    oa = _flash(qa.reshape(b * 4, 2, s, 64), ka.reshape(b * 4, s, 64), va.reshape(b * 4, 64, s),
                tq=512, tk=512, gb=2, out_dtype=BF16, name="flash_diff").reshape(b, 4, 2, s, 64)
    lam_init = 0.8 - 0.6 * math.exp(-0.3 * layer)
    lp = diff_lambda.astype(F32)
    lam = jnp.exp(jnp.sum(lp[0] * lp[1])) - jnp.exp(jnp.sum(lp[2] * lp[3])) + lam_init
    y_a = _merge_a(oa, jnp.full((1, 64), lam, F32), diff_subln.reshape(1, 64).astype(F32), 1.0 - lam_init, ts)

    qn, qr, k_aug, kw, kc, vc, vs, vw, gates = prep_b
    kv_rows = jnp.stack([kc, vc]).reshape(2, b, s // CMP_STRIDE, CMP_STRIDE * HEAD_DIM)
    pos_flat = jnp.broadcast_to(cmp_pos.reshape(2, 1, CMP_LEN * HEAD_DIM), (2, 8, CMP_LEN * HEAD_DIM)).astype(BF16)
    kvc = _nsa_compress(kv_rows, pos_flat, cmp_w1.astype(BF16), cmp_w2.astype(BF16))
    o_cmp, q_aug = _cmp_select(qn, qr, kvc, min(256, s))
    o_sel = _flash(q_aug, k_aug, vs, tq=256, tk=512, out_dtype=BF16, name="flash_sel")
    o_win = _flash(qr, kw, vw, tq=256, tk=256, window=NSA_WINDOW - 1, out_dtype=BF16, name="flash_win")
    y_b = _merge_b(o_cmp, o_sel, o_win, gates, ts)

    qc, kcat, vcv = prep_c
    dk = MLA_NOPE + MLA_ROPE
    y_c = _flash(qc.reshape(b * 4, 1, s, dk), kcat.reshape(b * 4, s, dk), vcv.reshape(b * 4, 64, s),
                 tq=512, tk=512, gb=4, merge_groups=True, out_dtype=BF16, name="flash_mla")

    qd, kd, vd = prep_d
    qd, kd, vd = qd.reshape(b * 4, s, 64), kd.reshape(b * 4, s, 64), vd.reshape(b * 4, 64, s)
    o_d = []
    for window, dil in DILATED_PATTERNS:
        l = s // dil
        qq, kk, vv = _residue_classes(qd, dil), _residue_classes(kd, dil), _residue_classes_t(vd, dil)
        g = qq.shape[0]
        gb = math.gcd(g, 16)
        o = _flash(qq.reshape(g, 1, l, 64), kk, vv, tq=128, tk=128, gb=gb, window=window // dil,
                   with_lse=True, name=f"flash_dil{dil}")
        o_d.append(_from_residue_classes(o.reshape(g, l, 128), dil).reshape(b, 4, s, 128))
    y_d = _merge_d(*o_d, ts)

    ys = [y.reshape(t, 256) for y in (y_a, y_b, y_c, y_d)]
    return _mm(ys, w_out.astype(BF16), res=x.reshape(t, d), tm=min(512, t), name="w_out").reshape(b, s, d)


def _cross_body(x_ref, g_ref, wq_ref, k_ref, v_ref, wo_ref, o_ref):
    x = x_ref[0]
    h = _rms(x, g_ref[...]).astype(BF16)
    q = jnp.dot(h, wq_ref[...], preferred_element_type=F32).astype(BF16)
    dh = q.shape[-1] // CROSS_HEADS
    outs = []
    for hd in range(CROSS_HEADS):
        sl = slice(hd * dh, (hd + 1) * dh)
        s = lax.dot_general(q[:, sl], k_ref[0, :, sl], (((1,), (1,)), ((), ())), preferred_element_type=F32)
        e = jnp.exp(s - jnp.max(s, axis=-1, keepdims=True))
        p = e / jnp.sum(e, axis=-1, keepdims=True)
        outs.append(jnp.dot(p.astype(BF16), v_ref[0, :, sl], preferred_element_type=F32))
    o = jnp.concatenate(outs, axis=-1).astype(BF16)
    o_ref[0] = x + jnp.dot(o, wo_ref[...], preferred_element_type=F32)


def _cross_attention(x, mem, g_x, g_mem, wq, wk, wv, wo):
    b, s, d = x.shape
    m = mem.shape[1]
    ts = min(512, s)
    dh = d // CROSS_HEADS
    k, v = _mm([mem.reshape(b * m, d)], jnp.concatenate([wk, wv], axis=1).astype(BF16), gain=g_mem,
               tm=min(512, b * m), splits=(d, d), out_dtype=BF16, name="cross_kv")
    full = lambda a: pl.BlockSpec(a.shape, lambda bi, i: (0,) * a.ndim)
    wq_s = (wq * dh ** -0.5).astype(BF16)
    wo_b = wo.astype(BF16)
    g2 = g_x.reshape(1, d).astype(F32)
    return pl.pallas_call(
        _cross_body,
        grid=(b, s // ts),
        in_specs=[pl.BlockSpec((1, ts, d), lambda bi, i: (bi, i, 0)), full(g2), full(wq_s),
                  pl.BlockSpec((1, m, d), lambda bi, i: (bi, 0, 0)),
                  pl.BlockSpec((1, m, d), lambda bi, i: (bi, 0, 0)), full(wo_b)],
        out_specs=pl.BlockSpec((1, ts, d), lambda bi, i: (bi, i, 0)),
        out_shape=jax.ShapeDtypeStruct((b, s, d), F32),
        compiler_params=_cparams("parallel", "parallel"),
        name="cross_attn",
    )(x, g2, wq_s, k.reshape(b, m, d), v.reshape(b, m, d), wo_b)


def _swiglu_body(x_ref, g_ref, wg_ref, wu_ref, wd_ref, o_ref, h_sc, acc_sc):
    f = pl.program_id(1)

    @pl.when(f == 0)
    def _():
        h_sc[...] = _rms(x_ref[...], g_ref[...]).astype(BF16)
        acc_sc[...] = x_ref[...]

    h = h_sc[...]
    gate = jnp.dot(h, wg_ref[...], preferred_element_type=F32)
    up = jnp.dot(h, wu_ref[...], preferred_element_type=F32)
    act = (gate * jax.nn.sigmoid(gate) * up).astype(BF16)
    acc_sc[...] += jnp.dot(act, wd_ref[...], preferred_element_type=F32)

    @pl.when(f == pl.num_programs(1) - 1)
    def _():
        o_ref[...] = acc_sc[...]


def _swiglu(x, gain, wg, wu, wd, tm=512, tf=1408):
    t, d = x.shape
    ff = wg.shape[1]
    tm, tf = min(tm, t), min(tf, ff)
    assert t % tm == 0 and ff % tf == 0
    return pl.pallas_call(
        _swiglu_body,
        grid=(t // tm, ff // tf),
        in_specs=[pl.BlockSpec((tm, d), lambda i, f: (i, 0)),
                  pl.BlockSpec((1, d), lambda i, f: (0, 0)),
                  pl.BlockSpec((d, tf), lambda i, f: (0, f)),
                  pl.BlockSpec((d, tf), lambda i, f: (0, f)),
                  pl.BlockSpec((tf, d), lambda i, f: (f, 0))],
        out_specs=pl.BlockSpec((tm, d), lambda i, f: (i, 0)),
        out_shape=jax.ShapeDtypeStruct((t, d), F32),
        scratch_shapes=[pltpu.VMEM((tm, d), BF16), pltpu.VMEM((tm, d), F32)],
        compiler_params=_cparams("parallel", "arbitrary"),
        name="swiglu",
    )(x, gain.reshape(1, d).astype(F32), wg.astype(BF16), wu.astype(BF16), wd.astype(BF16))


def _router_body(x_ref, g_ref, r_ref, h_ref, rt_ref):
    h = _rms(x_ref[...], g_ref[...])
    h_ref[...] = h
    h_hi = h.astype(BF16)
    h_lo = (h - h_hi.astype(F32)).astype(BF16)
    r = r_ref[...]
    r_hi = r.astype(BF16)
    r_lo = (r - r_hi.astype(F32)).astype(BF16)
    logits = (jnp.dot(h_hi, r_hi, preferred_element_type=F32) + jnp.dot(h_lo, r_hi, preferred_element_type=F32)
              + jnp.dot(h_hi, r_lo, preferred_element_type=F32))
    lane = lax.broadcasted_iota(jnp.int32, logits.shape, 1)
    lg = jnp.where(lane < N_EXPERTS, logits, -jnp.inf)
    v0 = jnp.max(lg, axis=-1, keepdims=True)
    e0 = jnp.min(jnp.where(lg == v0, lane, LANES), axis=-1, keepdims=True)
    lg1 = jnp.where(lane == e0, -jnp.inf, lg)
    v1 = jnp.max(lg1, axis=-1, keepdims=True)
    e1 = jnp.min(jnp.where(lg1 == v1, lane, LANES), axis=-1, keepdims=True)
    g1 = 1.0 / (1.0 + jnp.exp(v0 - v1))
    g0 = 1.0 - g1
    out = jnp.where(lane == 0, g0, jnp.where(lane == 1, g1, 0.0))
    out = jnp.where(lane == 2, e0.astype(F32), jnp.where(lane == 3, e1.astype(F32), out))
    rt_ref[...] = out


def _router(x, gain, router, tm=512):
    t, d = x.shape
    tm = min(tm, t)
    r_pad = jnp.zeros((d, LANES), F32).at[:, :N_EXPERTS].set(router.astype(F32))
    return pl.pallas_call(
        _router_body,
        grid=(t // tm,),
        in_specs=[pl.BlockSpec((tm, d), lambda i: (i, 0)), pl.BlockSpec((1, d), lambda i: (0, 0)),
                  pl.BlockSpec((d, LANES), lambda i: (0, 0))],
        out_specs=[pl.BlockSpec((tm, d), lambda i: (i, 0)), pl.BlockSpec((tm, LANES), lambda i: (i, 0))],
        out_shape=[jax.ShapeDtypeStruct((t, d), F32), jax.ShapeDtypeStruct((t, LANES), F32)],
        compiler_params=_cparams("parallel"),
        name="moe_router",
    )(x, gain.reshape(1, d).astype(F32), r_pad)


def _row_copy(src_hbm, src_row, dst_ref, dst_row, sem):
    return pltpu.make_async_copy(src_hbm.at[pl.ds(src_row, 1)], dst_ref.at[pl.ds(dst_row, 1)], sem)


def _rank_body(rt_ref, u_ref, ps_ref, dest_ref, carry_sc):
    k, i = pl.program_id(0), pl.program_id(1)
    tm = rt_ref.shape[0]

    @pl.when((k == 0) & (i == 0))
    def _():
        carry_sc[...] = jnp.zeros(carry_sc.shape, F32)

    rt_t = rt_ref[...].T
    e_row = jnp.where(k == 0, rt_t[2:3], rt_t[3:4]).astype(jnp.int32)
    sub = lax.broadcasted_iota(jnp.int32, (N_EXPERTS, tm), 0)
    onehot = sub == e_row
    ahead = jnp.dot(onehot.astype(BF16), u_ref[...], preferred_element_type=F32)
    row = ahead + carry_sc[:, :1] + ps_ref[:, :1]
    dest = jnp.sum(jnp.where(onehot, row, 0.0), axis=0, keepdims=True)
    dest_ref[...] = dest.astype(jnp.int32).reshape(dest_ref.shape)
    carry_sc[...] += jnp.sum(onehot.astype(F32), axis=1, keepdims=True)


def _assignment_rows(routing, pstart, tm=512):
    t = routing.shape[0]
    tm = min(tm, t)
    upper = jnp.asarray(np.triu(np.ones((tm, tm), np.float32), 1), BF16)
    ps = jnp.broadcast_to(pstart.astype(F32)[:, None], (N_EXPERTS, LANES))
    out = pl.pallas_call(
        _rank_body,
        grid=(TOP_K, t // tm),
        in_specs=[pl.BlockSpec((tm, LANES), lambda k, i: (i, 0)),
                  pl.BlockSpec((tm, tm), lambda k, i: (0, 0)),
                  pl.BlockSpec((N_EXPERTS, LANES), lambda k, i: (0, 0))],
        out_specs=pl.BlockSpec((1, 1, 1, tm), lambda k, i: (k, i, 0, 0)),
        out_shape=jax.ShapeDtypeStruct((TOP_K, t // tm, 1, tm), jnp.int32),
        scratch_shapes=[pltpu.VMEM((N_EXPERTS, LANES), F32)],
        compiler_params=_cparams("arbitrary", "arbitrary"),
        name="moe_rank",
    )(routing, upper, ps)
    return out.reshape(TOP_K * t)


def _dispatch_body(dest_ref, h_ref, xs_in, xs_hbm, sem):
    del xs_in
    tm = h_ref.shape[0]
    t = dest_ref.shape[0] // TOP_K
    base = pl.program_id(0) * tm

    def copy(r, k, dst_row):
        return pltpu.make_async_copy(h_ref.at[pl.ds(r, 1)], xs_hbm.at[pl.ds(dst_row, 1)], sem)

    def start(r, c):
        for k in range(TOP_K):
            copy(r, k, dest_ref[k * t + base + r]).start()
        return c

    lax.fori_loop(0, tm, start, 0, unroll=8)
    for r in range(tm):
        for k in range(TOP_K):
            copy(r, k, 0).wait()


def _dispatch(h, dest, n_rows, tm=256):
    t, d = h.shape
    tm = min(tm, t)
    grid_spec = pltpu.PrefetchScalarGridSpec(
        num_scalar_prefetch=1,
        grid=(t // tm,),
        in_specs=[pl.BlockSpec((tm, d), lambda i, dest: (i, 0)), pl.BlockSpec(memory_space=pl.ANY)],
        out_specs=pl.BlockSpec(memory_space=pl.ANY),
        scratch_shapes=[pltpu.SemaphoreType.DMA(())],
    )
    return pl.pallas_call(
        _dispatch_body,
        grid_spec=grid_spec,
        out_shape=jax.ShapeDtypeStruct((n_rows, d), F32),
        input_output_aliases={2: 0},
        compiler_params=_cparams("arbitrary"),
        name="moe_dispatch",
    )(dest, h, jnp.zeros((n_rows, d), F32))


def _experts_body(be_ref, nb_ref, x_ref, wg_ref, wu_ref, wd_ref, o_ref, xb_sc, acc_sc):
    i, f = pl.program_id(0), pl.program_id(1)

    @pl.when(i < nb_ref[0])
    def _():
        @pl.when(f == 0)
        def _():
            xb_sc[...] = x_ref[...].astype(BF16)
            acc_sc[...] = jnp.zeros(acc_sc.shape, F32)

        x = xb_sc[...]
        gate = jnp.dot(x, wg_ref[0], preferred_element_type=F32)
        up = jnp.dot(x, wu_ref[0], preferred_element_type=F32)
        act = (gate * jax.nn.sigmoid(gate) * up).astype(BF16)
        acc_sc[...] += jnp.dot(act, wd_ref[0], preferred_element_type=F32)

        @pl.when(f == pl.num_programs(1) - 1)
        def _():
            o_ref[...] = acc_sc[...]

    @pl.when(i >= nb_ref[0])
    def _():
        o_ref[...] = jnp.zeros(o_ref.shape, F32)


def _experts(xs, blk_e, n_used, wg, wu, wd, rows, tf=1792):
    n_rows, d = xs.shape
    ff = wg.shape[2]
    n_blk = n_rows // rows
    last = ff // tf - 1
    grid_spec = pltpu.PrefetchScalarGridSpec(
        num_scalar_prefetch=2,
        grid=(n_blk, ff // tf),
        in_specs=[pl.BlockSpec((rows, d), lambda i, f, be, nb: (jnp.minimum(i, nb[0] - 1), 0)),
                  pl.BlockSpec((1, d, tf), lambda i, f, be, nb: (be[i], 0, jnp.where(i < nb[0], f, last))),
                  pl.BlockSpec((1, d, tf), lambda i, f, be, nb: (be[i], 0, jnp.where(i < nb[0], f, last))),
                  pl.BlockSpec((1, tf, d), lambda i, f, be, nb: (be[i], jnp.where(i < nb[0], f, last), 0))],
        out_specs=pl.BlockSpec((rows, d), lambda i, f, be, nb: (i, 0)),
        scratch_shapes=[pltpu.VMEM((rows, d), BF16), pltpu.VMEM((rows, d), F32)],
    )
    return pl.pallas_call(
        _experts_body,
        grid_spec=grid_spec,
        out_shape=jax.ShapeDtypeStruct((n_rows, d), F32),
        compiler_params=_cparams("arbitrary", "arbitrary"),
        name="moe_experts",
    )(blk_e, n_used, xs, wg, wu, wd)


def _combine_body(dest_ref, x_ref, rt_ref, ys_hbm, o_ref, buf, sem):
    tm = x_ref.shape[0]
    n_tok = dest_ref.shape[0] // TOP_K
    i = pl.program_id(0)
    slot = i & 1

    def start(tile, s):
        def body(r, c):
            for k in range(TOP_K):
                _row_copy(ys_hbm, dest_ref[k * n_tok + tile * tm + r], buf.at[s, k], r, sem.at[s]).start()
            return c
        lax.fori_loop(0, tm, body, 0, unroll=8)

    @pl.when(i == 0)
    def _():
        start(0, 0)

    @pl.when(i + 1 < pl.num_programs(0))
    def _():
        start(i + 1, 1 - slot)

    for r in range(tm):
        for k in range(TOP_K):
            _row_copy(ys_hbm, 0, buf.at[slot, k], r, sem.at[slot]).wait()
    rt = rt_ref[...]
    o_ref[...] = x_ref[...] + rt[:, 0:1] * buf[slot, 0] + rt[:, 1:2] * buf[slot, 1]


def _combine(x, routing, ys, dest, tm=256):
    t, d = x.shape
    tm = min(tm, t)
    grid_spec = pltpu.PrefetchScalarGridSpec(
        num_scalar_prefetch=1,
        grid=(t // tm,),
        in_specs=[pl.BlockSpec((tm, d), lambda i, dest: (i, 0)),
                  pl.BlockSpec((tm, LANES), lambda i, dest: (i, 0)),
                  pl.BlockSpec(memory_space=pl.ANY)],
        out_specs=pl.BlockSpec((tm, d), lambda i, dest: (i, 0)),
        scratch_shapes=[pltpu.VMEM((2, TOP_K, tm, d), F32), pltpu.SemaphoreType.DMA((2,))],
    )
    return pl.pallas_call(
        _combine_body,
        grid_spec=grid_spec,
        out_shape=jax.ShapeDtypeStruct((t, d), F32),
        compiler_params=_cparams("arbitrary"),
        name="moe_combine",
    )(dest, x, routing, ys)


MOE_ROWS = 512


def _moe(x, gain, router, w_gate, w_up, w_down):
    t, d = x.shape
    h, routing = _router(x, gain, router)
    n_assign = t * TOP_K
    experts = routing[:, 2:2 + TOP_K].astype(jnp.int32)
    counts = jnp.sum((experts[:, :, None] == jnp.arange(N_EXPERTS)[None, None, :]).astype(jnp.int32), axis=(0, 1))
    padded = (counts + MOE_ROWS - 1) // MOE_ROWS * MOE_ROWS
    pend = jnp.cumsum(padded)
    pstart = pend - padded
    n_rows = -(-n_assign // MOE_ROWS) * MOE_ROWS + N_EXPERTS * MOE_ROWS
    n_blk = n_rows // MOE_ROWS
    blk_start = jnp.arange(n_blk, dtype=jnp.int32) * MOE_ROWS
    blk_e = jnp.minimum(jnp.sum((pend[None, :] <= blk_start[:, None]).astype(jnp.int32), axis=1), N_EXPERTS - 1)
    n_used = (pend[-1] // MOE_ROWS).astype(jnp.int32).reshape(1)
    dest = _assignment_rows(routing, pstart)
    xs = _dispatch(h, dest, n_rows)
    ys = _experts(xs, blk_e.astype(jnp.int32), n_used, w_gate.astype(BF16), w_up.astype(BF16),
                  w_down.astype(BF16), MOE_ROWS)
    return _combine(x, routing, ys, dest)


def _final_norm_body(x_ref, g_ref, o_ref):
    o_ref[...] = _rms(x_ref[...], g_ref[...])


def _final_norm(x, gain, tm=1024):
    t, d = x.shape
    tm = min(tm, t)
    return pl.pallas_call(
        _final_norm_body,
        grid=(t // tm,),
        in_specs=[pl.BlockSpec((tm, d), lambda i: (i, 0)), pl.BlockSpec((1, d), lambda i: (0, 0))],
        out_specs=pl.BlockSpec((tm, d), lambda i: (i, 0)),
        out_shape=jax.ShapeDtypeStruct((t, d), F32),
        compiler_params=_cparams("parallel"),
        name="final_norm",
    )(x, gain.reshape(1, d).astype(F32))


def _all_rope_tables(s):
    return {64: _rope_tables(s, 64), 32: _rope_tables(s, 32)}


def kernel(x, mem, norm_mix, w_in, w_out, diff_lambda, diff_subln, nsa_cmp_pos, nsa_cmp_w1, nsa_cmp_w2, mla_q_norm, mla_kv_norm, mla_w_uq, mla_w_ukv, norm_cross, norm_mem, cross_wq, cross_wk, cross_wv, cross_wo, norm_ffn, ffn_w_gate, ffn_w_up, ffn_w_down, moe_router, moe_w_gate, moe_w_up, moe_w_down, final_norm):
    b, s, d = x.shape
    depth = w_in.shape[0]
    tabs = _all_rope_tables(s)
    for l in range(depth):
        x = _token_mixers(x, l, tabs, norm_mix[l], w_in[l], w_out[l], diff_lambda[l], diff_subln[l],
                          nsa_cmp_pos[l], nsa_cmp_w1[l], nsa_cmp_w2[l], mla_q_norm[l], mla_kv_norm[l],
                          mla_w_uq[l], mla_w_ukv[l])
        x = _cross_attention(x, mem, norm_cross[l], norm_mem[l], cross_wq[l], cross_wk[l], cross_wv[l],
                             cross_wo[l])
        xt = x.reshape(b * s, d)
        i = l // 2
        if l % 2 == 0:
            xt = _swiglu(xt, norm_ffn[l], ffn_w_gate[i], ffn_w_up[i], ffn_w_down[i])
        else:
            xt = _moe(xt, norm_ffn[l], moe_router[i], moe_w_gate[i], moe_w_up[i], moe_w_down[i])
        x = xt.reshape(b, s, d)
    return _final_norm(x.reshape(b * s, d), final_norm).reshape(b, s, d)
```

```python
import functools
import math

import numpy as np
import jax
import jax.numpy as jnp
from jax import lax
from jax.experimental import pallas as pl
from jax.experimental.pallas import tpu as pltpu

F32 = jnp.float32
BF16 = jnp.bfloat16

HEAD_DIM = 64
GROUP_HEADS = 4
GROUP_WIDTH = HEAD_DIM * GROUP_HEADS
ROPE_THETA = 10000.0
NORM_EPS = 1e-6
DIFF_DK = HEAD_DIM // 2
CMP_LEN = 32
CMP_STRIDE = 16
SEL_BLOCK = 64
SEL_SHIFT = 6
SEL_TOPN = 16
NSA_WINDOW = 512
FORCED_SCORE = 1e6
MLA_NOPE = 64
MLA_ROPE = 32
MLA_DV = 64
DILATED_PATTERNS = ((128, 1), (512, 4), (2048, 16))
CROSS_HEADS = 4
N_EXPERTS = 8
TOP_K = 2

LANES = 128
V7X_VMEM_BYTES = 64 * 1024 * 1024
VMEM_LIMIT = V7X_VMEM_BYTES * 7 // 8

ONES_ROWS = 16
NEG_BIG = -1e30
SEL_BIAS = -1e9
LOG2E = math.log2(math.e)
LN2 = math.log(2.0)


def _cparams(*sem):
    return pltpu.CompilerParams(dimension_semantics=sem, vmem_limit_bytes=VMEM_LIMIT)


def _rms(xf, gain):
    ms = jnp.mean(xf * xf, axis=-1, keepdims=True)
    return xf * lax.rsqrt(ms + NORM_EPS) * gain


def _mm_body(*refs, nx, norm, res, splits):
    x_refs = refs[:nx]
    pos = nx
    g_ref = refs[pos] if norm else None
    pos += int(norm)
    w_ref = refs[pos]
    pos += 1
    r_ref = refs[pos] if res else None
    pos += int(res)
    o_refs = refs[pos:pos + len(splits)]
    xs_ref = refs[pos + len(splits)]

    @pl.when(pl.program_id(1) == 0)
    def _():
        off = 0
        for xr in x_refs:
            xv = xr[...]
            if norm:
                xv = _rms(xv.astype(F32), g_ref[...])
            kw = xv.shape[-1]
            xs_ref[:, off:off + kw] = xv.astype(BF16)
            off += kw

    acc = jnp.dot(xs_ref[...], w_ref[...], preferred_element_type=F32)
    if res:
        acc = acc + r_ref[...]
    off = 0
    for o_ref, width in zip(o_refs, splits):
        o_ref[...] = acc[:, off:off + width].astype(o_ref.dtype)
        off += width


def _mm(xs, w, *, gain=None, res=None, tm=512, tn=None, splits=None, out_dtype=F32, name="mm"):
    t = xs[0].shape[0]
    k, n = w.shape
    assert sum(x.shape[1] for x in xs) == k
    tn = n if tn is None else tn
    splits = (tn,) if splits is None else tuple(splits)
    assert sum(splits) == tn and (len(splits) == 1 or tn == n)
    tm = min(tm, t)
    assert t % tm == 0 and n % tn == 0
    norm = gain is not None
    assert not norm or len(xs) == 1
    in_specs = [pl.BlockSpec((tm, x.shape[1]), lambda i, j: (i, 0)) for x in xs]
    args = list(xs)
    if norm:
        in_specs.append(pl.BlockSpec((1, k), lambda i, j: (0, 0)))
        args.append(gain.reshape(1, k).astype(F32))
    in_specs.append(pl.BlockSpec((k, tn), lambda i, j: (0, j)))
    args.append(w)
    if res is not None:
        in_specs.append(pl.BlockSpec((tm, tn), lambda i, j: (i, j)))
        args.append(res)
    if len(splits) == 1:
        out_shape = [jax.ShapeDtypeStruct((t, n), out_dtype)]
        out_specs = [pl.BlockSpec((tm, tn), lambda i, j: (i, j))]
    else:
        out_shape = [jax.ShapeDtypeStruct((t, s), out_dtype) for s in splits]
        out_specs = [pl.BlockSpec((tm, s), lambda i, j: (i, 0)) for s in splits]
    outs = pl.pallas_call(
        functools.partial(_mm_body, nx=len(xs), norm=norm, res=res is not None, splits=splits),
        grid=(t // tm, n // tn),
        in_specs=in_specs,
        out_specs=out_specs,
        out_shape=out_shape,
        scratch_shapes=[pltpu.VMEM((tm, k), BF16)],
        compiler_params=_cparams("parallel", "arbitrary"),
        name=name,
    )(*args)
    return outs[0] if len(outs) == 1 else outs


def _rope_tables(s, group):
    half = group // 2
    pos = jnp.arange(s, dtype=F32)
    inv_freq = ROPE_THETA ** (-jnp.arange(half, dtype=F32) / half)
    ang = pos[:, None] * inv_freq[None, :]
    cos, sin = jnp.cos(ang), jnp.sin(ang)
    zero = jnp.zeros_like(sin)
    reps = LANES // group
    c = jnp.tile(jnp.concatenate([cos, cos], -1), (1, reps))
    s1 = jnp.tile(jnp.concatenate([zero, sin], -1), (1, reps))
    s2 = jnp.tile(jnp.concatenate([-sin, zero], -1), (1, reps))
    return c, s1, s2


def _rope(x, tabs, half):
    w = x.shape[-1]
    c, s1, s2 = (jnp.tile(t, (1, w // LANES)) for t in tabs)
    return x * c + pltpu.roll(x, half, 1) * s1 + pltpu.roll(x, w - half, 1) * s2


def _prep_a(z, t32, q_ref, k_ref, vt_ref):
    qk = _rope(z[:, :512], t32, DIFF_DK // 2)
    lane = lax.broadcasted_iota(jnp.int32, (z.shape[0], HEAD_DIM), 1)
    scale = DIFF_DK ** -0.5 * LOG2E
    for h in range(GROUP_HEADS):
        q = qk[:, h * 64:(h + 1) * 64] * scale
        q_ref[0, h, 0] = jnp.where(lane < DIFF_DK, q, 0.0).astype(BF16)
        q_ref[0, h, 1] = jnp.where(lane >= DIFF_DK, q, 0.0).astype(BF16)
        k_ref[0, h] = qk[:, 256 + h * 64:256 + (h + 1) * 64].astype(BF16)
    vt_ref[0] = z[:, 512:768].T.reshape(vt_ref.shape[1:]).astype(BF16)


def _prep_b(z, t64, qn_ref, qr_ref, ka_ref, kw_ref, kc_ref, vc_ref, vs_ref, vw_ref, g_ref):
    ts = z.shape[0]
    scale = HEAD_DIM ** -0.5 * LOG2E
    rp = _rope(z[:, :384], t64, HEAD_DIM // 2)
    for h in range(GROUP_HEADS):
        qn_ref[0, h] = (z[:, h * 64:(h + 1) * 64] * scale).astype(BF16)
        qr_ref[0, h] = (rp[:, h * 64:(h + 1) * 64] * scale).astype(BF16)
    kpos = pl.program_id(0) * ts + lax.broadcasted_iota(jnp.int32, (ts, SEL_BLOCK), 0)
    lane = lax.broadcasted_iota(jnp.int32, (ts, SEL_BLOCK), 1)
    onehot = jnp.where((kpos >> SEL_SHIFT) == lane, 1.0, 0.0)
    ka_ref[0] = jnp.concatenate([rp[:, 256:320], onehot], axis=-1).astype(BF16)
    kw_ref[0] = rp[:, 320:384].astype(BF16)
    kc_ref[0] = z[:, 384:448].astype(BF16)
    vc_ref[0] = z[:, 448:512].astype(BF16)
    vt = z[:, 512:640].T.astype(BF16)
    vs_ref[0] = vt[:HEAD_DIM]
    vw_ref[0] = vt[HEAD_DIM:]
    g_ref[0] = jax.nn.sigmoid(z[:, 640:768])


def _prep_c(z, t32, qg_ref, kg_ref, wq_ref, wkv_ref, q_ref, k_ref, vt_ref):
    scale = (MLA_NOPE + MLA_ROPE) ** -0.5 * LOG2E
    cq = _rms(z[:, :384], qg_ref[...]).astype(BF16)
    ckv = _rms(z[:, 384:512], kg_ref[...]).astype(BF16)
    qc = jnp.dot(cq, wq_ref[...], preferred_element_type=F32)
    kvc = jnp.dot(ckv, wkv_ref[...], preferred_element_type=F32)
    half = MLA_ROPE // 2
    q_rope = _rope(qc[:, 256:384], t32, half)
    k_rope = _rope(z[:, 512:640], t32, half)[:, :MLA_ROPE]
    for h in range(GROUP_HEADS):
        q = jnp.concatenate([qc[:, h * 64:(h + 1) * 64], q_rope[:, h * 32:(h + 1) * 32]], axis=-1)
        q_ref[0, h] = (q * scale).astype(BF16)
        k_ref[0, h] = jnp.concatenate([kvc[:, h * 64:(h + 1) * 64], k_rope], axis=-1).astype(BF16)
    vt_ref[0] = kvc[:, 256:512].T.reshape(vt_ref.shape[1:]).astype(BF16)


def _prep_d(z, t64, q_ref, k_ref, vt_ref):
    scale = HEAD_DIM ** -0.5 * LOG2E
    qk = _rope(z[:, :512], t64, HEAD_DIM // 2)
    for h in range(GROUP_HEADS):
        q_ref[0, h] = (qk[:, h * 64:(h + 1) * 64] * scale).astype(BF16)
        k_ref[0, h] = qk[:, 256 + h * 64:256 + (h + 1) * 64].astype(BF16)
    vt_ref[0] = z[:, 512:768].T.reshape(vt_ref.shape[1:]).astype(BF16)


Z_SPLITS = (768, 768, 640, 768)
N_PREP_OUT = (3, 9, 3, 3)


def _project_prep_body(*refs):
    x_ref, g_ref, w_ref = refs[:3]
    t64 = tuple(r[...] for r in refs[3:6])
    t32 = tuple(r[...] for r in refs[6:9])
    qg_ref, kg_ref, wq_ref, wkv_ref = refs[9:13]
    outs = refs[13:]
    h = _rms(x_ref[0], g_ref[...]).astype(BF16)
    z = jnp.dot(h, w_ref[...], preferred_element_type=F32)
    o0, o1, o2, o3 = np.cumsum((0,) + Z_SPLITS[:3])
    n0, n1, n2 = np.cumsum(N_PREP_OUT[:3])
    _prep_a(z[:, o0:o0 + Z_SPLITS[0]], t32, *outs[:n0])
    _prep_b(z[:, o1:o1 + Z_SPLITS[1]], t64, *outs[n0:n1])
    _prep_c(z[:, o2:o2 + Z_SPLITS[2]], t32, qg_ref, kg_ref, wq_ref, wkv_ref, *outs[n1:n2])
    _prep_d(z[:, o3:o3 + Z_SPLITS[3]], t64, *outs[n2:])


def _project_prep(x, gain, w, tabs64, tabs32, q_gain, kv_gain, wq, wkv, ts):
    b, s, d = x.shape
    dk = MLA_NOPE + MLA_ROPE
    full = lambda a: pl.BlockSpec(a.shape, lambda i, bi: (0,) * a.ndim)
    tspec = pl.BlockSpec((ts, LANES), lambda i, bi: (i, 0))
    h4 = lambda dd: pl.BlockSpec((1, 4, ts, dd), lambda i, bi: (bi, 0, i, 0))
    h4t = pl.BlockSpec((1, 4, 64, ts), lambda i, bi: (bi, 0, 0, i))
    sq = lambda dd: pl.BlockSpec((1, ts, dd), lambda i, bi: (bi, i, 0))
    t64 = pl.BlockSpec((1, 64, ts), lambda i, bi: (bi, 0, i))
    sd = lambda *shape, dt=BF16: jax.ShapeDtypeStruct(shape, dt)
    out_specs = [pl.BlockSpec((1, 4, 2, ts, 64), lambda i, bi: (bi, 0, 0, i, 0)), h4(64), h4t,
                 h4(64), h4(64), sq(128), sq(64), sq(64), sq(64), t64, t64, sq(128),
                 h4(dk), h4(dk), h4t,
                 h4(64), h4(64), h4t]
    out_shape = [sd(b, 4, 2, s, 64), sd(b, 4, s, 64), sd(b, 4, 64, s),
                 sd(b, 4, s, 64), sd(b, 4, s, 64), sd(b, s, 128), sd(b, s, 64), sd(b, s, 64), sd(b, s, 64),
                 sd(b, 64, s), sd(b, 64, s), sd(b, s, 128, dt=F32),
                 sd(b, 4, s, dk), sd(b, 4, s, dk), sd(b, 4, 64, s),
                 sd(b, 4, s, 64), sd(b, 4, s, 64), sd(b, 4, 64, s)]
    g2 = gain.reshape(1, d).astype(F32)
    outs = pl.pallas_call(
        _project_prep_body,
        grid=(s // ts, b),
        in_specs=[pl.BlockSpec((1, ts, d), lambda i, bi: (bi, i, 0)), full(g2), full(w)] + [tspec] * 6
                 + [full(q_gain), full(kv_gain), full(wq), full(wkv)],
        out_specs=out_specs,
        out_shape=out_shape,
        compiler_params=_cparams("parallel", "parallel"),
        name="w_in_prep",
    )(x, g2, w, *tabs64, *tabs32, q_gain, kv_gain, wq, wkv)
    n0, n1, n2 = np.cumsum(N_PREP_OUT[:3])
    return outs[:n0], outs[n0:n1], outs[n1:n2], outs[n2:]


def _flash_schedule(nq, tq, tk, sk, window):
    qi, kj, fl = [], [], []
    for i in range(nq):
        q_lo, q_hi = i * tq, i * tq + tq - 1
        j_hi = min(q_hi, sk - 1) // tk
        j_lo = 0 if window is None else max(0, q_lo - window) // tk
        for j in range(j_lo, j_hi + 1):
            k_min, k_max = j * tk, j * tk + tk - 1
            full = k_max <= q_lo and (window is None or q_hi - k_min <= window)
            qi.append(i)
            kj.append(j)
            fl.append((1 if j == j_lo else 0) | (2 if j == j_hi else 0) | (0 if full else 4))
    return (np.asarray(qi, np.int32), np.asarray(kj, np.int32), np.asarray(fl, np.int32))


def _flash_body(qi_ref, kj_ref, fl_ref, q_ref, k_ref, vt_ref, o_ref, m_sc, acc_sc, *,
                gb, hq, tq, tk, window, with_lse, merge_groups):
    t = pl.program_id(1)
    fl = fl_ref[t]
    rows = hq * tq
    dv = vt_ref.shape[1]

    @pl.when((fl & 1) != 0)
    def _():
        m_sc[...] = jnp.full(m_sc.shape, NEG_BIG, F32)
        acc_sc[...] = jnp.zeros(acc_sc.shape, F32)

    def step(masked):
        q = q_ref[...].reshape(gb, rows, q_ref.shape[-1])
        st = jnp.einsum("gkd,gqd->gkq", k_ref[...], q, preferred_element_type=F32)
        if masked:
            krow = lax.broadcasted_iota(jnp.int32, (tk, rows), 0)
            qcol = lax.broadcasted_iota(jnp.int32, (tk, rows), 1)
            if hq > 1:
                qcol = qcol & (tq - 1)
            dist = (qi_ref[t] * tq - kj_ref[t] * tk) + qcol - krow
            ok = dist >= 0
            if window is not None:
                ok = ok & (dist <= window)
            st = jnp.where(ok[None], st, NEG_BIG)
        m_prev = m_sc[...]
        m_new = jnp.maximum(m_prev, jnp.max(st, axis=1, keepdims=True))
        alpha = jnp.exp2(m_prev - m_new)
        p = jnp.exp2((st - m_new).astype(BF16))
        v_ext = jnp.concatenate([vt_ref[...], jnp.ones((gb, ONES_ROWS, tk), BF16)], axis=1)
        acc_sc[...] = alpha * acc_sc[...] + jnp.einsum("gdk,gkq->gdq", v_ext, p, preferred_element_type=F32)
        m_sc[...] = m_new

    @pl.when((fl & 4) != 0)
    def _():
        step(True)

    @pl.when((fl & 4) == 0)
    def _():
        step(False)

    @pl.when((fl & 2) != 0)
    def _():
        l = acc_sc[:, dv:dv + 1, :]
        o_t = acc_sc[:, :dv, :] / l
        if merge_groups:
            o_ref[0] = o_t.reshape(gb * dv, rows).T.astype(o_ref.dtype)
            return
        if with_lse:
            extra = jnp.broadcast_to(m_sc[...] * LN2 + jnp.log(l), o_t.shape)
        else:
            extra = jnp.zeros((gb, LANES - dv, rows), F32)
        dvo = o_ref.shape[-1]
        for g in range(gb):
            o = jnp.concatenate([o_t[g], extra[g]], axis=0).T
            o_ref[g] = o[:, :dvo].reshape(o_ref.shape[1:]).astype(o_ref.dtype)


def _flash(q, k, vt, *, tq, tk, gb=1, window=None, with_lse=False, merge_groups=False, out_dtype=F32,
           name="flash"):
    g, hq, sq, dk = q.shape
    _, dv, sk = vt.shape
    tq, tk = min(tq, sq), min(tk, sk)
    assert g % gb == 0 and sq % tq == 0 and sk % tk == 0
    assert tq & (tq - 1) == 0 and 2 * dv == LANES
    qi, kj, fl = _flash_schedule(sq // tq, tq, tk, sk, window)
    dvo = 2 * dv if with_lse else dv
    rows = hq * tq
    if merge_groups:
        assert hq == 1 and not with_lse and (gb * dv) % LANES == 0
        out_spec = pl.BlockSpec((1, tq, gb * dv), lambda gi, t, qi, kj, fl: (gi, qi[t], 0))
        out_shape = jax.ShapeDtypeStruct((g // gb, sq, gb * dv), out_dtype)
    else:
        out_spec = pl.BlockSpec((gb, hq, tq, dvo), lambda gi, t, qi, kj, fl: (gi, 0, qi[t], 0))
        out_shape = jax.ShapeDtypeStruct((g, hq, sq, dvo), out_dtype)
    grid_spec = pltpu.PrefetchScalarGridSpec(
        num_scalar_prefetch=3,
        grid=(g // gb, len(qi)),
        in_specs=[pl.BlockSpec((gb, hq, tq, dk), lambda gi, t, qi, kj, fl: (gi, 0, qi[t], 0)),
                  pl.BlockSpec((gb, tk, dk), lambda gi, t, qi, kj, fl: (gi, kj[t], 0)),
                  pl.BlockSpec((gb, dv, tk), lambda gi, t, qi, kj, fl: (gi, 0, kj[t]))],
        out_specs=out_spec,
        scratch_shapes=[pltpu.VMEM((gb, 1, rows), F32), pltpu.VMEM((gb, dv + ONES_ROWS, rows), F32)],
    )
    return pl.pallas_call(
        functools.partial(_flash_body, gb=gb, hq=hq, tq=tq, tk=tk, window=window, with_lse=with_lse,
                          merge_groups=merge_groups),
        grid_spec=grid_spec,
        out_shape=out_shape,
        compiler_params=_cparams("parallel", "arbitrary"),
        name=name,
    )(jnp.asarray(qi), jnp.asarray(kj), jnp.asarray(fl), q, k, vt)


def _band_body(q_ref, kp_ref, kc_ref, vp_ref, vc_ref, o_ref, *, tq, window):
    i = pl.program_id(1)
    gb, dv = vp_ref.shape[0], vp_ref.shape[1]
    k2 = jnp.concatenate([kp_ref[...], kc_ref[...]], axis=1)
    st = jnp.einsum("gkd,gqd->gkq", k2, q_ref[...], preferred_element_type=F32)
    krow = lax.broadcasted_iota(jnp.int32, (2 * tq, tq), 0)
    qcol = lax.broadcasted_iota(jnp.int32, (2 * tq, tq), 1)
    dist = qcol + tq - krow
    ok = (dist >= 0) & (dist <= window) & ((krow >= tq) | (i > 0))
    st = jnp.where(ok[None], st, NEG_BIG)
    m = jnp.max(st, axis=1, keepdims=True)
    p = jnp.exp2((st - m).astype(BF16))
    v_ext = jnp.concatenate([jnp.concatenate([vp_ref[...], vc_ref[...]], axis=2),
                             jnp.ones((gb, ONES_ROWS, 2 * tq), BF16)], axis=1)
    acc = jnp.einsum("gdk,gkq->gdq", v_ext, p, preferred_element_type=F32)
    l = acc[:, dv:dv + 1, :]
    o_t = acc[:, :dv, :] / l
    lse = jnp.broadcast_to(m * LN2 + jnp.log(l), o_t.shape)
    for g in range(gb):
        o_ref[g] = jnp.concatenate([o_t[g], lse[g]], axis=0).T.astype(o_ref.dtype)


def _band_attention(q, k, vt, *, tq, gb, window, name):
    g, s, dk = q.shape
    dv = vt.shape[1]
    tq = min(tq, s)
    assert window <= tq and g % gb == 0 and s % tq == 0 and 2 * dv == LANES
    prev = lambda i: jnp.maximum(i - 1, 0)
    return pl.pallas_call(
        functools.partial(_band_body, tq=tq, window=window),
        grid=(g // gb, s // tq),
        in_specs=[pl.BlockSpec((gb, tq, dk), lambda gi, i: (gi, i, 0)),
                  pl.BlockSpec((gb, tq, dk), lambda gi, i: (gi, prev(i), 0)),
                  pl.BlockSpec((gb, tq, dk), lambda gi, i: (gi, i, 0)),
                  pl.BlockSpec((gb, dv, tq), lambda gi, i: (gi, 0, prev(i))),
                  pl.BlockSpec((gb, dv, tq), lambda gi, i: (gi, 0, i))],
        out_specs=pl.BlockSpec((gb, tq, 2 * dv), lambda gi, i: (gi, i, 0)),
        out_shape=jax.ShapeDtypeStruct((g, s, 2 * dv), F32),
        compiler_params=_cparams("parallel", "parallel"),
        name=name,
    )(q, k, k, vt, vt)


def _gelu_tanh(x):
    return 0.5 * x * (1.0 + jnp.tanh(math.sqrt(2.0 / math.pi) * (x + 0.044715 * (x * x * x))))


def _compress_body(r_ref, pos_ref, w1_ref, w2_ref, o_ref, ot_ref):
    half = CMP_STRIDE * HEAD_DIM
    r = r_ref[0, 0]
    w1 = w1_ref[0]
    top = jnp.dot(r, w1[:half], preferred_element_type=F32)
    bot = jnp.dot(r, w1[half:], preferred_element_type=F32)
    nc = r.shape[0]
    posb = jnp.dot(pos_ref[0], w1, preferred_element_type=F32)[:1]
    pre = top + pltpu.roll(bot, nc - 1, 0) + posb
    hid = _gelu_tanh(pre).astype(BF16)
    o = jnp.dot(hid, w2_ref[0], preferred_element_type=F32)
    o_ref[0, 0] = o.astype(o_ref.dtype)
    ot_ref[0, 0] = jnp.concatenate([o, jnp.zeros_like(o)], axis=1).T[:HEAD_DIM].astype(ot_ref.dtype)


def _nsa_compress(kv_rows, pos_flat, w1, w2):
    _, b, nc, _ = kv_rows.shape
    return pl.pallas_call(
        _compress_body,
        grid=(2, b),
        in_specs=[pl.BlockSpec((1, 1, nc, 1024), lambda w, bi: (w, bi, 0, 0)),
                  pl.BlockSpec((1, 8, 2048), lambda w, bi: (w, 0, 0)),
                  pl.BlockSpec((1, 2048, 256), lambda w, bi: (w, 0, 0)),
                  pl.BlockSpec((1, 256, 64), lambda w, bi: (w, 0, 0))],
        out_specs=[pl.BlockSpec((1, 1, nc, 64), lambda w, bi: (w, bi, 0, 0)),
                   pl.BlockSpec((1, 1, 64, nc), lambda w, bi: (w, bi, 0, 0))],
        out_shape=[jax.ShapeDtypeStruct((2, b, nc, 64), BF16), jax.ShapeDtypeStruct((2, b, 64, nc), BF16)],
        compiler_params=_cparams("parallel", "parallel"),
        name="nsa_compress",
    )(kv_rows, pos_flat, w1, w2)


def _cmp_select_body(qn_ref, qr_ref, kc_ref, vct_ref, covt_ref, o_ref, qa_ref, *, tq, n_sel, k_top):
    ncp = kc_ref.shape[2]
    qpos = pl.program_id(1) * tq + lax.broadcasted_iota(jnp.int32, (ncp, tq), 1)
    blk_end = lax.broadcasted_iota(jnp.int32, (ncp, tq), 0) * CMP_STRIDE + (CMP_LEN - 1)
    mask = blk_end <= qpos
    kc, vct = kc_ref[0, 0], vct_ref[0, 0]
    p_sum = jnp.zeros((ncp, tq), F32)
    outs = []
    for h in range(GROUP_HEADS):
        st = lax.dot_general(kc, qn_ref[0, h], (((1,), (1,)), ((), ())), preferred_element_type=F32)
        st = jnp.where(mask, st, NEG_BIG)
        mx = jnp.max(st, axis=0, keepdims=True)
        mx = jnp.where(mx > 0.5 * NEG_BIG, mx, 0.0)
        e = jnp.where(mask, jnp.exp2(st - mx), 0.0)
        p = e / jnp.maximum(jnp.sum(e, axis=0, keepdims=True), 1e-30)
        outs.append(jnp.dot(vct, p.astype(BF16), preferred_element_type=F32))
        p_sum = p_sum + p
    o_ref[0] = jnp.concatenate(outs, axis=0).T.astype(o_ref.dtype)
    p_hi = p_sum.astype(BF16)
    p_lo = (p_sum - p_hi.astype(F32)).astype(BF16)
    covt = covt_ref[...]
    imp_t = (jnp.dot(covt, p_hi, preferred_element_type=F32) + jnp.dot(covt, p_lo, preferred_element_type=F32))
    nblk = SEL_BLOCK
    tpos = pl.program_id(1) * tq + lax.broadcasted_iota(jnp.int32, (nblk, tq), 1)
    cur = tpos >> SEL_SHIFT
    j = lax.broadcasted_iota(jnp.int32, (nblk, tq), 0)
    valid = j <= cur
    forced = (j == 0) | (j == cur) | (j == cur - 1)
    score = jnp.where(valid, jnp.where(forced, FORCED_SCORE, imp_t), -jnp.inf)
    sub = 8
    groups = [score[g * sub:(g + 1) * sub] for g in range(nblk // sub)]
    ranks = [jnp.zeros((sub, tq), jnp.int32) for _ in groups]
    jj = lax.broadcasted_iota(jnp.int32, (sub, tq), 0)
    for i in range(n_sel):
        row = jnp.broadcast_to(score[i:i + 1], (sub, tq))
        for g, sg in enumerate(groups):
            if i < g * sub:
                ahead = row >= sg
            elif i >= (g + 1) * sub:
                ahead = row > sg
            else:
                ahead = (row > sg) | ((row == sg) & (i - g * sub < jj))
            ranks[g] = ranks[g] + ahead.astype(jnp.int32)
    rank = jnp.concatenate(ranks, axis=0)
    bias_t = jnp.where(valid & (rank < k_top), 0.0, SEL_BIAS)
    bias = jnp.concatenate([bias_t, jnp.zeros((LANES - nblk, tq), F32)], axis=0).T[:, :nblk].astype(BF16)
    for h in range(GROUP_HEADS):
        qa_ref[0, h] = jnp.concatenate([qr_ref[0, h], bias], axis=-1)


def _cover_matrix(ncp, n_sel, lanes):
    c_start = np.arange(ncp)[:, None] * CMP_STRIDE
    s_start = np.arange(lanes)[None, :] * SEL_BLOCK
    cov = np.clip(np.minimum(c_start + CMP_LEN, s_start + SEL_BLOCK) - np.maximum(c_start, s_start), 0, None)
    cov = cov.astype(np.float32) / CMP_LEN
    cov[:, n_sel:] = 0.0
    cov[ncp - 1:, :] = 0.0
    return cov


def _cmp_select(qn, qr, kvc, kvct, tq):
    b, _, s, _ = qn.shape
    ncp = kvc.shape[2]
    n_sel = s // SEL_BLOCK
    assert n_sel <= SEL_BLOCK
    k_top = min(SEL_TOPN, n_sel)
    covt = jnp.asarray(_cover_matrix(ncp, n_sel, SEL_BLOCK).T, BF16)
    h4 = lambda d: pl.BlockSpec((1, 4, tq, d), lambda bi, i: (bi, 0, i, 0))
    return pl.pallas_call(
        functools.partial(_cmp_select_body, tq=tq, n_sel=n_sel, k_top=k_top),
        grid=(b, s // tq),
        in_specs=[h4(64), h4(64),
                  pl.BlockSpec((1, 1, ncp, 64), lambda bi, i: (0, bi, 0, 0)),
                  pl.BlockSpec((1, 1, 64, ncp), lambda bi, i: (1, bi, 0, 0)),
                  pl.BlockSpec(covt.shape, lambda bi, i: (0, 0))],
        out_specs=[pl.BlockSpec((1, tq, 256), lambda bi, i: (bi, i, 0)), h4(128)],
        out_shape=[jax.ShapeDtypeStruct((b, s, 256), BF16), jax.ShapeDtypeStruct((b, 4, s, 128), BF16)],
        compiler_params=_cparams("parallel", "parallel"),
        name="nsa_cmp_select",
    )(qn, qr, kvc, kvct, covt)


def _merge_a_body(o_ref, lam_ref, g_ref, y_ref, *, out_scale):
    outs = []
    for h in range(GROUP_HEADS):
        o = o_ref[0, h, 0].astype(F32) - lam_ref[...] * o_ref[0, h, 1].astype(F32)
        outs.append(_rms(o, g_ref[...]) * out_scale)
    y_ref[0] = jnp.concatenate(outs, axis=-1).astype(y_ref.dtype)


def _merge_a(o, lam, subln, out_scale, ts):
    b, _, _, s, _ = o.shape
    return pl.pallas_call(
        functools.partial(_merge_a_body, out_scale=out_scale),
        grid=(b, s // ts),
        in_specs=[pl.BlockSpec((1, 4, 2, ts, 64), lambda bi, i: (bi, 0, 0, i, 0)),
                  pl.BlockSpec((1, 64), lambda bi, i: (0, 0)),
                  pl.BlockSpec((1, 64), lambda bi, i: (0, 0))],
        out_specs=pl.BlockSpec((1, ts, 256), lambda bi, i: (bi, i, 0)),
        out_shape=jax.ShapeDtypeStruct((b, s, 256), BF16),
        compiler_params=_cparams("parallel", "parallel"),
        name="merge_a",
    )(o, lam, subln)


def _merge_b_body(oc_ref, os_ref, ow_ref, g_ref, y_ref):
    g = g_ref[0]
    outs = []
    for h in range(GROUP_HEADS):
        outs.append(g[:, 3 * h:3 * h + 1] * oc_ref[0, :, h * 64:(h + 1) * 64].astype(F32)
                    + g[:, 3 * h + 1:3 * h + 2] * os_ref[0, h].astype(F32)
                    + g[:, 3 * h + 2:3 * h + 3] * ow_ref[0, h].astype(F32))
    y_ref[0] = jnp.concatenate(outs, axis=-1).astype(y_ref.dtype)


def _merge_b(o_cmp, o_sel, o_win, gates, ts):
    b, s, _ = o_cmp.shape
    h4 = pl.BlockSpec((1, 4, ts, 64), lambda bi, i: (bi, 0, i, 0))
    return pl.pallas_call(
        _merge_b_body,
        grid=(b, s // ts),
        in_specs=[pl.BlockSpec((1, ts, 256), lambda bi, i: (bi, i, 0)), h4, h4,
                  pl.BlockSpec((1, ts, 128), lambda bi, i: (bi, i, 0))],
        out_specs=pl.BlockSpec((1, ts, 256), lambda bi, i: (bi, i, 0)),
        out_shape=jax.ShapeDtypeStruct((b, s, 256), BF16),
        compiler_params=_cparams("parallel", "parallel"),
        name="merge_b",
    )(o_cmp, o_sel, o_win, gates)


def _merge_d_body(o1_ref, o2_ref, o3_ref, y_ref):
    outs = []
    for h in range(GROUP_HEADS):
        e = [r[0, h] for r in (o1_ref, o2_ref, o3_ref)]
        lse = [x[:, 64:] for x in e]
        mx = jnp.maximum(jnp.maximum(lse[0], lse[1]), lse[2])
        w = [jnp.exp(x - mx) for x in lse]
        den = w[0] + w[1] + w[2]
        outs.append((w[0] * e[0][:, :64] + w[1] * e[1][:, :64] + w[2] * e[2][:, :64]) / den)
    y_ref[0] = jnp.concatenate(outs, axis=-1).astype(y_ref.dtype)


def _merge_d(o1, o2, o3, ts):
    b, _, s, _ = o1.shape
    h4 = pl.BlockSpec((1, 4, ts, 128), lambda bi, i: (bi, 0, i, 0))
    return pl.pallas_call(
        _merge_d_body,
        grid=(b, s // ts),
        in_specs=[h4, h4, h4],
        out_specs=pl.BlockSpec((1, ts, 256), lambda bi, i: (bi, i, 0)),
        out_shape=jax.ShapeDtypeStruct((b, s, 256), BF16),
        compiler_params=_cparams("parallel", "parallel"),
        name="merge_d",
    )(o1, o2, o3)


def _permute_w_in(w_in):
    a_cols = 768
    b0 = a_cols
    nq = 256
    seg = lambda i: w_in[:, b0 + nq + i * 64: b0 + nq + (i + 1) * 64]
    kc, vc, ks, vs, kw, vw = (seg(i) for i in range(6))
    gb = w_in[:, b0 + nq + 384: b0 + nq + 384 + 12]
    d = w_in.shape[0]
    w_b = jnp.concatenate([w_in[:, b0:b0 + nq], ks, kw, kc, vc, vs, vw, gb, jnp.zeros((d, 116), w_in.dtype)], axis=1)
    c0 = b0 + 652
    w_c = jnp.concatenate([w_in[:, c0:c0 + 544], jnp.zeros((d, 96), w_in.dtype)], axis=1)
    d0 = c0 + 544
    return jnp.concatenate([w_in[:, :a_cols], w_b, w_c, w_in[:, d0:d0 + 768]], axis=1).astype(BF16)


def _residue_classes(t, dil):
    g, s, d = t.shape
    return t.reshape(g, s // dil, dil, d).transpose(0, 2, 1, 3).reshape(g * dil, s // dil, d)


def _residue_classes_t(t, dil):
    g, d, s = t.shape
    return t.reshape(g, d, s // dil, dil).transpose(0, 3, 1, 2).reshape(g * dil, d, s // dil)


def _from_residue_classes(t, dil):
    gd, l, d = t.shape
    return t.reshape(gd // dil, dil, l, d).transpose(0, 2, 1, 3).reshape(gd // dil, l * dil, d)


def _token_mixers(x, layer, rope_tabs, norm_g, w_in, w_out, diff_lambda, diff_subln, cmp_pos, cmp_w1, cmp_w2,
                  q_norm, kv_norm, w_uq, w_ukv):
    b, s, d = x.shape
    t = b * s
    ts = min(512, s)
    wq = w_uq.reshape(-1, 4, MLA_NOPE + MLA_ROPE)
    wq = jnp.concatenate([wq[:, :, :MLA_NOPE].reshape(-1, 256), wq[:, :, MLA_NOPE:].reshape(-1, 128)], axis=1)
    wkv = w_ukv.reshape(-1, 4, MLA_NOPE + MLA_DV)
    wkv = jnp.concatenate([wkv[:, :, :MLA_NOPE].reshape(-1, 256), wkv[:, :, MLA_NOPE:].reshape(-1, 256)], axis=1)
    prep_a, prep_b, prep_c, prep_d = _project_prep(
        x, norm_g, _permute_w_in(w_in), rope_tabs[64], rope_tabs[32], q_norm.reshape(1, -1).astype(F32),
        kv_norm.reshape(1, -1).astype(F32), wq.astype(BF16), wkv.astype(BF16), ts)

    qa, ka, va = prep_a
    oa = _flash(qa.reshape(b * 4, 2, s, 64), ka.reshape(b * 4, s, 64), va.reshape(b * 4, 64, s),
                tq=512, tk=512, gb=2, out_dtype=BF16, name="flash_diff").reshape(b, 4, 2, s, 64)
    lam_init = 0.8 - 0.6 * math.exp(-0.3 * layer)
    lp = diff_lambda.astype(F32)
    lam = jnp.exp(jnp.sum(lp[0] * lp[1])) - jnp.exp(jnp.sum(lp[2] * lp[3])) + lam_init
    y_a = _merge_a(oa, jnp.full((1, 64), lam, F32), diff_subln.reshape(1, 64).astype(F32), 1.0 - lam_init, ts)

    qn, qr, k_aug, kw, kc, vc, vs, vw, gates = prep_b
    kv_rows = jnp.stack([kc, vc]).reshape(2, b, s // CMP_STRIDE, CMP_STRIDE * HEAD_DIM)
    pos_flat = jnp.broadcast_to(cmp_pos.reshape(2, 1, CMP_LEN * HEAD_DIM), (2, 8, CMP_LEN * HEAD_DIM)).astype(BF16)
    kvc, kvct = _nsa_compress(kv_rows, pos_flat, cmp_w1.astype(BF16), cmp_w2.astype(BF16))
    o_cmp, q_aug = _cmp_select(qn, qr, kvc, kvct, min(256, s))
    o_sel = _flash(q_aug, k_aug, vs, tq=256, tk=512, out_dtype=BF16, name="flash_sel")
    o_win = _flash(qr, kw, vw, tq=256, tk=256, window=NSA_WINDOW - 1, out_dtype=BF16, name="flash_win")
    y_b = _merge_b(o_cmp, o_sel, o_win, gates, ts)

    qc, kcat, vcv = prep_c
    dk = MLA_NOPE + MLA_ROPE
    y_c = _flash(qc.reshape(b * 4, 1, s, dk), kcat.reshape(b * 4, s, dk), vcv.reshape(b * 4, 64, s),
                 tq=512, tk=512, gb=4, merge_groups=True, out_dtype=BF16, name="flash_mla")

    qd, kd, vd = prep_d
    qd, kd, vd = qd.reshape(b * 4, s, 64), kd.reshape(b * 4, s, 64), vd.reshape(b * 4, 64, s)
    o_d = []
    for window, dil in DILATED_PATTERNS:
        l = s // dil
        qq, kk, vv = _residue_classes(qd, dil), _residue_classes(kd, dil), _residue_classes_t(vd, dil)
        g = qq.shape[0]
        gb = math.gcd(g, 16)
        o = _band_attention(qq, kk, vv, tq=128, gb=gb, window=window // dil, name=f"band_dil{dil}")
        o_d.append(_from_residue_classes(o, dil).reshape(b, 4, s, 128))
    y_d = _merge_d(*o_d, ts)

    ys = [y.reshape(t, 256) for y in (y_a, y_b, y_c, y_d)]
    return _mm(ys, w_out.astype(BF16), res=x.reshape(t, d), tm=min(512, t), name="w_out").reshape(b, s, d)


def _cross_body(x_ref, g_ref, wq_ref, k_ref, v_ref, wo_ref, o_ref):
    x = x_ref[0]
    h = _rms(x, g_ref[...]).astype(BF16)
    q = jnp.dot(h, wq_ref[...], preferred_element_type=F32).astype(BF16)
    dh = q.shape[-1] // CROSS_HEADS
    outs = []
    for hd in range(CROSS_HEADS):
        sl = slice(hd * dh, (hd + 1) * dh)
        s = lax.dot_general(q[:, sl], k_ref[0, :, sl], (((1,), (1,)), ((), ())), preferred_element_type=F32)
        e = jnp.exp(s - jnp.max(s, axis=-1, keepdims=True))
        p = e / jnp.sum(e, axis=-1, keepdims=True)
        outs.append(jnp.dot(p.astype(BF16), v_ref[0, :, sl], preferred_element_type=F32))
    o = jnp.concatenate(outs, axis=-1).astype(BF16)
    o_ref[0] = x + jnp.dot(o, wo_ref[...], preferred_element_type=F32)


def _cross_attention(x, mem, g_x, g_mem, wq, wk, wv, wo):
    b, s, d = x.shape
    m = mem.shape[1]
    ts = min(512, s)
    dh = d // CROSS_HEADS
    k, v = _mm([mem.reshape(b * m, d)], jnp.concatenate([wk, wv], axis=1).astype(BF16), gain=g_mem,
               tm=min(512, b * m), splits=(d, d), out_dtype=BF16, name="cross_kv")
    full = lambda a: pl.BlockSpec(a.shape, lambda bi, i: (0,) * a.ndim)
    wq_s = (wq * dh ** -0.5).astype(BF16)
    wo_b = wo.astype(BF16)
    g2 = g_x.reshape(1, d).astype(F32)
    return pl.pallas_call(
        _cross_body,
        grid=(b, s // ts),
        in_specs=[pl.BlockSpec((1, ts, d), lambda bi, i: (bi, i, 0)), full(g2), full(wq_s),
                  pl.BlockSpec((1, m, d), lambda bi, i: (bi, 0, 0)),
                  pl.BlockSpec((1, m, d), lambda bi, i: (bi, 0, 0)), full(wo_b)],
        out_specs=pl.BlockSpec((1, ts, d), lambda bi, i: (bi, i, 0)),
        out_shape=jax.ShapeDtypeStruct((b, s, d), F32),
        compiler_params=_cparams("parallel", "parallel"),
        name="cross_attn",
    )(x, g2, wq_s, k.reshape(b, m, d), v.reshape(b, m, d), wo_b)


def _swiglu_body(x_ref, g_ref, wg_ref, wu_ref, wd_ref, o_ref, h_sc, acc_sc):
    f = pl.program_id(1)

    @pl.when(f == 0)
    def _():
        h_sc[...] = _rms(x_ref[...], g_ref[...]).astype(BF16)
        acc_sc[...] = x_ref[...]

    h = h_sc[...]
    gate = jnp.dot(h, wg_ref[...], preferred_element_type=F32)
    up = jnp.dot(h, wu_ref[...], preferred_element_type=F32)
    act = (gate * jax.nn.sigmoid(gate) * up).astype(BF16)
    acc_sc[...] += jnp.dot(act, wd_ref[...], preferred_element_type=F32)

    @pl.when(f == pl.num_programs(1) - 1)
    def _():
        o_ref[...] = acc_sc[...]


def _swiglu(x, gain, wg, wu, wd, tm=512, tf=1408):
    t, d = x.shape
    ff = wg.shape[1]
    tm, tf = min(tm, t), min(tf, ff)
    assert t % tm == 0 and ff % tf == 0
    return pl.pallas_call(
        _swiglu_body,
        grid=(t // tm, ff // tf),
        in_specs=[pl.BlockSpec((tm, d), lambda i, f: (i, 0)),
                  pl.BlockSpec((1, d), lambda i, f: (0, 0)),
                  pl.BlockSpec((d, tf), lambda i, f: (0, f)),
                  pl.BlockSpec((d, tf), lambda i, f: (0, f)),
                  pl.BlockSpec((tf, d), lambda i, f: (f, 0))],
        out_specs=pl.BlockSpec((tm, d), lambda i, f: (i, 0)),
        out_shape=jax.ShapeDtypeStruct((t, d), F32),
        scratch_shapes=[pltpu.VMEM((tm, d), BF16), pltpu.VMEM((tm, d), F32)],
        compiler_params=_cparams("parallel", "arbitrary"),
        name="swiglu",
    )(x, gain.reshape(1, d).astype(F32), wg.astype(BF16), wu.astype(BF16), wd.astype(BF16))


def _router_body(x_ref, g_ref, r_ref, h_ref, rt_ref):
    h = _rms(x_ref[...], g_ref[...])
    h_ref[...] = h
    h_hi = h.astype(BF16)
    h_lo = (h - h_hi.astype(F32)).astype(BF16)
    r = r_ref[...]
    r_hi = r.astype(BF16)
    r_lo = (r - r_hi.astype(F32)).astype(BF16)
    logits = (jnp.dot(h_hi, r_hi, preferred_element_type=F32) + jnp.dot(h_lo, r_hi, preferred_element_type=F32)
              + jnp.dot(h_hi, r_lo, preferred_element_type=F32))
    lane = lax.broadcasted_iota(jnp.int32, logits.shape, 1)
    lg = jnp.where(lane < N_EXPERTS, logits, -jnp.inf)
    v0 = jnp.max(lg, axis=-1, keepdims=True)
    e0 = jnp.min(jnp.where(lg == v0, lane, LANES), axis=-1, keepdims=True)
    lg1 = jnp.where(lane == e0, -jnp.inf, lg)
    v1 = jnp.max(lg1, axis=-1, keepdims=True)
    e1 = jnp.min(jnp.where(lg1 == v1, lane, LANES), axis=-1, keepdims=True)
    g1 = 1.0 / (1.0 + jnp.exp(v0 - v1))
    g0 = 1.0 - g1
    out = jnp.where(lane == 0, g0, jnp.where(lane == 1, g1, 0.0))
    out = jnp.where(lane == 2, e0.astype(F32), jnp.where(lane == 3, e1.astype(F32), out))
    rt_ref[...] = out


def _router(x, gain, router, tm=512):
    t, d = x.shape
    tm = min(tm, t)
    r_pad = jnp.zeros((d, LANES), F32).at[:, :N_EXPERTS].set(router.astype(F32))
    return pl.pallas_call(
        _router_body,
        grid=(t // tm,),
        in_specs=[pl.BlockSpec((tm, d), lambda i: (i, 0)), pl.BlockSpec((1, d), lambda i: (0, 0)),
                  pl.BlockSpec((d, LANES), lambda i: (0, 0))],
        out_specs=[pl.BlockSpec((tm, d), lambda i: (i, 0)), pl.BlockSpec((tm, LANES), lambda i: (i, 0))],
        out_shape=[jax.ShapeDtypeStruct((t, d), F32), jax.ShapeDtypeStruct((t, LANES), F32)],
        compiler_params=_cparams("parallel"),
        name="moe_router",
    )(x, gain.reshape(1, d).astype(F32), r_pad)


def _row_copy(src_hbm, src_row, dst_ref, dst_row, sem):
    return pltpu.make_async_copy(src_hbm.at[pl.ds(src_row, 1)], dst_ref.at[pl.ds(dst_row, 1)], sem)


def _rank_body(rt_ref, u_ref, ps_ref, dest_ref, carry_sc):
    k, i = pl.program_id(0), pl.program_id(1)
    tm = rt_ref.shape[0]

    @pl.when((k == 0) & (i == 0))
    def _():
        carry_sc[...] = jnp.zeros(carry_sc.shape, F32)

    rt_t = rt_ref[...].T
    e_row = jnp.where(k == 0, rt_t[2:3], rt_t[3:4]).astype(jnp.int32)
    sub = lax.broadcasted_iota(jnp.int32, (N_EXPERTS, tm), 0)
    onehot = sub == e_row
    ahead = jnp.dot(onehot.astype(BF16), u_ref[...], preferred_element_type=F32)
    row = ahead + carry_sc[:, :1] + ps_ref[:, :1]
    dest = jnp.sum(jnp.where(onehot, row, 0.0), axis=0, keepdims=True)
    dest_ref[...] = dest.astype(jnp.int32).reshape(dest_ref.shape)
    carry_sc[...] += jnp.sum(onehot.astype(F32), axis=1, keepdims=True)


def _assignment_rows(routing, pstart, tm=512):
    t = routing.shape[0]
    tm = min(tm, t)
    upper = jnp.asarray(np.triu(np.ones((tm, tm), np.float32), 1), BF16)
    ps = jnp.broadcast_to(pstart.astype(F32)[:, None], (N_EXPERTS, LANES))
    out = pl.pallas_call(
        _rank_body,
        grid=(TOP_K, t // tm),
        in_specs=[pl.BlockSpec((tm, LANES), lambda k, i: (i, 0)),
                  pl.BlockSpec((tm, tm), lambda k, i: (0, 0)),
                  pl.BlockSpec((N_EXPERTS, LANES), lambda k, i: (0, 0))],
        out_specs=pl.BlockSpec((1, 1, 1, tm), lambda k, i: (k, i, 0, 0)),
        out_shape=jax.ShapeDtypeStruct((TOP_K, t // tm, 1, tm), jnp.int32),
        scratch_shapes=[pltpu.VMEM((N_EXPERTS, LANES), F32)],
        compiler_params=_cparams("arbitrary", "arbitrary"),
        name="moe_rank",
    )(routing, upper, ps)
    return out.reshape(TOP_K * t)


def _dispatch_body(dest_ref, h_ref, xs_in, xs_hbm, sem):
    del xs_in
    tm = h_ref.shape[0]
    t = dest_ref.shape[0] // TOP_K
    base = pl.program_id(0) * tm

    def copy(r, k, dst_row):
        return pltpu.make_async_copy(h_ref.at[pl.ds(r, 1)], xs_hbm.at[pl.ds(dst_row, 1)], sem)

    def start(r, c):
        for k in range(TOP_K):
            copy(r, k, dest_ref[k * t + base + r]).start()
        return c

    lax.fori_loop(0, tm, start, 0, unroll=8)
    for r in range(tm):
        for k in range(TOP_K):
            copy(r, k, 0).wait()


def _dispatch(h, dest, n_rows, tm=256):
    t, d = h.shape
    tm = min(tm, t)
    grid_spec = pltpu.PrefetchScalarGridSpec(
        num_scalar_prefetch=1,
        grid=(t // tm,),
        in_specs=[pl.BlockSpec((tm, d), lambda i, dest: (i, 0)), pl.BlockSpec(memory_space=pl.ANY)],
        out_specs=pl.BlockSpec(memory_space=pl.ANY),
        scratch_shapes=[pltpu.SemaphoreType.DMA(())],
    )
    return pl.pallas_call(
        _dispatch_body,
        grid_spec=grid_spec,
        out_shape=jax.ShapeDtypeStruct((n_rows, d), F32),
        input_output_aliases={2: 0},
        compiler_params=_cparams("arbitrary"),
        name="moe_dispatch",
    )(dest, h, jnp.zeros((n_rows, d), F32))


def _experts_body(be_ref, nb_ref, x_ref, wg_ref, wu_ref, wd_ref, o_ref, xb_sc, acc_sc):
    i, f = pl.program_id(0), pl.program_id(1)

    @pl.when(i < nb_ref[0])
    def _():
        @pl.when(f == 0)
        def _():
            xb_sc[...] = x_ref[...].astype(BF16)
            acc_sc[...] = jnp.zeros(acc_sc.shape, F32)

        x = xb_sc[...]
        gate = jnp.dot(x, wg_ref[0], preferred_element_type=F32)
        up = jnp.dot(x, wu_ref[0], preferred_element_type=F32)
        act = (gate * jax.nn.sigmoid(gate) * up).astype(BF16)
        acc_sc[...] += jnp.dot(act, wd_ref[0], preferred_element_type=F32)

        @pl.when(f == pl.num_programs(1) - 1)
        def _():
            o_ref[...] = acc_sc[...]

    @pl.when(i >= nb_ref[0])
    def _():
        o_ref[...] = jnp.zeros(o_ref.shape, F32)


def _experts(xs, blk_e, n_used, wg, wu, wd, rows, tf=1792):
    n_rows, d = xs.shape
    ff = wg.shape[2]
    n_blk = n_rows // rows
    last = ff // tf - 1
    grid_spec = pltpu.PrefetchScalarGridSpec(
        num_scalar_prefetch=2,
        grid=(n_blk, ff // tf),
        in_specs=[pl.BlockSpec((rows, d), lambda i, f, be, nb: (jnp.minimum(i, nb[0] - 1), 0)),
                  pl.BlockSpec((1, d, tf), lambda i, f, be, nb: (be[i], 0, jnp.where(i < nb[0], f, last))),
                  pl.BlockSpec((1, d, tf), lambda i, f, be, nb: (be[i], 0, jnp.where(i < nb[0], f, last))),
                  pl.BlockSpec((1, tf, d), lambda i, f, be, nb: (be[i], jnp.where(i < nb[0], f, last), 0))],
        out_specs=pl.BlockSpec((rows, d), lambda i, f, be, nb: (i, 0)),
        scratch_shapes=[pltpu.VMEM((rows, d), BF16), pltpu.VMEM((rows, d), F32)],
    )
    return pl.pallas_call(
        _experts_body,
        grid_spec=grid_spec,
        out_shape=jax.ShapeDtypeStruct((n_rows, d), F32),
        compiler_params=_cparams("arbitrary", "arbitrary"),
        name="moe_experts",
    )(blk_e, n_used, xs, wg, wu, wd)


def _combine_body(dest_ref, x_ref, rt_ref, ys_hbm, o_ref, buf, sem):
    tm = x_ref.shape[0]
    n_tok = dest_ref.shape[0] // TOP_K
    i = pl.program_id(0)
    slot = i & 1

    def start(tile, s):
        def body(r, c):
            for k in range(TOP_K):
                _row_copy(ys_hbm, dest_ref[k * n_tok + tile * tm + r], buf.at[s, k], r, sem.at[s]).start()
            return c
        lax.fori_loop(0, tm, body, 0, unroll=8)

    @pl.when(i == 0)
    def _():
        start(0, 0)

    @pl.when(i + 1 < pl.num_programs(0))
    def _():
        start(i + 1, 1 - slot)

    for r in range(tm):
        for k in range(TOP_K):
            _row_copy(ys_hbm, 0, buf.at[slot, k], r, sem.at[slot]).wait()
    rt = rt_ref[...]
    o_ref[...] = x_ref[...] + rt[:, 0:1] * buf[slot, 0] + rt[:, 1:2] * buf[slot, 1]


def _combine(x, routing, ys, dest, tm=256):
    t, d = x.shape
    tm = min(tm, t)
    grid_spec = pltpu.PrefetchScalarGridSpec(
        num_scalar_prefetch=1,
        grid=(t // tm,),
        in_specs=[pl.BlockSpec((tm, d), lambda i, dest: (i, 0)),
                  pl.BlockSpec((tm, LANES), lambda i, dest: (i, 0)),
                  pl.BlockSpec(memory_space=pl.ANY)],
        out_specs=pl.BlockSpec((tm, d), lambda i, dest: (i, 0)),
        scratch_shapes=[pltpu.VMEM((2, TOP_K, tm, d), F32), pltpu.SemaphoreType.DMA((2,))],
    )
    return pl.pallas_call(
        _combine_body,
        grid_spec=grid_spec,
        out_shape=jax.ShapeDtypeStruct((t, d), F32),
        compiler_params=_cparams("arbitrary"),
        name="moe_combine",
    )(dest, x, routing, ys)


MOE_ROWS = 512


def _moe(x, gain, router, w_gate, w_up, w_down):
    t, d = x.shape
    h, routing = _router(x, gain, router)
    n_assign = t * TOP_K
    experts = routing[:, 2:2 + TOP_K].astype(jnp.int32)
    counts = jnp.sum((experts[:, :, None] == jnp.arange(N_EXPERTS)[None, None, :]).astype(jnp.int32), axis=(0, 1))
    padded = (counts + MOE_ROWS - 1) // MOE_ROWS * MOE_ROWS
    pend = jnp.cumsum(padded)
    pstart = pend - padded
    n_rows = -(-n_assign // MOE_ROWS) * MOE_ROWS + N_EXPERTS * MOE_ROWS
    n_blk = n_rows // MOE_ROWS
    blk_start = jnp.arange(n_blk, dtype=jnp.int32) * MOE_ROWS
    blk_e = jnp.minimum(jnp.sum((pend[None, :] <= blk_start[:, None]).astype(jnp.int32), axis=1), N_EXPERTS - 1)
    n_used = (pend[-1] // MOE_ROWS).astype(jnp.int32).reshape(1)
    dest = _assignment_rows(routing, pstart)
    xs = _dispatch(h, dest, n_rows)
    ys = _experts(xs, blk_e.astype(jnp.int32), n_used, w_gate.astype(BF16), w_up.astype(BF16),
                  w_down.astype(BF16), MOE_ROWS)
    return _combine(x, routing, ys, dest)


def _final_norm_body(x_ref, g_ref, o_ref):
    o_ref[...] = _rms(x_ref[...], g_ref[...])


def _final_norm(x, gain, tm=1024):
    t, d = x.shape
    tm = min(tm, t)
    return pl.pallas_call(
        _final_norm_body,
        grid=(t // tm,),
        in_specs=[pl.BlockSpec((tm, d), lambda i: (i, 0)), pl.BlockSpec((1, d), lambda i: (0, 0))],
        out_specs=pl.BlockSpec((tm, d), lambda i: (i, 0)),
        out_shape=jax.ShapeDtypeStruct((t, d), F32),
        compiler_params=_cparams("parallel"),
        name="final_norm",
    )(x, gain.reshape(1, d).astype(F32))


def _all_rope_tables(s):
    return {64: _rope_tables(s, 64), 32: _rope_tables(s, 32)}


def kernel(x, mem, norm_mix, w_in, w_out, diff_lambda, diff_subln, nsa_cmp_pos, nsa_cmp_w1, nsa_cmp_w2, mla_q_norm, mla_kv_norm, mla_w_uq, mla_w_ukv, norm_cross, norm_mem, cross_wq, cross_wk, cross_wv, cross_wo, norm_ffn, ffn_w_gate, ffn_w_up, ffn_w_down, moe_router, moe_w_gate, moe_w_up, moe_w_down, final_norm):
    b, s, d = x.shape
    depth = w_in.shape[0]
    tabs = _all_rope_tables(s)
    for l in range(depth):
        x = _token_mixers(x, l, tabs, norm_mix[l], w_in[l], w_out[l], diff_lambda[l], diff_subln[l],
                          nsa_cmp_pos[l], nsa_cmp_w1[l], nsa_cmp_w2[l], mla_q_norm[l], mla_kv_norm[l],
                          mla_w_uq[l], mla_w_ukv[l])
        x = _cross_attention(x, mem, norm_cross[l], norm_mem[l], cross_wq[l], cross_wk[l], cross_wv[l],
                             cross_wo[l])
        xt = x.reshape(b * s, d)
        i = l // 2
        if l % 2 == 0:
            xt = _swiglu(xt, norm_ffn[l], ffn_w_gate[i], ffn_w_up[i], ffn_w_down[i])
        else:
            xt = _moe(xt, norm_ffn[l], moe_router[i], moe_w_gate[i], moe_w_up[i], moe_w_down[i])
        x = xt.reshape(b, s, d)
    return _final_norm(x.reshape(b * s, d), final_norm).reshape(b, s, d)
```

```python
import functools
import math

import numpy as np
import jax
import jax.numpy as jnp
from jax import lax
from jax.experimental import pallas as pl
from jax.experimental.pallas import tpu as pltpu

F32 = jnp.float32
BF16 = jnp.bfloat16

HEAD_DIM = 64
GROUP_HEADS = 4
GROUP_WIDTH = HEAD_DIM * GROUP_HEADS
ROPE_THETA = 10000.0
NORM_EPS = 1e-6
DIFF_DK = HEAD_DIM // 2
CMP_LEN = 32
CMP_STRIDE = 16
SEL_BLOCK = 64
SEL_SHIFT = 6
SEL_TOPN = 16
NSA_WINDOW = 512
FORCED_SCORE = 1e6
MLA_NOPE = 64
MLA_ROPE = 32
MLA_DV = 64
DILATED_PATTERNS = ((128, 1), (512, 4), (2048, 16))
CROSS_HEADS = 4
N_EXPERTS = 8
TOP_K = 2

LANES = 128
V7X_VMEM_BYTES = 64 * 1024 * 1024
VMEM_LIMIT = V7X_VMEM_BYTES * 7 // 8

ONES_ROWS = 16
NEG_BIG = -1e30
SEL_BIAS = -1e9
LOG2E = math.log2(math.e)
LN2 = math.log(2.0)


def _cparams(*sem):
    return pltpu.CompilerParams(dimension_semantics=sem, vmem_limit_bytes=VMEM_LIMIT)


def _rms(xf, gain):
    ms = jnp.mean(xf * xf, axis=-1, keepdims=True)
    return xf * lax.rsqrt(ms + NORM_EPS) * gain


def _mm_body(*refs, nx, norm, res, splits):
    x_refs = refs[:nx]
    pos = nx
    g_ref = refs[pos] if norm else None
    pos += int(norm)
    w_ref = refs[pos]
    pos += 1
    r_ref = refs[pos] if res else None
    pos += int(res)
    o_refs = refs[pos:pos + len(splits)]
    xs_ref = refs[pos + len(splits)]

    @pl.when(pl.program_id(1) == 0)
    def _():
        off = 0
        for xr in x_refs:
            xv = xr[...]
            if norm:
                xv = _rms(xv.astype(F32), g_ref[...])
            kw = xv.shape[-1]
            xs_ref[:, off:off + kw] = xv.astype(BF16)
            off += kw

    acc = jnp.dot(xs_ref[...], w_ref[...], preferred_element_type=F32)
    if res:
        acc = acc + r_ref[...]
    off = 0
    for o_ref, width in zip(o_refs, splits):
        o_ref[...] = acc[:, off:off + width].astype(o_ref.dtype)
        off += width


def _mm(xs, w, *, gain=None, res=None, tm=512, tn=None, splits=None, out_dtype=F32, name="mm"):
    t = xs[0].shape[0]
    k, n = w.shape
    assert sum(x.shape[1] for x in xs) == k
    tn = n if tn is None else tn
    splits = (tn,) if splits is None else tuple(splits)
    assert sum(splits) == tn and (len(splits) == 1 or tn == n)
    tm = min(tm, t)
    assert t % tm == 0 and n % tn == 0
    norm = gain is not None
    assert not norm or len(xs) == 1
    in_specs = [pl.BlockSpec((tm, x.shape[1]), lambda i, j: (i, 0)) for x in xs]
    args = list(xs)
    if norm:
        in_specs.append(pl.BlockSpec((1, k), lambda i, j: (0, 0)))
        args.append(gain.reshape(1, k).astype(F32))
    in_specs.append(pl.BlockSpec((k, tn), lambda i, j: (0, j)))
    args.append(w)
    if res is not None:
        in_specs.append(pl.BlockSpec((tm, tn), lambda i, j: (i, j)))
        args.append(res)
    if len(splits) == 1:
        out_shape = [jax.ShapeDtypeStruct((t, n), out_dtype)]
        out_specs = [pl.BlockSpec((tm, tn), lambda i, j: (i, j))]
    else:
        out_shape = [jax.ShapeDtypeStruct((t, s), out_dtype) for s in splits]
        out_specs = [pl.BlockSpec((tm, s), lambda i, j: (i, 0)) for s in splits]
    outs = pl.pallas_call(
        functools.partial(_mm_body, nx=len(xs), norm=norm, res=res is not None, splits=splits),
        grid=(t // tm, n // tn),
        in_specs=in_specs,
        out_specs=out_specs,
        out_shape=out_shape,
        scratch_shapes=[pltpu.VMEM((tm, k), BF16)],
        compiler_params=_cparams("parallel", "arbitrary"),
        name=name,
    )(*args)
    return outs[0] if len(outs) == 1 else outs


def _rope_tables(s, group):
    half = group // 2
    pos = jnp.arange(s, dtype=F32)
    inv_freq = ROPE_THETA ** (-jnp.arange(half, dtype=F32) / half)
    ang = pos[:, None] * inv_freq[None, :]
    cos, sin = jnp.cos(ang), jnp.sin(ang)
    zero = jnp.zeros_like(sin)
    reps = LANES // group
    c = jnp.tile(jnp.concatenate([cos, cos], -1), (1, reps))
    s1 = jnp.tile(jnp.concatenate([zero, sin], -1), (1, reps))
    s2 = jnp.tile(jnp.concatenate([-sin, zero], -1), (1, reps))
    return c, s1, s2


def _rope(x, tabs, half):
    w = x.shape[-1]
    c, s1, s2 = (jnp.tile(t, (1, w // LANES)) for t in tabs)
    return x * c + pltpu.roll(x, half, 1) * s1 + pltpu.roll(x, w - half, 1) * s2


def _prep_a(z, t32, q_ref, k_ref, vt_ref):
    qk = _rope(z[:, :512], t32, DIFF_DK // 2)
    lane = lax.broadcasted_iota(jnp.int32, (z.shape[0], HEAD_DIM), 1)
    scale = DIFF_DK ** -0.5 * LOG2E
    for h in range(GROUP_HEADS):
        q = qk[:, h * 64:(h + 1) * 64] * scale
        q_ref[0, h, 0] = jnp.where(lane < DIFF_DK, q, 0.0).astype(BF16)
        q_ref[0, h, 1] = jnp.where(lane >= DIFF_DK, q, 0.0).astype(BF16)
        k_ref[0, h] = qk[:, 256 + h * 64:256 + (h + 1) * 64].astype(BF16)
    vt_ref[0] = z[:, 512:768].T.reshape(vt_ref.shape[1:]).astype(BF16)


def _prep_b(z, t64, qn_ref, qr_ref, ka_ref, kw_ref, kc_ref, vc_ref, vs_ref, vw_ref, g_ref):
    ts = z.shape[0]
    scale = HEAD_DIM ** -0.5 * LOG2E
    rp = _rope(z[:, :384], t64, HEAD_DIM // 2)
    for h in range(GROUP_HEADS):
        qn_ref[0, h] = (z[:, h * 64:(h + 1) * 64] * scale).astype(BF16)
        qr_ref[0, h] = (rp[:, h * 64:(h + 1) * 64] * scale).astype(BF16)
    kpos = pl.program_id(0) * ts + lax.broadcasted_iota(jnp.int32, (ts, SEL_BLOCK), 0)
    lane = lax.broadcasted_iota(jnp.int32, (ts, SEL_BLOCK), 1)
    onehot = jnp.where((kpos >> SEL_SHIFT) == lane, 1.0, 0.0)
    ka_ref[0] = jnp.concatenate([rp[:, 256:320], onehot], axis=-1).astype(BF16)
    kw_ref[0] = rp[:, 320:384].astype(BF16)
    kc_ref[0] = z[:, 384:448].astype(BF16)
    vc_ref[0] = z[:, 448:512].astype(BF16)
    vt = z[:, 512:640].T.astype(BF16)
    vs_ref[0] = vt[:HEAD_DIM]
    vw_ref[0] = vt[HEAD_DIM:]
    g_ref[0] = jax.nn.sigmoid(z[:, 640:768])


def _prep_c(z, t32, qg_ref, kg_ref, wq_ref, wkv_ref, q_ref, k_ref, vt_ref):
    scale = (MLA_NOPE + MLA_ROPE) ** -0.5 * LOG2E
    cq = _rms(z[:, :384], qg_ref[...]).astype(BF16)
    ckv = _rms(z[:, 384:512], kg_ref[...]).astype(BF16)
    qc = jnp.dot(cq, wq_ref[...], preferred_element_type=F32)
    kvc = jnp.dot(ckv, wkv_ref[...], preferred_element_type=F32)
    half = MLA_ROPE // 2
    q_rope = _rope(qc[:, 256:384], t32, half)
    k_rope = _rope(z[:, 512:640], t32, half)[:, :MLA_ROPE]
    for h in range(GROUP_HEADS):
        q = jnp.concatenate([qc[:, h * 64:(h + 1) * 64], q_rope[:, h * 32:(h + 1) * 32]], axis=-1)
        q_ref[0, h] = (q * scale).astype(BF16)
        k_ref[0, h] = jnp.concatenate([kvc[:, h * 64:(h + 1) * 64], k_rope], axis=-1).astype(BF16)
    vt_ref[0] = kvc[:, 256:512].T.reshape(vt_ref.shape[1:]).astype(BF16)


def _prep_d(z, t64, q_ref, k_ref, vt_ref):
    scale = HEAD_DIM ** -0.5 * LOG2E
    qk = _rope(z[:, :512], t64, HEAD_DIM // 2)
    for h in range(GROUP_HEADS):
        q_ref[0, h] = (qk[:, h * 64:(h + 1) * 64] * scale).astype(BF16)
        k_ref[0, h] = qk[:, 256 + h * 64:256 + (h + 1) * 64].astype(BF16)
    vt_ref[0] = z[:, 512:768].T.reshape(vt_ref.shape[1:]).astype(BF16)


Z_SPLITS = (768, 768, 640, 768)
N_PREP_OUT = (3, 9, 3, 3)


def _project_prep_body(*refs):
    x_ref, g_ref, w_ref = refs[:3]
    t64 = tuple(r[...] for r in refs[3:6])
    t32 = tuple(r[...] for r in refs[6:9])
    qg_ref, kg_ref, wq_ref, wkv_ref = refs[9:13]
    outs = refs[13:]
    h = _rms(x_ref[0], g_ref[...]).astype(BF16)
    z = jnp.dot(h, w_ref[...], preferred_element_type=F32)
    o0, o1, o2, o3 = np.cumsum((0,) + Z_SPLITS[:3])
    n0, n1, n2 = np.cumsum(N_PREP_OUT[:3])
    _prep_a(z[:, o0:o0 + Z_SPLITS[0]], t32, *outs[:n0])
    _prep_b(z[:, o1:o1 + Z_SPLITS[1]], t64, *outs[n0:n1])
    _prep_c(z[:, o2:o2 + Z_SPLITS[2]], t32, qg_ref, kg_ref, wq_ref, wkv_ref, *outs[n1:n2])
    _prep_d(z[:, o3:o3 + Z_SPLITS[3]], t64, *outs[n2:])


def _project_prep(x, gain, w, tabs64, tabs32, q_gain, kv_gain, wq, wkv, ts):
    b, s, d = x.shape
    dk = MLA_NOPE + MLA_ROPE
    full = lambda a: pl.BlockSpec(a.shape, lambda i, bi: (0,) * a.ndim)
    tspec = pl.BlockSpec((ts, LANES), lambda i, bi: (i, 0))
    h4 = lambda dd: pl.BlockSpec((1, 4, ts, dd), lambda i, bi: (bi, 0, i, 0))
    h4t = pl.BlockSpec((1, 4, 64, ts), lambda i, bi: (bi, 0, 0, i))
    sq = lambda dd: pl.BlockSpec((1, ts, dd), lambda i, bi: (bi, i, 0))
    t64 = pl.BlockSpec((1, 64, ts), lambda i, bi: (bi, 0, i))
    sd = lambda *shape, dt=BF16: jax.ShapeDtypeStruct(shape, dt)
    out_specs = [pl.BlockSpec((1, 4, 2, ts, 64), lambda i, bi: (bi, 0, 0, i, 0)), h4(64), h4t,
                 h4(64), h4(64), sq(128), sq(64), sq(64), sq(64), t64, t64, sq(128),
                 h4(dk), h4(dk), h4t,
                 h4(64), h4(64), h4t]
    out_shape = [sd(b, 4, 2, s, 64), sd(b, 4, s, 64), sd(b, 4, 64, s),
                 sd(b, 4, s, 64), sd(b, 4, s, 64), sd(b, s, 128), sd(b, s, 64), sd(b, s, 64), sd(b, s, 64),
                 sd(b, 64, s), sd(b, 64, s), sd(b, s, 128, dt=F32),
                 sd(b, 4, s, dk), sd(b, 4, s, dk), sd(b, 4, 64, s),
                 sd(b, 4, s, 64), sd(b, 4, s, 64), sd(b, 4, 64, s)]
    g2 = gain.reshape(1, d).astype(F32)
    outs = pl.pallas_call(
        _project_prep_body,
        grid=(s // ts, b),
        in_specs=[pl.BlockSpec((1, ts, d), lambda i, bi: (bi, i, 0)), full(g2), full(w)] + [tspec] * 6
                 + [full(q_gain), full(kv_gain), full(wq), full(wkv)],
        out_specs=out_specs,
        out_shape=out_shape,
        compiler_params=_cparams("parallel", "parallel"),
        name="w_in_prep",
    )(x, g2, w, *tabs64, *tabs32, q_gain, kv_gain, wq, wkv)
    n0, n1, n2 = np.cumsum(N_PREP_OUT[:3])
    return outs[:n0], outs[n0:n1], outs[n1:n2], outs[n2:]


def _flash_schedule(nq, tq, tk, sk, window):
    qi, kj, fl = [], [], []
    for i in range(nq):
        q_lo, q_hi = i * tq, i * tq + tq - 1
        j_hi = min(q_hi, sk - 1) // tk
        j_lo = 0 if window is None else max(0, q_lo - window) // tk
        for j in range(j_lo, j_hi + 1):
            k_min, k_max = j * tk, j * tk + tk - 1
            full = k_max <= q_lo and (window is None or q_hi - k_min <= window)
            qi.append(i)
            kj.append(j)
            fl.append((1 if j == j_lo else 0) | (2 if j == j_hi else 0) | (0 if full else 4))
    return (np.asarray(qi, np.int32), np.asarray(kj, np.int32), np.asarray(fl, np.int32))


def _flash_body(qi_ref, kj_ref, fl_ref, q_ref, k_ref, vt_ref, o_ref, m_sc, acc_sc, *,
                gb, hq, tq, tk, window, with_lse, merge_groups):
    t = pl.program_id(1)
    fl = fl_ref[t]
    rows = hq * tq
    dv = vt_ref.shape[1]

    @pl.when((fl & 1) != 0)
    def _():
        m_sc[...] = jnp.full(m_sc.shape, NEG_BIG, F32)
        acc_sc[...] = jnp.zeros(acc_sc.shape, F32)

    def step(masked):
        q = q_ref[...].reshape(gb, rows, q_ref.shape[-1])
        st = jnp.einsum("gkd,gqd->gkq", k_ref[...], q, preferred_element_type=F32)
        if masked:
            krow = lax.broadcasted_iota(jnp.int32, (tk, rows), 0)
            qcol = lax.broadcasted_iota(jnp.int32, (tk, rows), 1)
            if hq > 1:
                qcol = qcol & (tq - 1)
            dist = (qi_ref[t] * tq - kj_ref[t] * tk) + qcol - krow
            ok = dist >= 0
            if window is not None:
                ok = ok & (dist <= window)
            st = jnp.where(ok[None], st, NEG_BIG)
        m_prev = m_sc[...]
        m_new = jnp.maximum(m_prev, jnp.max(st, axis=1, keepdims=True))
        alpha = jnp.exp2(m_prev - m_new)
        p = jnp.exp2((st - m_new).astype(BF16))
        v_ext = jnp.concatenate([vt_ref[...], jnp.ones((gb, ONES_ROWS, tk), BF16)], axis=1)
        acc_sc[...] = alpha * acc_sc[...] + jnp.einsum("gdk,gkq->gdq", v_ext, p, preferred_element_type=F32)
        m_sc[...] = m_new

    @pl.when((fl & 4) != 0)
    def _():
        step(True)

    @pl.when((fl & 4) == 0)
    def _():
        step(False)

    @pl.when((fl & 2) != 0)
    def _():
        l = acc_sc[:, dv:dv + 1, :]
        o_t = acc_sc[:, :dv, :] / l
        if merge_groups:
            o_ref[0] = o_t.reshape(gb * dv, rows).T.astype(o_ref.dtype)
            return
        if with_lse:
            extra = jnp.broadcast_to(m_sc[...] * LN2 + jnp.log(l), o_t.shape)
        else:
            extra = jnp.zeros((gb, LANES - dv, rows), F32)
        dvo = o_ref.shape[-1]
        for g in range(gb):
            o = jnp.concatenate([o_t[g], extra[g]], axis=0).T
            o_ref[g] = o[:, :dvo].reshape(o_ref.shape[1:]).astype(o_ref.dtype)


def _flash(q, k, vt, *, tq, tk, gb=1, window=None, with_lse=False, merge_groups=False, out_dtype=F32,
           name="flash"):
    g, hq, sq, dk = q.shape
    _, dv, sk = vt.shape
    tq, tk = min(tq, sq), min(tk, sk)
    assert g % gb == 0 and sq % tq == 0 and sk % tk == 0
    assert tq & (tq - 1) == 0 and 2 * dv == LANES
    qi, kj, fl = _flash_schedule(sq // tq, tq, tk, sk, window)
    dvo = 2 * dv if with_lse else dv
    rows = hq * tq
    if merge_groups:
        assert hq == 1 and not with_lse and (gb * dv) % LANES == 0
        out_spec = pl.BlockSpec((1, tq, gb * dv), lambda gi, t, qi, kj, fl: (gi, qi[t], 0))
        out_shape = jax.ShapeDtypeStruct((g // gb, sq, gb * dv), out_dtype)
    else:
        out_spec = pl.BlockSpec((gb, hq, tq, dvo), lambda gi, t, qi, kj, fl: (gi, 0, qi[t], 0))
        out_shape = jax.ShapeDtypeStruct((g, hq, sq, dvo), out_dtype)
    grid_spec = pltpu.PrefetchScalarGridSpec(
        num_scalar_prefetch=3,
        grid=(g // gb, len(qi)),
        in_specs=[pl.BlockSpec((gb, hq, tq, dk), lambda gi, t, qi, kj, fl: (gi, 0, qi[t], 0)),
                  pl.BlockSpec((gb, tk, dk), lambda gi, t, qi, kj, fl: (gi, kj[t], 0)),
                  pl.BlockSpec((gb, dv, tk), lambda gi, t, qi, kj, fl: (gi, 0, kj[t]))],
        out_specs=out_spec,
        scratch_shapes=[pltpu.VMEM((gb, 1, rows), F32), pltpu.VMEM((gb, dv + ONES_ROWS, rows), F32)],
    )
    return pl.pallas_call(
        functools.partial(_flash_body, gb=gb, hq=hq, tq=tq, tk=tk, window=window, with_lse=with_lse,
                          merge_groups=merge_groups),
        grid_spec=grid_spec,
        out_shape=out_shape,
        compiler_params=_cparams("parallel", "arbitrary"),
        name=name,
    )(jnp.asarray(qi), jnp.asarray(kj), jnp.asarray(fl), q, k, vt)


def _band_body(q_ref, kp_ref, kc_ref, vp_ref, vc_ref, o_ref, *, tq, window):
    i = pl.program_id(1)
    gb, dv = vp_ref.shape[0], vp_ref.shape[1]
    k2 = jnp.concatenate([kp_ref[...], kc_ref[...]], axis=1)
    st = jnp.einsum("gkd,gqd->gkq", k2, q_ref[...], preferred_element_type=F32)
    krow = lax.broadcasted_iota(jnp.int32, (2 * tq, tq), 0)
    qcol = lax.broadcasted_iota(jnp.int32, (2 * tq, tq), 1)
    dist = qcol + tq - krow
    ok = (dist >= 0) & (dist <= window) & ((krow >= tq) | (i > 0))
    st = jnp.where(ok[None], st, NEG_BIG)
    m = jnp.max(st, axis=1, keepdims=True)
    p = jnp.exp2((st - m).astype(BF16))
    v_ext = jnp.concatenate([jnp.concatenate([vp_ref[...], vc_ref[...]], axis=2),
                             jnp.ones((gb, ONES_ROWS, 2 * tq), BF16)], axis=1)
    acc = jnp.einsum("gdk,gkq->gdq", v_ext, p, preferred_element_type=F32)
    l = acc[:, dv:dv + 1, :]
    o_t = acc[:, :dv, :] / l
    lse = jnp.broadcast_to(m * LN2 + jnp.log(l), o_t.shape)
    for g in range(gb):
        o_ref[g] = jnp.concatenate([o_t[g], lse[g]], axis=0).T.astype(o_ref.dtype)


def _band_attention(q, k, vt, *, tq, gb, window, name):
    g, s, dk = q.shape
    dv = vt.shape[1]
    tq = min(tq, s)
    assert window <= tq and g % gb == 0 and s % tq == 0 and 2 * dv == LANES
    prev = lambda i: jnp.maximum(i - 1, 0)
    return pl.pallas_call(
        functools.partial(_band_body, tq=tq, window=window),
        grid=(g // gb, s // tq),
        in_specs=[pl.BlockSpec((gb, tq, dk), lambda gi, i: (gi, i, 0)),
                  pl.BlockSpec((gb, tq, dk), lambda gi, i: (gi, prev(i), 0)),
                  pl.BlockSpec((gb, tq, dk), lambda gi, i: (gi, i, 0)),
                  pl.BlockSpec((gb, dv, tq), lambda gi, i: (gi, 0, prev(i))),
                  pl.BlockSpec((gb, dv, tq), lambda gi, i: (gi, 0, i))],
        out_specs=pl.BlockSpec((gb, tq, 2 * dv), lambda gi, i: (gi, i, 0)),
        out_shape=jax.ShapeDtypeStruct((g, s, 2 * dv), F32),
        compiler_params=_cparams("parallel", "parallel"),
        name=name,
    )(q, k, k, vt, vt)


def _gelu_tanh(x):
    return 0.5 * x * (1.0 + jnp.tanh(math.sqrt(2.0 / math.pi) * (x + 0.044715 * (x * x * x))))


def _compress_body(r_ref, pos_ref, w1_ref, w2_ref, o_ref, ot_ref):
    half = CMP_STRIDE * HEAD_DIM
    r = r_ref[0, 0]
    w1 = w1_ref[0]
    top = jnp.dot(r, w1[:half], preferred_element_type=F32)
    bot = jnp.dot(r, w1[half:], preferred_element_type=F32)
    nc = r.shape[0]
    posb = jnp.dot(pos_ref[0], w1, preferred_element_type=F32)[:1]
    pre = top + pltpu.roll(bot, nc - 1, 0) + posb
    hid = _gelu_tanh(pre).astype(BF16)
    o = jnp.dot(hid, w2_ref[0], preferred_element_type=F32)
    o_ref[0, 0] = o.astype(o_ref.dtype)
    ot_ref[0, 0] = jnp.concatenate([o, jnp.zeros_like(o)], axis=1).T[:HEAD_DIM].astype(ot_ref.dtype)


def _nsa_compress(kv_rows, pos_flat, w1, w2):
    _, b, nc, _ = kv_rows.shape
    return pl.pallas_call(
        _compress_body,
        grid=(2, b),
        in_specs=[pl.BlockSpec((1, 1, nc, 1024), lambda w, bi: (w, bi, 0, 0)),
                  pl.BlockSpec((1, 8, 2048), lambda w, bi: (w, 0, 0)),
                  pl.BlockSpec((1, 2048, 256), lambda w, bi: (w, 0, 0)),
                  pl.BlockSpec((1, 256, 64), lambda w, bi: (w, 0, 0))],
        out_specs=[pl.BlockSpec((1, 1, nc, 64), lambda w, bi: (w, bi, 0, 0)),
                   pl.BlockSpec((1, 1, 64, nc), lambda w, bi: (w, bi, 0, 0))],
        out_shape=[jax.ShapeDtypeStruct((2, b, nc, 64), BF16), jax.ShapeDtypeStruct((2, b, 64, nc), BF16)],
        compiler_params=_cparams("parallel", "parallel"),
        name="nsa_compress",
    )(kv_rows, pos_flat, w1, w2)


def _cmp_select_body(qn_ref, qr_ref, kc_ref, vct_ref, covt_ref, o_ref, qa_ref, *, tq, n_sel, k_top):
    ncp = kc_ref.shape[2]
    qpos = pl.program_id(1) * tq + lax.broadcasted_iota(jnp.int32, (ncp, tq), 1)
    blk_end = lax.broadcasted_iota(jnp.int32, (ncp, tq), 0) * CMP_STRIDE + (CMP_LEN - 1)
    mask = blk_end <= qpos
    kc, vct = kc_ref[0, 0], vct_ref[0, 0]
    p_sum = jnp.zeros((ncp, tq), F32)
    outs = []
    for h in range(GROUP_HEADS):
        st = lax.dot_general(kc, qn_ref[0, h], (((1,), (1,)), ((), ())), preferred_element_type=F32)
        st = jnp.where(mask, st, NEG_BIG)
        mx = jnp.max(st, axis=0, keepdims=True)
        mx = jnp.where(mx > 0.5 * NEG_BIG, mx, 0.0)
        e = jnp.where(mask, jnp.exp2(st - mx), 0.0)
        p = e / jnp.maximum(jnp.sum(e, axis=0, keepdims=True), 1e-30)
        outs.append(jnp.dot(vct, p.astype(BF16), preferred_element_type=F32))
        p_sum = p_sum + p
    o_ref[0] = jnp.concatenate(outs, axis=0).T.astype(o_ref.dtype)
    p_hi = p_sum.astype(BF16)
    p_lo = (p_sum - p_hi.astype(F32)).astype(BF16)
    covt = covt_ref[...]
    imp_t = (jnp.dot(covt, p_hi, preferred_element_type=F32) + jnp.dot(covt, p_lo, preferred_element_type=F32))
    nblk = SEL_BLOCK
    tpos = pl.program_id(1) * tq + lax.broadcasted_iota(jnp.int32, (nblk, tq), 1)
    cur = tpos >> SEL_SHIFT
    j = lax.broadcasted_iota(jnp.int32, (nblk, tq), 0)
    valid = j <= cur
    forced = (j == 0) | (j == cur) | (j == cur - 1)
    score = jnp.where(valid, jnp.where(forced, FORCED_SCORE, imp_t), -jnp.inf)
    sub = 8
    groups = [score[g * sub:(g + 1) * sub] for g in range(nblk // sub)]
    ranks = [jnp.zeros((sub, tq), jnp.int32) for _ in groups]
    jj = lax.broadcasted_iota(jnp.int32, (sub, tq), 0)
    for i in range(n_sel):
        row = jnp.broadcast_to(score[i:i + 1], (sub, tq))
        for g, sg in enumerate(groups):
            if i < g * sub:
                ahead = row >= sg
            elif i >= (g + 1) * sub:
                ahead = row > sg
            else:
                ahead = (row > sg) | ((row == sg) & (i - g * sub < jj))
            ranks[g] = ranks[g] + ahead.astype(jnp.int32)
    rank = jnp.concatenate(ranks, axis=0)
    bias_t = jnp.where(valid & (rank < k_top), 0.0, SEL_BIAS)
    bias = jnp.concatenate([bias_t, jnp.zeros((LANES - nblk, tq), F32)], axis=0).T[:, :nblk].astype(BF16)
    for h in range(GROUP_HEADS):
        qa_ref[0, h] = jnp.concatenate([qr_ref[0, h], bias], axis=-1)


def _cover_matrix(ncp, n_sel, lanes):
    c_start = np.arange(ncp)[:, None] * CMP_STRIDE
    s_start = np.arange(lanes)[None, :] * SEL_BLOCK
    cov = np.clip(np.minimum(c_start + CMP_LEN, s_start + SEL_BLOCK) - np.maximum(c_start, s_start), 0, None)
    cov = cov.astype(np.float32) / CMP_LEN
    cov[:, n_sel:] = 0.0
    cov[ncp - 1:, :] = 0.0
    return cov


def _cmp_select(qn, qr, kvc, kvct, tq):
    b, _, s, _ = qn.shape
    ncp = kvc.shape[2]
    n_sel = s // SEL_BLOCK
    assert n_sel <= SEL_BLOCK
    k_top = min(SEL_TOPN, n_sel)
    covt = jnp.asarray(_cover_matrix(ncp, n_sel, SEL_BLOCK).T, BF16)
    h4 = lambda d: pl.BlockSpec((1, 4, tq, d), lambda bi, i: (bi, 0, i, 0))
    return pl.pallas_call(
        functools.partial(_cmp_select_body, tq=tq, n_sel=n_sel, k_top=k_top),
        grid=(b, s // tq),
        in_specs=[h4(64), h4(64),
                  pl.BlockSpec((1, 1, ncp, 64), lambda bi, i: (0, bi, 0, 0)),
                  pl.BlockSpec((1, 1, 64, ncp), lambda bi, i: (1, bi, 0, 0)),
                  pl.BlockSpec(covt.shape, lambda bi, i: (0, 0))],
        out_specs=[pl.BlockSpec((1, tq, 256), lambda bi, i: (bi, i, 0)), h4(128)],
        out_shape=[jax.ShapeDtypeStruct((b, s, 256), BF16), jax.ShapeDtypeStruct((b, 4, s, 128), BF16)],
        compiler_params=_cparams("parallel", "parallel"),
        name="nsa_cmp_select",
    )(qn, qr, kvc, kvct, covt)


def _merge_a(o_ref, lam_ref, g_ref, out_scale):
    outs = []
    for h in range(GROUP_HEADS):
        o = o_ref[0, h, 0].astype(F32) - lam_ref[...] * o_ref[0, h, 1].astype(F32)
        outs.append(_rms(o, g_ref[...]) * out_scale)
    return outs


def _merge_b(oc_ref, os_ref, ow_ref, g_ref):
    g = g_ref[0]
    outs = []
    for h in range(GROUP_HEADS):
        outs.append(g[:, 3 * h:3 * h + 1] * oc_ref[0, :, h * 64:(h + 1) * 64].astype(F32)
                    + g[:, 3 * h + 1:3 * h + 2] * os_ref[0, h].astype(F32)
                    + g[:, 3 * h + 2:3 * h + 3] * ow_ref[0, h].astype(F32))
    return outs


def _merge_d(o1_ref, o2_ref, o3_ref):
    outs = []
    for h in range(GROUP_HEADS):
        e = [r[0, h] for r in (o1_ref, o2_ref, o3_ref)]
        lse = [x[:, 64:] for x in e]
        mx = jnp.maximum(jnp.maximum(lse[0], lse[1]), lse[2])
        w = [jnp.exp(x - mx) for x in lse]
        den = w[0] + w[1] + w[2]
        outs.append((w[0] * e[0][:, :64] + w[1] * e[1][:, :64] + w[2] * e[2][:, :64]) / den)
    return outs


def _out_proj_body(x_ref, oa_ref, lam_ref, sub_ref, oc_ref, os_ref, ow_ref, g_ref, yc_ref, d1_ref, d2_ref, d3_ref,
                   w_ref, o_ref, *, out_scale):
    parts = (_merge_a(oa_ref, lam_ref, sub_ref, out_scale) + _merge_b(oc_ref, os_ref, ow_ref, g_ref)
             + [yc_ref[0].astype(F32)] + _merge_d(d1_ref, d2_ref, d3_ref))
    y = jnp.concatenate(parts, axis=-1).astype(BF16)
    o_ref[0] = x_ref[0] + jnp.dot(y, w_ref[...], preferred_element_type=F32)


def _out_proj(x, oa, lam, subln, out_scale, o_cmp, o_sel, o_win, gates, y_c, o_d, w, ts):
    b, s, d = x.shape
    row = lambda dd: pl.BlockSpec((1, ts, dd), lambda bi, i: (bi, i, 0))
    h4 = lambda dd: pl.BlockSpec((1, 4, ts, dd), lambda bi, i: (bi, 0, i, 0))
    vec = pl.BlockSpec((1, 64), lambda bi, i: (0, 0))
    return pl.pallas_call(
        functools.partial(_out_proj_body, out_scale=out_scale),
        grid=(b, s // ts),
        in_specs=[row(d), pl.BlockSpec((1, 4, 2, ts, 64), lambda bi, i: (bi, 0, 0, i, 0)), vec, vec,
                  row(256), h4(64), h4(64), row(128), row(256), h4(128), h4(128), h4(128),
                  pl.BlockSpec(w.shape, lambda bi, i: (0, 0))],
        out_specs=row(d),
        out_shape=jax.ShapeDtypeStruct((b, s, d), F32),
        compiler_params=_cparams("parallel", "parallel"),
        name="w_out_merge",
    )(x, oa, lam, subln, o_cmp, o_sel, o_win, gates, y_c, *o_d, w)


def _permute_w_in(w_in):
    a_cols = 768
    b0 = a_cols
    nq = 256
    seg = lambda i: w_in[:, b0 + nq + i * 64: b0 + nq + (i + 1) * 64]
    kc, vc, ks, vs, kw, vw = (seg(i) for i in range(6))
    gb = w_in[:, b0 + nq + 384: b0 + nq + 384 + 12]
    d = w_in.shape[0]
    w_b = jnp.concatenate([w_in[:, b0:b0 + nq], ks, kw, kc, vc, vs, vw, gb, jnp.zeros((d, 116), w_in.dtype)], axis=1)
    c0 = b0 + 652
    w_c = jnp.concatenate([w_in[:, c0:c0 + 544], jnp.zeros((d, 96), w_in.dtype)], axis=1)
    d0 = c0 + 544
    return jnp.concatenate([w_in[:, :a_cols], w_b, w_c, w_in[:, d0:d0 + 768]], axis=1).astype(BF16)


def _residue_classes(t, dil):
    g, s, d = t.shape
    return t.reshape(g, s // dil, dil, d).transpose(0, 2, 1, 3).reshape(g * dil, s // dil, d)


def _residue_classes_t(t, dil):
    g, d, s = t.shape
    return t.reshape(g, d, s // dil, dil).transpose(0, 3, 1, 2).reshape(g * dil, d, s // dil)


def _from_residue_classes(t, dil):
    gd, l, d = t.shape
    return t.reshape(gd // dil, dil, l, d).transpose(0, 2, 1, 3).reshape(gd // dil, l * dil, d)


def _token_mixers(x, layer, rope_tabs, norm_g, w_in, w_out, diff_lambda, diff_subln, cmp_pos, cmp_w1, cmp_w2,
                  q_norm, kv_norm, w_uq, w_ukv):
    b, s, d = x.shape
    t = b * s
    ts = min(512, s)
    wq = w_uq.reshape(-1, 4, MLA_NOPE + MLA_ROPE)
    wq = jnp.concatenate([wq[:, :, :MLA_NOPE].reshape(-1, 256), wq[:, :, MLA_NOPE:].reshape(-1, 128)], axis=1)
    wkv = w_ukv.reshape(-1, 4, MLA_NOPE + MLA_DV)
    wkv = jnp.concatenate([wkv[:, :, :MLA_NOPE].reshape(-1, 256), wkv[:, :, MLA_NOPE:].reshape(-1, 256)], axis=1)
    prep_a, prep_b, prep_c, prep_d = _project_prep(
        x, norm_g, _permute_w_in(w_in), rope_tabs[64], rope_tabs[32], q_norm.reshape(1, -1).astype(F32),
        kv_norm.reshape(1, -1).astype(F32), wq.astype(BF16), wkv.astype(BF16), ts)

    qa, ka, va = prep_a
    oa = _flash(qa.reshape(b * 4, 2, s, 64), ka.reshape(b * 4, s, 64), va.reshape(b * 4, 64, s),
                tq=512, tk=512, gb=2, out_dtype=BF16, name="flash_diff").reshape(b, 4, 2, s, 64)
    lam_init = 0.8 - 0.6 * math.exp(-0.3 * layer)
    lp = diff_lambda.astype(F32)
    lam = jnp.exp(jnp.sum(lp[0] * lp[1])) - jnp.exp(jnp.sum(lp[2] * lp[3])) + lam_init

    qn, qr, k_aug, kw, kc, vc, vs, vw, gates = prep_b
    kv_rows = jnp.stack([kc, vc]).reshape(2, b, s // CMP_STRIDE, CMP_STRIDE * HEAD_DIM)
    pos_flat = jnp.broadcast_to(cmp_pos.reshape(2, 1, CMP_LEN * HEAD_DIM), (2, 8, CMP_LEN * HEAD_DIM)).astype(BF16)
    kvc, kvct = _nsa_compress(kv_rows, pos_flat, cmp_w1.astype(BF16), cmp_w2.astype(BF16))
    o_cmp, q_aug = _cmp_select(qn, qr, kvc, kvct, min(256, s))
    o_sel = _flash(q_aug, k_aug, vs, tq=256, tk=512, out_dtype=BF16, name="flash_sel")
    o_win = _flash(qr, kw, vw, tq=256, tk=256, window=NSA_WINDOW - 1, out_dtype=BF16, name="flash_win")

    qc, kcat, vcv = prep_c
    dk = MLA_NOPE + MLA_ROPE
    y_c = _flash(qc.reshape(b * 4, 1, s, dk), kcat.reshape(b * 4, s, dk), vcv.reshape(b * 4, 64, s),
                 tq=512, tk=512, gb=4, merge_groups=True, out_dtype=BF16, name="flash_mla")

    qd, kd, vd = prep_d
    qd, kd, vd = qd.reshape(b * 4, s, 64), kd.reshape(b * 4, s, 64), vd.reshape(b * 4, 64, s)
    o_d = []
    for window, dil in DILATED_PATTERNS:
        l = s // dil
        qq, kk, vv = _residue_classes(qd, dil), _residue_classes(kd, dil), _residue_classes_t(vd, dil)
        g = qq.shape[0]
        gb = math.gcd(g, 16)
        o = _band_attention(qq, kk, vv, tq=128, gb=gb, window=window // dil, name=f"band_dil{dil}")
        o_d.append(_from_residue_classes(o, dil).reshape(b, 4, s, 128))
    return _out_proj(x, oa, jnp.full((1, 64), lam, F32), diff_subln.reshape(1, 64).astype(F32), 1.0 - lam_init,
                     o_cmp, o_sel, o_win, gates, y_c, o_d, w_out.astype(BF16), ts)


def _cross_body(x_ref, g_ref, wq_ref, k_ref, v_ref, wo_ref, o_ref):
    x = x_ref[0]
    h = _rms(x, g_ref[...]).astype(BF16)
    q = jnp.dot(h, wq_ref[...], preferred_element_type=F32).astype(BF16)
    dh = q.shape[-1] // CROSS_HEADS
    outs = []
    for hd in range(CROSS_HEADS):
        sl = slice(hd * dh, (hd + 1) * dh)
        s = lax.dot_general(q[:, sl], k_ref[0, :, sl], (((1,), (1,)), ((), ())), preferred_element_type=F32)
        e = jnp.exp(s - jnp.max(s, axis=-1, keepdims=True))
        p = e / jnp.sum(e, axis=-1, keepdims=True)
        outs.append(jnp.dot(p.astype(BF16), v_ref[0, :, sl], preferred_element_type=F32))
    o = jnp.concatenate(outs, axis=-1).astype(BF16)
    o_ref[0] = x + jnp.dot(o, wo_ref[...], preferred_element_type=F32)


def _cross_attention(x, mem, g_x, g_mem, wq, wk, wv, wo):
    b, s, d = x.shape
    m = mem.shape[1]
    ts = min(512, s)
    dh = d // CROSS_HEADS
    k, v = _mm([mem.reshape(b * m, d)], jnp.concatenate([wk, wv], axis=1).astype(BF16), gain=g_mem,
               tm=min(512, b * m), splits=(d, d), out_dtype=BF16, name="cross_kv")
    full = lambda a: pl.BlockSpec(a.shape, lambda bi, i: (0,) * a.ndim)
    wq_s = (wq * dh ** -0.5).astype(BF16)
    wo_b = wo.astype(BF16)
    g2 = g_x.reshape(1, d).astype(F32)
    return pl.pallas_call(
        _cross_body,
        grid=(b, s // ts),
        in_specs=[pl.BlockSpec((1, ts, d), lambda bi, i: (bi, i, 0)), full(g2), full(wq_s),
                  pl.BlockSpec((1, m, d), lambda bi, i: (bi, 0, 0)),
                  pl.BlockSpec((1, m, d), lambda bi, i: (bi, 0, 0)), full(wo_b)],
        out_specs=pl.BlockSpec((1, ts, d), lambda bi, i: (bi, i, 0)),
        out_shape=jax.ShapeDtypeStruct((b, s, d), F32),
        compiler_params=_cparams("parallel", "parallel"),
        name="cross_attn",
    )(x, g2, wq_s, k.reshape(b, m, d), v.reshape(b, m, d), wo_b)


def _swiglu_body(x_ref, g_ref, wg_ref, wu_ref, wd_ref, o_ref, h_sc, acc_sc):
    f = pl.program_id(1)

    @pl.when(f == 0)
    def _():
        h_sc[...] = _rms(x_ref[...], g_ref[...]).astype(BF16)
        acc_sc[...] = x_ref[...]

    h = h_sc[...]
    gate = jnp.dot(h, wg_ref[...], preferred_element_type=F32)
    up = jnp.dot(h, wu_ref[...], preferred_element_type=F32)
    act = (gate * jax.nn.sigmoid(gate) * up).astype(BF16)
    acc_sc[...] += jnp.dot(act, wd_ref[...], preferred_element_type=F32)

    @pl.when(f == pl.num_programs(1) - 1)
    def _():
        o_ref[...] = acc_sc[...]


def _swiglu(x, gain, wg, wu, wd, tm=512, tf=1408):
    t, d = x.shape
    ff = wg.shape[1]
    tm, tf = min(tm, t), min(tf, ff)
    assert t % tm == 0 and ff % tf == 0
    return pl.pallas_call(
        _swiglu_body,
        grid=(t // tm, ff // tf),
        in_specs=[pl.BlockSpec((tm, d), lambda i, f: (i, 0)),
                  pl.BlockSpec((1, d), lambda i, f: (0, 0)),
                  pl.BlockSpec((d, tf), lambda i, f: (0, f)),
                  pl.BlockSpec((d, tf), lambda i, f: (0, f)),
                  pl.BlockSpec((tf, d), lambda i, f: (f, 0))],
        out_specs=pl.BlockSpec((tm, d), lambda i, f: (i, 0)),
        out_shape=jax.ShapeDtypeStruct((t, d), F32),
        scratch_shapes=[pltpu.VMEM((tm, d), BF16), pltpu.VMEM((tm, d), F32)],
        compiler_params=_cparams("parallel", "arbitrary"),
        name="swiglu",
    )(x, gain.reshape(1, d).astype(F32), wg.astype(BF16), wu.astype(BF16), wd.astype(BF16))


def _router_body(x_ref, g_ref, r_ref, h_ref, rt_ref):
    h = _rms(x_ref[...], g_ref[...])
    h_ref[...] = h
    h_hi = h.astype(BF16)
    h_lo = (h - h_hi.astype(F32)).astype(BF16)
    r = r_ref[...]
    r_hi = r.astype(BF16)
    r_lo = (r - r_hi.astype(F32)).astype(BF16)
    logits = (jnp.dot(h_hi, r_hi, preferred_element_type=F32) + jnp.dot(h_lo, r_hi, preferred_element_type=F32)
              + jnp.dot(h_hi, r_lo, preferred_element_type=F32))
    lane = lax.broadcasted_iota(jnp.int32, logits.shape, 1)
    lg = jnp.where(lane < N_EXPERTS, logits, -jnp.inf)
    v0 = jnp.max(lg, axis=-1, keepdims=True)
    e0 = jnp.min(jnp.where(lg == v0, lane, LANES), axis=-1, keepdims=True)
    lg1 = jnp.where(lane == e0, -jnp.inf, lg)
    v1 = jnp.max(lg1, axis=-1, keepdims=True)
    e1 = jnp.min(jnp.where(lg1 == v1, lane, LANES), axis=-1, keepdims=True)
    g1 = 1.0 / (1.0 + jnp.exp(v0 - v1))
    g0 = 1.0 - g1
    out = jnp.where(lane == 0, g0, jnp.where(lane == 1, g1, 0.0))
    out = jnp.where(lane == 2, e0.astype(F32), jnp.where(lane == 3, e1.astype(F32), out))
    rt_ref[...] = out


def _router(x, gain, router, tm=512):
    t, d = x.shape
    tm = min(tm, t)
    r_pad = jnp.zeros((d, LANES), F32).at[:, :N_EXPERTS].set(router.astype(F32))
    return pl.pallas_call(
        _router_body,
        grid=(t // tm,),
        in_specs=[pl.BlockSpec((tm, d), lambda i: (i, 0)), pl.BlockSpec((1, d), lambda i: (0, 0)),
                  pl.BlockSpec((d, LANES), lambda i: (0, 0))],
        out_specs=[pl.BlockSpec((tm, d), lambda i: (i, 0)), pl.BlockSpec((tm, LANES), lambda i: (i, 0))],
        out_shape=[jax.ShapeDtypeStruct((t, d), F32), jax.ShapeDtypeStruct((t, LANES), F32)],
        compiler_params=_cparams("parallel"),
        name="moe_router",
    )(x, gain.reshape(1, d).astype(F32), r_pad)


def _row_copy(src_hbm, src_row, dst_ref, dst_row, sem):
    return pltpu.make_async_copy(src_hbm.at[pl.ds(src_row, 1)], dst_ref.at[pl.ds(dst_row, 1)], sem)


def _rank_body(rt_ref, u_ref, ps_ref, dest_ref, carry_sc):
    k, i = pl.program_id(0), pl.program_id(1)
    tm = rt_ref.shape[0]

    @pl.when((k == 0) & (i == 0))
    def _():
        carry_sc[...] = jnp.zeros(carry_sc.shape, F32)

    rt_t = rt_ref[...].T
    e_row = jnp.where(k == 0, rt_t[2:3], rt_t[3:4]).astype(jnp.int32)
    sub = lax.broadcasted_iota(jnp.int32, (N_EXPERTS, tm), 0)
    onehot = sub == e_row
    ahead = jnp.dot(onehot.astype(BF16), u_ref[...], preferred_element_type=F32)
    row = ahead + carry_sc[:, :1] + ps_ref[:, :1]
    dest = jnp.sum(jnp.where(onehot, row, 0.0), axis=0, keepdims=True)
    dest_ref[...] = dest.astype(jnp.int32).reshape(dest_ref.shape)
    carry_sc[...] += jnp.sum(onehot.astype(F32), axis=1, keepdims=True)


def _assignment_rows(routing, pstart, tm=512):
    t = routing.shape[0]
    tm = min(tm, t)
    upper = jnp.asarray(np.triu(np.ones((tm, tm), np.float32), 1), BF16)
    ps = jnp.broadcast_to(pstart.astype(F32)[:, None], (N_EXPERTS, LANES))
    out = pl.pallas_call(
        _rank_body,
        grid=(TOP_K, t // tm),
        in_specs=[pl.BlockSpec((tm, LANES), lambda k, i: (i, 0)),
                  pl.BlockSpec((tm, tm), lambda k, i: (0, 0)),
                  pl.BlockSpec((N_EXPERTS, LANES), lambda k, i: (0, 0))],
        out_specs=pl.BlockSpec((1, 1, 1, tm), lambda k, i: (k, i, 0, 0)),
        out_shape=jax.ShapeDtypeStruct((TOP_K, t // tm, 1, tm), jnp.int32),
        scratch_shapes=[pltpu.VMEM((N_EXPERTS, LANES), F32)],
        compiler_params=_cparams("arbitrary", "arbitrary"),
        name="moe_rank",
    )(routing, upper, ps)
    return out.reshape(TOP_K * t)


def _dispatch_body(dest_ref, h_ref, xs_in, xs_hbm, sem):
    del xs_in
    tm = h_ref.shape[0]
    t = dest_ref.shape[0] // TOP_K
    base = pl.program_id(0) * tm

    def copy(r, k, dst_row):
        return pltpu.make_async_copy(h_ref.at[pl.ds(r, 1)], xs_hbm.at[pl.ds(dst_row, 1)], sem)

    def start(r, c):
        for k in range(TOP_K):
            copy(r, k, dest_ref[k * t + base + r]).start()
        return c

    lax.fori_loop(0, tm, start, 0, unroll=8)
    for r in range(tm):
        for k in range(TOP_K):
            copy(r, k, 0).wait()


def _dispatch(h, dest, n_rows, tm=256):
    t, d = h.shape
    tm = min(tm, t)
    grid_spec = pltpu.PrefetchScalarGridSpec(
        num_scalar_prefetch=1,
        grid=(t // tm,),
        in_specs=[pl.BlockSpec((tm, d), lambda i, dest: (i, 0)), pl.BlockSpec(memory_space=pl.ANY)],
        out_specs=pl.BlockSpec(memory_space=pl.ANY),
        scratch_shapes=[pltpu.SemaphoreType.DMA(())],
    )
    return pl.pallas_call(
        _dispatch_body,
        grid_spec=grid_spec,
        out_shape=jax.ShapeDtypeStruct((n_rows, d), F32),
        input_output_aliases={2: 0},
        compiler_params=_cparams("arbitrary"),
        name="moe_dispatch",
    )(dest, h, jnp.zeros((n_rows, d), F32))


def _experts_body(be_ref, nb_ref, x_ref, wg_ref, wu_ref, wd_ref, o_ref, xb_sc, acc_sc):
    i, f = pl.program_id(0), pl.program_id(1)

    @pl.when(i < nb_ref[0])
    def _():
        @pl.when(f == 0)
        def _():
            xb_sc[...] = x_ref[...].astype(BF16)
            acc_sc[...] = jnp.zeros(acc_sc.shape, F32)

        x = xb_sc[...]
        gate = jnp.dot(x, wg_ref[0], preferred_element_type=F32)
        up = jnp.dot(x, wu_ref[0], preferred_element_type=F32)
        act = (gate * jax.nn.sigmoid(gate) * up).astype(BF16)
        acc_sc[...] += jnp.dot(act, wd_ref[0], preferred_element_type=F32)

        @pl.when(f == pl.num_programs(1) - 1)
        def _():
            o_ref[...] = acc_sc[...]

    @pl.when(i >= nb_ref[0])
    def _():
        o_ref[...] = jnp.zeros(o_ref.shape, F32)


def _experts(xs, blk_e, n_used, wg, wu, wd, rows, tf=1792):
    n_rows, d = xs.shape
    ff = wg.shape[2]
    n_blk = n_rows // rows
    last = ff // tf - 1
    grid_spec = pltpu.PrefetchScalarGridSpec(
        num_scalar_prefetch=2,
        grid=(n_blk, ff // tf),
        in_specs=[pl.BlockSpec((rows, d), lambda i, f, be, nb: (jnp.minimum(i, nb[0] - 1), 0)),
                  pl.BlockSpec((1, d, tf), lambda i, f, be, nb: (be[i], 0, jnp.where(i < nb[0], f, last))),
                  pl.BlockSpec((1, d, tf), lambda i, f, be, nb: (be[i], 0, jnp.where(i < nb[0], f, last))),
                  pl.BlockSpec((1, tf, d), lambda i, f, be, nb: (be[i], jnp.where(i < nb[0], f, last), 0))],
        out_specs=pl.BlockSpec((rows, d), lambda i, f, be, nb: (i, 0)),
        scratch_shapes=[pltpu.VMEM((rows, d), BF16), pltpu.VMEM((rows, d), F32)],
    )
    return pl.pallas_call(
        _experts_body,
        grid_spec=grid_spec,
        out_shape=jax.ShapeDtypeStruct((n_rows, d), F32),
        compiler_params=_cparams("arbitrary", "arbitrary"),
        name="moe_experts",
    )(blk_e, n_used, xs, wg, wu, wd)


def _combine_body(dest_ref, x_ref, rt_ref, ys_hbm, o_ref, buf, sem):
    tm = x_ref.shape[0]
    n_tok = dest_ref.shape[0] // TOP_K
    i = pl.program_id(0)
    slot = i & 1

    def start(tile, s):
        def body(r, c):
            for k in range(TOP_K):
                _row_copy(ys_hbm, dest_ref[k * n_tok + tile * tm + r], buf.at[s, k], r, sem.at[s]).start()
            return c
        lax.fori_loop(0, tm, body, 0, unroll=8)

    @pl.when(i == 0)
    def _():
        start(0, 0)

    @pl.when(i + 1 < pl.num_programs(0))
    def _():
        start(i + 1, 1 - slot)

    for r in range(tm):
        for k in range(TOP_K):
            _row_copy(ys_hbm, 0, buf.at[slot, k], r, sem.at[slot]).wait()
    rt = rt_ref[...]
    o_ref[...] = x_ref[...] + rt[:, 0:1] * buf[slot, 0] + rt[:, 1:2] * buf[slot, 1]


def _combine(x, routing, ys, dest, tm=256):
    t, d = x.shape
    tm = min(tm, t)
    grid_spec = pltpu.PrefetchScalarGridSpec(
        num_scalar_prefetch=1,
        grid=(t // tm,),
        in_specs=[pl.BlockSpec((tm, d), lambda i, dest: (i, 0)),
                  pl.BlockSpec((tm, LANES), lambda i, dest: (i, 0)),
                  pl.BlockSpec(memory_space=pl.ANY)],
        out_specs=pl.BlockSpec((tm, d), lambda i, dest: (i, 0)),
        scratch_shapes=[pltpu.VMEM((2, TOP_K, tm, d), F32), pltpu.SemaphoreType.DMA((2,))],
    )
    return pl.pallas_call(
        _combine_body,
        grid_spec=grid_spec,
        out_shape=jax.ShapeDtypeStruct((t, d), F32),
        compiler_params=_cparams("arbitrary"),
        name="moe_combine",
    )(dest, x, routing, ys)


MOE_ROWS = 512


def _moe(x, gain, router, w_gate, w_up, w_down):
    t, d = x.shape
    h, routing = _router(x, gain, router)
    n_assign = t * TOP_K
    experts = routing[:, 2:2 + TOP_K].astype(jnp.int32)
    counts = jnp.sum((experts[:, :, None] == jnp.arange(N_EXPERTS)[None, None, :]).astype(jnp.int32), axis=(0, 1))
    padded = (counts + MOE_ROWS - 1) // MOE_ROWS * MOE_ROWS
    pend = jnp.cumsum(padded)
    pstart = pend - padded
    n_rows = -(-n_assign // MOE_ROWS) * MOE_ROWS + N_EXPERTS * MOE_ROWS
    n_blk = n_rows // MOE_ROWS
    blk_start = jnp.arange(n_blk, dtype=jnp.int32) * MOE_ROWS
    blk_e = jnp.minimum(jnp.sum((pend[None, :] <= blk_start[:, None]).astype(jnp.int32), axis=1), N_EXPERTS - 1)
    n_used = (pend[-1] // MOE_ROWS).astype(jnp.int32).reshape(1)
    dest = _assignment_rows(routing, pstart)
    xs = _dispatch(h, dest, n_rows)
    ys = _experts(xs, blk_e.astype(jnp.int32), n_used, w_gate.astype(BF16), w_up.astype(BF16),
                  w_down.astype(BF16), MOE_ROWS)
    return _combine(x, routing, ys, dest)


def _final_norm_body(x_ref, g_ref, o_ref):
    o_ref[...] = _rms(x_ref[...], g_ref[...])


def _final_norm(x, gain, tm=1024):
    t, d = x.shape
    tm = min(tm, t)
    return pl.pallas_call(
        _final_norm_body,
        grid=(t // tm,),
        in_specs=[pl.BlockSpec((tm, d), lambda i: (i, 0)), pl.BlockSpec((1, d), lambda i: (0, 0))],
        out_specs=pl.BlockSpec((tm, d), lambda i: (i, 0)),
        out_shape=jax.ShapeDtypeStruct((t, d), F32),
        compiler_params=_cparams("parallel"),
        name="final_norm",
    )(x, gain.reshape(1, d).astype(F32))


def _all_rope_tables(s):
    return {64: _rope_tables(s, 64), 32: _rope_tables(s, 32)}


def kernel(x, mem, norm_mix, w_in, w_out, diff_lambda, diff_subln, nsa_cmp_pos, nsa_cmp_w1, nsa_cmp_w2, mla_q_norm, mla_kv_norm, mla_w_uq, mla_w_ukv, norm_cross, norm_mem, cross_wq, cross_wk, cross_wv, cross_wo, norm_ffn, ffn_w_gate, ffn_w_up, ffn_w_down, moe_router, moe_w_gate, moe_w_up, moe_w_down, final_norm):
    b, s, d = x.shape
    depth = w_in.shape[0]
    tabs = _all_rope_tables(s)
    for l in range(depth):
        x = _token_mixers(x, l, tabs, norm_mix[l], w_in[l], w_out[l], diff_lambda[l], diff_subln[l],
                          nsa_cmp_pos[l], nsa_cmp_w1[l], nsa_cmp_w2[l], mla_q_norm[l], mla_kv_norm[l],
                          mla_w_uq[l], mla_w_ukv[l])
        x = _cross_attention(x, mem, norm_cross[l], norm_mem[l], cross_wq[l], cross_wk[l], cross_wv[l],
                             cross_wo[l])
        xt = x.reshape(b * s, d)
        i = l // 2
        if l % 2 == 0:
            xt = _swiglu(xt, norm_ffn[l], ffn_w_gate[i], ffn_w_up[i], ffn_w_down[i])
        else:
            xt = _moe(xt, norm_ffn[l], moe_router[i], moe_w_gate[i], moe_w_up[i], moe_w_down[i])
        x = xt.reshape(b, s, d)
    return _final_norm(x.reshape(b * s, d), final_norm).reshape(b, s, d)
```

```python
import functools
import math

import numpy as np
import jax
import jax.numpy as jnp
from jax import lax
from jax.experimental import pallas as pl
from jax.experimental.pallas import tpu as pltpu

F32 = jnp.float32
BF16 = jnp.bfloat16

HEAD_DIM = 64
GROUP_HEADS = 4
GROUP_WIDTH = HEAD_DIM * GROUP_HEADS
ROPE_THETA = 10000.0
NORM_EPS = 1e-6
DIFF_DK = HEAD_DIM // 2
CMP_LEN = 32
CMP_STRIDE = 16
SEL_BLOCK = 64
SEL_SHIFT = 6
SEL_TOPN = 16
NSA_WINDOW = 512
FORCED_SCORE = 1e6
MLA_NOPE = 64
MLA_ROPE = 32
MLA_DV = 64
DILATED_PATTERNS = ((128, 1), (512, 4), (2048, 16))
CROSS_HEADS = 4
N_EXPERTS = 8
TOP_K = 2

LANES = 128
V7X_VMEM_BYTES = 64 * 1024 * 1024
VMEM_LIMIT = V7X_VMEM_BYTES * 7 // 8

ONES_ROWS = 16
NEG_BIG = -1e30
SEL_BIAS = -1e9
LOG2E = math.log2(math.e)
LN2 = math.log(2.0)


def _cparams(*sem):
    return pltpu.CompilerParams(dimension_semantics=sem, vmem_limit_bytes=VMEM_LIMIT)


def _rms(xf, gain):
    ms = jnp.mean(xf * xf, axis=-1, keepdims=True)
    return xf * lax.rsqrt(ms + NORM_EPS) * gain


def _mm_body(*refs, nx, norm, res, splits):
    x_refs = refs[:nx]
    pos = nx
    g_ref = refs[pos] if norm else None
    pos += int(norm)
    w_ref = refs[pos]
    pos += 1
    r_ref = refs[pos] if res else None
    pos += int(res)
    o_refs = refs[pos:pos + len(splits)]
    xs_ref = refs[pos + len(splits)]

    @pl.when(pl.program_id(1) == 0)
    def _():
        off = 0
        for xr in x_refs:
            xv = xr[...]
            if norm:
                xv = _rms(xv.astype(F32), g_ref[...])
            kw = xv.shape[-1]
            xs_ref[:, off:off + kw] = xv.astype(BF16)
            off += kw

    acc = jnp.dot(xs_ref[...], w_ref[...], preferred_element_type=F32)
    if res:
        acc = acc + r_ref[...]
    off = 0
    for o_ref, width in zip(o_refs, splits):
        o_ref[...] = acc[:, off:off + width].astype(o_ref.dtype)
        off += width


def _mm(xs, w, *, gain=None, res=None, tm=512, tn=None, splits=None, out_dtype=F32, name="mm"):
    t = xs[0].shape[0]
    k, n = w.shape
    assert sum(x.shape[1] for x in xs) == k
    tn = n if tn is None else tn
    splits = (tn,) if splits is None else tuple(splits)
    assert sum(splits) == tn and (len(splits) == 1 or tn == n)
    tm = min(tm, t)
    assert t % tm == 0 and n % tn == 0
    norm = gain is not None
    assert not norm or len(xs) == 1
    in_specs = [pl.BlockSpec((tm, x.shape[1]), lambda i, j: (i, 0)) for x in xs]
    args = list(xs)
    if norm:
        in_specs.append(pl.BlockSpec((1, k), lambda i, j: (0, 0)))
        args.append(gain.reshape(1, k).astype(F32))
    in_specs.append(pl.BlockSpec((k, tn), lambda i, j: (0, j)))
    args.append(w)
    if res is not None:
        in_specs.append(pl.BlockSpec((tm, tn), lambda i, j: (i, j)))
        args.append(res)
    if len(splits) == 1:
        out_shape = [jax.ShapeDtypeStruct((t, n), out_dtype)]
        out_specs = [pl.BlockSpec((tm, tn), lambda i, j: (i, j))]
    else:
        out_shape = [jax.ShapeDtypeStruct((t, s), out_dtype) for s in splits]
        out_specs = [pl.BlockSpec((tm, s), lambda i, j: (i, 0)) for s in splits]
    outs = pl.pallas_call(
        functools.partial(_mm_body, nx=len(xs), norm=norm, res=res is not None, splits=splits),
        grid=(t // tm, n // tn),
        in_specs=in_specs,
        out_specs=out_specs,
        out_shape=out_shape,
        scratch_shapes=[pltpu.VMEM((tm, k), BF16)],
        compiler_params=_cparams("parallel", "arbitrary"),
        name=name,
    )(*args)
    return outs[0] if len(outs) == 1 else outs


def _rope_tables(s, group):
    half = group // 2
    pos = jnp.arange(s, dtype=F32)
    inv_freq = ROPE_THETA ** (-jnp.arange(half, dtype=F32) / half)
    ang = pos[:, None] * inv_freq[None, :]
    cos, sin = jnp.cos(ang), jnp.sin(ang)
    zero = jnp.zeros_like(sin)
    reps = LANES // group
    c = jnp.tile(jnp.concatenate([cos, cos], -1), (1, reps))
    s1 = jnp.tile(jnp.concatenate([zero, sin], -1), (1, reps))
    s2 = jnp.tile(jnp.concatenate([-sin, zero], -1), (1, reps))
    return c, s1, s2


def _rope(x, tabs, half):
    w = x.shape[-1]
    c, s1, s2 = (jnp.tile(t, (1, w // LANES)) for t in tabs)
    return x * c + pltpu.roll(x, half, 1) * s1 + pltpu.roll(x, w - half, 1) * s2


def _prep_a(z, t32, q_ref, k_ref, vt_ref):
    qk = _rope(z[:, :512], t32, DIFF_DK // 2)
    lane = lax.broadcasted_iota(jnp.int32, (z.shape[0], HEAD_DIM), 1)
    scale = DIFF_DK ** -0.5 * LOG2E
    for h in range(GROUP_HEADS):
        q = qk[:, h * 64:(h + 1) * 64] * scale
        q_ref[0, h, 0] = jnp.where(lane < DIFF_DK, q, 0.0).astype(BF16)
        q_ref[0, h, 1] = jnp.where(lane >= DIFF_DK, q, 0.0).astype(BF16)
        k_ref[0, h] = qk[:, 256 + h * 64:256 + (h + 1) * 64].astype(BF16)
    vt_ref[0] = z[:, 512:768].T.reshape(vt_ref.shape[1:]).astype(BF16)


def _prep_b(z, t64, qn_ref, qr_ref, ka_ref, kw_ref, kc_ref, vc_ref, vs_ref, vw_ref, g_ref):
    ts = z.shape[0]
    scale = HEAD_DIM ** -0.5 * LOG2E
    rp = _rope(z[:, :384], t64, HEAD_DIM // 2)
    for h in range(GROUP_HEADS):
        qn_ref[0, h] = (z[:, h * 64:(h + 1) * 64] * scale).astype(BF16)
        qr_ref[0, h] = (rp[:, h * 64:(h + 1) * 64] * scale).astype(BF16)
    kpos = pl.program_id(0) * ts + lax.broadcasted_iota(jnp.int32, (ts, SEL_BLOCK), 0)
    lane = lax.broadcasted_iota(jnp.int32, (ts, SEL_BLOCK), 1)
    onehot = jnp.where((kpos >> SEL_SHIFT) == lane, 1.0, 0.0)
    ka_ref[0] = jnp.concatenate([rp[:, 256:320], onehot], axis=-1).astype(BF16)
    kw_ref[0] = rp[:, 320:384].astype(BF16)
    kc_ref[0] = z[:, 384:448].astype(BF16)
    vc_ref[0] = z[:, 448:512].astype(BF16)
    vt = z[:, 512:640].T.astype(BF16)
    vs_ref[0] = vt[:HEAD_DIM]
    vw_ref[0] = vt[HEAD_DIM:]
    g_ref[0] = jax.nn.sigmoid(z[:, 640:768])


def _prep_c(z, t32, qg_ref, kg_ref, wq_ref, wkv_ref, q_ref, k_ref, vt_ref):
    scale = (MLA_NOPE + MLA_ROPE) ** -0.5 * LOG2E
    cq = _rms(z[:, :384], qg_ref[...]).astype(BF16)
    ckv = _rms(z[:, 384:512], kg_ref[...]).astype(BF16)
    qc = jnp.dot(cq, wq_ref[...], preferred_element_type=F32)
    kvc = jnp.dot(ckv, wkv_ref[...], preferred_element_type=F32)
    half = MLA_ROPE // 2
    q_rope = _rope(qc[:, 256:384], t32, half)
    k_rope = _rope(z[:, 512:640], t32, half)[:, :MLA_ROPE]
    for h in range(GROUP_HEADS):
        q = jnp.concatenate([qc[:, h * 64:(h + 1) * 64], q_rope[:, h * 32:(h + 1) * 32]], axis=-1)
        q_ref[0, h] = (q * scale).astype(BF16)
        k_ref[0, h] = jnp.concatenate([kvc[:, h * 64:(h + 1) * 64], k_rope], axis=-1).astype(BF16)
    vt_ref[0] = kvc[:, 256:512].T.reshape(vt_ref.shape[1:]).astype(BF16)


def _prep_d(z, t64, q_ref, k_ref, vt_ref):
    scale = HEAD_DIM ** -0.5 * LOG2E
    qk = _rope(z[:, :512], t64, HEAD_DIM // 2)
    for h in range(GROUP_HEADS):
        q_ref[0, h] = (qk[:, h * 64:(h + 1) * 64] * scale).astype(BF16)
        k_ref[0, h] = qk[:, 256 + h * 64:256 + (h + 1) * 64].astype(BF16)
    vt_ref[0] = z[:, 512:768].T.reshape(vt_ref.shape[1:]).astype(BF16)


Z_SPLITS = (768, 768, 640, 768)
N_PREP_OUT = (3, 9, 3, 3)


def _project_prep_body(*refs):
    x_ref, g_ref, w_ref = refs[:3]
    t64 = tuple(r[...] for r in refs[3:6])
    t32 = tuple(r[...] for r in refs[6:9])
    qg_ref, kg_ref, wq_ref, wkv_ref = refs[9:13]
    outs = refs[13:]
    h = _rms(x_ref[0], g_ref[...]).astype(BF16)
    z = jnp.dot(h, w_ref[...], preferred_element_type=F32)
    o0, o1, o2, o3 = np.cumsum((0,) + Z_SPLITS[:3])
    n0, n1, n2 = np.cumsum(N_PREP_OUT[:3])
    _prep_a(z[:, o0:o0 + Z_SPLITS[0]], t32, *outs[:n0])
    _prep_b(z[:, o1:o1 + Z_SPLITS[1]], t64, *outs[n0:n1])
    _prep_c(z[:, o2:o2 + Z_SPLITS[2]], t32, qg_ref, kg_ref, wq_ref, wkv_ref, *outs[n1:n2])
    _prep_d(z[:, o3:o3 + Z_SPLITS[3]], t64, *outs[n2:])


def _project_prep(x, gain, w, tabs64, tabs32, q_gain, kv_gain, wq, wkv, ts):
    b, s, d = x.shape
    dk = MLA_NOPE + MLA_ROPE
    full = lambda a: pl.BlockSpec(a.shape, lambda i, bi: (0,) * a.ndim)
    tspec = pl.BlockSpec((ts, LANES), lambda i, bi: (i, 0))
    h4 = lambda dd: pl.BlockSpec((1, 4, ts, dd), lambda i, bi: (bi, 0, i, 0))
    h4t = pl.BlockSpec((1, 4, 64, ts), lambda i, bi: (bi, 0, 0, i))
    sq = lambda dd: pl.BlockSpec((1, ts, dd), lambda i, bi: (bi, i, 0))
    t64 = pl.BlockSpec((1, 64, ts), lambda i, bi: (bi, 0, i))
    sd = lambda *shape, dt=BF16: jax.ShapeDtypeStruct(shape, dt)
    out_specs = [pl.BlockSpec((1, 4, 2, ts, 64), lambda i, bi: (bi, 0, 0, i, 0)), h4(64), h4t,
                 h4(64), h4(64), sq(128), sq(64), sq(64), sq(64), t64, t64, sq(128),
                 h4(dk), h4(dk), h4t,
                 h4(64), h4(64), h4t]
    out_shape = [sd(b, 4, 2, s, 64), sd(b, 4, s, 64), sd(b, 4, 64, s),
                 sd(b, 4, s, 64), sd(b, 4, s, 64), sd(b, s, 128), sd(b, s, 64), sd(b, s, 64), sd(b, s, 64),
                 sd(b, 64, s), sd(b, 64, s), sd(b, s, 128, dt=F32),
                 sd(b, 4, s, dk), sd(b, 4, s, dk), sd(b, 4, 64, s),
                 sd(b, 4, s, 64), sd(b, 4, s, 64), sd(b, 4, 64, s)]
    g2 = gain.reshape(1, d).astype(F32)
    outs = pl.pallas_call(
        _project_prep_body,
        grid=(s // ts, b),
        in_specs=[pl.BlockSpec((1, ts, d), lambda i, bi: (bi, i, 0)), full(g2), full(w)] + [tspec] * 6
                 + [full(q_gain), full(kv_gain), full(wq), full(wkv)],
        out_specs=out_specs,
        out_shape=out_shape,
        compiler_params=_cparams("parallel", "parallel"),
        name="w_in_prep",
    )(x, g2, w, *tabs64, *tabs32, q_gain, kv_gain, wq, wkv)
    n0, n1, n2 = np.cumsum(N_PREP_OUT[:3])
    return outs[:n0], outs[n0:n1], outs[n1:n2], outs[n2:]


def _flash_schedule(nq, tq, tk, sk, window):
    qi, kj, fl = [], [], []
    for i in range(nq):
        q_lo, q_hi = i * tq, i * tq + tq - 1
        j_hi = min(q_hi, sk - 1) // tk
        j_lo = 0 if window is None else max(0, q_lo - window) // tk
        for j in range(j_lo, j_hi + 1):
            k_min, k_max = j * tk, j * tk + tk - 1
            full = k_max <= q_lo and (window is None or q_hi - k_min <= window)
            qi.append(i)
            kj.append(j)
            fl.append((1 if j == j_lo else 0) | (2 if j == j_hi else 0) | (0 if full else 4))
    return (np.asarray(qi, np.int32), np.asarray(kj, np.int32), np.asarray(fl, np.int32))


def _flash_body(qi_ref, kj_ref, fl_ref, q_ref, k_ref, vt_ref, o_ref, m_sc, acc_sc, *,
                gb, hq, tq, tk, window, with_lse, merge_groups):
    t = pl.program_id(1)
    fl = fl_ref[t]
    rows = hq * tq
    dv = vt_ref.shape[1]

    @pl.when((fl & 1) != 0)
    def _():
        m_sc[...] = jnp.full(m_sc.shape, NEG_BIG, F32)
        acc_sc[...] = jnp.zeros(acc_sc.shape, F32)

    def step(masked):
        q = q_ref[...].reshape(gb, rows, q_ref.shape[-1])
        st = jnp.einsum("gkd,gqd->gkq", k_ref[...], q, preferred_element_type=F32)
        if masked:
            krow = lax.broadcasted_iota(jnp.int32, (tk, rows), 0)
            qcol = lax.broadcasted_iota(jnp.int32, (tk, rows), 1)
            if hq > 1:
                qcol = qcol & (tq - 1)
            dist = (qi_ref[t] * tq - kj_ref[t] * tk) + qcol - krow
            ok = dist >= 0
            if window is not None:
                ok = ok & (dist <= window)
            st = jnp.where(ok[None], st, NEG_BIG)
        m_prev = m_sc[...]
        m_new = jnp.maximum(m_prev, jnp.max(st, axis=1, keepdims=True))
        alpha = jnp.exp2(m_prev - m_new)
        p = jnp.exp2((st - m_new).astype(BF16))
        v_ext = jnp.concatenate([vt_ref[...], jnp.ones((gb, ONES_ROWS, tk), BF16)], axis=1)
        acc_sc[...] = alpha * acc_sc[...] + jnp.einsum("gdk,gkq->gdq", v_ext, p, preferred_element_type=F32)
        m_sc[...] = m_new

    @pl.when((fl & 4) != 0)
    def _():
        step(True)

    @pl.when((fl & 4) == 0)
    def _():
        step(False)

    @pl.when((fl & 2) != 0)
    def _():
        l = acc_sc[:, dv:dv + 1, :]
        o_t = acc_sc[:, :dv, :] / l
        if merge_groups:
            o_ref[0] = o_t.reshape(gb * dv, rows).T.astype(o_ref.dtype)
            return
        if with_lse:
            extra = jnp.broadcast_to(m_sc[...] * LN2 + jnp.log(l), o_t.shape)
        else:
            extra = jnp.zeros((gb, LANES - dv, rows), F32)
        dvo = o_ref.shape[-1]
        for g in range(gb):
            o = jnp.concatenate([o_t[g], extra[g]], axis=0).T
            o_ref[g] = o[:, :dvo].reshape(o_ref.shape[1:]).astype(o_ref.dtype)


def _flash(q, k, vt, *, tq, tk, gb=1, window=None, with_lse=False, merge_groups=False, out_dtype=F32,
           name="flash"):
    g, hq, sq, dk = q.shape
    _, dv, sk = vt.shape
    tq, tk = min(tq, sq), min(tk, sk)
    assert g % gb == 0 and sq % tq == 0 and sk % tk == 0
    assert tq & (tq - 1) == 0 and 2 * dv == LANES
    qi, kj, fl = _flash_schedule(sq // tq, tq, tk, sk, window)
    dvo = 2 * dv if with_lse else dv
    rows = hq * tq
    if merge_groups:
        assert hq == 1 and not with_lse and (gb * dv) % LANES == 0
        out_spec = pl.BlockSpec((1, tq, gb * dv), lambda gi, t, qi, kj, fl: (gi, qi[t], 0))
        out_shape = jax.ShapeDtypeStruct((g // gb, sq, gb * dv), out_dtype)
    else:
        out_spec = pl.BlockSpec((gb, hq, tq, dvo), lambda gi, t, qi, kj, fl: (gi, 0, qi[t], 0))
        out_shape = jax.ShapeDtypeStruct((g, hq, sq, dvo), out_dtype)
    grid_spec = pltpu.PrefetchScalarGridSpec(
        num_scalar_prefetch=3,
        grid=(g // gb, len(qi)),
        in_specs=[pl.BlockSpec((gb, hq, tq, dk), lambda gi, t, qi, kj, fl: (gi, 0, qi[t], 0)),
                  pl.BlockSpec((gb, tk, dk), lambda gi, t, qi, kj, fl: (gi, kj[t], 0)),
                  pl.BlockSpec((gb, dv, tk), lambda gi, t, qi, kj, fl: (gi, 0, kj[t]))],
        out_specs=out_spec,
        scratch_shapes=[pltpu.VMEM((gb, 1, rows), F32), pltpu.VMEM((gb, dv + ONES_ROWS, rows), F32)],
    )
    return pl.pallas_call(
        functools.partial(_flash_body, gb=gb, hq=hq, tq=tq, tk=tk, window=window, with_lse=with_lse,
                          merge_groups=merge_groups),
        grid_spec=grid_spec,
        out_shape=out_shape,
        compiler_params=_cparams("parallel", "arbitrary"),
        name=name,
    )(jnp.asarray(qi), jnp.asarray(kj), jnp.asarray(fl), q, k, vt)


def _band_body(q_ref, kp_ref, kc_ref, vp_ref, vc_ref, o_ref, *, tq, window):
    i = pl.program_id(1)
    gb, dv = vp_ref.shape[0], vp_ref.shape[1]
    k2 = jnp.concatenate([kp_ref[...], kc_ref[...]], axis=1)
    st = jnp.einsum("gkd,gqd->gkq", k2, q_ref[...], preferred_element_type=F32)
    krow = lax.broadcasted_iota(jnp.int32, (2 * tq, tq), 0)
    qcol = lax.broadcasted_iota(jnp.int32, (2 * tq, tq), 1)
    dist = qcol + tq - krow
    ok = (dist >= 0) & (dist <= window) & ((krow >= tq) | (i > 0))
    st = jnp.where(ok[None], st, NEG_BIG)
    m = jnp.max(st, axis=1, keepdims=True)
    p = jnp.exp2((st - m).astype(BF16))
    v_ext = jnp.concatenate([jnp.concatenate([vp_ref[...], vc_ref[...]], axis=2),
                             jnp.ones((gb, ONES_ROWS, 2 * tq), BF16)], axis=1)
    acc = jnp.einsum("gdk,gkq->gdq", v_ext, p, preferred_element_type=F32)
    l = acc[:, dv:dv + 1, :]
    o_t = acc[:, :dv, :] / l
    lse = jnp.broadcast_to(m * LN2 + jnp.log(l), o_t.shape)
    for g in range(gb):
        o_ref[g] = jnp.concatenate([o_t[g], lse[g]], axis=0).T.astype(o_ref.dtype)


def _band_attention(q, k, vt, *, tq, gb, window, name):
    g, s, dk = q.shape
    dv = vt.shape[1]
    tq = min(tq, s)
    assert window <= tq and g % gb == 0 and s % tq == 0 and 2 * dv == LANES
    prev = lambda i: jnp.maximum(i - 1, 0)
    return pl.pallas_call(
        functools.partial(_band_body, tq=tq, window=window),
        grid=(g // gb, s // tq),
        in_specs=[pl.BlockSpec((gb, tq, dk), lambda gi, i: (gi, i, 0)),
                  pl.BlockSpec((gb, tq, dk), lambda gi, i: (gi, prev(i), 0)),
                  pl.BlockSpec((gb, tq, dk), lambda gi, i: (gi, i, 0)),
                  pl.BlockSpec((gb, dv, tq), lambda gi, i: (gi, 0, prev(i))),
                  pl.BlockSpec((gb, dv, tq), lambda gi, i: (gi, 0, i))],
        out_specs=pl.BlockSpec((gb, tq, 2 * dv), lambda gi, i: (gi, i, 0)),
        out_shape=jax.ShapeDtypeStruct((g, s, 2 * dv), F32),
        compiler_params=_cparams("parallel", "parallel"),
        name=name,
    )(q, k, k, vt, vt)


def _gelu_tanh(x):
    return 0.5 * x * (1.0 + jnp.tanh(math.sqrt(2.0 / math.pi) * (x + 0.044715 * (x * x * x))))


def _compress_body(r_ref, pos_ref, w1_ref, w2_ref, o_ref, ot_ref):
    half = CMP_STRIDE * HEAD_DIM
    r = r_ref[0, 0]
    w1 = w1_ref[0]
    top = jnp.dot(r, w1[:half], preferred_element_type=F32)
    bot = jnp.dot(r, w1[half:], preferred_element_type=F32)
    nc = r.shape[0]
    posb = jnp.dot(pos_ref[0], w1, preferred_element_type=F32)[:1]
    pre = top + pltpu.roll(bot, nc - 1, 0) + posb
    hid = _gelu_tanh(pre).astype(BF16)
    o = jnp.dot(hid, w2_ref[0], preferred_element_type=F32)
    o_ref[0, 0] = o.astype(o_ref.dtype)
    ot_ref[0, 0] = jnp.concatenate([o, jnp.zeros_like(o)], axis=1).T[:HEAD_DIM].astype(ot_ref.dtype)


def _nsa_compress(kv_rows, pos_flat, w1, w2):
    _, b, nc, _ = kv_rows.shape
    return pl.pallas_call(
        _compress_body,
        grid=(2, b),
        in_specs=[pl.BlockSpec((1, 1, nc, 1024), lambda w, bi: (w, bi, 0, 0)),
                  pl.BlockSpec((1, 8, 2048), lambda w, bi: (w, 0, 0)),
                  pl.BlockSpec((1, 2048, 256), lambda w, bi: (w, 0, 0)),
                  pl.BlockSpec((1, 256, 64), lambda w, bi: (w, 0, 0))],
        out_specs=[pl.BlockSpec((1, 1, nc, 64), lambda w, bi: (w, bi, 0, 0)),
                   pl.BlockSpec((1, 1, 64, nc), lambda w, bi: (w, bi, 0, 0))],
        out_shape=[jax.ShapeDtypeStruct((2, b, nc, 64), BF16), jax.ShapeDtypeStruct((2, b, 64, nc), BF16)],
        compiler_params=_cparams("parallel", "parallel"),
        name="nsa_compress",
    )(kv_rows, pos_flat, w1, w2)


def _cmp_select_body(qn_ref, qr_ref, kc_ref, vct_ref, covt_ref, o_ref, qa_ref, *, tq, n_sel, k_top):
    ncp = kc_ref.shape[2]
    qpos = pl.program_id(1) * tq + lax.broadcasted_iota(jnp.int32, (ncp, tq), 1)
    blk_end = lax.broadcasted_iota(jnp.int32, (ncp, tq), 0) * CMP_STRIDE + (CMP_LEN - 1)
    mask = blk_end <= qpos
    kc, vct = kc_ref[0, 0], vct_ref[0, 0]
    p_sum = jnp.zeros((ncp, tq), F32)
    outs = []
    for h in range(GROUP_HEADS):
        st = lax.dot_general(kc, qn_ref[0, h], (((1,), (1,)), ((), ())), preferred_element_type=F32)
        st = jnp.where(mask, st, NEG_BIG)
        mx = jnp.max(st, axis=0, keepdims=True)
        mx = jnp.where(mx > 0.5 * NEG_BIG, mx, 0.0)
        e = jnp.where(mask, jnp.exp2(st - mx), 0.0)
        p = e / jnp.maximum(jnp.sum(e, axis=0, keepdims=True), 1e-30)
        outs.append(jnp.dot(vct, p.astype(BF16), preferred_element_type=F32))
        p_sum = p_sum + p
    o_ref[0] = jnp.concatenate(outs, axis=0).T.astype(o_ref.dtype)
    p_hi = p_sum.astype(BF16)
    p_lo = (p_sum - p_hi.astype(F32)).astype(BF16)
    covt = covt_ref[...]
    imp_t = (jnp.dot(covt, p_hi, preferred_element_type=F32) + jnp.dot(covt, p_lo, preferred_element_type=F32))
    nblk = SEL_BLOCK
    tpos = pl.program_id(1) * tq + lax.broadcasted_iota(jnp.int32, (nblk, tq), 1)
    cur = tpos >> SEL_SHIFT
    j = lax.broadcasted_iota(jnp.int32, (nblk, tq), 0)
    valid = j <= cur
    forced = (j == 0) | (j == cur) | (j == cur - 1)
    score = jnp.where(valid, jnp.where(forced, FORCED_SCORE, imp_t), -jnp.inf)
    sub = 8
    groups = [score[g * sub:(g + 1) * sub] for g in range(nblk // sub)]
    ranks = [jnp.zeros((sub, tq), jnp.int32) for _ in groups]
    jj = lax.broadcasted_iota(jnp.int32, (sub, tq), 0)
    for i in range(n_sel):
        row = jnp.broadcast_to(score[i:i + 1], (sub, tq))
        for g, sg in enumerate(groups):
            if i < g * sub:
                ahead = row >= sg
            elif i >= (g + 1) * sub:
                ahead = row > sg
            else:
                ahead = (row > sg) | ((row == sg) & (i - g * sub < jj))
            ranks[g] = ranks[g] + ahead.astype(jnp.int32)
    rank = jnp.concatenate(ranks, axis=0)
    bias_t = jnp.where(valid & (rank < k_top), 0.0, SEL_BIAS)
    bias = jnp.concatenate([bias_t, jnp.zeros((LANES - nblk, tq), F32)], axis=0).T[:, :nblk].astype(BF16)
    for h in range(GROUP_HEADS):
        qa_ref[0, h] = jnp.concatenate([qr_ref[0, h], bias], axis=-1)


def _cover_matrix(ncp, n_sel, lanes):
    c_start = np.arange(ncp)[:, None] * CMP_STRIDE
    s_start = np.arange(lanes)[None, :] * SEL_BLOCK
    cov = np.clip(np.minimum(c_start + CMP_LEN, s_start + SEL_BLOCK) - np.maximum(c_start, s_start), 0, None)
    cov = cov.astype(np.float32) / CMP_LEN
    cov[:, n_sel:] = 0.0
    cov[ncp - 1:, :] = 0.0
    return cov


def _cmp_select(qn, qr, kvc, kvct, tq):
    b, _, s, _ = qn.shape
    ncp = kvc.shape[2]
    n_sel = s // SEL_BLOCK
    assert n_sel <= SEL_BLOCK
    k_top = min(SEL_TOPN, n_sel)
    covt = jnp.asarray(_cover_matrix(ncp, n_sel, SEL_BLOCK).T, BF16)
    h4 = lambda d: pl.BlockSpec((1, 4, tq, d), lambda bi, i: (bi, 0, i, 0))
    return pl.pallas_call(
        functools.partial(_cmp_select_body, tq=tq, n_sel=n_sel, k_top=k_top),
        grid=(b, s // tq),
        in_specs=[h4(64), h4(64),
                  pl.BlockSpec((1, 1, ncp, 64), lambda bi, i: (0, bi, 0, 0)),
                  pl.BlockSpec((1, 1, 64, ncp), lambda bi, i: (1, bi, 0, 0)),
                  pl.BlockSpec(covt.shape, lambda bi, i: (0, 0))],
        out_specs=[pl.BlockSpec((1, tq, 256), lambda bi, i: (bi, i, 0)), h4(128)],
        out_shape=[jax.ShapeDtypeStruct((b, s, 256), BF16), jax.ShapeDtypeStruct((b, 4, s, 128), BF16)],
        compiler_params=_cparams("parallel", "parallel"),
        name="nsa_cmp_select",
    )(qn, qr, kvc, kvct, covt)


def _merge_a(o_ref, lam_ref, g_ref, out_scale):
    outs = []
    for h in range(GROUP_HEADS):
        o = o_ref[0, h, 0].astype(F32) - lam_ref[...] * o_ref[0, h, 1].astype(F32)
        outs.append(_rms(o, g_ref[...]) * out_scale)
    return outs


def _merge_b(oc_ref, os_ref, ow_ref, g_ref):
    g = g_ref[0]
    outs = []
    for h in range(GROUP_HEADS):
        outs.append(g[:, 3 * h:3 * h + 1] * oc_ref[0, :, h * 64:(h + 1) * 64].astype(F32)
                    + g[:, 3 * h + 1:3 * h + 2] * os_ref[0, h].astype(F32)
                    + g[:, 3 * h + 2:3 * h + 3] * ow_ref[0, h].astype(F32))
    return outs


def _merge_d(o1_ref, o2_ref, o3_ref):
    outs = []
    for h in range(GROUP_HEADS):
        e = [r[0, h] for r in (o1_ref, o2_ref, o3_ref)]
        lse = [x[:, 64:] for x in e]
        mx = jnp.maximum(jnp.maximum(lse[0], lse[1]), lse[2])
        w = [jnp.exp(x - mx) for x in lse]
        den = w[0] + w[1] + w[2]
        outs.append((w[0] * e[0][:, :64] + w[1] * e[1][:, :64] + w[2] * e[2][:, :64]) / den)
    return outs


def _out_proj_body(x_ref, oa_ref, lam_ref, sub_ref, oc_ref, os_ref, ow_ref, g_ref, yc_ref, d1_ref, d2_ref, d3_ref,
                   w_ref, o_ref, *, out_scale):
    parts = (_merge_a(oa_ref, lam_ref, sub_ref, out_scale) + _merge_b(oc_ref, os_ref, ow_ref, g_ref)
             + [yc_ref[0].astype(F32)] + _merge_d(d1_ref, d2_ref, d3_ref))
    y = jnp.concatenate(parts, axis=-1).astype(BF16)
    o_ref[0] = x_ref[0] + jnp.dot(y, w_ref[...], preferred_element_type=F32)


def _out_proj(x, oa, lam, subln, out_scale, o_cmp, o_sel, o_win, gates, y_c, o_d, w, ts):
    b, s, d = x.shape
    row = lambda dd: pl.BlockSpec((1, ts, dd), lambda bi, i: (bi, i, 0))
    h4 = lambda dd: pl.BlockSpec((1, 4, ts, dd), lambda bi, i: (bi, 0, i, 0))
    vec = pl.BlockSpec((1, 64), lambda bi, i: (0, 0))
    return pl.pallas_call(
        functools.partial(_out_proj_body, out_scale=out_scale),
        grid=(b, s // ts),
        in_specs=[row(d), pl.BlockSpec((1, 4, 2, ts, 64), lambda bi, i: (bi, 0, 0, i, 0)), vec, vec,
                  row(256), h4(64), h4(64), row(128), row(256), h4(128), h4(128), h4(128),
                  pl.BlockSpec(w.shape, lambda bi, i: (0, 0))],
        out_specs=row(d),
        out_shape=jax.ShapeDtypeStruct((b, s, d), F32),
        compiler_params=_cparams("parallel", "parallel"),
        name="w_out_merge",
    )(x, oa, lam, subln, o_cmp, o_sel, o_win, gates, y_c, *o_d, w)


def _permute_w_in(w_in):
    a_cols = 768
    b0 = a_cols
    nq = 256
    seg = lambda i: w_in[:, b0 + nq + i * 64: b0 + nq + (i + 1) * 64]
    kc, vc, ks, vs, kw, vw = (seg(i) for i in range(6))
    gb = w_in[:, b0 + nq + 384: b0 + nq + 384 + 12]
    d = w_in.shape[0]
    w_b = jnp.concatenate([w_in[:, b0:b0 + nq], ks, kw, kc, vc, vs, vw, gb, jnp.zeros((d, 116), w_in.dtype)], axis=1)
    c0 = b0 + 652
    w_c = jnp.concatenate([w_in[:, c0:c0 + 544], jnp.zeros((d, 96), w_in.dtype)], axis=1)
    d0 = c0 + 544
    return jnp.concatenate([w_in[:, :a_cols], w_b, w_c, w_in[:, d0:d0 + 768]], axis=1).astype(BF16)


def _residue_classes(t, dil):
    g, s, d = t.shape
    return t.reshape(g, s // dil, dil, d).transpose(0, 2, 1, 3).reshape(g * dil, s // dil, d)


def _residue_classes_t(t, dil):
    g, d, s = t.shape
    return t.reshape(g, d, s // dil, dil).transpose(0, 3, 1, 2).reshape(g * dil, d, s // dil)


def _from_residue_classes(t, dil):
    gd, l, d = t.shape
    return t.reshape(gd // dil, dil, l, d).transpose(0, 2, 1, 3).reshape(gd // dil, l * dil, d)


def _token_mixers(x, layer, rope_tabs, norm_g, w_in, w_out, diff_lambda, diff_subln, cmp_pos, cmp_w1, cmp_w2,
                  q_norm, kv_norm, w_uq, w_ukv):
    b, s, d = x.shape
    t = b * s
    ts = min(512, s)
    wq = w_uq.reshape(-1, 4, MLA_NOPE + MLA_ROPE)
    wq = jnp.concatenate([wq[:, :, :MLA_NOPE].reshape(-1, 256), wq[:, :, MLA_NOPE:].reshape(-1, 128)], axis=1)
    wkv = w_ukv.reshape(-1, 4, MLA_NOPE + MLA_DV)
    wkv = jnp.concatenate([wkv[:, :, :MLA_NOPE].reshape(-1, 256), wkv[:, :, MLA_NOPE:].reshape(-1, 256)], axis=1)
    prep_a, prep_b, prep_c, prep_d = _project_prep(
        x, norm_g, _permute_w_in(w_in), rope_tabs[64], rope_tabs[32], q_norm.reshape(1, -1).astype(F32),
        kv_norm.reshape(1, -1).astype(F32), wq.astype(BF16), wkv.astype(BF16), ts)

    qa, ka, va = prep_a
    oa = _flash(qa.reshape(b * 4, 2, s, 64), ka.reshape(b * 4, s, 64), va.reshape(b * 4, 64, s),
                tq=512, tk=512, gb=8, out_dtype=BF16, name="flash_diff").reshape(b, 4, 2, s, 64)
    lam_init = 0.8 - 0.6 * math.exp(-0.3 * layer)
    lp = diff_lambda.astype(F32)
    lam = jnp.exp(jnp.sum(lp[0] * lp[1])) - jnp.exp(jnp.sum(lp[2] * lp[3])) + lam_init

    qn, qr, k_aug, kw, kc, vc, vs, vw, gates = prep_b
    kv_rows = jnp.stack([kc, vc]).reshape(2, b, s // CMP_STRIDE, CMP_STRIDE * HEAD_DIM)
    pos_flat = jnp.broadcast_to(cmp_pos.reshape(2, 1, CMP_LEN * HEAD_DIM), (2, 8, CMP_LEN * HEAD_DIM)).astype(BF16)
    kvc, kvct = _nsa_compress(kv_rows, pos_flat, cmp_w1.astype(BF16), cmp_w2.astype(BF16))
    o_cmp, q_aug = _cmp_select(qn, qr, kvc, kvct, min(256, s))
    gb_b = math.gcd(b, 4)
    o_sel = _flash(q_aug, k_aug, vs, tq=256, tk=512, gb=gb_b, out_dtype=BF16, name="flash_sel")
    o_win = _flash(qr, kw, vw, tq=256, tk=256, gb=gb_b, window=NSA_WINDOW - 1, out_dtype=BF16, name="flash_win")

    qc, kcat, vcv = prep_c
    dk = MLA_NOPE + MLA_ROPE
    y_c = _flash(qc.reshape(b * 4, 1, s, dk), kcat.reshape(b * 4, s, dk), vcv.reshape(b * 4, 64, s),
                 tq=512, tk=512, gb=4, merge_groups=True, out_dtype=BF16, name="flash_mla")

    qd, kd, vd = prep_d
    qd, kd, vd = qd.reshape(b * 4, s, 64), kd.reshape(b * 4, s, 64), vd.reshape(b * 4, 64, s)
    o_d = []
    for window, dil in DILATED_PATTERNS:
        l = s // dil
        qq, kk, vv = _residue_classes(qd, dil), _residue_classes(kd, dil), _residue_classes_t(vd, dil)
        g = qq.shape[0]
        gb = math.gcd(g, 16)
        o = _band_attention(qq, kk, vv, tq=128, gb=gb, window=window // dil, name=f"band_dil{dil}")
        o_d.append(_from_residue_classes(o, dil).reshape(b, 4, s, 128))
    return _out_proj(x, oa, jnp.full((1, 64), lam, F32), diff_subln.reshape(1, 64).astype(F32), 1.0 - lam_init,
                     o_cmp, o_sel, o_win, gates, y_c, o_d, w_out.astype(BF16), ts)


def _cross_body(x_ref, g_ref, wq_ref, k_ref, v_ref, wo_ref, o_ref):
    x = x_ref[0]
    h = _rms(x, g_ref[...]).astype(BF16)
    q = jnp.dot(h, wq_ref[...], preferred_element_type=F32).astype(BF16)
    dh = q.shape[-1] // CROSS_HEADS
    outs = []
    for hd in range(CROSS_HEADS):
        sl = slice(hd * dh, (hd + 1) * dh)
        s = lax.dot_general(q[:, sl], k_ref[0, :, sl], (((1,), (1,)), ((), ())), preferred_element_type=F32)
        e = jnp.exp(s - jnp.max(s, axis=-1, keepdims=True))
        p = e / jnp.sum(e, axis=-1, keepdims=True)
        outs.append(jnp.dot(p.astype(BF16), v_ref[0, :, sl], preferred_element_type=F32))
    o = jnp.concatenate(outs, axis=-1).astype(BF16)
    o_ref[0] = x + jnp.dot(o, wo_ref[...], preferred_element_type=F32)


def _cross_attention(x, mem, g_x, g_mem, wq, wk, wv, wo):
    b, s, d = x.shape
    m = mem.shape[1]
    ts = min(512, s)
    dh = d // CROSS_HEADS
    k, v = _mm([mem.reshape(b * m, d)], jnp.concatenate([wk, wv], axis=1).astype(BF16), gain=g_mem,
               tm=min(512, b * m), splits=(d, d), out_dtype=BF16, name="cross_kv")
    full = lambda a: pl.BlockSpec(a.shape, lambda bi, i: (0,) * a.ndim)
    wq_s = (wq * dh ** -0.5).astype(BF16)
    wo_b = wo.astype(BF16)
    g2 = g_x.reshape(1, d).astype(F32)
    return pl.pallas_call(
        _cross_body,
        grid=(b, s // ts),
        in_specs=[pl.BlockSpec((1, ts, d), lambda bi, i: (bi, i, 0)), full(g2), full(wq_s),
                  pl.BlockSpec((1, m, d), lambda bi, i: (bi, 0, 0)),
                  pl.BlockSpec((1, m, d), lambda bi, i: (bi, 0, 0)), full(wo_b)],
        out_specs=pl.BlockSpec((1, ts, d), lambda bi, i: (bi, i, 0)),
        out_shape=jax.ShapeDtypeStruct((b, s, d), F32),
        compiler_params=_cparams("parallel", "parallel"),
        name="cross_attn",
    )(x, g2, wq_s, k.reshape(b, m, d), v.reshape(b, m, d), wo_b)


def _swiglu_body(x_ref, g_ref, wg_ref, wu_ref, wd_ref, o_ref, h_sc, acc_sc):
    f = pl.program_id(1)

    @pl.when(f == 0)
    def _():
        h_sc[...] = _rms(x_ref[...], g_ref[...]).astype(BF16)
        acc_sc[...] = x_ref[...]

    h = h_sc[...]
    gate = jnp.dot(h, wg_ref[...], preferred_element_type=F32)
    up = jnp.dot(h, wu_ref[...], preferred_element_type=F32)
    act = (gate * jax.nn.sigmoid(gate) * up).astype(BF16)
    acc_sc[...] += jnp.dot(act, wd_ref[...], preferred_element_type=F32)

    @pl.when(f == pl.num_programs(1) - 1)
    def _():
        o_ref[...] = acc_sc[...]


def _swiglu(x, gain, wg, wu, wd, tm=512, tf=1408):
    t, d = x.shape
    ff = wg.shape[1]
    tm, tf = min(tm, t), min(tf, ff)
    assert t % tm == 0 and ff % tf == 0
    return pl.pallas_call(
        _swiglu_body,
        grid=(t // tm, ff // tf),
        in_specs=[pl.BlockSpec((tm, d), lambda i, f: (i, 0)),
                  pl.BlockSpec((1, d), lambda i, f: (0, 0)),
                  pl.BlockSpec((d, tf), lambda i, f: (0, f)),
                  pl.BlockSpec((d, tf), lambda i, f: (0, f)),
                  pl.BlockSpec((tf, d), lambda i, f: (f, 0))],
        out_specs=pl.BlockSpec((tm, d), lambda i, f: (i, 0)),
        out_shape=jax.ShapeDtypeStruct((t, d), F32),
        scratch_shapes=[pltpu.VMEM((tm, d), BF16), pltpu.VMEM((tm, d), F32)],
        compiler_params=_cparams("parallel", "arbitrary"),
        name="swiglu",
    )(x, gain.reshape(1, d).astype(F32), wg.astype(BF16), wu.astype(BF16), wd.astype(BF16))


def _router_body(x_ref, g_ref, r_ref, h_ref, rt_ref):
    h = _rms(x_ref[...], g_ref[...])
    h_ref[...] = h
    h_hi = h.astype(BF16)
    h_lo = (h - h_hi.astype(F32)).astype(BF16)
    r = r_ref[...]
    r_hi = r.astype(BF16)
    r_lo = (r - r_hi.astype(F32)).astype(BF16)
    logits = (jnp.dot(h_hi, r_hi, preferred_element_type=F32) + jnp.dot(h_lo, r_hi, preferred_element_type=F32)
              + jnp.dot(h_hi, r_lo, preferred_element_type=F32))
    lane = lax.broadcasted_iota(jnp.int32, logits.shape, 1)
    lg = jnp.where(lane < N_EXPERTS, logits, -jnp.inf)
    v0 = jnp.max(lg, axis=-1, keepdims=True)
    e0 = jnp.min(jnp.where(lg == v0, lane, LANES), axis=-1, keepdims=True)
    lg1 = jnp.where(lane == e0, -jnp.inf, lg)
    v1 = jnp.max(lg1, axis=-1, keepdims=True)
    e1 = jnp.min(jnp.where(lg1 == v1, lane, LANES), axis=-1, keepdims=True)
    g1 = 1.0 / (1.0 + jnp.exp(v0 - v1))
    g0 = 1.0 - g1
    out = jnp.where(lane == 0, g0, jnp.where(lane == 1, g1, 0.0))
    out = jnp.where(lane == 2, e0.astype(F32), jnp.where(lane == 3, e1.astype(F32), out))
    rt_ref[...] = out


def _router(x, gain, router, tm=512):
    t, d = x.shape
    tm = min(tm, t)
    r_pad = jnp.zeros((d, LANES), F32).at[:, :N_EXPERTS].set(router.astype(F32))
    return pl.pallas_call(
        _router_body,
        grid=(t // tm,),
        in_specs=[pl.BlockSpec((tm, d), lambda i: (i, 0)), pl.BlockSpec((1, d), lambda i: (0, 0)),
                  pl.BlockSpec((d, LANES), lambda i: (0, 0))],
        out_specs=[pl.BlockSpec((tm, d), lambda i: (i, 0)), pl.BlockSpec((tm, LANES), lambda i: (i, 0))],
        out_shape=[jax.ShapeDtypeStruct((t, d), F32), jax.ShapeDtypeStruct((t, LANES), F32)],
        compiler_params=_cparams("parallel"),
        name="moe_router",
    )(x, gain.reshape(1, d).astype(F32), r_pad)


def _row_copy(src_hbm, src_row, dst_ref, dst_row, sem):
    return pltpu.make_async_copy(src_hbm.at[pl.ds(src_row, 1)], dst_ref.at[pl.ds(dst_row, 1)], sem)


def _rank_body(rt_ref, u_ref, ps_ref, dest_ref, carry_sc):
    k, i = pl.program_id(0), pl.program_id(1)
    tm = rt_ref.shape[0]

    @pl.when((k == 0) & (i == 0))
    def _():
        carry_sc[...] = jnp.zeros(carry_sc.shape, F32)

    rt_t = rt_ref[...].T
    e_row = jnp.where(k == 0, rt_t[2:3], rt_t[3:4]).astype(jnp.int32)
    sub = lax.broadcasted_iota(jnp.int32, (N_EXPERTS, tm), 0)
    onehot = sub == e_row
    ahead = jnp.dot(onehot.astype(BF16), u_ref[...], preferred_element_type=F32)
    row = ahead + carry_sc[:, :1] + ps_ref[:, :1]
    dest = jnp.sum(jnp.where(onehot, row, 0.0), axis=0, keepdims=True)
    dest_ref[...] = dest.astype(jnp.int32).reshape(dest_ref.shape)
    carry_sc[...] += jnp.sum(onehot.astype(F32), axis=1, keepdims=True)


def _assignment_rows(routing, pstart, tm=512):
    t = routing.shape[0]
    tm = min(tm, t)
    upper = jnp.asarray(np.triu(np.ones((tm, tm), np.float32), 1), BF16)
    ps = jnp.broadcast_to(pstart.astype(F32)[:, None], (N_EXPERTS, LANES))
    out = pl.pallas_call(
        _rank_body,
        grid=(TOP_K, t // tm),
        in_specs=[pl.BlockSpec((tm, LANES), lambda k, i: (i, 0)),
                  pl.BlockSpec((tm, tm), lambda k, i: (0, 0)),
                  pl.BlockSpec((N_EXPERTS, LANES), lambda k, i: (0, 0))],
        out_specs=pl.BlockSpec((1, 1, 1, tm), lambda k, i: (k, i, 0, 0)),
        out_shape=jax.ShapeDtypeStruct((TOP_K, t // tm, 1, tm), jnp.int32),
        scratch_shapes=[pltpu.VMEM((N_EXPERTS, LANES), F32)],
        compiler_params=_cparams("arbitrary", "arbitrary"),
        name="moe_rank",
    )(routing, upper, ps)
    return out.reshape(TOP_K * t)


def _dispatch_body(dest_ref, h_ref, xs_in, xs_hbm, sem):
    del xs_in
    tm = h_ref.shape[0]
    t = dest_ref.shape[0] // TOP_K
    base = pl.program_id(0) * tm

    def copy(r, k, dst_row):
        return pltpu.make_async_copy(h_ref.at[pl.ds(r, 1)], xs_hbm.at[pl.ds(dst_row, 1)], sem)

    def start(r, c):
        for k in range(TOP_K):
            copy(r, k, dest_ref[k * t + base + r]).start()
        return c

    lax.fori_loop(0, tm, start, 0, unroll=8)
    for r in range(tm):
        for k in range(TOP_K):
            copy(r, k, 0).wait()


def _dispatch(h, dest, n_rows, tm=256):
    t, d = h.shape
    tm = min(tm, t)
    grid_spec = pltpu.PrefetchScalarGridSpec(
        num_scalar_prefetch=1,
        grid=(t // tm,),
        in_specs=[pl.BlockSpec((tm, d), lambda i, dest: (i, 0)), pl.BlockSpec(memory_space=pl.ANY)],
        out_specs=pl.BlockSpec(memory_space=pl.ANY),
        scratch_shapes=[pltpu.SemaphoreType.DMA(())],
    )
    return pl.pallas_call(
        _dispatch_body,
        grid_spec=grid_spec,
        out_shape=jax.ShapeDtypeStruct((n_rows, d), F32),
        input_output_aliases={2: 0},
        compiler_params=_cparams("arbitrary"),
        name="moe_dispatch",
    )(dest, h, jnp.zeros((n_rows, d), F32))


def _experts_body(be_ref, nb_ref, x_ref, wg_ref, wu_ref, wd_ref, o_ref, xb_sc, acc_sc):
    i, f = pl.program_id(0), pl.program_id(1)

    @pl.when(i < nb_ref[0])
    def _():
        @pl.when(f == 0)
        def _():
            xb_sc[...] = x_ref[...].astype(BF16)
            acc_sc[...] = jnp.zeros(acc_sc.shape, F32)

        x = xb_sc[...]
        gate = jnp.dot(x, wg_ref[0], preferred_element_type=F32)
        up = jnp.dot(x, wu_ref[0], preferred_element_type=F32)
        act = (gate * jax.nn.sigmoid(gate) * up).astype(BF16)
        acc_sc[...] += jnp.dot(act, wd_ref[0], preferred_element_type=F32)

        @pl.when(f == pl.num_programs(1) - 1)
        def _():
            o_ref[...] = acc_sc[...]

    @pl.when(i >= nb_ref[0])
    def _():
        o_ref[...] = jnp.zeros(o_ref.shape, F32)


def _experts(xs, blk_e, n_used, wg, wu, wd, rows, tf=1792):
    n_rows, d = xs.shape
    ff = wg.shape[2]
    n_blk = n_rows // rows
    last = ff // tf - 1
    grid_spec = pltpu.PrefetchScalarGridSpec(
        num_scalar_prefetch=2,
        grid=(n_blk, ff // tf),
        in_specs=[pl.BlockSpec((rows, d), lambda i, f, be, nb: (jnp.minimum(i, nb[0] - 1), 0)),
                  pl.BlockSpec((1, d, tf), lambda i, f, be, nb: (be[i], 0, jnp.where(i < nb[0], f, last))),
                  pl.BlockSpec((1, d, tf), lambda i, f, be, nb: (be[i], 0, jnp.where(i < nb[0], f, last))),
                  pl.BlockSpec((1, tf, d), lambda i, f, be, nb: (be[i], jnp.where(i < nb[0], f, last), 0))],
        out_specs=pl.BlockSpec((rows, d), lambda i, f, be, nb: (i, 0)),
        scratch_shapes=[pltpu.VMEM((rows, d), BF16), pltpu.VMEM((rows, d), F32)],
    )
    return pl.pallas_call(
        _experts_body,
        grid_spec=grid_spec,
        out_shape=jax.ShapeDtypeStruct((n_rows, d), F32),
        compiler_params=_cparams("arbitrary", "arbitrary"),
        name="moe_experts",
    )(blk_e, n_used, xs, wg, wu, wd)


def _combine_body(dest_ref, x_ref, rt_ref, ys_hbm, o_ref, buf, sem):
    tm = x_ref.shape[0]
    n_tok = dest_ref.shape[0] // TOP_K
    i = pl.program_id(0)
    slot = i & 1

    def start(tile, s):
        def body(r, c):
            for k in range(TOP_K):
                _row_copy(ys_hbm, dest_ref[k * n_tok + tile * tm + r], buf.at[s, k], r, sem.at[s]).start()
            return c
        lax.fori_loop(0, tm, body, 0, unroll=8)

    @pl.when(i == 0)
    def _():
        start(0, 0)

    @pl.when(i + 1 < pl.num_programs(0))
    def _():
        start(i + 1, 1 - slot)

    for r in range(tm):
        for k in range(TOP_K):
            _row_copy(ys_hbm, 0, buf.at[slot, k], r, sem.at[slot]).wait()
    rt = rt_ref[...]
    o_ref[...] = x_ref[...] + rt[:, 0:1] * buf[slot, 0] + rt[:, 1:2] * buf[slot, 1]


def _combine(x, routing, ys, dest, tm=256):
    t, d = x.shape
    tm = min(tm, t)
    grid_spec = pltpu.PrefetchScalarGridSpec(
        num_scalar_prefetch=1,
        grid=(t // tm,),
        in_specs=[pl.BlockSpec((tm, d), lambda i, dest: (i, 0)),
                  pl.BlockSpec((tm, LANES), lambda i, dest: (i, 0)),
                  pl.BlockSpec(memory_space=pl.ANY)],
        out_specs=pl.BlockSpec((tm, d), lambda i, dest: (i, 0)),
        scratch_shapes=[pltpu.VMEM((2, TOP_K, tm, d), F32), pltpu.SemaphoreType.DMA((2,))],
    )
    return pl.pallas_call(
        _combine_body,
        grid_spec=grid_spec,
        out_shape=jax.ShapeDtypeStruct((t, d), F32),
        compiler_params=_cparams("arbitrary"),
        name="moe_combine",
    )(dest, x, routing, ys)


MOE_ROWS = 512


def _moe(x, gain, router, w_gate, w_up, w_down):
    t, d = x.shape
    h, routing = _router(x, gain, router)
    n_assign = t * TOP_K
    experts = routing[:, 2:2 + TOP_K].astype(jnp.int32)
    counts = jnp.sum((experts[:, :, None] == jnp.arange(N_EXPERTS)[None, None, :]).astype(jnp.int32), axis=(0, 1))
    padded = (counts + MOE_ROWS - 1) // MOE_ROWS * MOE_ROWS
    pend = jnp.cumsum(padded)
    pstart = pend - padded
    n_rows = -(-n_assign // MOE_ROWS) * MOE_ROWS + N_EXPERTS * MOE_ROWS
    n_blk = n_rows // MOE_ROWS
    blk_start = jnp.arange(n_blk, dtype=jnp.int32) * MOE_ROWS
    blk_e = jnp.minimum(jnp.sum((pend[None, :] <= blk_start[:, None]).astype(jnp.int32), axis=1), N_EXPERTS - 1)
    n_used = (pend[-1] // MOE_ROWS).astype(jnp.int32).reshape(1)
    dest = _assignment_rows(routing, pstart)
    xs = _dispatch(h, dest, n_rows)
    ys = _experts(xs, blk_e.astype(jnp.int32), n_used, w_gate.astype(BF16), w_up.astype(BF16),
                  w_down.astype(BF16), MOE_ROWS)
    return _combine(x, routing, ys, dest)


def _final_norm_body(x_ref, g_ref, o_ref):
    o_ref[...] = _rms(x_ref[...], g_ref[...])


def _final_norm(x, gain, tm=1024):
    t, d = x.shape
    tm = min(tm, t)
    return pl.pallas_call(
        _final_norm_body,
        grid=(t // tm,),
        in_specs=[pl.BlockSpec((tm, d), lambda i: (i, 0)), pl.BlockSpec((1, d), lambda i: (0, 0))],
        out_specs=pl.BlockSpec((tm, d), lambda i: (i, 0)),
        out_shape=jax.ShapeDtypeStruct((t, d), F32),
        compiler_params=_cparams("parallel"),
        name="final_norm",
    )(x, gain.reshape(1, d).astype(F32))


def _all_rope_tables(s):
    return {64: _rope_tables(s, 64), 32: _rope_tables(s, 32)}


def kernel(x, mem, norm_mix, w_in, w_out, diff_lambda, diff_subln, nsa_cmp_pos, nsa_cmp_w1, nsa_cmp_w2, mla_q_norm, mla_kv_norm, mla_w_uq, mla_w_ukv, norm_cross, norm_mem, cross_wq, cross_wk, cross_wv, cross_wo, norm_ffn, ffn_w_gate, ffn_w_up, ffn_w_down, moe_router, moe_w_gate, moe_w_up, moe_w_down, final_norm):
    b, s, d = x.shape
    depth = w_in.shape[0]
    tabs = _all_rope_tables(s)
    for l in range(depth):
        x = _token_mixers(x, l, tabs, norm_mix[l], w_in[l], w_out[l], diff_lambda[l], diff_subln[l],
                          nsa_cmp_pos[l], nsa_cmp_w1[l], nsa_cmp_w2[l], mla_q_norm[l], mla_kv_norm[l],
                          mla_w_uq[l], mla_w_ukv[l])
        x = _cross_attention(x, mem, norm_cross[l], norm_mem[l], cross_wq[l], cross_wk[l], cross_wv[l],
                             cross_wo[l])
        xt = x.reshape(b * s, d)
        i = l // 2
        if l % 2 == 0:
            xt = _swiglu(xt, norm_ffn[l], ffn_w_gate[i], ffn_w_up[i], ffn_w_down[i])
        else:
            xt = _moe(xt, norm_ffn[l], moe_router[i], moe_w_gate[i], moe_w_up[i], moe_w_down[i])
        x = xt.reshape(b, s, d)
    return _final_norm(x.reshape(b * s, d), final_norm).reshape(b, s, d)
```

```python
import functools
import math

import numpy as np
import jax
import jax.numpy as jnp
from jax import lax
from jax.experimental import pallas as pl
from jax.experimental.pallas import tpu as pltpu

F32 = jnp.float32
BF16 = jnp.bfloat16

HEAD_DIM = 64
GROUP_HEADS = 4
GROUP_WIDTH = HEAD_DIM * GROUP_HEADS
ROPE_THETA = 10000.0
NORM_EPS = 1e-6
DIFF_DK = HEAD_DIM // 2
CMP_LEN = 32
CMP_STRIDE = 16
SEL_BLOCK = 64
SEL_SHIFT = 6
SEL_TOPN = 16
NSA_WINDOW = 512
FORCED_SCORE = 1e6
MLA_NOPE = 64
MLA_ROPE = 32
MLA_DV = 64
DILATED_PATTERNS = ((128, 1), (512, 4), (2048, 16))
CROSS_HEADS = 4
N_EXPERTS = 8
TOP_K = 2

LANES = 128
V7X_VMEM_BYTES = 64 * 1024 * 1024
VMEM_LIMIT = V7X_VMEM_BYTES * 7 // 8

ONES_ROWS = 16
NEG_BIG = -1e30
SEL_BIAS = -1e9
LOG2E = math.log2(math.e)
LN2 = math.log(2.0)


def _cparams(*sem):
    return pltpu.CompilerParams(dimension_semantics=sem, vmem_limit_bytes=VMEM_LIMIT)


def _rms(xf, gain):
    ms = jnp.mean(xf * xf, axis=-1, keepdims=True)
    return xf * lax.rsqrt(ms + NORM_EPS) * gain


def _mm_body(*refs, nx, norm, res, splits):
    x_refs = refs[:nx]
    pos = nx
    g_ref = refs[pos] if norm else None
    pos += int(norm)
    w_ref = refs[pos]
    pos += 1
    r_ref = refs[pos] if res else None
    pos += int(res)
    o_refs = refs[pos:pos + len(splits)]
    xs_ref = refs[pos + len(splits)]

    @pl.when(pl.program_id(1) == 0)
    def _():
        off = 0
        for xr in x_refs:
            xv = xr[...]
            if norm:
                xv = _rms(xv.astype(F32), g_ref[...])
            kw = xv.shape[-1]
            xs_ref[:, off:off + kw] = xv.astype(BF16)
            off += kw

    acc = jnp.dot(xs_ref[...], w_ref[...], preferred_element_type=F32)
    if res:
        acc = acc + r_ref[...]
    off = 0
    for o_ref, width in zip(o_refs, splits):
        o_ref[...] = acc[:, off:off + width].astype(o_ref.dtype)
        off += width


def _mm(xs, w, *, gain=None, res=None, tm=512, tn=None, splits=None, out_dtype=F32, name="mm"):
    t = xs[0].shape[0]
    k, n = w.shape
    assert sum(x.shape[1] for x in xs) == k
    tn = n if tn is None else tn
    splits = (tn,) if splits is None else tuple(splits)
    assert sum(splits) == tn and (len(splits) == 1 or tn == n)
    tm = min(tm, t)
    assert t % tm == 0 and n % tn == 0
    norm = gain is not None
    assert not norm or len(xs) == 1
    in_specs = [pl.BlockSpec((tm, x.shape[1]), lambda i, j: (i, 0)) for x in xs]
    args = list(xs)
    if norm:
        in_specs.append(pl.BlockSpec((1, k), lambda i, j: (0, 0)))
        args.append(gain.reshape(1, k).astype(F32))
    in_specs.append(pl.BlockSpec((k, tn), lambda i, j: (0, j)))
    args.append(w)
    if res is not None:
        in_specs.append(pl.BlockSpec((tm, tn), lambda i, j: (i, j)))
        args.append(res)
    if len(splits) == 1:
        out_shape = [jax.ShapeDtypeStruct((t, n), out_dtype)]
        out_specs = [pl.BlockSpec((tm, tn), lambda i, j: (i, j))]
    else:
        out_shape = [jax.ShapeDtypeStruct((t, s), out_dtype) for s in splits]
        out_specs = [pl.BlockSpec((tm, s), lambda i, j: (i, 0)) for s in splits]
    outs = pl.pallas_call(
        functools.partial(_mm_body, nx=len(xs), norm=norm, res=res is not None, splits=splits),
        grid=(t // tm, n // tn),
        in_specs=in_specs,
        out_specs=out_specs,
        out_shape=out_shape,
        scratch_shapes=[pltpu.VMEM((tm, k), BF16)],
        compiler_params=_cparams("parallel", "arbitrary"),
        name=name,
    )(*args)
    return outs[0] if len(outs) == 1 else outs


def _rope_tables(s, group):
    half = group // 2
    pos = jnp.arange(s, dtype=F32)
    inv_freq = ROPE_THETA ** (-jnp.arange(half, dtype=F32) / half)
    ang = pos[:, None] * inv_freq[None, :]
    cos, sin = jnp.cos(ang), jnp.sin(ang)
    zero = jnp.zeros_like(sin)
    reps = LANES // group
    c = jnp.tile(jnp.concatenate([cos, cos], -1), (1, reps))
    s1 = jnp.tile(jnp.concatenate([zero, sin], -1), (1, reps))
    s2 = jnp.tile(jnp.concatenate([-sin, zero], -1), (1, reps))
    return c, s1, s2


def _rope(x, tabs, half):
    w = x.shape[-1]
    c, s1, s2 = (jnp.tile(t, (1, w // LANES)) for t in tabs)
    return x * c + pltpu.roll(x, half, 1) * s1 + pltpu.roll(x, w - half, 1) * s2


def _prep_a(z, t32, q_ref, k_ref, vt_ref):
    qk = _rope(z[:, :512], t32, DIFF_DK // 2)
    lane = lax.broadcasted_iota(jnp.int32, (z.shape[0], HEAD_DIM), 1)
    scale = DIFF_DK ** -0.5 * LOG2E
    for h in range(GROUP_HEADS):
        q = qk[:, h * 64:(h + 1) * 64] * scale
        q_ref[0, h, 0] = jnp.where(lane < DIFF_DK, q, 0.0).astype(BF16)
        q_ref[0, h, 1] = jnp.where(lane >= DIFF_DK, q, 0.0).astype(BF16)
        k_ref[0, h] = qk[:, 256 + h * 64:256 + (h + 1) * 64].astype(BF16)
    vt_ref[0] = z[:, 512:768].T.reshape(vt_ref.shape[1:]).astype(BF16)


def _prep_b(z, t64, qn_ref, qr_ref, ka_ref, kw_ref, kc_ref, vc_ref, vs_ref, vw_ref, g_ref):
    ts = z.shape[0]
    scale = HEAD_DIM ** -0.5 * LOG2E
    rp = _rope(z[:, :384], t64, HEAD_DIM // 2)
    for h in range(GROUP_HEADS):
        qn_ref[0, h] = (z[:, h * 64:(h + 1) * 64] * scale).astype(BF16)
        qr_ref[0, h] = (rp[:, h * 64:(h + 1) * 64] * scale).astype(BF16)
    kpos = pl.program_id(0) * ts + lax.broadcasted_iota(jnp.int32, (ts, SEL_BLOCK), 0)
    lane = lax.broadcasted_iota(jnp.int32, (ts, SEL_BLOCK), 1)
    onehot = jnp.where((kpos >> SEL_SHIFT) == lane, 1.0, 0.0)
    ka_ref[0] = jnp.concatenate([rp[:, 256:320], onehot], axis=-1).astype(BF16)
    kw_ref[0] = rp[:, 320:384].astype(BF16)
    kc_ref[0] = z[:, 384:448].astype(BF16)
    vc_ref[0] = z[:, 448:512].astype(BF16)
    vt = z[:, 512:640].T.astype(BF16)
    vs_ref[0] = vt[:HEAD_DIM]
    vw_ref[0] = vt[HEAD_DIM:]
    g_ref[0] = jax.nn.sigmoid(z[:, 640:768])


def _prep_c(z, t32, qg_ref, kg_ref, wq_ref, wkv_ref, q_ref, k_ref, vt_ref):
    scale = (MLA_NOPE + MLA_ROPE) ** -0.5 * LOG2E
    cq = _rms(z[:, :384], qg_ref[...]).astype(BF16)
    ckv = _rms(z[:, 384:512], kg_ref[...]).astype(BF16)
    qc = jnp.dot(cq, wq_ref[...], preferred_element_type=F32)
    kvc = jnp.dot(ckv, wkv_ref[...], preferred_element_type=F32)
    half = MLA_ROPE // 2
    q_rope = _rope(qc[:, 256:384], t32, half)
    k_rope = _rope(z[:, 512:640], t32, half)[:, :MLA_ROPE]
    for h in range(GROUP_HEADS):
        q = jnp.concatenate([qc[:, h * 64:(h + 1) * 64], q_rope[:, h * 32:(h + 1) * 32]], axis=-1)
        q_ref[0, h] = (q * scale).astype(BF16)
        k_ref[0, h] = jnp.concatenate([kvc[:, h * 64:(h + 1) * 64], k_rope], axis=-1).astype(BF16)
    vt_ref[0] = kvc[:, 256:512].T.reshape(vt_ref.shape[1:]).astype(BF16)


def _prep_d(z, t64, q_ref, k_ref, vt_ref):
    scale = HEAD_DIM ** -0.5 * LOG2E
    qk = _rope(z[:, :512], t64, HEAD_DIM // 2)
    for h in range(GROUP_HEADS):
        q_ref[0, h] = (qk[:, h * 64:(h + 1) * 64] * scale).astype(BF16)
        k_ref[0, h] = qk[:, 256 + h * 64:256 + (h + 1) * 64].astype(BF16)
    vt_ref[0] = z[:, 512:768].T.reshape(vt_ref.shape[1:]).astype(BF16)


Z_SPLITS = (768, 768, 640, 768)
N_PREP_OUT = (3, 9, 3, 3)


def _project_prep_body(*refs):
    x_ref, g_ref, w_ref = refs[:3]
    t64 = tuple(r[...] for r in refs[3:6])
    t32 = tuple(r[...] for r in refs[6:9])
    qg_ref, kg_ref, wq_ref, wkv_ref = refs[9:13]
    outs = refs[13:]
    h = _rms(x_ref[0], g_ref[...]).astype(BF16)
    z = jnp.dot(h, w_ref[...], preferred_element_type=F32)
    o0, o1, o2, o3 = np.cumsum((0,) + Z_SPLITS[:3])
    n0, n1, n2 = np.cumsum(N_PREP_OUT[:3])
    _prep_a(z[:, o0:o0 + Z_SPLITS[0]], t32, *outs[:n0])
    _prep_b(z[:, o1:o1 + Z_SPLITS[1]], t64, *outs[n0:n1])
    _prep_c(z[:, o2:o2 + Z_SPLITS[2]], t32, qg_ref, kg_ref, wq_ref, wkv_ref, *outs[n1:n2])
    _prep_d(z[:, o3:o3 + Z_SPLITS[3]], t64, *outs[n2:])


def _project_prep(x, gain, w, tabs64, tabs32, q_gain, kv_gain, wq, wkv, ts):
    b, s, d = x.shape
    dk = MLA_NOPE + MLA_ROPE
    full = lambda a: pl.BlockSpec(a.shape, lambda i, bi: (0,) * a.ndim)
    tspec = pl.BlockSpec((ts, LANES), lambda i, bi: (i, 0))
    h4 = lambda dd: pl.BlockSpec((1, 4, ts, dd), lambda i, bi: (bi, 0, i, 0))
    h4t = pl.BlockSpec((1, 4, 64, ts), lambda i, bi: (bi, 0, 0, i))
    sq = lambda dd: pl.BlockSpec((1, ts, dd), lambda i, bi: (bi, i, 0))
    t64 = pl.BlockSpec((1, 64, ts), lambda i, bi: (bi, 0, i))
    sd = lambda *shape, dt=BF16: jax.ShapeDtypeStruct(shape, dt)
    out_specs = [pl.BlockSpec((1, 4, 2, ts, 64), lambda i, bi: (bi, 0, 0, i, 0)), h4(64), h4t,
                 h4(64), h4(64), sq(128), sq(64), sq(64), sq(64), t64, t64, sq(128),
                 h4(dk), h4(dk), h4t,
                 h4(64), h4(64), h4t]
    out_shape = [sd(b, 4, 2, s, 64), sd(b, 4, s, 64), sd(b, 4, 64, s),
                 sd(b, 4, s, 64), sd(b, 4, s, 64), sd(b, s, 128), sd(b, s, 64), sd(b, s, 64), sd(b, s, 64),
                 sd(b, 64, s), sd(b, 64, s), sd(b, s, 128, dt=F32),
                 sd(b, 4, s, dk), sd(b, 4, s, dk), sd(b, 4, 64, s),
                 sd(b, 4, s, 64), sd(b, 4, s, 64), sd(b, 4, 64, s)]
    g2 = gain.reshape(1, d).astype(F32)
    outs = pl.pallas_call(
        _project_prep_body,
        grid=(s // ts, b),
        in_specs=[pl.BlockSpec((1, ts, d), lambda i, bi: (bi, i, 0)), full(g2), full(w)] + [tspec] * 6
                 + [full(q_gain), full(kv_gain), full(wq), full(wkv)],
        out_specs=out_specs,
        out_shape=out_shape,
        compiler_params=_cparams("parallel", "parallel"),
        name="w_in_prep",
    )(x, g2, w, *tabs64, *tabs32, q_gain, kv_gain, wq, wkv)
    n0, n1, n2 = np.cumsum(N_PREP_OUT[:3])
    return outs[:n0], outs[n0:n1], outs[n1:n2], outs[n2:]


def _flash_schedule(nq, tq, tk, sk, window):
    qi, kj, fl = [], [], []
    for i in range(nq):
        q_lo, q_hi = i * tq, i * tq + tq - 1
        j_hi = min(q_hi, sk - 1) // tk
        j_lo = 0 if window is None else max(0, q_lo - window) // tk
        for j in range(j_lo, j_hi + 1):
            k_min, k_max = j * tk, j * tk + tk - 1
            full = k_max <= q_lo and (window is None or q_hi - k_min <= window)
            qi.append(i)
            kj.append(j)
            fl.append((1 if j == j_lo else 0) | (2 if j == j_hi else 0) | (0 if full else 4))
    return (np.asarray(qi, np.int32), np.asarray(kj, np.int32), np.asarray(fl, np.int32))


def _flash_body(qi_ref, kj_ref, fl_ref, q_ref, k_ref, vt_ref, o_ref, m_sc, acc_sc, *,
                gb, hq, tq, tk, window, with_lse, merge_groups):
    t = pl.program_id(1)
    fl = fl_ref[t]
    rows = hq * tq
    dv = vt_ref.shape[1]

    @pl.when((fl & 1) != 0)
    def _():
        m_sc[...] = jnp.full(m_sc.shape, NEG_BIG, F32)
        acc_sc[...] = jnp.zeros(acc_sc.shape, F32)

    def step(masked):
        q = q_ref[...].reshape(gb, rows, q_ref.shape[-1])
        st = jnp.einsum("gkd,gqd->gkq", k_ref[...], q, preferred_element_type=F32)
        if masked:
            krow = lax.broadcasted_iota(jnp.int32, (tk, rows), 0)
            qcol = lax.broadcasted_iota(jnp.int32, (tk, rows), 1)
            if hq > 1:
                qcol = qcol & (tq - 1)
            dist = (qi_ref[t] * tq - kj_ref[t] * tk) + qcol - krow
            ok = dist >= 0
            if window is not None:
                ok = ok & (dist <= window)
            st = jnp.where(ok[None], st, NEG_BIG)
        m_prev = m_sc[...]
        m_new = jnp.maximum(m_prev, jnp.max(st, axis=1, keepdims=True))
        alpha = jnp.exp2(m_prev - m_new)
        p = jnp.exp2((st - m_new).astype(BF16))
        v_ext = jnp.concatenate([vt_ref[...], jnp.ones((gb, ONES_ROWS, tk), BF16)], axis=1)
        acc_sc[...] = alpha * acc_sc[...] + jnp.einsum("gdk,gkq->gdq", v_ext, p, preferred_element_type=F32)
        m_sc[...] = m_new

    @pl.when((fl & 4) != 0)
    def _():
        step(True)

    @pl.when((fl & 4) == 0)
    def _():
        step(False)

    @pl.when((fl & 2) != 0)
    def _():
        l = acc_sc[:, dv:dv + 1, :]
        o_t = acc_sc[:, :dv, :] / l
        if merge_groups:
            for n in range(gb // merge_groups):
                heads = o_t[n * merge_groups:(n + 1) * merge_groups]
                o_ref[n] = heads.reshape(merge_groups * dv, rows).T.astype(o_ref.dtype)
            return
        if with_lse:
            extra = jnp.broadcast_to(m_sc[...] * LN2 + jnp.log(l), o_t.shape)
        else:
            extra = jnp.zeros((gb, LANES - dv, rows), F32)
        dvo = o_ref.shape[-1]
        for g in range(gb):
            o = jnp.concatenate([o_t[g], extra[g]], axis=0).T
            o_ref[g] = o[:, :dvo].reshape(o_ref.shape[1:]).astype(o_ref.dtype)


def _flash(q, k, vt, *, tq, tk, gb=1, window=None, with_lse=False, merge_groups=0, out_dtype=F32,
           name="flash"):
    g, hq, sq, dk = q.shape
    _, dv, sk = vt.shape
    tq, tk = min(tq, sq), min(tk, sk)
    assert g % gb == 0 and sq % tq == 0 and sk % tk == 0
    assert tq & (tq - 1) == 0 and 2 * dv == LANES
    qi, kj, fl = _flash_schedule(sq // tq, tq, tk, sk, window)
    dvo = 2 * dv if with_lse else dv
    rows = hq * tq
    if merge_groups:
        mh = merge_groups
        assert hq == 1 and not with_lse and (mh * dv) % LANES == 0 and gb % mh == 0
        out_spec = pl.BlockSpec((gb // mh, tq, mh * dv), lambda gi, t, qi, kj, fl: (gi, qi[t], 0))
        out_shape = jax.ShapeDtypeStruct((g // mh, sq, mh * dv), out_dtype)
    else:
        out_spec = pl.BlockSpec((gb, hq, tq, dvo), lambda gi, t, qi, kj, fl: (gi, 0, qi[t], 0))
        out_shape = jax.ShapeDtypeStruct((g, hq, sq, dvo), out_dtype)
    grid_spec = pltpu.PrefetchScalarGridSpec(
        num_scalar_prefetch=3,
        grid=(g // gb, len(qi)),
        in_specs=[pl.BlockSpec((gb, hq, tq, dk), lambda gi, t, qi, kj, fl: (gi, 0, qi[t], 0)),
                  pl.BlockSpec((gb, tk, dk), lambda gi, t, qi, kj, fl: (gi, kj[t], 0)),
                  pl.BlockSpec((gb, dv, tk), lambda gi, t, qi, kj, fl: (gi, 0, kj[t]))],
        out_specs=out_spec,
        scratch_shapes=[pltpu.VMEM((gb, 1, rows), F32), pltpu.VMEM((gb, dv + ONES_ROWS, rows), F32)],
    )
    return pl.pallas_call(
        functools.partial(_flash_body, gb=gb, hq=hq, tq=tq, tk=tk, window=window, with_lse=with_lse,
                          merge_groups=merge_groups),
        grid_spec=grid_spec,
        out_shape=out_shape,
        compiler_params=_cparams("parallel", "arbitrary"),
        name=name,
    )(jnp.asarray(qi), jnp.asarray(kj), jnp.asarray(fl), q, k, vt)


def _band_body(q_ref, kp_ref, kc_ref, vp_ref, vc_ref, o_ref, *, tq, window):
    i = pl.program_id(1)
    gb, dv = vp_ref.shape[0], vp_ref.shape[1]
    k2 = jnp.concatenate([kp_ref[...], kc_ref[...]], axis=1)
    st = jnp.einsum("gkd,gqd->gkq", k2, q_ref[...], preferred_element_type=F32)
    krow = lax.broadcasted_iota(jnp.int32, (2 * tq, tq), 0)
    qcol = lax.broadcasted_iota(jnp.int32, (2 * tq, tq), 1)
    dist = qcol + tq - krow
    ok = (dist >= 0) & (dist <= window) & ((krow >= tq) | (i > 0))
    st = jnp.where(ok[None], st, NEG_BIG)
    m = jnp.max(st, axis=1, keepdims=True)
    p = jnp.exp2((st - m).astype(BF16))
    v_ext = jnp.concatenate([jnp.concatenate([vp_ref[...], vc_ref[...]], axis=2),
                             jnp.ones((gb, ONES_ROWS, 2 * tq), BF16)], axis=1)
    acc = jnp.einsum("gdk,gkq->gdq", v_ext, p, preferred_element_type=F32)
    l = acc[:, dv:dv + 1, :]
    o_t = acc[:, :dv, :] / l
    lse = jnp.broadcast_to(m * LN2 + jnp.log(l), o_t.shape)
    for g in range(gb):
        o_ref[g] = jnp.concatenate([o_t[g], lse[g]], axis=0).T.astype(o_ref.dtype)


def _band_attention(q, k, vt, *, tq, gb, window, name):
    g, s, dk = q.shape
    dv = vt.shape[1]
    tq = min(tq, s)
    assert window <= tq and g % gb == 0 and s % tq == 0 and 2 * dv == LANES
    prev = lambda i: jnp.maximum(i - 1, 0)
    return pl.pallas_call(
        functools.partial(_band_body, tq=tq, window=window),
        grid=(g // gb, s // tq),
        in_specs=[pl.BlockSpec((gb, tq, dk), lambda gi, i: (gi, i, 0)),
                  pl.BlockSpec((gb, tq, dk), lambda gi, i: (gi, prev(i), 0)),
                  pl.BlockSpec((gb, tq, dk), lambda gi, i: (gi, i, 0)),
                  pl.BlockSpec((gb, dv, tq), lambda gi, i: (gi, 0, prev(i))),
                  pl.BlockSpec((gb, dv, tq), lambda gi, i: (gi, 0, i))],
        out_specs=pl.BlockSpec((gb, tq, 2 * dv), lambda gi, i: (gi, i, 0)),
        out_shape=jax.ShapeDtypeStruct((g, s, 2 * dv), F32),
        compiler_params=_cparams("parallel", "parallel"),
        name=name,
    )(q, k, k, vt, vt)


def _gelu_tanh(x):
    return 0.5 * x * (1.0 + jnp.tanh(math.sqrt(2.0 / math.pi) * (x + 0.044715 * (x * x * x))))


def _compress_body(r_ref, pos_ref, w1_ref, w2_ref, o_ref, ot_ref):
    half = CMP_STRIDE * HEAD_DIM
    r = r_ref[0, 0]
    w1 = w1_ref[0]
    top = jnp.dot(r, w1[:half], preferred_element_type=F32)
    bot = jnp.dot(r, w1[half:], preferred_element_type=F32)
    nc = r.shape[0]
    posb = jnp.dot(pos_ref[0], w1, preferred_element_type=F32)[:1]
    pre = top + pltpu.roll(bot, nc - 1, 0) + posb
    hid = _gelu_tanh(pre).astype(BF16)
    o = jnp.dot(hid, w2_ref[0], preferred_element_type=F32)
    o_ref[0, 0] = o.astype(o_ref.dtype)
    ot_ref[0, 0] = jnp.concatenate([o, jnp.zeros_like(o)], axis=1).T[:HEAD_DIM].astype(ot_ref.dtype)


def _nsa_compress(kv_rows, pos_flat, w1, w2):
    _, b, nc, _ = kv_rows.shape
    return pl.pallas_call(
        _compress_body,
        grid=(2, b),
        in_specs=[pl.BlockSpec((1, 1, nc, 1024), lambda w, bi: (w, bi, 0, 0)),
                  pl.BlockSpec((1, 8, 2048), lambda w, bi: (w, 0, 0)),
                  pl.BlockSpec((1, 2048, 256), lambda w, bi: (w, 0, 0)),
                  pl.BlockSpec((1, 256, 64), lambda w, bi: (w, 0, 0))],
        out_specs=[pl.BlockSpec((1, 1, nc, 64), lambda w, bi: (w, bi, 0, 0)),
                   pl.BlockSpec((1, 1, 64, nc), lambda w, bi: (w, bi, 0, 0))],
        out_shape=[jax.ShapeDtypeStruct((2, b, nc, 64), BF16), jax.ShapeDtypeStruct((2, b, 64, nc), BF16)],
        compiler_params=_cparams("parallel", "parallel"),
        name="nsa_compress",
    )(kv_rows, pos_flat, w1, w2)


def _cmp_select_body(qn_ref, qr_ref, kc_ref, vct_ref, covt_ref, o_ref, qa_ref, *, tq, n_sel, k_top):
    ncp = kc_ref.shape[2]
    qpos = pl.program_id(1) * tq + lax.broadcasted_iota(jnp.int32, (ncp, tq), 1)
    blk_end = lax.broadcasted_iota(jnp.int32, (ncp, tq), 0) * CMP_STRIDE + (CMP_LEN - 1)
    mask = blk_end <= qpos
    kc, vct = kc_ref[0, 0], vct_ref[0, 0]
    p_sum = jnp.zeros((ncp, tq), F32)
    outs = []
    for h in range(GROUP_HEADS):
        st = lax.dot_general(kc, qn_ref[0, h], (((1,), (1,)), ((), ())), preferred_element_type=F32)
        st = jnp.where(mask, st, NEG_BIG)
        mx = jnp.max(st, axis=0, keepdims=True)
        mx = jnp.where(mx > 0.5 * NEG_BIG, mx, 0.0)
        e = jnp.where(mask, jnp.exp2(st - mx), 0.0)
        p = e / jnp.maximum(jnp.sum(e, axis=0, keepdims=True), 1e-30)
        outs.append(jnp.dot(vct, p.astype(BF16), preferred_element_type=F32))
        p_sum = p_sum + p
    o_ref[0] = jnp.concatenate(outs, axis=0).T.astype(o_ref.dtype)
    p_hi = p_sum.astype(BF16)
    p_lo = (p_sum - p_hi.astype(F32)).astype(BF16)
    covt = covt_ref[...]
    imp_t = (jnp.dot(covt, p_hi, preferred_element_type=F32) + jnp.dot(covt, p_lo, preferred_element_type=F32))
    nblk = SEL_BLOCK
    tpos = pl.program_id(1) * tq + lax.broadcasted_iota(jnp.int32, (nblk, tq), 1)
    cur = tpos >> SEL_SHIFT
    j = lax.broadcasted_iota(jnp.int32, (nblk, tq), 0)
    valid = j <= cur
    forced = (j == 0) | (j == cur) | (j == cur - 1)
    score = jnp.where(valid, jnp.where(forced, FORCED_SCORE, imp_t), -jnp.inf)
    sub = 8
    groups = [score[g * sub:(g + 1) * sub] for g in range(nblk // sub)]
    ranks = [jnp.zeros((sub, tq), jnp.int32) for _ in groups]
    jj = lax.broadcasted_iota(jnp.int32, (sub, tq), 0)
    for i in range(n_sel):
        row = jnp.broadcast_to(score[i:i + 1], (sub, tq))
        for g, sg in enumerate(groups):
            if i < g * sub:
                ahead = row >= sg
            elif i >= (g + 1) * sub:
                ahead = row > sg
            else:
                ahead = (row > sg) | ((row == sg) & (i - g * sub < jj))
            ranks[g] = ranks[g] + ahead.astype(jnp.int32)
    rank = jnp.concatenate(ranks, axis=0)
    bias_t = jnp.where(valid & (rank < k_top), 0.0, SEL_BIAS)
    bias = jnp.concatenate([bias_t, jnp.zeros((LANES - nblk, tq), F32)], axis=0).T[:, :nblk].astype(BF16)
    for h in range(GROUP_HEADS):
        qa_ref[0, h] = jnp.concatenate([qr_ref[0, h], bias], axis=-1)


def _cover_matrix(ncp, n_sel, lanes):
    c_start = np.arange(ncp)[:, None] * CMP_STRIDE
    s_start = np.arange(lanes)[None, :] * SEL_BLOCK
    cov = np.clip(np.minimum(c_start + CMP_LEN, s_start + SEL_BLOCK) - np.maximum(c_start, s_start), 0, None)
    cov = cov.astype(np.float32) / CMP_LEN
    cov[:, n_sel:] = 0.0
    cov[ncp - 1:, :] = 0.0
    return cov


def _cmp_select(qn, qr, kvc, kvct, tq):
    b, _, s, _ = qn.shape
    ncp = kvc.shape[2]
    n_sel = s // SEL_BLOCK
    assert n_sel <= SEL_BLOCK
    k_top = min(SEL_TOPN, n_sel)
    covt = jnp.asarray(_cover_matrix(ncp, n_sel, SEL_BLOCK).T, BF16)
    h4 = lambda d: pl.BlockSpec((1, 4, tq, d), lambda bi, i: (bi, 0, i, 0))
    return pl.pallas_call(
        functools.partial(_cmp_select_body, tq=tq, n_sel=n_sel, k_top=k_top),
        grid=(b, s // tq),
        in_specs=[h4(64), h4(64),
                  pl.BlockSpec((1, 1, ncp, 64), lambda bi, i: (0, bi, 0, 0)),
                  pl.BlockSpec((1, 1, 64, ncp), lambda bi, i: (1, bi, 0, 0)),
                  pl.BlockSpec(covt.shape, lambda bi, i: (0, 0))],
        out_specs=[pl.BlockSpec((1, tq, 256), lambda bi, i: (bi, i, 0)), h4(128)],
        out_shape=[jax.ShapeDtypeStruct((b, s, 256), BF16), jax.ShapeDtypeStruct((b, 4, s, 128), BF16)],
        compiler_params=_cparams("parallel", "parallel"),
        name="nsa_cmp_select",
    )(qn, qr, kvc, kvct, covt)


def _merge_a(o_ref, lam_ref, g_ref, out_scale):
    outs = []
    for h in range(GROUP_HEADS):
        o = o_ref[0, h, 0].astype(F32) - lam_ref[...] * o_ref[0, h, 1].astype(F32)
        outs.append(_rms(o, g_ref[...]) * out_scale)
    return outs


def _merge_b(oc_ref, os_ref, ow_ref, g_ref):
    g = g_ref[0]
    outs = []
    for h in range(GROUP_HEADS):
        outs.append(g[:, 3 * h:3 * h + 1] * oc_ref[0, :, h * 64:(h + 1) * 64].astype(F32)
                    + g[:, 3 * h + 1:3 * h + 2] * os_ref[0, h].astype(F32)
                    + g[:, 3 * h + 2:3 * h + 3] * ow_ref[0, h].astype(F32))
    return outs


def _merge_d(o1_ref, o2_ref, o3_ref):
    outs = []
    for h in range(GROUP_HEADS):
        e = [r[0, h] for r in (o1_ref, o2_ref, o3_ref)]
        lse = [x[:, 64:] for x in e]
        mx = jnp.maximum(jnp.maximum(lse[0], lse[1]), lse[2])
        w = [jnp.exp(x - mx) for x in lse]
        den = w[0] + w[1] + w[2]
        outs.append((w[0] * e[0][:, :64] + w[1] * e[1][:, :64] + w[2] * e[2][:, :64]) / den)
    return outs


def _out_proj_body(x_ref, oa_ref, lam_ref, sub_ref, oc_ref, os_ref, ow_ref, g_ref, yc_ref, d1_ref, d2_ref, d3_ref,
                   w_ref, o_ref, *, out_scale):
    parts = (_merge_a(oa_ref, lam_ref, sub_ref, out_scale) + _merge_b(oc_ref, os_ref, ow_ref, g_ref)
             + [yc_ref[0].astype(F32)] + _merge_d(d1_ref, d2_ref, d3_ref))
    y = jnp.concatenate(parts, axis=-1).astype(BF16)
    o_ref[0] = x_ref[0] + jnp.dot(y, w_ref[...], preferred_element_type=F32)


def _out_proj(x, oa, lam, subln, out_scale, o_cmp, o_sel, o_win, gates, y_c, o_d, w, ts):
    b, s, d = x.shape
    row = lambda dd: pl.BlockSpec((1, ts, dd), lambda bi, i: (bi, i, 0))
    h4 = lambda dd: pl.BlockSpec((1, 4, ts, dd), lambda bi, i: (bi, 0, i, 0))
    vec = pl.BlockSpec((1, 64), lambda bi, i: (0, 0))
    return pl.pallas_call(
        functools.partial(_out_proj_body, out_scale=out_scale),
        grid=(b, s // ts),
        in_specs=[row(d), pl.BlockSpec((1, 4, 2, ts, 64), lambda bi, i: (bi, 0, 0, i, 0)), vec, vec,
                  row(256), h4(64), h4(64), row(128), row(256), h4(128), h4(128), h4(128),
                  pl.BlockSpec(w.shape, lambda bi, i: (0, 0))],
        out_specs=row(d),
        out_shape=jax.ShapeDtypeStruct((b, s, d), F32),
        compiler_params=_cparams("parallel", "parallel"),
        name="w_out_merge",
    )(x, oa, lam, subln, o_cmp, o_sel, o_win, gates, y_c, *o_d, w)


def _permute_w_in(w_in):
    a_cols = 768
    b0 = a_cols
    nq = 256
    seg = lambda i: w_in[:, b0 + nq + i * 64: b0 + nq + (i + 1) * 64]
    kc, vc, ks, vs, kw, vw = (seg(i) for i in range(6))
    gb = w_in[:, b0 + nq + 384: b0 + nq + 384 + 12]
    d = w_in.shape[0]
    w_b = jnp.concatenate([w_in[:, b0:b0 + nq], ks, kw, kc, vc, vs, vw, gb, jnp.zeros((d, 116), w_in.dtype)], axis=1)
    c0 = b0 + 652
    w_c = jnp.concatenate([w_in[:, c0:c0 + 544], jnp.zeros((d, 96), w_in.dtype)], axis=1)
    d0 = c0 + 544
    return jnp.concatenate([w_in[:, :a_cols], w_b, w_c, w_in[:, d0:d0 + 768]], axis=1).astype(BF16)


def _residue_classes(t, dil):
    g, s, d = t.shape
    return t.reshape(g, s // dil, dil, d).transpose(0, 2, 1, 3).reshape(g * dil, s // dil, d)


def _residue_classes_t(t, dil):
    g, d, s = t.shape
    return t.reshape(g, d, s // dil, dil).transpose(0, 3, 1, 2).reshape(g * dil, d, s // dil)


def _from_residue_classes(t, dil):
    gd, l, d = t.shape
    return t.reshape(gd // dil, dil, l, d).transpose(0, 2, 1, 3).reshape(gd // dil, l * dil, d)


def _token_mixers(x, layer, rope_tabs, norm_g, w_in, w_out, diff_lambda, diff_subln, cmp_pos, cmp_w1, cmp_w2,
                  q_norm, kv_norm, w_uq, w_ukv):
    b, s, d = x.shape
    t = b * s
    ts = min(512, s)
    wq = w_uq.reshape(-1, 4, MLA_NOPE + MLA_ROPE)
    wq = jnp.concatenate([wq[:, :, :MLA_NOPE].reshape(-1, 256), wq[:, :, MLA_NOPE:].reshape(-1, 128)], axis=1)
    wkv = w_ukv.reshape(-1, 4, MLA_NOPE + MLA_DV)
    wkv = jnp.concatenate([wkv[:, :, :MLA_NOPE].reshape(-1, 256), wkv[:, :, MLA_NOPE:].reshape(-1, 256)], axis=1)
    prep_a, prep_b, prep_c, prep_d = _project_prep(
        x, norm_g, _permute_w_in(w_in), rope_tabs[64], rope_tabs[32], q_norm.reshape(1, -1).astype(F32),
        kv_norm.reshape(1, -1).astype(F32), wq.astype(BF16), wkv.astype(BF16), ts)

    qa, ka, va = prep_a
    oa = _flash(qa.reshape(b * 4, 2, s, 64), ka.reshape(b * 4, s, 64), va.reshape(b * 4, 64, s),
                tq=512, tk=512, gb=math.gcd(b * 4, 8), out_dtype=BF16, name="flash_diff").reshape(b, 4, 2, s, 64)
    lam_init = 0.8 - 0.6 * math.exp(-0.3 * layer)
    lp = diff_lambda.astype(F32)
    lam = jnp.exp(jnp.sum(lp[0] * lp[1])) - jnp.exp(jnp.sum(lp[2] * lp[3])) + lam_init

    qn, qr, k_aug, kw, kc, vc, vs, vw, gates = prep_b
    kv_rows = jnp.stack([kc, vc]).reshape(2, b, s // CMP_STRIDE, CMP_STRIDE * HEAD_DIM)
    pos_flat = jnp.broadcast_to(cmp_pos.reshape(2, 1, CMP_LEN * HEAD_DIM), (2, 8, CMP_LEN * HEAD_DIM)).astype(BF16)
    kvc, kvct = _nsa_compress(kv_rows, pos_flat, cmp_w1.astype(BF16), cmp_w2.astype(BF16))
    o_cmp, q_aug = _cmp_select(qn, qr, kvc, kvct, min(256, s))
    gb_b = math.gcd(b, 4)
    o_sel = _flash(q_aug, k_aug, vs, tq=256, tk=512, gb=gb_b, out_dtype=BF16, name="flash_sel")
    o_win = _flash(qr, kw, vw, tq=256, tk=256, gb=gb_b, window=NSA_WINDOW - 1, out_dtype=BF16, name="flash_win")

    qc, kcat, vcv = prep_c
    dk = MLA_NOPE + MLA_ROPE
    y_c = _flash(qc.reshape(b * 4, 1, s, dk), kcat.reshape(b * 4, s, dk), vcv.reshape(b * 4, 64, s),
                 tq=512, tk=512, gb=math.gcd(b * 4, 16), merge_groups=4, out_dtype=BF16, name="flash_mla")

    qd, kd, vd = prep_d
    qd, kd, vd = qd.reshape(b * 4, s, 64), kd.reshape(b * 4, s, 64), vd.reshape(b * 4, 64, s)
    o_d = []
    for window, dil in DILATED_PATTERNS:
        l = s // dil
        qq, kk, vv = _residue_classes(qd, dil), _residue_classes(kd, dil), _residue_classes_t(vd, dil)
        g = qq.shape[0]
        gb = math.gcd(g, 32)
        o = _band_attention(qq, kk, vv, tq=128, gb=gb, window=window // dil, name=f"band_dil{dil}")
        o_d.append(_from_residue_classes(o, dil).reshape(b, 4, s, 128))
    return _out_proj(x, oa, jnp.full((1, 64), lam, F32), diff_subln.reshape(1, 64).astype(F32), 1.0 - lam_init,
                     o_cmp, o_sel, o_win, gates, y_c, o_d, w_out.astype(BF16), ts)


def _cross_body(x_ref, g_ref, wq_ref, k_ref, v_ref, wo_ref, o_ref):
    x = x_ref[0]
    h = _rms(x, g_ref[...]).astype(BF16)
    q = jnp.dot(h, wq_ref[...], preferred_element_type=F32).astype(BF16)
    dh = q.shape[-1] // CROSS_HEADS
    outs = []
    for hd in range(CROSS_HEADS):
        sl = slice(hd * dh, (hd + 1) * dh)
        s = lax.dot_general(q[:, sl], k_ref[0, :, sl], (((1,), (1,)), ((), ())), preferred_element_type=F32)
        e = jnp.exp(s - jnp.max(s, axis=-1, keepdims=True))
        p = e / jnp.sum(e, axis=-1, keepdims=True)
        outs.append(jnp.dot(p.astype(BF16), v_ref[0, :, sl], preferred_element_type=F32))
    o = jnp.concatenate(outs, axis=-1).astype(BF16)
    o_ref[0] = x + jnp.dot(o, wo_ref[...], preferred_element_type=F32)


def _cross_attention(x, mem, g_x, g_mem, wq, wk, wv, wo):
    b, s, d = x.shape
    m = mem.shape[1]
    ts = min(512, s)
    dh = d // CROSS_HEADS
    k, v = _mm([mem.reshape(b * m, d)], jnp.concatenate([wk, wv], axis=1).astype(BF16), gain=g_mem,
               tm=min(512, b * m), splits=(d, d), out_dtype=BF16, name="cross_kv")
    full = lambda a: pl.BlockSpec(a.shape, lambda bi, i: (0,) * a.ndim)
    wq_s = (wq * dh ** -0.5).astype(BF16)
    wo_b = wo.astype(BF16)
    g2 = g_x.reshape(1, d).astype(F32)
    return pl.pallas_call(
        _cross_body,
        grid=(b, s // ts),
        in_specs=[pl.BlockSpec((1, ts, d), lambda bi, i: (bi, i, 0)), full(g2), full(wq_s),
                  pl.BlockSpec((1, m, d), lambda bi, i: (bi, 0, 0)),
                  pl.BlockSpec((1, m, d), lambda bi, i: (bi, 0, 0)), full(wo_b)],
        out_specs=pl.BlockSpec((1, ts, d), lambda bi, i: (bi, i, 0)),
        out_shape=jax.ShapeDtypeStruct((b, s, d), F32),
        compiler_params=_cparams("parallel", "parallel"),
        name="cross_attn",
    )(x, g2, wq_s, k.reshape(b, m, d), v.reshape(b, m, d), wo_b)


def _swiglu_body(x_ref, g_ref, wg_ref, wu_ref, wd_ref, o_ref, h_sc, acc_sc):
    f = pl.program_id(1)

    @pl.when(f == 0)
    def _():
        h_sc[...] = _rms(x_ref[...], g_ref[...]).astype(BF16)
        acc_sc[...] = x_ref[...]

    h = h_sc[...]
    gate = jnp.dot(h, wg_ref[...], preferred_element_type=F32)
    up = jnp.dot(h, wu_ref[...], preferred_element_type=F32)
    act = (gate * jax.nn.sigmoid(gate) * up).astype(BF16)
    acc_sc[...] += jnp.dot(act, wd_ref[...], preferred_element_type=F32)

    @pl.when(f == pl.num_programs(1) - 1)
    def _():
        o_ref[...] = acc_sc[...]


def _swiglu(x, gain, wg, wu, wd, tm=512, tf=1408):
    t, d = x.shape
    ff = wg.shape[1]
    tm, tf = min(tm, t), min(tf, ff)
    assert t % tm == 0 and ff % tf == 0
    return pl.pallas_call(
        _swiglu_body,
        grid=(t // tm, ff // tf),
        in_specs=[pl.BlockSpec((tm, d), lambda i, f: (i, 0)),
                  pl.BlockSpec((1, d), lambda i, f: (0, 0)),
                  pl.BlockSpec((d, tf), lambda i, f: (0, f)),
                  pl.BlockSpec((d, tf), lambda i, f: (0, f)),
                  pl.BlockSpec((tf, d), lambda i, f: (f, 0))],
        out_specs=pl.BlockSpec((tm, d), lambda i, f: (i, 0)),
        out_shape=jax.ShapeDtypeStruct((t, d), F32),
        scratch_shapes=[pltpu.VMEM((tm, d), BF16), pltpu.VMEM((tm, d), F32)],
        compiler_params=_cparams("parallel", "arbitrary"),
        name="swiglu",
    )(x, gain.reshape(1, d).astype(F32), wg.astype(BF16), wu.astype(BF16), wd.astype(BF16))


def _router_body(x_ref, g_ref, r_ref, h_ref, rt_ref):
    h = _rms(x_ref[...], g_ref[...])
    h_ref[...] = h
    h_hi = h.astype(BF16)
    h_lo = (h - h_hi.astype(F32)).astype(BF16)
    r = r_ref[...]
    r_hi = r.astype(BF16)
    r_lo = (r - r_hi.astype(F32)).astype(BF16)
    logits = (jnp.dot(h_hi, r_hi, preferred_element_type=F32) + jnp.dot(h_lo, r_hi, preferred_element_type=F32)
              + jnp.dot(h_hi, r_lo, preferred_element_type=F32))
    lane = lax.broadcasted_iota(jnp.int32, logits.shape, 1)
    lg = jnp.where(lane < N_EXPERTS, logits, -jnp.inf)
    v0 = jnp.max(lg, axis=-1, keepdims=True)
    e0 = jnp.min(jnp.where(lg == v0, lane, LANES), axis=-1, keepdims=True)
    lg1 = jnp.where(lane == e0, -jnp.inf, lg)
    v1 = jnp.max(lg1, axis=-1, keepdims=True)
    e1 = jnp.min(jnp.where(lg1 == v1, lane, LANES), axis=-1, keepdims=True)
    g1 = 1.0 / (1.0 + jnp.exp(v0 - v1))
    g0 = 1.0 - g1
    out = jnp.where(lane == 0, g0, jnp.where(lane == 1, g1, 0.0))
    out = jnp.where(lane == 2, e0.astype(F32), jnp.where(lane == 3, e1.astype(F32), out))
    rt_ref[...] = out


def _router(x, gain, router, tm=512):
    t, d = x.shape
    tm = min(tm, t)
    r_pad = jnp.zeros((d, LANES), F32).at[:, :N_EXPERTS].set(router.astype(F32))
    return pl.pallas_call(
        _router_body,
        grid=(t // tm,),
        in_specs=[pl.BlockSpec((tm, d), lambda i: (i, 0)), pl.BlockSpec((1, d), lambda i: (0, 0)),
                  pl.BlockSpec((d, LANES), lambda i: (0, 0))],
        out_specs=[pl.BlockSpec((tm, d), lambda i: (i, 0)), pl.BlockSpec((tm, LANES), lambda i: (i, 0))],
        out_shape=[jax.ShapeDtypeStruct((t, d), F32), jax.ShapeDtypeStruct((t, LANES), F32)],
        compiler_params=_cparams("parallel"),
        name="moe_router",
    )(x, gain.reshape(1, d).astype(F32), r_pad)


def _row_copy(src_hbm, src_row, dst_ref, dst_row, sem):
    return pltpu.make_async_copy(src_hbm.at[pl.ds(src_row, 1)], dst_ref.at[pl.ds(dst_row, 1)], sem)


def _rank_body(rt_ref, u_ref, ps_ref, dest_ref, carry_sc):
    k, i = pl.program_id(0), pl.program_id(1)
    tm = rt_ref.shape[0]

    @pl.when((k == 0) & (i == 0))
    def _():
        carry_sc[...] = jnp.zeros(carry_sc.shape, F32)

    rt_t = rt_ref[...].T
    e_row = jnp.where(k == 0, rt_t[2:3], rt_t[3:4]).astype(jnp.int32)
    sub = lax.broadcasted_iota(jnp.int32, (N_EXPERTS, tm), 0)
    onehot = sub == e_row
    ahead = jnp.dot(onehot.astype(BF16), u_ref[...], preferred_element_type=F32)
    row = ahead + carry_sc[:, :1] + ps_ref[:, :1]
    dest = jnp.sum(jnp.where(onehot, row, 0.0), axis=0, keepdims=True)
    dest_ref[...] = dest.astype(jnp.int32).reshape(dest_ref.shape)
    carry_sc[...] += jnp.sum(onehot.astype(F32), axis=1, keepdims=True)


def _assignment_rows(routing, pstart, tm=512):
    t = routing.shape[0]
    tm = min(tm, t)
    upper = jnp.asarray(np.triu(np.ones((tm, tm), np.float32), 1), BF16)
    ps = jnp.broadcast_to(pstart.astype(F32)[:, None], (N_EXPERTS, LANES))
    out = pl.pallas_call(
        _rank_body,
        grid=(TOP_K, t // tm),
        in_specs=[pl.BlockSpec((tm, LANES), lambda k, i: (i, 0)),
                  pl.BlockSpec((tm, tm), lambda k, i: (0, 0)),
                  pl.BlockSpec((N_EXPERTS, LANES), lambda k, i: (0, 0))],
        out_specs=pl.BlockSpec((1, 1, 1, tm), lambda k, i: (k, i, 0, 0)),
        out_shape=jax.ShapeDtypeStruct((TOP_K, t // tm, 1, tm), jnp.int32),
        scratch_shapes=[pltpu.VMEM((N_EXPERTS, LANES), F32)],
        compiler_params=_cparams("arbitrary", "arbitrary"),
        name="moe_rank",
    )(routing, upper, ps)
    return out.reshape(TOP_K * t)


def _dispatch_body(dest_ref, h_ref, xs_in, xs_hbm, sem):
    del xs_in
    tm = h_ref.shape[0]
    t = dest_ref.shape[0] // TOP_K
    base = pl.program_id(0) * tm

    def copy(r, k, dst_row):
        return pltpu.make_async_copy(h_ref.at[pl.ds(r, 1)], xs_hbm.at[pl.ds(dst_row, 1)], sem)

    def start(r, c):
        for k in range(TOP_K):
            copy(r, k, dest_ref[k * t + base + r]).start()
        return c

    lax.fori_loop(0, tm, start, 0, unroll=8)
    for r in range(tm):
        for k in range(TOP_K):
            copy(r, k, 0).wait()


def _dispatch(h, dest, n_rows, tm=256):
    t, d = h.shape
    tm = min(tm, t)
    grid_spec = pltpu.PrefetchScalarGridSpec(
        num_scalar_prefetch=1,
        grid=(t // tm,),
        in_specs=[pl.BlockSpec((tm, d), lambda i, dest: (i, 0)), pl.BlockSpec(memory_space=pl.ANY)],
        out_specs=pl.BlockSpec(memory_space=pl.ANY),
        scratch_shapes=[pltpu.SemaphoreType.DMA(())],
    )
    return pl.pallas_call(
        _dispatch_body,
        grid_spec=grid_spec,
        out_shape=jax.ShapeDtypeStruct((n_rows, d), F32),
        input_output_aliases={2: 0},
        compiler_params=_cparams("arbitrary"),
        name="moe_dispatch",
    )(dest, h, jnp.zeros((n_rows, d), F32))


def _experts_body(be_ref, nb_ref, x_ref, wg_ref, wu_ref, wd_ref, o_ref, xb_sc, acc_sc):
    i, f = pl.program_id(0), pl.program_id(1)

    @pl.when(i < nb_ref[0])
    def _():
        @pl.when(f == 0)
        def _():
            xb_sc[...] = x_ref[...].astype(BF16)
            acc_sc[...] = jnp.zeros(acc_sc.shape, F32)

        x = xb_sc[...]
        gate = jnp.dot(x, wg_ref[0], preferred_element_type=F32)
        up = jnp.dot(x, wu_ref[0], preferred_element_type=F32)
        act = (gate * jax.nn.sigmoid(gate) * up).astype(BF16)
        acc_sc[...] += jnp.dot(act, wd_ref[0], preferred_element_type=F32)

        @pl.when(f == pl.num_programs(1) - 1)
        def _():
            o_ref[...] = acc_sc[...]

    @pl.when(i >= nb_ref[0])
    def _():
        o_ref[...] = jnp.zeros(o_ref.shape, F32)


def _experts(xs, blk_e, n_used, wg, wu, wd, rows, tf=1792):
    n_rows, d = xs.shape
    ff = wg.shape[2]
    n_blk = n_rows // rows
    last = ff // tf - 1
    grid_spec = pltpu.PrefetchScalarGridSpec(
        num_scalar_prefetch=2,
        grid=(n_blk, ff // tf),
        in_specs=[pl.BlockSpec((rows, d), lambda i, f, be, nb: (jnp.minimum(i, nb[0] - 1), 0)),
                  pl.BlockSpec((1, d, tf), lambda i, f, be, nb: (be[i], 0, jnp.where(i < nb[0], f, last))),
                  pl.BlockSpec((1, d, tf), lambda i, f, be, nb: (be[i], 0, jnp.where(i < nb[0], f, last))),
                  pl.BlockSpec((1, tf, d), lambda i, f, be, nb: (be[i], jnp.where(i < nb[0], f, last), 0))],
        out_specs=pl.BlockSpec((rows, d), lambda i, f, be, nb: (i, 0)),
        scratch_shapes=[pltpu.VMEM((rows, d), BF16), pltpu.VMEM((rows, d), F32)],
    )
    return pl.pallas_call(
        _experts_body,
        grid_spec=grid_spec,
        out_shape=jax.ShapeDtypeStruct((n_rows, d), F32),
        compiler_params=_cparams("arbitrary", "arbitrary"),
        name="moe_experts",
    )(blk_e, n_used, xs, wg, wu, wd)


def _combine_body(dest_ref, x_ref, rt_ref, ys_hbm, o_ref, buf, sem):
    tm = x_ref.shape[0]
    n_tok = dest_ref.shape[0] // TOP_K
    i = pl.program_id(0)
    slot = i & 1

    def start(tile, s):
        def body(r, c):
            for k in range(TOP_K):
                _row_copy(ys_hbm, dest_ref[k * n_tok + tile * tm + r], buf.at[s, k], r, sem.at[s]).start()
            return c
        lax.fori_loop(0, tm, body, 0, unroll=8)

    @pl.when(i == 0)
    def _():
        start(0, 0)

    @pl.when(i + 1 < pl.num_programs(0))
    def _():
        start(i + 1, 1 - slot)

    for r in range(tm):
        for k in range(TOP_K):
            _row_copy(ys_hbm, 0, buf.at[slot, k], r, sem.at[slot]).wait()
    rt = rt_ref[...]
    o_ref[...] = x_ref[...] + rt[:, 0:1] * buf[slot, 0] + rt[:, 1:2] * buf[slot, 1]


def _combine(x, routing, ys, dest, tm=256):
    t, d = x.shape
    tm = min(tm, t)
    grid_spec = pltpu.PrefetchScalarGridSpec(
        num_scalar_prefetch=1,
        grid=(t // tm,),
        in_specs=[pl.BlockSpec((tm, d), lambda i, dest: (i, 0)),
                  pl.BlockSpec((tm, LANES), lambda i, dest: (i, 0)),
                  pl.BlockSpec(memory_space=pl.ANY)],
        out_specs=pl.BlockSpec((tm, d), lambda i, dest: (i, 0)),
        scratch_shapes=[pltpu.VMEM((2, TOP_K, tm, d), F32), pltpu.SemaphoreType.DMA((2,))],
    )
    return pl.pallas_call(
        _combine_body,
        grid_spec=grid_spec,
        out_shape=jax.ShapeDtypeStruct((t, d), F32),
        compiler_params=_cparams("arbitrary"),
        name="moe_combine",
    )(dest, x, routing, ys)


MOE_ROWS = 512


def _moe(x, gain, router, w_gate, w_up, w_down):
    t, d = x.shape
    h, routing = _router(x, gain, router)
    n_assign = t * TOP_K
    experts = routing[:, 2:2 + TOP_K].astype(jnp.int32)
    counts = jnp.sum((experts[:, :, None] == jnp.arange(N_EXPERTS)[None, None, :]).astype(jnp.int32), axis=(0, 1))
    padded = (counts + MOE_ROWS - 1) // MOE_ROWS * MOE_ROWS
    pend = jnp.cumsum(padded)
    pstart = pend - padded
    n_rows = -(-n_assign // MOE_ROWS) * MOE_ROWS + N_EXPERTS * MOE_ROWS
    n_blk = n_rows // MOE_ROWS
    blk_start = jnp.arange(n_blk, dtype=jnp.int32) * MOE_ROWS
    blk_e = jnp.minimum(jnp.sum((pend[None, :] <= blk_start[:, None]).astype(jnp.int32), axis=1), N_EXPERTS - 1)
    n_used = (pend[-1] // MOE_ROWS).astype(jnp.int32).reshape(1)
    dest = _assignment_rows(routing, pstart)
    xs = _dispatch(h, dest, n_rows)
    ys = _experts(xs, blk_e.astype(jnp.int32), n_used, w_gate.astype(BF16), w_up.astype(BF16),
                  w_down.astype(BF16), MOE_ROWS)
    return _combine(x, routing, ys, dest)


def _final_norm_body(x_ref, g_ref, o_ref):
    o_ref[...] = _rms(x_ref[...], g_ref[...])


def _final_norm(x, gain, tm=1024):
    t, d = x.shape
    tm = min(tm, t)
    return pl.pallas_call(
        _final_norm_body,
        grid=(t // tm,),
        in_specs=[pl.BlockSpec((tm, d), lambda i: (i, 0)), pl.BlockSpec((1, d), lambda i: (0, 0))],
        out_specs=pl.BlockSpec((tm, d), lambda i: (i, 0)),
        out_shape=jax.ShapeDtypeStruct((t, d), F32),
        compiler_params=_cparams("parallel"),
        name="final_norm",
    )(x, gain.reshape(1, d).astype(F32))


def _all_rope_tables(s):
    return {64: _rope_tables(s, 64), 32: _rope_tables(s, 32)}


def kernel(x, mem, norm_mix, w_in, w_out, diff_lambda, diff_subln, nsa_cmp_pos, nsa_cmp_w1, nsa_cmp_w2, mla_q_norm, mla_kv_norm, mla_w_uq, mla_w_ukv, norm_cross, norm_mem, cross_wq, cross_wk, cross_wv, cross_wo, norm_ffn, ffn_w_gate, ffn_w_up, ffn_w_down, moe_router, moe_w_gate, moe_w_up, moe_w_down, final_norm):
    b, s, d = x.shape
    depth = w_in.shape[0]
    tabs = _all_rope_tables(s)
    for l in range(depth):
        x = _token_mixers(x, l, tabs, norm_mix[l], w_in[l], w_out[l], diff_lambda[l], diff_subln[l],
                          nsa_cmp_pos[l], nsa_cmp_w1[l], nsa_cmp_w2[l], mla_q_norm[l], mla_kv_norm[l],
                          mla_w_uq[l], mla_w_ukv[l])
        x = _cross_attention(x, mem, norm_cross[l], norm_mem[l], cross_wq[l], cross_wk[l], cross_wv[l],
                             cross_wo[l])
        xt = x.reshape(b * s, d)
        i = l // 2
        if l % 2 == 0:
            xt = _swiglu(xt, norm_ffn[l], ffn_w_gate[i], ffn_w_up[i], ffn_w_down[i])
        else:
            xt = _moe(xt, norm_ffn[l], moe_router[i], moe_w_gate[i], moe_w_up[i], moe_w_down[i])
        x = xt.reshape(b, s, d)
    return _final_norm(x.reshape(b * s, d), final_norm).reshape(b, s, d)
```

```python
import functools
import math

import numpy as np
import jax
import jax.numpy as jnp
from jax import lax
from jax.experimental import pallas as pl
from jax.experimental.pallas import tpu as pltpu

F32 = jnp.float32
BF16 = jnp.bfloat16

HEAD_DIM = 64
GROUP_HEADS = 4
GROUP_WIDTH = HEAD_DIM * GROUP_HEADS
ROPE_THETA = 10000.0
NORM_EPS = 1e-6
DIFF_DK = HEAD_DIM // 2
CMP_LEN = 32
CMP_STRIDE = 16
SEL_BLOCK = 64
SEL_SHIFT = 6
SEL_TOPN = 16
NSA_WINDOW = 512
FORCED_SCORE = 1e6
MLA_NOPE = 64
MLA_ROPE = 32
MLA_DV = 64
DILATED_PATTERNS = ((128, 1), (512, 4), (2048, 16))
CROSS_HEADS = 4
N_EXPERTS = 8
TOP_K = 2

LANES = 128
V7X_VMEM_BYTES = 64 * 1024 * 1024
VMEM_LIMIT = V7X_VMEM_BYTES * 7 // 8

ONES_ROWS = 16
NEG_BIG = -1e30
SEL_BIAS = -1e9
LOG2E = math.log2(math.e)
LN2 = math.log(2.0)


def _cparams(*sem):
    return pltpu.CompilerParams(dimension_semantics=sem, vmem_limit_bytes=VMEM_LIMIT)


def _rms(xf, gain):
    ms = jnp.mean(xf * xf, axis=-1, keepdims=True)
    return xf * lax.rsqrt(ms + NORM_EPS) * gain


def _mm_body(*refs, nx, norm, res, splits):
    x_refs = refs[:nx]
    pos = nx
    g_ref = refs[pos] if norm else None
    pos += int(norm)
    w_ref = refs[pos]
    pos += 1
    r_ref = refs[pos] if res else None
    pos += int(res)
    o_refs = refs[pos:pos + len(splits)]
    xs_ref = refs[pos + len(splits)]

    @pl.when(pl.program_id(1) == 0)
    def _():
        off = 0
        for xr in x_refs:
            xv = xr[...]
            if norm:
                xv = _rms(xv.astype(F32), g_ref[...])
            kw = xv.shape[-1]
            xs_ref[:, off:off + kw] = xv.astype(BF16)
            off += kw

    acc = jnp.dot(xs_ref[...], w_ref[...], preferred_element_type=F32)
    if res:
        acc = acc + r_ref[...]
    off = 0
    for o_ref, width in zip(o_refs, splits):
        o_ref[...] = acc[:, off:off + width].astype(o_ref.dtype)
        off += width


def _mm(xs, w, *, gain=None, res=None, tm=512, tn=None, splits=None, out_dtype=F32, name="mm"):
    t = xs[0].shape[0]
    k, n = w.shape
    assert sum(x.shape[1] for x in xs) == k
    tn = n if tn is None else tn
    splits = (tn,) if splits is None else tuple(splits)
    assert sum(splits) == tn and (len(splits) == 1 or tn == n)
    tm = min(tm, t)
    assert t % tm == 0 and n % tn == 0
    norm = gain is not None
    assert not norm or len(xs) == 1
    in_specs = [pl.BlockSpec((tm, x.shape[1]), lambda i, j: (i, 0)) for x in xs]
    args = list(xs)
    if norm:
        in_specs.append(pl.BlockSpec((1, k), lambda i, j: (0, 0)))
        args.append(gain.reshape(1, k).astype(F32))
    in_specs.append(pl.BlockSpec((k, tn), lambda i, j: (0, j)))
    args.append(w)
    if res is not None:
        in_specs.append(pl.BlockSpec((tm, tn), lambda i, j: (i, j)))
        args.append(res)
    if len(splits) == 1:
        out_shape = [jax.ShapeDtypeStruct((t, n), out_dtype)]
        out_specs = [pl.BlockSpec((tm, tn), lambda i, j: (i, j))]
    else:
        out_shape = [jax.ShapeDtypeStruct((t, s), out_dtype) for s in splits]
        out_specs = [pl.BlockSpec((tm, s), lambda i, j: (i, 0)) for s in splits]
    outs = pl.pallas_call(
        functools.partial(_mm_body, nx=len(xs), norm=norm, res=res is not None, splits=splits),
        grid=(t // tm, n // tn),
        in_specs=in_specs,
        out_specs=out_specs,
        out_shape=out_shape,
        scratch_shapes=[pltpu.VMEM((tm, k), BF16)],
        compiler_params=_cparams("parallel", "arbitrary"),
        name=name,
    )(*args)
    return outs[0] if len(outs) == 1 else outs


def _rope_tables(s, group):
    half = group // 2
    pos = jnp.arange(s, dtype=F32)
    inv_freq = ROPE_THETA ** (-jnp.arange(half, dtype=F32) / half)
    ang = pos[:, None] * inv_freq[None, :]
    cos, sin = jnp.cos(ang), jnp.sin(ang)
    zero = jnp.zeros_like(sin)
    reps = LANES // group
    c = jnp.tile(jnp.concatenate([cos, cos], -1), (1, reps))
    s1 = jnp.tile(jnp.concatenate([zero, sin], -1), (1, reps))
    s2 = jnp.tile(jnp.concatenate([-sin, zero], -1), (1, reps))
    return c, s1, s2


def _rope(x, tabs, half):
    w = x.shape[-1]
    c, s1, s2 = (jnp.tile(t, (1, w // LANES)) for t in tabs)
    return x * c + pltpu.roll(x, half, 1) * s1 + pltpu.roll(x, w - half, 1) * s2


def _prep_a(z, t32, q_ref, k_ref, vt_ref):
    qk = _rope(z[:, :512], t32, DIFF_DK // 2)
    lane = lax.broadcasted_iota(jnp.int32, (z.shape[0], HEAD_DIM), 1)
    scale = DIFF_DK ** -0.5 * LOG2E
    for h in range(GROUP_HEADS):
        q = qk[:, h * 64:(h + 1) * 64] * scale
        q_ref[0, h, 0] = jnp.where(lane < DIFF_DK, q, 0.0).astype(BF16)
        q_ref[0, h, 1] = jnp.where(lane >= DIFF_DK, q, 0.0).astype(BF16)
        k_ref[0, h] = qk[:, 256 + h * 64:256 + (h + 1) * 64].astype(BF16)
    vt_ref[0] = z[:, 512:768].T.reshape(vt_ref.shape[1:]).astype(BF16)


def _prep_b(z, t64, qn_ref, qr_ref, ka_ref, kw_ref, kc_ref, vc_ref, vs_ref, vw_ref, g_ref):
    ts = z.shape[0]
    scale = HEAD_DIM ** -0.5 * LOG2E
    rp = _rope(z[:, :384], t64, HEAD_DIM // 2)
    for h in range(GROUP_HEADS):
        qn_ref[0, h] = (z[:, h * 64:(h + 1) * 64] * scale).astype(BF16)
        qr_ref[0, h] = (rp[:, h * 64:(h + 1) * 64] * scale).astype(BF16)
    kpos = pl.program_id(0) * ts + lax.broadcasted_iota(jnp.int32, (ts, SEL_BLOCK), 0)
    lane = lax.broadcasted_iota(jnp.int32, (ts, SEL_BLOCK), 1)
    onehot = jnp.where((kpos >> SEL_SHIFT) == lane, 1.0, 0.0)
    ka_ref[0] = jnp.concatenate([rp[:, 256:320], onehot], axis=-1).astype(BF16)
    kw_ref[0] = rp[:, 320:384].astype(BF16)
    kc_ref[0] = z[:, 384:448].astype(BF16)
    vc_ref[0] = z[:, 448:512].astype(BF16)
    vt = z[:, 512:640].T.astype(BF16)
    vs_ref[0] = vt[:HEAD_DIM]
    vw_ref[0] = vt[HEAD_DIM:]
    g_ref[0] = jax.nn.sigmoid(z[:, 640:768])


def _prep_c(z, t32, qg_ref, kg_ref, wq_ref, wkv_ref, q_ref, k_ref, vt_ref):
    scale = (MLA_NOPE + MLA_ROPE) ** -0.5 * LOG2E
    cq = _rms(z[:, :384], qg_ref[...]).astype(BF16)
    ckv = _rms(z[:, 384:512], kg_ref[...]).astype(BF16)
    qc = jnp.dot(cq, wq_ref[...], preferred_element_type=F32)
    kvc = jnp.dot(ckv, wkv_ref[...], preferred_element_type=F32)
    half = MLA_ROPE // 2
    q_rope = _rope(qc[:, 256:384], t32, half)
    k_rope = _rope(z[:, 512:640], t32, half)[:, :MLA_ROPE]
    for h in range(GROUP_HEADS):
        q = jnp.concatenate([qc[:, h * 64:(h + 1) * 64], q_rope[:, h * 32:(h + 1) * 32]], axis=-1)
        q_ref[0, h] = (q * scale).astype(BF16)
        k_ref[0, h] = jnp.concatenate([kvc[:, h * 64:(h + 1) * 64], k_rope], axis=-1).astype(BF16)
    vt_ref[0] = kvc[:, 256:512].T.reshape(vt_ref.shape[1:]).astype(BF16)


def _prep_d(z, t64, q_ref, k_ref, vt_ref):
    scale = HEAD_DIM ** -0.5 * LOG2E
    qk = _rope(z[:, :512], t64, HEAD_DIM // 2)
    for h in range(GROUP_HEADS):
        q_ref[0, h] = (qk[:, h * 64:(h + 1) * 64] * scale).astype(BF16)
        k_ref[0, h] = qk[:, 256 + h * 64:256 + (h + 1) * 64].astype(BF16)
    vt_ref[0] = z[:, 512:768].T.reshape(vt_ref.shape[1:]).astype(BF16)


Z_SPLITS = (768, 768, 640, 768)
N_PREP_OUT = (3, 9, 3, 3)


def _project_prep_body(*refs):
    x_ref, g_ref, w_ref = refs[:3]
    t64 = tuple(r[...] for r in refs[3:6])
    t32 = tuple(r[...] for r in refs[6:9])
    qg_ref, kg_ref, wq_ref, wkv_ref = refs[9:13]
    outs = refs[13:]
    h = _rms(x_ref[0], g_ref[...]).astype(BF16)
    z = jnp.dot(h, w_ref[...], preferred_element_type=F32)
    o0, o1, o2, o3 = np.cumsum((0,) + Z_SPLITS[:3])
    n0, n1, n2 = np.cumsum(N_PREP_OUT[:3])
    _prep_a(z[:, o0:o0 + Z_SPLITS[0]], t32, *outs[:n0])
    _prep_b(z[:, o1:o1 + Z_SPLITS[1]], t64, *outs[n0:n1])
    _prep_c(z[:, o2:o2 + Z_SPLITS[2]], t32, qg_ref, kg_ref, wq_ref, wkv_ref, *outs[n1:n2])
    _prep_d(z[:, o3:o3 + Z_SPLITS[3]], t64, *outs[n2:])


def _project_prep(x, gain, w, tabs64, tabs32, q_gain, kv_gain, wq, wkv, ts):
    b, s, d = x.shape
    dk = MLA_NOPE + MLA_ROPE
    full = lambda a: pl.BlockSpec(a.shape, lambda i, bi: (0,) * a.ndim)
    tspec = pl.BlockSpec((ts, LANES), lambda i, bi: (i, 0))
    h4 = lambda dd: pl.BlockSpec((1, 4, ts, dd), lambda i, bi: (bi, 0, i, 0))
    h4t = pl.BlockSpec((1, 4, 64, ts), lambda i, bi: (bi, 0, 0, i))
    sq = lambda dd: pl.BlockSpec((1, ts, dd), lambda i, bi: (bi, i, 0))
    t64 = pl.BlockSpec((1, 64, ts), lambda i, bi: (bi, 0, i))
    sd = lambda *shape, dt=BF16: jax.ShapeDtypeStruct(shape, dt)
    out_specs = [pl.BlockSpec((1, 4, 2, ts, 64), lambda i, bi: (bi, 0, 0, i, 0)), h4(64), h4t,
                 h4(64), h4(64), sq(128), sq(64), sq(64), sq(64), t64, t64, sq(128),
                 h4(dk), h4(dk), h4t,
                 h4(64), h4(64), h4t]
    out_shape = [sd(b, 4, 2, s, 64), sd(b, 4, s, 64), sd(b, 4, 64, s),
                 sd(b, 4, s, 64), sd(b, 4, s, 64), sd(b, s, 128), sd(b, s, 64), sd(b, s, 64), sd(b, s, 64),
                 sd(b, 64, s), sd(b, 64, s), sd(b, s, 128, dt=F32),
                 sd(b, 4, s, dk), sd(b, 4, s, dk), sd(b, 4, 64, s),
                 sd(b, 4, s, 64), sd(b, 4, s, 64), sd(b, 4, 64, s)]
    g2 = gain.reshape(1, d).astype(F32)
    outs = pl.pallas_call(
        _project_prep_body,
        grid=(s // ts, b),
        in_specs=[pl.BlockSpec((1, ts, d), lambda i, bi: (bi, i, 0)), full(g2), full(w)] + [tspec] * 6
                 + [full(q_gain), full(kv_gain), full(wq), full(wkv)],
        out_specs=out_specs,
        out_shape=out_shape,
        compiler_params=_cparams("parallel", "parallel"),
        name="w_in_prep",
    )(x, g2, w, *tabs64, *tabs32, q_gain, kv_gain, wq, wkv)
    n0, n1, n2 = np.cumsum(N_PREP_OUT[:3])
    return outs[:n0], outs[n0:n1], outs[n1:n2], outs[n2:]


def _flash_schedule(nq, tq, tk, sk, window):
    qi, kj, fl = [], [], []
    for i in range(nq):
        q_lo, q_hi = i * tq, i * tq + tq - 1
        j_hi = min(q_hi, sk - 1) // tk
        j_lo = 0 if window is None else max(0, q_lo - window) // tk
        for j in range(j_lo, j_hi + 1):
            k_min, k_max = j * tk, j * tk + tk - 1
            full = k_max <= q_lo and (window is None or q_hi - k_min <= window)
            qi.append(i)
            kj.append(j)
            fl.append((1 if j == j_lo else 0) | (2 if j == j_hi else 0) | (0 if full else 4))
    return (np.asarray(qi, np.int32), np.asarray(kj, np.int32), np.asarray(fl, np.int32))


def _flash_body(qi_ref, kj_ref, fl_ref, q_ref, k_ref, vt_ref, o_ref, m_sc, acc_sc, *,
                gb, hq, tq, tk, window, with_lse, merge_groups):
    t = pl.program_id(1)
    fl = fl_ref[t]
    rows = hq * tq
    dv = vt_ref.shape[1]

    @pl.when((fl & 1) != 0)
    def _():
        m_sc[...] = jnp.full(m_sc.shape, NEG_BIG, F32)
        acc_sc[...] = jnp.zeros(acc_sc.shape, F32)

    def step(masked):
        q = q_ref[...].reshape(gb, rows, q_ref.shape[-1])
        st = jnp.einsum("gkd,gqd->gkq", k_ref[...], q, preferred_element_type=F32)
        if masked:
            krow = lax.broadcasted_iota(jnp.int32, (tk, rows), 0)
            qcol = lax.broadcasted_iota(jnp.int32, (tk, rows), 1)
            if hq > 1:
                qcol = qcol & (tq - 1)
            dist = (qi_ref[t] * tq - kj_ref[t] * tk) + qcol - krow
            ok = dist >= 0
            if window is not None:
                ok = ok & (dist <= window)
            st = jnp.where(ok[None], st, NEG_BIG)
        m_prev = m_sc[...]
        m_new = jnp.maximum(m_prev, jnp.max(st, axis=1, keepdims=True))
        alpha = jnp.exp2(m_prev - m_new)
        p = jnp.exp2((st - m_new).astype(BF16))
        v_ext = jnp.concatenate([vt_ref[...], jnp.ones((gb, ONES_ROWS, tk), BF16)], axis=1)
        acc_sc[...] = alpha * acc_sc[...] + jnp.einsum("gdk,gkq->gdq", v_ext, p, preferred_element_type=F32)
        m_sc[...] = m_new

    @pl.when((fl & 4) != 0)
    def _():
        step(True)

    @pl.when((fl & 4) == 0)
    def _():
        step(False)

    @pl.when((fl & 2) != 0)
    def _():
        l = acc_sc[:, dv:dv + 1, :]
        o_t = acc_sc[:, :dv, :] / l
        if merge_groups:
            for n in range(gb // merge_groups):
                heads = o_t[n * merge_groups:(n + 1) * merge_groups]
                o_ref[n] = heads.reshape(merge_groups * dv, rows).T.astype(o_ref.dtype)
            return
        if with_lse:
            extra = jnp.broadcast_to(m_sc[...] * LN2 + jnp.log(l), o_t.shape)
        else:
            extra = jnp.zeros((gb, LANES - dv, rows), F32)
        dvo = o_ref.shape[-1]
        for g in range(gb):
            o = jnp.concatenate([o_t[g], extra[g]], axis=0).T
            o_ref[g] = o[:, :dvo].reshape(o_ref.shape[1:]).astype(o_ref.dtype)


def _flash(q, k, vt, *, tq, tk, gb=1, window=None, with_lse=False, merge_groups=0, out_dtype=F32,
           name="flash"):
    g, hq, sq, dk = q.shape
    _, dv, sk = vt.shape
    tq, tk = min(tq, sq), min(tk, sk)
    assert g % gb == 0 and sq % tq == 0 and sk % tk == 0
    assert tq & (tq - 1) == 0 and 2 * dv == LANES
    qi, kj, fl = _flash_schedule(sq // tq, tq, tk, sk, window)
    dvo = 2 * dv if with_lse else dv
    rows = hq * tq
    if merge_groups:
        mh = merge_groups
        assert hq == 1 and not with_lse and (mh * dv) % LANES == 0 and gb % mh == 0
        out_spec = pl.BlockSpec((gb // mh, tq, mh * dv), lambda gi, t, qi, kj, fl: (gi, qi[t], 0))
        out_shape = jax.ShapeDtypeStruct((g // mh, sq, mh * dv), out_dtype)
    else:
        out_spec = pl.BlockSpec((gb, hq, tq, dvo), lambda gi, t, qi, kj, fl: (gi, 0, qi[t], 0))
        out_shape = jax.ShapeDtypeStruct((g, hq, sq, dvo), out_dtype)
    grid_spec = pltpu.PrefetchScalarGridSpec(
        num_scalar_prefetch=3,
        grid=(g // gb, len(qi)),
        in_specs=[pl.BlockSpec((gb, hq, tq, dk), lambda gi, t, qi, kj, fl: (gi, 0, qi[t], 0)),
                  pl.BlockSpec((gb, tk, dk), lambda gi, t, qi, kj, fl: (gi, kj[t], 0)),
                  pl.BlockSpec((gb, dv, tk), lambda gi, t, qi, kj, fl: (gi, 0, kj[t]))],
        out_specs=out_spec,
        scratch_shapes=[pltpu.VMEM((gb, 1, rows), F32), pltpu.VMEM((gb, dv + ONES_ROWS, rows), F32)],
    )
    return pl.pallas_call(
        functools.partial(_flash_body, gb=gb, hq=hq, tq=tq, tk=tk, window=window, with_lse=with_lse,
                          merge_groups=merge_groups),
        grid_spec=grid_spec,
        out_shape=out_shape,
        compiler_params=_cparams("parallel", "arbitrary"),
        name=name,
    )(jnp.asarray(qi), jnp.asarray(kj), jnp.asarray(fl), q, k, vt)


def _band_body(q_ref, kp_ref, kc_ref, vp_ref, vc_ref, o_ref, *, tq, window):
    i = pl.program_id(1)
    gb, dv = vp_ref.shape[0], vp_ref.shape[1]
    k2 = jnp.concatenate([kp_ref[...], kc_ref[...]], axis=1)
    st = jnp.einsum("gkd,gqd->gkq", k2, q_ref[...], preferred_element_type=F32)
    krow = lax.broadcasted_iota(jnp.int32, (2 * tq, tq), 0)
    qcol = lax.broadcasted_iota(jnp.int32, (2 * tq, tq), 1)
    dist = qcol + tq - krow
    ok = (dist >= 0) & (dist <= window) & ((krow >= tq) | (i > 0))
    st = jnp.where(ok[None], st, NEG_BIG)
    m = jnp.max(st, axis=1, keepdims=True)
    p = jnp.exp2((st - m).astype(BF16))
    v_ext = jnp.concatenate([jnp.concatenate([vp_ref[...], vc_ref[...]], axis=2),
                             jnp.ones((gb, ONES_ROWS, 2 * tq), BF16)], axis=1)
    acc = jnp.einsum("gdk,gkq->gdq", v_ext, p, preferred_element_type=F32)
    l = acc[:, dv:dv + 1, :]
    o_t = acc[:, :dv, :] / l
    lse = jnp.broadcast_to(m * LN2 + jnp.log(l), o_t.shape)
    for g in range(gb):
        o_ref[g] = jnp.concatenate([o_t[g], lse[g]], axis=0).T.astype(o_ref.dtype)


def _band_attention(q, k, vt, *, tq, gb, window, name):
    g, s, dk = q.shape
    dv = vt.shape[1]
    tq = min(tq, s)
    assert window <= tq and g % gb == 0 and s % tq == 0 and 2 * dv == LANES
    prev = lambda i: jnp.maximum(i - 1, 0)
    return pl.pallas_call(
        functools.partial(_band_body, tq=tq, window=window),
        grid=(g // gb, s // tq),
        in_specs=[pl.BlockSpec((gb, tq, dk), lambda gi, i: (gi, i, 0)),
                  pl.BlockSpec((gb, tq, dk), lambda gi, i: (gi, prev(i), 0)),
                  pl.BlockSpec((gb, tq, dk), lambda gi, i: (gi, i, 0)),
                  pl.BlockSpec((gb, dv, tq), lambda gi, i: (gi, 0, prev(i))),
                  pl.BlockSpec((gb, dv, tq), lambda gi, i: (gi, 0, i))],
        out_specs=pl.BlockSpec((gb, tq, 2 * dv), lambda gi, i: (gi, i, 0)),
        out_shape=jax.ShapeDtypeStruct((g, s, 2 * dv), F32),
        compiler_params=_cparams("parallel", "parallel"),
        name=name,
    )(q, k, k, vt, vt)


def _gelu_tanh(x):
    return 0.5 * x * (1.0 + jnp.tanh(math.sqrt(2.0 / math.pi) * (x + 0.044715 * (x * x * x))))


def _compress_body(r_ref, pos_ref, w1_ref, w2_ref, o_ref, ot_ref):
    half = CMP_STRIDE * HEAD_DIM
    r = r_ref[0, 0]
    w1 = w1_ref[0]
    top = jnp.dot(r, w1[:half], preferred_element_type=F32)
    bot = jnp.dot(r, w1[half:], preferred_element_type=F32)
    nc = r.shape[0]
    posb = jnp.dot(pos_ref[0], w1, preferred_element_type=F32)[:1]
    pre = top + pltpu.roll(bot, nc - 1, 0) + posb
    hid = _gelu_tanh(pre).astype(BF16)
    o = jnp.dot(hid, w2_ref[0], preferred_element_type=F32)
    o_ref[0, 0] = o.astype(o_ref.dtype)
    ot_ref[0, 0] = jnp.concatenate([o, jnp.zeros_like(o)], axis=1).T[:HEAD_DIM].astype(ot_ref.dtype)


def _nsa_compress(kv_rows, pos_flat, w1, w2):
    _, b, nc, _ = kv_rows.shape
    return pl.pallas_call(
        _compress_body,
        grid=(2, b),
        in_specs=[pl.BlockSpec((1, 1, nc, 1024), lambda w, bi: (w, bi, 0, 0)),
                  pl.BlockSpec((1, 8, 2048), lambda w, bi: (w, 0, 0)),
                  pl.BlockSpec((1, 2048, 256), lambda w, bi: (w, 0, 0)),
                  pl.BlockSpec((1, 256, 64), lambda w, bi: (w, 0, 0))],
        out_specs=[pl.BlockSpec((1, 1, nc, 64), lambda w, bi: (w, bi, 0, 0)),
                   pl.BlockSpec((1, 1, 64, nc), lambda w, bi: (w, bi, 0, 0))],
        out_shape=[jax.ShapeDtypeStruct((2, b, nc, 64), BF16), jax.ShapeDtypeStruct((2, b, 64, nc), BF16)],
        compiler_params=_cparams("parallel", "parallel"),
        name="nsa_compress",
    )(kv_rows, pos_flat, w1, w2)


def _cmp_select_body(qn_ref, qr_ref, kc_ref, vct_ref, covt_ref, o_ref, qa_ref, *, tq, n_sel, k_top):
    ncp = kc_ref.shape[2]
    qpos = pl.program_id(1) * tq + lax.broadcasted_iota(jnp.int32, (ncp, tq), 1)
    blk_end = lax.broadcasted_iota(jnp.int32, (ncp, tq), 0) * CMP_STRIDE + (CMP_LEN - 1)
    mask = blk_end <= qpos
    kc, vct = kc_ref[0, 0], vct_ref[0, 0]
    p_sum = jnp.zeros((ncp, tq), F32)
    outs = []
    for h in range(GROUP_HEADS):
        st = lax.dot_general(kc, qn_ref[0, h], (((1,), (1,)), ((), ())), preferred_element_type=F32)
        st = jnp.where(mask, st, NEG_BIG)
        mx = jnp.max(st, axis=0, keepdims=True)
        mx = jnp.where(mx > 0.5 * NEG_BIG, mx, 0.0)
        e = jnp.where(mask, jnp.exp2(st - mx), 0.0)
        p = e / jnp.maximum(jnp.sum(e, axis=0, keepdims=True), 1e-30)
        outs.append(jnp.dot(vct, p.astype(BF16), preferred_element_type=F32))
        p_sum = p_sum + p
    o_ref[0] = jnp.concatenate(outs, axis=0).T.astype(o_ref.dtype)
    p_hi = p_sum.astype(BF16)
    p_lo = (p_sum - p_hi.astype(F32)).astype(BF16)
    covt = covt_ref[...]
    imp_t = (jnp.dot(covt, p_hi, preferred_element_type=F32) + jnp.dot(covt, p_lo, preferred_element_type=F32))
    nblk = SEL_BLOCK
    tpos = pl.program_id(1) * tq + lax.broadcasted_iota(jnp.int32, (nblk, tq), 1)
    cur = tpos >> SEL_SHIFT
    j = lax.broadcasted_iota(jnp.int32, (nblk, tq), 0)
    valid = j <= cur
    forced = (j == 0) | (j == cur) | (j == cur - 1)
    score = jnp.where(valid, jnp.where(forced, FORCED_SCORE, imp_t), -jnp.inf)
    sub = 8
    groups = [score[g * sub:(g + 1) * sub] for g in range(nblk // sub)]
    ranks = [jnp.zeros((sub, tq), jnp.int32) for _ in groups]
    jj = lax.broadcasted_iota(jnp.int32, (sub, tq), 0)
    for i in range(n_sel):
        row = jnp.broadcast_to(score[i:i + 1], (sub, tq))
        for g, sg in enumerate(groups):
            if i < g * sub:
                ahead = row >= sg
            elif i >= (g + 1) * sub:
                ahead = row > sg
            else:
                ahead = (row > sg) | ((row == sg) & (i - g * sub < jj))
            ranks[g] = ranks[g] + ahead.astype(jnp.int32)
    rank = jnp.concatenate(ranks, axis=0)
    bias_t = jnp.where(valid & (rank < k_top), 0.0, SEL_BIAS)
    bias = jnp.concatenate([bias_t, jnp.zeros((LANES - nblk, tq), F32)], axis=0).T[:, :nblk].astype(BF16)
    for h in range(GROUP_HEADS):
        qa_ref[0, h] = jnp.concatenate([qr_ref[0, h], bias], axis=-1)


def _cover_matrix(ncp, n_sel, lanes):
    c_start = np.arange(ncp)[:, None] * CMP_STRIDE
    s_start = np.arange(lanes)[None, :] * SEL_BLOCK
    cov = np.clip(np.minimum(c_start + CMP_LEN, s_start + SEL_BLOCK) - np.maximum(c_start, s_start), 0, None)
    cov = cov.astype(np.float32) / CMP_LEN
    cov[:, n_sel:] = 0.0
    cov[ncp - 1:, :] = 0.0
    return cov


def _cmp_select(qn, qr, kvc, kvct, tq):
    b, _, s, _ = qn.shape
    ncp = kvc.shape[2]
    n_sel = s // SEL_BLOCK
    assert n_sel <= SEL_BLOCK
    k_top = min(SEL_TOPN, n_sel)
    covt = jnp.asarray(_cover_matrix(ncp, n_sel, SEL_BLOCK).T, BF16)
    h4 = lambda d: pl.BlockSpec((1, 4, tq, d), lambda bi, i: (bi, 0, i, 0))
    return pl.pallas_call(
        functools.partial(_cmp_select_body, tq=tq, n_sel=n_sel, k_top=k_top),
        grid=(b, s // tq),
        in_specs=[h4(64), h4(64),
                  pl.BlockSpec((1, 1, ncp, 64), lambda bi, i: (0, bi, 0, 0)),
                  pl.BlockSpec((1, 1, 64, ncp), lambda bi, i: (1, bi, 0, 0)),
                  pl.BlockSpec(covt.shape, lambda bi, i: (0, 0))],
        out_specs=[pl.BlockSpec((1, tq, 256), lambda bi, i: (bi, i, 0)), h4(128)],
        out_shape=[jax.ShapeDtypeStruct((b, s, 256), BF16), jax.ShapeDtypeStruct((b, 4, s, 128), BF16)],
        compiler_params=_cparams("parallel", "parallel"),
        name="nsa_cmp_select",
    )(qn, qr, kvc, kvct, covt)


def _merge_a(o_ref, lam_ref, g_ref, out_scale):
    outs = []
    for h in range(GROUP_HEADS):
        o = o_ref[0, h, 0].astype(F32) - lam_ref[...] * o_ref[0, h, 1].astype(F32)
        outs.append(_rms(o, g_ref[...]) * out_scale)
    return outs


def _merge_b(oc_ref, os_ref, ow_ref, g_ref):
    g = g_ref[0]
    outs = []
    for h in range(GROUP_HEADS):
        outs.append(g[:, 3 * h:3 * h + 1] * oc_ref[0, :, h * 64:(h + 1) * 64].astype(F32)
                    + g[:, 3 * h + 1:3 * h + 2] * os_ref[0, h].astype(F32)
                    + g[:, 3 * h + 2:3 * h + 3] * ow_ref[0, h].astype(F32))
    return outs


def _merge_d(o1_ref, o2_ref, o3_ref):
    outs = []
    for h in range(GROUP_HEADS):
        e = [r[0, h] for r in (o1_ref, o2_ref, o3_ref)]
        lse = [x[:, 64:] for x in e]
        mx = jnp.maximum(jnp.maximum(lse[0], lse[1]), lse[2])
        w = [jnp.exp(x - mx) for x in lse]
        den = w[0] + w[1] + w[2]
        outs.append((w[0] * e[0][:, :64] + w[1] * e[1][:, :64] + w[2] * e[2][:, :64]) / den)
    return outs


def _out_proj_body(x_ref, oa_ref, lam_ref, sub_ref, oc_ref, os_ref, ow_ref, g_ref, yc_ref, d1_ref, d2_ref, d3_ref,
                   w_ref, o_ref, *, out_scale):
    parts = (_merge_a(oa_ref, lam_ref, sub_ref, out_scale) + _merge_b(oc_ref, os_ref, ow_ref, g_ref)
             + [yc_ref[0].astype(F32)] + _merge_d(d1_ref, d2_ref, d3_ref))
    y = jnp.concatenate(parts, axis=-1).astype(BF16)
    o_ref[0] = x_ref[0] + jnp.dot(y, w_ref[...], preferred_element_type=F32)


def _out_proj(x, oa, lam, subln, out_scale, o_cmp, o_sel, o_win, gates, y_c, o_d, w, ts):
    b, s, d = x.shape
    row = lambda dd: pl.BlockSpec((1, ts, dd), lambda bi, i: (bi, i, 0))
    h4 = lambda dd: pl.BlockSpec((1, 4, ts, dd), lambda bi, i: (bi, 0, i, 0))
    vec = pl.BlockSpec((1, 64), lambda bi, i: (0, 0))
    return pl.pallas_call(
        functools.partial(_out_proj_body, out_scale=out_scale),
        grid=(b, s // ts),
        in_specs=[row(d), pl.BlockSpec((1, 4, 2, ts, 64), lambda bi, i: (bi, 0, 0, i, 0)), vec, vec,
                  row(256), h4(64), h4(64), row(128), row(256), h4(128), h4(128), h4(128),
                  pl.BlockSpec(w.shape, lambda bi, i: (0, 0))],
        out_specs=row(d),
        out_shape=jax.ShapeDtypeStruct((b, s, d), F32),
        compiler_params=_cparams("parallel", "parallel"),
        name="w_out_merge",
    )(x, oa, lam, subln, o_cmp, o_sel, o_win, gates, y_c, *o_d, w)


def _permute_w_in(w_in):
    a_cols = 768
    b0 = a_cols
    nq = 256
    seg = lambda i: w_in[:, b0 + nq + i * 64: b0 + nq + (i + 1) * 64]
    kc, vc, ks, vs, kw, vw = (seg(i) for i in range(6))
    gb = w_in[:, b0 + nq + 384: b0 + nq + 384 + 12]
    d = w_in.shape[0]
    w_b = jnp.concatenate([w_in[:, b0:b0 + nq], ks, kw, kc, vc, vs, vw, gb, jnp.zeros((d, 116), w_in.dtype)], axis=1)
    c0 = b0 + 652
    w_c = jnp.concatenate([w_in[:, c0:c0 + 544], jnp.zeros((d, 96), w_in.dtype)], axis=1)
    d0 = c0 + 544
    return jnp.concatenate([w_in[:, :a_cols], w_b, w_c, w_in[:, d0:d0 + 768]], axis=1).astype(BF16)


def _residue_classes(t, dil):
    g, s, d = t.shape
    return t.reshape(g, s // dil, dil, d).transpose(0, 2, 1, 3).reshape(g * dil, s // dil, d)


def _residue_classes_t(t, dil):
    g, d, s = t.shape
    return t.reshape(g, d, s // dil, dil).transpose(0, 3, 1, 2).reshape(g * dil, d, s // dil)


def _from_residue_classes(t, dil):
    gd, l, d = t.shape
    return t.reshape(gd // dil, dil, l, d).transpose(0, 2, 1, 3).reshape(gd // dil, l * dil, d)


def _token_mixers(x, layer, rope_tabs, norm_g, w_in, w_out, diff_lambda, diff_subln, cmp_pos, cmp_w1, cmp_w2,
                  q_norm, kv_norm, w_uq, w_ukv):
    b, s, d = x.shape
    t = b * s
    ts = min(512, s)
    wq = w_uq.reshape(-1, 4, MLA_NOPE + MLA_ROPE)
    wq = jnp.concatenate([wq[:, :, :MLA_NOPE].reshape(-1, 256), wq[:, :, MLA_NOPE:].reshape(-1, 128)], axis=1)
    wkv = w_ukv.reshape(-1, 4, MLA_NOPE + MLA_DV)
    wkv = jnp.concatenate([wkv[:, :, :MLA_NOPE].reshape(-1, 256), wkv[:, :, MLA_NOPE:].reshape(-1, 256)], axis=1)
    prep_a, prep_b, prep_c, prep_d = _project_prep(
        x, norm_g, _permute_w_in(w_in), rope_tabs[64], rope_tabs[32], q_norm.reshape(1, -1).astype(F32),
        kv_norm.reshape(1, -1).astype(F32), wq.astype(BF16), wkv.astype(BF16), ts)

    qa, ka, va = prep_a
    oa = _flash(qa.reshape(b * 4, 2, s, 64), ka.reshape(b * 4, s, 64), va.reshape(b * 4, 64, s),
                tq=512, tk=512, gb=math.gcd(b * 4, 8), out_dtype=BF16, name="flash_diff").reshape(b, 4, 2, s, 64)
    lam_init = 0.8 - 0.6 * math.exp(-0.3 * layer)
    lp = diff_lambda.astype(F32)
    lam = jnp.exp(jnp.sum(lp[0] * lp[1])) - jnp.exp(jnp.sum(lp[2] * lp[3])) + lam_init

    qn, qr, k_aug, kw, kc, vc, vs, vw, gates = prep_b
    kv_rows = jnp.stack([kc, vc]).reshape(2, b, s // CMP_STRIDE, CMP_STRIDE * HEAD_DIM)
    pos_flat = jnp.broadcast_to(cmp_pos.reshape(2, 1, CMP_LEN * HEAD_DIM), (2, 8, CMP_LEN * HEAD_DIM)).astype(BF16)
    kvc, kvct = _nsa_compress(kv_rows, pos_flat, cmp_w1.astype(BF16), cmp_w2.astype(BF16))
    o_cmp, q_aug = _cmp_select(qn, qr, kvc, kvct, min(256, s))
    gb_b = math.gcd(b, 4)
    o_sel = _flash(q_aug, k_aug, vs, tq=256, tk=512, gb=gb_b, out_dtype=BF16, name="flash_sel")
    o_win = _flash(qr, kw, vw, tq=256, tk=256, gb=gb_b, window=NSA_WINDOW - 1, out_dtype=BF16, name="flash_win")

    qc, kcat, vcv = prep_c
    dk = MLA_NOPE + MLA_ROPE
    y_c = _flash(qc.reshape(b * 4, 1, s, dk), kcat.reshape(b * 4, s, dk), vcv.reshape(b * 4, 64, s),
                 tq=512, tk=512, gb=math.gcd(b * 4, 16), merge_groups=4, out_dtype=BF16, name="flash_mla")

    qd, kd, vd = prep_d
    qd, kd, vd = qd.reshape(b * 4, s, 64), kd.reshape(b * 4, s, 64), vd.reshape(b * 4, 64, s)
    o_d = []
    for window, dil in DILATED_PATTERNS:
        l = s // dil
        qq, kk, vv = _residue_classes(qd, dil), _residue_classes(kd, dil), _residue_classes_t(vd, dil)
        g = qq.shape[0]
        gb = math.gcd(g, 32)
        o = _band_attention(qq, kk, vv, tq=128, gb=gb, window=window // dil, name=f"band_dil{dil}")
        o_d.append(_from_residue_classes(o, dil).reshape(b, 4, s, 128))
    return _out_proj(x, oa, jnp.full((1, 64), lam, F32), diff_subln.reshape(1, 64).astype(F32), 1.0 - lam_init,
                     o_cmp, o_sel, o_win, gates, y_c, o_d, w_out.astype(BF16), ts)


def _cross_body(x_ref, g_ref, wq_ref, k_ref, v_ref, wo_ref, o_ref):
    x = x_ref[0]
    h = _rms(x, g_ref[...]).astype(BF16)
    q = jnp.dot(h, wq_ref[...], preferred_element_type=F32).astype(BF16)
    dh = q.shape[-1] // CROSS_HEADS
    outs = []
    for hd in range(CROSS_HEADS):
        sl = slice(hd * dh, (hd + 1) * dh)
        s = lax.dot_general(q[:, sl], k_ref[0, :, sl], (((1,), (1,)), ((), ())), preferred_element_type=F32)
        e = jnp.exp(s - jnp.max(s, axis=-1, keepdims=True))
        p = e / jnp.sum(e, axis=-1, keepdims=True)
        outs.append(jnp.dot(p.astype(BF16), v_ref[0, :, sl], preferred_element_type=F32))
    o = jnp.concatenate(outs, axis=-1).astype(BF16)
    o_ref[0] = x + jnp.dot(o, wo_ref[...], preferred_element_type=F32)


def _cross_attention(x, mem, g_x, g_mem, wq, wk, wv, wo):
    b, s, d = x.shape
    m = mem.shape[1]
    ts = min(512, s)
    dh = d // CROSS_HEADS
    k, v = _mm([mem.reshape(b * m, d)], jnp.concatenate([wk, wv], axis=1).astype(BF16), gain=g_mem,
               tm=min(512, b * m), splits=(d, d), out_dtype=BF16, name="cross_kv")
    full = lambda a: pl.BlockSpec(a.shape, lambda bi, i: (0,) * a.ndim)
    wq_s = (wq * dh ** -0.5).astype(BF16)
    wo_b = wo.astype(BF16)
    g2 = g_x.reshape(1, d).astype(F32)
    return pl.pallas_call(
        _cross_body,
        grid=(b, s // ts),
        in_specs=[pl.BlockSpec((1, ts, d), lambda bi, i: (bi, i, 0)), full(g2), full(wq_s),
                  pl.BlockSpec((1, m, d), lambda bi, i: (bi, 0, 0)),
                  pl.BlockSpec((1, m, d), lambda bi, i: (bi, 0, 0)), full(wo_b)],
        out_specs=pl.BlockSpec((1, ts, d), lambda bi, i: (bi, i, 0)),
        out_shape=jax.ShapeDtypeStruct((b, s, d), F32),
        compiler_params=_cparams("parallel", "parallel"),
        name="cross_attn",
    )(x, g2, wq_s, k.reshape(b, m, d), v.reshape(b, m, d), wo_b)


def _swiglu_body(x_ref, g_ref, wg_ref, wu_ref, wd_ref, o_ref, h_sc, acc_sc):
    f = pl.program_id(1)

    @pl.when(f == 0)
    def _():
        h_sc[...] = _rms(x_ref[...], g_ref[...]).astype(BF16)
        acc_sc[...] = x_ref[...]

    h = h_sc[...]
    gate = jnp.dot(h, wg_ref[...], preferred_element_type=F32)
    up = jnp.dot(h, wu_ref[...], preferred_element_type=F32)
    act = (gate * jax.nn.sigmoid(gate) * up).astype(BF16)
    acc_sc[...] += jnp.dot(act, wd_ref[...], preferred_element_type=F32)

    @pl.when(f == pl.num_programs(1) - 1)
    def _():
        o_ref[...] = acc_sc[...]


def _swiglu(x, gain, wg, wu, wd, tm=512, tf=1408):
    t, d = x.shape
    ff = wg.shape[1]
    tm, tf = min(tm, t), min(tf, ff)
    assert t % tm == 0 and ff % tf == 0
    return pl.pallas_call(
        _swiglu_body,
        grid=(t // tm, ff // tf),
        in_specs=[pl.BlockSpec((tm, d), lambda i, f: (i, 0)),
                  pl.BlockSpec((1, d), lambda i, f: (0, 0)),
                  pl.BlockSpec((d, tf), lambda i, f: (0, f)),
                  pl.BlockSpec((d, tf), lambda i, f: (0, f)),
                  pl.BlockSpec((tf, d), lambda i, f: (f, 0))],
        out_specs=pl.BlockSpec((tm, d), lambda i, f: (i, 0)),
        out_shape=jax.ShapeDtypeStruct((t, d), F32),
        scratch_shapes=[pltpu.VMEM((tm, d), BF16), pltpu.VMEM((tm, d), F32)],
        compiler_params=_cparams("parallel", "arbitrary"),
        name="swiglu",
    )(x, gain.reshape(1, d).astype(F32), wg.astype(BF16), wu.astype(BF16), wd.astype(BF16))


def _router_body(x_ref, g_ref, r_ref, h_ref, rt_ref):
    h = _rms(x_ref[...], g_ref[...])
    h_ref[...] = h
    h_hi = h.astype(BF16)
    h_lo = (h - h_hi.astype(F32)).astype(BF16)
    r = r_ref[...]
    r_hi = r.astype(BF16)
    r_lo = (r - r_hi.astype(F32)).astype(BF16)
    logits = (jnp.dot(h_hi, r_hi, preferred_element_type=F32) + jnp.dot(h_lo, r_hi, preferred_element_type=F32)
              + jnp.dot(h_hi, r_lo, preferred_element_type=F32))
    lane = lax.broadcasted_iota(jnp.int32, logits.shape, 1)
    lg = jnp.where(lane < N_EXPERTS, logits, -jnp.inf)
    v0 = jnp.max(lg, axis=-1, keepdims=True)
    e0 = jnp.min(jnp.where(lg == v0, lane, LANES), axis=-1, keepdims=True)
    lg1 = jnp.where(lane == e0, -jnp.inf, lg)
    v1 = jnp.max(lg1, axis=-1, keepdims=True)
    e1 = jnp.min(jnp.where(lg1 == v1, lane, LANES), axis=-1, keepdims=True)
    g1 = 1.0 / (1.0 + jnp.exp(v0 - v1))
    g0 = 1.0 - g1
    out = jnp.where(lane == 0, g0, jnp.where(lane == 1, g1, 0.0))
    out = jnp.where(lane == 2, e0.astype(F32), jnp.where(lane == 3, e1.astype(F32), out))
    rt_ref[...] = out


def _router(x, gain, router, tm=512):
    t, d = x.shape
    tm = min(tm, t)
    r_pad = jnp.zeros((d, LANES), F32).at[:, :N_EXPERTS].set(router.astype(F32))
    return pl.pallas_call(
        _router_body,
        grid=(t // tm,),
        in_specs=[pl.BlockSpec((tm, d), lambda i: (i, 0)), pl.BlockSpec((1, d), lambda i: (0, 0)),
                  pl.BlockSpec((d, LANES), lambda i: (0, 0))],
        out_specs=[pl.BlockSpec((tm, d), lambda i: (i, 0)), pl.BlockSpec((tm, LANES), lambda i: (i, 0))],
        out_shape=[jax.ShapeDtypeStruct((t, d), F32), jax.ShapeDtypeStruct((t, LANES), F32)],
        compiler_params=_cparams("parallel"),
        name="moe_router",
    )(x, gain.reshape(1, d).astype(F32), r_pad)


def _row_copy(src_hbm, src_row, dst_ref, dst_row, sem):
    return pltpu.make_async_copy(src_hbm.at[pl.ds(src_row, 1)], dst_ref.at[pl.ds(dst_row, 1)], sem)


def _rank_body(rt_ref, u_ref, ps_ref, dest_ref, carry_sc):
    k, i = pl.program_id(0), pl.program_id(1)
    tm = rt_ref.shape[0]

    @pl.when((k == 0) & (i == 0))
    def _():
        carry_sc[...] = jnp.zeros(carry_sc.shape, F32)

    rt_t = rt_ref[...].T
    e_row = jnp.where(k == 0, rt_t[2:3], rt_t[3:4]).astype(jnp.int32)
    sub = lax.broadcasted_iota(jnp.int32, (N_EXPERTS, tm), 0)
    onehot = sub == e_row
    ahead = jnp.dot(onehot.astype(BF16), u_ref[...], preferred_element_type=F32)
    row = ahead + carry_sc[:, :1] + ps_ref[:, :1]
    dest = jnp.sum(jnp.where(onehot, row, 0.0), axis=0, keepdims=True)
    dest_ref[...] = dest.astype(jnp.int32).reshape(dest_ref.shape)
    carry_sc[...] += jnp.sum(onehot.astype(F32), axis=1, keepdims=True)


def _assignment_rows(routing, pstart, tm=512):
    t = routing.shape[0]
    tm = min(tm, t)
    upper = jnp.asarray(np.triu(np.ones((tm, tm), np.float32), 1), BF16)
    ps = jnp.broadcast_to(pstart.astype(F32)[:, None], (N_EXPERTS, LANES))
    out = pl.pallas_call(
        _rank_body,
        grid=(TOP_K, t // tm),
        in_specs=[pl.BlockSpec((tm, LANES), lambda k, i: (i, 0)),
                  pl.BlockSpec((tm, tm), lambda k, i: (0, 0)),
                  pl.BlockSpec((N_EXPERTS, LANES), lambda k, i: (0, 0))],
        out_specs=pl.BlockSpec((1, 1, 1, tm), lambda k, i: (k, i, 0, 0)),
        out_shape=jax.ShapeDtypeStruct((TOP_K, t // tm, 1, tm), jnp.int32),
        scratch_shapes=[pltpu.VMEM((N_EXPERTS, LANES), F32)],
        compiler_params=_cparams("arbitrary", "arbitrary"),
        name="moe_rank",
    )(routing, upper, ps)
    return out.reshape(TOP_K * t)


def _dispatch_body(dest_ref, h_ref, xs_in, xs_hbm, sem):
    del xs_in
    tm = h_ref.shape[0]
    t = dest_ref.shape[0] // TOP_K
    base = pl.program_id(0) * tm

    def copy(r, k, dst_row):
        return pltpu.make_async_copy(h_ref.at[pl.ds(r, 1)], xs_hbm.at[pl.ds(dst_row, 1)], sem)

    def start(r, c):
        for k in range(TOP_K):
            copy(r, k, dest_ref[k * t + base + r]).start()
        return c

    lax.fori_loop(0, tm, start, 0, unroll=8)
    for r in range(tm):
        for k in range(TOP_K):
            copy(r, k, 0).wait()


def _dispatch(h, dest, n_rows, tm=256):
    t, d = h.shape
    tm = min(tm, t)
    grid_spec = pltpu.PrefetchScalarGridSpec(
        num_scalar_prefetch=1,
        grid=(t // tm,),
        in_specs=[pl.BlockSpec((tm, d), lambda i, dest: (i, 0)), pl.BlockSpec(memory_space=pl.ANY)],
        out_specs=pl.BlockSpec(memory_space=pl.ANY),
        scratch_shapes=[pltpu.SemaphoreType.DMA(())],
    )
    return pl.pallas_call(
        _dispatch_body,
        grid_spec=grid_spec,
        out_shape=jax.ShapeDtypeStruct((n_rows, d), F32),
        input_output_aliases={2: 0},
        compiler_params=_cparams("arbitrary"),
        name="moe_dispatch",
    )(dest, h, jnp.zeros((n_rows, d), F32))


def _experts_body(be_ref, nb_ref, x_ref, wg_ref, wu_ref, wd_ref, o_ref, xb_sc, acc_sc):
    i, f = pl.program_id(0), pl.program_id(1)

    @pl.when(i < nb_ref[0])
    def _():
        @pl.when(f == 0)
        def _():
            xb_sc[...] = x_ref[...].astype(BF16)
            acc_sc[...] = jnp.zeros(acc_sc.shape, F32)

        x = xb_sc[...]
        gate = jnp.dot(x, wg_ref[0, 0], preferred_element_type=F32)
        up = jnp.dot(x, wu_ref[0, 0], preferred_element_type=F32)
        act = (gate * jax.nn.sigmoid(gate) * up).astype(BF16)
        acc_sc[...] += jnp.dot(act, wd_ref[0, 0], preferred_element_type=F32)

        @pl.when(f == pl.num_programs(1) - 1)
        def _():
            o_ref[...] = acc_sc[...]

    @pl.when(i >= nb_ref[0])
    def _():
        o_ref[...] = jnp.zeros(o_ref.shape, F32)


def _experts(xs, blk_e, n_used, wg, wu, wd, layer, rows, tf=1792):
    n_rows, d = xs.shape
    ff = wg.shape[3]
    n_blk = n_rows // rows
    last = ff // tf - 1
    grid_spec = pltpu.PrefetchScalarGridSpec(
        num_scalar_prefetch=2,
        grid=(n_blk, ff // tf),
        in_specs=[pl.BlockSpec((rows, d), lambda i, f, be, nb: (jnp.minimum(i, nb[0] - 1), 0)),
                  pl.BlockSpec((1, 1, d, tf), lambda i, f, be, nb: (layer, be[i], 0, jnp.where(i < nb[0], f, last))),
                  pl.BlockSpec((1, 1, d, tf), lambda i, f, be, nb: (layer, be[i], 0, jnp.where(i < nb[0], f, last))),
                  pl.BlockSpec((1, 1, tf, d), lambda i, f, be, nb: (layer, be[i], jnp.where(i < nb[0], f, last), 0))],
        out_specs=pl.BlockSpec((rows, d), lambda i, f, be, nb: (i, 0)),
        scratch_shapes=[pltpu.VMEM((rows, d), BF16), pltpu.VMEM((rows, d), F32)],
    )
    return pl.pallas_call(
        _experts_body,
        grid_spec=grid_spec,
        out_shape=jax.ShapeDtypeStruct((n_rows, d), F32),
        compiler_params=_cparams("arbitrary", "arbitrary"),
        name="moe_experts",
    )(blk_e, n_used, xs, wg, wu, wd)


def _combine_body(dest_ref, x_ref, rt_ref, ys_hbm, o_ref, buf, sem):
    tm = x_ref.shape[0]
    n_tok = dest_ref.shape[0] // TOP_K
    i = pl.program_id(0)
    slot = i & 1

    def start(tile, s):
        def body(r, c):
            for k in range(TOP_K):
                _row_copy(ys_hbm, dest_ref[k * n_tok + tile * tm + r], buf.at[s, k], r, sem.at[s]).start()
            return c
        lax.fori_loop(0, tm, body, 0, unroll=8)

    @pl.when(i == 0)
    def _():
        start(0, 0)

    @pl.when(i + 1 < pl.num_programs(0))
    def _():
        start(i + 1, 1 - slot)

    for r in range(tm):
        for k in range(TOP_K):
            _row_copy(ys_hbm, 0, buf.at[slot, k], r, sem.at[slot]).wait()
    rt = rt_ref[...]
    o_ref[...] = x_ref[...] + rt[:, 0:1] * buf[slot, 0] + rt[:, 1:2] * buf[slot, 1]


def _combine(x, routing, ys, dest, tm=256):
    t, d = x.shape
    tm = min(tm, t)
    grid_spec = pltpu.PrefetchScalarGridSpec(
        num_scalar_prefetch=1,
        grid=(t // tm,),
        in_specs=[pl.BlockSpec((tm, d), lambda i, dest: (i, 0)),
                  pl.BlockSpec((tm, LANES), lambda i, dest: (i, 0)),
                  pl.BlockSpec(memory_space=pl.ANY)],
        out_specs=pl.BlockSpec((tm, d), lambda i, dest: (i, 0)),
        scratch_shapes=[pltpu.VMEM((2, TOP_K, tm, d), F32), pltpu.SemaphoreType.DMA((2,))],
    )
    return pl.pallas_call(
        _combine_body,
        grid_spec=grid_spec,
        out_shape=jax.ShapeDtypeStruct((t, d), F32),
        compiler_params=_cparams("arbitrary"),
        name="moe_combine",
    )(dest, x, routing, ys)


MOE_ROWS = 512


def _moe(x, gain, router, w_gate, w_up, w_down, layer):
    t, d = x.shape
    h, routing = _router(x, gain, router)
    n_assign = t * TOP_K
    experts = routing[:, 2:2 + TOP_K].astype(jnp.int32)
    counts = jnp.sum((experts[:, :, None] == jnp.arange(N_EXPERTS)[None, None, :]).astype(jnp.int32), axis=(0, 1))
    padded = (counts + MOE_ROWS - 1) // MOE_ROWS * MOE_ROWS
    pend = jnp.cumsum(padded)
    pstart = pend - padded
    n_rows = -(-n_assign // MOE_ROWS) * MOE_ROWS + N_EXPERTS * MOE_ROWS
    n_blk = n_rows // MOE_ROWS
    blk_start = jnp.arange(n_blk, dtype=jnp.int32) * MOE_ROWS
    blk_e = jnp.minimum(jnp.sum((pend[None, :] <= blk_start[:, None]).astype(jnp.int32), axis=1), N_EXPERTS - 1)
    n_used = (pend[-1] // MOE_ROWS).astype(jnp.int32).reshape(1)
    dest = _assignment_rows(routing, pstart)
    xs = _dispatch(h, dest, n_rows)
    ys = _experts(xs, blk_e.astype(jnp.int32), n_used, w_gate, w_up, w_down, layer, MOE_ROWS)
    return _combine(x, routing, ys, dest)


def _final_norm_body(x_ref, g_ref, o_ref):
    o_ref[...] = _rms(x_ref[...], g_ref[...])


def _final_norm(x, gain, tm=1024):
    t, d = x.shape
    tm = min(tm, t)
    return pl.pallas_call(
        _final_norm_body,
        grid=(t // tm,),
        in_specs=[pl.BlockSpec((tm, d), lambda i: (i, 0)), pl.BlockSpec((1, d), lambda i: (0, 0))],
        out_specs=pl.BlockSpec((tm, d), lambda i: (i, 0)),
        out_shape=jax.ShapeDtypeStruct((t, d), F32),
        compiler_params=_cparams("parallel"),
        name="final_norm",
    )(x, gain.reshape(1, d).astype(F32))


def _all_rope_tables(s):
    return {64: _rope_tables(s, 64), 32: _rope_tables(s, 32)}


def kernel(x, mem, norm_mix, w_in, w_out, diff_lambda, diff_subln, nsa_cmp_pos, nsa_cmp_w1, nsa_cmp_w2, mla_q_norm, mla_kv_norm, mla_w_uq, mla_w_ukv, norm_cross, norm_mem, cross_wq, cross_wk, cross_wv, cross_wo, norm_ffn, ffn_w_gate, ffn_w_up, ffn_w_down, moe_router, moe_w_gate, moe_w_up, moe_w_down, final_norm):
    b, s, d = x.shape
    depth = w_in.shape[0]
    tabs = _all_rope_tables(s)
    moe_wg, moe_wu, moe_wd = moe_w_gate.astype(BF16), moe_w_up.astype(BF16), moe_w_down.astype(BF16)
    for l in range(depth):
        x = _token_mixers(x, l, tabs, norm_mix[l], w_in[l], w_out[l], diff_lambda[l], diff_subln[l],
                          nsa_cmp_pos[l], nsa_cmp_w1[l], nsa_cmp_w2[l], mla_q_norm[l], mla_kv_norm[l],
                          mla_w_uq[l], mla_w_ukv[l])
        x = _cross_attention(x, mem, norm_cross[l], norm_mem[l], cross_wq[l], cross_wk[l], cross_wv[l],
                             cross_wo[l])
        xt = x.reshape(b * s, d)
        i = l // 2
        if l % 2 == 0:
            xt = _swiglu(xt, norm_ffn[l], ffn_w_gate[i], ffn_w_up[i], ffn_w_down[i])
        else:
            xt = _moe(xt, norm_ffn[l], moe_router[i], moe_wg, moe_wu, moe_wd, i)
        x = xt.reshape(b, s, d)
    return _final_norm(x.reshape(b * s, d), final_norm).reshape(b, s, d)
```

```python
import functools
import math

import numpy as np
import jax
import jax.numpy as jnp
from jax import lax
from jax.experimental import pallas as pl
from jax.experimental.pallas import tpu as pltpu

F32 = jnp.float32
BF16 = jnp.bfloat16

HEAD_DIM = 64
GROUP_HEADS = 4
GROUP_WIDTH = HEAD_DIM * GROUP_HEADS
ROPE_THETA = 10000.0
NORM_EPS = 1e-6
DIFF_DK = HEAD_DIM // 2
CMP_LEN = 32
CMP_STRIDE = 16
SEL_BLOCK = 64
SEL_SHIFT = 6
SEL_TOPN = 16
NSA_WINDOW = 512
FORCED_SCORE = 1e6
MLA_NOPE = 64
MLA_ROPE = 32
MLA_DV = 64
DILATED_PATTERNS = ((128, 1), (512, 4), (2048, 16))
CROSS_HEADS = 4
N_EXPERTS = 8
TOP_K = 2

LANES = 128
V7X_VMEM_BYTES = 64 * 1024 * 1024
VMEM_LIMIT = V7X_VMEM_BYTES * 7 // 8

ONES_ROWS = 16
NEG_BIG = -1e30
SEL_BIAS = -1e9
LOG2E = math.log2(math.e)
LN2 = math.log(2.0)


def _cparams(*sem):
    return pltpu.CompilerParams(dimension_semantics=sem, vmem_limit_bytes=VMEM_LIMIT)


def _rms(xf, gain):
    ms = jnp.mean(xf * xf, axis=-1, keepdims=True)
    return xf * lax.rsqrt(ms + NORM_EPS) * gain


def _mm_body(*refs, nx, norm, res, splits):
    x_refs = refs[:nx]
    pos = nx
    g_ref = refs[pos] if norm else None
    pos += int(norm)
    w_ref = refs[pos]
    pos += 1
    r_ref = refs[pos] if res else None
    pos += int(res)
    o_refs = refs[pos:pos + len(splits)]
    xs_ref = refs[pos + len(splits)]

    @pl.when(pl.program_id(1) == 0)
    def _():
        off = 0
        for xr in x_refs:
            xv = xr[...]
            if norm:
                xv = _rms(xv.astype(F32), g_ref[...])
            kw = xv.shape[-1]
            xs_ref[:, off:off + kw] = xv.astype(BF16)
            off += kw

    acc = jnp.dot(xs_ref[...], w_ref[...], preferred_element_type=F32)
    if res:
        acc = acc + r_ref[...]
    off = 0
    for o_ref, width in zip(o_refs, splits):
        o_ref[...] = acc[:, off:off + width].astype(o_ref.dtype)
        off += width


def _mm(xs, w, *, gain=None, res=None, tm=512, tn=None, splits=None, out_dtype=F32, name="mm"):
    t = xs[0].shape[0]
    k, n = w.shape
    assert sum(x.shape[1] for x in xs) == k
    tn = n if tn is None else tn
    splits = (tn,) if splits is None else tuple(splits)
    assert sum(splits) == tn and (len(splits) == 1 or tn == n)
    tm = min(tm, t)
    assert t % tm == 0 and n % tn == 0
    norm = gain is not None
    assert not norm or len(xs) == 1
    in_specs = [pl.BlockSpec((tm, x.shape[1]), lambda i, j: (i, 0)) for x in xs]
    args = list(xs)
    if norm:
        in_specs.append(pl.BlockSpec((1, k), lambda i, j: (0, 0)))
        args.append(gain.reshape(1, k).astype(F32))
    in_specs.append(pl.BlockSpec((k, tn), lambda i, j: (0, j)))
    args.append(w)
    if res is not None:
        in_specs.append(pl.BlockSpec((tm, tn), lambda i, j: (i, j)))
        args.append(res)
    if len(splits) == 1:
        out_shape = [jax.ShapeDtypeStruct((t, n), out_dtype)]
        out_specs = [pl.BlockSpec((tm, tn), lambda i, j: (i, j))]
    else:
        out_shape = [jax.ShapeDtypeStruct((t, s), out_dtype) for s in splits]
        out_specs = [pl.BlockSpec((tm, s), lambda i, j: (i, 0)) for s in splits]
    outs = pl.pallas_call(
        functools.partial(_mm_body, nx=len(xs), norm=norm, res=res is not None, splits=splits),
        grid=(t // tm, n // tn),
        in_specs=in_specs,
        out_specs=out_specs,
        out_shape=out_shape,
        scratch_shapes=[pltpu.VMEM((tm, k), BF16)],
        compiler_params=_cparams("parallel", "arbitrary"),
        name=name,
    )(*args)
    return outs[0] if len(outs) == 1 else outs


def _rope_tables(s, group):
    half = group // 2
    pos = jnp.arange(s, dtype=F32)
    inv_freq = ROPE_THETA ** (-jnp.arange(half, dtype=F32) / half)
    ang = pos[:, None] * inv_freq[None, :]
    cos, sin = jnp.cos(ang), jnp.sin(ang)
    zero = jnp.zeros_like(sin)
    reps = LANES // group
    c = jnp.tile(jnp.concatenate([cos, cos], -1), (1, reps))
    s1 = jnp.tile(jnp.concatenate([zero, sin], -1), (1, reps))
    s2 = jnp.tile(jnp.concatenate([-sin, zero], -1), (1, reps))
    return c, s1, s2


def _rope(x, tabs, half):
    w = x.shape[-1]
    c, s1, s2 = (jnp.tile(t, (1, w // LANES)) for t in tabs)
    return x * c + pltpu.roll(x, half, 1) * s1 + pltpu.roll(x, w - half, 1) * s2


def _prep_a(z, t32, q_ref, k_ref, vt_ref):
    qk = _rope(z[:, :512], t32, DIFF_DK // 2)
    lane = lax.broadcasted_iota(jnp.int32, (z.shape[0], HEAD_DIM), 1)
    scale = DIFF_DK ** -0.5 * LOG2E
    for h in range(GROUP_HEADS):
        q = qk[:, h * 64:(h + 1) * 64] * scale
        q_ref[0, h, 0] = jnp.where(lane < DIFF_DK, q, 0.0).astype(BF16)
        q_ref[0, h, 1] = jnp.where(lane >= DIFF_DK, q, 0.0).astype(BF16)
        k_ref[0, h] = qk[:, 256 + h * 64:256 + (h + 1) * 64].astype(BF16)
    vt_ref[0] = z[:, 512:768].T.reshape(vt_ref.shape[1:]).astype(BF16)


def _prep_b(z, t64, qn_ref, qr_ref, ka_ref, kw_ref, kc_ref, vc_ref, vs_ref, vw_ref, g_ref):
    ts = z.shape[0]
    scale = HEAD_DIM ** -0.5 * LOG2E
    rp = _rope(z[:, :384], t64, HEAD_DIM // 2)
    for h in range(GROUP_HEADS):
        qn_ref[0, h] = (z[:, h * 64:(h + 1) * 64] * scale).astype(BF16)
        qr_ref[0, h] = (rp[:, h * 64:(h + 1) * 64] * scale).astype(BF16)
    kpos = pl.program_id(0) * ts + lax.broadcasted_iota(jnp.int32, (ts, SEL_BLOCK), 0)
    lane = lax.broadcasted_iota(jnp.int32, (ts, SEL_BLOCK), 1)
    onehot = jnp.where((kpos >> SEL_SHIFT) == lane, 1.0, 0.0)
    ka_ref[0] = jnp.concatenate([rp[:, 256:320], onehot], axis=-1).astype(BF16)
    kw_ref[0] = rp[:, 320:384].astype(BF16)
    kc_ref[0] = z[:, 384:448].astype(BF16)
    vc_ref[0] = z[:, 448:512].astype(BF16)
    vt = z[:, 512:640].T.astype(BF16)
    vs_ref[0] = vt[:HEAD_DIM]
    vw_ref[0] = vt[HEAD_DIM:]
    g_ref[0] = jax.nn.sigmoid(z[:, 640:768])


def _prep_c(z, t32, qg_ref, kg_ref, wq_ref, wkv_ref, q_ref, k_ref, vt_ref):
    scale = (MLA_NOPE + MLA_ROPE) ** -0.5 * LOG2E
    cq = _rms(z[:, :384], qg_ref[...]).astype(BF16)
    ckv = _rms(z[:, 384:512], kg_ref[...]).astype(BF16)
    qc = jnp.dot(cq, wq_ref[...], preferred_element_type=F32)
    kvc = jnp.dot(ckv, wkv_ref[...], preferred_element_type=F32)
    half = MLA_ROPE // 2
    q_rope = _rope(qc[:, 256:384], t32, half)
    k_rope = _rope(z[:, 512:640], t32, half)[:, :MLA_ROPE]
    for h in range(GROUP_HEADS):
        q = jnp.concatenate([qc[:, h * 64:(h + 1) * 64], q_rope[:, h * 32:(h + 1) * 32]], axis=-1)
        q_ref[0, h] = (q * scale).astype(BF16)
        k_ref[0, h] = jnp.concatenate([kvc[:, h * 64:(h + 1) * 64], k_rope], axis=-1).astype(BF16)
    vt_ref[0] = kvc[:, 256:512].T.reshape(vt_ref.shape[1:]).astype(BF16)


def _prep_d(z, t64, q_ref, k_ref, vt_ref):
    scale = HEAD_DIM ** -0.5 * LOG2E
    qk = _rope(z[:, :512], t64, HEAD_DIM // 2)
    for h in range(GROUP_HEADS):
        q_ref[0, h] = (qk[:, h * 64:(h + 1) * 64] * scale).astype(BF16)
        k_ref[0, h] = qk[:, 256 + h * 64:256 + (h + 1) * 64].astype(BF16)
    vt_ref[0] = z[:, 512:768].T.reshape(vt_ref.shape[1:]).astype(BF16)


Z_SPLITS = (768, 768, 640, 768)
N_PREP_OUT = (3, 9, 3, 3)


def _project_prep_body(*refs):
    x_ref, g_ref, w_ref = refs[:3]
    t64 = tuple(r[...] for r in refs[3:6])
    t32 = tuple(r[...] for r in refs[6:9])
    qg_ref, kg_ref, wq_ref, wkv_ref = refs[9:13]
    outs = refs[13:]
    h = _rms(x_ref[0], g_ref[...]).astype(BF16)
    z = jnp.dot(h, w_ref[0], preferred_element_type=F32)
    o0, o1, o2, o3 = np.cumsum((0,) + Z_SPLITS[:3])
    n0, n1, n2 = np.cumsum(N_PREP_OUT[:3])
    _prep_a(z[:, o0:o0 + Z_SPLITS[0]], t32, *outs[:n0])
    _prep_b(z[:, o1:o1 + Z_SPLITS[1]], t64, *outs[n0:n1])
    _prep_c(z[:, o2:o2 + Z_SPLITS[2]], t32, qg_ref, kg_ref, wq_ref, wkv_ref, *outs[n1:n2])
    _prep_d(z[:, o3:o3 + Z_SPLITS[3]], t64, *outs[n2:])


def _project_prep(x, gain, w, layer, tabs64, tabs32, q_gain, kv_gain, wq, wkv, ts):
    b, s, d = x.shape
    dk = MLA_NOPE + MLA_ROPE
    full = lambda a: pl.BlockSpec(a.shape, lambda i, bi: (0,) * a.ndim)
    tspec = pl.BlockSpec((ts, LANES), lambda i, bi: (i, 0))
    h4 = lambda dd: pl.BlockSpec((1, 4, ts, dd), lambda i, bi: (bi, 0, i, 0))
    h4t = pl.BlockSpec((1, 4, 64, ts), lambda i, bi: (bi, 0, 0, i))
    sq = lambda dd: pl.BlockSpec((1, ts, dd), lambda i, bi: (bi, i, 0))
    t64 = pl.BlockSpec((1, 64, ts), lambda i, bi: (bi, 0, i))
    sd = lambda *shape, dt=BF16: jax.ShapeDtypeStruct(shape, dt)
    out_specs = [pl.BlockSpec((1, 4, 2, ts, 64), lambda i, bi: (bi, 0, 0, i, 0)), h4(64), h4t,
                 h4(64), h4(64), sq(128), sq(64), sq(64), sq(64), t64, t64, sq(128),
                 h4(dk), h4(dk), h4t,
                 h4(64), h4(64), h4t]
    out_shape = [sd(b, 4, 2, s, 64), sd(b, 4, s, 64), sd(b, 4, 64, s),
                 sd(b, 4, s, 64), sd(b, 4, s, 64), sd(b, s, 128), sd(b, s, 64), sd(b, s, 64), sd(b, s, 64),
                 sd(b, 64, s), sd(b, 64, s), sd(b, s, 128, dt=F32),
                 sd(b, 4, s, dk), sd(b, 4, s, dk), sd(b, 4, 64, s),
                 sd(b, 4, s, 64), sd(b, 4, s, 64), sd(b, 4, 64, s)]
    g2 = gain.reshape(1, d).astype(F32)
    outs = pl.pallas_call(
        _project_prep_body,
        grid=(s // ts, b),
        in_specs=[pl.BlockSpec((1, ts, d), lambda i, bi: (bi, i, 0)), full(g2),
                  pl.BlockSpec((1,) + w.shape[1:], lambda i, bi: (layer, 0, 0))] + [tspec] * 6
                 + [full(q_gain), full(kv_gain), full(wq), full(wkv)],
        out_specs=out_specs,
        out_shape=out_shape,
        compiler_params=_cparams("parallel", "parallel"),
        name="w_in_prep",
    )(x, g2, w, *tabs64, *tabs32, q_gain, kv_gain, wq, wkv)
    n0, n1, n2 = np.cumsum(N_PREP_OUT[:3])
    return outs[:n0], outs[n0:n1], outs[n1:n2], outs[n2:]


def _flash_schedule(nq, tq, tk, sk, window):
    qi, kj, fl = [], [], []
    for i in range(nq):
        q_lo, q_hi = i * tq, i * tq + tq - 1
        j_hi = min(q_hi, sk - 1) // tk
        j_lo = 0 if window is None else max(0, q_lo - window) // tk
        for j in range(j_lo, j_hi + 1):
            k_min, k_max = j * tk, j * tk + tk - 1
            full = k_max <= q_lo and (window is None or q_hi - k_min <= window)
            qi.append(i)
            kj.append(j)
            fl.append((1 if j == j_lo else 0) | (2 if j == j_hi else 0) | (0 if full else 4))
    return (np.asarray(qi, np.int32), np.asarray(kj, np.int32), np.asarray(fl, np.int32))


def _flash_body(qi_ref, kj_ref, fl_ref, q_ref, k_ref, vt_ref, o_ref, m_sc, acc_sc, *,
                gb, hq, tq, tk, window, with_lse, merge_groups):
    t = pl.program_id(1)
    fl = fl_ref[t]
    rows = hq * tq
    dv = vt_ref.shape[1]

    @pl.when((fl & 1) != 0)
    def _():
        m_sc[...] = jnp.full(m_sc.shape, NEG_BIG, F32)
        acc_sc[...] = jnp.zeros(acc_sc.shape, F32)

    def step(masked):
        q = q_ref[...].reshape(gb, rows, q_ref.shape[-1])
        st = jnp.einsum("gkd,gqd->gkq", k_ref[...], q, preferred_element_type=F32)
        if masked:
            krow = lax.broadcasted_iota(jnp.int32, (tk, rows), 0)
            qcol = lax.broadcasted_iota(jnp.int32, (tk, rows), 1)
            if hq > 1:
                qcol = qcol & (tq - 1)
            dist = (qi_ref[t] * tq - kj_ref[t] * tk) + qcol - krow
            ok = dist >= 0
            if window is not None:
                ok = ok & (dist <= window)
            st = jnp.where(ok[None], st, NEG_BIG)
        m_prev = m_sc[...]
        m_new = jnp.maximum(m_prev, jnp.max(st, axis=1, keepdims=True))
        alpha = jnp.exp2(m_prev - m_new)
        p = jnp.exp2((st - m_new).astype(BF16))
        v_ext = jnp.concatenate([vt_ref[...], jnp.ones((gb, ONES_ROWS, tk), BF16)], axis=1)
        acc_sc[...] = alpha * acc_sc[...] + jnp.einsum("gdk,gkq->gdq", v_ext, p, preferred_element_type=F32)
        m_sc[...] = m_new

    @pl.when((fl & 4) != 0)
    def _():
        step(True)

    @pl.when((fl & 4) == 0)
    def _():
        step(False)

    @pl.when((fl & 2) != 0)
    def _():
        l = acc_sc[:, dv:dv + 1, :]
        o_t = acc_sc[:, :dv, :] / l
        if merge_groups:
            for n in range(gb // merge_groups):
                heads = o_t[n * merge_groups:(n + 1) * merge_groups]
                o_ref[n] = heads.reshape(merge_groups * dv, rows).T.astype(o_ref.dtype)
            return
        if with_lse:
            extra = jnp.broadcast_to(m_sc[...] * LN2 + jnp.log(l), o_t.shape)
        else:
            extra = jnp.zeros((gb, LANES - dv, rows), F32)
        dvo = o_ref.shape[-1]
        for g in range(gb):
            o = jnp.concatenate([o_t[g], extra[g]], axis=0).T
            o_ref[g] = o[:, :dvo].reshape(o_ref.shape[1:]).astype(o_ref.dtype)


def _flash(q, k, vt, *, tq, tk, gb=1, window=None, with_lse=False, merge_groups=0, out_dtype=F32,
           name="flash"):
    g, hq, sq, dk = q.shape
    _, dv, sk = vt.shape
    tq, tk = min(tq, sq), min(tk, sk)
    assert g % gb == 0 and sq % tq == 0 and sk % tk == 0
    assert tq & (tq - 1) == 0 and 2 * dv == LANES
    qi, kj, fl = _flash_schedule(sq // tq, tq, tk, sk, window)
    dvo = 2 * dv if with_lse else dv
    rows = hq * tq
    if merge_groups:
        mh = merge_groups
        assert hq == 1 and not with_lse and (mh * dv) % LANES == 0 and gb % mh == 0
        out_spec = pl.BlockSpec((gb // mh, tq, mh * dv), lambda gi, t, qi, kj, fl: (gi, qi[t], 0))
        out_shape = jax.ShapeDtypeStruct((g // mh, sq, mh * dv), out_dtype)
    else:
        out_spec = pl.BlockSpec((gb, hq, tq, dvo), lambda gi, t, qi, kj, fl: (gi, 0, qi[t], 0))
        out_shape = jax.ShapeDtypeStruct((g, hq, sq, dvo), out_dtype)
    grid_spec = pltpu.PrefetchScalarGridSpec(
        num_scalar_prefetch=3,
        grid=(g // gb, len(qi)),
        in_specs=[pl.BlockSpec((gb, hq, tq, dk), lambda gi, t, qi, kj, fl: (gi, 0, qi[t], 0)),
                  pl.BlockSpec((gb, tk, dk), lambda gi, t, qi, kj, fl: (gi, kj[t], 0)),
                  pl.BlockSpec((gb, dv, tk), lambda gi, t, qi, kj, fl: (gi, 0, kj[t]))],
        out_specs=out_spec,
        scratch_shapes=[pltpu.VMEM((gb, 1, rows), F32), pltpu.VMEM((gb, dv + ONES_ROWS, rows), F32)],
    )
    return pl.pallas_call(
        functools.partial(_flash_body, gb=gb, hq=hq, tq=tq, tk=tk, window=window, with_lse=with_lse,
                          merge_groups=merge_groups),
        grid_spec=grid_spec,
        out_shape=out_shape,
        compiler_params=_cparams("parallel", "arbitrary"),
        name=name,
    )(jnp.asarray(qi), jnp.asarray(kj), jnp.asarray(fl), q, k, vt)


def _band_body(q_ref, kp_ref, kc_ref, vp_ref, vc_ref, o_ref, *, tq, window):
    i = pl.program_id(1)
    gb, dv = vp_ref.shape[0], vp_ref.shape[1]
    k2 = jnp.concatenate([kp_ref[...], kc_ref[...]], axis=1)
    st = jnp.einsum("gkd,gqd->gkq", k2, q_ref[...], preferred_element_type=F32)
    krow = lax.broadcasted_iota(jnp.int32, (2 * tq, tq), 0)
    qcol = lax.broadcasted_iota(jnp.int32, (2 * tq, tq), 1)
    dist = qcol + tq - krow
    ok = (dist >= 0) & (dist <= window) & ((krow >= tq) | (i > 0))
    st = jnp.where(ok[None], st, NEG_BIG)
    m = jnp.max(st, axis=1, keepdims=True)
    p = jnp.exp2((st - m).astype(BF16))
    v_ext = jnp.concatenate([jnp.concatenate([vp_ref[...], vc_ref[...]], axis=2),
                             jnp.ones((gb, ONES_ROWS, 2 * tq), BF16)], axis=1)
    acc = jnp.einsum("gdk,gkq->gdq", v_ext, p, preferred_element_type=F32)
    l = acc[:, dv:dv + 1, :]
    o_t = acc[:, :dv, :] / l
    lse = jnp.broadcast_to(m * LN2 + jnp.log(l), o_t.shape)
    for g in range(gb):
        o_ref[g] = jnp.concatenate([o_t[g], lse[g]], axis=0).T.astype(o_ref.dtype)


def _band_attention(q, k, vt, *, tq, gb, window, name):
    g, s, dk = q.shape
    dv = vt.shape[1]
    tq = min(tq, s)
    assert window <= tq and g % gb == 0 and s % tq == 0 and 2 * dv == LANES
    prev = lambda i: jnp.maximum(i - 1, 0)
    return pl.pallas_call(
        functools.partial(_band_body, tq=tq, window=window),
        grid=(g // gb, s // tq),
        in_specs=[pl.BlockSpec((gb, tq, dk), lambda gi, i: (gi, i, 0)),
                  pl.BlockSpec((gb, tq, dk), lambda gi, i: (gi, prev(i), 0)),
                  pl.BlockSpec((gb, tq, dk), lambda gi, i: (gi, i, 0)),
                  pl.BlockSpec((gb, dv, tq), lambda gi, i: (gi, 0, prev(i))),
                  pl.BlockSpec((gb, dv, tq), lambda gi, i: (gi, 0, i))],
        out_specs=pl.BlockSpec((gb, tq, 2 * dv), lambda gi, i: (gi, i, 0)),
        out_shape=jax.ShapeDtypeStruct((g, s, 2 * dv), F32),
        compiler_params=_cparams("parallel", "parallel"),
        name=name,
    )(q, k, k, vt, vt)


def _gelu_tanh(x):
    return 0.5 * x * (1.0 + jnp.tanh(math.sqrt(2.0 / math.pi) * (x + 0.044715 * (x * x * x))))


def _compress_body(r_ref, pos_ref, w1_ref, w2_ref, o_ref, ot_ref):
    half = CMP_STRIDE * HEAD_DIM
    r = r_ref[0, 0]
    w1 = w1_ref[0]
    top = jnp.dot(r, w1[:half], preferred_element_type=F32)
    bot = jnp.dot(r, w1[half:], preferred_element_type=F32)
    nc = r.shape[0]
    posb = jnp.dot(pos_ref[0], w1, preferred_element_type=F32)[:1]
    pre = top + pltpu.roll(bot, nc - 1, 0) + posb
    hid = _gelu_tanh(pre).astype(BF16)
    o = jnp.dot(hid, w2_ref[0], preferred_element_type=F32)
    o_ref[0, 0] = o.astype(o_ref.dtype)
    ot_ref[0, 0] = jnp.concatenate([o, jnp.zeros_like(o)], axis=1).T[:HEAD_DIM].astype(ot_ref.dtype)


def _nsa_compress(kv_rows, pos_flat, w1, w2):
    _, b, nc, _ = kv_rows.shape
    return pl.pallas_call(
        _compress_body,
        grid=(2, b),
        in_specs=[pl.BlockSpec((1, 1, nc, 1024), lambda w, bi: (w, bi, 0, 0)),
                  pl.BlockSpec((1, 8, 2048), lambda w, bi: (w, 0, 0)),
                  pl.BlockSpec((1, 2048, 256), lambda w, bi: (w, 0, 0)),
                  pl.BlockSpec((1, 256, 64), lambda w, bi: (w, 0, 0))],
        out_specs=[pl.BlockSpec((1, 1, nc, 64), lambda w, bi: (w, bi, 0, 0)),
                   pl.BlockSpec((1, 1, 64, nc), lambda w, bi: (w, bi, 0, 0))],
        out_shape=[jax.ShapeDtypeStruct((2, b, nc, 64), BF16), jax.ShapeDtypeStruct((2, b, 64, nc), BF16)],
        compiler_params=_cparams("parallel", "parallel"),
        name="nsa_compress",
    )(kv_rows, pos_flat, w1, w2)


def _cmp_select_body(qn_ref, qr_ref, kc_ref, vct_ref, covt_ref, o_ref, qa_ref, *, tq, n_sel, k_top):
    ncp = kc_ref.shape[2]
    qpos = pl.program_id(1) * tq + lax.broadcasted_iota(jnp.int32, (ncp, tq), 1)
    blk_end = lax.broadcasted_iota(jnp.int32, (ncp, tq), 0) * CMP_STRIDE + (CMP_LEN - 1)
    mask = blk_end <= qpos
    kc, vct = kc_ref[0, 0], vct_ref[0, 0]
    p_sum = jnp.zeros((ncp, tq), F32)
    outs = []
    for h in range(GROUP_HEADS):
        st = lax.dot_general(kc, qn_ref[0, h], (((1,), (1,)), ((), ())), preferred_element_type=F32)
        st = jnp.where(mask, st, NEG_BIG)
        mx = jnp.max(st, axis=0, keepdims=True)
        mx = jnp.where(mx > 0.5 * NEG_BIG, mx, 0.0)
        e = jnp.where(mask, jnp.exp2(st - mx), 0.0)
        p = e / jnp.maximum(jnp.sum(e, axis=0, keepdims=True), 1e-30)
        outs.append(jnp.dot(vct, p.astype(BF16), preferred_element_type=F32))
        p_sum = p_sum + p
    o_ref[0] = jnp.concatenate(outs, axis=0).T.astype(o_ref.dtype)
    p_hi = p_sum.astype(BF16)
    p_lo = (p_sum - p_hi.astype(F32)).astype(BF16)
    covt = covt_ref[...]
    imp_t = (jnp.dot(covt, p_hi, preferred_element_type=F32) + jnp.dot(covt, p_lo, preferred_element_type=F32))
    nblk = SEL_BLOCK
    tpos = pl.program_id(1) * tq + lax.broadcasted_iota(jnp.int32, (nblk, tq), 1)
    cur = tpos >> SEL_SHIFT
    j = lax.broadcasted_iota(jnp.int32, (nblk, tq), 0)
    valid = j <= cur
    forced = (j == 0) | (j == cur) | (j == cur - 1)
    score = jnp.where(valid, jnp.where(forced, FORCED_SCORE, imp_t), -jnp.inf)
    sub = 8
    groups = [score[g * sub:(g + 1) * sub] for g in range(nblk // sub)]
    ranks = [jnp.zeros((sub, tq), jnp.int32) for _ in groups]
    jj = lax.broadcasted_iota(jnp.int32, (sub, tq), 0)
    for i in range(n_sel):
        row = jnp.broadcast_to(score[i:i + 1], (sub, tq))
        for g, sg in enumerate(groups):
            if i < g * sub:
                ahead = row >= sg
            elif i >= (g + 1) * sub:
                ahead = row > sg
            else:
                ahead = (row > sg) | ((row == sg) & (i - g * sub < jj))
            ranks[g] = ranks[g] + ahead.astype(jnp.int32)
    rank = jnp.concatenate(ranks, axis=0)
    bias_t = jnp.where(valid & (rank < k_top), 0.0, SEL_BIAS)
    bias = jnp.concatenate([bias_t, jnp.zeros((LANES - nblk, tq), F32)], axis=0).T[:, :nblk].astype(BF16)
    for h in range(GROUP_HEADS):
        qa_ref[0, h] = jnp.concatenate([qr_ref[0, h], bias], axis=-1)


def _cover_matrix(ncp, n_sel, lanes):
    c_start = np.arange(ncp)[:, None] * CMP_STRIDE
    s_start = np.arange(lanes)[None, :] * SEL_BLOCK
    cov = np.clip(np.minimum(c_start + CMP_LEN, s_start + SEL_BLOCK) - np.maximum(c_start, s_start), 0, None)
    cov = cov.astype(np.float32) / CMP_LEN
    cov[:, n_sel:] = 0.0
    cov[ncp - 1:, :] = 0.0
    return cov


def _cmp_select(qn, qr, kvc, kvct, tq):
    b, _, s, _ = qn.shape
    ncp = kvc.shape[2]
    n_sel = s // SEL_BLOCK
    assert n_sel <= SEL_BLOCK
    k_top = min(SEL_TOPN, n_sel)
    covt = jnp.asarray(_cover_matrix(ncp, n_sel, SEL_BLOCK).T, BF16)
    h4 = lambda d: pl.BlockSpec((1, 4, tq, d), lambda bi, i: (bi, 0, i, 0))
    return pl.pallas_call(
        functools.partial(_cmp_select_body, tq=tq, n_sel=n_sel, k_top=k_top),
        grid=(b, s // tq),
        in_specs=[h4(64), h4(64),
                  pl.BlockSpec((1, 1, ncp, 64), lambda bi, i: (0, bi, 0, 0)),
                  pl.BlockSpec((1, 1, 64, ncp), lambda bi, i: (1, bi, 0, 0)),
                  pl.BlockSpec(covt.shape, lambda bi, i: (0, 0))],
        out_specs=[pl.BlockSpec((1, tq, 256), lambda bi, i: (bi, i, 0)), h4(128)],
        out_shape=[jax.ShapeDtypeStruct((b, s, 256), BF16), jax.ShapeDtypeStruct((b, 4, s, 128), BF16)],
        compiler_params=_cparams("parallel", "parallel"),
        name="nsa_cmp_select",
    )(qn, qr, kvc, kvct, covt)


def _merge_a(o_ref, lam_ref, g_ref, out_scale):
    outs = []
    for h in range(GROUP_HEADS):
        o = o_ref[0, h, 0].astype(F32) - lam_ref[...] * o_ref[0, h, 1].astype(F32)
        outs.append(_rms(o, g_ref[...]) * out_scale)
    return outs


def _merge_b(oc_ref, os_ref, ow_ref, g_ref):
    g = g_ref[0]
    outs = []
    for h in range(GROUP_HEADS):
        outs.append(g[:, 3 * h:3 * h + 1] * oc_ref[0, :, h * 64:(h + 1) * 64].astype(F32)
                    + g[:, 3 * h + 1:3 * h + 2] * os_ref[0, h].astype(F32)
                    + g[:, 3 * h + 2:3 * h + 3] * ow_ref[0, h].astype(F32))
    return outs


def _merge_d(o1_ref, o2_ref, o3_ref):
    outs = []
    for h in range(GROUP_HEADS):
        e = [r[0, h] for r in (o1_ref, o2_ref, o3_ref)]
        lse = [x[:, 64:] for x in e]
        mx = jnp.maximum(jnp.maximum(lse[0], lse[1]), lse[2])
        w = [jnp.exp(x - mx) for x in lse]
        den = w[0] + w[1] + w[2]
        outs.append((w[0] * e[0][:, :64] + w[1] * e[1][:, :64] + w[2] * e[2][:, :64]) / den)
    return outs


def _out_proj_body(x_ref, oa_ref, lam_ref, sub_ref, oc_ref, os_ref, ow_ref, g_ref, yc_ref, d1_ref, d2_ref, d3_ref,
                   w_ref, o_ref, *, out_scale):
    parts = (_merge_a(oa_ref, lam_ref, sub_ref, out_scale) + _merge_b(oc_ref, os_ref, ow_ref, g_ref)
             + [yc_ref[0].astype(F32)] + _merge_d(d1_ref, d2_ref, d3_ref))
    y = jnp.concatenate(parts, axis=-1).astype(BF16)
    o_ref[0] = x_ref[0] + jnp.dot(y, w_ref[...], preferred_element_type=F32)


def _out_proj(x, oa, lam, subln, out_scale, o_cmp, o_sel, o_win, gates, y_c, o_d, w, ts):
    b, s, d = x.shape
    row = lambda dd: pl.BlockSpec((1, ts, dd), lambda bi, i: (bi, i, 0))
    h4 = lambda dd: pl.BlockSpec((1, 4, ts, dd), lambda bi, i: (bi, 0, i, 0))
    vec = pl.BlockSpec((1, 64), lambda bi, i: (0, 0))
    return pl.pallas_call(
        functools.partial(_out_proj_body, out_scale=out_scale),
        grid=(b, s // ts),
        in_specs=[row(d), pl.BlockSpec((1, 4, 2, ts, 64), lambda bi, i: (bi, 0, 0, i, 0)), vec, vec,
                  row(256), h4(64), h4(64), row(128), row(256), h4(128), h4(128), h4(128),
                  pl.BlockSpec(w.shape, lambda bi, i: (0, 0))],
        out_specs=row(d),
        out_shape=jax.ShapeDtypeStruct((b, s, d), F32),
        compiler_params=_cparams("parallel", "parallel"),
        name="w_out_merge",
    )(x, oa, lam, subln, o_cmp, o_sel, o_win, gates, y_c, *o_d, w)


def _permute_w_in_body(w_ref, o_ref):
    w = w_ref[0]
    d = w.shape[0]
    b0, nq = 768, 256
    seg = lambda i: w[:, b0 + nq + i * 64: b0 + nq + (i + 1) * 64]
    kc, vc, ks, vs, kw, vw = (seg(i) for i in range(6))
    gb = w[:, b0 + nq + 384: b0 + nq + 384 + 12]
    c0 = b0 + 652
    d0 = c0 + 544
    out = jnp.concatenate([w[:, :b0], w[:, b0:b0 + nq], ks, kw, kc, vc, vs, vw, gb, jnp.zeros((d, 116), F32),
                           w[:, c0:c0 + 544], jnp.zeros((d, 96), F32), w[:, d0:d0 + 768]], axis=1)
    o_ref[0] = out.astype(o_ref.dtype)


def _permute_w_in(w_in):
    l, d, n = w_in.shape
    n_out = sum(Z_SPLITS)
    return pl.pallas_call(
        _permute_w_in_body,
        grid=(l,),
        in_specs=[pl.BlockSpec((1, d, n), lambda i: (i, 0, 0))],
        out_specs=pl.BlockSpec((1, d, n_out), lambda i: (i, 0, 0)),
        out_shape=jax.ShapeDtypeStruct((l, d, n_out), BF16),
        compiler_params=_cparams("parallel"),
        name="w_in_permute",
    )(w_in)


def _residue_classes(t, dil):
    g, s, d = t.shape
    return t.reshape(g, s // dil, dil, d).transpose(0, 2, 1, 3).reshape(g * dil, s // dil, d)


def _residue_classes_t(t, dil):
    g, d, s = t.shape
    return t.reshape(g, d, s // dil, dil).transpose(0, 3, 1, 2).reshape(g * dil, d, s // dil)


def _from_residue_classes(t, dil):
    gd, l, d = t.shape
    return t.reshape(gd // dil, dil, l, d).transpose(0, 2, 1, 3).reshape(gd // dil, l * dil, d)


def _token_mixers(x, layer, rope_tabs, norm_g, w_in, w_out, diff_lambda, diff_subln, cmp_pos, cmp_w1, cmp_w2,
                  q_norm, kv_norm, w_uq, w_ukv):
    b, s, d = x.shape
    t = b * s
    ts = min(512, s)
    wq = w_uq.reshape(-1, 4, MLA_NOPE + MLA_ROPE)
    wq = jnp.concatenate([wq[:, :, :MLA_NOPE].reshape(-1, 256), wq[:, :, MLA_NOPE:].reshape(-1, 128)], axis=1)
    wkv = w_ukv.reshape(-1, 4, MLA_NOPE + MLA_DV)
    wkv = jnp.concatenate([wkv[:, :, :MLA_NOPE].reshape(-1, 256), wkv[:, :, MLA_NOPE:].reshape(-1, 256)], axis=1)
    prep_a, prep_b, prep_c, prep_d = _project_prep(
        x, norm_g, w_in, layer, rope_tabs[64], rope_tabs[32], q_norm.reshape(1, -1).astype(F32),
        kv_norm.reshape(1, -1).astype(F32), wq.astype(BF16), wkv.astype(BF16), ts)

    qa, ka, va = prep_a
    oa = _flash(qa.reshape(b * 4, 2, s, 64), ka.reshape(b * 4, s, 64), va.reshape(b * 4, 64, s),
                tq=512, tk=512, gb=math.gcd(b * 4, 8), out_dtype=BF16, name="flash_diff").reshape(b, 4, 2, s, 64)
    lam_init = 0.8 - 0.6 * math.exp(-0.3 * layer)
    lp = diff_lambda.astype(F32)
    lam = jnp.exp(jnp.sum(lp[0] * lp[1])) - jnp.exp(jnp.sum(lp[2] * lp[3])) + lam_init

    qn, qr, k_aug, kw, kc, vc, vs, vw, gates = prep_b
    kv_rows = jnp.stack([kc, vc]).reshape(2, b, s // CMP_STRIDE, CMP_STRIDE * HEAD_DIM)
    pos_flat = jnp.broadcast_to(cmp_pos.reshape(2, 1, CMP_LEN * HEAD_DIM), (2, 8, CMP_LEN * HEAD_DIM)).astype(BF16)
    kvc, kvct = _nsa_compress(kv_rows, pos_flat, cmp_w1.astype(BF16), cmp_w2.astype(BF16))
    o_cmp, q_aug = _cmp_select(qn, qr, kvc, kvct, min(256, s))
    gb_b = math.gcd(b, 4)
    o_sel = _flash(q_aug, k_aug, vs, tq=256, tk=512, gb=gb_b, out_dtype=BF16, name="flash_sel")
    o_win = _flash(qr, kw, vw, tq=256, tk=256, gb=gb_b, window=NSA_WINDOW - 1, out_dtype=BF16, name="flash_win")

    qc, kcat, vcv = prep_c
    dk = MLA_NOPE + MLA_ROPE
    y_c = _flash(qc.reshape(b * 4, 1, s, dk), kcat.reshape(b * 4, s, dk), vcv.reshape(b * 4, 64, s),
                 tq=512, tk=512, gb=math.gcd(b * 4, 16), merge_groups=4, out_dtype=BF16, name="flash_mla")

    qd, kd, vd = prep_d
    qd, kd, vd = qd.reshape(b * 4, s, 64), kd.reshape(b * 4, s, 64), vd.reshape(b * 4, 64, s)
    o_d = []
    for window, dil in DILATED_PATTERNS:
        l = s // dil
        qq, kk, vv = _residue_classes(qd, dil), _residue_classes(kd, dil), _residue_classes_t(vd, dil)
        g = qq.shape[0]
        gb = math.gcd(g, 32)
        o = _band_attention(qq, kk, vv, tq=128, gb=gb, window=window // dil, name=f"band_dil{dil}")
        o_d.append(_from_residue_classes(o, dil).reshape(b, 4, s, 128))
    return _out_proj(x, oa, jnp.full((1, 64), lam, F32), diff_subln.reshape(1, 64).astype(F32), 1.0 - lam_init,
                     o_cmp, o_sel, o_win, gates, y_c, o_d, w_out.astype(BF16), ts)


def _cross_body(x_ref, g_ref, wq_ref, k_ref, v_ref, wo_ref, o_ref):
    x = x_ref[0]
    h = _rms(x, g_ref[...]).astype(BF16)
    q = jnp.dot(h, wq_ref[...], preferred_element_type=F32).astype(BF16)
    dh = q.shape[-1] // CROSS_HEADS
    outs = []
    for hd in range(CROSS_HEADS):
        sl = slice(hd * dh, (hd + 1) * dh)
        s = lax.dot_general(q[:, sl], k_ref[0, :, sl], (((1,), (1,)), ((), ())), preferred_element_type=F32)
        e = jnp.exp(s - jnp.max(s, axis=-1, keepdims=True))
        p = e / jnp.sum(e, axis=-1, keepdims=True)
        outs.append(jnp.dot(p.astype(BF16), v_ref[0, :, sl], preferred_element_type=F32))
    o = jnp.concatenate(outs, axis=-1).astype(BF16)
    o_ref[0] = x + jnp.dot(o, wo_ref[...], preferred_element_type=F32)


def _cross_attention(x, mem, g_x, g_mem, wq, wk, wv, wo):
    b, s, d = x.shape
    m = mem.shape[1]
    ts = min(512, s)
    dh = d // CROSS_HEADS
    k, v = _mm([mem.reshape(b * m, d)], jnp.concatenate([wk, wv], axis=1).astype(BF16), gain=g_mem,
               tm=min(512, b * m), splits=(d, d), out_dtype=BF16, name="cross_kv")
    full = lambda a: pl.BlockSpec(a.shape, lambda bi, i: (0,) * a.ndim)
    wq_s = (wq * dh ** -0.5).astype(BF16)
    wo_b = wo.astype(BF16)
    g2 = g_x.reshape(1, d).astype(F32)
    return pl.pallas_call(
        _cross_body,
        grid=(b, s // ts),
        in_specs=[pl.BlockSpec((1, ts, d), lambda bi, i: (bi, i, 0)), full(g2), full(wq_s),
                  pl.BlockSpec((1, m, d), lambda bi, i: (bi, 0, 0)),
                  pl.BlockSpec((1, m, d), lambda bi, i: (bi, 0, 0)), full(wo_b)],
        out_specs=pl.BlockSpec((1, ts, d), lambda bi, i: (bi, i, 0)),
        out_shape=jax.ShapeDtypeStruct((b, s, d), F32),
        compiler_params=_cparams("parallel", "parallel"),
        name="cross_attn",
    )(x, g2, wq_s, k.reshape(b, m, d), v.reshape(b, m, d), wo_b)


def _swiglu_body(x_ref, g_ref, wg_ref, wu_ref, wd_ref, o_ref, h_sc, acc_sc):
    f = pl.program_id(1)

    @pl.when(f == 0)
    def _():
        h_sc[...] = _rms(x_ref[...], g_ref[...]).astype(BF16)
        acc_sc[...] = x_ref[...]

    h = h_sc[...]
    gate = jnp.dot(h, wg_ref[...], preferred_element_type=F32)
    up = jnp.dot(h, wu_ref[...], preferred_element_type=F32)
    act = (gate * jax.nn.sigmoid(gate) * up).astype(BF16)
    acc_sc[...] += jnp.dot(act, wd_ref[...], preferred_element_type=F32)

    @pl.when(f == pl.num_programs(1) - 1)
    def _():
        o_ref[...] = acc_sc[...]


def _swiglu(x, gain, wg, wu, wd, tm=512, tf=1408):
    t, d = x.shape
    ff = wg.shape[1]
    tm, tf = min(tm, t), min(tf, ff)
    assert t % tm == 0 and ff % tf == 0
    return pl.pallas_call(
        _swiglu_body,
        grid=(t // tm, ff // tf),
        in_specs=[pl.BlockSpec((tm, d), lambda i, f: (i, 0)),
                  pl.BlockSpec((1, d), lambda i, f: (0, 0)),
                  pl.BlockSpec((d, tf), lambda i, f: (0, f)),
                  pl.BlockSpec((d, tf), lambda i, f: (0, f)),
                  pl.BlockSpec((tf, d), lambda i, f: (f, 0))],
        out_specs=pl.BlockSpec((tm, d), lambda i, f: (i, 0)),
        out_shape=jax.ShapeDtypeStruct((t, d), F32),
        scratch_shapes=[pltpu.VMEM((tm, d), BF16), pltpu.VMEM((tm, d), F32)],
        compiler_params=_cparams("parallel", "arbitrary"),
        name="swiglu",
    )(x, gain.reshape(1, d).astype(F32), wg.astype(BF16), wu.astype(BF16), wd.astype(BF16))


def _router_body(x_ref, g_ref, r_ref, h_ref, rt_ref):
    h = _rms(x_ref[...], g_ref[...])
    h_ref[...] = h
    h_hi = h.astype(BF16)
    h_lo = (h - h_hi.astype(F32)).astype(BF16)
    r = r_ref[...]
    r_hi = r.astype(BF16)
    r_lo = (r - r_hi.astype(F32)).astype(BF16)
    logits = (jnp.dot(h_hi, r_hi, preferred_element_type=F32) + jnp.dot(h_lo, r_hi, preferred_element_type=F32)
              + jnp.dot(h_hi, r_lo, preferred_element_type=F32))
    lane = lax.broadcasted_iota(jnp.int32, logits.shape, 1)
    lg = jnp.where(lane < N_EXPERTS, logits, -jnp.inf)
    v0 = jnp.max(lg, axis=-1, keepdims=True)
    e0 = jnp.min(jnp.where(lg == v0, lane, LANES), axis=-1, keepdims=True)
    lg1 = jnp.where(lane == e0, -jnp.inf, lg)
    v1 = jnp.max(lg1, axis=-1, keepdims=True)
    e1 = jnp.min(jnp.where(lg1 == v1, lane, LANES), axis=-1, keepdims=True)
    g1 = 1.0 / (1.0 + jnp.exp(v0 - v1))
    g0 = 1.0 - g1
    out = jnp.where(lane == 0, g0, jnp.where(lane == 1, g1, 0.0))
    out = jnp.where(lane == 2, e0.astype(F32), jnp.where(lane == 3, e1.astype(F32), out))
    rt_ref[...] = out


def _router(x, gain, router, tm=512):
    t, d = x.shape
    tm = min(tm, t)
    r_pad = jnp.zeros((d, LANES), F32).at[:, :N_EXPERTS].set(router.astype(F32))
    return pl.pallas_call(
        _router_body,
        grid=(t // tm,),
        in_specs=[pl.BlockSpec((tm, d), lambda i: (i, 0)), pl.BlockSpec((1, d), lambda i: (0, 0)),
                  pl.BlockSpec((d, LANES), lambda i: (0, 0))],
        out_specs=[pl.BlockSpec((tm, d), lambda i: (i, 0)), pl.BlockSpec((tm, LANES), lambda i: (i, 0))],
        out_shape=[jax.ShapeDtypeStruct((t, d), F32), jax.ShapeDtypeStruct((t, LANES), F32)],
        compiler_params=_cparams("parallel"),
        name="moe_router",
    )(x, gain.reshape(1, d).astype(F32), r_pad)


def _row_copy(src_hbm, src_row, dst_ref, dst_row, sem):
    return pltpu.make_async_copy(src_hbm.at[pl.ds(src_row, 1)], dst_ref.at[pl.ds(dst_row, 1)], sem)


def _rank_body(rt_ref, u_ref, ps_ref, dest_ref, carry_sc):
    k, i = pl.program_id(0), pl.program_id(1)
    tm = rt_ref.shape[0]

    @pl.when((k == 0) & (i == 0))
    def _():
        carry_sc[...] = jnp.zeros(carry_sc.shape, F32)

    rt_t = rt_ref[...].T
    e_row = jnp.where(k == 0, rt_t[2:3], rt_t[3:4]).astype(jnp.int32)
    sub = lax.broadcasted_iota(jnp.int32, (N_EXPERTS, tm), 0)
    onehot = sub == e_row
    ahead = jnp.dot(onehot.astype(BF16), u_ref[...], preferred_element_type=F32)
    row = ahead + carry_sc[:, :1] + ps_ref[:, :1]
    dest = jnp.sum(jnp.where(onehot, row, 0.0), axis=0, keepdims=True)
    dest_ref[...] = dest.astype(jnp.int32).reshape(dest_ref.shape)
    carry_sc[...] += jnp.sum(onehot.astype(F32), axis=1, keepdims=True)


def _assignment_rows(routing, pstart, tm=512):
    t = routing.shape[0]
    tm = min(tm, t)
    upper = jnp.asarray(np.triu(np.ones((tm, tm), np.float32), 1), BF16)
    ps = jnp.broadcast_to(pstart.astype(F32)[:, None], (N_EXPERTS, LANES))
    out = pl.pallas_call(
        _rank_body,
        grid=(TOP_K, t // tm),
        in_specs=[pl.BlockSpec((tm, LANES), lambda k, i: (i, 0)),
                  pl.BlockSpec((tm, tm), lambda k, i: (0, 0)),
                  pl.BlockSpec((N_EXPERTS, LANES), lambda k, i: (0, 0))],
        out_specs=pl.BlockSpec((1, 1, 1, tm), lambda k, i: (k, i, 0, 0)),
        out_shape=jax.ShapeDtypeStruct((TOP_K, t // tm, 1, tm), jnp.int32),
        scratch_shapes=[pltpu.VMEM((N_EXPERTS, LANES), F32)],
        compiler_params=_cparams("arbitrary", "arbitrary"),
        name="moe_rank",
    )(routing, upper, ps)
    return out.reshape(TOP_K * t)


def _dispatch_body(dest_ref, h_ref, xs_in, xs_hbm, sem):
    del xs_in
    tm = h_ref.shape[0]
    t = dest_ref.shape[0] // TOP_K
    base = pl.program_id(0) * tm

    def copy(r, k, dst_row):
        return pltpu.make_async_copy(h_ref.at[pl.ds(r, 1)], xs_hbm.at[pl.ds(dst_row, 1)], sem)

    def start(r, c):
        for k in range(TOP_K):
            copy(r, k, dest_ref[k * t + base + r]).start()
        return c

    lax.fori_loop(0, tm, start, 0, unroll=8)
    for r in range(tm):
        for k in range(TOP_K):
            copy(r, k, 0).wait()


def _dispatch(h, dest, n_rows, tm=256):
    t, d = h.shape
    tm = min(tm, t)
    grid_spec = pltpu.PrefetchScalarGridSpec(
        num_scalar_prefetch=1,
        grid=(t // tm,),
        in_specs=[pl.BlockSpec((tm, d), lambda i, dest: (i, 0)), pl.BlockSpec(memory_space=pl.ANY)],
        out_specs=pl.BlockSpec(memory_space=pl.ANY),
        scratch_shapes=[pltpu.SemaphoreType.DMA(())],
    )
    return pl.pallas_call(
        _dispatch_body,
        grid_spec=grid_spec,
        out_shape=jax.ShapeDtypeStruct((n_rows, d), F32),
        input_output_aliases={2: 0},
        compiler_params=_cparams("arbitrary"),
        name="moe_dispatch",
    )(dest, h, jnp.zeros((n_rows, d), F32))


def _experts_body(be_ref, nb_ref, x_ref, wg_ref, wu_ref, wd_ref, o_ref, xb_sc, acc_sc):
    i, f = pl.program_id(0), pl.program_id(1)

    @pl.when(i < nb_ref[0])
    def _():
        @pl.when(f == 0)
        def _():
            xb_sc[...] = x_ref[...].astype(BF16)
            acc_sc[...] = jnp.zeros(acc_sc.shape, F32)

        x = xb_sc[...]
        gate = jnp.dot(x, wg_ref[0, 0], preferred_element_type=F32)
        up = jnp.dot(x, wu_ref[0, 0], preferred_element_type=F32)
        act = (gate * jax.nn.sigmoid(gate) * up).astype(BF16)
        acc_sc[...] += jnp.dot(act, wd_ref[0, 0], preferred_element_type=F32)

        @pl.when(f == pl.num_programs(1) - 1)
        def _():
            o_ref[...] = acc_sc[...]

    @pl.when(i >= nb_ref[0])
    def _():
        o_ref[...] = jnp.zeros(o_ref.shape, F32)


def _experts(xs, blk_e, n_used, wg, wu, wd, layer, rows, tf=1792):
    n_rows, d = xs.shape
    ff = wg.shape[3]
    n_blk = n_rows // rows
    last = ff // tf - 1
    grid_spec = pltpu.PrefetchScalarGridSpec(
        num_scalar_prefetch=2,
        grid=(n_blk, ff // tf),
        in_specs=[pl.BlockSpec((rows, d), lambda i, f, be, nb: (jnp.minimum(i, nb[0] - 1), 0)),
                  pl.BlockSpec((1, 1, d, tf), lambda i, f, be, nb: (layer, be[i], 0, jnp.where(i < nb[0], f, last))),
                  pl.BlockSpec((1, 1, d, tf), lambda i, f, be, nb: (layer, be[i], 0, jnp.where(i < nb[0], f, last))),
                  pl.BlockSpec((1, 1, tf, d), lambda i, f, be, nb: (layer, be[i], jnp.where(i < nb[0], f, last), 0))],
        out_specs=pl.BlockSpec((rows, d), lambda i, f, be, nb: (i, 0)),
        scratch_shapes=[pltpu.VMEM((rows, d), BF16), pltpu.VMEM((rows, d), F32)],
    )
    return pl.pallas_call(
        _experts_body,
        grid_spec=grid_spec,
        out_shape=jax.ShapeDtypeStruct((n_rows, d), F32),
        compiler_params=_cparams("arbitrary", "arbitrary"),
        name="moe_experts",
    )(blk_e, n_used, xs, wg, wu, wd)


def _combine_body(dest_ref, x_ref, rt_ref, ys_hbm, o_ref, buf, sem):
    tm = x_ref.shape[0]
    n_tok = dest_ref.shape[0] // TOP_K
    i = pl.program_id(0)
    slot = i & 1

    def start(tile, s):
        def body(r, c):
            for k in range(TOP_K):
                _row_copy(ys_hbm, dest_ref[k * n_tok + tile * tm + r], buf.at[s, k], r, sem.at[s]).start()
            return c
        lax.fori_loop(0, tm, body, 0, unroll=8)

    @pl.when(i == 0)
    def _():
        start(0, 0)

    @pl.when(i + 1 < pl.num_programs(0))
    def _():
        start(i + 1, 1 - slot)

    for r in range(tm):
        for k in range(TOP_K):
            _row_copy(ys_hbm, 0, buf.at[slot, k], r, sem.at[slot]).wait()
    rt = rt_ref[...]
    o_ref[...] = x_ref[...] + rt[:, 0:1] * buf[slot, 0] + rt[:, 1:2] * buf[slot, 1]


def _combine(x, routing, ys, dest, tm=256):
    t, d = x.shape
    tm = min(tm, t)
    grid_spec = pltpu.PrefetchScalarGridSpec(
        num_scalar_prefetch=1,
        grid=(t // tm,),
        in_specs=[pl.BlockSpec((tm, d), lambda i, dest: (i, 0)),
                  pl.BlockSpec((tm, LANES), lambda i, dest: (i, 0)),
                  pl.BlockSpec(memory_space=pl.ANY)],
        out_specs=pl.BlockSpec((tm, d), lambda i, dest: (i, 0)),
        scratch_shapes=[pltpu.VMEM((2, TOP_K, tm, d), F32), pltpu.SemaphoreType.DMA((2,))],
    )
    return pl.pallas_call(
        _combine_body,
        grid_spec=grid_spec,
        out_shape=jax.ShapeDtypeStruct((t, d), F32),
        compiler_params=_cparams("arbitrary"),
        name="moe_combine",
    )(dest, x, routing, ys)


MOE_ROWS = 512


def _moe(x, gain, router, w_gate, w_up, w_down, layer):
    t, d = x.shape
    h, routing = _router(x, gain, router)
    n_assign = t * TOP_K
    experts = routing[:, 2:2 + TOP_K].astype(jnp.int32)
    counts = jnp.sum((experts[:, :, None] == jnp.arange(N_EXPERTS)[None, None, :]).astype(jnp.int32), axis=(0, 1))
    padded = (counts + MOE_ROWS - 1) // MOE_ROWS * MOE_ROWS
    pend = jnp.cumsum(padded)
    pstart = pend - padded
    n_rows = -(-n_assign // MOE_ROWS) * MOE_ROWS + N_EXPERTS * MOE_ROWS
    n_blk = n_rows // MOE_ROWS
    blk_start = jnp.arange(n_blk, dtype=jnp.int32) * MOE_ROWS
    blk_e = jnp.minimum(jnp.sum((pend[None, :] <= blk_start[:, None]).astype(jnp.int32), axis=1), N_EXPERTS - 1)
    n_used = (pend[-1] // MOE_ROWS).astype(jnp.int32).reshape(1)
    dest = _assignment_rows(routing, pstart)
    xs = _dispatch(h, dest, n_rows)
    ys = _experts(xs, blk_e.astype(jnp.int32), n_used, w_gate, w_up, w_down, layer, MOE_ROWS)
    return _combine(x, routing, ys, dest)


def _final_norm_body(x_ref, g_ref, o_ref):
    o_ref[...] = _rms(x_ref[...], g_ref[...])


def _final_norm(x, gain, tm=1024):
    t, d = x.shape
    tm = min(tm, t)
    return pl.pallas_call(
        _final_norm_body,
        grid=(t // tm,),
        in_specs=[pl.BlockSpec((tm, d), lambda i: (i, 0)), pl.BlockSpec((1, d), lambda i: (0, 0))],
        out_specs=pl.BlockSpec((tm, d), lambda i: (i, 0)),
        out_shape=jax.ShapeDtypeStruct((t, d), F32),
        compiler_params=_cparams("parallel"),
        name="final_norm",
    )(x, gain.reshape(1, d).astype(F32))


def _all_rope_tables(s):
    return {64: _rope_tables(s, 64), 32: _rope_tables(s, 32)}


def kernel(x, mem, norm_mix, w_in, w_out, diff_lambda, diff_subln, nsa_cmp_pos, nsa_cmp_w1, nsa_cmp_w2, mla_q_norm, mla_kv_norm, mla_w_uq, mla_w_ukv, norm_cross, norm_mem, cross_wq, cross_wk, cross_wv, cross_wo, norm_ffn, ffn_w_gate, ffn_w_up, ffn_w_down, moe_router, moe_w_gate, moe_w_up, moe_w_down, final_norm):
    b, s, d = x.shape
    depth = w_in.shape[0]
    tabs = _all_rope_tables(s)
    w_in_p = _permute_w_in(w_in)
    moe_wg, moe_wu, moe_wd = moe_w_gate.astype(BF16), moe_w_up.astype(BF16), moe_w_down.astype(BF16)
    for l in range(depth):
        x = _token_mixers(x, l, tabs, norm_mix[l], w_in_p, w_out[l], diff_lambda[l], diff_subln[l],
                          nsa_cmp_pos[l], nsa_cmp_w1[l], nsa_cmp_w2[l], mla_q_norm[l], mla_kv_norm[l],
                          mla_w_uq[l], mla_w_ukv[l])
        x = _cross_attention(x, mem, norm_cross[l], norm_mem[l], cross_wq[l], cross_wk[l], cross_wv[l],
                             cross_wo[l])
        xt = x.reshape(b * s, d)
        i = l // 2
        if l % 2 == 0:
            xt = _swiglu(xt, norm_ffn[l], ffn_w_gate[i], ffn_w_up[i], ffn_w_down[i])
        else:
            xt = _moe(xt, norm_ffn[l], moe_router[i], moe_wg, moe_wu, moe_wd, i)
        x = xt.reshape(b, s, d)
    return _final_norm(x.reshape(b * s, d), final_norm).reshape(b, s, d)
```

```python
import functools
import math

import numpy as np
import jax
import jax.numpy as jnp
from jax import lax
from jax.experimental import pallas as pl
from jax.experimental.pallas import tpu as pltpu

F32 = jnp.float32
BF16 = jnp.bfloat16

HEAD_DIM = 64
GROUP_HEADS = 4
GROUP_WIDTH = HEAD_DIM * GROUP_HEADS
ROPE_THETA = 10000.0
NORM_EPS = 1e-6
DIFF_DK = HEAD_DIM // 2
CMP_LEN = 32
CMP_STRIDE = 16
SEL_BLOCK = 64
SEL_SHIFT = 6
SEL_TOPN = 16
NSA_WINDOW = 512
FORCED_SCORE = 1e6
MLA_NOPE = 64
MLA_ROPE = 32
MLA_DV = 64
DILATED_PATTERNS = ((128, 1), (512, 4), (2048, 16))
CROSS_HEADS = 4
N_EXPERTS = 8
TOP_K = 2

LANES = 128
V7X_VMEM_BYTES = 64 * 1024 * 1024
VMEM_LIMIT = V7X_VMEM_BYTES * 7 // 8

ONES_ROWS = 16
NEG_BIG = -1e30
SEL_BIAS = -1e9
LOG2E = math.log2(math.e)
LN2 = math.log(2.0)


def _cparams(*sem):
    return pltpu.CompilerParams(dimension_semantics=sem, vmem_limit_bytes=VMEM_LIMIT)


def _rms(xf, gain):
    ms = jnp.mean(xf * xf, axis=-1, keepdims=True)
    return xf * lax.rsqrt(ms + NORM_EPS) * gain


def _mm_body(*refs, nx, norm, res, splits):
    x_refs = refs[:nx]
    pos = nx
    g_ref = refs[pos] if norm else None
    pos += int(norm)
    w_ref = refs[pos]
    pos += 1
    r_ref = refs[pos] if res else None
    pos += int(res)
    o_refs = refs[pos:pos + len(splits)]
    xs_ref = refs[pos + len(splits)]

    @pl.when(pl.program_id(1) == 0)
    def _():
        off = 0
        for xr in x_refs:
            xv = xr[...]
            if norm:
                xv = _rms(xv.astype(F32), g_ref[...])
            kw = xv.shape[-1]
            xs_ref[:, off:off + kw] = xv.astype(BF16)
            off += kw

    acc = jnp.dot(xs_ref[...], w_ref[...], preferred_element_type=F32)
    if res:
        acc = acc + r_ref[...]
    off = 0
    for o_ref, width in zip(o_refs, splits):
        o_ref[...] = acc[:, off:off + width].astype(o_ref.dtype)
        off += width


def _mm(xs, w, *, gain=None, res=None, tm=512, tn=None, splits=None, out_dtype=F32, name="mm"):
    t = xs[0].shape[0]
    k, n = w.shape
    assert sum(x.shape[1] for x in xs) == k
    tn = n if tn is None else tn
    splits = (tn,) if splits is None else tuple(splits)
    assert sum(splits) == tn and (len(splits) == 1 or tn == n)
    tm = min(tm, t)
    assert t % tm == 0 and n % tn == 0
    norm = gain is not None
    assert not norm or len(xs) == 1
    in_specs = [pl.BlockSpec((tm, x.shape[1]), lambda i, j: (i, 0)) for x in xs]
    args = list(xs)
    if norm:
        in_specs.append(pl.BlockSpec((1, k), lambda i, j: (0, 0)))
        args.append(gain.reshape(1, k).astype(F32))
    in_specs.append(pl.BlockSpec((k, tn), lambda i, j: (0, j)))
    args.append(w)
    if res is not None:
        in_specs.append(pl.BlockSpec((tm, tn), lambda i, j: (i, j)))
        args.append(res)
    if len(splits) == 1:
        out_shape = [jax.ShapeDtypeStruct((t, n), out_dtype)]
        out_specs = [pl.BlockSpec((tm, tn), lambda i, j: (i, j))]
    else:
        out_shape = [jax.ShapeDtypeStruct((t, s), out_dtype) for s in splits]
        out_specs = [pl.BlockSpec((tm, s), lambda i, j: (i, 0)) for s in splits]
    outs = pl.pallas_call(
        functools.partial(_mm_body, nx=len(xs), norm=norm, res=res is not None, splits=splits),
        grid=(t // tm, n // tn),
        in_specs=in_specs,
        out_specs=out_specs,
        out_shape=out_shape,
        scratch_shapes=[pltpu.VMEM((tm, k), BF16)],
        compiler_params=_cparams("parallel", "arbitrary"),
        name=name,
    )(*args)
    return outs[0] if len(outs) == 1 else outs


def _rope_tables(s, group):
    half = group // 2
    pos = jnp.arange(s, dtype=F32)
    inv_freq = ROPE_THETA ** (-jnp.arange(half, dtype=F32) / half)
    ang = pos[:, None] * inv_freq[None, :]
    cos, sin = jnp.cos(ang), jnp.sin(ang)
    zero = jnp.zeros_like(sin)
    reps = LANES // group
    c = jnp.tile(jnp.concatenate([cos, cos], -1), (1, reps))
    s1 = jnp.tile(jnp.concatenate([zero, sin], -1), (1, reps))
    s2 = jnp.tile(jnp.concatenate([-sin, zero], -1), (1, reps))
    return c, s1, s2


def _rope(x, tabs, half):
    w = x.shape[-1]
    c, s1, s2 = (jnp.tile(t, (1, w // LANES)) for t in tabs)
    return x * c + pltpu.roll(x, half, 1) * s1 + pltpu.roll(x, w - half, 1) * s2


def _prep_a(z, t32, q_ref, k_ref, vt_ref):
    qk = _rope(z[:, :512], t32, DIFF_DK // 2)
    lane = lax.broadcasted_iota(jnp.int32, (z.shape[0], HEAD_DIM), 1)
    scale = DIFF_DK ** -0.5 * LOG2E
    for h in range(GROUP_HEADS):
        q = qk[:, h * 64:(h + 1) * 64] * scale
        q_ref[0, h, 0] = jnp.where(lane < DIFF_DK, q, 0.0).astype(BF16)
        q_ref[0, h, 1] = jnp.where(lane >= DIFF_DK, q, 0.0).astype(BF16)
        k_ref[0, h] = qk[:, 256 + h * 64:256 + (h + 1) * 64].astype(BF16)
    vt_ref[0] = z[:, 512:768].T.reshape(vt_ref.shape[1:]).astype(BF16)


def _prep_b(z, t64, qn_ref, qr_ref, ka_ref, kw_ref, kc_ref, vc_ref, vs_ref, vw_ref, g_ref):
    ts = z.shape[0]
    scale = HEAD_DIM ** -0.5 * LOG2E
    rp = _rope(z[:, :384], t64, HEAD_DIM // 2)
    for h in range(GROUP_HEADS):
        qn_ref[0, h] = (z[:, h * 64:(h + 1) * 64] * scale).astype(BF16)
        qr_ref[0, h] = (rp[:, h * 64:(h + 1) * 64] * scale).astype(BF16)
    kpos = pl.program_id(0) * ts + lax.broadcasted_iota(jnp.int32, (ts, SEL_BLOCK), 0)
    lane = lax.broadcasted_iota(jnp.int32, (ts, SEL_BLOCK), 1)
    onehot = jnp.where((kpos >> SEL_SHIFT) == lane, 1.0, 0.0)
    ka_ref[0] = jnp.concatenate([rp[:, 256:320], onehot], axis=-1).astype(BF16)
    kw_ref[0] = rp[:, 320:384].astype(BF16)
    kc_ref[0] = z[:, 384:448].astype(BF16)
    vc_ref[0] = z[:, 448:512].astype(BF16)
    vt = z[:, 512:640].T.astype(BF16)
    vs_ref[0] = vt[:HEAD_DIM]
    vw_ref[0] = vt[HEAD_DIM:]
    g_ref[0] = jax.nn.sigmoid(z[:, 640:768])


def _prep_c(z, t32, qg_ref, kg_ref, wq_ref, wkv_ref, q_ref, k_ref, vt_ref):
    scale = (MLA_NOPE + MLA_ROPE) ** -0.5 * LOG2E
    cq = _rms(z[:, :384], qg_ref[...]).astype(BF16)
    ckv = _rms(z[:, 384:512], kg_ref[...]).astype(BF16)
    qc = jnp.dot(cq, wq_ref[...], preferred_element_type=F32)
    kvc = jnp.dot(ckv, wkv_ref[...], preferred_element_type=F32)
    half = MLA_ROPE // 2
    q_rope = _rope(qc[:, 256:384], t32, half)
    k_rope = _rope(z[:, 512:640], t32, half)[:, :MLA_ROPE]
    for h in range(GROUP_HEADS):
        q = jnp.concatenate([qc[:, h * 64:(h + 1) * 64], q_rope[:, h * 32:(h + 1) * 32]], axis=-1)
        q_ref[0, h] = (q * scale).astype(BF16)
        k_ref[0, h] = jnp.concatenate([kvc[:, h * 64:(h + 1) * 64], k_rope], axis=-1).astype(BF16)
    vt_ref[0] = kvc[:, 256:512].T.reshape(vt_ref.shape[1:]).astype(BF16)


def _prep_d(z, t64, q_ref, k_ref, vt_ref):
    scale = HEAD_DIM ** -0.5 * LOG2E
    qk = _rope(z[:, :512], t64, HEAD_DIM // 2)
    for h in range(GROUP_HEADS):
        q_ref[0, h] = (qk[:, h * 64:(h + 1) * 64] * scale).astype(BF16)
        k_ref[0, h] = qk[:, 256 + h * 64:256 + (h + 1) * 64].astype(BF16)
    vt_ref[0] = z[:, 512:768].T.reshape(vt_ref.shape[1:]).astype(BF16)


Z_SPLITS = (768, 768, 640, 768)
N_PREP_OUT = (3, 9, 3, 3)


def _project_prep_body(*refs):
    x_ref, g_ref, w_ref = refs[:3]
    t64 = tuple(r[...] for r in refs[3:6])
    t32 = tuple(r[...] for r in refs[6:9])
    qg_ref, kg_ref, wq_ref, wkv_ref = refs[9:13]
    outs = refs[13:]
    h = _rms(x_ref[0], g_ref[...]).astype(BF16)
    z = jnp.dot(h, w_ref[0], preferred_element_type=F32)
    o0, o1, o2, o3 = np.cumsum((0,) + Z_SPLITS[:3])
    n0, n1, n2 = np.cumsum(N_PREP_OUT[:3])
    _prep_a(z[:, o0:o0 + Z_SPLITS[0]], t32, *outs[:n0])
    _prep_b(z[:, o1:o1 + Z_SPLITS[1]], t64, *outs[n0:n1])
    _prep_c(z[:, o2:o2 + Z_SPLITS[2]], t32, qg_ref, kg_ref, wq_ref, wkv_ref, *outs[n1:n2])
    _prep_d(z[:, o3:o3 + Z_SPLITS[3]], t64, *outs[n2:])


def _project_prep(x, gain, w, layer, tabs64, tabs32, q_gain, kv_gain, wq, wkv, ts):
    b, s, d = x.shape
    dk = MLA_NOPE + MLA_ROPE
    full = lambda a: pl.BlockSpec(a.shape, lambda i, bi: (0,) * a.ndim)
    tspec = pl.BlockSpec((ts, LANES), lambda i, bi: (i, 0))
    h4 = lambda dd: pl.BlockSpec((1, 4, ts, dd), lambda i, bi: (bi, 0, i, 0))
    h4t = pl.BlockSpec((1, 4, 64, ts), lambda i, bi: (bi, 0, 0, i))
    sq = lambda dd: pl.BlockSpec((1, ts, dd), lambda i, bi: (bi, i, 0))
    t64 = pl.BlockSpec((1, 64, ts), lambda i, bi: (bi, 0, i))
    sd = lambda *shape, dt=BF16: jax.ShapeDtypeStruct(shape, dt)
    out_specs = [pl.BlockSpec((1, 4, 2, ts, 64), lambda i, bi: (bi, 0, 0, i, 0)), h4(64), h4t,
                 h4(64), h4(64), sq(128), sq(64), sq(64), sq(64), t64, t64, sq(128),
                 h4(dk), h4(dk), h4t,
                 h4(64), h4(64), h4t]
    out_shape = [sd(b, 4, 2, s, 64), sd(b, 4, s, 64), sd(b, 4, 64, s),
                 sd(b, 4, s, 64), sd(b, 4, s, 64), sd(b, s, 128), sd(b, s, 64), sd(b, s, 64), sd(b, s, 64),
                 sd(b, 64, s), sd(b, 64, s), sd(b, s, 128, dt=F32),
                 sd(b, 4, s, dk), sd(b, 4, s, dk), sd(b, 4, 64, s),
                 sd(b, 4, s, 64), sd(b, 4, s, 64), sd(b, 4, 64, s)]
    g2 = gain.reshape(1, d).astype(F32)
    outs = pl.pallas_call(
        _project_prep_body,
        grid=(s // ts, b),
        in_specs=[pl.BlockSpec((1, ts, d), lambda i, bi: (bi, i, 0)), full(g2),
                  pl.BlockSpec((1,) + w.shape[1:], lambda i, bi: (layer, 0, 0))] + [tspec] * 6
                 + [full(q_gain), full(kv_gain), full(wq), full(wkv)],
        out_specs=out_specs,
        out_shape=out_shape,
        compiler_params=_cparams("parallel", "parallel"),
        name="w_in_prep",
    )(x, g2, w, *tabs64, *tabs32, q_gain, kv_gain, wq, wkv)
    n0, n1, n2 = np.cumsum(N_PREP_OUT[:3])
    return outs[:n0], outs[n0:n1], outs[n1:n2], outs[n2:]


def _flash_schedule(nq, tq, tk, sk, window):
    qi, kj, fl = [], [], []
    for i in range(nq):
        q_lo, q_hi = i * tq, i * tq + tq - 1
        j_hi = min(q_hi, sk - 1) // tk
        j_lo = 0 if window is None else max(0, q_lo - window) // tk
        for j in range(j_lo, j_hi + 1):
            k_min, k_max = j * tk, j * tk + tk - 1
            full = k_max <= q_lo and (window is None or q_hi - k_min <= window)
            qi.append(i)
            kj.append(j)
            fl.append((1 if j == j_lo else 0) | (2 if j == j_hi else 0) | (0 if full else 4))
    return (np.asarray(qi, np.int32), np.asarray(kj, np.int32), np.asarray(fl, np.int32))


def _flash_body(qi_ref, kj_ref, fl_ref, q_ref, k_ref, vt_ref, o_ref, m_sc, acc_sc, *,
                gb, hq, tq, tk, window, with_lse, merge_groups):
    t = pl.program_id(1)
    fl = fl_ref[t]
    rows = hq * tq
    dv = vt_ref.shape[1]

    @pl.when((fl & 1) != 0)
    def _():
        m_sc[...] = jnp.full(m_sc.shape, NEG_BIG, F32)
        acc_sc[...] = jnp.zeros(acc_sc.shape, F32)

    def step(masked):
        q = q_ref[...].reshape(gb, rows, q_ref.shape[-1])
        st = jnp.einsum("gkd,gqd->gkq", k_ref[...], q, preferred_element_type=F32)
        if masked:
            krow = lax.broadcasted_iota(jnp.int32, (tk, rows), 0)
            qcol = lax.broadcasted_iota(jnp.int32, (tk, rows), 1)
            if hq > 1:
                qcol = qcol & (tq - 1)
            dist = (qi_ref[t] * tq - kj_ref[t] * tk) + qcol - krow
            ok = dist >= 0
            if window is not None:
                ok = ok & (dist <= window)
            st = jnp.where(ok[None], st, NEG_BIG)
        m_prev = m_sc[...]
        m_new = jnp.maximum(m_prev, jnp.max(st, axis=1, keepdims=True))
        alpha = jnp.exp2(m_prev - m_new)
        p = jnp.exp2((st - m_new).astype(BF16))
        v_ext = jnp.concatenate([vt_ref[...], jnp.ones((gb, ONES_ROWS, tk), BF16)], axis=1)
        acc_sc[...] = alpha * acc_sc[...] + jnp.einsum("gdk,gkq->gdq", v_ext, p, preferred_element_type=F32)
        m_sc[...] = m_new

    @pl.when((fl & 4) != 0)
    def _():
        step(True)

    @pl.when((fl & 4) == 0)
    def _():
        step(False)

    @pl.when((fl & 2) != 0)
    def _():
        l = acc_sc[:, dv:dv + 1, :]
        o_t = acc_sc[:, :dv, :] / l
        if merge_groups:
            for n in range(gb // merge_groups):
                heads = o_t[n * merge_groups:(n + 1) * merge_groups]
                o_ref[n] = heads.reshape(merge_groups * dv, rows).T.astype(o_ref.dtype)
            return
        if with_lse:
            extra = jnp.broadcast_to(m_sc[...] * LN2 + jnp.log(l), o_t.shape)
        else:
            extra = jnp.zeros((gb, LANES - dv, rows), F32)
        dvo = o_ref.shape[-1]
        for g in range(gb):
            o = jnp.concatenate([o_t[g], extra[g]], axis=0).T
            o_ref[g] = o[:, :dvo].reshape(o_ref.shape[1:]).astype(o_ref.dtype)


def _flash(q, k, vt, *, tq, tk, gb=1, window=None, with_lse=False, merge_groups=0, out_dtype=F32,
           name="flash"):
    g, hq, sq, dk = q.shape
    _, dv, sk = vt.shape
    tq, tk = min(tq, sq), min(tk, sk)
    assert g % gb == 0 and sq % tq == 0 and sk % tk == 0
    assert tq & (tq - 1) == 0 and 2 * dv == LANES
    qi, kj, fl = _flash_schedule(sq // tq, tq, tk, sk, window)
    dvo = 2 * dv if with_lse else dv
    rows = hq * tq
    if merge_groups:
        mh = merge_groups
        assert hq == 1 and not with_lse and (mh * dv) % LANES == 0 and gb % mh == 0
        out_spec = pl.BlockSpec((gb // mh, tq, mh * dv), lambda gi, t, qi, kj, fl: (gi, qi[t], 0))
        out_shape = jax.ShapeDtypeStruct((g // mh, sq, mh * dv), out_dtype)
    else:
        out_spec = pl.BlockSpec((gb, hq, tq, dvo), lambda gi, t, qi, kj, fl: (gi, 0, qi[t], 0))
        out_shape = jax.ShapeDtypeStruct((g, hq, sq, dvo), out_dtype)
    grid_spec = pltpu.PrefetchScalarGridSpec(
        num_scalar_prefetch=3,
        grid=(g // gb, len(qi)),
        in_specs=[pl.BlockSpec((gb, hq, tq, dk), lambda gi, t, qi, kj, fl: (gi, 0, qi[t], 0)),
                  pl.BlockSpec((gb, tk, dk), lambda gi, t, qi, kj, fl: (gi, kj[t], 0)),
                  pl.BlockSpec((gb, dv, tk), lambda gi, t, qi, kj, fl: (gi, 0, kj[t]))],
        out_specs=out_spec,
        scratch_shapes=[pltpu.VMEM((gb, 1, rows), F32), pltpu.VMEM((gb, dv + ONES_ROWS, rows), F32)],
    )
    return pl.pallas_call(
        functools.partial(_flash_body, gb=gb, hq=hq, tq=tq, tk=tk, window=window, with_lse=with_lse,
                          merge_groups=merge_groups),
        grid_spec=grid_spec,
        out_shape=out_shape,
        compiler_params=_cparams("parallel", "arbitrary"),
        name=name,
    )(jnp.asarray(qi), jnp.asarray(kj), jnp.asarray(fl), q, k, vt)


def _band_body(q_ref, kp_ref, kc_ref, vp_ref, vc_ref, o_ref, *, tq, window):
    i = pl.program_id(1)
    gb, dv = vp_ref.shape[0], vp_ref.shape[1]
    k2 = jnp.concatenate([kp_ref[...], kc_ref[...]], axis=1)
    st = jnp.einsum("gkd,gqd->gkq", k2, q_ref[...], preferred_element_type=F32)
    krow = lax.broadcasted_iota(jnp.int32, (2 * tq, tq), 0)
    qcol = lax.broadcasted_iota(jnp.int32, (2 * tq, tq), 1)
    dist = qcol + tq - krow
    ok = (dist >= 0) & (dist <= window) & ((krow >= tq) | (i > 0))
    st = jnp.where(ok[None], st, NEG_BIG)
    m = jnp.max(st, axis=1, keepdims=True)
    p = jnp.exp2((st - m).astype(BF16))
    v_ext = jnp.concatenate([jnp.concatenate([vp_ref[...], vc_ref[...]], axis=2),
                             jnp.ones((gb, ONES_ROWS, 2 * tq), BF16)], axis=1)
    acc = jnp.einsum("gdk,gkq->gdq", v_ext, p, preferred_element_type=F32)
    l = acc[:, dv:dv + 1, :]
    o_t = acc[:, :dv, :] / l
    lse = jnp.broadcast_to(m * LN2 + jnp.log(l), o_t.shape)
    for g in range(gb):
        o_ref[g] = jnp.concatenate([o_t[g], lse[g]], axis=0).T.astype(o_ref.dtype)


def _band_attention(q, k, vt, *, tq, gb, window, name):
    g, s, dk = q.shape
    dv = vt.shape[1]
    tq = min(tq, s)
    assert window <= tq and g % gb == 0 and s % tq == 0 and 2 * dv == LANES
    prev = lambda i: jnp.maximum(i - 1, 0)
    return pl.pallas_call(
        functools.partial(_band_body, tq=tq, window=window),
        grid=(g // gb, s // tq),
        in_specs=[pl.BlockSpec((gb, tq, dk), lambda gi, i: (gi, i, 0)),
                  pl.BlockSpec((gb, tq, dk), lambda gi, i: (gi, prev(i), 0)),
                  pl.BlockSpec((gb, tq, dk), lambda gi, i: (gi, i, 0)),
                  pl.BlockSpec((gb, dv, tq), lambda gi, i: (gi, 0, prev(i))),
                  pl.BlockSpec((gb, dv, tq), lambda gi, i: (gi, 0, i))],
        out_specs=pl.BlockSpec((gb, tq, 2 * dv), lambda gi, i: (gi, i, 0)),
        out_shape=jax.ShapeDtypeStruct((g, s, 2 * dv), F32),
        compiler_params=_cparams("parallel", "parallel"),
        name=name,
    )(q, k, k, vt, vt)


def _gelu_tanh(x):
    return 0.5 * x * (1.0 + jnp.tanh(math.sqrt(2.0 / math.pi) * (x + 0.044715 * (x * x * x))))


def _compress_body(r_ref, pos_ref, w1_ref, w2_ref, o_ref, ot_ref):
    half = CMP_STRIDE * HEAD_DIM
    r = r_ref[0, 0]
    w1 = w1_ref[0]
    top = jnp.dot(r, w1[:half], preferred_element_type=F32)
    bot = jnp.dot(r, w1[half:], preferred_element_type=F32)
    nc = r.shape[0]
    posb = jnp.dot(pos_ref[0], w1, preferred_element_type=F32)[:1]
    pre = top + pltpu.roll(bot, nc - 1, 0) + posb
    hid = _gelu_tanh(pre).astype(BF16)
    o = jnp.dot(hid, w2_ref[0], preferred_element_type=F32)
    o_ref[0, 0] = o.astype(o_ref.dtype)
    ot_ref[0, 0] = jnp.concatenate([o, jnp.zeros_like(o)], axis=1).T[:HEAD_DIM].astype(ot_ref.dtype)


def _nsa_compress(kv_rows, pos_flat, w1, w2):
    _, b, nc, _ = kv_rows.shape
    return pl.pallas_call(
        _compress_body,
        grid=(2, b),
        in_specs=[pl.BlockSpec((1, 1, nc, 1024), lambda w, bi: (w, bi, 0, 0)),
                  pl.BlockSpec((1, 8, 2048), lambda w, bi: (w, 0, 0)),
                  pl.BlockSpec((1, 2048, 256), lambda w, bi: (w, 0, 0)),
                  pl.BlockSpec((1, 256, 64), lambda w, bi: (w, 0, 0))],
        out_specs=[pl.BlockSpec((1, 1, nc, 64), lambda w, bi: (w, bi, 0, 0)),
                   pl.BlockSpec((1, 1, 64, nc), lambda w, bi: (w, bi, 0, 0))],
        out_shape=[jax.ShapeDtypeStruct((2, b, nc, 64), BF16), jax.ShapeDtypeStruct((2, b, 64, nc), BF16)],
        compiler_params=_cparams("parallel", "parallel"),
        name="nsa_compress",
    )(kv_rows, pos_flat, w1, w2)


def _cmp_select_body(qn_ref, qr_ref, kc_ref, vct_ref, covt_ref, o_ref, qa_ref, *, tq, n_sel, k_top):
    ncp = kc_ref.shape[2]
    qpos = pl.program_id(1) * tq + lax.broadcasted_iota(jnp.int32, (ncp, tq), 1)
    blk_end = lax.broadcasted_iota(jnp.int32, (ncp, tq), 0) * CMP_STRIDE + (CMP_LEN - 1)
    mask = blk_end <= qpos
    kc, vct = kc_ref[0, 0], vct_ref[0, 0]
    p_sum = jnp.zeros((ncp, tq), F32)
    outs = []
    for h in range(GROUP_HEADS):
        st = lax.dot_general(kc, qn_ref[0, h], (((1,), (1,)), ((), ())), preferred_element_type=F32)
        st = jnp.where(mask, st, NEG_BIG)
        mx = jnp.max(st, axis=0, keepdims=True)
        mx = jnp.where(mx > 0.5 * NEG_BIG, mx, 0.0)
        e = jnp.where(mask, jnp.exp2(st - mx), 0.0)
        p = e / jnp.maximum(jnp.sum(e, axis=0, keepdims=True), 1e-30)
        outs.append(jnp.dot(vct, p.astype(BF16), preferred_element_type=F32))
        p_sum = p_sum + p
    o_ref[0] = jnp.concatenate(outs, axis=0).T.astype(o_ref.dtype)
    p_hi = p_sum.astype(BF16)
    p_lo = (p_sum - p_hi.astype(F32)).astype(BF16)
    covt = covt_ref[...]
    imp_t = (jnp.dot(covt, p_hi, preferred_element_type=F32) + jnp.dot(covt, p_lo, preferred_element_type=F32))
    nblk = SEL_BLOCK
    tpos = pl.program_id(1) * tq + lax.broadcasted_iota(jnp.int32, (nblk, tq), 1)
    cur = tpos >> SEL_SHIFT
    j = lax.broadcasted_iota(jnp.int32, (nblk, tq), 0)
    valid = j <= cur
    forced = (j == 0) | (j == cur) | (j == cur - 1)
    score = jnp.where(valid, jnp.where(forced, FORCED_SCORE, imp_t), -jnp.inf)
    sub = 8
    groups = [score[g * sub:(g + 1) * sub] for g in range(nblk // sub)]
    ranks = [jnp.zeros((sub, tq), jnp.int32) for _ in groups]
    jj = lax.broadcasted_iota(jnp.int32, (sub, tq), 0)
    for i in range(n_sel):
        row = jnp.broadcast_to(score[i:i + 1], (sub, tq))
        for g, sg in enumerate(groups):
            if i < g * sub:
                ahead = row >= sg
            elif i >= (g + 1) * sub:
                ahead = row > sg
            else:
                ahead = (row > sg) | ((row == sg) & (i - g * sub < jj))
            ranks[g] = ranks[g] + ahead.astype(jnp.int32)
    rank = jnp.concatenate(ranks, axis=0)
    bias_t = jnp.where(valid & (rank < k_top), 0.0, SEL_BIAS)
    bias = jnp.concatenate([bias_t, jnp.zeros((LANES - nblk, tq), F32)], axis=0).T[:, :nblk].astype(BF16)
    for h in range(GROUP_HEADS):
        qa_ref[0, h] = jnp.concatenate([qr_ref[0, h], bias], axis=-1)


def _cover_matrix(ncp, n_sel, lanes):
    c_start = np.arange(ncp)[:, None] * CMP_STRIDE
    s_start = np.arange(lanes)[None, :] * SEL_BLOCK
    cov = np.clip(np.minimum(c_start + CMP_LEN, s_start + SEL_BLOCK) - np.maximum(c_start, s_start), 0, None)
    cov = cov.astype(np.float32) / CMP_LEN
    cov[:, n_sel:] = 0.0
    cov[ncp - 1:, :] = 0.0
    return cov


def _cmp_select(qn, qr, kvc, kvct, tq):
    b, _, s, _ = qn.shape
    ncp = kvc.shape[2]
    n_sel = s // SEL_BLOCK
    assert n_sel <= SEL_BLOCK
    k_top = min(SEL_TOPN, n_sel)
    covt = jnp.asarray(_cover_matrix(ncp, n_sel, SEL_BLOCK).T, BF16)
    h4 = lambda d: pl.BlockSpec((1, 4, tq, d), lambda bi, i: (bi, 0, i, 0))
    return pl.pallas_call(
        functools.partial(_cmp_select_body, tq=tq, n_sel=n_sel, k_top=k_top),
        grid=(b, s // tq),
        in_specs=[h4(64), h4(64),
                  pl.BlockSpec((1, 1, ncp, 64), lambda bi, i: (0, bi, 0, 0)),
                  pl.BlockSpec((1, 1, 64, ncp), lambda bi, i: (1, bi, 0, 0)),
                  pl.BlockSpec(covt.shape, lambda bi, i: (0, 0))],
        out_specs=[pl.BlockSpec((1, tq, 256), lambda bi, i: (bi, i, 0)), h4(128)],
        out_shape=[jax.ShapeDtypeStruct((b, s, 256), BF16), jax.ShapeDtypeStruct((b, 4, s, 128), BF16)],
        compiler_params=_cparams("parallel", "parallel"),
        name="nsa_cmp_select",
    )(qn, qr, kvc, kvct, covt)


def _merge_a(o_ref, lam_ref, g_ref, out_scale):
    outs = []
    for h in range(GROUP_HEADS):
        o = o_ref[0, h, 0].astype(F32) - lam_ref[...] * o_ref[0, h, 1].astype(F32)
        outs.append(_rms(o, g_ref[...]) * out_scale)
    return outs


def _merge_b(oc_ref, os_ref, ow_ref, g_ref):
    g = g_ref[0]
    outs = []
    for h in range(GROUP_HEADS):
        outs.append(g[:, 3 * h:3 * h + 1] * oc_ref[0, :, h * 64:(h + 1) * 64].astype(F32)
                    + g[:, 3 * h + 1:3 * h + 2] * os_ref[0, h].astype(F32)
                    + g[:, 3 * h + 2:3 * h + 3] * ow_ref[0, h].astype(F32))
    return outs


def _merge_d(o1_ref, o2_ref, o3_ref):
    outs = []
    for h in range(GROUP_HEADS):
        e = [r[0, h] for r in (o1_ref, o2_ref, o3_ref)]
        lse = [x[:, 64:] for x in e]
        mx = jnp.maximum(jnp.maximum(lse[0], lse[1]), lse[2])
        w = [jnp.exp(x - mx) for x in lse]
        den = w[0] + w[1] + w[2]
        outs.append((w[0] * e[0][:, :64] + w[1] * e[1][:, :64] + w[2] * e[2][:, :64]) / den)
    return outs


def _out_proj_body(x_ref, oa_ref, lam_ref, sub_ref, oc_ref, os_ref, ow_ref, g_ref, yc_ref, d1_ref, d2_ref, d3_ref,
                   w_ref, o_ref, *, out_scale):
    parts = (_merge_a(oa_ref, lam_ref, sub_ref, out_scale) + _merge_b(oc_ref, os_ref, ow_ref, g_ref)
             + [yc_ref[0].astype(F32)] + _merge_d(d1_ref, d2_ref, d3_ref))
    y = jnp.concatenate(parts, axis=-1).astype(BF16)
    o_ref[0] = x_ref[0] + jnp.dot(y, w_ref[...], preferred_element_type=F32)


def _out_proj(x, oa, lam, subln, out_scale, o_cmp, o_sel, o_win, gates, y_c, o_d, w, ts):
    b, s, d = x.shape
    row = lambda dd: pl.BlockSpec((1, ts, dd), lambda bi, i: (bi, i, 0))
    h4 = lambda dd: pl.BlockSpec((1, 4, ts, dd), lambda bi, i: (bi, 0, i, 0))
    vec = pl.BlockSpec((1, 64), lambda bi, i: (0, 0))
    return pl.pallas_call(
        functools.partial(_out_proj_body, out_scale=out_scale),
        grid=(b, s // ts),
        in_specs=[row(d), pl.BlockSpec((1, 4, 2, ts, 64), lambda bi, i: (bi, 0, 0, i, 0)), vec, vec,
                  row(256), h4(64), h4(64), row(128), row(256), h4(128), h4(128), h4(128),
                  pl.BlockSpec(w.shape, lambda bi, i: (0, 0))],
        out_specs=row(d),
        out_shape=jax.ShapeDtypeStruct((b, s, d), F32),
        compiler_params=_cparams("parallel", "parallel"),
        name="w_out_merge",
    )(x, oa, lam, subln, o_cmp, o_sel, o_win, gates, y_c, *o_d, w)


def _permute_w_in_body(w_ref, o_ref):
    w = w_ref[0]
    d = w.shape[0]
    b0, nq = 768, 256
    seg = lambda i: w[:, b0 + nq + i * 64: b0 + nq + (i + 1) * 64]
    kc, vc, ks, vs, kw, vw = (seg(i) for i in range(6))
    gb = w[:, b0 + nq + 384: b0 + nq + 384 + 12]
    c0 = b0 + 652
    d0 = c0 + 544
    out = jnp.concatenate([w[:, :b0], w[:, b0:b0 + nq], ks, kw, kc, vc, vs, vw, gb, jnp.zeros((d, 116), F32),
                           w[:, c0:c0 + 544], jnp.zeros((d, 96), F32), w[:, d0:d0 + 768]], axis=1)
    o_ref[0] = out.astype(o_ref.dtype)


def _permute_w_in(w_in):
    l, d, n = w_in.shape
    n_out = sum(Z_SPLITS)
    return pl.pallas_call(
        _permute_w_in_body,
        grid=(l,),
        in_specs=[pl.BlockSpec((1, d, n), lambda i: (i, 0, 0))],
        out_specs=pl.BlockSpec((1, d, n_out), lambda i: (i, 0, 0)),
        out_shape=jax.ShapeDtypeStruct((l, d, n_out), BF16),
        compiler_params=_cparams("parallel"),
        name="w_in_permute",
    )(w_in)


def _residue_classes(t, dil):
    g, s, d = t.shape
    return t.reshape(g, s // dil, dil, d).transpose(0, 2, 1, 3).reshape(g * dil, s // dil, d)


def _residue_classes_t(t, dil):
    g, d, s = t.shape
    return t.reshape(g, d, s // dil, dil).transpose(0, 3, 1, 2).reshape(g * dil, d, s // dil)


def _from_residue_classes(t, dil):
    gd, l, d = t.shape
    return t.reshape(gd // dil, dil, l, d).transpose(0, 2, 1, 3).reshape(gd // dil, l * dil, d)


def _token_mixers(x, layer, rope_tabs, norm_g, w_in, w_out, diff_lambda, diff_subln, cmp_pos, cmp_w1, cmp_w2,
                  q_norm, kv_norm, w_uq, w_ukv):
    b, s, d = x.shape
    t = b * s
    ts = min(512, s)
    wq = w_uq.reshape(-1, 4, MLA_NOPE + MLA_ROPE)
    wq = jnp.concatenate([wq[:, :, :MLA_NOPE].reshape(-1, 256), wq[:, :, MLA_NOPE:].reshape(-1, 128)], axis=1)
    wkv = w_ukv.reshape(-1, 4, MLA_NOPE + MLA_DV)
    wkv = jnp.concatenate([wkv[:, :, :MLA_NOPE].reshape(-1, 256), wkv[:, :, MLA_NOPE:].reshape(-1, 256)], axis=1)
    prep_a, prep_b, prep_c, prep_d = _project_prep(
        x, norm_g, w_in, layer, rope_tabs[64], rope_tabs[32], q_norm.reshape(1, -1).astype(F32),
        kv_norm.reshape(1, -1).astype(F32), wq.astype(BF16), wkv.astype(BF16), ts)

    qa, ka, va = prep_a
    oa = _flash(qa.reshape(b * 4, 2, s, 64), ka.reshape(b * 4, s, 64), va.reshape(b * 4, 64, s),
                tq=512, tk=512, gb=math.gcd(b * 4, 8), out_dtype=BF16, name="flash_diff").reshape(b, 4, 2, s, 64)
    lam_init = 0.8 - 0.6 * math.exp(-0.3 * layer)
    lp = diff_lambda.astype(F32)
    lam = jnp.exp(jnp.sum(lp[0] * lp[1])) - jnp.exp(jnp.sum(lp[2] * lp[3])) + lam_init

    qn, qr, k_aug, kw, kc, vc, vs, vw, gates = prep_b
    kv_rows = jnp.stack([kc, vc]).reshape(2, b, s // CMP_STRIDE, CMP_STRIDE * HEAD_DIM)
    pos_flat = jnp.broadcast_to(cmp_pos.reshape(2, 1, CMP_LEN * HEAD_DIM), (2, 8, CMP_LEN * HEAD_DIM)).astype(BF16)
    kvc, kvct = _nsa_compress(kv_rows, pos_flat, cmp_w1.astype(BF16), cmp_w2.astype(BF16))
    o_cmp, q_aug = _cmp_select(qn, qr, kvc, kvct, min(256, s))
    gb_b = math.gcd(b, 4)
    o_sel = _flash(q_aug, k_aug, vs, tq=256, tk=512, gb=gb_b, out_dtype=BF16, name="flash_sel")
    o_win = _flash(qr, kw, vw, tq=256, tk=256, gb=gb_b, window=NSA_WINDOW - 1, out_dtype=BF16, name="flash_win")

    qc, kcat, vcv = prep_c
    dk = MLA_NOPE + MLA_ROPE
    y_c = _flash(qc.reshape(b * 4, 1, s, dk), kcat.reshape(b * 4, s, dk), vcv.reshape(b * 4, 64, s),
                 tq=512, tk=512, gb=math.gcd(b * 4, 16), merge_groups=4, out_dtype=BF16, name="flash_mla")

    qd, kd, vd = prep_d
    qd, kd, vd = qd.reshape(b * 4, s, 64), kd.reshape(b * 4, s, 64), vd.reshape(b * 4, 64, s)
    o_d = []
    for window, dil in DILATED_PATTERNS:
        l = s // dil
        qq, kk, vv = _residue_classes(qd, dil), _residue_classes(kd, dil), _residue_classes_t(vd, dil)
        g = qq.shape[0]
        gb = math.gcd(g, 32)
        o = _band_attention(qq, kk, vv, tq=128, gb=gb, window=window // dil, name=f"band_dil{dil}")
        o_d.append(_from_residue_classes(o, dil).reshape(b, 4, s, 128))
    return _out_proj(x, oa, jnp.full((1, 64), lam, F32), diff_subln.reshape(1, 64).astype(F32), 1.0 - lam_init,
                     o_cmp, o_sel, o_win, gates, y_c, o_d, w_out.astype(BF16), ts)


def _cross_body(x_ref, g_ref, wq_ref, k_ref, v_ref, wo_ref, o_ref):
    x = x_ref[0]
    h = _rms(x, g_ref[...]).astype(BF16)
    q = jnp.dot(h, wq_ref[...], preferred_element_type=F32).astype(BF16)
    dh = q.shape[-1] // CROSS_HEADS
    outs = []
    for hd in range(CROSS_HEADS):
        sl = slice(hd * dh, (hd + 1) * dh)
        s = lax.dot_general(q[:, sl], k_ref[0, :, sl], (((1,), (1,)), ((), ())), preferred_element_type=F32)
        e = jnp.exp(s - jnp.max(s, axis=-1, keepdims=True))
        p = e / jnp.sum(e, axis=-1, keepdims=True)
        outs.append(jnp.dot(p.astype(BF16), v_ref[0, :, sl], preferred_element_type=F32))
    o = jnp.concatenate(outs, axis=-1).astype(BF16)
    o_ref[0] = x + jnp.dot(o, wo_ref[...], preferred_element_type=F32)


def _cross_attention(x, mem, g_x, g_mem, wq, wk, wv, wo):
    b, s, d = x.shape
    m = mem.shape[1]
    ts = min(512, s)
    dh = d // CROSS_HEADS
    k, v = _mm([mem.reshape(b * m, d)], jnp.concatenate([wk, wv], axis=1).astype(BF16), gain=g_mem,
               tm=min(512, b * m), splits=(d, d), out_dtype=BF16, name="cross_kv")
    full = lambda a: pl.BlockSpec(a.shape, lambda bi, i: (0,) * a.ndim)
    wq_s = (wq * dh ** -0.5).astype(BF16)
    wo_b = wo.astype(BF16)
    g2 = g_x.reshape(1, d).astype(F32)
    return pl.pallas_call(
        _cross_body,
        grid=(b, s // ts),
        in_specs=[pl.BlockSpec((1, ts, d), lambda bi, i: (bi, i, 0)), full(g2), full(wq_s),
                  pl.BlockSpec((1, m, d), lambda bi, i: (bi, 0, 0)),
                  pl.BlockSpec((1, m, d), lambda bi, i: (bi, 0, 0)), full(wo_b)],
        out_specs=pl.BlockSpec((1, ts, d), lambda bi, i: (bi, i, 0)),
        out_shape=jax.ShapeDtypeStruct((b, s, d), F32),
        compiler_params=_cparams("parallel", "parallel"),
        name="cross_attn",
    )(x, g2, wq_s, k.reshape(b, m, d), v.reshape(b, m, d), wo_b)


def _swiglu_body(x_ref, g_ref, wg_ref, wu_ref, wd_ref, o_ref, h_sc, acc_sc):
    f = pl.program_id(1)

    @pl.when(f == 0)
    def _():
        h_sc[...] = _rms(x_ref[...], g_ref[...]).astype(BF16)
        acc_sc[...] = x_ref[...]

    h = h_sc[...]
    gate = jnp.dot(h, wg_ref[...], preferred_element_type=F32)
    up = jnp.dot(h, wu_ref[...], preferred_element_type=F32)
    act = (gate * jax.nn.sigmoid(gate) * up).astype(BF16)
    acc_sc[...] += jnp.dot(act, wd_ref[...], preferred_element_type=F32)

    @pl.when(f == pl.num_programs(1) - 1)
    def _():
        o_ref[...] = acc_sc[...]


def _swiglu(x, gain, wg, wu, wd, tm=512, tf=1408):
    t, d = x.shape
    ff = wg.shape[1]
    tm, tf = min(tm, t), min(tf, ff)
    assert t % tm == 0 and ff % tf == 0
    return pl.pallas_call(
        _swiglu_body,
        grid=(t // tm, ff // tf),
        in_specs=[pl.BlockSpec((tm, d), lambda i, f: (i, 0)),
                  pl.BlockSpec((1, d), lambda i, f: (0, 0)),
                  pl.BlockSpec((d, tf), lambda i, f: (0, f)),
                  pl.BlockSpec((d, tf), lambda i, f: (0, f)),
                  pl.BlockSpec((tf, d), lambda i, f: (f, 0))],
        out_specs=pl.BlockSpec((tm, d), lambda i, f: (i, 0)),
        out_shape=jax.ShapeDtypeStruct((t, d), F32),
        scratch_shapes=[pltpu.VMEM((tm, d), BF16), pltpu.VMEM((tm, d), F32)],
        compiler_params=_cparams("parallel", "arbitrary"),
        name="swiglu",
    )(x, gain.reshape(1, d).astype(F32), wg.astype(BF16), wu.astype(BF16), wd.astype(BF16))


def _router_body(x_ref, g_ref, r_ref, h_ref, rt_ref):
    h = _rms(x_ref[...], g_ref[...])
    h_ref[...] = h
    h_hi = h.astype(BF16)
    h_lo = (h - h_hi.astype(F32)).astype(BF16)
    r = r_ref[...]
    r_hi = r.astype(BF16)
    r_lo = (r - r_hi.astype(F32)).astype(BF16)
    logits = (jnp.dot(h_hi, r_hi, preferred_element_type=F32) + jnp.dot(h_lo, r_hi, preferred_element_type=F32)
              + jnp.dot(h_hi, r_lo, preferred_element_type=F32))
    lane = lax.broadcasted_iota(jnp.int32, logits.shape, 1)
    lg = jnp.where(lane < N_EXPERTS, logits, -jnp.inf)
    v0 = jnp.max(lg, axis=-1, keepdims=True)
    e0 = jnp.min(jnp.where(lg == v0, lane, LANES), axis=-1, keepdims=True)
    lg1 = jnp.where(lane == e0, -jnp.inf, lg)
    v1 = jnp.max(lg1, axis=-1, keepdims=True)
    e1 = jnp.min(jnp.where(lg1 == v1, lane, LANES), axis=-1, keepdims=True)
    g1 = 1.0 / (1.0 + jnp.exp(v0 - v1))
    g0 = 1.0 - g1
    out = jnp.where(lane == 0, g0, jnp.where(lane == 1, g1, 0.0))
    out = jnp.where(lane == 2, e0.astype(F32), jnp.where(lane == 3, e1.astype(F32), out))
    rt_ref[...] = out


def _router(x, gain, router, tm=512):
    t, d = x.shape
    tm = min(tm, t)
    r_pad = jnp.zeros((d, LANES), F32).at[:, :N_EXPERTS].set(router.astype(F32))
    return pl.pallas_call(
        _router_body,
        grid=(t // tm,),
        in_specs=[pl.BlockSpec((tm, d), lambda i: (i, 0)), pl.BlockSpec((1, d), lambda i: (0, 0)),
                  pl.BlockSpec((d, LANES), lambda i: (0, 0))],
        out_specs=[pl.BlockSpec((tm, d), lambda i: (i, 0)), pl.BlockSpec((tm, LANES), lambda i: (i, 0))],
        out_shape=[jax.ShapeDtypeStruct((t, d), F32), jax.ShapeDtypeStruct((t, LANES), F32)],
        compiler_params=_cparams("parallel"),
        name="moe_router",
    )(x, gain.reshape(1, d).astype(F32), r_pad)


def _row_copy(src_hbm, src_row, dst_ref, dst_row, sem):
    return pltpu.make_async_copy(src_hbm.at[pl.ds(src_row, 1)], dst_ref.at[pl.ds(dst_row, 1)], sem)


def _rank_body(rt_ref, u_ref, ps_ref, dest_ref, carry_sc):
    k, i = pl.program_id(0), pl.program_id(1)
    tm = rt_ref.shape[0]

    @pl.when((k == 0) & (i == 0))
    def _():
        carry_sc[...] = jnp.zeros(carry_sc.shape, F32)

    rt_t = rt_ref[...].T
    e_row = jnp.where(k == 0, rt_t[2:3], rt_t[3:4]).astype(jnp.int32)
    sub = lax.broadcasted_iota(jnp.int32, (N_EXPERTS, tm), 0)
    onehot = sub == e_row
    ahead = jnp.dot(onehot.astype(BF16), u_ref[...], preferred_element_type=F32)
    row = ahead + carry_sc[:, :1] + ps_ref[:, :1]
    dest = jnp.sum(jnp.where(onehot, row, 0.0), axis=0, keepdims=True)
    dest_ref[...] = dest.astype(jnp.int32).reshape(dest_ref.shape)
    carry_sc[...] += jnp.sum(onehot.astype(F32), axis=1, keepdims=True)


def _assignment_rows(routing, pstart, tm=512):
    t = routing.shape[0]
    tm = min(tm, t)
    upper = jnp.asarray(np.triu(np.ones((tm, tm), np.float32), 1), BF16)
    ps = jnp.broadcast_to(pstart.astype(F32)[:, None], (N_EXPERTS, LANES))
    out = pl.pallas_call(
        _rank_body,
        grid=(TOP_K, t // tm),
        in_specs=[pl.BlockSpec((tm, LANES), lambda k, i: (i, 0)),
                  pl.BlockSpec((tm, tm), lambda k, i: (0, 0)),
                  pl.BlockSpec((N_EXPERTS, LANES), lambda k, i: (0, 0))],
        out_specs=pl.BlockSpec((1, 1, 1, tm), lambda k, i: (k, i, 0, 0)),
        out_shape=jax.ShapeDtypeStruct((TOP_K, t // tm, 1, tm), jnp.int32),
        scratch_shapes=[pltpu.VMEM((N_EXPERTS, LANES), F32)],
        compiler_params=_cparams("arbitrary", "arbitrary"),
        name="moe_rank",
    )(routing, upper, ps)
    return out.reshape(TOP_K * t)


def _dispatch_body(dest_ref, h_ref, xs_in, xs_hbm, sem):
    del xs_in
    tm = h_ref.shape[0]
    t = dest_ref.shape[0] // TOP_K
    base = pl.program_id(0) * tm

    def copy(r, k, dst_row):
        return pltpu.make_async_copy(h_ref.at[pl.ds(r, 1)], xs_hbm.at[pl.ds(dst_row, 1)], sem)

    def start(r, c):
        for k in range(TOP_K):
            copy(r, k, dest_ref[k * t + base + r]).start()
        return c

    lax.fori_loop(0, tm, start, 0, unroll=8)
    for r in range(tm):
        for k in range(TOP_K):
            copy(r, k, 0).wait()


def _dispatch(h, dest, n_rows, xs_init, tm=256):
    t, d = h.shape
    tm = min(tm, t)
    grid_spec = pltpu.PrefetchScalarGridSpec(
        num_scalar_prefetch=1,
        grid=(t // tm,),
        in_specs=[pl.BlockSpec((tm, d), lambda i, dest: (i, 0)), pl.BlockSpec(memory_space=pl.ANY)],
        out_specs=pl.BlockSpec(memory_space=pl.ANY),
        scratch_shapes=[pltpu.SemaphoreType.DMA(())],
    )
    return pl.pallas_call(
        _dispatch_body,
        grid_spec=grid_spec,
        out_shape=jax.ShapeDtypeStruct((n_rows, d), F32),
        input_output_aliases={2: 0},
        compiler_params=_cparams("arbitrary"),
        name="moe_dispatch",
    )(dest, h, jnp.zeros((n_rows, d), F32) if xs_init is None else xs_init)


def _experts_body(be_ref, nb_ref, x_ref, wg_ref, wu_ref, wd_ref, o_ref, xb_sc, acc_sc):
    i, f = pl.program_id(0), pl.program_id(1)

    @pl.when(i < nb_ref[0])
    def _():
        @pl.when(f == 0)
        def _():
            xb_sc[...] = x_ref[...].astype(BF16)
            acc_sc[...] = jnp.zeros(acc_sc.shape, F32)

        x = xb_sc[...]
        gate = jnp.dot(x, wg_ref[0, 0], preferred_element_type=F32)
        up = jnp.dot(x, wu_ref[0, 0], preferred_element_type=F32)
        act = (gate * jax.nn.sigmoid(gate) * up).astype(BF16)
        acc_sc[...] += jnp.dot(act, wd_ref[0, 0], preferred_element_type=F32)

        @pl.when(f == pl.num_programs(1) - 1)
        def _():
            o_ref[...] = acc_sc[...]

    @pl.when(i >= nb_ref[0])
    def _():
        o_ref[...] = jnp.zeros(o_ref.shape, F32)


def _experts(xs, blk_e, n_used, wg, wu, wd, layer, rows, tf=1792):
    n_rows, d = xs.shape
    ff = wg.shape[3]
    n_blk = n_rows // rows
    last = ff // tf - 1
    grid_spec = pltpu.PrefetchScalarGridSpec(
        num_scalar_prefetch=2,
        grid=(n_blk, ff // tf),
        in_specs=[pl.BlockSpec((rows, d), lambda i, f, be, nb: (jnp.minimum(i, nb[0] - 1), 0)),
                  pl.BlockSpec((1, 1, d, tf), lambda i, f, be, nb: (layer, be[i], 0, jnp.where(i < nb[0], f, last))),
                  pl.BlockSpec((1, 1, d, tf), lambda i, f, be, nb: (layer, be[i], 0, jnp.where(i < nb[0], f, last))),
                  pl.BlockSpec((1, 1, tf, d), lambda i, f, be, nb: (layer, be[i], jnp.where(i < nb[0], f, last), 0))],
        out_specs=pl.BlockSpec((rows, d), lambda i, f, be, nb: (i, 0)),
        scratch_shapes=[pltpu.VMEM((rows, d), BF16), pltpu.VMEM((rows, d), F32)],
    )
    return pl.pallas_call(
        _experts_body,
        grid_spec=grid_spec,
        out_shape=jax.ShapeDtypeStruct((n_rows, d), F32),
        compiler_params=_cparams("arbitrary", "arbitrary"),
        name="moe_experts",
    )(blk_e, n_used, xs, wg, wu, wd)


def _combine_body(dest_ref, x_ref, rt_ref, ys_hbm, o_ref, buf, sem):
    tm = x_ref.shape[0]
    n_tok = dest_ref.shape[0] // TOP_K
    i = pl.program_id(0)
    slot = i & 1

    def start(tile, s):
        def body(r, c):
            for k in range(TOP_K):
                _row_copy(ys_hbm, dest_ref[k * n_tok + tile * tm + r], buf.at[s, k], r, sem.at[s]).start()
            return c
        lax.fori_loop(0, tm, body, 0, unroll=8)

    @pl.when(i == 0)
    def _():
        start(0, 0)

    @pl.when(i + 1 < pl.num_programs(0))
    def _():
        start(i + 1, 1 - slot)

    for r in range(tm):
        for k in range(TOP_K):
            _row_copy(ys_hbm, 0, buf.at[slot, k], r, sem.at[slot]).wait()
    rt = rt_ref[...]
    o_ref[...] = x_ref[...] + rt[:, 0:1] * buf[slot, 0] + rt[:, 1:2] * buf[slot, 1]


def _combine(x, routing, ys, dest, tm=256):
    t, d = x.shape
    tm = min(tm, t)
    grid_spec = pltpu.PrefetchScalarGridSpec(
        num_scalar_prefetch=1,
        grid=(t // tm,),
        in_specs=[pl.BlockSpec((tm, d), lambda i, dest: (i, 0)),
                  pl.BlockSpec((tm, LANES), lambda i, dest: (i, 0)),
                  pl.BlockSpec(memory_space=pl.ANY)],
        out_specs=pl.BlockSpec((tm, d), lambda i, dest: (i, 0)),
        scratch_shapes=[pltpu.VMEM((2, TOP_K, tm, d), F32), pltpu.SemaphoreType.DMA((2,))],
    )
    return pl.pallas_call(
        _combine_body,
        grid_spec=grid_spec,
        out_shape=jax.ShapeDtypeStruct((t, d), F32),
        compiler_params=_cparams("arbitrary"),
        name="moe_combine",
    )(dest, x, routing, ys)


MOE_ROWS = 512


def _moe(x, gain, router, w_gate, w_up, w_down, layer, xs_init=None):
    t, d = x.shape
    h, routing = _router(x, gain, router)
    n_assign = t * TOP_K
    experts = routing[:, 2:2 + TOP_K].astype(jnp.int32)
    counts = jnp.sum((experts[:, :, None] == jnp.arange(N_EXPERTS)[None, None, :]).astype(jnp.int32), axis=(0, 1))
    padded = (counts + MOE_ROWS - 1) // MOE_ROWS * MOE_ROWS
    pend = jnp.cumsum(padded)
    pstart = pend - padded
    n_rows = -(-n_assign // MOE_ROWS) * MOE_ROWS + N_EXPERTS * MOE_ROWS
    n_blk = n_rows // MOE_ROWS
    blk_start = jnp.arange(n_blk, dtype=jnp.int32) * MOE_ROWS
    blk_e = jnp.minimum(jnp.sum((pend[None, :] <= blk_start[:, None]).astype(jnp.int32), axis=1), N_EXPERTS - 1)
    n_used = (pend[-1] // MOE_ROWS).astype(jnp.int32).reshape(1)
    dest = _assignment_rows(routing, pstart)
    xs = _dispatch(h, dest, n_rows, xs_init)
    ys = _experts(xs, blk_e.astype(jnp.int32), n_used, w_gate, w_up, w_down, layer, MOE_ROWS)
    return _combine(x, routing, ys, dest), xs


def _final_norm_body(x_ref, g_ref, o_ref):
    o_ref[...] = _rms(x_ref[...], g_ref[...])


def _final_norm(x, gain, tm=1024):
    t, d = x.shape
    tm = min(tm, t)
    return pl.pallas_call(
        _final_norm_body,
        grid=(t // tm,),
        in_specs=[pl.BlockSpec((tm, d), lambda i: (i, 0)), pl.BlockSpec((1, d), lambda i: (0, 0))],
        out_specs=pl.BlockSpec((tm, d), lambda i: (i, 0)),
        out_shape=jax.ShapeDtypeStruct((t, d), F32),
        compiler_params=_cparams("parallel"),
        name="final_norm",
    )(x, gain.reshape(1, d).astype(F32))


def _all_rope_tables(s):
    return {64: _rope_tables(s, 64), 32: _rope_tables(s, 32)}


def kernel(x, mem, norm_mix, w_in, w_out, diff_lambda, diff_subln, nsa_cmp_pos, nsa_cmp_w1, nsa_cmp_w2, mla_q_norm, mla_kv_norm, mla_w_uq, mla_w_ukv, norm_cross, norm_mem, cross_wq, cross_wk, cross_wv, cross_wo, norm_ffn, ffn_w_gate, ffn_w_up, ffn_w_down, moe_router, moe_w_gate, moe_w_up, moe_w_down, final_norm):
    b, s, d = x.shape
    depth = w_in.shape[0]
    tabs = _all_rope_tables(s)
    w_in_p = _permute_w_in(w_in)
    moe_wg, moe_wu, moe_wd = moe_w_gate.astype(BF16), moe_w_up.astype(BF16), moe_w_down.astype(BF16)
    xs_buf = None
    for l in range(depth):
        x = _token_mixers(x, l, tabs, norm_mix[l], w_in_p, w_out[l], diff_lambda[l], diff_subln[l],
                          nsa_cmp_pos[l], nsa_cmp_w1[l], nsa_cmp_w2[l], mla_q_norm[l], mla_kv_norm[l],
                          mla_w_uq[l], mla_w_ukv[l])
        x = _cross_attention(x, mem, norm_cross[l], norm_mem[l], cross_wq[l], cross_wk[l], cross_wv[l],
                             cross_wo[l])
        xt = x.reshape(b * s, d)
        i = l // 2
        if l % 2 == 0:
            xt = _swiglu(xt, norm_ffn[l], ffn_w_gate[i], ffn_w_up[i], ffn_w_down[i])
        else:
            xt, xs_buf = _moe(xt, norm_ffn[l], moe_router[i], moe_wg, moe_wu, moe_wd, i, xs_buf)
        x = xt.reshape(b, s, d)
    return _final_norm(x.reshape(b * s, d), final_norm).reshape(b, s, d)
```

```python
import functools
import math

import numpy as np
import jax
import jax.numpy as jnp
from jax import lax
from jax.experimental import pallas as pl
from jax.experimental.pallas import tpu as pltpu

F32 = jnp.float32
BF16 = jnp.bfloat16

HEAD_DIM = 64
GROUP_HEADS = 4
GROUP_WIDTH = HEAD_DIM * GROUP_HEADS
ROPE_THETA = 10000.0
NORM_EPS = 1e-6
DIFF_DK = HEAD_DIM // 2
CMP_LEN = 32
CMP_STRIDE = 16
SEL_BLOCK = 64
SEL_SHIFT = 6
SEL_TOPN = 16
NSA_WINDOW = 512
FORCED_SCORE = 1e6
MLA_NOPE = 64
MLA_ROPE = 32
MLA_DV = 64
DILATED_PATTERNS = ((128, 1), (512, 4), (2048, 16))
CROSS_HEADS = 4
N_EXPERTS = 8
TOP_K = 2

LANES = 128
V7X_VMEM_BYTES = 64 * 1024 * 1024
VMEM_LIMIT = V7X_VMEM_BYTES * 7 // 8

ONES_ROWS = 16
NEG_BIG = -1e30
SEL_BIAS = -1e9
LOG2E = math.log2(math.e)
LN2 = math.log(2.0)


def _cparams(*sem):
    return pltpu.CompilerParams(dimension_semantics=sem, vmem_limit_bytes=VMEM_LIMIT)


def _rms(xf, gain):
    ms = jnp.mean(xf * xf, axis=-1, keepdims=True)
    return xf * lax.rsqrt(ms + NORM_EPS) * gain


def _mm_body(*refs, nx, norm, res, splits):
    x_refs = refs[:nx]
    pos = nx
    g_ref = refs[pos] if norm else None
    pos += int(norm)
    w_ref = refs[pos]
    pos += 1
    r_ref = refs[pos] if res else None
    pos += int(res)
    o_refs = refs[pos:pos + len(splits)]
    xs_ref = refs[pos + len(splits)]

    @pl.when(pl.program_id(1) == 0)
    def _():
        off = 0
        for xr in x_refs:
            xv = xr[...]
            if norm:
                xv = _rms(xv.astype(F32), g_ref[...])
            kw = xv.shape[-1]
            xs_ref[:, off:off + kw] = xv.astype(BF16)
            off += kw

    acc = jnp.dot(xs_ref[...], w_ref[...], preferred_element_type=F32)
    if res:
        acc = acc + r_ref[...]
    off = 0
    for o_ref, width in zip(o_refs, splits):
        o_ref[...] = acc[:, off:off + width].astype(o_ref.dtype)
        off += width


def _mm(xs, w, *, gain=None, res=None, tm=512, tn=None, splits=None, out_dtype=F32, name="mm"):
    t = xs[0].shape[0]
    k, n = w.shape
    assert sum(x.shape[1] for x in xs) == k
    tn = n if tn is None else tn
    splits = (tn,) if splits is None else tuple(splits)
    assert sum(splits) == tn and (len(splits) == 1 or tn == n)
    tm = min(tm, t)
    assert t % tm == 0 and n % tn == 0
    norm = gain is not None
    assert not norm or len(xs) == 1
    in_specs = [pl.BlockSpec((tm, x.shape[1]), lambda i, j: (i, 0)) for x in xs]
    args = list(xs)
    if norm:
        in_specs.append(pl.BlockSpec((1, k), lambda i, j: (0, 0)))
        args.append(gain.reshape(1, k).astype(F32))
    in_specs.append(pl.BlockSpec((k, tn), lambda i, j: (0, j)))
    args.append(w)
    if res is not None:
        in_specs.append(pl.BlockSpec((tm, tn), lambda i, j: (i, j)))
        args.append(res)
    if len(splits) == 1:
        out_shape = [jax.ShapeDtypeStruct((t, n), out_dtype)]
        out_specs = [pl.BlockSpec((tm, tn), lambda i, j: (i, j))]
    else:
        out_shape = [jax.ShapeDtypeStruct((t, s), out_dtype) for s in splits]
        out_specs = [pl.BlockSpec((tm, s), lambda i, j: (i, 0)) for s in splits]
    outs = pl.pallas_call(
        functools.partial(_mm_body, nx=len(xs), norm=norm, res=res is not None, splits=splits),
        grid=(t // tm, n // tn),
        in_specs=in_specs,
        out_specs=out_specs,
        out_shape=out_shape,
        scratch_shapes=[pltpu.VMEM((tm, k), BF16)],
        compiler_params=_cparams("parallel", "arbitrary"),
        name=name,
    )(*args)
    return outs[0] if len(outs) == 1 else outs


def _rope_tables(s, group):
    half = group // 2
    pos = jnp.arange(s, dtype=F32)
    inv_freq = ROPE_THETA ** (-jnp.arange(half, dtype=F32) / half)
    ang = pos[:, None] * inv_freq[None, :]
    cos, sin = jnp.cos(ang), jnp.sin(ang)
    zero = jnp.zeros_like(sin)
    reps = LANES // group
    c = jnp.tile(jnp.concatenate([cos, cos], -1), (1, reps))
    s1 = jnp.tile(jnp.concatenate([zero, sin], -1), (1, reps))
    s2 = jnp.tile(jnp.concatenate([-sin, zero], -1), (1, reps))
    return c, s1, s2


def _rope(x, tabs, half):
    w = x.shape[-1]
    c, s1, s2 = (jnp.tile(t, (1, w // LANES)) for t in tabs)
    return x * c + pltpu.roll(x, half, 1) * s1 + pltpu.roll(x, w - half, 1) * s2


def _prep_a(z, t32, q_ref, k_ref, vt_ref):
    qk = _rope(z[:, :512], t32, DIFF_DK // 2)
    lane = lax.broadcasted_iota(jnp.int32, (z.shape[0], HEAD_DIM), 1)
    scale = DIFF_DK ** -0.5 * LOG2E
    for h in range(GROUP_HEADS):
        q = qk[:, h * 64:(h + 1) * 64] * scale
        q_ref[0, h, 0] = jnp.where(lane < DIFF_DK, q, 0.0).astype(BF16)
        q_ref[0, h, 1] = jnp.where(lane >= DIFF_DK, q, 0.0).astype(BF16)
        k_ref[0, h] = qk[:, 256 + h * 64:256 + (h + 1) * 64].astype(BF16)
    vt_ref[0] = z[:, 512:768].T.reshape(vt_ref.shape[1:]).astype(BF16)


def _prep_b(z, t64, qn_ref, qr_ref, ka_ref, kw_ref, kc_ref, vc_ref, vs_ref, vw_ref, g_ref):
    ts = z.shape[0]
    scale = HEAD_DIM ** -0.5 * LOG2E
    rp = _rope(z[:, :384], t64, HEAD_DIM // 2)
    for h in range(GROUP_HEADS):
        qn_ref[0, h] = (z[:, h * 64:(h + 1) * 64] * scale).astype(BF16)
        qr_ref[0, h] = (rp[:, h * 64:(h + 1) * 64] * scale).astype(BF16)
    kpos = pl.program_id(0) * ts + lax.broadcasted_iota(jnp.int32, (ts, SEL_BLOCK), 0)
    lane = lax.broadcasted_iota(jnp.int32, (ts, SEL_BLOCK), 1)
    onehot = jnp.where((kpos >> SEL_SHIFT) == lane, 1.0, 0.0)
    ka_ref[0] = jnp.concatenate([rp[:, 256:320], onehot], axis=-1).astype(BF16)
    kw_ref[0] = rp[:, 320:384].astype(BF16)
    kc_ref[0] = z[:, 384:448].astype(BF16)
    vc_ref[0] = z[:, 448:512].astype(BF16)
    vt = z[:, 512:640].T.astype(BF16)
    vs_ref[0] = vt[:HEAD_DIM]
    vw_ref[0] = vt[HEAD_DIM:]
    g_ref[0] = jax.nn.sigmoid(z[:, 640:768])


def _prep_c(z, t32, qg_ref, kg_ref, wq_ref, wkv_ref, q_ref, k_ref, vt_ref):
    scale = (MLA_NOPE + MLA_ROPE) ** -0.5 * LOG2E
    cq = _rms(z[:, :384], qg_ref[...]).astype(BF16)
    ckv = _rms(z[:, 384:512], kg_ref[...]).astype(BF16)
    qc = jnp.dot(cq, wq_ref[...], preferred_element_type=F32)
    kvc = jnp.dot(ckv, wkv_ref[...], preferred_element_type=F32)
    half = MLA_ROPE // 2
    q_rope = _rope(qc[:, 256:384], t32, half)
    k_rope = _rope(z[:, 512:640], t32, half)[:, :MLA_ROPE]
    for h in range(GROUP_HEADS):
        q = jnp.concatenate([qc[:, h * 64:(h + 1) * 64], q_rope[:, h * 32:(h + 1) * 32]], axis=-1)
        q_ref[0, h] = (q * scale).astype(BF16)
        k_ref[0, h] = jnp.concatenate([kvc[:, h * 64:(h + 1) * 64], k_rope], axis=-1).astype(BF16)
    vt_ref[0] = kvc[:, 256:512].T.reshape(vt_ref.shape[1:]).astype(BF16)


def _prep_d(z, t64, q_ref, k_ref, vt_ref):
    scale = HEAD_DIM ** -0.5 * LOG2E
    qk = _rope(z[:, :512], t64, HEAD_DIM // 2)
    for h in range(GROUP_HEADS):
        q_ref[0, h] = (qk[:, h * 64:(h + 1) * 64] * scale).astype(BF16)
        k_ref[0, h] = qk[:, 256 + h * 64:256 + (h + 1) * 64].astype(BF16)
    vt_ref[0] = z[:, 512:768].T.reshape(vt_ref.shape[1:]).astype(BF16)


Z_SPLITS = (768, 768, 640, 768)
N_PREP_OUT = (3, 9, 3, 3)


def _project_prep_body(*refs):
    x_ref, g_ref, w_ref = refs[:3]
    t64 = tuple(r[...] for r in refs[3:6])
    t32 = tuple(r[...] for r in refs[6:9])
    qg_ref, kg_ref, wq_ref, wkv_ref = refs[9:13]
    outs = refs[13:]
    h = _rms(x_ref[0], g_ref[...]).astype(BF16)
    z = jnp.dot(h, w_ref[0], preferred_element_type=F32)
    o0, o1, o2, o3 = np.cumsum((0,) + Z_SPLITS[:3])
    n0, n1, n2 = np.cumsum(N_PREP_OUT[:3])
    _prep_a(z[:, o0:o0 + Z_SPLITS[0]], t32, *outs[:n0])
    _prep_b(z[:, o1:o1 + Z_SPLITS[1]], t64, *outs[n0:n1])
    _prep_c(z[:, o2:o2 + Z_SPLITS[2]], t32, qg_ref, kg_ref, wq_ref, wkv_ref, *outs[n1:n2])
    _prep_d(z[:, o3:o3 + Z_SPLITS[3]], t64, *outs[n2:])


def _project_prep(x, gain, w, layer, tabs64, tabs32, q_gain, kv_gain, wq, wkv, ts):
    b, s, d = x.shape
    dk = MLA_NOPE + MLA_ROPE
    full = lambda a: pl.BlockSpec(a.shape, lambda i, bi: (0,) * a.ndim)
    tspec = pl.BlockSpec((ts, LANES), lambda i, bi: (i, 0))
    h4 = lambda dd: pl.BlockSpec((1, 4, ts, dd), lambda i, bi: (bi, 0, i, 0))
    h4t = pl.BlockSpec((1, 4, 64, ts), lambda i, bi: (bi, 0, 0, i))
    sq = lambda dd: pl.BlockSpec((1, ts, dd), lambda i, bi: (bi, i, 0))
    t64 = pl.BlockSpec((1, 64, ts), lambda i, bi: (bi, 0, i))
    sd = lambda *shape, dt=BF16: jax.ShapeDtypeStruct(shape, dt)
    out_specs = [pl.BlockSpec((1, 4, 2, ts, 64), lambda i, bi: (bi, 0, 0, i, 0)), h4(64), h4t,
                 h4(64), h4(64), sq(128), sq(64), sq(64), sq(64), t64, t64, sq(128),
                 h4(dk), h4(dk), h4t,
                 h4(64), h4(64), h4t]
    out_shape = [sd(b, 4, 2, s, 64), sd(b, 4, s, 64), sd(b, 4, 64, s),
                 sd(b, 4, s, 64), sd(b, 4, s, 64), sd(b, s, 128), sd(b, s, 64), sd(b, s, 64), sd(b, s, 64),
                 sd(b, 64, s), sd(b, 64, s), sd(b, s, 128, dt=F32),
                 sd(b, 4, s, dk), sd(b, 4, s, dk), sd(b, 4, 64, s),
                 sd(b, 4, s, 64), sd(b, 4, s, 64), sd(b, 4, 64, s)]
    g2 = gain.reshape(1, d).astype(F32)
    outs = pl.pallas_call(
        _project_prep_body,
        grid=(s // ts, b),
        in_specs=[pl.BlockSpec((1, ts, d), lambda i, bi: (bi, i, 0)), full(g2),
                  pl.BlockSpec((1,) + w.shape[1:], lambda i, bi: (layer, 0, 0))] + [tspec] * 6
                 + [full(q_gain), full(kv_gain), full(wq), full(wkv)],
        out_specs=out_specs,
        out_shape=out_shape,
        compiler_params=_cparams("parallel", "parallel"),
        name="w_in_prep",
    )(x, g2, w, *tabs64, *tabs32, q_gain, kv_gain, wq, wkv)
    n0, n1, n2 = np.cumsum(N_PREP_OUT[:3])
    return outs[:n0], outs[n0:n1], outs[n1:n2], outs[n2:]


def _flash_schedule(nq, tq, tk, sk, window):
    qi, kj, fl = [], [], []
    for i in range(nq):
        q_lo, q_hi = i * tq, i * tq + tq - 1
        j_hi = min(q_hi, sk - 1) // tk
        j_lo = 0 if window is None else max(0, q_lo - window) // tk
        for j in range(j_lo, j_hi + 1):
            k_min, k_max = j * tk, j * tk + tk - 1
            full = k_max <= q_lo and (window is None or q_hi - k_min <= window)
            qi.append(i)
            kj.append(j)
            fl.append((1 if j == j_lo else 0) | (2 if j == j_hi else 0) | (0 if full else 4))
    return (np.asarray(qi, np.int32), np.asarray(kj, np.int32), np.asarray(fl, np.int32))


def _flash_body(qi_ref, kj_ref, fl_ref, q_ref, k_ref, vt_ref, o_ref, m_sc, acc_sc, *,
                gb, hq, tq, tk, window, with_lse, merge_groups):
    t = pl.program_id(1)
    fl = fl_ref[t]
    rows = hq * tq
    dv = vt_ref.shape[1]

    @pl.when((fl & 1) != 0)
    def _():
        m_sc[...] = jnp.full(m_sc.shape, NEG_BIG, F32)
        acc_sc[...] = jnp.zeros(acc_sc.shape, F32)

    def step(masked):
        q = q_ref[...].reshape(gb, rows, q_ref.shape[-1])
        st = jnp.einsum("gkd,gqd->gkq", k_ref[...], q, preferred_element_type=F32)
        if masked:
            krow = lax.broadcasted_iota(jnp.int32, (tk, rows), 0)
            qcol = lax.broadcasted_iota(jnp.int32, (tk, rows), 1)
            if hq > 1:
                qcol = qcol & (tq - 1)
            dist = (qi_ref[t] * tq - kj_ref[t] * tk) + qcol - krow
            ok = dist >= 0
            if window is not None:
                ok = ok & (dist <= window)
            st = jnp.where(ok[None], st, NEG_BIG)
        m_prev = m_sc[...]
        m_new = jnp.maximum(m_prev, jnp.max(st, axis=1, keepdims=True))
        alpha = jnp.exp2(m_prev - m_new)
        p = jnp.exp2((st - m_new).astype(BF16))
        v_ext = jnp.concatenate([vt_ref[...], jnp.ones((gb, ONES_ROWS, tk), BF16)], axis=1)
        acc_sc[...] = alpha * acc_sc[...] + jnp.einsum("gdk,gkq->gdq", v_ext, p, preferred_element_type=F32)
        m_sc[...] = m_new

    @pl.when((fl & 4) != 0)
    def _():
        step(True)

    @pl.when((fl & 4) == 0)
    def _():
        step(False)

    @pl.when((fl & 2) != 0)
    def _():
        l = acc_sc[:, dv:dv + 1, :]
        o_t = acc_sc[:, :dv, :] / l
        if merge_groups:
            for n in range(gb // merge_groups):
                heads = o_t[n * merge_groups:(n + 1) * merge_groups]
                o_ref[n] = heads.reshape(merge_groups * dv, rows).T.astype(o_ref.dtype)
            return
        if with_lse:
            extra = jnp.broadcast_to(m_sc[...] * LN2 + jnp.log(l), o_t.shape)
        else:
            extra = jnp.zeros((gb, LANES - dv, rows), F32)
        dvo = o_ref.shape[-1]
        for g in range(gb):
            o = jnp.concatenate([o_t[g], extra[g]], axis=0).T
            o_ref[g] = o[:, :dvo].reshape(o_ref.shape[1:]).astype(o_ref.dtype)


def _flash(q, k, vt, *, tq, tk, gb=1, window=None, with_lse=False, merge_groups=0, out_dtype=F32,
           name="flash"):
    g, hq, sq, dk = q.shape
    _, dv, sk = vt.shape
    tq, tk = min(tq, sq), min(tk, sk)
    assert g % gb == 0 and sq % tq == 0 and sk % tk == 0
    assert tq & (tq - 1) == 0 and 2 * dv == LANES
    qi, kj, fl = _flash_schedule(sq // tq, tq, tk, sk, window)
    dvo = 2 * dv if with_lse else dv
    rows = hq * tq
    if merge_groups:
        mh = merge_groups
        assert hq == 1 and not with_lse and (mh * dv) % LANES == 0 and gb % mh == 0
        out_spec = pl.BlockSpec((gb // mh, tq, mh * dv), lambda gi, t, qi, kj, fl: (gi, qi[t], 0))
        out_shape = jax.ShapeDtypeStruct((g // mh, sq, mh * dv), out_dtype)
    else:
        out_spec = pl.BlockSpec((gb, hq, tq, dvo), lambda gi, t, qi, kj, fl: (gi, 0, qi[t], 0))
        out_shape = jax.ShapeDtypeStruct((g, hq, sq, dvo), out_dtype)
    grid_spec = pltpu.PrefetchScalarGridSpec(
        num_scalar_prefetch=3,
        grid=(g // gb, len(qi)),
        in_specs=[pl.BlockSpec((gb, hq, tq, dk), lambda gi, t, qi, kj, fl: (gi, 0, qi[t], 0)),
                  pl.BlockSpec((gb, tk, dk), lambda gi, t, qi, kj, fl: (gi, kj[t], 0)),
                  pl.BlockSpec((gb, dv, tk), lambda gi, t, qi, kj, fl: (gi, 0, kj[t]))],
        out_specs=out_spec,
        scratch_shapes=[pltpu.VMEM((gb, 1, rows), F32), pltpu.VMEM((gb, dv + ONES_ROWS, rows), F32)],
    )
    return pl.pallas_call(
        functools.partial(_flash_body, gb=gb, hq=hq, tq=tq, tk=tk, window=window, with_lse=with_lse,
                          merge_groups=merge_groups),
        grid_spec=grid_spec,
        out_shape=out_shape,
        compiler_params=_cparams("parallel", "arbitrary"),
        name=name,
    )(jnp.asarray(qi), jnp.asarray(kj), jnp.asarray(fl), q, k, vt)


def _band_body(q_ref, kp_ref, kc_ref, vp_ref, vc_ref, o_ref, *, tq, window):
    i = pl.program_id(1)
    gb, dv = vp_ref.shape[0], vp_ref.shape[1]
    k2 = jnp.concatenate([kp_ref[...], kc_ref[...]], axis=1)
    st = jnp.einsum("gkd,gqd->gkq", k2, q_ref[...], preferred_element_type=F32)
    krow = lax.broadcasted_iota(jnp.int32, (2 * tq, tq), 0)
    qcol = lax.broadcasted_iota(jnp.int32, (2 * tq, tq), 1)
    dist = qcol + tq - krow
    ok = (dist >= 0) & (dist <= window) & ((krow >= tq) | (i > 0))
    st = jnp.where(ok[None], st, NEG_BIG)
    m = jnp.max(st, axis=1, keepdims=True)
    p = jnp.exp2((st - m).astype(BF16))
    v_ext = jnp.concatenate([jnp.concatenate([vp_ref[...], vc_ref[...]], axis=2),
                             jnp.ones((gb, ONES_ROWS, 2 * tq), BF16)], axis=1)
    acc = jnp.einsum("gdk,gkq->gdq", v_ext, p, preferred_element_type=F32)
    l = acc[:, dv:dv + 1, :]
    o_t = acc[:, :dv, :] / l
    lse = jnp.broadcast_to(m * LN2 + jnp.log(l), o_t.shape)
    for g in range(gb):
        o_ref[g] = jnp.concatenate([o_t[g], lse[g]], axis=0).T.astype(o_ref.dtype)


def _band_attention(q, k, vt, *, tq, gb, window, name):
    g, s, dk = q.shape
    dv = vt.shape[1]
    tq = min(tq, s)
    assert window <= tq and g % gb == 0 and s % tq == 0 and 2 * dv == LANES
    prev = lambda i: jnp.maximum(i - 1, 0)
    return pl.pallas_call(
        functools.partial(_band_body, tq=tq, window=window),
        grid=(g // gb, s // tq),
        in_specs=[pl.BlockSpec((gb, tq, dk), lambda gi, i: (gi, i, 0)),
                  pl.BlockSpec((gb, tq, dk), lambda gi, i: (gi, prev(i), 0)),
                  pl.BlockSpec((gb, tq, dk), lambda gi, i: (gi, i, 0)),
                  pl.BlockSpec((gb, dv, tq), lambda gi, i: (gi, 0, prev(i))),
                  pl.BlockSpec((gb, dv, tq), lambda gi, i: (gi, 0, i))],
        out_specs=pl.BlockSpec((gb, tq, 2 * dv), lambda gi, i: (gi, i, 0)),
        out_shape=jax.ShapeDtypeStruct((g, s, 2 * dv), F32),
        compiler_params=_cparams("parallel", "parallel"),
        name=name,
    )(q, k, k, vt, vt)


def _gelu_tanh(x):
    return 0.5 * x * (1.0 + jnp.tanh(math.sqrt(2.0 / math.pi) * (x + 0.044715 * (x * x * x))))


def _compress_body(r_ref, pos_ref, w1_ref, w2_ref, o_ref, ot_ref):
    half = CMP_STRIDE * HEAD_DIM
    r = r_ref[0, 0]
    w1 = w1_ref[0]
    top = jnp.dot(r, w1[:half], preferred_element_type=F32)
    bot = jnp.dot(r, w1[half:], preferred_element_type=F32)
    nc = r.shape[0]
    posb = jnp.dot(pos_ref[0], w1, preferred_element_type=F32)[:1]
    pre = top + pltpu.roll(bot, nc - 1, 0) + posb
    hid = _gelu_tanh(pre).astype(BF16)
    o = jnp.dot(hid, w2_ref[0], preferred_element_type=F32)
    o_ref[0, 0] = o.astype(o_ref.dtype)
    ot_ref[0, 0] = jnp.concatenate([o, jnp.zeros_like(o)], axis=1).T[:HEAD_DIM].astype(ot_ref.dtype)


def _nsa_compress(kv_rows, pos_flat, w1, w2):
    _, b, nc, _ = kv_rows.shape
    return pl.pallas_call(
        _compress_body,
        grid=(2, b),
        in_specs=[pl.BlockSpec((1, 1, nc, 1024), lambda w, bi: (w, bi, 0, 0)),
                  pl.BlockSpec((1, 8, 2048), lambda w, bi: (w, 0, 0)),
                  pl.BlockSpec((1, 2048, 256), lambda w, bi: (w, 0, 0)),
                  pl.BlockSpec((1, 256, 64), lambda w, bi: (w, 0, 0))],
        out_specs=[pl.BlockSpec((1, 1, nc, 64), lambda w, bi: (w, bi, 0, 0)),
                   pl.BlockSpec((1, 1, 64, nc), lambda w, bi: (w, bi, 0, 0))],
        out_shape=[jax.ShapeDtypeStruct((2, b, nc, 64), BF16), jax.ShapeDtypeStruct((2, b, 64, nc), BF16)],
        compiler_params=_cparams("parallel", "parallel"),
        name="nsa_compress",
    )(kv_rows, pos_flat, w1, w2)


def _cmp_select_body(qn_ref, qr_ref, kc_ref, vct_ref, covt_ref, o_ref, qa_ref, *, tq, n_sel, k_top):
    ncp = kc_ref.shape[2]
    qpos = pl.program_id(1) * tq + lax.broadcasted_iota(jnp.int32, (ncp, tq), 1)
    blk_end = lax.broadcasted_iota(jnp.int32, (ncp, tq), 0) * CMP_STRIDE + (CMP_LEN - 1)
    mask = blk_end <= qpos
    kc, vct = kc_ref[0, 0], vct_ref[0, 0]
    p_sum = jnp.zeros((ncp, tq), F32)
    outs = []
    for h in range(GROUP_HEADS):
        st = lax.dot_general(kc, qn_ref[0, h], (((1,), (1,)), ((), ())), preferred_element_type=F32)
        st = jnp.where(mask, st, NEG_BIG)
        mx = jnp.max(st, axis=0, keepdims=True)
        mx = jnp.where(mx > 0.5 * NEG_BIG, mx, 0.0)
        e = jnp.where(mask, jnp.exp2(st - mx), 0.0)
        p = e / jnp.maximum(jnp.sum(e, axis=0, keepdims=True), 1e-30)
        outs.append(jnp.dot(vct, p.astype(BF16), preferred_element_type=F32))
        p_sum = p_sum + p
    o_ref[0] = jnp.concatenate(outs, axis=0).T.astype(o_ref.dtype)
    p_hi = p_sum.astype(BF16)
    p_lo = (p_sum - p_hi.astype(F32)).astype(BF16)
    covt = covt_ref[...]
    imp_t = (jnp.dot(covt, p_hi, preferred_element_type=F32) + jnp.dot(covt, p_lo, preferred_element_type=F32))
    nblk = SEL_BLOCK
    tpos = pl.program_id(1) * tq + lax.broadcasted_iota(jnp.int32, (nblk, tq), 1)
    cur = tpos >> SEL_SHIFT
    j = lax.broadcasted_iota(jnp.int32, (nblk, tq), 0)
    valid = j <= cur
    forced = (j == 0) | (j == cur) | (j == cur - 1)
    score = jnp.where(valid, jnp.where(forced, FORCED_SCORE, imp_t), -jnp.inf)
    sub = 8
    groups = [score[g * sub:(g + 1) * sub] for g in range(nblk // sub)]
    ranks = [jnp.zeros((sub, tq), jnp.int32) for _ in groups]
    jj = lax.broadcasted_iota(jnp.int32, (sub, tq), 0)
    for i in range(n_sel):
        row = jnp.broadcast_to(score[i:i + 1], (sub, tq))
        for g, sg in enumerate(groups):
            if i < g * sub:
                ahead = row >= sg
            elif i >= (g + 1) * sub:
                ahead = row > sg
            else:
                ahead = (row > sg) | ((row == sg) & (i - g * sub < jj))
            ranks[g] = ranks[g] + ahead.astype(jnp.int32)
    rank = jnp.concatenate(ranks, axis=0)
    bias_t = jnp.where(valid & (rank < k_top), 0.0, SEL_BIAS)
    bias = jnp.concatenate([bias_t, jnp.zeros((LANES - nblk, tq), F32)], axis=0).T[:, :nblk].astype(BF16)
    for h in range(GROUP_HEADS):
        qa_ref[0, h] = jnp.concatenate([qr_ref[0, h], bias], axis=-1)


def _cover_matrix(ncp, n_sel, lanes):
    c_start = np.arange(ncp)[:, None] * CMP_STRIDE
    s_start = np.arange(lanes)[None, :] * SEL_BLOCK
    cov = np.clip(np.minimum(c_start + CMP_LEN, s_start + SEL_BLOCK) - np.maximum(c_start, s_start), 0, None)
    cov = cov.astype(np.float32) / CMP_LEN
    cov[:, n_sel:] = 0.0
    cov[ncp - 1:, :] = 0.0
    return cov


def _cmp_select(qn, qr, kvc, kvct, tq):
    b, _, s, _ = qn.shape
    ncp = kvc.shape[2]
    n_sel = s // SEL_BLOCK
    assert n_sel <= SEL_BLOCK
    k_top = min(SEL_TOPN, n_sel)
    covt = jnp.asarray(_cover_matrix(ncp, n_sel, SEL_BLOCK).T, BF16)
    h4 = lambda d: pl.BlockSpec((1, 4, tq, d), lambda bi, i: (bi, 0, i, 0))
    return pl.pallas_call(
        functools.partial(_cmp_select_body, tq=tq, n_sel=n_sel, k_top=k_top),
        grid=(b, s // tq),
        in_specs=[h4(64), h4(64),
                  pl.BlockSpec((1, 1, ncp, 64), lambda bi, i: (0, bi, 0, 0)),
                  pl.BlockSpec((1, 1, 64, ncp), lambda bi, i: (1, bi, 0, 0)),
                  pl.BlockSpec(covt.shape, lambda bi, i: (0, 0))],
        out_specs=[pl.BlockSpec((1, tq, 256), lambda bi, i: (bi, i, 0)), h4(128)],
        out_shape=[jax.ShapeDtypeStruct((b, s, 256), BF16), jax.ShapeDtypeStruct((b, 4, s, 128), BF16)],
        compiler_params=_cparams("parallel", "parallel"),
        name="nsa_cmp_select",
    )(qn, qr, kvc, kvct, covt)


def _merge_a(o_ref, lam_ref, g_ref, out_scale):
    outs = []
    for h in range(GROUP_HEADS):
        o = o_ref[0, h, 0].astype(F32) - lam_ref[...] * o_ref[0, h, 1].astype(F32)
        outs.append(_rms(o, g_ref[...]) * out_scale)
    return outs


def _merge_b(oc_ref, os_ref, ow_ref, g_ref):
    g = g_ref[0]
    outs = []
    for h in range(GROUP_HEADS):
        outs.append(g[:, 3 * h:3 * h + 1] * oc_ref[0, :, h * 64:(h + 1) * 64].astype(F32)
                    + g[:, 3 * h + 1:3 * h + 2] * os_ref[0, h].astype(F32)
                    + g[:, 3 * h + 2:3 * h + 3] * ow_ref[0, h].astype(F32))
    return outs


def _merge_d(o1_ref, o2_ref, o3_ref):
    outs = []
    for h in range(GROUP_HEADS):
        e = [r[0, h] for r in (o1_ref, o2_ref, o3_ref)]
        lse = [x[:, 64:] for x in e]
        mx = jnp.maximum(jnp.maximum(lse[0], lse[1]), lse[2])
        w = [jnp.exp(x - mx) for x in lse]
        den = w[0] + w[1] + w[2]
        outs.append((w[0] * e[0][:, :64] + w[1] * e[1][:, :64] + w[2] * e[2][:, :64]) / den)
    return outs


def _out_proj_body(x_ref, oa_ref, lam_ref, sub_ref, oc_ref, os_ref, ow_ref, g_ref, yc_ref, d1_ref, d2_ref, d3_ref,
                   w_ref, o_ref, *, out_scale):
    parts = (_merge_a(oa_ref, lam_ref, sub_ref, out_scale) + _merge_b(oc_ref, os_ref, ow_ref, g_ref)
             + [yc_ref[0].astype(F32)] + _merge_d(d1_ref, d2_ref, d3_ref))
    y = jnp.concatenate(parts, axis=-1).astype(BF16)
    o_ref[0] = x_ref[0] + jnp.dot(y, w_ref[...], preferred_element_type=F32)


def _out_proj(x, oa, lam, subln, out_scale, o_cmp, o_sel, o_win, gates, y_c, o_d, w, ts):
    b, s, d = x.shape
    row = lambda dd: pl.BlockSpec((1, ts, dd), lambda bi, i: (bi, i, 0))
    h4 = lambda dd: pl.BlockSpec((1, 4, ts, dd), lambda bi, i: (bi, 0, i, 0))
    vec = pl.BlockSpec((1, 64), lambda bi, i: (0, 0))
    return pl.pallas_call(
        functools.partial(_out_proj_body, out_scale=out_scale),
        grid=(b, s // ts),
        in_specs=[row(d), pl.BlockSpec((1, 4, 2, ts, 64), lambda bi, i: (bi, 0, 0, i, 0)), vec, vec,
                  row(256), h4(64), h4(64), row(128), row(256), h4(128), h4(128), h4(128),
                  pl.BlockSpec(w.shape, lambda bi, i: (0, 0))],
        out_specs=row(d),
        out_shape=jax.ShapeDtypeStruct((b, s, d), F32),
        compiler_params=_cparams("parallel", "parallel"),
        name="w_out_merge",
    )(x, oa, lam, subln, o_cmp, o_sel, o_win, gates, y_c, *o_d, w)


def _permute_w_in_body(w_ref, o_ref):
    w = w_ref[0]
    d = w.shape[0]
    b0, nq = 768, 256
    seg = lambda i: w[:, b0 + nq + i * 64: b0 + nq + (i + 1) * 64]
    kc, vc, ks, vs, kw, vw = (seg(i) for i in range(6))
    gb = w[:, b0 + nq + 384: b0 + nq + 384 + 12]
    c0 = b0 + 652
    d0 = c0 + 544
    out = jnp.concatenate([w[:, :b0], w[:, b0:b0 + nq], ks, kw, kc, vc, vs, vw, gb, jnp.zeros((d, 116), F32),
                           w[:, c0:c0 + 544], jnp.zeros((d, 96), F32), w[:, d0:d0 + 768]], axis=1)
    o_ref[0] = out.astype(o_ref.dtype)


def _permute_w_in(w_in):
    l, d, n = w_in.shape
    n_out = sum(Z_SPLITS)
    return pl.pallas_call(
        _permute_w_in_body,
        grid=(l,),
        in_specs=[pl.BlockSpec((1, d, n), lambda i: (i, 0, 0))],
        out_specs=pl.BlockSpec((1, d, n_out), lambda i: (i, 0, 0)),
        out_shape=jax.ShapeDtypeStruct((l, d, n_out), BF16),
        compiler_params=_cparams("parallel"),
        name="w_in_permute",
    )(w_in)


def _residue_classes(t, dil):
    g, s, d = t.shape
    return t.reshape(g, s // dil, dil, d).transpose(0, 2, 1, 3).reshape(g * dil, s // dil, d)


def _residue_classes_t(t, dil):
    g, d, s = t.shape
    return t.reshape(g, d, s // dil, dil).transpose(0, 3, 1, 2).reshape(g * dil, d, s // dil)


def _from_residue_classes(t, dil):
    gd, l, d = t.shape
    return t.reshape(gd // dil, dil, l, d).transpose(0, 2, 1, 3).reshape(gd // dil, l * dil, d)


def _token_mixers(x, layer, rope_tabs, norm_g, w_in, w_out, diff_lambda, diff_subln, cmp_pos, cmp_w1, cmp_w2,
                  q_norm, kv_norm, w_uq, w_ukv):
    b, s, d = x.shape
    t = b * s
    ts = min(512, s)
    wq = w_uq.reshape(-1, 4, MLA_NOPE + MLA_ROPE)
    wq = jnp.concatenate([wq[:, :, :MLA_NOPE].reshape(-1, 256), wq[:, :, MLA_NOPE:].reshape(-1, 128)], axis=1)
    wkv = w_ukv.reshape(-1, 4, MLA_NOPE + MLA_DV)
    wkv = jnp.concatenate([wkv[:, :, :MLA_NOPE].reshape(-1, 256), wkv[:, :, MLA_NOPE:].reshape(-1, 256)], axis=1)
    prep_a, prep_b, prep_c, prep_d = _project_prep(
        x, norm_g, w_in, layer, rope_tabs[64], rope_tabs[32], q_norm.reshape(1, -1).astype(F32),
        kv_norm.reshape(1, -1).astype(F32), wq.astype(BF16), wkv.astype(BF16), ts)

    qa, ka, va = prep_a
    oa = _flash(qa.reshape(b * 4, 2, s, 64), ka.reshape(b * 4, s, 64), va.reshape(b * 4, 64, s),
                tq=512, tk=512, gb=math.gcd(b * 4, 8), out_dtype=BF16, name="flash_diff").reshape(b, 4, 2, s, 64)
    lam_init = 0.8 - 0.6 * math.exp(-0.3 * layer)
    lp = diff_lambda.astype(F32)
    lam = jnp.exp(jnp.sum(lp[0] * lp[1])) - jnp.exp(jnp.sum(lp[2] * lp[3])) + lam_init

    qn, qr, k_aug, kw, kc, vc, vs, vw, gates = prep_b
    kv_rows = jnp.stack([kc, vc]).reshape(2, b, s // CMP_STRIDE, CMP_STRIDE * HEAD_DIM)
    pos_flat = jnp.broadcast_to(cmp_pos.reshape(2, 1, CMP_LEN * HEAD_DIM), (2, 8, CMP_LEN * HEAD_DIM)).astype(BF16)
    kvc, kvct = _nsa_compress(kv_rows, pos_flat, cmp_w1.astype(BF16), cmp_w2.astype(BF16))
    o_cmp, q_aug = _cmp_select(qn, qr, kvc, kvct, min(256, s))
    gb_b = math.gcd(b, 4)
    o_sel = _flash(q_aug, k_aug, vs, tq=256, tk=512, gb=gb_b, out_dtype=BF16, name="flash_sel")
    o_win = _flash(qr, kw, vw, tq=256, tk=256, gb=gb_b, window=NSA_WINDOW - 1, out_dtype=BF16, name="flash_win")

    qc, kcat, vcv = prep_c
    dk = MLA_NOPE + MLA_ROPE
    y_c = _flash(qc.reshape(b * 4, 1, s, dk), kcat.reshape(b * 4, s, dk), vcv.reshape(b * 4, 64, s),
                 tq=512, tk=512, gb=math.gcd(b * 4, 16), merge_groups=4, out_dtype=BF16, name="flash_mla")

    qd, kd, vd = prep_d
    qd, kd, vd = qd.reshape(b * 4, s, 64), kd.reshape(b * 4, s, 64), vd.reshape(b * 4, 64, s)
    o_d = []
    for window, dil in DILATED_PATTERNS:
        l = s // dil
        qq, kk, vv = _residue_classes(qd, dil), _residue_classes(kd, dil), _residue_classes_t(vd, dil)
        g = qq.shape[0]
        gb = math.gcd(g, 32)
        o = _band_attention(qq, kk, vv, tq=128, gb=gb, window=window // dil, name=f"band_dil{dil}")
        o_d.append(_from_residue_classes(o, dil).reshape(b, 4, s, 128))
    return _out_proj(x, oa, jnp.full((1, 64), lam, F32), diff_subln.reshape(1, 64).astype(F32), 1.0 - lam_init,
                     o_cmp, o_sel, o_win, gates, y_c, o_d, w_out.astype(BF16), ts)


def _cross_body(x_ref, g_ref, wq_ref, k_ref, v_ref, wo_ref, o_ref):
    x = x_ref[0]
    h = _rms(x, g_ref[...]).astype(BF16)
    q = jnp.dot(h, wq_ref[...], preferred_element_type=F32).astype(BF16)
    dh = q.shape[-1] // CROSS_HEADS
    outs = []
    for hd in range(CROSS_HEADS):
        sl = slice(hd * dh, (hd + 1) * dh)
        s = lax.dot_general(q[:, sl], k_ref[0, :, sl], (((1,), (1,)), ((), ())), preferred_element_type=F32)
        e = jnp.exp(s - jnp.max(s, axis=-1, keepdims=True))
        p = e / jnp.sum(e, axis=-1, keepdims=True)
        outs.append(jnp.dot(p.astype(BF16), v_ref[0, :, sl], preferred_element_type=F32))
    o = jnp.concatenate(outs, axis=-1).astype(BF16)
    o_ref[0] = x + jnp.dot(o, wo_ref[...], preferred_element_type=F32)


def _cross_attention(x, mem, g_x, g_mem, wq, wk, wv, wo):
    b, s, d = x.shape
    m = mem.shape[1]
    ts = min(512, s)
    dh = d // CROSS_HEADS
    k, v = _mm([mem.reshape(b * m, d)], jnp.concatenate([wk, wv], axis=1).astype(BF16), gain=g_mem,
               tm=min(512, b * m), splits=(d, d), out_dtype=BF16, name="cross_kv")
    full = lambda a: pl.BlockSpec(a.shape, lambda bi, i: (0,) * a.ndim)
    wq_s = (wq * dh ** -0.5).astype(BF16)
    wo_b = wo.astype(BF16)
    g2 = g_x.reshape(1, d).astype(F32)
    return pl.pallas_call(
        _cross_body,
        grid=(b, s // ts),
        in_specs=[pl.BlockSpec((1, ts, d), lambda bi, i: (bi, i, 0)), full(g2), full(wq_s),
                  pl.BlockSpec((1, m, d), lambda bi, i: (bi, 0, 0)),
                  pl.BlockSpec((1, m, d), lambda bi, i: (bi, 0, 0)), full(wo_b)],
        out_specs=pl.BlockSpec((1, ts, d), lambda bi, i: (bi, i, 0)),
        out_shape=jax.ShapeDtypeStruct((b, s, d), F32),
        compiler_params=_cparams("parallel", "parallel"),
        name="cross_attn",
    )(x, g2, wq_s, k.reshape(b, m, d), v.reshape(b, m, d), wo_b)


def _swiglu_body(x_ref, g_ref, wg_ref, wu_ref, wd_ref, o_ref, h_sc, acc_sc):
    f = pl.program_id(1)

    @pl.when(f == 0)
    def _():
        h_sc[...] = _rms(x_ref[...], g_ref[...]).astype(BF16)
        acc_sc[...] = x_ref[...]

    h = h_sc[...]
    gate = jnp.dot(h, wg_ref[...], preferred_element_type=F32)
    up = jnp.dot(h, wu_ref[...], preferred_element_type=F32)
    act = (gate * jax.nn.sigmoid(gate) * up).astype(BF16)
    acc_sc[...] += jnp.dot(act, wd_ref[...], preferred_element_type=F32)

    @pl.when(f == pl.num_programs(1) - 1)
    def _():
        o_ref[...] = acc_sc[...]


def _swiglu(x, gain, wg, wu, wd, tm=512, tf=1408):
    t, d = x.shape
    ff = wg.shape[1]
    tm, tf = min(tm, t), min(tf, ff)
    assert t % tm == 0 and ff % tf == 0
    return pl.pallas_call(
        _swiglu_body,
        grid=(t // tm, ff // tf),
        in_specs=[pl.BlockSpec((tm, d), lambda i, f: (i, 0)),
                  pl.BlockSpec((1, d), lambda i, f: (0, 0)),
                  pl.BlockSpec((d, tf), lambda i, f: (0, f)),
                  pl.BlockSpec((d, tf), lambda i, f: (0, f)),
                  pl.BlockSpec((tf, d), lambda i, f: (f, 0))],
        out_specs=pl.BlockSpec((tm, d), lambda i, f: (i, 0)),
        out_shape=jax.ShapeDtypeStruct((t, d), F32),
        scratch_shapes=[pltpu.VMEM((tm, d), BF16), pltpu.VMEM((tm, d), F32)],
        compiler_params=_cparams("parallel", "arbitrary"),
        name="swiglu",
    )(x, gain.reshape(1, d).astype(F32), wg.astype(BF16), wu.astype(BF16), wd.astype(BF16))


def _router_body(x_ref, g_ref, r_ref, h_ref, rt_ref):
    h = _rms(x_ref[...], g_ref[...])
    h_ref[...] = h
    h_hi = h.astype(BF16)
    h_lo = (h - h_hi.astype(F32)).astype(BF16)
    r = r_ref[...]
    r_hi = r.astype(BF16)
    r_lo = (r - r_hi.astype(F32)).astype(BF16)
    logits = (jnp.dot(h_hi, r_hi, preferred_element_type=F32) + jnp.dot(h_lo, r_hi, preferred_element_type=F32)
              + jnp.dot(h_hi, r_lo, preferred_element_type=F32))
    lane = lax.broadcasted_iota(jnp.int32, logits.shape, 1)
    lg = jnp.where(lane < N_EXPERTS, logits, -jnp.inf)
    v0 = jnp.max(lg, axis=-1, keepdims=True)
    e0 = jnp.min(jnp.where(lg == v0, lane, LANES), axis=-1, keepdims=True)
    lg1 = jnp.where(lane == e0, -jnp.inf, lg)
    v1 = jnp.max(lg1, axis=-1, keepdims=True)
    e1 = jnp.min(jnp.where(lg1 == v1, lane, LANES), axis=-1, keepdims=True)
    g1 = 1.0 / (1.0 + jnp.exp(v0 - v1))
    g0 = 1.0 - g1
    out = jnp.where(lane == 0, g0, jnp.where(lane == 1, g1, 0.0))
    out = jnp.where(lane == 2, e0.astype(F32), jnp.where(lane == 3, e1.astype(F32), out))
    rt_ref[...] = out


def _router(x, gain, router, tm=512):
    t, d = x.shape
    tm = min(tm, t)
    r_pad = jnp.zeros((d, LANES), F32).at[:, :N_EXPERTS].set(router.astype(F32))
    return pl.pallas_call(
        _router_body,
        grid=(t // tm,),
        in_specs=[pl.BlockSpec((tm, d), lambda i: (i, 0)), pl.BlockSpec((1, d), lambda i: (0, 0)),
                  pl.BlockSpec((d, LANES), lambda i: (0, 0))],
        out_specs=[pl.BlockSpec((tm, d), lambda i: (i, 0)), pl.BlockSpec((tm, LANES), lambda i: (i, 0))],
        out_shape=[jax.ShapeDtypeStruct((t, d), F32), jax.ShapeDtypeStruct((t, LANES), F32)],
        compiler_params=_cparams("parallel"),
        name="moe_router",
    )(x, gain.reshape(1, d).astype(F32), r_pad)


def _row_copy(src_hbm, src_row, dst_ref, dst_row, sem):
    return pltpu.make_async_copy(src_hbm.at[pl.ds(src_row, 1)], dst_ref.at[pl.ds(dst_row, 1)], sem)


def _rank_body(rt_ref, u_ref, ps_ref, dest_ref, carry_sc):
    k, i = pl.program_id(0), pl.program_id(1)
    tm = rt_ref.shape[0]

    @pl.when((k == 0) & (i == 0))
    def _():
        carry_sc[...] = jnp.zeros(carry_sc.shape, F32)

    rt_t = rt_ref[...].T
    e_row = jnp.where(k == 0, rt_t[2:3], rt_t[3:4]).astype(jnp.int32)
    sub = lax.broadcasted_iota(jnp.int32, (N_EXPERTS, tm), 0)
    onehot = sub == e_row
    ahead = jnp.dot(onehot.astype(BF16), u_ref[...], preferred_element_type=F32)
    row = ahead + carry_sc[:, :1] + ps_ref[:, :1]
    dest = jnp.sum(jnp.where(onehot, row, 0.0), axis=0, keepdims=True)
    dest_ref[...] = dest.astype(jnp.int32).reshape(dest_ref.shape)
    carry_sc[...] += jnp.sum(onehot.astype(F32), axis=1, keepdims=True)


def _assignment_rows(routing, pstart, tm=512):
    t = routing.shape[0]
    tm = min(tm, t)
    upper = jnp.asarray(np.triu(np.ones((tm, tm), np.float32), 1), BF16)
    ps = jnp.broadcast_to(pstart.astype(F32)[:, None], (N_EXPERTS, LANES))
    out = pl.pallas_call(
        _rank_body,
        grid=(TOP_K, t // tm),
        in_specs=[pl.BlockSpec((tm, LANES), lambda k, i: (i, 0)),
                  pl.BlockSpec((tm, tm), lambda k, i: (0, 0)),
                  pl.BlockSpec((N_EXPERTS, LANES), lambda k, i: (0, 0))],
        out_specs=pl.BlockSpec((1, 1, 1, tm), lambda k, i: (k, i, 0, 0)),
        out_shape=jax.ShapeDtypeStruct((TOP_K, t // tm, 1, tm), jnp.int32),
        scratch_shapes=[pltpu.VMEM((N_EXPERTS, LANES), F32)],
        compiler_params=_cparams("arbitrary", "arbitrary"),
        name="moe_rank",
    )(routing, upper, ps)
    return out.reshape(TOP_K * t)


def _dispatch_body(dest_ref, h_ref, xs_in, xs_hbm, sem):
    del xs_in
    tm = h_ref.shape[0]
    t = dest_ref.shape[0] // TOP_K
    base = pl.program_id(0) * tm

    def copy(r, k, dst_row):
        return pltpu.make_async_copy(h_ref.at[pl.ds(r, 1)], xs_hbm.at[pl.ds(dst_row, 1)], sem)

    def start(r, c):
        for k in range(TOP_K):
            copy(r, k, dest_ref[k * t + base + r]).start(priority=k % 2)
        return c

    lax.fori_loop(0, tm, start, 0, unroll=8)
    for r in range(tm):
        for k in range(TOP_K):
            copy(r, k, 0).wait()


def _dispatch(h, dest, n_rows, xs_init, tm=256):
    t, d = h.shape
    tm = min(tm, t)
    grid_spec = pltpu.PrefetchScalarGridSpec(
        num_scalar_prefetch=1,
        grid=(t // tm,),
        in_specs=[pl.BlockSpec((tm, d), lambda i, dest: (i, 0)), pl.BlockSpec(memory_space=pl.ANY)],
        out_specs=pl.BlockSpec(memory_space=pl.ANY),
        scratch_shapes=[pltpu.SemaphoreType.DMA(())],
    )
    return pl.pallas_call(
        _dispatch_body,
        grid_spec=grid_spec,
        out_shape=jax.ShapeDtypeStruct((n_rows, d), F32),
        input_output_aliases={2: 0},
        compiler_params=_cparams("arbitrary"),
        name="moe_dispatch",
    )(dest, h, jnp.zeros((n_rows, d), F32) if xs_init is None else xs_init)


def _experts_body(be_ref, nb_ref, x_ref, wg_ref, wu_ref, wd_ref, o_ref, xb_sc, acc_sc):
    i, f = pl.program_id(0), pl.program_id(1)

    @pl.when(i < nb_ref[0])
    def _():
        @pl.when(f == 0)
        def _():
            xb_sc[...] = x_ref[...].astype(BF16)
            acc_sc[...] = jnp.zeros(acc_sc.shape, F32)

        x = xb_sc[...]
        gate = jnp.dot(x, wg_ref[0, 0], preferred_element_type=F32)
        up = jnp.dot(x, wu_ref[0, 0], preferred_element_type=F32)
        act = (gate * jax.nn.sigmoid(gate) * up).astype(BF16)
        acc_sc[...] += jnp.dot(act, wd_ref[0, 0], preferred_element_type=F32)

        @pl.when(f == pl.num_programs(1) - 1)
        def _():
            o_ref[...] = acc_sc[...]

    @pl.when(i >= nb_ref[0])
    def _():
        o_ref[...] = jnp.zeros(o_ref.shape, F32)


def _experts(xs, blk_e, n_used, wg, wu, wd, layer, rows, tf=1792):
    n_rows, d = xs.shape
    ff = wg.shape[3]
    n_blk = n_rows // rows
    last = ff // tf - 1
    grid_spec = pltpu.PrefetchScalarGridSpec(
        num_scalar_prefetch=2,
        grid=(n_blk, ff // tf),
        in_specs=[pl.BlockSpec((rows, d), lambda i, f, be, nb: (jnp.minimum(i, nb[0] - 1), 0)),
                  pl.BlockSpec((1, 1, d, tf), lambda i, f, be, nb: (layer, be[i], 0, jnp.where(i < nb[0], f, last))),
                  pl.BlockSpec((1, 1, d, tf), lambda i, f, be, nb: (layer, be[i], 0, jnp.where(i < nb[0], f, last))),
                  pl.BlockSpec((1, 1, tf, d), lambda i, f, be, nb: (layer, be[i], jnp.where(i < nb[0], f, last), 0))],
        out_specs=pl.BlockSpec((rows, d), lambda i, f, be, nb: (i, 0)),
        scratch_shapes=[pltpu.VMEM((rows, d), BF16), pltpu.VMEM((rows, d), F32)],
    )
    return pl.pallas_call(
        _experts_body,
        grid_spec=grid_spec,
        out_shape=jax.ShapeDtypeStruct((n_rows, d), F32),
        compiler_params=_cparams("arbitrary", "arbitrary"),
        name="moe_experts",
    )(blk_e, n_used, xs, wg, wu, wd)


def _combine_body(dest_ref, x_ref, rt_ref, ys_hbm, o_ref, buf, sem):
    tm = x_ref.shape[0]
    n_tok = dest_ref.shape[0] // TOP_K
    i = pl.program_id(0)
    slot = i & 1

    def start(tile, s):
        def body(r, c):
            for k in range(TOP_K):
                _row_copy(ys_hbm, dest_ref[k * n_tok + tile * tm + r], buf.at[s, k], r,
                          sem.at[s]).start(priority=k % 2)
            return c
        lax.fori_loop(0, tm, body, 0, unroll=8)

    @pl.when(i == 0)
    def _():
        start(0, 0)

    @pl.when(i + 1 < pl.num_programs(0))
    def _():
        start(i + 1, 1 - slot)

    for r in range(tm):
        for k in range(TOP_K):
            _row_copy(ys_hbm, 0, buf.at[slot, k], r, sem.at[slot]).wait()
    rt = rt_ref[...]
    o_ref[...] = x_ref[...] + rt[:, 0:1] * buf[slot, 0] + rt[:, 1:2] * buf[slot, 1]


def _combine(x, routing, ys, dest, tm=256):
    t, d = x.shape
    tm = min(tm, t)
    grid_spec = pltpu.PrefetchScalarGridSpec(
        num_scalar_prefetch=1,
        grid=(t // tm,),
        in_specs=[pl.BlockSpec((tm, d), lambda i, dest: (i, 0)),
                  pl.BlockSpec((tm, LANES), lambda i, dest: (i, 0)),
                  pl.BlockSpec(memory_space=pl.ANY)],
        out_specs=pl.BlockSpec((tm, d), lambda i, dest: (i, 0)),
        scratch_shapes=[pltpu.VMEM((2, TOP_K, tm, d), F32), pltpu.SemaphoreType.DMA((2,))],
    )
    return pl.pallas_call(
        _combine_body,
        grid_spec=grid_spec,
        out_shape=jax.ShapeDtypeStruct((t, d), F32),
        compiler_params=_cparams("arbitrary"),
        name="moe_combine",
    )(dest, x, routing, ys)


MOE_ROWS = 512


def _moe(x, gain, router, w_gate, w_up, w_down, layer, xs_init=None):
    t, d = x.shape
    h, routing = _router(x, gain, router)
    n_assign = t * TOP_K
    experts = routing[:, 2:2 + TOP_K].astype(jnp.int32)
    counts = jnp.sum((experts[:, :, None] == jnp.arange(N_EXPERTS)[None, None, :]).astype(jnp.int32), axis=(0, 1))
    padded = (counts + MOE_ROWS - 1) // MOE_ROWS * MOE_ROWS
    pend = jnp.cumsum(padded)
    pstart = pend - padded
    n_rows = -(-n_assign // MOE_ROWS) * MOE_ROWS + N_EXPERTS * MOE_ROWS
    n_blk = n_rows // MOE_ROWS
    blk_start = jnp.arange(n_blk, dtype=jnp.int32) * MOE_ROWS
    blk_e = jnp.minimum(jnp.sum((pend[None, :] <= blk_start[:, None]).astype(jnp.int32), axis=1), N_EXPERTS - 1)
    n_used = (pend[-1] // MOE_ROWS).astype(jnp.int32).reshape(1)
    dest = _assignment_rows(routing, pstart)
    xs = _dispatch(h, dest, n_rows, xs_init)
    ys = _experts(xs, blk_e.astype(jnp.int32), n_used, w_gate, w_up, w_down, layer, MOE_ROWS)
    return _combine(x, routing, ys, dest), xs


def _final_norm_body(x_ref, g_ref, o_ref):
    o_ref[...] = _rms(x_ref[...], g_ref[...])


def _final_norm(x, gain, tm=1024):
    t, d = x.shape
    tm = min(tm, t)
    return pl.pallas_call(
        _final_norm_body,
        grid=(t // tm,),
        in_specs=[pl.BlockSpec((tm, d), lambda i: (i, 0)), pl.BlockSpec((1, d), lambda i: (0, 0))],
        out_specs=pl.BlockSpec((tm, d), lambda i: (i, 0)),
        out_shape=jax.ShapeDtypeStruct((t, d), F32),
        compiler_params=_cparams("parallel"),
        name="final_norm",
    )(x, gain.reshape(1, d).astype(F32))


def _all_rope_tables(s):
    return {64: _rope_tables(s, 64), 32: _rope_tables(s, 32)}


def kernel(x, mem, norm_mix, w_in, w_out, diff_lambda, diff_subln, nsa_cmp_pos, nsa_cmp_w1, nsa_cmp_w2, mla_q_norm, mla_kv_norm, mla_w_uq, mla_w_ukv, norm_cross, norm_mem, cross_wq, cross_wk, cross_wv, cross_wo, norm_ffn, ffn_w_gate, ffn_w_up, ffn_w_down, moe_router, moe_w_gate, moe_w_up, moe_w_down, final_norm):
    b, s, d = x.shape
    depth = w_in.shape[0]
    tabs = _all_rope_tables(s)
    w_in_p = _permute_w_in(w_in)
    moe_wg, moe_wu, moe_wd = moe_w_gate.astype(BF16), moe_w_up.astype(BF16), moe_w_down.astype(BF16)
    xs_buf = None
    for l in range(depth):
        x = _token_mixers(x, l, tabs, norm_mix[l], w_in_p, w_out[l], diff_lambda[l], diff_subln[l],
                          nsa_cmp_pos[l], nsa_cmp_w1[l], nsa_cmp_w2[l], mla_q_norm[l], mla_kv_norm[l],
                          mla_w_uq[l], mla_w_ukv[l])
        x = _cross_attention(x, mem, norm_cross[l], norm_mem[l], cross_wq[l], cross_wk[l], cross_wv[l],
                             cross_wo[l])
        xt = x.reshape(b * s, d)
        i = l // 2
        if l % 2 == 0:
            xt = _swiglu(xt, norm_ffn[l], ffn_w_gate[i], ffn_w_up[i], ffn_w_down[i])
        else:
            xt, xs_buf = _moe(xt, norm_ffn[l], moe_router[i], moe_wg, moe_wu, moe_wd, i, xs_buf)
        x = xt.reshape(b, s, d)
    return _final_norm(x.reshape(b * s, d), final_norm).reshape(b, s, d)
```
